```python
import math
import jax, jax.numpy as jnp
from jax import lax
import numpy as np

D_MODEL = 2048
BATCH = 8
SEQ = 8192
DEPTH = 4

CHUNK = 64
GDN_HEAD_DIM = 128
GDN_WIDTH = D_MODEL // 2
GDN_HEADS = GDN_WIDTH // GDN_HEAD_DIM
SC_WIDTH = D_MODEL - GDN_WIDTH
GDN_CONV = 4
SC_CONV = 3
FFN_CONV = 3
N_MEM = 256
XATTN_HEADS = 4
XATTN_HEAD_DIM = D_MODEL // XATTN_HEADS
D_FF = ((8 * D_MODEL // 3 + 255) // 256) * 256
N_MIX_IN = 4 * GDN_WIDTH + 2 * GDN_HEADS + 3 * SC_WIDTH
EPS = 1e-6

kernel_name = 'hybrid_gdn_shortconv_memxattn_convffn'


def rmsnorm(x, w):
    xf = x.astype(jnp.float32)
    y = xf * lax.rsqrt(jnp.mean(xf * xf, axis=-1, keepdims=True) + EPS)
    return (y * w.astype(jnp.float32)).astype(x.dtype)


def l2norm(x):
    return x * lax.rsqrt(jnp.sum(x * x, axis=-1, keepdims=True) + EPS)


def causal_dwconv(x, w):
    K = w.shape[0]
    S = x.shape[1]
    w = w.astype(x.dtype)
    xp = jnp.pad(x, ((0, 0), (K - 1, 0), (0, 0)))
    y = xp[:, 0:S] * w[0]
    for j in range(1, K):
        y = y + xp[:, j:j + S] * w[j]
    return y


def gated_delta_rule(q, k, v, g, beta):
    Bsz, S, H, DK = q.shape
    DV = v.shape[-1]
    N = S // CHUNK

    def to_chunks(t):
        t = t.reshape((Bsz, N, CHUNK, H) + t.shape[3:])
        return jnp.moveaxis(t, 3, 1)

    q, k, v, g, beta = (to_chunks(t) for t in (q, k, v, g, beta))
    q = q * (DK ** -0.5)
    g = jnp.cumsum(g, axis=-1)
    causal = jnp.tril(jnp.ones((CHUNK, CHUNK), dtype=bool))
    strict = jnp.tril(jnp.ones((CHUNK, CHUNK), dtype=bool), k=-1)
    decay = jnp.exp(jnp.where(causal, g[..., :, None] - g[..., None, :], -jnp.inf))
    k_beta = k * beta[..., None]
    a_strict = jnp.where(strict, jnp.einsum('bhncd,bhnmd->bhncm', k_beta, k) * decay, 0.0)
    eye = jnp.eye(CHUNK, dtype=jnp.float32)
    rhs = jnp.concatenate([v * beta[..., None], k_beta * jnp.exp(g)[..., None]], axis=-1)
    sol = lax.linalg.triangular_solve(eye + a_strict, rhs, left_side=True, lower=True,
                                      unit_diagonal=True)
    u, w = sol[..., :DV], sol[..., DV:]
    attn = jnp.einsum('bhncd,bhnmd->bhncm', q, k) * decay
    q_dec = q * jnp.exp(g)[..., None]
    g_last = g[..., -1]
    k_dec = k * jnp.exp(g_last[..., None] - g)[..., None]

    def step(state, xs):
        q_i, k_i, u_i, w_i, attn_i, gl_i = xs
        v_new = u_i - jnp.einsum('bhcd,bhde->bhce', w_i, state)
        o_i = (jnp.einsum('bhcd,bhde->bhce', q_i, state)
               + jnp.einsum('bhcm,bhme->bhce', attn_i, v_new))
        state = (state * jnp.exp(gl_i)[..., None, None]
                 + jnp.einsum('bhcd,bhce->bhde', k_i, v_new))
        return state, o_i

    xs = tuple(jnp.moveaxis(t, 2, 0) for t in (q_dec, k_dec, u, w, attn, g_last))
    state0 = jnp.zeros((Bsz, H, DK, DV), jnp.float32)
    _, o = lax.scan(step, state0, xs)
    return jnp.transpose(o, (1, 0, 3, 2, 4)).reshape(Bsz, S, H, DV)


def gdn_group(proj, conv_w, a_log, dt_bias, out_gain):
    Bsz, S, _ = proj.shape
    W, H, Dh = GDN_WIDTH, GDN_HEADS, GDN_HEAD_DIM
    qkv = jax.nn.silu(causal_dwconv(proj[..., :3 * W], conv_w)).astype(jnp.float32)
    q = l2norm(qkv[..., :W].reshape(Bsz, S, H, Dh))
    k = l2norm(qkv[..., W:2 * W].reshape(Bsz, S, H, Dh))
    v = qkv[..., 2 * W:].reshape(Bsz, S, H, Dh)
    z = proj[..., 3 * W:4 * W].reshape(Bsz, S, H, Dh)
    b_raw = proj[..., 4 * W:4 * W + H].astype(jnp.float32)
    a_raw = proj[..., 4 * W + H:4 * W + 2 * H].astype(jnp.float32)
    beta = jax.nn.sigmoid(b_raw)
    g = -jnp.exp(a_log.astype(jnp.float32)) * jax.nn.softplus(a_raw + dt_bias.astype(jnp.float32))
    o = gated_delta_rule(q, k, v, g, beta)
    o = rmsnorm(o, out_gain).astype(proj.dtype) * jax.nn.silu(z)
    return o.reshape(Bsz, S, W)


def shortconv_group(proj, conv_w):
    off = 4 * GDN_WIDTH + 2 * GDN_HEADS
    b_gate = proj[..., off:off + SC_WIDTH]
    c_gate = proj[..., off + SC_WIDTH:off + 2 * SC_WIDTH]
    h = proj[..., off + 2 * SC_WIDTH:off + 3 * SC_WIDTH]
    return b_gate * causal_dwconv(c_gate * h, conv_w)


def memory_xattn(h, mem_n, w_q, w_k, w_v, w_o):
    Bsz, S, _ = h.shape
    q = (h @ w_q).reshape(Bsz, S, XATTN_HEADS, XATTN_HEAD_DIM)
    k = (mem_n @ w_k).reshape(Bsz, N_MEM, XATTN_HEADS, XATTN_HEAD_DIM)
    v = (mem_n @ w_v).reshape(Bsz, N_MEM, XATTN_HEADS, XATTN_HEAD_DIM)
    s = jnp.einsum('bshd,bmhd->bhsm', q, k).astype(jnp.float32) * (XATTN_HEAD_DIM ** -0.5)
    p = jax.nn.softmax(s, axis=-1).astype(v.dtype)
    o = jnp.einsum('bhsm,bmhd->bshd', p, v).reshape(Bsz, S, D_MODEL)
    return o @ w_o


def conv_ffn(h, w_up, conv_w, w_down):
    u = causal_dwconv(h @ w_up, conv_w)
    gate, up = u[..., :D_FF], u[..., D_FF:]
    return (jax.nn.silu(gate) * up) @ w_down


def _fwd_setup_inputs(seed: int = 0) -> dict:
    key = jax.random.key(seed)
    ks = jax.random.split(key, 24)
    L, D = DEPTH, D_MODEL
    out_scale = (3 * DEPTH) ** -0.5

    def normal(k, shape, std):
        return jax.random.normal(k, shape, jnp.float32) * std

    def gain(k, shape):
        return 1.0 + normal(k, shape, 0.02)

    dt = jnp.exp(jax.random.uniform(ks[6], (L, GDN_HEADS), jnp.float32,
                                    math.log(1e-3), math.log(1e-1)))
    return {
        'x': normal(ks[0], (BATCH, SEQ, D), 1.0),
        'mem': normal(ks[1], (BATCH, N_MEM, D), 1.0),
        'mix_norm': gain(ks[2], (L, D)),
        'w_mix_in': normal(ks[3], (L, D, N_MIX_IN), D ** -0.5),
        'gdn_conv': normal(ks[4], (L, GDN_CONV, 3 * GDN_WIDTH), GDN_CONV ** -0.5),
        'gdn_a_log': jnp.log(jax.random.uniform(ks[5], (L, GDN_HEADS), jnp.float32, 1.0, 16.0)),
        'gdn_dt_bias': dt + jnp.log(-jnp.expm1(-dt)),
        'gdn_out_norm': gain(ks[7], (L, GDN_HEAD_DIM)),
        'sc_conv': normal(ks[8], (L, SC_CONV, SC_WIDTH), SC_CONV ** -0.5),
        'w_mix_out': normal(ks[9], (L, D, D), D ** -0.5 * out_scale),
        'xattn_norm': gain(ks[10], (L, D)),
        'mem_norm': gain(ks[11], (L, D)),
        'w_xq': normal(ks[12], (L, D, D), D ** -0.5),
        'w_xk': normal(ks[13], (L, D, D), D ** -0.5),
        'w_xv': normal(ks[14], (L, D, D), D ** -0.5),
        'w_xo': normal(ks[15], (L, D, D), D ** -0.5 * out_scale),
        'ffn_norm': gain(ks[16], (L, D)),
        'w_ffn_up': normal(ks[17], (L, D, 2 * D_FF), D ** -0.5),
        'ffn_conv': normal(ks[18], (L, FFN_CONV, 2 * D_FF), FFN_CONV ** -0.5),
        'w_ffn_down': normal(ks[19], (L, D_FF, D), D_FF ** -0.5 * out_scale),
        'final_norm': gain(ks[20], (D,)),
    }


def _fwd_reference(x, mem, mix_norm, w_mix_in, gdn_conv, gdn_a_log, gdn_dt_bias, gdn_out_norm,
              sc_conv, w_mix_out, xattn_norm, mem_norm, w_xq, w_xk, w_xv, w_xo,
              ffn_norm, w_ffn_up, ffn_conv, w_ffn_down, final_norm):
    for l in range(DEPTH):
        h = rmsnorm(x, mix_norm[l])
        proj = h @ w_mix_in[l]
        y_gdn = gdn_group(proj, gdn_conv[l], gdn_a_log[l], gdn_dt_bias[l], gdn_out_norm[l])
        y_sc = shortconv_group(proj, sc_conv[l])
        x = x + jnp.concatenate([y_gdn, y_sc], axis=-1) @ w_mix_out[l]
        h = rmsnorm(x, xattn_norm[l])
        mem_n = rmsnorm(mem, mem_norm[l])
        x = x + memory_xattn(h, mem_n, w_xq[l], w_xk[l], w_xv[l], w_xo[l])
        h = rmsnorm(x, ffn_norm[l])
        x = x + conv_ffn(h, w_ffn_up[l], ffn_conv[l], w_ffn_down[l])
    return rmsnorm(x, final_norm)


import jax as _jax
import jax.numpy as _jnp

TWIN_FORMAT = 'train_step'
FWD_PARAMS = ['x', 'mem', 'mix_norm', 'w_mix_in', 'gdn_conv', 'gdn_a_log', 'gdn_dt_bias', 'gdn_out_norm', 'sc_conv', 'w_mix_out', 'xattn_norm', 'mem_norm', 'w_xq', 'w_xk', 'w_xv', 'w_xo', 'ffn_norm', 'w_ffn_up', 'ffn_conv', 'w_ffn_down', 'final_norm']
TWIN_WEIGHTS = ['mix_norm', 'w_mix_in', 'gdn_conv', 'gdn_a_log', 'gdn_dt_bias', 'gdn_out_norm', 'sc_conv', 'w_mix_out', 'xattn_norm', 'mem_norm', 'w_xq', 'w_xk', 'w_xv', 'w_xo', 'ffn_norm', 'w_ffn_up', 'ffn_conv', 'w_ffn_down', 'final_norm']
TWIN_DIFF_INPUT = 'x'
TWIN_INPUTS = ['x', 'mem', 'mix_norm', 'w_mix_in', 'gdn_conv', 'gdn_a_log', 'gdn_dt_bias', 'gdn_out_norm', 'sc_conv', 'w_mix_out', 'xattn_norm', 'mem_norm', 'w_xq', 'w_xk', 'w_xv', 'w_xo', 'ffn_norm', 'w_ffn_up', 'ffn_conv', 'w_ffn_down', 'final_norm', 'loss_target', 'm_mix_norm', 'm_w_mix_in', 'm_gdn_conv', 'm_gdn_a_log', 'm_gdn_dt_bias', 'm_gdn_out_norm', 'm_sc_conv', 'm_w_mix_out', 'm_xattn_norm', 'm_mem_norm', 'm_w_xq', 'm_w_xk', 'm_w_xv', 'm_w_xo', 'm_ffn_norm', 'm_w_ffn_up', 'm_ffn_conv', 'm_w_ffn_down', 'm_final_norm', 'v_mix_norm', 'v_w_mix_in', 'v_gdn_conv', 'v_gdn_a_log', 'v_gdn_dt_bias', 'v_gdn_out_norm', 'v_sc_conv', 'v_w_mix_out', 'v_xattn_norm', 'v_mem_norm', 'v_w_xq', 'v_w_xk', 'v_w_xv', 'v_w_xo', 'v_ffn_norm', 'v_w_ffn_up', 'v_ffn_conv', 'v_w_ffn_down', 'v_final_norm']
TWIN_OUTPUTS = ['loss', 'grad_x', 'grad_mix_norm', 'grad_w_mix_in', 'grad_gdn_conv', 'grad_gdn_a_log', 'grad_gdn_dt_bias', 'grad_gdn_out_norm', 'grad_sc_conv', 'grad_w_mix_out', 'grad_xattn_norm', 'grad_mem_norm', 'grad_w_xq', 'grad_w_xk', 'grad_w_xv', 'grad_w_xo', 'grad_ffn_norm', 'grad_w_ffn_up', 'grad_ffn_conv', 'grad_w_ffn_down', 'grad_final_norm', 'delta_mix_norm', 'delta_w_mix_in', 'delta_gdn_conv', 'delta_gdn_a_log', 'delta_gdn_dt_bias', 'delta_gdn_out_norm', 'delta_sc_conv', 'delta_w_mix_out', 'delta_xattn_norm', 'delta_mem_norm', 'delta_w_xq', 'delta_w_xk', 'delta_w_xv', 'delta_w_xo', 'delta_ffn_norm', 'delta_w_ffn_up', 'delta_ffn_conv', 'delta_w_ffn_down', 'delta_final_norm', 'new_m_mix_norm', 'new_m_w_mix_in', 'new_m_gdn_conv', 'new_m_gdn_a_log', 'new_m_gdn_dt_bias', 'new_m_gdn_out_norm', 'new_m_sc_conv', 'new_m_w_mix_out', 'new_m_xattn_norm', 'new_m_mem_norm', 'new_m_w_xq', 'new_m_w_xk', 'new_m_w_xv', 'new_m_w_xo', 'new_m_ffn_norm', 'new_m_w_ffn_up', 'new_m_ffn_conv', 'new_m_w_ffn_down', 'new_m_final_norm', 'new_v_mix_norm', 'new_v_w_mix_in', 'new_v_gdn_conv', 'new_v_gdn_a_log', 'new_v_gdn_dt_bias', 'new_v_gdn_out_norm', 'new_v_sc_conv', 'new_v_w_mix_out', 'new_v_xattn_norm', 'new_v_mem_norm', 'new_v_w_xq', 'new_v_w_xk', 'new_v_w_xv', 'new_v_w_xo', 'new_v_ffn_norm', 'new_v_w_ffn_up', 'new_v_ffn_conv', 'new_v_w_ffn_down', 'new_v_final_norm']
TWIN_LEAF_KINDS = {'loss': 'loss', 'grad_x': 'grad_x', 'grad_mix_norm': 'grad_w', 'grad_w_mix_in': 'grad_w', 'grad_gdn_conv': 'grad_w', 'grad_gdn_a_log': 'grad_w', 'grad_gdn_dt_bias': 'grad_w', 'grad_gdn_out_norm': 'grad_w', 'grad_sc_conv': 'grad_w', 'grad_w_mix_out': 'grad_w', 'grad_xattn_norm': 'grad_w', 'grad_mem_norm': 'grad_w', 'grad_w_xq': 'grad_w', 'grad_w_xk': 'grad_w', 'grad_w_xv': 'grad_w', 'grad_w_xo': 'grad_w', 'grad_ffn_norm': 'grad_w', 'grad_w_ffn_up': 'grad_w', 'grad_ffn_conv': 'grad_w', 'grad_w_ffn_down': 'grad_w', 'grad_final_norm': 'grad_w', 'delta_mix_norm': 'delta_w', 'delta_w_mix_in': 'delta_w', 'delta_gdn_conv': 'delta_w', 'delta_gdn_a_log': 'delta_w', 'delta_gdn_dt_bias': 'delta_w', 'delta_gdn_out_norm': 'delta_w', 'delta_sc_conv': 'delta_w', 'delta_w_mix_out': 'delta_w', 'delta_xattn_norm': 'delta_w', 'delta_mem_norm': 'delta_w', 'delta_w_xq': 'delta_w', 'delta_w_xk': 'delta_w', 'delta_w_xv': 'delta_w', 'delta_w_xo': 'delta_w', 'delta_ffn_norm': 'delta_w', 'delta_w_ffn_up': 'delta_w', 'delta_ffn_conv': 'delta_w', 'delta_w_ffn_down': 'delta_w', 'delta_final_norm': 'delta_w', 'new_m_mix_norm': 'new_m', 'new_m_w_mix_in': 'new_m', 'new_m_gdn_conv': 'new_m', 'new_m_gdn_a_log': 'new_m', 'new_m_gdn_dt_bias': 'new_m', 'new_m_gdn_out_norm': 'new_m', 'new_m_sc_conv': 'new_m', 'new_m_w_mix_out': 'new_m', 'new_m_xattn_norm': 'new_m', 'new_m_mem_norm': 'new_m', 'new_m_w_xq': 'new_m', 'new_m_w_xk': 'new_m', 'new_m_w_xv': 'new_m', 'new_m_w_xo': 'new_m', 'new_m_ffn_norm': 'new_m', 'new_m_w_ffn_up': 'new_m', 'new_m_ffn_conv': 'new_m', 'new_m_w_ffn_down': 'new_m', 'new_m_final_norm': 'new_m', 'new_v_mix_norm': 'new_v', 'new_v_w_mix_in': 'new_v', 'new_v_gdn_conv': 'new_v', 'new_v_gdn_a_log': 'new_v', 'new_v_gdn_dt_bias': 'new_v', 'new_v_gdn_out_norm': 'new_v', 'new_v_sc_conv': 'new_v', 'new_v_w_mix_out': 'new_v', 'new_v_xattn_norm': 'new_v', 'new_v_mem_norm': 'new_v', 'new_v_w_xq': 'new_v', 'new_v_w_xk': 'new_v', 'new_v_w_xv': 'new_v', 'new_v_w_xo': 'new_v', 'new_v_ffn_norm': 'new_v', 'new_v_w_ffn_up': 'new_v', 'new_v_ffn_conv': 'new_v', 'new_v_w_ffn_down': 'new_v', 'new_v_final_norm': 'new_v'}


def _forward(args):
    return _fwd_reference(*[args[k] for k in FWD_PARAMS])


def _output_shape():
    def fwd():
        inp = _fwd_setup_inputs(0)
        return _fwd_reference(*[inp[k] for k in FWD_PARAMS])
    out = _jax.eval_shape(fwd)
    return out.shape, out.dtype

N_MICROBATCH = 1
ADAM_LR = 0.001
ADAM_B1 = 0.9
ADAM_B2 = 0.999
ADAM_EPS = 1e-08
ADAM_WD = 0.01
ADAM_STEP = 10
PER_EXAMPLE_BATCH_AXIS = {'x': 0, 'mem': 0, 'loss_target': 0}
SHARED_INPUTS = []
_WEIGHT_DTYPES = {'mix_norm': _jnp.float32, 'w_mix_in': _jnp.float32, 'gdn_conv': _jnp.float32, 'gdn_a_log': _jnp.float32, 'gdn_dt_bias': _jnp.float32, 'gdn_out_norm': _jnp.float32, 'sc_conv': _jnp.float32, 'w_mix_out': _jnp.float32, 'xattn_norm': _jnp.float32, 'mem_norm': _jnp.float32, 'w_xq': _jnp.float32, 'w_xk': _jnp.float32, 'w_xv': _jnp.float32, 'w_xo': _jnp.float32, 'ffn_norm': _jnp.float32, 'w_ffn_up': _jnp.float32, 'ffn_conv': _jnp.float32, 'w_ffn_down': _jnp.float32, 'final_norm': _jnp.float32}
MOMENT_SCALE = {'mix_norm': 5.214797e-02, 'w_mix_in': 2.747831e-02, 'gdn_conv': 1.673802e-02, 'gdn_a_log': 9.260993e-02, 'gdn_dt_bias': 8.871433e-02, 'gdn_out_norm': 5.934462e-02, 'sc_conv': 3.769390e-02, 'w_mix_out': 1.035817e-01, 'xattn_norm': 4.018066e-03, 'mem_norm': 5.506168e-03, 'w_xq': 3.781861e-03, 'w_xk': 3.786156e-03, 'w_xv': 3.869744e-03, 'w_xo': 1.337100e-02, 'ffn_norm': 3.162926e-02, 'w_ffn_up': 1.340712e-02, 'ffn_conv': 1.360465e-02, 'w_ffn_down': 7.576547e-02, 'final_norm': 3.198567e+01}


def _to_microbatches(a, axis):
    t = _jnp.moveaxis(a, axis, 0)
    t = t.reshape((N_MICROBATCH, t.shape[0] // N_MICROBATCH) + t.shape[1:])
    return _jnp.moveaxis(t, 1, axis + 1)


def setup_inputs(seed: int = 0) -> dict:
    inp = _fwd_setup_inputs(seed)
    key = _jax.random.fold_in(_jax.random.key(seed), 7919)
    shape, _ = _output_shape()
    out = dict(inp)
    out["loss_target"] = _jax.random.normal(_jax.random.fold_in(key, 0), shape, _jnp.float32)
    for i, name in enumerate(TWIN_WEIGHTS):
        w = inp[name].astype(_jnp.float32)
        if MOMENT_SCALE is None:
            s = _jnp.sqrt(_jnp.mean(_jnp.square(w)) + 1e-30)
        else:
            s = MOMENT_SCALE[name]
        km, kv = _jax.random.split(_jax.random.fold_in(key, i + 1))
        out[name] = w
        out["m_" + name] = s * _jax.random.normal(km, w.shape, _jnp.float32)
        out["v_" + name] = (s * s) * _jax.random.uniform(kv, w.shape, _jnp.float32, 0.5, 1.5)
    if N_MICROBATCH > 1:
        for name, axis in PER_EXAMPLE_BATCH_AXIS.items():
            out[name] = _to_microbatches(out[name], axis)
    return {'x': out['x'], 'mem': out['mem'], 'mix_norm': out['mix_norm'], 'w_mix_in': out['w_mix_in'], 'gdn_conv': out['gdn_conv'], 'gdn_a_log': out['gdn_a_log'], 'gdn_dt_bias': out['gdn_dt_bias'], 'gdn_out_norm': out['gdn_out_norm'], 'sc_conv': out['sc_conv'], 'w_mix_out': out['w_mix_out'], 'xattn_norm': out['xattn_norm'], 'mem_norm': out['mem_norm'], 'w_xq': out['w_xq'], 'w_xk': out['w_xk'], 'w_xv': out['w_xv'], 'w_xo': out['w_xo'], 'ffn_norm': out['ffn_norm'], 'w_ffn_up': out['w_ffn_up'], 'ffn_conv': out['ffn_conv'], 'w_ffn_down': out['w_ffn_down'], 'final_norm': out['final_norm'], 'loss_target': out['loss_target'], 'm_mix_norm': out['m_mix_norm'], 'm_w_mix_in': out['m_w_mix_in'], 'm_gdn_conv': out['m_gdn_conv'], 'm_gdn_a_log': out['m_gdn_a_log'], 'm_gdn_dt_bias': out['m_gdn_dt_bias'], 'm_gdn_out_norm': out['m_gdn_out_norm'], 'm_sc_conv': out['m_sc_conv'], 'm_w_mix_out': out['m_w_mix_out'], 'm_xattn_norm': out['m_xattn_norm'], 'm_mem_norm': out['m_mem_norm'], 'm_w_xq': out['m_w_xq'], 'm_w_xk': out['m_w_xk'], 'm_w_xv': out['m_w_xv'], 'm_w_xo': out['m_w_xo'], 'm_ffn_norm': out['m_ffn_norm'], 'm_w_ffn_up': out['m_w_ffn_up'], 'm_ffn_conv': out['m_ffn_conv'], 'm_w_ffn_down': out['m_w_ffn_down'], 'm_final_norm': out['m_final_norm'], 'v_mix_norm': out['v_mix_norm'], 'v_w_mix_in': out['v_w_mix_in'], 'v_gdn_conv': out['v_gdn_conv'], 'v_gdn_a_log': out['v_gdn_a_log'], 'v_gdn_dt_bias': out['v_gdn_dt_bias'], 'v_gdn_out_norm': out['v_gdn_out_norm'], 'v_sc_conv': out['v_sc_conv'], 'v_w_mix_out': out['v_w_mix_out'], 'v_xattn_norm': out['v_xattn_norm'], 'v_mem_norm': out['v_mem_norm'], 'v_w_xq': out['v_w_xq'], 'v_w_xk': out['v_w_xk'], 'v_w_xv': out['v_w_xv'], 'v_w_xo': out['v_w_xo'], 'v_ffn_norm': out['v_ffn_norm'], 'v_w_ffn_up': out['v_w_ffn_up'], 'v_ffn_conv': out['v_ffn_conv'], 'v_w_ffn_down': out['v_w_ffn_down'], 'v_final_norm': out['v_final_norm']}


def _loss(weights, diff, rest, loss_target):
    with _jax.named_scope("forward"):
        args = {**rest, TWIN_DIFF_INPUT: diff, **{k: w.astype(_WEIGHT_DTYPES[k]) for k, w in weights.items()}}
        y = _forward(args)
    with _jax.named_scope("loss_head"):
        err = _jnp.square(y.astype(_jnp.float32) - loss_target)
        return 0.5 * _jnp.sum(_jnp.mean(err, axis=-1)) if err.ndim else 0.5 * err


def _adamw(w, g, m, v):
    m = ADAM_B1 * m + (1.0 - ADAM_B1) * g
    v = ADAM_B2 * v + (1.0 - ADAM_B2) * _jnp.square(g)
    m_hat = m / (1.0 - ADAM_B1 ** ADAM_STEP)
    v_hat = v / (1.0 - ADAM_B2 ** ADAM_STEP)
    delta = -ADAM_LR * (m_hat / (_jnp.sqrt(v_hat) + ADAM_EPS) + ADAM_WD * w)
    return delta, m, v


def reference(x, mem, mix_norm, w_mix_in, gdn_conv, gdn_a_log, gdn_dt_bias, gdn_out_norm, sc_conv, w_mix_out, xattn_norm, mem_norm, w_xq, w_xk, w_xv, w_xo, ffn_norm, w_ffn_up, ffn_conv, w_ffn_down, final_norm, loss_target, m_mix_norm, m_w_mix_in, m_gdn_conv, m_gdn_a_log, m_gdn_dt_bias, m_gdn_out_norm, m_sc_conv, m_w_mix_out, m_xattn_norm, m_mem_norm, m_w_xq, m_w_xk, m_w_xv, m_w_xo, m_ffn_norm, m_w_ffn_up, m_ffn_conv, m_w_ffn_down, m_final_norm, v_mix_norm, v_w_mix_in, v_gdn_conv, v_gdn_a_log, v_gdn_dt_bias, v_gdn_out_norm, v_sc_conv, v_w_mix_out, v_xattn_norm, v_mem_norm, v_w_xq, v_w_xk, v_w_xv, v_w_xo, v_ffn_norm, v_w_ffn_up, v_ffn_conv, v_w_ffn_down, v_final_norm):
    given = dict(x=x, mem=mem, mix_norm=mix_norm, w_mix_in=w_mix_in, gdn_conv=gdn_conv, gdn_a_log=gdn_a_log, gdn_dt_bias=gdn_dt_bias, gdn_out_norm=gdn_out_norm, sc_conv=sc_conv, w_mix_out=w_mix_out, xattn_norm=xattn_norm, mem_norm=mem_norm, w_xq=w_xq, w_xk=w_xk, w_xv=w_xv, w_xo=w_xo, ffn_norm=ffn_norm, w_ffn_up=w_ffn_up, ffn_conv=ffn_conv, w_ffn_down=w_ffn_down, final_norm=final_norm, loss_target=loss_target, m_mix_norm=m_mix_norm, m_w_mix_in=m_w_mix_in, m_gdn_conv=m_gdn_conv, m_gdn_a_log=m_gdn_a_log, m_gdn_dt_bias=m_gdn_dt_bias, m_gdn_out_norm=m_gdn_out_norm, m_sc_conv=m_sc_conv, m_w_mix_out=m_w_mix_out, m_xattn_norm=m_xattn_norm, m_mem_norm=m_mem_norm, m_w_xq=m_w_xq, m_w_xk=m_w_xk, m_w_xv=m_w_xv, m_w_xo=m_w_xo, m_ffn_norm=m_ffn_norm, m_w_ffn_up=m_w_ffn_up, m_ffn_conv=m_ffn_conv, m_w_ffn_down=m_w_ffn_down, m_final_norm=m_final_norm, v_mix_norm=v_mix_norm, v_w_mix_in=v_w_mix_in, v_gdn_conv=v_gdn_conv, v_gdn_a_log=v_gdn_a_log, v_gdn_dt_bias=v_gdn_dt_bias, v_gdn_out_norm=v_gdn_out_norm, v_sc_conv=v_sc_conv, v_w_mix_out=v_w_mix_out, v_xattn_norm=v_xattn_norm, v_mem_norm=v_mem_norm, v_w_xq=v_w_xq, v_w_xk=v_w_xk, v_w_xv=v_w_xv, v_w_xo=v_w_xo, v_ffn_norm=v_ffn_norm, v_w_ffn_up=v_w_ffn_up, v_ffn_conv=v_ffn_conv, v_w_ffn_down=v_w_ffn_down, v_final_norm=v_final_norm)
    weights = {n: given[n] for n in TWIN_WEIGHTS}
    shared = {n: given[n] for n in SHARED_INPUTS}
    per_example = {n: given[n] for n in ['x', 'mem']}
    grad_fn = _jax.value_and_grad(_loss, argnums=(0, 1))

    def one_microbatch(ex, loss_target):
        ex = dict(ex)
        diff = ex.pop(TWIN_DIFF_INPUT)
        return grad_fn(weights, diff, {**shared, **ex}, loss_target)

    if N_MICROBATCH == 1:
        loss, (grad_w, grad_x) = one_microbatch(per_example, given["loss_target"])
    else:
        def body(carry, xs):
            loss_sum, grad_sum = carry
            l_k, (gw_k, gx_k) = one_microbatch(xs[0], xs[1])
            with _jax.named_scope("update"):
                return (loss_sum + l_k, _jax.tree.map(_jnp.add, grad_sum, gw_k)), gx_k

        init = (_jnp.zeros((), _jnp.float32), _jax.tree.map(_jnp.zeros_like, weights))
        (loss, grad_w), grad_x = _jax.lax.scan(body, init, (per_example, given["loss_target"]))
    with _jax.named_scope("update"):
        delta_w, new_m, new_v = {}, {}, {}
        for n in TWIN_WEIGHTS:
            delta_w[n], new_m[n], new_v[n] = _adamw(weights[n], grad_w[n], given["m_" + n], given["v_" + n])
    return (loss, grad_x, *[grad_w[n] for n in TWIN_WEIGHTS], *[delta_w[n] for n in TWIN_WEIGHTS],
            *[new_m[n] for n in TWIN_WEIGHTS], *[new_v[n] for n in TWIN_WEIGHTS])
```

```python
import functools

import jax
import jax.numpy as jnp
from jax import lax
from jax.experimental import pallas as pl
from jax.experimental.pallas import tpu as pltpu

F32 = jnp.float32
MXU_DTYPE = jnp.bfloat16
EPS = 1e-6
CHUNK = 64
XATTN_HEADS = 4
LANES = 128
HALO = 16
VMEM_LIMIT = 48 * 1024 * 1024
ADAM_LR, ADAM_B1, ADAM_B2, ADAM_EPS, ADAM_WD, ADAM_STEP = 0.001, 0.9, 0.999, 1e-08, 0.01, 10
MESH = pl.DeviceIdType.MESH
N_CHIPS = 4
N_DEV = 8

_DIMS = {
    "nn": (((1,), (0,)), ((), ())),
    "nt": (((1,), (1,)), ((), ())),
    "tn": (((0,), (0,)), ((), ())),
}


def _tile(n, pref, align=LANES):
    if n <= pref:
        return n
    t = (pref // align) * align
    while t >= align:
        if n % t == 0:
            return t
        t -= align
    return n


def _params(*sem):
    return pltpu.CompilerParams(dimension_semantics=sem, vmem_limit_bytes=VMEM_LIMIT)


def _dot(a, b, form, hi=False):
    if hi:
        return lax.dot_general(a.astype(F32), b.astype(F32), _DIMS[form], precision=lax.Precision.HIGHEST,
                               preferred_element_type=F32)
    return lax.dot_general(a.astype(MXU_DTYPE), b.astype(MXU_DTYPE), _DIMS[form], preferred_element_type=F32)


@functools.partial(jax.custom_vjp, nondiff_argnums=(2, 3))
def _dot_d(a, b, form, hi):
    return _dot(a, b, form, hi)


def _dot_d_fwd(a, b, form, hi):
    return _dot(a, b, form, hi), (a, b)


def _dot_d_bwd(form, hi, res, g):
    a, b = res
    if form == "nn":
        da, db = _dot_d(g, b, "nt", hi), _dot_d(a, g, "tn", hi)
    elif form == "nt":
        da, db = _dot_d(g, b, "nn", hi), _dot_d(g, a, "tn", hi)
    else:
        da, db = _dot_d(b, g, "nt", hi), _dot_d(a, g, "nn", hi)
    return da.astype(a.dtype), db.astype(b.dtype)


_dot_d.defvjp(_dot_d_fwd, _dot_d_bwd)


def _tri_inv_impl(a, mmh):
    c = a.shape[0]
    r = lax.broadcasted_iota(jnp.int32, (c, c), 0)
    s = lax.broadcasted_iota(jnp.int32, (c, c), 1)
    eye = (r == s).astype(F32)
    diag_blk = (r // 16) == (s // 16)
    d = jnp.where(diag_blk, a, 0.0)
    low = a - d
    d2 = mmh(d, d)
    d4 = mmh(d2, d2)
    d8 = mmh(d4, d4)
    td = mmh(mmh(mmh(eye - d, eye + d2), eye + d4), eye + d8)
    n = mmh(td, low)
    acc = eye - n
    p = n
    pw = 1
    while 2 * pw < c // 16:
        p = mmh(p, p)
        pw *= 2
        acc = mmh(acc, eye + p)
    return mmh(acc, td)


def _mmh_plain(a, b):
    return _dot(a, b, "nn", True)


@jax.custom_vjp
def _tri_inv_d(a):
    return _tri_inv_impl(a, _mmh_plain)


def _tri_inv_d_fwd(a):
    t = _tri_inv_impl(a, _mmh_plain)
    return t, t


def _tri_inv_d_bwd(t, g):
    return (-_dot(_dot(t, g, "tn", True), t, "nt", True),)


_tri_inv_d.defvjp(_tri_inv_d_fwd, _tri_inv_d_bwd)


class _Ops:
    def __init__(self, diff):
        self.diff = diff

    def mm(self, a, b, form="nn"):
        return _dot_d(a, b, form, False) if self.diff else _dot(a, b, form, False)

    def mmh(self, a, b, form="nn"):
        return _dot_d(a, b, form, True) if self.diff else _dot(a, b, form, True)

    def tri_inv(self, a):
        return _tri_inv_d(a) if self.diff else _tri_inv_impl(a, _mmh_plain)


_PLAIN = _Ops(False)
_DIFF = _Ops(True)


def _sigmoid(x):
    return 1.0 / (1.0 + jnp.exp(-x))


def _silu(x):
    return x * _sigmoid(x)


def _softplus(x):
    return jnp.maximum(x, 0.0) + jnp.log(1.0 + jnp.exp(-jnp.abs(x)))


def _rms(x, g):
    return x * lax.rsqrt(jnp.mean(x * x, axis=-1, keepdims=True) + EPS) * g


def _matmul(name, a, b, form, out_dtype, add=None, tm=1024, tn=512, tk=2048):
    if form == "nn":
        (m, k), (k2, n) = a.shape, b.shape
    elif form == "nt":
        (m, k), (n, k2) = a.shape, b.shape
    else:
        (k, m), (k2, n) = a.shape, b.shape
    assert k == k2, (name, a.shape, b.shape, form)
    tm, tn, tk = _tile(m, tm), _tile(n, tn), _tile(k, tk)
    nk = k // tk
    if form == "nn":
        a_spec = pl.BlockSpec((tm, tk), lambda i, j, kk: (i, kk))
        b_spec = pl.BlockSpec((tk, tn), lambda i, j, kk: (kk, j))
    elif form == "nt":
        a_spec = pl.BlockSpec((tm, tk), lambda i, j, kk: (i, kk))
        b_spec = pl.BlockSpec((tn, tk), lambda i, j, kk: (j, kk))
    else:
        a_spec = pl.BlockSpec((tk, tm), lambda i, j, kk: (kk, i))
        b_spec = pl.BlockSpec((tk, tn), lambda i, j, kk: (kk, j))
    o_spec = pl.BlockSpec((tm, tn), lambda i, j, kk: (i, j))
    has_add = add is not None

    def body(*refs):
        a_ref, b_ref = refs[0], refs[1]
        add_ref = refs[2] if has_add else None
        o_ref = refs[3] if has_add else refs[2]

        def finish(acc):
            if has_add:
                acc = acc + add_ref[...].astype(F32)
            o_ref[...] = acc.astype(o_ref.dtype)

        p = _dot(a_ref[...], b_ref[...], form)
        if nk == 1:
            finish(p)
        else:
            acc_ref = refs[-1]
            kk = pl.program_id(2)

            @pl.when(kk == 0)
            def _():
                acc_ref[...] = p

            @pl.when(kk > 0)
            def _():
                acc_ref[...] += p

            @pl.when(kk == nk - 1)
            def _():
                finish(acc_ref[...])

    return pl.pallas_call(
        body, name=name, grid=(m // tm, n // tn, nk),
        in_specs=[a_spec, b_spec] + ([o_spec] if has_add else []), out_specs=o_spec,
        out_shape=jax.ShapeDtypeStruct((m, n), out_dtype),
        scratch_shapes=[pltpu.VMEM((tm, tn), F32)] if nk > 1 else [],
        compiler_params=_params("parallel", "parallel", "arbitrary"),
    )(*((a, b, add) if has_add else (a, b)))


def _rms_fwd(name, x, g):
    t, d = x.shape
    tm = _tile(t, 512, 16)

    def body(x_ref, g_ref, o_ref):
        o_ref[...] = _rms(x_ref[...], g_ref[...]).astype(o_ref.dtype)

    return pl.pallas_call(
        body, name=name, grid=(t // tm,),
        in_specs=[pl.BlockSpec((tm, d), lambda i: (i, 0)), pl.BlockSpec((1, d), lambda i: (0, 0))],
        out_specs=pl.BlockSpec((tm, d), lambda i: (i, 0)),
        out_shape=jax.ShapeDtypeStruct((t, d), MXU_DTYPE), compiler_params=_params("parallel"),
    )(x, g.reshape(1, d))


def _rms_bwd(name, x, g, dh, dres=None):
    t, d = x.shape
    tm = _tile(t, 256, 16)
    has_res = dres is not None

    def body(*refs):
        x_ref, g_ref, dh_ref = refs[:3]
        dres_ref = refs[3] if has_res else None
        dx_ref, dg_ref = refs[-2:]
        _, vjp = jax.vjp(_rms, x_ref[...], g_ref[...])
        dx, dg = vjp(dh_ref[...].astype(F32))
        if has_res:
            dx = dx + dres_ref[...]
        dx_ref[...] = dx
        first = pl.program_id(0) == 0

        @pl.when(first)
        def _():
            dg_ref[...] = dg

        @pl.when(jnp.logical_not(first))
        def _():
            dg_ref[...] += dg

    row = pl.BlockSpec((tm, d), lambda i: (i, 0))
    vec = pl.BlockSpec((1, d), lambda i: (0, 0))
    dx, dg = pl.pallas_call(
        body, name=name, grid=(t // tm,),
        in_specs=[row, vec, row] + ([row] if has_res else []), out_specs=[row, vec],
        out_shape=[jax.ShapeDtypeStruct((t, d), F32), jax.ShapeDtypeStruct((1, d), F32)],
        compiler_params=_params("arbitrary"),
    )(*((x, g.reshape(1, d), dh) + ((dres,) if has_res else ())))
    return dx, dg.reshape(d)


def _final_loss(x, g, target):
    t, d = x.shape
    tm = _tile(t, 256, 16)

    def body(x_ref, g_ref, t_ref, loss_ref, dx_ref, dg_ref):
        y, vjp = jax.vjp(_rms, x_ref[...], g_ref[...])
        err = y - t_ref[...]
        dx, dg = vjp(err * (1.0 / d))
        dx_ref[...] = dx
        part = jnp.zeros((1, LANES), F32) + 0.5 * jnp.sum(jnp.mean(err * err, axis=-1, keepdims=True))
        first = pl.program_id(0) == 0

        @pl.when(first)
        def _():
            dg_ref[...] = dg
            loss_ref[...] = part

        @pl.when(jnp.logical_not(first))
        def _():
            dg_ref[...] += dg
            loss_ref[...] += part

    row = pl.BlockSpec((tm, d), lambda i: (i, 0))
    vec = pl.BlockSpec((1, d), lambda i: (0, 0))
    loss, dx, dg = pl.pallas_call(
        body, name="final_loss", grid=(t // tm,), in_specs=[row, vec, row],
        out_specs=[pl.BlockSpec((1, LANES), lambda i: (0, 0)), row, vec],
        out_shape=[jax.ShapeDtypeStruct((1, LANES), F32), jax.ShapeDtypeStruct((t, d), F32),
                   jax.ShapeDtypeStruct((1, d), F32)],
        compiler_params=_params("arbitrary"),
    )(x, g.reshape(1, d), target)
    return loss, dx, dg.reshape(d)


def _conv_taps(x_ext, w, rows):
    kk = w.shape[0]
    y = x_ext[HALO:] * w[kk - 1:kk, :]
    for j in range(kk - 1):
        y = y + pltpu.roll(x_ext, kk - 1 - j, axis=0)[HALO:] * w[j:j + 1, :]
    return y


def _col_specs(tm, tn, col0, t_rows):
    assert col0 % tn == 0 and tm % HALO == 0
    c0 = col0 // tn
    per, last = tm // HALO, t_rows // HALO - 1
    tile = pl.BlockSpec((tm, tn), lambda j, i: (i, c0 + j))
    prev = pl.BlockSpec((HALO, tn), lambda j, i: (jnp.maximum(i * per - 1, 0), c0 + j))
    nxt = pl.BlockSpec((HALO, tn), lambda j, i: (jnp.minimum((i + 1) * per, last), c0 + j))
    return tile, prev, nxt


def _conv_fwd(name, xa, xa_col, w, w_col, ncols, out_dtype, xb=None, xb_col=0, gate=None, gate_col=0):
    t = xa.shape[0]
    kk = w.shape[0]
    tm, tn = _tile(t, 512, HALO), _tile(ncols, 512)
    nrow = t // tm
    has_b, has_g = xb is not None, gate is not None

    def body(*refs):
        refs = list(refs)
        xa_ref, xap_ref = refs.pop(0), refs.pop(0)
        xb_ref, xbp_ref = (refs.pop(0), refs.pop(0)) if has_b else (None, None)
        w_ref = refs.pop(0)
        g_ref = refs.pop(0) if has_g else None
        o_ref = refs.pop(0)
        i = pl.program_id(1)
        x, xp = xa_ref[...].astype(F32), xap_ref[...].astype(F32)
        if has_b:
            x, xp = x * xb_ref[...].astype(F32), xp * xbp_ref[...].astype(F32)
        xp = jnp.where(i == 0, 0.0, xp)
        y = _conv_taps(jnp.concatenate([xp, x], axis=0), w_ref[...], tm)
        if has_g:
            y = y * g_ref[...].astype(F32)
        o_ref[...] = y.astype(o_ref.dtype)

    a_tile, a_prev, _ = _col_specs(tm, tn, xa_col, t)
    ins, specs = [xa, xa], [a_tile, a_prev]
    if has_b:
        b_tile, b_prev, _ = _col_specs(tm, tn, xb_col, t)
        ins, specs = ins + [xb, xb], specs + [b_tile, b_prev]
    assert w_col % tn == 0
    ins, specs = ins + [w], specs + [pl.BlockSpec((kk, tn), lambda j, i: (0, w_col // tn + j))]
    if has_g:
        ins, specs = ins + [gate], specs + [_col_specs(tm, tn, gate_col, t)[0]]
    return pl.pallas_call(
        body, name=name, grid=(ncols // tn, nrow), in_specs=specs,
        out_specs=pl.BlockSpec((tm, tn), lambda j, i: (i, j)),
        out_shape=jax.ShapeDtypeStruct((t, ncols), out_dtype), compiler_params=_params("parallel", "parallel"),
    )(*ins)


def _conv_bwd(name, xa, xa_col, w, w_col, dy, dy_col, ncols, dx_dtype, xb=None, xb_col=0, gate=None, gate_col=0):
    t = xa.shape[0]
    kk = w.shape[0]
    tm, tn = _tile(t, 512, HALO), _tile(ncols, 512)
    nrow = t // tm
    has_b, has_g = xb is not None, gate is not None

    def body(*refs):
        refs = list(refs)
        xa_ref, xap_ref = refs.pop(0), refs.pop(0)
        xb_ref, xbp_ref = (refs.pop(0), refs.pop(0)) if has_b else (None, None)
        w_ref = refs.pop(0)
        dy_ref, dyn_ref = refs.pop(0), refs.pop(0)
        g_ref, gn_ref = (refs.pop(0), refs.pop(0)) if has_g else (None, None)
        dxa_ref = refs.pop(0)
        dxb_ref = refs.pop(0) if has_b else None
        dg_ref = refs.pop(0) if has_g else None
        dw_ref = refs.pop(0)
        i = pl.program_id(1)
        wv = w_ref[...]
        xa_t, xa_p = xa_ref[...].astype(F32), xap_ref[...].astype(F32)
        x, xp = xa_t, xa_p
        if has_b:
            xb_t = xb_ref[...].astype(F32)
            x, xp = x * xb_t, xp * xbp_ref[...].astype(F32)
        xp = jnp.where(i == 0, 0.0, xp)
        x_ext = jnp.concatenate([xp, x], axis=0)
        dyv, dyn = dy_ref[...].astype(F32), dyn_ref[...].astype(F32)
        if has_g:
            dg_ref[...] = (dyv * _conv_taps(x_ext, wv, tm)).astype(dg_ref.dtype)
            dyv, dyn = dyv * g_ref[...].astype(F32), dyn * gn_ref[...].astype(F32)
        dyn = jnp.where(i == nrow - 1, 0.0, dyn)
        dy_ext = jnp.concatenate([dyv, dyn], axis=0)
        dx = dyv * wv[kk - 1:kk, :]
        row8 = lax.broadcasted_iota(jnp.int32, (8, tn), 0)
        dw = jnp.where(row8 == kk - 1, jnp.sum(dyv * x, axis=0, keepdims=True), 0.0)
        for j in range(kk - 1):
            s = kk - 1 - j
            dx = dx + pltpu.roll(dy_ext, tm + HALO - s, axis=0)[:tm] * wv[j:j + 1, :]
            dwj = jnp.sum(dyv * pltpu.roll(x_ext, s, axis=0)[HALO:], axis=0, keepdims=True)
            dw = dw + jnp.where(row8 == j, dwj, 0.0)
        if has_b:
            dxa_ref[...] = (dx * xb_t).astype(dxa_ref.dtype)
            dxb_ref[...] = (dx * xa_t).astype(dxb_ref.dtype)
        else:
            dxa_ref[...] = dx.astype(dxa_ref.dtype)

        @pl.when(i == 0)
        def _():
            dw_ref[...] = dw

        @pl.when(i > 0)
        def _():
            dw_ref[...] += dw

    a_tile, a_prev, _ = _col_specs(tm, tn, xa_col, t)
    ins, specs = [xa, xa], [a_tile, a_prev]
    if has_b:
        b_tile, b_prev, _ = _col_specs(tm, tn, xb_col, t)
        ins, specs = ins + [xb, xb], specs + [b_tile, b_prev]
    assert w_col % tn == 0
    ins, specs = ins + [w], specs + [pl.BlockSpec((kk, tn), lambda j, i: (0, w_col // tn + j))]
    d_tile, _, d_next = _col_specs(tm, tn, dy_col, t)
    ins, specs = ins + [dy, dy], specs + [d_tile, d_next]
    if has_g:
        g_tile, _, g_next = _col_specs(tm, tn, gate_col, t)
        ins, specs = ins + [gate, gate], specs + [g_tile, g_next]
    out_tile = pl.BlockSpec((tm, tn), lambda j, i: (i, j))
    shapes, ospecs = [jax.ShapeDtypeStruct((t, ncols), dx_dtype)], [out_tile]
    if has_b:
        shapes, ospecs = shapes + [jax.ShapeDtypeStruct((t, ncols), dx_dtype)], ospecs + [out_tile]
    if has_g:
        shapes, ospecs = shapes + [jax.ShapeDtypeStruct((t, ncols), dx_dtype)], ospecs + [out_tile]
    shapes, ospecs = shapes + [jax.ShapeDtypeStruct((8, ncols), F32)], ospecs + [pl.BlockSpec((8, tn), lambda j, i: (0, j))]
    outs = list(pl.pallas_call(
        body, name=name, grid=(ncols // tn, nrow), in_specs=specs, out_specs=ospecs, out_shape=shapes,
        compiler_params=_params("parallel", "arbitrary"),
    )(*ins))
    dxa = outs.pop(0)
    dxb = outs.pop(0) if has_b else None
    dgate = outs.pop(0) if has_g else None
    return dxa, dxb, dgate, outs.pop(0)[:kk]


def _swiglu_fwd(u):
    t, f2 = u.shape
    f = f2 // 2
    tm, tn = _tile(t, 512, 16), _tile(f, 512)
    nf = f // tn

    def body(g_ref, u_ref, o_ref):
        o_ref[...] = (_silu(g_ref[...].astype(F32)) * u_ref[...].astype(F32)).astype(o_ref.dtype)

    return pl.pallas_call(
        body, name="swiglu_fwd", grid=(t // tm, nf),
        in_specs=[pl.BlockSpec((tm, tn), lambda i, j: (i, j)), pl.BlockSpec((tm, tn), lambda i, j: (i, nf + j))],
        out_specs=pl.BlockSpec((tm, tn), lambda i, j: (i, j)),
        out_shape=jax.ShapeDtypeStruct((t, f), MXU_DTYPE), compiler_params=_params("parallel", "parallel"),
    )(u, u)


def _swiglu_bwd(u, da):
    t, f2 = u.shape
    f = f2 // 2
    tm, tn = _tile(t, 512, 16), _tile(f, 512)
    nf = f // tn

    def body(g_ref, u_ref, da_ref, o_ref):
        g, up, d = g_ref[...].astype(F32), u_ref[...].astype(F32), da_ref[...].astype(F32)
        sg = _sigmoid(g)
        dgate = d * up * (sg * (1.0 + g * (1.0 - sg)))
        dup = d * (g * sg)
        o_ref[...] = jnp.where(pl.program_id(1) < nf, dgate, dup).astype(o_ref.dtype)

    return pl.pallas_call(
        body, name="swiglu_bwd", grid=(t // tm, 2 * nf),
        in_specs=[pl.BlockSpec((tm, tn), lambda i, j: (i, j % nf)),
                  pl.BlockSpec((tm, tn), lambda i, j: (i, nf + j % nf)),
                  pl.BlockSpec((tm, tn), lambda i, j: (i, j % nf))],
        out_specs=pl.BlockSpec((tm, tn), lambda i, j: (i, j)),
        out_shape=jax.ShapeDtypeStruct((t, f2), MXU_DTYPE), compiler_params=_params("parallel", "parallel"),
    )(u, u, da)


def _gdn_prep(ops, qc, kc, vc, b_col, a_col, a_log, dt_bias):
    c, dh = qc.shape
    q, k, v = _silu(qc), _silu(kc), _silu(vc)
    q = q * lax.rsqrt(jnp.sum(q * q, axis=-1, keepdims=True) + EPS) * (dh ** -0.5)
    k = k * lax.rsqrt(jnp.sum(k * k, axis=-1, keepdims=True) + EPS)
    beta = _sigmoid(b_col)
    g_col = -jnp.exp(a_log) * _softplus(a_col + dt_bias)
    r = lax.broadcasted_iota(jnp.int32, (c, c), 0)
    s = lax.broadcasted_iota(jnp.int32, (c, c), 1)
    g_row = jnp.sum(jnp.where(r == s, g_col, 0.0), axis=0, keepdims=True)
    gc_col = jnp.sum(jnp.where(s <= r, g_row, 0.0), axis=1, keepdims=True)
    gc_row = jnp.sum(jnp.where(r <= s, g_col, 0.0), axis=0, keepdims=True)
    decay = jnp.exp(jnp.where(s <= r, gc_col - gc_row, -1e30))
    kb = k * beta
    a = jnp.where(s < r, ops.mm(kb, k, "nt") * decay, 0.0)
    tinv = ops.tri_inv(a)
    e_col = jnp.exp(gc_col)
    u = ops.mmh(tinv, v * beta)
    w = ops.mmh(tinv, kb * e_col)
    attn = ops.mm(q, k, "nt") * decay
    g_last = jnp.sum(g_col, axis=0, keepdims=True)
    return u, w, attn, q * e_col, k * jnp.exp(g_last - gc_col), g_last


def _gdn_step(ops, state, u, w, attn, q_dec, k_dec, g_last):
    v_new = u - ops.mm(w, state)
    o = ops.mm(q_dec, state) + ops.mm(attn, v_new)
    return o, state * jnp.exp(g_last) + ops.mm(k_dec, v_new, "tn")


def _gdn_blocks(t, heads):
    tc = _tile(t, 256, CHUNK)
    hb = 2 if heads % 2 == 0 else 1
    return tc, hb


def _gdn_prep_fwd(qkv, b_col, a_col, a_log, dt_bias, heads, dh):
    t = qkv.shape[0]
    tc, hb = _gdn_blocks(t, heads)
    ncb, nhb, width = tc // CHUNK, heads // hb, heads * dh
    nc = t // CHUNK

    def body(q_ref, k_ref, v_ref, b_ref, a_ref, al_ref, dt_ref, u_ref, w_ref, p_ref, qd_ref, kd_ref, gl_ref):
        for h in range(hb):
            ls = slice(h * dh, (h + 1) * dh)
            for c in range(ncb):
                rs = slice(c * CHUNK, (c + 1) * CHUNK)
                u, w, p, qd, kd, gl = _gdn_prep(_PLAIN, q_ref[rs, ls], k_ref[rs, ls], v_ref[rs, ls], b_ref[h, c],
                                                a_ref[h, c], al_ref[h], dt_ref[h])
                u_ref[rs, ls] = u
                w_ref[rs, ls] = w.astype(w_ref.dtype)
                qd_ref[rs, ls] = qd.astype(qd_ref.dtype)
                kd_ref[rs, ls] = kd.astype(kd_ref.dtype)
                p_ref[h, rs, :] = p.astype(p_ref.dtype)
                gl_ref[h, c] = gl

    def tok(off):
        return pl.BlockSpec((tc, hb * dh), lambda i, j: (i, off * nhb + j))

    gate = pl.BlockSpec((hb, ncb, CHUNK, 1), lambda i, j: (j, i, 0, 0))
    scal = pl.BlockSpec((hb, 1, 1), lambda i, j: (j, 0, 0))
    return pl.pallas_call(
        body, name="gdn_prep_fwd", grid=(t // tc, nhb),
        in_specs=[tok(0), tok(1), tok(2), gate, gate, scal, scal],
        out_specs=[tok(0), tok(0), pl.BlockSpec((hb, tc, CHUNK), lambda i, j: (j, i, 0)), tok(0), tok(0),
                   pl.BlockSpec((hb, ncb, 1, 1), lambda i, j: (j, i, 0, 0))],
        out_shape=[jax.ShapeDtypeStruct((t, width), F32), jax.ShapeDtypeStruct((t, width), MXU_DTYPE),
                   jax.ShapeDtypeStruct((heads, t, CHUNK), MXU_DTYPE), jax.ShapeDtypeStruct((t, width), MXU_DTYPE),
                   jax.ShapeDtypeStruct((t, width), MXU_DTYPE), jax.ShapeDtypeStruct((heads, nc, 1, 1), F32)],
        compiler_params=_params("parallel", "parallel"),
    )(qkv, qkv, qkv, b_col, a_col, a_log, dt_bias)


def _gdn_prep_bwd(qkv, b_col, a_col, a_log, dt_bias, du, dw, dp, dqd, dkd, dgl, heads, dh):
    t = qkv.shape[0]
    tc, hb = _gdn_blocks(t, heads)
    ncb, nhb, width = tc // CHUNK, heads // hb, heads * dh
    nc = t // CHUNK

    def body(q_ref, k_ref, v_ref, b_ref, a_ref, al_ref, dt_ref, du_ref, dw_ref, dp_ref, dqd_ref, dkd_ref, dgl_ref,
             dq_ref, dk_ref, dv_ref, db_ref, da_ref, dal_ref, ddt_ref):
        first = pl.program_id(1) == 0
        for h in range(hb):
            ls = slice(h * dh, (h + 1) * dh)
            dal, ddt = jnp.zeros((1, 1), F32), jnp.zeros((1, 1), F32)
            for c in range(ncb):
                rs = slice(c * CHUNK, (c + 1) * CHUNK)
                _, vjp = jax.vjp(functools.partial(_gdn_prep, _DIFF), q_ref[rs, ls], k_ref[rs, ls], v_ref[rs, ls],
                                 b_ref[h, c], a_ref[h, c], al_ref[h], dt_ref[h])
                dq, dk, dv, db, da, dal_c, ddt_c = vjp((du_ref[rs, ls], dw_ref[rs, ls], dp_ref[h, rs, :],
                                                        dqd_ref[rs, ls], dkd_ref[rs, ls], dgl_ref[h, c]))
                dq_ref[rs, ls] = dq
                dk_ref[rs, ls] = dk
                dv_ref[rs, ls] = dv
                db_ref[h, c] = db
                da_ref[h, c] = da
                dal, ddt = dal + dal_c, ddt + ddt_c

            @pl.when(first)
            def _():
                dal_ref[h] = dal
                ddt_ref[h] = ddt

            @pl.when(jnp.logical_not(first))
            def _():
                dal_ref[h] += dal
                ddt_ref[h] += ddt

    def tok(off):
        return pl.BlockSpec((tc, hb * dh), lambda j, i: (i, off * nhb + j))

    gate = pl.BlockSpec((hb, ncb, CHUNK, 1), lambda j, i: (j, i, 0, 0))
    scal = pl.BlockSpec((hb, 1, 1), lambda j, i: (j, 0, 0))
    pspec = pl.BlockSpec((hb, tc, CHUNK), lambda j, i: (j, i, 0))
    glspec = pl.BlockSpec((hb, ncb, 1, 1), lambda j, i: (j, i, 0, 0))
    tokf = jax.ShapeDtypeStruct((t, width), F32)
    gatef = jax.ShapeDtypeStruct((heads, nc, CHUNK, 1), F32)
    scalf = jax.ShapeDtypeStruct((heads, 1, 1), F32)
    return pl.pallas_call(
        body, name="gdn_prep_bwd", grid=(nhb, t // tc),
        in_specs=[tok(0), tok(1), tok(2), gate, gate, scal, scal, tok(0), tok(0), pspec, tok(0), tok(0), glspec],
        out_specs=[tok(0), tok(0), tok(0), gate, gate, scal, scal],
        out_shape=[tokf, tokf, tokf, gatef, gatef, scalf, scalf],
        compiler_params=_params("arbitrary", "arbitrary"),
    )(qkv, qkv, qkv, b_col, a_col, a_log, dt_bias, du, dw, dp, dqd, dkd, dgl)


def _gdn_scan_fwd(u, w, p, qd, kd, gl, heads, dh):
    t = u.shape[0]
    tc, hb = _gdn_blocks(t, heads)
    ncb, nhb = tc // CHUNK, heads // hb
    nc = t // CHUNK

    def body(u_ref, w_ref, p_ref, qd_ref, kd_ref, gl_ref, o_ref, s_ref, state):
        @pl.when(pl.program_id(1) == 0)
        def _():
            state[...] = jnp.zeros_like(state)

        for c in range(ncb):
            rs = slice(c * CHUNK, (c + 1) * CHUNK)
            for h in range(hb):
                ls = slice(h * dh, (h + 1) * dh)
                s_in = state[h]
                s_ref[h, c] = s_in
                o, s_out = _gdn_step(_PLAIN, s_in, u_ref[rs, ls], w_ref[rs, ls], p_ref[h, rs, :], qd_ref[rs, ls],
                                     kd_ref[rs, ls], gl_ref[h, c])
                o_ref[rs, ls] = o
                state[h] = s_out

    tok = pl.BlockSpec((tc, hb * dh), lambda j, i: (i, j))
    pspec = pl.BlockSpec((hb, tc, CHUNK), lambda j, i: (j, i, 0))
    glspec = pl.BlockSpec((hb, ncb, 1, 1), lambda j, i: (j, i, 0, 0))
    return pl.pallas_call(
        body, name="gdn_scan_fwd", grid=(nhb, t // tc),
        in_specs=[tok, tok, pspec, tok, tok, glspec],
        out_specs=[tok, pl.BlockSpec((hb, ncb, dh, dh), lambda j, i: (j, i, 0, 0))],
        out_shape=[jax.ShapeDtypeStruct((t, heads * dh), F32), jax.ShapeDtypeStruct((heads, nc, dh, dh), F32)],
        scratch_shapes=[pltpu.VMEM((hb, dh, dh), F32)],
        compiler_params=_params("arbitrary", "arbitrary"),
    )(u, w, p, qd, kd, gl)


def _gdn_scan_bwd(u, w, p, qd, kd, gl, states, do, heads, dh):
    t = u.shape[0]
    tc, hb = _gdn_blocks(t, heads)
    ncb, nhb = tc // CHUNK, heads // hb
    nc, nt = t // CHUNK, t // tc

    def body(u_ref, w_ref, p_ref, qd_ref, kd_ref, gl_ref, s_ref, do_ref,
             du_ref, dw_ref, dp_ref, dqd_ref, dkd_ref, dgl_ref, dstate):
        @pl.when(pl.program_id(1) == 0)
        def _():
            dstate[...] = jnp.zeros_like(dstate)

        for c in reversed(range(ncb)):
            rs = slice(c * CHUNK, (c + 1) * CHUNK)
            for h in range(hb):
                ls = slice(h * dh, (h + 1) * dh)
                _, vjp = jax.vjp(functools.partial(_gdn_step, _DIFF), s_ref[h, c], u_ref[rs, ls],
                                 w_ref[rs, ls].astype(F32), p_ref[h, rs, :].astype(F32), qd_ref[rs, ls].astype(F32),
                                 kd_ref[rs, ls].astype(F32), gl_ref[h, c])
                ds, du, dw, dp, dqd, dkd, dgl = vjp((do_ref[rs, ls], dstate[h]))
                dstate[h] = ds
                du_ref[rs, ls] = du
                dw_ref[rs, ls] = dw
                dqd_ref[rs, ls] = dqd
                dkd_ref[rs, ls] = dkd
                dp_ref[h, rs, :] = dp
                dgl_ref[h, c] = dgl

    tok = pl.BlockSpec((tc, hb * dh), lambda j, i: (nt - 1 - i, j))
    pspec = pl.BlockSpec((hb, tc, CHUNK), lambda j, i: (j, nt - 1 - i, 0))
    glspec = pl.BlockSpec((hb, ncb, 1, 1), lambda j, i: (j, nt - 1 - i, 0, 0))
    sspec = pl.BlockSpec((hb, ncb, dh, dh), lambda j, i: (j, nt - 1 - i, 0, 0))
    tokf = jax.ShapeDtypeStruct((t, heads * dh), F32)
    return pl.pallas_call(
        body, name="gdn_scan_bwd", grid=(nhb, nt),
        in_specs=[tok, tok, pspec, tok, tok, glspec, sspec, tok],
        out_specs=[tok, tok, pspec, tok, tok, glspec],
        out_shape=[tokf, tokf, jax.ShapeDtypeStruct((heads, t, CHUNK), F32), tokf, tokf,
                   jax.ShapeDtypeStruct((heads, nc, 1, 1), F32)],
        scratch_shapes=[pltpu.VMEM((hb, dh, dh), F32)],
        compiler_params=_params("arbitrary", "arbitrary"),
    )(u, w, p, qd, kd, gl, states, do)


def _gdn_post(o, z, gain):
    return _rms(o, gain) * _silu(z)


def _gdn_post_fwd(o, pm, z_col, gain, heads, dh):
    t = o.shape[0]
    tm = _tile(t, 512, 16)
    z0 = z_col // dh

    def body(o_ref, z_ref, g_ref, y_ref):
        y_ref[...] = _gdn_post(o_ref[...], z_ref[...], g_ref[...]).astype(y_ref.dtype)

    return pl.pallas_call(
        body, name="gdn_post_fwd", grid=(t // tm, heads),
        in_specs=[pl.BlockSpec((tm, dh), lambda i, h: (i, h)), pl.BlockSpec((tm, dh), lambda i, h: (i, z0 + h)),
                  pl.BlockSpec((1, dh), lambda i, h: (0, 0))],
        out_specs=pl.BlockSpec((tm, dh), lambda i, h: (i, h)),
        out_shape=jax.ShapeDtypeStruct((t, heads * dh), MXU_DTYPE), compiler_params=_params("parallel", "parallel"),
    )(o, pm, gain.reshape(1, dh))


def _gdn_post_bwd(o, pm, z_col, gain, dy, heads, dh):
    t = o.shape[0]
    tm = _tile(t, 512, 16)
    z0 = z_col // dh

    def body(o_ref, z_ref, g_ref, dy_ref, do_ref, dz_ref, dg_ref):
        _, vjp = jax.vjp(_gdn_post, o_ref[...], z_ref[...], g_ref[...])
        do, dz, dg = vjp(dy_ref[...])
        do_ref[...] = do
        dz_ref[...] = dz.astype(dz_ref.dtype)
        first = jnp.logical_and(pl.program_id(0) == 0, pl.program_id(1) == 0)

        @pl.when(first)
        def _():
            dg_ref[...] = dg

        @pl.when(jnp.logical_not(first))
        def _():
            dg_ref[...] += dg

    blk = pl.BlockSpec((tm, dh), lambda i, h: (i, h))
    vec = pl.BlockSpec((1, dh), lambda i, h: (0, 0))
    do, dz, dg = pl.pallas_call(
        body, name="gdn_post_bwd", grid=(t // tm, heads),
        in_specs=[blk, pl.BlockSpec((tm, dh), lambda i, h: (i, z0 + h)), vec, blk],
        out_specs=[blk, blk, vec],
        out_shape=[jax.ShapeDtypeStruct((t, heads * dh), F32), jax.ShapeDtypeStruct((t, heads * dh), MXU_DTYPE),
                   jax.ShapeDtypeStruct((1, dh), F32)],
        compiler_params=_params("arbitrary", "arbitrary"),
    )(o, pm, gain.reshape(1, dh), dy)
    return do, dz, dg.reshape(dh)


def _attn(ops, q, kv):
    d = q.shape[1]
    hd = d // XATTN_HEADS
    outs = []
    for h in range(XATTN_HEADS):
        qh, kh, vh = q[:, h * hd:(h + 1) * hd], kv[:, h * hd:(h + 1) * hd], kv[:, d + h * hd:d + (h + 1) * hd]
        s = ops.mm(qh, kh, "nt") * (hd ** -0.5)
        e = jnp.exp(s - lax.stop_gradient(jnp.max(s, axis=-1, keepdims=True)))
        outs.append(ops.mm(e / jnp.sum(e, axis=-1, keepdims=True), vh))
    return jnp.concatenate(outs, axis=1)


def _attn_fwd(q, kv):
    t, d = q.shape
    nm = kv.shape[0]
    tm = _tile(t, 512, 16)

    def body(q_ref, kv_ref, o_ref):
        o_ref[...] = _attn(_PLAIN, q_ref[...], kv_ref[...]).astype(o_ref.dtype)

    return pl.pallas_call(
        body, name="xattn_fwd", grid=(t // tm,),
        in_specs=[pl.BlockSpec((tm, d), lambda i: (i, 0)), pl.BlockSpec((nm, 2 * d), lambda i: (0, 0))],
        out_specs=pl.BlockSpec((tm, d), lambda i: (i, 0)),
        out_shape=jax.ShapeDtypeStruct((t, d), MXU_DTYPE), compiler_params=_params("parallel"),
    )(q, kv)


def _attn_bwd(q, kv, do):
    t, d = q.shape
    nm = kv.shape[0]
    tm = _tile(t, 256, 16)

    def body(q_ref, kv_ref, do_ref, dq_ref, dkv_ref):
        _, vjp = jax.vjp(functools.partial(_attn, _DIFF), q_ref[...].astype(F32), kv_ref[...].astype(F32))
        dq, dkv = vjp(do_ref[...].astype(F32))
        dq_ref[...] = dq.astype(dq_ref.dtype)
        first = pl.program_id(0) == 0

        @pl.when(first)
        def _():
            dkv_ref[...] = dkv

        @pl.when(jnp.logical_not(first))
        def _():
            dkv_ref[...] += dkv

    row = pl.BlockSpec((tm, d), lambda i: (i, 0))
    full = pl.BlockSpec((nm, 2 * d), lambda i: (0, 0))
    return pl.pallas_call(
        body, name="xattn_bwd", grid=(t // tm,), in_specs=[row, full, row], out_specs=[row, full],
        out_shape=[jax.ShapeDtypeStruct((t, d), MXU_DTYPE), jax.ShapeDtypeStruct((nm, 2 * d), F32)],
        compiler_params=_params("arbitrary"),
    )(q, kv, do)


def _adamw(name, w, g, m, v):
    shape = w.shape
    cols = shape[-1]
    rows = w.size // cols
    w2, g2, m2, v2 = (a.reshape(rows, cols) for a in (w, g, m, v))
    tr = _tile(rows, max(8, (1 << 18) // cols // 8 * 8), 8)

    def body(w_ref, g_ref, m_ref, v_ref, d_ref, nm_ref, nv_ref):
        gv = g_ref[...]
        nm = ADAM_B1 * m_ref[...] + (1.0 - ADAM_B1) * gv
        nv = ADAM_B2 * v_ref[...] + (1.0 - ADAM_B2) * jnp.square(gv)
        m_hat = nm / (1.0 - ADAM_B1 ** ADAM_STEP)
        v_hat = nv / (1.0 - ADAM_B2 ** ADAM_STEP)
        d_ref[...] = -ADAM_LR * (m_hat / (jnp.sqrt(v_hat) + ADAM_EPS) + ADAM_WD * w_ref[...])
        nm_ref[...] = nm
        nv_ref[...] = nv

    blk = pl.BlockSpec((tr, cols), lambda i: (i, 0))
    out = jax.ShapeDtypeStruct((rows, cols), F32)
    d, nm, nv = pl.pallas_call(
        body, name=name, grid=(rows // tr,), in_specs=[blk] * 4, out_specs=[blk] * 3, out_shape=[out] * 3,
        compiler_params=_params("parallel"),
    )(w2, g2, m2, v2)
    return d.reshape(shape), nm.reshape(shape), nv.reshape(shape)


def _gate_cols(pba, heads):
    t = pba.shape[0]
    to_col = lambda a: a.T.reshape(heads, t // CHUNK, CHUNK, 1)
    return to_col(pba[:, :heads]), to_col(pba[:, heads:2 * heads])


def _layer_fwd(x, mem, p, heads, dh):
    wid = heads * dh
    sc = x.shape[1] - wid
    s = {"x0": x}
    s["h1"] = _rms_fwd("rms_mix", x, p["mix_norm"])
    s["pm"] = pm = _matmul("mm_mix_in", s["h1"], p["wmain"], "nn", F32)
    pba = _matmul("mm_mix_ba", s["h1"], p["wba"], "nn", F32)
    s["qkv"] = _conv_fwd("conv_gdn", pm, 0, p["gdn_conv"], 0, 3 * wid, F32)
    s["b_col"], s["a_col"] = _gate_cols(pba, heads)
    s["prep"] = _gdn_prep_fwd(s["qkv"], s["b_col"], s["a_col"], p["a_log"], p["dt_bias"], heads, dh)
    s["o"], s["states"] = _gdn_scan_fwd(*s["prep"], heads, dh)
    y_gdn = _gdn_post_fwd(s["o"], pm, 3 * wid, p["gdn_out_norm"], heads, dh)
    y_sc = _conv_fwd("conv_sc", pm, 4 * wid + sc, p["sc_conv"], 0, sc, MXU_DTYPE, xb=pm, xb_col=4 * wid + 2 * sc,
                     gate=pm, gate_col=4 * wid)
    s["ycat"] = jnp.concatenate([y_gdn, y_sc], axis=1)
    s["x1"] = x1 = _matmul("mm_mix_out", s["ycat"], p["wout"], "nn", F32, add=x)
    s["h2"] = _rms_fwd("rms_xattn", x1, p["xattn_norm"])
    s["q"] = _matmul("mm_xq", s["h2"], p["wq"], "nn", MXU_DTYPE)
    s["memn"] = _rms_fwd("rms_mem", mem, p["mem_norm"])
    s["kv"] = _matmul("mm_xkv", s["memn"], p["wkv"], "nn", MXU_DTYPE)
    s["ao"] = _attn_fwd(s["q"], s["kv"])
    s["x2"] = x2 = _matmul("mm_xo", s["ao"], p["wo"], "nn", F32, add=x1)
    s["h3"] = _rms_fwd("rms_ffn", x2, p["ffn_norm"])
    s["upre"] = _matmul("mm_ffn_up", s["h3"], p["wup"], "nn", MXU_DTYPE)
    s["uc"] = _conv_fwd("conv_ffn", s["upre"], 0, p["ffn_conv"], 0, s["upre"].shape[1], MXU_DTYPE)
    s["act"] = _swiglu_fwd(s["uc"])
    return _matmul("mm_ffn_down", s["act"], p["wdown"], "nn", F32, add=x2), s


def _layer_bwd(dx3, mem, s, p, heads, dh):
    wid = heads * dh
    sc = dx3.shape[1] - wid
    t = dx3.shape[0]
    pm = s["pm"]
    g = {}
    da = _matmul("mm_ffn_down_dx", dx3, p["wdown"], "nt", MXU_DTYPE)
    g["wdown"] = _matmul("mm_ffn_down_dw", s["act"], dx3, "tn", F32)
    du = _swiglu_bwd(s["uc"], da)
    dupre, _, _, g["ffn_conv"] = _conv_bwd("conv_ffn_bwd", s["upre"], 0, p["ffn_conv"], 0, du, 0, du.shape[1],
                                           MXU_DTYPE)
    dh3 = _matmul("mm_ffn_up_dx", dupre, p["wup"], "nt", F32)
    g["wup"] = _matmul("mm_ffn_up_dw", s["h3"], dupre, "tn", F32)
    dx2, g["ffn_norm"] = _rms_bwd("rms_ffn_bwd", s["x2"], p["ffn_norm"], dh3, dx3)
    dao = _matmul("mm_xo_dx", dx2, p["wo"], "nt", MXU_DTYPE)
    g["wo"] = _matmul("mm_xo_dw", s["ao"], dx2, "tn", F32)
    dq, dkv = _attn_bwd(s["q"], s["kv"], dao)
    dh2 = _matmul("mm_xq_dx", dq, p["wq"], "nt", F32)
    g["wq"] = _matmul("mm_xq_dw", s["h2"], dq, "tn", F32)
    dmemn = _matmul("mm_xkv_dx", dkv, p["wkv"], "nt", F32)
    g["wkv"] = _matmul("mm_xkv_dw", s["memn"], dkv, "tn", F32)
    _, g["mem_norm"] = _rms_bwd("rms_mem_bwd", mem, p["mem_norm"], dmemn)
    dx1, g["xattn_norm"] = _rms_bwd("rms_xattn_bwd", s["x1"], p["xattn_norm"], dh2, dx2)
    dycat = _matmul("mm_mix_out_dx", dx1, p["wout"], "nt", F32)
    g["wout"] = _matmul("mm_mix_out_dw", s["ycat"], dx1, "tn", F32)
    d_c, d_h, d_b, g["sc_conv"] = _conv_bwd("conv_sc_bwd", pm, 4 * wid + sc, p["sc_conv"], 0, dycat, wid, sc,
                                             MXU_DTYPE, xb=pm, xb_col=4 * wid + 2 * sc, gate=pm, gate_col=4 * wid)
    do, dz, g["gdn_out_norm"] = _gdn_post_bwd(s["o"], pm, 3 * wid, p["gdn_out_norm"], dycat, heads, dh)
    dprep = _gdn_scan_bwd(*s["prep"], s["states"], do, heads, dh)
    dqc, dkc, dvc, db_col, da_col, g["a_log"], g["dt_bias"] = _gdn_prep_bwd(
        s["qkv"], s["b_col"], s["a_col"], p["a_log"], p["dt_bias"], *dprep, heads, dh)
    dqkv, _, _, g["gdn_conv"] = _conv_bwd("conv_gdn_bwd", pm, 0, p["gdn_conv"], 0,
                                          jnp.concatenate([dqc, dkc, dvc], axis=1), 0, 3 * wid, MXU_DTYPE)
    dpm = jnp.concatenate([dqkv, dz, d_b, d_c, d_h], axis=1)
    from_col = lambda a: a.reshape(heads, t).T
    dpba = jnp.concatenate([from_col(db_col), from_col(da_col), jnp.zeros((t, LANES - 2 * heads), F32)],
                           axis=1).astype(MXU_DTYPE)
    dh1 = _matmul("mm_mix_in_dx", dpm, p["wmain"], "nt", F32)
    dh1 = _matmul("mm_mix_ba_dx", dpba, p["wba"], "nt", F32, add=dh1)
    g["wmain"] = _matmul("mm_mix_in_dw", s["h1"], dpm, "tn", F32)
    g["wba"] = _matmul("mm_mix_ba_dw", s["h1"], dpba, "tn", F32)
    dx0, g["mix_norm"] = _rms_bwd("rms_mix_bwd", s["x0"], p["mix_norm"], dh1, dx1)
    return dx0, g


def _local_step(x, mem, target, w):
    depth, heads = w["gdn_a_log"].shape
    dh = w["gdn_out_norm"].shape[1]
    wid = heads * dh
    win = w["w_mix_in"]
    layers = []
    for l in range(depth):
        layers.append({
            "mix_norm": w["mix_norm"][l], "xattn_norm": w["xattn_norm"][l], "mem_norm": w["mem_norm"][l],
            "ffn_norm": w["ffn_norm"][l], "gdn_out_norm": w["gdn_out_norm"][l],
            "a_log": w["gdn_a_log"][l].reshape(heads, 1, 1), "dt_bias": w["gdn_dt_bias"][l].reshape(heads, 1, 1),
            "gdn_conv": w["gdn_conv"][l], "sc_conv": w["sc_conv"][l], "ffn_conv": w["ffn_conv"][l],
            "wmain": jnp.concatenate([win[l, :, :4 * wid], win[l, :, 4 * wid + 2 * heads:]], axis=1),
            "wba": jnp.pad(win[l, :, 4 * wid:4 * wid + 2 * heads], ((0, 0), (0, LANES - 2 * heads))),
            "wout": w["w_mix_out"][l], "wq": w["w_xq"][l],
            "wkv": jnp.concatenate([w["w_xk"][l], w["w_xv"][l]], axis=1), "wo": w["w_xo"][l],
            "wup": w["w_ffn_up"][l], "wdown": w["w_ffn_down"][l],
        })
    saved = []
    for p in layers:
        x, s = _layer_fwd(x, mem, p, heads, dh)
        saved.append(s)
    loss, dx, g_final = _final_loss(x, w["final_norm"], target)
    per_layer = []
    for p, s in zip(reversed(layers), reversed(saved)):
        dx, g = _layer_bwd(dx, mem, s, p, heads, dh)
        per_layer.append(g)
    per_layer.reverse()
    d = x.shape[1]
    stack = lambda k: jnp.stack([g[k] for g in per_layer])
    gmain, gba, gkv = stack("wmain"), stack("wba"), stack("wkv")
    grads = {
        "mix_norm": stack("mix_norm"),
        "w_mix_in": jnp.concatenate([gmain[:, :, :4 * wid], gba[:, :, :2 * heads], gmain[:, :, 4 * wid:]], axis=2),
        "gdn_conv": stack("gdn_conv"), "gdn_a_log": stack("a_log").reshape(depth, heads),
        "gdn_dt_bias": stack("dt_bias").reshape(depth, heads), "gdn_out_norm": stack("gdn_out_norm"),
        "sc_conv": stack("sc_conv"), "w_mix_out": stack("wout"), "xattn_norm": stack("xattn_norm"),
        "mem_norm": stack("mem_norm"), "w_xq": stack("wq"), "w_xk": gkv[:, :, :d], "w_xv": gkv[:, :, d:],
        "w_xo": stack("wo"), "ffn_norm": stack("ffn_norm"), "w_ffn_up": stack("wup"), "ffn_conv": stack("ffn_conv"),
        "w_ffn_down": stack("wdown"), "final_norm": g_final,
    }
    return loss, dx, grads


WIRE_DTYPE = jnp.bfloat16
_ANY = pl.BlockSpec(memory_space=pl.ANY)
_VMEM = pl.BlockSpec(memory_space=pltpu.VMEM)


def _mesh_pos():
    return lax.axis_index("x"), lax.axis_index("y"), lax.axis_index("c")


def _other_chips(x, y):
    return [(1 - x, y), (x, 1 - y), (1 - x, 1 - y)]


def _allgather_chips(name, src):
    _, r, lanes = src.shape

    def body(src_ref, out_ref, send_sems, recv_sems, local_sem):
        x, y, c = _mesh_pos()
        me, sibling, chips = 2 * x + y, (x, y, 1 - c), _other_chips(x, y)
        local = pltpu.make_async_copy(src_ref, out_ref.at[me], local_sem)
        local.start()

        def copy(k, src_r, chip, half, to):
            return pltpu.make_async_remote_copy(src_ref=src_r, dst_ref=out_ref.at[chip, half], send_sem=send_sems.at[k],
                                                recv_sem=recv_sems.at[k], device_id=to, device_id_type=MESH)

        sends = [copy(k, src_ref.at[c], me, c, (cx, cy, c)) for k, (cx, cy) in enumerate(chips)]
        for cp in sends:
            cp.start()
        for k, (cx, cy) in enumerate(chips):
            chip = 2 * cx + cy
            copy(k, src_ref.at[c], chip, c, sibling).wait_recv()
            fwd = copy(3 + k, out_ref.at[chip, c], chip, c, sibling)
            fwd.start()
            sends.append(fwd)
        for k, (cx, cy) in enumerate(chips):
            copy(3 + k, src_ref.at[c], 2 * cx + cy, 1 - c, sibling).wait_recv()
        for cp in sends:
            cp.wait_send()
        local.wait()

    return pl.pallas_call(
        body, name=name, in_specs=[_ANY], out_specs=_ANY,
        out_shape=jax.ShapeDtypeStruct((N_CHIPS, 2, r, lanes), src.dtype),
        scratch_shapes=[pltpu.SemaphoreType.DMA((6,)), pltpu.SemaphoreType.DMA((6,)), pltpu.SemaphoreType.DMA],
    )(src)


def _sibling_exchange(g):
    _, n, r, lanes = g.shape

    def body(g_ref, out_ref, send_sem, recv_sem):
        x, y, c = _mesh_pos()
        cp = pltpu.make_async_remote_copy(src_ref=g_ref.at[1 - c], dst_ref=out_ref, send_sem=send_sem,
                                          recv_sem=recv_sem, device_id=(x, y, 1 - c), device_id_type=MESH)
        cp.start()
        cp.wait()

    return pl.pallas_call(
        body, name="rs_sibling_exchange", in_specs=[_ANY], out_specs=_ANY,
        out_shape=jax.ShapeDtypeStruct((n, r, lanes), g.dtype),
        scratch_shapes=[pltpu.SemaphoreType.DMA, pltpu.SemaphoreType.DMA],
    )(g)


def _chip_exchange(s):
    _, r, lanes = s.shape

    def body(s_ref, out_ref, send_sems, recv_sems):
        x, y, c = _mesh_pos()
        copies = [
            pltpu.make_async_remote_copy(src_ref=s_ref.at[2 * cx + cy], dst_ref=out_ref.at[k], send_sem=send_sems.at[k],
                                         recv_sem=recv_sems.at[k], device_id=(cx, cy, c), device_id_type=MESH)
            for k, (cx, cy) in enumerate(_other_chips(x, y))]
        for cp in copies:
            cp.start()
        for cp in copies:
            cp.wait()

    return pl.pallas_call(
        body, name="rs_chip_exchange", in_specs=[_ANY], out_specs=_ANY,
        out_shape=jax.ShapeDtypeStruct((3, r, lanes), s.dtype),
        scratch_shapes=[pltpu.SemaphoreType.DMA((3,)), pltpu.SemaphoreType.DMA((3,))],
    )(s)


def _sibling_share(v):
    r, lanes = v.shape

    def body(v_ref, out_ref, send_sem, recv_sem, local_sem):
        x, y, c = _mesh_pos()
        local = pltpu.make_async_copy(v_ref, out_ref.at[c], local_sem)
        local.start()
        cp = pltpu.make_async_remote_copy(src_ref=v_ref, dst_ref=out_ref.at[c], send_sem=send_sem, recv_sem=recv_sem,
                                          device_id=(x, y, 1 - c), device_id_type=MESH)
        cp.start()
        cp.wait()
        local.wait()

    return pl.pallas_call(
        body, name="rs_sibling_share", in_specs=[_ANY], out_specs=_ANY,
        out_shape=jax.ShapeDtypeStruct((2, r, lanes), v.dtype),
        scratch_shapes=[pltpu.SemaphoreType.DMA, pltpu.SemaphoreType.DMA, pltpu.SemaphoreType.DMA],
    )(v)


def _allreduce_small(v):
    r, lanes = v.shape

    def body(v_ref, sum_ref, gath, send_sems, recv_sems):
        x, y, c = _mesh_pos()
        me = 4 * x + 2 * y + c
        gath[me] = v_ref[...]
        copies = []
        for rel in range(1, N_DEV):
            peer = tuple(1 - p if (rel >> b) & 1 else p for p, b in ((x, 2), (y, 1), (c, 0)))
            copies.append(pltpu.make_async_remote_copy(
                src_ref=v_ref, dst_ref=gath.at[me], send_sem=send_sems.at[rel - 1], recv_sem=recv_sems.at[rel - 1],
                device_id=peer, device_id_type=MESH))
        for cp in copies:
            cp.start()
        for cp in copies:
            cp.wait()
        total = gath[0]
        for k in range(1, N_DEV):
            total = total + gath[k]
        sum_ref[...] = total

    return pl.pallas_call(
        body, name="allreduce_small", in_specs=[_VMEM], out_specs=_VMEM,
        out_shape=jax.ShapeDtypeStruct((r, lanes), F32),
        scratch_shapes=[pltpu.VMEM((N_DEV, r, lanes), F32), pltpu.SemaphoreType.DMA((N_DEV - 1,)),
                        pltpu.SemaphoreType.DMA((N_DEV - 1,))],
    )(v)


def _sum_rows(name, terms, out_dtype):
    r, lanes = terms[0][0].shape[1:]
    tr = _tile(r, 2048, 16)

    def body(*refs):
        acc = refs[0][...].astype(F32)
        for ref in refs[1:-1]:
            acc = acc + ref[...].astype(F32)
        refs[-1][...] = acc.astype(out_dtype)

    specs = [pl.BlockSpec((None, tr, lanes), functools.partial(lambda k, i: (k, i, 0), k)) for _, k in terms]
    return pl.pallas_call(
        body, name=name, grid=(r // tr,), in_specs=specs, out_specs=pl.BlockSpec((tr, lanes), lambda i: (i, 0)),
        out_shape=jax.ShapeDtypeStruct((r, lanes), out_dtype), compiler_params=_params("parallel"),
    )(*[a for a, _ in terms])


_SHARDED = (("w_mix_in", 2), ("gdn_conv", 2), ("sc_conv", 2), ("w_mix_out", 1), ("w_xq", 1), ("w_xk", 1),
            ("w_xv", 1), ("w_xo", 1), ("w_ffn_up", 2), ("ffn_conv", 2), ("w_ffn_down", 1))
_CONVS = ("gdn_conv", "sc_conv", "ffn_conv")
_REPLICATED = ("mix_norm", "gdn_a_log", "gdn_dt_bias", "gdn_out_norm", "xattn_norm", "mem_norm", "ffn_norm",
               "final_norm")
_WEIGHTS = ("mix_norm", "w_mix_in", "gdn_conv", "gdn_a_log", "gdn_dt_bias", "gdn_out_norm", "sc_conv", "w_mix_out",
            "xattn_norm", "mem_norm", "w_xq", "w_xk", "w_xv", "w_xo", "ffn_norm", "w_ffn_up", "ffn_conv",
            "w_ffn_down", "final_norm")


def _pad_rows(flat, groups):
    unit = groups * 16 * LANES
    p = flat.shape[-1]
    pad = -p % unit
    if pad:
        flat = jnp.pad(flat, [(0, 0)] * (flat.ndim - 1) + [(0, pad)])
    return flat.reshape(flat.shape[:-1] + (groups, (p + pad) // (groups * LANES), LANES))


def _pack_shards(shards, dtype):
    return _pad_rows(jnp.concatenate([s.astype(dtype).reshape(-1) for s in shards]), 2)


def _split_flat(flat, shapes):
    out, off = [], 0
    for shp in shapes:
        size = 1
        for n in shp:
            size *= n
        out.append(flat[..., off:off + size].reshape(flat.shape[:-1] + tuple(shp)))
        off += size
    return out


def _unpack_gathered(gath, shapes, axes):
    parts = _split_flat(gath.reshape(N_CHIPS, -1), shapes)
    full = []
    for part, shp, ax in zip(parts, shapes, axes):
        part = jnp.moveaxis(part, 0, ax)
        full.append(part.reshape(tuple(shp[:ax]) + (N_CHIPS * shp[ax],) + tuple(shp[ax + 1:])))
    return full


def _pack_grads(grads, axes, dtype):
    per_chip = []
    for g, ax in zip(grads, axes):
        shp = g.shape
        g = g.reshape(shp[:ax] + (N_CHIPS, shp[ax] // N_CHIPS) + shp[ax + 1:])
        per_chip.append(jnp.moveaxis(g, ax, 0).reshape(N_CHIPS, -1).astype(dtype))
    return jnp.moveaxis(_pad_rows(jnp.concatenate(per_chip, axis=1), 2), 1, 0)


def kernel(x, mem, mix_norm, w_mix_in, gdn_conv, gdn_a_log, gdn_dt_bias, gdn_out_norm, sc_conv, w_mix_out, xattn_norm, mem_norm, w_xq, w_xk, w_xv, w_xo, ffn_norm, w_ffn_up, ffn_conv, w_ffn_down, final_norm, loss_target, m_mix_norm, m_w_mix_in, m_gdn_conv, m_gdn_a_log, m_gdn_dt_bias, m_gdn_out_norm, m_sc_conv, m_w_mix_out, m_xattn_norm, m_mem_norm, m_w_xq, m_w_xk, m_w_xv, m_w_xo, m_ffn_norm, m_w_ffn_up, m_ffn_conv, m_w_ffn_down, m_final_norm, v_mix_norm, v_w_mix_in, v_gdn_conv, v_gdn_a_log, v_gdn_dt_bias, v_gdn_out_norm, v_sc_conv, v_w_mix_out, v_xattn_norm, v_mem_norm, v_w_xq, v_w_xk, v_w_xv, v_w_xo, v_ffn_norm, v_w_ffn_up, v_ffn_conv, v_w_ffn_down, v_final_norm):
    w = dict(zip(_WEIGHTS, (mix_norm, w_mix_in, gdn_conv, gdn_a_log, gdn_dt_bias, gdn_out_norm, sc_conv, w_mix_out,
                            xattn_norm, mem_norm, w_xq, w_xk, w_xv, w_xo, ffn_norm, w_ffn_up, ffn_conv, w_ffn_down,
                            final_norm)))
    m = dict(zip(_WEIGHTS, (m_mix_norm, m_w_mix_in, m_gdn_conv, m_gdn_a_log, m_gdn_dt_bias, m_gdn_out_norm, m_sc_conv,
                            m_w_mix_out, m_xattn_norm, m_mem_norm, m_w_xq, m_w_xk, m_w_xv, m_w_xo, m_ffn_norm,
                            m_w_ffn_up, m_ffn_conv, m_w_ffn_down, m_final_norm)))
    v = dict(zip(_WEIGHTS, (v_mix_norm, v_w_mix_in, v_gdn_conv, v_gdn_a_log, v_gdn_dt_bias, v_gdn_out_norm, v_sc_conv,
                            v_w_mix_out, v_xattn_norm, v_mem_norm, v_w_xq, v_w_xk, v_w_xv, v_w_xo, v_ffn_norm,
                            v_w_ffn_up, v_ffn_conv, v_w_ffn_down, v_final_norm)))
    core = lax.axis_index("c")
    chip = 2 * lax.axis_index("x") + lax.axis_index("y")

    mats = [(n, ax) for n, ax in _SHARDED if n not in _CONVS]
    convs = [(n, ax) for n, ax in _SHARDED if n in _CONVS]
    full = dict(w)
    for group, dtype, name in ((mats, WIRE_DTYPE, "allgather_weights"), (convs, F32, "allgather_convs")):
        gath = _allgather_chips(name, _pack_shards([w[n] for n, _ in group], dtype))
        for (n, _), arr in zip(group, _unpack_gathered(gath, [w[n].shape for n, _ in group], [ax for _, ax in group])):
            full[n] = arr

    loss_row, dx, grads = _local_step(x[0], mem[0], loss_target[0], full)

    packed = _pack_grads([grads[n] for n, _ in _SHARDED], [ax for _, ax in _SHARDED], WIRE_DTYPE)
    from_sibling = _sibling_exchange(packed)
    mine = lax.dynamic_index_in_dim(packed, core, 0, keepdims=True).reshape((1, -1, LANES))
    chip_sums = _sum_rows("rs_sum_sibling", [(mine, 0), (from_sibling.reshape(mine.shape), 0)], WIRE_DTYPE)
    chip_sums = chip_sums.reshape(from_sibling.shape)
    from_chips = _chip_exchange(chip_sums)
    own = lax.dynamic_index_in_dim(chip_sums, chip, 0, keepdims=True)
    reduced = _sum_rows("rs_sum_chips", [(own, 0), (from_chips, 0), (from_chips, 1), (from_chips, 2)], F32)
    shard = _sibling_share(reduced).reshape(-1)
    g_shard = dict(zip([n for n, _ in _SHARDED], _split_flat(shard, [w[n].shape for n, _ in _SHARDED])))

    small = jnp.concatenate([grads[n].reshape(-1) for n in _REPLICATED] + [loss_row[0, :1]])
    small_sum = _allreduce_small(_pad_rows(small, 1)[0]).reshape(-1)
    parts = _split_flat(small_sum, [w[n].shape for n in _REPLICATED] + [(1,)])
    g_rep = dict(zip(_REPLICATED, parts[:-1]))
    loss = parts[-1][0]

    grad, delta, new_m, new_v = {}, {}, {}, {}
    for n, _ in _SHARDED:
        grad[n] = g_shard[n]
        delta[n], new_m[n], new_v[n] = _adamw("adamw_" + n, w[n], g_shard[n], m[n], v[n])
    pack_rep = lambda t: _pad_rows(jnp.concatenate([t[n].reshape(-1) for n in _REPLICATED]), 1)[0]
    outs = _adamw("adamw_replicated", pack_rep(w), pack_rep(g_rep), pack_rep(m), pack_rep(v))
    shapes = [w[n].shape for n in _REPLICATED]
    for tgt, packed_out in zip((delta, new_m, new_v), outs):
        tgt.update(zip(_REPLICATED, _split_flat(packed_out.reshape(-1), shapes)))
    grad.update(g_rep)
    return (loss, dx[None], *[grad[n] for n in _WEIGHTS], *[delta[n] for n in _WEIGHTS],
            *[new_m[n] for n in _WEIGHTS], *[new_v[n] for n in _WEIGHTS])
```

```python
import functools

import jax
import jax.numpy as jnp
from jax import lax
from jax.experimental import pallas as pl
from jax.experimental.pallas import tpu as pltpu

F32 = jnp.float32
MXU_DTYPE = jnp.bfloat16
SOLVE_PRECISION = lax.Precision.HIGHEST
EPS = 1e-6
CHUNK = 64
XATTN_HEADS = 4
LANES = 128
HALO = 16
VMEM_LIMIT = 48 * 1024 * 1024
ADAM_LR, ADAM_B1, ADAM_B2, ADAM_EPS, ADAM_WD, ADAM_STEP = 0.001, 0.9, 0.999, 1e-08, 0.01, 10
MESH = pl.DeviceIdType.MESH
N_CHIPS = 4
N_DEV = 8

_DIMS = {
    "nn": (((1,), (0,)), ((), ())),
    "nt": (((1,), (1,)), ((), ())),
    "tn": (((0,), (0,)), ((), ())),
}


def _tile(n, pref, align=LANES):
    if n <= pref:
        return n
    t = (pref // align) * align
    while t >= align:
        if n % t == 0:
            return t
        t -= align
    return n


def _params(*sem):
    return pltpu.CompilerParams(dimension_semantics=sem, vmem_limit_bytes=VMEM_LIMIT)


def _dot(a, b, form, hi=False):
    (ca, cb), _ = _DIMS[form]
    dims = (((ca[0] + 1,), (cb[0] + 1,)), ((0,), (0,))) if a.ndim == 3 else _DIMS[form]
    if hi:
        return lax.dot_general(a.astype(F32), b.astype(F32), dims, precision=SOLVE_PRECISION,
                               preferred_element_type=F32)
    return lax.dot_general(a.astype(MXU_DTYPE), b.astype(MXU_DTYPE), dims, preferred_element_type=F32)


@functools.partial(jax.custom_vjp, nondiff_argnums=(2, 3))
def _dot_d(a, b, form, hi):
    return _dot(a, b, form, hi)


def _dot_d_fwd(a, b, form, hi):
    return _dot(a, b, form, hi), (a, b)


def _dot_d_bwd(form, hi, res, g):
    a, b = res
    if form == "nn":
        da, db = _dot_d(g, b, "nt", hi), _dot_d(a, g, "tn", hi)
    elif form == "nt":
        da, db = _dot_d(g, b, "nn", hi), _dot_d(g, a, "tn", hi)
    else:
        da, db = _dot_d(b, g, "nt", hi), _dot_d(a, g, "nn", hi)
    return da.astype(a.dtype), db.astype(b.dtype)


_dot_d.defvjp(_dot_d_fwd, _dot_d_bwd)


def _tri_inv_impl(a, mmh):
    c = a.shape[-1]
    r = lax.broadcasted_iota(jnp.int32, (c, c), 0)
    s = lax.broadcasted_iota(jnp.int32, (c, c), 1)
    eye = (r == s).astype(F32)
    diag_blk = (r // 16) == (s // 16)
    d = jnp.where(diag_blk, a, 0.0)
    low = a - d
    d2 = mmh(d, d)
    d4 = mmh(d2, d2)
    d8 = mmh(d4, d4)
    td = mmh(mmh(mmh(eye - d, eye + d2), eye + d4), eye + d8)
    n = mmh(td, low)
    acc = eye - n
    p = n
    pw = 1
    while 2 * pw < c // 16:
        p = mmh(p, p)
        pw *= 2
        acc = mmh(acc, eye + p)
    return mmh(acc, td)


def _mmh_plain(a, b):
    return _dot(a, b, "nn", True)


@jax.custom_vjp
def _tri_inv_d(a):
    return _tri_inv_impl(a, _mmh_plain)


def _tri_inv_d_fwd(a):
    t = _tri_inv_impl(a, _mmh_plain)
    return t, t


def _tri_inv_d_bwd(t, g):
    return (-_dot(_dot(t, g, "tn", True), t, "nt", True),)


_tri_inv_d.defvjp(_tri_inv_d_fwd, _tri_inv_d_bwd)


class _Ops:
    def __init__(self, diff):
        self.diff = diff

    def mm(self, a, b, form="nn"):
        return _dot_d(a, b, form, False) if self.diff else _dot(a, b, form, False)

    def mmh(self, a, b, form="nn"):
        return _dot_d(a, b, form, True) if self.diff else _dot(a, b, form, True)

    def tri_inv(self, a):
        return _tri_inv_d(a) if self.diff else _tri_inv_impl(a, _mmh_plain)


_PLAIN = _Ops(False)
_DIFF = _Ops(True)


def _sigmoid(x):
    return 1.0 / (1.0 + jnp.exp(-x))


def _silu(x):
    return x * _sigmoid(x)


def _softplus(x):
    return jnp.maximum(x, 0.0) + jnp.log(1.0 + jnp.exp(-jnp.abs(x)))


def _rms(x, g):
    return x * lax.rsqrt(jnp.mean(x * x, axis=-1, keepdims=True) + EPS) * g


def _matmul(name, a, b, form, out_dtype, add=None, tm=1024, tn=512, tk=2048):
    if form == "nn":
        (m, k), (k2, n) = a.shape, b.shape
    elif form == "nt":
        (m, k), (n, k2) = a.shape, b.shape
    else:
        (k, m), (k2, n) = a.shape, b.shape
    assert k == k2, (name, a.shape, b.shape, form)
    tm, tn, tk = _tile(m, tm), _tile(n, tn), _tile(k, tk)
    nk = k // tk
    if form == "nn":
        a_spec = pl.BlockSpec((tm, tk), lambda i, j, kk: (i, kk))
        b_spec = pl.BlockSpec((tk, tn), lambda i, j, kk: (kk, j))
    elif form == "nt":
        a_spec = pl.BlockSpec((tm, tk), lambda i, j, kk: (i, kk))
        b_spec = pl.BlockSpec((tn, tk), lambda i, j, kk: (j, kk))
    else:
        a_spec = pl.BlockSpec((tk, tm), lambda i, j, kk: (kk, i))
        b_spec = pl.BlockSpec((tk, tn), lambda i, j, kk: (kk, j))
    o_spec = pl.BlockSpec((tm, tn), lambda i, j, kk: (i, j))
    has_add = add is not None

    def body(*refs):
        a_ref, b_ref = refs[0], refs[1]
        add_ref = refs[2] if has_add else None
        o_ref = refs[3] if has_add else refs[2]

        def finish(acc):
            if has_add:
                acc = acc + add_ref[...].astype(F32)
            o_ref[...] = acc.astype(o_ref.dtype)

        p = _dot(a_ref[...], b_ref[...], form)
        if nk == 1:
            finish(p)
        else:
            acc_ref = refs[-1]
            kk = pl.program_id(2)

            @pl.when(kk == 0)
            def _():
                acc_ref[...] = p

            @pl.when(kk > 0)
            def _():
                acc_ref[...] += p

            @pl.when(kk == nk - 1)
            def _():
                finish(acc_ref[...])

    return pl.pallas_call(
        body, name=name, grid=(m // tm, n // tn, nk),
        in_specs=[a_spec, b_spec] + ([o_spec] if has_add else []), out_specs=o_spec,
        out_shape=jax.ShapeDtypeStruct((m, n), out_dtype),
        scratch_shapes=[pltpu.VMEM((tm, tn), F32)] if nk > 1 else [],
        compiler_params=_params("parallel", "parallel", "arbitrary"),
    )(*((a, b, add) if has_add else (a, b)))


def _rms_fwd(name, x, g):
    t, d = x.shape
    tm = _tile(t, 512, 16)

    def body(x_ref, g_ref, o_ref):
        o_ref[...] = _rms(x_ref[...], g_ref[...]).astype(o_ref.dtype)

    return pl.pallas_call(
        body, name=name, grid=(t // tm,),
        in_specs=[pl.BlockSpec((tm, d), lambda i: (i, 0)), pl.BlockSpec((1, d), lambda i: (0, 0))],
        out_specs=pl.BlockSpec((tm, d), lambda i: (i, 0)),
        out_shape=jax.ShapeDtypeStruct((t, d), MXU_DTYPE), compiler_params=_params("parallel"),
    )(x, g.reshape(1, d))


def _rms_bwd(name, x, g, dh, dres=None):
    t, d = x.shape
    tm = _tile(t, 256, 16)
    has_res = dres is not None

    def body(*refs):
        x_ref, g_ref, dh_ref = refs[:3]
        dres_ref = refs[3] if has_res else None
        dx_ref, dg_ref = refs[-2:]
        _, vjp = jax.vjp(_rms, x_ref[...], g_ref[...])
        dx, dg = vjp(dh_ref[...].astype(F32))
        if has_res:
            dx = dx + dres_ref[...]
        dx_ref[...] = dx
        first = pl.program_id(0) == 0

        @pl.when(first)
        def _():
            dg_ref[...] = dg

        @pl.when(jnp.logical_not(first))
        def _():
            dg_ref[...] += dg

    row = pl.BlockSpec((tm, d), lambda i: (i, 0))
    vec = pl.BlockSpec((1, d), lambda i: (0, 0))
    dx, dg = pl.pallas_call(
        body, name=name, grid=(t // tm,),
        in_specs=[row, vec, row] + ([row] if has_res else []), out_specs=[row, vec],
        out_shape=[jax.ShapeDtypeStruct((t, d), F32), jax.ShapeDtypeStruct((1, d), F32)],
        compiler_params=_params("arbitrary"),
    )(*((x, g.reshape(1, d), dh) + ((dres,) if has_res else ())))
    return dx, dg.reshape(d)


def _final_loss(x, g, target):
    t, d = x.shape
    tm = _tile(t, 256, 16)

    def body(x_ref, g_ref, t_ref, loss_ref, dx_ref, dg_ref):
        y, vjp = jax.vjp(_rms, x_ref[...], g_ref[...])
        err = y - t_ref[...]
        dx, dg = vjp(err * (1.0 / d))
        dx_ref[...] = dx
        part = jnp.zeros((1, LANES), F32) + 0.5 * jnp.sum(jnp.mean(err * err, axis=-1, keepdims=True))
        first = pl.program_id(0) == 0

        @pl.when(first)
        def _():
            dg_ref[...] = dg
            loss_ref[...] = part

        @pl.when(jnp.logical_not(first))
        def _():
            dg_ref[...] += dg
            loss_ref[...] += part

    row = pl.BlockSpec((tm, d), lambda i: (i, 0))
    vec = pl.BlockSpec((1, d), lambda i: (0, 0))
    loss, dx, dg = pl.pallas_call(
        body, name="final_loss", grid=(t // tm,), in_specs=[row, vec, row],
        out_specs=[pl.BlockSpec((1, LANES), lambda i: (0, 0)), row, vec],
        out_shape=[jax.ShapeDtypeStruct((1, LANES), F32), jax.ShapeDtypeStruct((t, d), F32),
                   jax.ShapeDtypeStruct((1, d), F32)],
        compiler_params=_params("arbitrary"),
    )(x, g.reshape(1, d), target)
    return loss, dx, dg.reshape(d)


def _conv_taps(x_ext, w, rows):
    kk = w.shape[0]
    y = x_ext[HALO:] * w[kk - 1:kk, :]
    for j in range(kk - 1):
        y = y + pltpu.roll(x_ext, kk - 1 - j, axis=0)[HALO:] * w[j:j + 1, :]
    return y


def _col_specs(tm, tn, col0, t_rows):
    assert col0 % tn == 0 and tm % HALO == 0
    c0 = col0 // tn
    per, last = tm // HALO, t_rows // HALO - 1
    tile = pl.BlockSpec((tm, tn), lambda j, i: (i, c0 + j))
    prev = pl.BlockSpec((HALO, tn), lambda j, i: (jnp.maximum(i * per - 1, 0), c0 + j))
    nxt = pl.BlockSpec((HALO, tn), lambda j, i: (jnp.minimum((i + 1) * per, last), c0 + j))
    return tile, prev, nxt


def _conv_fwd(name, xa, xa_col, w, w_col, ncols, out_dtype, xb=None, xb_col=0, gate=None, gate_col=0):
    t = xa.shape[0]
    kk = w.shape[0]
    tm, tn = _tile(t, 512, HALO), _tile(ncols, 512)
    nrow = t // tm
    has_b, has_g = xb is not None, gate is not None

    def body(*refs):
        refs = list(refs)
        xa_ref, xap_ref = refs.pop(0), refs.pop(0)
        xb_ref, xbp_ref = (refs.pop(0), refs.pop(0)) if has_b else (None, None)
        w_ref = refs.pop(0)
        g_ref = refs.pop(0) if has_g else None
        o_ref = refs.pop(0)
        i = pl.program_id(1)
        x, xp = xa_ref[...].astype(F32), xap_ref[...].astype(F32)
        if has_b:
            x, xp = x * xb_ref[...].astype(F32), xp * xbp_ref[...].astype(F32)
        xp = jnp.where(i == 0, 0.0, xp)
        y = _conv_taps(jnp.concatenate([xp, x], axis=0), w_ref[...], tm)
        if has_g:
            y = y * g_ref[...].astype(F32)
        o_ref[...] = y.astype(o_ref.dtype)

    a_tile, a_prev, _ = _col_specs(tm, tn, xa_col, t)
    ins, specs = [xa, xa], [a_tile, a_prev]
    if has_b:
        b_tile, b_prev, _ = _col_specs(tm, tn, xb_col, t)
        ins, specs = ins + [xb, xb], specs + [b_tile, b_prev]
    assert w_col % tn == 0
    ins, specs = ins + [w], specs + [pl.BlockSpec((kk, tn), lambda j, i: (0, w_col // tn + j))]
    if has_g:
        ins, specs = ins + [gate], specs + [_col_specs(tm, tn, gate_col, t)[0]]
    return pl.pallas_call(
        body, name=name, grid=(ncols // tn, nrow), in_specs=specs,
        out_specs=pl.BlockSpec((tm, tn), lambda j, i: (i, j)),
        out_shape=jax.ShapeDtypeStruct((t, ncols), out_dtype), compiler_params=_params("parallel", "parallel"),
    )(*ins)


def _conv_bwd(name, xa, xa_col, w, w_col, dy, dy_col, ncols, dx_dtype, xb=None, xb_col=0, gate=None, gate_col=0):
    t = xa.shape[0]
    kk = w.shape[0]
    tm, tn = _tile(t, 512, HALO), _tile(ncols, 512)
    nrow = t // tm
    has_b, has_g = xb is not None, gate is not None

    def body(*refs):
        refs = list(refs)
        xa_ref, xap_ref = refs.pop(0), refs.pop(0)
        xb_ref, xbp_ref = (refs.pop(0), refs.pop(0)) if has_b else (None, None)
        w_ref = refs.pop(0)
        dy_ref, dyn_ref = refs.pop(0), refs.pop(0)
        g_ref, gn_ref = (refs.pop(0), refs.pop(0)) if has_g else (None, None)
        dxa_ref = refs.pop(0)
        dxb_ref = refs.pop(0) if has_b else None
        dg_ref = refs.pop(0) if has_g else None
        dw_ref = refs.pop(0)
        i = pl.program_id(1)
        wv = w_ref[...]
        xa_t, xa_p = xa_ref[...].astype(F32), xap_ref[...].astype(F32)
        x, xp = xa_t, xa_p
        if has_b:
            xb_t = xb_ref[...].astype(F32)
            x, xp = x * xb_t, xp * xbp_ref[...].astype(F32)
        xp = jnp.where(i == 0, 0.0, xp)
        x_ext = jnp.concatenate([xp, x], axis=0)
        dyv, dyn = dy_ref[...].astype(F32), dyn_ref[...].astype(F32)
        if has_g:
            dg_ref[...] = (dyv * _conv_taps(x_ext, wv, tm)).astype(dg_ref.dtype)
            dyv, dyn = dyv * g_ref[...].astype(F32), dyn * gn_ref[...].astype(F32)
        dyn = jnp.where(i == nrow - 1, 0.0, dyn)
        dy_ext = jnp.concatenate([dyv, dyn], axis=0)
        dx = dyv * wv[kk - 1:kk, :]
        row8 = lax.broadcasted_iota(jnp.int32, (8, tn), 0)
        dw = jnp.where(row8 == kk - 1, jnp.sum(dyv * x, axis=0, keepdims=True), 0.0)
        for j in range(kk - 1):
            s = kk - 1 - j
            dx = dx + pltpu.roll(dy_ext, tm + HALO - s, axis=0)[:tm] * wv[j:j + 1, :]
            dwj = jnp.sum(dyv * pltpu.roll(x_ext, s, axis=0)[HALO:], axis=0, keepdims=True)
            dw = dw + jnp.where(row8 == j, dwj, 0.0)
        if has_b:
            dxa_ref[...] = (dx * xb_t).astype(dxa_ref.dtype)
            dxb_ref[...] = (dx * xa_t).astype(dxb_ref.dtype)
        else:
            dxa_ref[...] = dx.astype(dxa_ref.dtype)

        @pl.when(i == 0)
        def _():
            dw_ref[...] = dw

        @pl.when(i > 0)
        def _():
            dw_ref[...] += dw

    a_tile, a_prev, _ = _col_specs(tm, tn, xa_col, t)
    ins, specs = [xa, xa], [a_tile, a_prev]
    if has_b:
        b_tile, b_prev, _ = _col_specs(tm, tn, xb_col, t)
        ins, specs = ins + [xb, xb], specs + [b_tile, b_prev]
    assert w_col % tn == 0
    ins, specs = ins + [w], specs + [pl.BlockSpec((kk, tn), lambda j, i: (0, w_col // tn + j))]
    d_tile, _, d_next = _col_specs(tm, tn, dy_col, t)
    ins, specs = ins + [dy, dy], specs + [d_tile, d_next]
    if has_g:
        g_tile, _, g_next = _col_specs(tm, tn, gate_col, t)
        ins, specs = ins + [gate, gate], specs + [g_tile, g_next]
    out_tile = pl.BlockSpec((tm, tn), lambda j, i: (i, j))
    shapes, ospecs = [jax.ShapeDtypeStruct((t, ncols), dx_dtype)], [out_tile]
    if has_b:
        shapes, ospecs = shapes + [jax.ShapeDtypeStruct((t, ncols), dx_dtype)], ospecs + [out_tile]
    if has_g:
        shapes, ospecs = shapes + [jax.ShapeDtypeStruct((t, ncols), dx_dtype)], ospecs + [out_tile]
    shapes, ospecs = shapes + [jax.ShapeDtypeStruct((8, ncols), F32)], ospecs + [pl.BlockSpec((8, tn), lambda j, i: (0, j))]
    outs = list(pl.pallas_call(
        body, name=name, grid=(ncols // tn, nrow), in_specs=specs, out_specs=ospecs, out_shape=shapes,
        compiler_params=_params("parallel", "arbitrary"),
    )(*ins))
    dxa = outs.pop(0)
    dxb = outs.pop(0) if has_b else None
    dgate = outs.pop(0) if has_g else None
    return dxa, dxb, dgate, outs.pop(0)[:kk]


def _swiglu_fwd(u):
    t, f2 = u.shape
    f = f2 // 2
    tm, tn = _tile(t, 512, 16), _tile(f, 512)
    nf = f // tn

    def body(g_ref, u_ref, o_ref):
        o_ref[...] = (_silu(g_ref[...].astype(F32)) * u_ref[...].astype(F32)).astype(o_ref.dtype)

    return pl.pallas_call(
        body, name="swiglu_fwd", grid=(t // tm, nf),
        in_specs=[pl.BlockSpec((tm, tn), lambda i, j: (i, j)), pl.BlockSpec((tm, tn), lambda i, j: (i, nf + j))],
        out_specs=pl.BlockSpec((tm, tn), lambda i, j: (i, j)),
        out_shape=jax.ShapeDtypeStruct((t, f), MXU_DTYPE), compiler_params=_params("parallel", "parallel"),
    )(u, u)


def _swiglu_bwd(u, da):
    t, f2 = u.shape
    f = f2 // 2
    tm, tn = _tile(t, 512, 16), _tile(f, 512)
    nf = f // tn

    def body(g_ref, u_ref, da_ref, o_ref):
        g, up, d = g_ref[...].astype(F32), u_ref[...].astype(F32), da_ref[...].astype(F32)
        sg = _sigmoid(g)
        dgate = d * up * (sg * (1.0 + g * (1.0 - sg)))
        dup = d * (g * sg)
        o_ref[...] = jnp.where(pl.program_id(1) < nf, dgate, dup).astype(o_ref.dtype)

    return pl.pallas_call(
        body, name="swiglu_bwd", grid=(t // tm, 2 * nf),
        in_specs=[pl.BlockSpec((tm, tn), lambda i, j: (i, j % nf)),
                  pl.BlockSpec((tm, tn), lambda i, j: (i, nf + j % nf)),
                  pl.BlockSpec((tm, tn), lambda i, j: (i, j % nf))],
        out_specs=pl.BlockSpec((tm, tn), lambda i, j: (i, j)),
        out_shape=jax.ShapeDtypeStruct((t, f2), MXU_DTYPE), compiler_params=_params("parallel", "parallel"),
    )(u, u, da)


def _gdn_prep(ops, qc, kc, vc, b_col, a_col, a_log, dt_bias):
    c, dh = qc.shape[-2:]
    q, k, v = _silu(qc), _silu(kc), _silu(vc)
    q = q * lax.rsqrt(jnp.sum(q * q, axis=-1, keepdims=True) + EPS) * (dh ** -0.5)
    k = k * lax.rsqrt(jnp.sum(k * k, axis=-1, keepdims=True) + EPS)
    beta = _sigmoid(b_col)
    g_col = -jnp.exp(a_log) * _softplus(a_col + dt_bias)
    r = lax.broadcasted_iota(jnp.int32, (c, c), 0)
    s = lax.broadcasted_iota(jnp.int32, (c, c), 1)
    g_row = jnp.sum(jnp.where(r == s, g_col, 0.0), axis=-2, keepdims=True)
    gc_col = jnp.sum(jnp.where(s <= r, g_row, 0.0), axis=-1, keepdims=True)
    gc_row = jnp.sum(jnp.where(r <= s, g_col, 0.0), axis=-2, keepdims=True)
    decay = jnp.exp(jnp.where(s <= r, gc_col - gc_row, -1e30))
    kb = k * beta
    a = jnp.where(s < r, ops.mm(kb, k, "nt") * decay, 0.0)
    tinv = ops.tri_inv(a)
    e_col = jnp.exp(gc_col)
    u = ops.mmh(tinv, v * beta)
    w = ops.mmh(tinv, kb * e_col)
    attn = ops.mm(q, k, "nt") * decay
    g_last = jnp.sum(g_col, axis=-2, keepdims=True)
    return u, w, attn, q * e_col, k * jnp.exp(g_last - gc_col), g_last


def _gdn_step(ops, state, u, w, attn, q_dec, k_dec, g_last):
    v_new = u - ops.mm(w, state)
    o = ops.mm(q_dec, state) + ops.mm(attn, v_new)
    return o, state * jnp.exp(g_last) + ops.mm(k_dec, v_new, "tn")


def _gdn_blocks(t, heads):
    tc = _tile(t, 256, CHUNK)
    hb = 2 if heads % 2 == 0 else 1
    return tc, hb


def _to_chunks(ref, hb, dh):
    tc = ref.shape[0]
    return jnp.concatenate([ref[:, h * dh:(h + 1) * dh].astype(F32).reshape(tc // CHUNK, CHUNK, dh)
                            for h in range(hb)], axis=0)


def _from_chunks(ref, val, hb, dh):
    tc = ref.shape[0]
    ncb = tc // CHUNK
    for h in range(hb):
        ref[:, h * dh:(h + 1) * dh] = val[h * ncb:(h + 1) * ncb].reshape(tc, dh).astype(ref.dtype)


def _per_chunk(s, ncb):
    hb = s.shape[0]
    return jnp.broadcast_to(s[:, None], (hb, ncb, 1, 1)).reshape(hb * ncb, 1, 1)


def _gdn_prep_fwd(qkv, b_col, a_col, a_log, dt_bias, heads, dh):
    t = qkv.shape[0]
    tc, hb = _gdn_blocks(t, heads)
    ncb, nhb, width = tc // CHUNK, heads // hb, heads * dh
    nc = t // CHUNK

    def body(q_ref, k_ref, v_ref, b_ref, a_ref, al_ref, dt_ref, u_ref, w_ref, p_ref, qd_ref, kd_ref, gl_ref):
        u, w, p, qd, kd, gl = _gdn_prep(
            _PLAIN, _to_chunks(q_ref, hb, dh), _to_chunks(k_ref, hb, dh), _to_chunks(v_ref, hb, dh),
            b_ref[...].reshape(hb * ncb, CHUNK, 1), a_ref[...].reshape(hb * ncb, CHUNK, 1),
            _per_chunk(al_ref[...], ncb), _per_chunk(dt_ref[...], ncb))
        _from_chunks(u_ref, u, hb, dh)
        _from_chunks(w_ref, w, hb, dh)
        _from_chunks(qd_ref, qd, hb, dh)
        _from_chunks(kd_ref, kd, hb, dh)
        p_ref[...] = p.reshape(hb, tc, CHUNK).astype(p_ref.dtype)
        gl_ref[...] = gl.reshape(hb, ncb, 1, 1)

    def tok(off):
        return pl.BlockSpec((tc, hb * dh), lambda i, j: (i, off * nhb + j))

    gate = pl.BlockSpec((hb, ncb, CHUNK, 1), lambda i, j: (j, i, 0, 0))
    scal = pl.BlockSpec((hb, 1, 1), lambda i, j: (j, 0, 0))
    return pl.pallas_call(
        body, name="gdn_prep_fwd", grid=(t // tc, nhb),
        in_specs=[tok(0), tok(1), tok(2), gate, gate, scal, scal],
        out_specs=[tok(0), tok(0), pl.BlockSpec((hb, tc, CHUNK), lambda i, j: (j, i, 0)), tok(0), tok(0),
                   pl.BlockSpec((hb, ncb, 1, 1), lambda i, j: (j, i, 0, 0))],
        out_shape=[jax.ShapeDtypeStruct((t, width), F32), jax.ShapeDtypeStruct((t, width), MXU_DTYPE),
                   jax.ShapeDtypeStruct((heads, t, CHUNK), MXU_DTYPE), jax.ShapeDtypeStruct((t, width), MXU_DTYPE),
                   jax.ShapeDtypeStruct((t, width), MXU_DTYPE), jax.ShapeDtypeStruct((heads, nc, 1, 1), F32)],
        compiler_params=_params("parallel", "parallel"),
    )(qkv, qkv, qkv, b_col, a_col, a_log, dt_bias)


def _gdn_prep_bwd(qkv, b_col, a_col, a_log, dt_bias, du, dw, dp, dqd, dkd, dgl, heads, dh):
    t = qkv.shape[0]
    tc, hb = _gdn_blocks(t, heads)
    ncb, nhb, width = tc // CHUNK, heads // hb, heads * dh
    nc = t // CHUNK

    def body(q_ref, k_ref, v_ref, b_ref, a_ref, al_ref, dt_ref, du_ref, dw_ref, dp_ref, dqd_ref, dkd_ref, dgl_ref,
             dq_ref, dk_ref, dv_ref, db_ref, da_ref, dal_ref, ddt_ref):
        first = pl.program_id(1) == 0

        def prep(q, k, v, b, a, al, dt):
            return _gdn_prep(_DIFF, q, k, v, b, a, _per_chunk(al, ncb), _per_chunk(dt, ncb))

        _, vjp = jax.vjp(prep, _to_chunks(q_ref, hb, dh), _to_chunks(k_ref, hb, dh), _to_chunks(v_ref, hb, dh),
                         b_ref[...].reshape(hb * ncb, CHUNK, 1), a_ref[...].reshape(hb * ncb, CHUNK, 1),
                         al_ref[...], dt_ref[...])
        dq, dk, dv, db, da, dal, ddt = vjp((
            _to_chunks(du_ref, hb, dh), _to_chunks(dw_ref, hb, dh), dp_ref[...].reshape(hb * ncb, CHUNK, CHUNK),
            _to_chunks(dqd_ref, hb, dh), _to_chunks(dkd_ref, hb, dh), dgl_ref[...].reshape(hb * ncb, 1, 1)))
        _from_chunks(dq_ref, dq, hb, dh)
        _from_chunks(dk_ref, dk, hb, dh)
        _from_chunks(dv_ref, dv, hb, dh)
        db_ref[...] = db.reshape(hb, ncb, CHUNK, 1)
        da_ref[...] = da.reshape(hb, ncb, CHUNK, 1)

        @pl.when(first)
        def _():
            dal_ref[...] = dal
            ddt_ref[...] = ddt

        @pl.when(jnp.logical_not(first))
        def _():
            dal_ref[...] += dal
            ddt_ref[...] += ddt

    def tok(off):
        return pl.BlockSpec((tc, hb * dh), lambda j, i: (i, off * nhb + j))

    gate = pl.BlockSpec((hb, ncb, CHUNK, 1), lambda j, i: (j, i, 0, 0))
    scal = pl.BlockSpec((hb, 1, 1), lambda j, i: (j, 0, 0))
    pspec = pl.BlockSpec((hb, tc, CHUNK), lambda j, i: (j, i, 0))
    glspec = pl.BlockSpec((hb, ncb, 1, 1), lambda j, i: (j, i, 0, 0))
    tokf = jax.ShapeDtypeStruct((t, width), F32)
    gatef = jax.ShapeDtypeStruct((heads, nc, CHUNK, 1), F32)
    scalf = jax.ShapeDtypeStruct((heads, 1, 1), F32)
    return pl.pallas_call(
        body, name="gdn_prep_bwd", grid=(nhb, t // tc),
        in_specs=[tok(0), tok(1), tok(2), gate, gate, scal, scal, tok(0), tok(0), pspec, tok(0), tok(0), glspec],
        out_specs=[tok(0), tok(0), tok(0), gate, gate, scal, scal],
        out_shape=[tokf, tokf, tokf, gatef, gatef, scalf, scalf],
        compiler_params=_params("arbitrary", "arbitrary"),
    )(qkv, qkv, qkv, b_col, a_col, a_log, dt_bias, du, dw, dp, dqd, dkd, dgl)


def _gdn_scan_fwd(u, w, p, qd, kd, gl, heads, dh):
    t = u.shape[0]
    tc, hb = _gdn_blocks(t, heads)
    ncb, nhb = tc // CHUNK, heads // hb
    nc = t // CHUNK

    def body(u_ref, w_ref, p_ref, qd_ref, kd_ref, gl_ref, o_ref, s_ref, state):
        @pl.when(pl.program_id(1) == 0)
        def _():
            state[...] = jnp.zeros_like(state)

        for c in range(ncb):
            rs = slice(c * CHUNK, (c + 1) * CHUNK)
            for h in range(hb):
                ls = slice(h * dh, (h + 1) * dh)
                s_in = state[h]
                s_ref[h, c] = s_in
                o, s_out = _gdn_step(_PLAIN, s_in, u_ref[rs, ls], w_ref[rs, ls], p_ref[h, rs, :], qd_ref[rs, ls],
                                     kd_ref[rs, ls], gl_ref[h, c])
                o_ref[rs, ls] = o
                state[h] = s_out

    tok = pl.BlockSpec((tc, hb * dh), lambda j, i: (i, j))
    pspec = pl.BlockSpec((hb, tc, CHUNK), lambda j, i: (j, i, 0))
    glspec = pl.BlockSpec((hb, ncb, 1, 1), lambda j, i: (j, i, 0, 0))
    return pl.pallas_call(
        body, name="gdn_scan_fwd", grid=(nhb, t // tc),
        in_specs=[tok, tok, pspec, tok, tok, glspec],
        out_specs=[tok, pl.BlockSpec((hb, ncb, dh, dh), lambda j, i: (j, i, 0, 0))],
        out_shape=[jax.ShapeDtypeStruct((t, heads * dh), F32), jax.ShapeDtypeStruct((heads, nc, dh, dh), F32)],
        scratch_shapes=[pltpu.VMEM((hb, dh, dh), F32)],
        compiler_params=_params("arbitrary", "arbitrary"),
    )(u, w, p, qd, kd, gl)


def _gdn_scan_bwd(u, w, p, qd, kd, gl, states, do, heads, dh):
    t = u.shape[0]
    tc, hb = _gdn_blocks(t, heads)
    ncb, nhb = tc // CHUNK, heads // hb
    nc, nt = t // CHUNK, t // tc

    def body(u_ref, w_ref, p_ref, qd_ref, kd_ref, gl_ref, s_ref, do_ref,
             du_ref, dw_ref, dp_ref, dqd_ref, dkd_ref, dgl_ref, dstate):
        @pl.when(pl.program_id(1) == 0)
        def _():
            dstate[...] = jnp.zeros_like(dstate)

        for c in reversed(range(ncb)):
            rs = slice(c * CHUNK, (c + 1) * CHUNK)
            for h in range(hb):
                ls = slice(h * dh, (h + 1) * dh)
                _, vjp = jax.vjp(functools.partial(_gdn_step, _DIFF), s_ref[h, c], u_ref[rs, ls],
                                 w_ref[rs, ls].astype(F32), p_ref[h, rs, :].astype(F32), qd_ref[rs, ls].astype(F32),
                                 kd_ref[rs, ls].astype(F32), gl_ref[h, c])
                ds, du, dw, dp, dqd, dkd, dgl = vjp((do_ref[rs, ls], dstate[h]))
                dstate[h] = ds
                du_ref[rs, ls] = du
                dw_ref[rs, ls] = dw
                dqd_ref[rs, ls] = dqd
                dkd_ref[rs, ls] = dkd
                dp_ref[h, rs, :] = dp
                dgl_ref[h, c] = dgl

    tok = pl.BlockSpec((tc, hb * dh), lambda j, i: (nt - 1 - i, j))
    pspec = pl.BlockSpec((hb, tc, CHUNK), lambda j, i: (j, nt - 1 - i, 0))
    glspec = pl.BlockSpec((hb, ncb, 1, 1), lambda j, i: (j, nt - 1 - i, 0, 0))
    sspec = pl.BlockSpec((hb, ncb, dh, dh), lambda j, i: (j, nt - 1 - i, 0, 0))
    tokf = jax.ShapeDtypeStruct((t, heads * dh), F32)
    return pl.pallas_call(
        body, name="gdn_scan_bwd", grid=(nhb, nt),
        in_specs=[tok, tok, pspec, tok, tok, glspec, sspec, tok],
        out_specs=[tok, tok, pspec, tok, tok, glspec],
        out_shape=[tokf, tokf, jax.ShapeDtypeStruct((heads, t, CHUNK), F32), tokf, tokf,
                   jax.ShapeDtypeStruct((heads, nc, 1, 1), F32)],
        scratch_shapes=[pltpu.VMEM((hb, dh, dh), F32)],
        compiler_params=_params("arbitrary", "arbitrary"),
    )(u, w, p, qd, kd, gl, states, do)


def _gdn_post(o, z, gain):
    return _rms(o, gain) * _silu(z)


def _gdn_post_fwd(o, pm, z_col, gain, heads, dh):
    t = o.shape[0]
    tm = _tile(t, 512, 16)
    z0 = z_col // dh

    def body(o_ref, z_ref, g_ref, y_ref):
        y_ref[...] = _gdn_post(o_ref[...], z_ref[...], g_ref[...]).astype(y_ref.dtype)

    return pl.pallas_call(
        body, name="gdn_post_fwd", grid=(t // tm, heads),
        in_specs=[pl.BlockSpec((tm, dh), lambda i, h: (i, h)), pl.BlockSpec((tm, dh), lambda i, h: (i, z0 + h)),
                  pl.BlockSpec((1, dh), lambda i, h: (0, 0))],
        out_specs=pl.BlockSpec((tm, dh), lambda i, h: (i, h)),
        out_shape=jax.ShapeDtypeStruct((t, heads * dh), MXU_DTYPE), compiler_params=_params("parallel", "parallel"),
    )(o, pm, gain.reshape(1, dh))


def _gdn_post_bwd(o, pm, z_col, gain, dy, heads, dh):
    t = o.shape[0]
    tm = _tile(t, 512, 16)
    z0 = z_col // dh

    def body(o_ref, z_ref, g_ref, dy_ref, do_ref, dz_ref, dg_ref):
        _, vjp = jax.vjp(_gdn_post, o_ref[...], z_ref[...], g_ref[...])
        do, dz, dg = vjp(dy_ref[...])
        do_ref[...] = do
        dz_ref[...] = dz.astype(dz_ref.dtype)
        first = jnp.logical_and(pl.program_id(0) == 0, pl.program_id(1) == 0)

        @pl.when(first)
        def _():
            dg_ref[...] = dg

        @pl.when(jnp.logical_not(first))
        def _():
            dg_ref[...] += dg

    blk = pl.BlockSpec((tm, dh), lambda i, h: (i, h))
    vec = pl.BlockSpec((1, dh), lambda i, h: (0, 0))
    do, dz, dg = pl.pallas_call(
        body, name="gdn_post_bwd", grid=(t // tm, heads),
        in_specs=[blk, pl.BlockSpec((tm, dh), lambda i, h: (i, z0 + h)), vec, blk],
        out_specs=[blk, blk, vec],
        out_shape=[jax.ShapeDtypeStruct((t, heads * dh), F32), jax.ShapeDtypeStruct((t, heads * dh), MXU_DTYPE),
                   jax.ShapeDtypeStruct((1, dh), F32)],
        compiler_params=_params("arbitrary", "arbitrary"),
    )(o, pm, gain.reshape(1, dh), dy)
    return do, dz, dg.reshape(dh)


def _attn(ops, q, kv):
    d = q.shape[1]
    hd = d // XATTN_HEADS
    outs = []
    for h in range(XATTN_HEADS):
        qh, kh, vh = q[:, h * hd:(h + 1) * hd], kv[:, h * hd:(h + 1) * hd], kv[:, d + h * hd:d + (h + 1) * hd]
        s = ops.mm(qh, kh, "nt") * (hd ** -0.5)
        e = jnp.exp(s - lax.stop_gradient(jnp.max(s, axis=-1, keepdims=True)))
        outs.append(ops.mm(e / jnp.sum(e, axis=-1, keepdims=True), vh))
    return jnp.concatenate(outs, axis=1)


def _attn_fwd(q, kv):
    t, d = q.shape
    nm = kv.shape[0]
    tm = _tile(t, 512, 16)

    def body(q_ref, kv_ref, o_ref):
        o_ref[...] = _attn(_PLAIN, q_ref[...], kv_ref[...]).astype(o_ref.dtype)

    return pl.pallas_call(
        body, name="xattn_fwd", grid=(t // tm,),
        in_specs=[pl.BlockSpec((tm, d), lambda i: (i, 0)), pl.BlockSpec((nm, 2 * d), lambda i: (0, 0))],
        out_specs=pl.BlockSpec((tm, d), lambda i: (i, 0)),
        out_shape=jax.ShapeDtypeStruct((t, d), MXU_DTYPE), compiler_params=_params("parallel"),
    )(q, kv)


def _attn_bwd(q, kv, do):
    t, d = q.shape
    nm = kv.shape[0]
    tm = _tile(t, 256, 16)

    def body(q_ref, kv_ref, do_ref, dq_ref, dkv_ref):
        _, vjp = jax.vjp(functools.partial(_attn, _DIFF), q_ref[...].astype(F32), kv_ref[...].astype(F32))
        dq, dkv = vjp(do_ref[...].astype(F32))
        dq_ref[...] = dq.astype(dq_ref.dtype)
        first = pl.program_id(0) == 0

        @pl.when(first)
        def _():
            dkv_ref[...] = dkv

        @pl.when(jnp.logical_not(first))
        def _():
            dkv_ref[...] += dkv

    row = pl.BlockSpec((tm, d), lambda i: (i, 0))
    full = pl.BlockSpec((nm, 2 * d), lambda i: (0, 0))
    return pl.pallas_call(
        body, name="xattn_bwd", grid=(t // tm,), in_specs=[row, full, row], out_specs=[row, full],
        out_shape=[jax.ShapeDtypeStruct((t, d), MXU_DTYPE), jax.ShapeDtypeStruct((nm, 2 * d), F32)],
        compiler_params=_params("arbitrary"),
    )(q, kv, do)


def _adamw(name, w, g, m, v):
    shape = w.shape
    cols = shape[-1]
    rows = w.size // cols
    w2, g2, m2, v2 = (a.reshape(rows, cols) for a in (w, g, m, v))
    tr = _tile(rows, max(8, (1 << 18) // cols // 8 * 8), 8)

    def body(w_ref, g_ref, m_ref, v_ref, d_ref, nm_ref, nv_ref):
        gv = g_ref[...]
        nm = ADAM_B1 * m_ref[...] + (1.0 - ADAM_B1) * gv
        nv = ADAM_B2 * v_ref[...] + (1.0 - ADAM_B2) * jnp.square(gv)
        m_hat = nm / (1.0 - ADAM_B1 ** ADAM_STEP)
        v_hat = nv / (1.0 - ADAM_B2 ** ADAM_STEP)
        d_ref[...] = -ADAM_LR * (m_hat / (jnp.sqrt(v_hat) + ADAM_EPS) + ADAM_WD * w_ref[...])
        nm_ref[...] = nm
        nv_ref[...] = nv

    blk = pl.BlockSpec((tr, cols), lambda i: (i, 0))
    out = jax.ShapeDtypeStruct((rows, cols), F32)
    d, nm, nv = pl.pallas_call(
        body, name=name, grid=(rows // tr,), in_specs=[blk] * 4, out_specs=[blk] * 3, out_shape=[out] * 3,
        compiler_params=_params("parallel"),
    )(w2, g2, m2, v2)
    return d.reshape(shape), nm.reshape(shape), nv.reshape(shape)


def _gate_cols(pba, heads):
    t = pba.shape[0]
    to_col = lambda a: a.T.reshape(heads, t // CHUNK, CHUNK, 1)
    return to_col(pba[:, :heads]), to_col(pba[:, heads:2 * heads])


def _layer_fwd(x, mem, p, heads, dh):
    wid = heads * dh
    sc = x.shape[1] - wid
    s = {"x0": x}
    s["h1"] = _rms_fwd("rms_mix", x, p["mix_norm"])
    s["pm"] = pm = _matmul("mm_mix_in", s["h1"], p["wmain"], "nn", F32)
    pba = _matmul("mm_mix_ba", s["h1"], p["wba"], "nn", F32)
    s["qkv"] = _conv_fwd("conv_gdn", pm, 0, p["gdn_conv"], 0, 3 * wid, F32)
    s["b_col"], s["a_col"] = _gate_cols(pba, heads)
    s["prep"] = _gdn_prep_fwd(s["qkv"], s["b_col"], s["a_col"], p["a_log"], p["dt_bias"], heads, dh)
    s["o"], s["states"] = _gdn_scan_fwd(*s["prep"], heads, dh)
    y_gdn = _gdn_post_fwd(s["o"], pm, 3 * wid, p["gdn_out_norm"], heads, dh)
    y_sc = _conv_fwd("conv_sc", pm, 4 * wid + sc, p["sc_conv"], 0, sc, MXU_DTYPE, xb=pm, xb_col=4 * wid + 2 * sc,
                     gate=pm, gate_col=4 * wid)
    s["ycat"] = jnp.concatenate([y_gdn, y_sc], axis=1)
    s["x1"] = x1 = _matmul("mm_mix_out", s["ycat"], p["wout"], "nn", F32, add=x)
    s["h2"] = _rms_fwd("rms_xattn", x1, p["xattn_norm"])
    s["q"] = _matmul("mm_xq", s["h2"], p["wq"], "nn", MXU_DTYPE)
    s["memn"] = _rms_fwd("rms_mem", mem, p["mem_norm"])
    s["kv"] = _matmul("mm_xkv", s["memn"], p["wkv"], "nn", MXU_DTYPE)
    s["ao"] = _attn_fwd(s["q"], s["kv"])
    s["x2"] = x2 = _matmul("mm_xo", s["ao"], p["wo"], "nn", F32, add=x1)
    s["h3"] = _rms_fwd("rms_ffn", x2, p["ffn_norm"])
    s["upre"] = _matmul("mm_ffn_up", s["h3"], p["wup"], "nn", MXU_DTYPE)
    s["uc"] = _conv_fwd("conv_ffn", s["upre"], 0, p["ffn_conv"], 0, s["upre"].shape[1], MXU_DTYPE)
    s["act"] = _swiglu_fwd(s["uc"])
    return _matmul("mm_ffn_down", s["act"], p["wdown"], "nn", F32, add=x2), s


def _layer_bwd(dx3, mem, s, p, heads, dh):
    wid = heads * dh
    sc = dx3.shape[1] - wid
    t = dx3.shape[0]
    pm = s["pm"]
    g = {}
    da = _matmul("mm_ffn_down_dx", dx3, p["wdown"], "nt", MXU_DTYPE)
    g["wdown"] = _matmul("mm_ffn_down_dw", s["act"], dx3, "tn", F32)
    du = _swiglu_bwd(s["uc"], da)
    dupre, _, _, g["ffn_conv"] = _conv_bwd("conv_ffn_bwd", s["upre"], 0, p["ffn_conv"], 0, du, 0, du.shape[1],
                                           MXU_DTYPE)
    dh3 = _matmul("mm_ffn_up_dx", dupre, p["wup"], "nt", F32)
    g["wup"] = _matmul("mm_ffn_up_dw", s["h3"], dupre, "tn", F32)
    dx2, g["ffn_norm"] = _rms_bwd("rms_ffn_bwd", s["x2"], p["ffn_norm"], dh3, dx3)
    dao = _matmul("mm_xo_dx", dx2, p["wo"], "nt", MXU_DTYPE)
    g["wo"] = _matmul("mm_xo_dw", s["ao"], dx2, "tn", F32)
    dq, dkv = _attn_bwd(s["q"], s["kv"], dao)
    dh2 = _matmul("mm_xq_dx", dq, p["wq"], "nt", F32)
    g["wq"] = _matmul("mm_xq_dw", s["h2"], dq, "tn", F32)
    dmemn = _matmul("mm_xkv_dx", dkv, p["wkv"], "nt", F32)
    g["wkv"] = _matmul("mm_xkv_dw", s["memn"], dkv, "tn", F32)
    _, g["mem_norm"] = _rms_bwd("rms_mem_bwd", mem, p["mem_norm"], dmemn)
    dx1, g["xattn_norm"] = _rms_bwd("rms_xattn_bwd", s["x1"], p["xattn_norm"], dh2, dx2)
    dycat = _matmul("mm_mix_out_dx", dx1, p["wout"], "nt", F32)
    g["wout"] = _matmul("mm_mix_out_dw", s["ycat"], dx1, "tn", F32)
    d_c, d_h, d_b, g["sc_conv"] = _conv_bwd("conv_sc_bwd", pm, 4 * wid + sc, p["sc_conv"], 0, dycat, wid, sc,
                                             MXU_DTYPE, xb=pm, xb_col=4 * wid + 2 * sc, gate=pm, gate_col=4 * wid)
    do, dz, g["gdn_out_norm"] = _gdn_post_bwd(s["o"], pm, 3 * wid, p["gdn_out_norm"], dycat, heads, dh)
    dprep = _gdn_scan_bwd(*s["prep"], s["states"], do, heads, dh)
    dqc, dkc, dvc, db_col, da_col, g["a_log"], g["dt_bias"] = _gdn_prep_bwd(
        s["qkv"], s["b_col"], s["a_col"], p["a_log"], p["dt_bias"], *dprep, heads, dh)
    dqkv, _, _, g["gdn_conv"] = _conv_bwd("conv_gdn_bwd", pm, 0, p["gdn_conv"], 0,
                                          jnp.concatenate([dqc, dkc, dvc], axis=1), 0, 3 * wid, MXU_DTYPE)
    dpm = jnp.concatenate([dqkv, dz, d_b, d_c, d_h], axis=1)
    from_col = lambda a: a.reshape(heads, t).T
    dpba = jnp.concatenate([from_col(db_col), from_col(da_col), jnp.zeros((t, LANES - 2 * heads), F32)],
                           axis=1).astype(MXU_DTYPE)
    dh1 = _matmul("mm_mix_in_dx", dpm, p["wmain"], "nt", F32)
    dh1 = _matmul("mm_mix_ba_dx", dpba, p["wba"], "nt", F32, add=dh1)
    g["wmain"] = _matmul("mm_mix_in_dw", s["h1"], dpm, "tn", F32)
    g["wba"] = _matmul("mm_mix_ba_dw", s["h1"], dpba, "tn", F32)
    dx0, g["mix_norm"] = _rms_bwd("rms_mix_bwd", s["x0"], p["mix_norm"], dh1, dx1)
    return dx0, g


def _layer_params(win, wout, wq, wk, wv, wo, wup, wdown, small, heads, dh):
    wid = heads * dh
    p = dict(small)
    p.update({
        "a_log": small["a_log"].reshape(heads, 1, 1), "dt_bias": small["dt_bias"].reshape(heads, 1, 1),
        "wmain": jnp.concatenate([win[:, :4 * wid], win[:, 4 * wid + 2 * heads:]], axis=1),
        "wba": jnp.pad(win[:, 4 * wid:4 * wid + 2 * heads], ((0, 0), (0, LANES - 2 * heads))),
        "wout": wout, "wq": wq, "wkv": jnp.concatenate([wk, wv], axis=1), "wo": wo, "wup": wup, "wdown": wdown,
    })
    return p


def _local_step(x, mem, target, layers, final_norm, heads, dh):
    saved = []
    for p in layers:
        x, s = _layer_fwd(x, mem, p, heads, dh)
        saved.append(s)
    loss, dx, g_final = _final_loss(x, final_norm, target)
    per_layer = []
    for p, s in zip(reversed(layers), reversed(saved)):
        dx, g = _layer_bwd(dx, mem, s, p, heads, dh)
        per_layer.append(g)
    per_layer.reverse()
    return loss, dx, per_layer, g_final


WIRE_DTYPE = jnp.bfloat16
_ANY = pl.BlockSpec(memory_space=pl.ANY)
_VMEM = pl.BlockSpec(memory_space=pltpu.VMEM)


def _mesh_pos():
    return lax.axis_index("x"), lax.axis_index("y"), lax.axis_index("c")


def _other_chips(x, y):
    return [(1 - x, y), (x, 1 - y), (1 - x, 1 - y)]


def _push(src, dst, sems, k, to):
    return pltpu.make_async_remote_copy(src_ref=src, dst_ref=dst, send_sem=sems[0].at[k], recv_sem=sems[1].at[k],
                                        device_id=to, device_id_type=MESH)


def _sem_pairs(n):
    return [pltpu.SemaphoreType.DMA((n,)), pltpu.SemaphoreType.DMA((n,))]


def _allgather_chips(srcs):
    n = len(srcs)

    def body(*refs):
        src, out, sems = refs[:n], refs[n:2 * n], refs[2 * n:]
        x, y, c = _mesh_pos()
        me, sibling, chips = 2 * x + y, (x, y, 1 - c), _other_chips(x, y)
        started, halves = [], []
        for t in range(n):
            half = src[t].shape[0] // 2
            mine, other = pl.ds(c * half, half), pl.ds((1 - c) * half, half)
            halves.append((mine, other))
            started.append(_push(src[t], out[t].at[me], sems, 7 * t + 6, sibling))
            started += [_push(src[t].at[mine], out[t].at[me, mine], sems, 7 * t + k, (cx, cy, c))
                        for k, (cx, cy) in enumerate(chips)]
        for cp in started:
            cp.start()
        for t in range(n):
            for k, (cx, cy) in enumerate(chips):
                landed = out[t].at[2 * cx + cy, halves[t][0]]
                _push(landed, landed, sems, 7 * t + k, sibling).wait_recv()
                fwd = _push(landed, landed, sems, 7 * t + 3 + k, sibling)
                fwd.start()
                started.append(fwd)
        for t in range(n):
            for k, (cx, cy) in enumerate(chips):
                landed = out[t].at[2 * cx + cy, halves[t][1]]
                _push(landed, landed, sems, 7 * t + 3 + k, sibling).wait_recv()
            _push(src[t], out[t].at[me], sems, 7 * t + 6, sibling).wait_recv()
        for cp in started:
            cp.wait_send()

    return pl.pallas_call(
        body, name="allgather_weights", in_specs=[_ANY] * n, out_specs=[_ANY] * n,
        out_shape=[jax.ShapeDtypeStruct((N_CHIPS,) + s.shape, s.dtype) for s in srcs], scratch_shapes=_sem_pairs(7 * n),
    )(*srcs)


def _sibling_exchange(bufs):
    n = len(bufs)

    def body(*refs):
        src, out, sems = refs[:n], refs[n:2 * n], refs[2 * n:]
        x, y, c = _mesh_pos()
        copies = [_push(src[t].at[1 - c], out[t], sems, t, (x, y, 1 - c)) for t in range(n)]
        for cp in copies:
            cp.start()
        for cp in copies:
            cp.wait()

    return pl.pallas_call(
        body, name="rs_sibling_exchange", in_specs=[_ANY] * n, out_specs=[_ANY] * n,
        out_shape=[jax.ShapeDtypeStruct(b.shape[1:], b.dtype) for b in bufs], scratch_shapes=_sem_pairs(n),
    )(*bufs)


def _chip_exchange(bufs):
    n = len(bufs)

    def body(*refs):
        src, out, sems = refs[:n], refs[n:2 * n], refs[2 * n:]
        x, y, c = _mesh_pos()
        copies = [_push(src[t].at[2 * cx + cy], out[t].at[k], sems, 3 * t + k, (cx, cy, c))
                  for t in range(n) for k, (cx, cy) in enumerate(_other_chips(x, y))]
        for cp in copies:
            cp.start()
        for cp in copies:
            cp.wait()

    return pl.pallas_call(
        body, name="rs_chip_exchange", in_specs=[_ANY] * n, out_specs=[_ANY] * n,
        out_shape=[jax.ShapeDtypeStruct((3,) + b.shape[1:], b.dtype) for b in bufs], scratch_shapes=_sem_pairs(3 * n),
    )(*bufs)


def _sibling_share(bufs):
    n = len(bufs)

    def body(*refs):
        src, out, sems = refs[:n], refs[n:2 * n], refs[2 * n:]
        x, y, c = _mesh_pos()
        copies = [_push(src[t].at[c], out[t].at[c], sems, t, (x, y, 1 - c)) for t in range(n)]
        for cp in copies:
            cp.start()
        for cp in copies:
            cp.wait()

    return pl.pallas_call(
        body, name="rs_sibling_share", in_specs=[_ANY] * n, out_specs=[_ANY] * n,
        out_shape=[jax.ShapeDtypeStruct(b.shape, b.dtype) for b in bufs], scratch_shapes=_sem_pairs(n),
        input_output_aliases={t: t for t in range(n)},
    )(*bufs)


def _allreduce_small(v):
    r, lanes = v.shape

    def body(v_ref, sum_ref, gath, send_sems, recv_sems):
        x, y, c = _mesh_pos()
        me = 4 * x + 2 * y + c
        gath[me] = v_ref[...]
        copies = []
        for rel in range(1, N_DEV):
            peer = tuple(1 - p if (rel >> b) & 1 else p for p, b in ((x, 2), (y, 1), (c, 0)))
            copies.append(pltpu.make_async_remote_copy(
                src_ref=v_ref, dst_ref=gath.at[me], send_sem=send_sems.at[rel - 1], recv_sem=recv_sems.at[rel - 1],
                device_id=peer, device_id_type=MESH))
        for cp in copies:
            cp.start()
        for cp in copies:
            cp.wait()
        total = gath[0]
        for k in range(1, N_DEV):
            total = total + gath[k]
        sum_ref[...] = total

    return pl.pallas_call(
        body, name="allreduce_small", in_specs=[_VMEM], out_specs=_VMEM,
        out_shape=jax.ShapeDtypeStruct((r, lanes), F32),
        scratch_shapes=[pltpu.VMEM((N_DEV, r, lanes), F32)] + _sem_pairs(N_DEV - 1),
        compiler_params=pltpu.CompilerParams(vmem_limit_bytes=VMEM_LIMIT),
    )(v)


def _sum_tile(rows, width):
    return _tile(rows, max(16, (1 << 19) // width // 16 * 16), 16)


def _sum_sibling(x, recv, core):
    _, n, w = x.shape
    tr = _sum_tile(n, w)

    def body(idx_ref, x_ref, r_ref, o_ref):
        o_ref[...] = (x_ref[...].astype(F32) + r_ref[...].astype(F32)).astype(o_ref.dtype)

    row = pl.BlockSpec((tr, w), lambda i, idx: (i, 0))
    return pl.pallas_call(
        body, name="rs_sum_sibling",
        grid_spec=pltpu.PrefetchScalarGridSpec(
            num_scalar_prefetch=1, grid=(n // tr,),
            in_specs=[pl.BlockSpec((None, tr, w), lambda i, idx: (idx[0], i, 0)), row], out_specs=row),
        out_shape=jax.ShapeDtypeStruct((n, w), x.dtype), compiler_params=_params("parallel"),
    )(core.reshape(1), x, recv)


def _sum_chips(s, recv, chip, core):
    _, m, w = s.shape
    tr = _sum_tile(m, w)

    def body(idx_ref, s_ref, r0_ref, r1_ref, r2_ref, o_ref):
        o_ref[...] = ((s_ref[...].astype(F32) + r0_ref[...].astype(F32)) + r1_ref[...].astype(F32)) \
            + r2_ref[...].astype(F32)

    def got(k):
        return pl.BlockSpec((None, tr, w), lambda i, idx: (k, i, 0))

    return pl.pallas_call(
        body, name="rs_sum_chips",
        grid_spec=pltpu.PrefetchScalarGridSpec(
            num_scalar_prefetch=1, grid=(m // tr,),
            in_specs=[pl.BlockSpec((None, tr, w), lambda i, idx: (idx[0], i, 0)), got(0), got(1), got(2)],
            out_specs=pl.BlockSpec((None, tr, w), lambda i, idx: (idx[1], i, 0))),
        out_shape=jax.ShapeDtypeStruct((2, m, w), F32), compiler_params=_params("parallel"),
    )(jnp.stack([chip, core]), s, recv, recv, recv)


_ROWS = ("w_mix_out", "w_xq", "w_xk", "w_xv", "w_xo", "w_ffn_down")
_CONVS = ("gdn_conv", "sc_conv", "ffn_conv")
_REPLICATED = ("mix_norm", "gdn_a_log", "gdn_dt_bias", "gdn_out_norm", "xattn_norm", "mem_norm", "ffn_norm",
               "final_norm")
_WEIGHTS = ("mix_norm", "w_mix_in", "gdn_conv", "gdn_a_log", "gdn_dt_bias", "gdn_out_norm", "sc_conv", "w_mix_out",
            "xattn_norm", "mem_norm", "w_xq", "w_xk", "w_xv", "w_xo", "ffn_norm", "w_ffn_up", "ffn_conv",
            "w_ffn_down", "final_norm")


def _pad_rows(flat, groups):
    unit = groups * 16 * LANES
    p = flat.shape[-1]
    pad = -p % unit
    if pad:
        flat = jnp.pad(flat, [(0, 0)] * (flat.ndim - 1) + [(0, pad)])
    return flat.reshape(flat.shape[:-1] + (groups, (p + pad) // (groups * LANES), LANES))


def _split_flat(flat, shapes):
    out, off = [], 0
    for shp in shapes:
        size = 1
        for n in shp:
            size *= n
        out.append(flat[..., off:off + size].reshape(flat.shape[:-1] + tuple(shp)))
        off += size
    return out


def _by_chip(g, axis):
    rows, cols = g.shape
    if axis == 0:
        return g.reshape(N_CHIPS, rows // N_CHIPS, cols)
    return g.reshape(rows, N_CHIPS, cols // N_CHIPS).transpose(1, 0, 2)


def _halves_by_chip(per_layer):
    depth = len(per_layer)
    _, rows, w = per_layer[0].shape
    x = jnp.stack(per_layer).astype(WIRE_DTYPE).reshape(2, depth // 2, N_CHIPS, rows, w)
    return x.transpose(0, 2, 1, 3, 4).reshape(2, N_CHIPS, depth // 2 * rows, w)


def kernel(x, mem, mix_norm, w_mix_in, gdn_conv, gdn_a_log, gdn_dt_bias, gdn_out_norm, sc_conv, w_mix_out, xattn_norm, mem_norm, w_xq, w_xk, w_xv, w_xo, ffn_norm, w_ffn_up, ffn_conv, w_ffn_down, final_norm, loss_target, m_mix_norm, m_w_mix_in, m_gdn_conv, m_gdn_a_log, m_gdn_dt_bias, m_gdn_out_norm, m_sc_conv, m_w_mix_out, m_xattn_norm, m_mem_norm, m_w_xq, m_w_xk, m_w_xv, m_w_xo, m_ffn_norm, m_w_ffn_up, m_ffn_conv, m_w_ffn_down, m_final_norm, v_mix_norm, v_w_mix_in, v_gdn_conv, v_gdn_a_log, v_gdn_dt_bias, v_gdn_out_norm, v_sc_conv, v_w_mix_out, v_xattn_norm, v_mem_norm, v_w_xq, v_w_xk, v_w_xv, v_w_xo, v_ffn_norm, v_w_ffn_up, v_ffn_conv, v_w_ffn_down, v_final_norm):
    w = dict(zip(_WEIGHTS, (mix_norm, w_mix_in, gdn_conv, gdn_a_log, gdn_dt_bias, gdn_out_norm, sc_conv, w_mix_out,
                            xattn_norm, mem_norm, w_xq, w_xk, w_xv, w_xo, ffn_norm, w_ffn_up, ffn_conv, w_ffn_down,
                            final_norm)))
    m = dict(zip(_WEIGHTS, (m_mix_norm, m_w_mix_in, m_gdn_conv, m_gdn_a_log, m_gdn_dt_bias, m_gdn_out_norm, m_sc_conv,
                            m_w_mix_out, m_xattn_norm, m_mem_norm, m_w_xq, m_w_xk, m_w_xv, m_w_xo, m_ffn_norm,
                            m_w_ffn_up, m_ffn_conv, m_w_ffn_down, m_final_norm)))
    v = dict(zip(_WEIGHTS, (v_mix_norm, v_w_mix_in, v_gdn_conv, v_gdn_a_log, v_gdn_dt_bias, v_gdn_out_norm, v_sc_conv,
                            v_w_mix_out, v_xattn_norm, v_mem_norm, v_w_xq, v_w_xk, v_w_xv, v_w_xo, v_ffn_norm,
                            v_w_ffn_up, v_ffn_conv, v_w_ffn_down, v_final_norm)))
    core = lax.axis_index("c")
    chip = 2 * lax.axis_index("x") + lax.axis_index("y")
    depth, heads = gdn_a_log.shape
    dh = gdn_out_norm.shape[1]
    d, wid = x.shape[2], heads * dh

    row_sizes = [w[n].shape[1] for n in _ROWS]
    row_offs = [sum(row_sizes[:k]) for k in range(len(_ROWS))]
    src_rows = jnp.concatenate([w[n] for n in _ROWS], axis=1).astype(WIRE_DTYPE)
    src_convs = _pad_rows(jnp.concatenate([w[n].reshape(-1) for n in _CONVS]), 2)
    g_in, g_up, g_rows, g_convs = _allgather_chips(
        [w_mix_in.astype(WIRE_DTYPE), w_ffn_up.astype(WIRE_DTYPE), src_rows, src_convs])
    conv_full = {n: jnp.moveaxis(part, 0, 2).reshape(depth, part.shape[2], -1)
                 for n, part in zip(_CONVS, _split_flat(g_convs.reshape(N_CHIPS, -1), [w[n].shape for n in _CONVS]))}
    layers = []
    for l in range(depth):
        cols = lambda g: jnp.concatenate([g[j, l] for j in range(N_CHIPS)], axis=1)
        rows = lambda k: jnp.concatenate(
            [g_rows[j, l, row_offs[k]:row_offs[k] + row_sizes[k]] for j in range(N_CHIPS)], axis=0)
        small = {"mix_norm": mix_norm[l], "xattn_norm": xattn_norm[l], "mem_norm": mem_norm[l],
                 "ffn_norm": ffn_norm[l], "gdn_out_norm": gdn_out_norm[l], "a_log": gdn_a_log[l],
                 "dt_bias": gdn_dt_bias[l], "gdn_conv": conv_full["gdn_conv"][l], "sc_conv": conv_full["sc_conv"][l],
                 "ffn_conv": conv_full["ffn_conv"][l]}
        layers.append(_layer_params(cols(g_in), rows(0), rows(1), rows(2), rows(3), rows(4), cols(g_up), rows(5),
                                    small, heads, dh))

    loss_row, dx, per_layer, g_final = _local_step(x[0], mem[0], loss_target[0], layers, final_norm, heads, dh)

    def in_by_chip(g):
        return _by_chip(jnp.concatenate([g["wmain"][:, :4 * wid], g["wba"][:, :2 * heads], g["wmain"][:, 4 * wid:]],
                                        axis=1), 1)

    def rows_by_chip(g):
        parts = (g["wout"], g["wq"], g["wkv"][:, :d], g["wkv"][:, d:], g["wo"], g["wdown"])
        return jnp.concatenate([_by_chip(p, 0) for p in parts], axis=1)

    bufs = [_halves_by_chip([in_by_chip(g) for g in per_layer]),
            _halves_by_chip([_by_chip(g["wup"], 1) for g in per_layer]),
            _halves_by_chip([rows_by_chip(g) for g in per_layer])]
    from_sibling = _sibling_exchange(bufs)
    chip_sums = [_sum_sibling(b.reshape(2, -1, b.shape[-1]), r.reshape(-1, r.shape[-1]), core).reshape(r.shape)
                 for b, r in zip(bufs, from_sibling)]
    from_chips = _chip_exchange(chip_sums)
    reduced = _sibling_share([_sum_chips(s, r, chip, core) for s, r in zip(chip_sums, from_chips)])
    g_in_s, g_up_s, g_rows_s = (r.reshape(depth, -1, r.shape[-1]) for r in reduced)
    grad = {"w_mix_in": g_in_s, "w_ffn_up": g_up_s}
    for n, off, size in zip(_ROWS, row_offs, row_sizes):
        grad[n] = g_rows_s[:, off:off + size]

    stack = lambda k: jnp.stack([g[k] for g in per_layer])
    small_g = {"mix_norm": stack("mix_norm"), "gdn_a_log": stack("a_log").reshape(depth, heads),
               "gdn_dt_bias": stack("dt_bias").reshape(depth, heads), "gdn_out_norm": stack("gdn_out_norm"),
               "xattn_norm": stack("xattn_norm"), "mem_norm": stack("mem_norm"), "ffn_norm": stack("ffn_norm"),
               "final_norm": g_final, "gdn_conv": stack("gdn_conv"), "sc_conv": stack("sc_conv"),
               "ffn_conv": stack("ffn_conv")}
    names = _REPLICATED + _CONVS
    small = jnp.concatenate([small_g[n].reshape(-1) for n in names] + [loss_row[0, :1]])
    small_sum = _allreduce_small(_pad_rows(small, 1)[0]).reshape(-1)
    parts = _split_flat(small_sum, [small_g[n].shape for n in names] + [(1,)])
    g_rep = dict(zip(_REPLICATED, parts[:len(_REPLICATED)]))
    for n, part in zip(_CONVS, parts[len(_REPLICATED):-1]):
        grad[n] = lax.dynamic_slice_in_dim(part, chip * w[n].shape[2], w[n].shape[2], axis=2)
    loss = parts[-1][0]

    delta, new_m, new_v = {}, {}, {}
    for n in ("w_mix_in", "w_ffn_up") + _ROWS + _CONVS:
        delta[n], new_m[n], new_v[n] = _adamw("adamw_" + n, w[n], grad[n], m[n], v[n])
    pack_rep = lambda t: _pad_rows(jnp.concatenate([t[n].reshape(-1) for n in _REPLICATED]), 1)[0]
    outs = _adamw("adamw_replicated", pack_rep(w), pack_rep(g_rep), pack_rep(m), pack_rep(v))
    shapes = [w[n].shape for n in _REPLICATED]
    for tgt, packed_out in zip((delta, new_m, new_v), outs):
        tgt.update(zip(_REPLICATED, _split_flat(packed_out.reshape(-1), shapes)))
    grad.update(g_rep)
    return (loss, dx[None], *[grad[n] for n in _WEIGHTS], *[delta[n] for n in _WEIGHTS],
            *[new_m[n] for n in _WEIGHTS], *[new_v[n] for n in _WEIGHTS])
```

```python
import functools

import jax
import jax.numpy as jnp
from jax import lax
from jax.experimental import pallas as pl
from jax.experimental.pallas import tpu as pltpu

F32 = jnp.float32
MXU_DTYPE = jnp.bfloat16
WIRE_DTYPE = jnp.bfloat16
SOLVE_PRECISION = lax.Precision.HIGH
EPS = 1e-6
CHUNK = 64
XATTN_HEADS = 4
LANES = 128
HALO = 16
VMEM_LIMIT = 52 * 1024 * 1024
ADAM_LR, ADAM_B1, ADAM_B2, ADAM_EPS, ADAM_WD, ADAM_STEP = 0.001, 0.9, 0.999, 1e-08, 0.01, 10
MESH = pl.DeviceIdType.MESH
N_CHIPS = 4
N_DEV = 8

_DIMS = {
    "nn": (((1,), (0,)), ((), ())),
    "nt": (((1,), (1,)), ((), ())),
    "tn": (((0,), (0,)), ((), ())),
}


def _tile(n, pref, align=LANES):
    if n <= pref:
        return n
    t = (pref // align) * align
    while t >= align:
        if n % t == 0:
            return t
        t -= align
    return n


def _params(*sem):
    return pltpu.CompilerParams(dimension_semantics=sem, vmem_limit_bytes=VMEM_LIMIT)


def _dot(a, b, form, hi=False):
    (ca, cb), _ = _DIMS[form]
    dims = (((ca[0] + 1,), (cb[0] + 1,)), ((0,), (0,))) if a.ndim == 3 else _DIMS[form]
    if hi:
        return lax.dot_general(a.astype(F32), b.astype(F32), dims, precision=SOLVE_PRECISION,
                               preferred_element_type=F32)
    return lax.dot_general(a.astype(MXU_DTYPE), b.astype(MXU_DTYPE), dims, preferred_element_type=F32)


@functools.partial(jax.custom_vjp, nondiff_argnums=(2, 3))
def _dot_d(a, b, form, hi):
    return _dot(a, b, form, hi)


def _dot_d_fwd(a, b, form, hi):
    return _dot(a, b, form, hi), (a, b)


def _dot_d_bwd(form, hi, res, g):
    a, b = res
    if form == "nn":
        da, db = _dot_d(g, b, "nt", hi), _dot_d(a, g, "tn", hi)
    elif form == "nt":
        da, db = _dot_d(g, b, "nn", hi), _dot_d(g, a, "tn", hi)
    else:
        da, db = _dot_d(b, g, "nt", hi), _dot_d(a, g, "nn", hi)
    return da.astype(a.dtype), db.astype(b.dtype)


_dot_d.defvjp(_dot_d_fwd, _dot_d_bwd)


def _tri_inv_impl(a, mmh):
    c = a.shape[-1]
    r = lax.broadcasted_iota(jnp.int32, (c, c), 0)
    s = lax.broadcasted_iota(jnp.int32, (c, c), 1)
    eye = (r == s).astype(F32)
    diag_blk = (r // 16) == (s // 16)
    d = jnp.where(diag_blk, a, 0.0)
    low = a - d
    d2 = mmh(d, d)
    d4 = mmh(d2, d2)
    d8 = mmh(d4, d4)
    td = mmh(mmh(mmh(eye - d, eye + d2), eye + d4), eye + d8)
    n = mmh(td, low)
    acc = eye - n
    p = n
    pw = 1
    while 2 * pw < c // 16:
        p = mmh(p, p)
        pw *= 2
        acc = mmh(acc, eye + p)
    return mmh(acc, td)


def _mmh_plain(a, b):
    return _dot(a, b, "nn", True)


@jax.custom_vjp
def _tri_inv_d(a):
    return _tri_inv_impl(a, _mmh_plain)


def _tri_inv_d_fwd(a):
    t = _tri_inv_impl(a, _mmh_plain)
    return t, t


def _tri_inv_d_bwd(t, g):
    return (-_dot(_dot(t, g, "tn", True), t, "nt", True),)


_tri_inv_d.defvjp(_tri_inv_d_fwd, _tri_inv_d_bwd)


class _Ops:
    def __init__(self, diff):
        self.diff = diff

    def mm(self, a, b, form="nn"):
        return _dot_d(a, b, form, False) if self.diff else _dot(a, b, form, False)

    def mmh(self, a, b, form="nn"):
        return _dot_d(a, b, form, True) if self.diff else _dot(a, b, form, True)

    def tri_inv(self, a):
        return _tri_inv_d(a) if self.diff else _tri_inv_impl(a, _mmh_plain)


_PLAIN = _Ops(False)
_DIFF = _Ops(True)


def _sigmoid(x):
    return 1.0 / (1.0 + jnp.exp(-x))


def _silu(x):
    return x * _sigmoid(x)


def _softplus(x):
    return jnp.maximum(x, 0.0) + jnp.log(1.0 + jnp.exp(-jnp.abs(x)))


def _rms(x, g):
    return x * lax.rsqrt(jnp.mean(x * x, axis=-1, keepdims=True) + EPS) * g


def _matmul_tiles(m, n, k):
    if k <= 2048:
        return _tile(m, 1024), _tile(n, 1408), k
    if k <= 8192:
        return _tile(m, 512), _tile(n, 512), k
    return _tile(m, 1024), _tile(n, 1024), _tile(k, 2816)


def _matmul(name, a, b, form, out_dtype, add=None):
    if form == "nn":
        (m, k), (k2, n) = a.shape, b.shape
    elif form == "nt":
        (m, k), (n, k2) = a.shape, b.shape
    else:
        (k, m), (k2, n) = a.shape, b.shape
    assert k == k2, (name, a.shape, b.shape, form)
    tm, tn, tk = _matmul_tiles(m, n, k)
    nk = k // tk
    out_bytes = tm * tn * (jnp.dtype(out_dtype).itemsize + (4 if add is not None else 0))
    vmem = 2 * (tm * tk * a.dtype.itemsize + tk * tn * b.dtype.itemsize + out_bytes) + (tm * tn * 4 if nk > 1 else 0)
    assert vmem <= VMEM_LIMIT, (name, tm, tn, tk, vmem)
    if form == "nn":
        a_spec = pl.BlockSpec((tm, tk), lambda i, j, kk: (i, kk))
        b_spec = pl.BlockSpec((tk, tn), lambda i, j, kk: (kk, j))
    elif form == "nt":
        a_spec = pl.BlockSpec((tm, tk), lambda i, j, kk: (i, kk))
        b_spec = pl.BlockSpec((tn, tk), lambda i, j, kk: (j, kk))
    else:
        a_spec = pl.BlockSpec((tk, tm), lambda i, j, kk: (kk, i))
        b_spec = pl.BlockSpec((tk, tn), lambda i, j, kk: (kk, j))
    o_spec = pl.BlockSpec((tm, tn), lambda i, j, kk: (i, j))
    has_add = add is not None

    def body(*refs):
        a_ref, b_ref = refs[0], refs[1]
        add_ref = refs[2] if has_add else None
        o_ref = refs[3] if has_add else refs[2]

        def finish(acc):
            if has_add:
                acc = acc + add_ref[...].astype(F32)
            o_ref[...] = acc.astype(o_ref.dtype)

        p = _dot(a_ref[...], b_ref[...], form)
        if nk == 1:
            finish(p)
        else:
            acc_ref = refs[-1]
            kk = pl.program_id(2)

            @pl.when(kk == 0)
            def _():
                acc_ref[...] = p

            @pl.when(kk > 0)
            def _():
                acc_ref[...] += p

            @pl.when(kk == nk - 1)
            def _():
                finish(acc_ref[...])

    return pl.pallas_call(
        body, name=name, grid=(m // tm, n // tn, nk),
        in_specs=[a_spec, b_spec] + ([o_spec] if has_add else []), out_specs=o_spec,
        out_shape=jax.ShapeDtypeStruct((m, n), out_dtype),
        scratch_shapes=[pltpu.VMEM((tm, tn), F32)] if nk > 1 else [],
        compiler_params=_params("parallel", "parallel", "arbitrary"),
    )(*((a, b, add) if has_add else (a, b)))


def _rms_fwd(name, x, g):
    t, d = x.shape
    tm = _tile(t, 512, 16)

    def body(x_ref, g_ref, o_ref):
        o_ref[...] = _rms(x_ref[...], g_ref[...]).astype(o_ref.dtype)

    return pl.pallas_call(
        body, name=name, grid=(t // tm,),
        in_specs=[pl.BlockSpec((tm, d), lambda i: (i, 0)), pl.BlockSpec((1, d), lambda i: (0, 0))],
        out_specs=pl.BlockSpec((tm, d), lambda i: (i, 0)),
        out_shape=jax.ShapeDtypeStruct((t, d), MXU_DTYPE), compiler_params=_params("parallel"),
    )(x, g.reshape(1, d))


def _rms_bwd(name, x, g, dh, dres=None):
    t, d = x.shape
    tm = _tile(t, 256, 16)
    has_res = dres is not None

    def body(*refs):
        x_ref, g_ref, dh_ref = refs[:3]
        dres_ref = refs[3] if has_res else None
        dx_ref, dxb_ref, dg_ref = refs[-3:]
        _, vjp = jax.vjp(_rms, x_ref[...], g_ref[...])
        dx, dg = vjp(dh_ref[...].astype(F32))
        if has_res:
            dx = dx + dres_ref[...]
        dx_ref[...] = dx
        dxb_ref[...] = dx.astype(dxb_ref.dtype)
        first = pl.program_id(0) == 0

        @pl.when(first)
        def _():
            dg_ref[...] = dg

        @pl.when(jnp.logical_not(first))
        def _():
            dg_ref[...] += dg

    row = pl.BlockSpec((tm, d), lambda i: (i, 0))
    vec = pl.BlockSpec((1, d), lambda i: (0, 0))
    dx, dxb, dg = pl.pallas_call(
        body, name=name, grid=(t // tm,),
        in_specs=[row, vec, row] + ([row] if has_res else []), out_specs=[row, row, vec],
        out_shape=[jax.ShapeDtypeStruct((t, d), F32), jax.ShapeDtypeStruct((t, d), MXU_DTYPE),
                   jax.ShapeDtypeStruct((1, d), F32)],
        compiler_params=_params("arbitrary"),
    )(*((x, g.reshape(1, d), dh) + ((dres,) if has_res else ())))
    return dx, dxb, dg.reshape(d)


def _final_loss(x, g, target):
    t, d = x.shape
    tm = _tile(t, 256, 16)

    def body(x_ref, g_ref, t_ref, loss_ref, dx_ref, dxb_ref, dg_ref):
        y, vjp = jax.vjp(_rms, x_ref[...], g_ref[...])
        err = y - t_ref[...]
        dx, dg = vjp(err * (1.0 / d))
        dx_ref[...] = dx
        dxb_ref[...] = dx.astype(dxb_ref.dtype)
        part = jnp.zeros((1, LANES), F32) + 0.5 * jnp.sum(jnp.mean(err * err, axis=-1, keepdims=True))
        first = pl.program_id(0) == 0

        @pl.when(first)
        def _():
            dg_ref[...] = dg
            loss_ref[...] = part

        @pl.when(jnp.logical_not(first))
        def _():
            dg_ref[...] += dg
            loss_ref[...] += part

    row = pl.BlockSpec((tm, d), lambda i: (i, 0))
    vec = pl.BlockSpec((1, d), lambda i: (0, 0))
    loss, dx, dxb, dg = pl.pallas_call(
        body, name="final_loss", grid=(t // tm,), in_specs=[row, vec, row],
        out_specs=[pl.BlockSpec((1, LANES), lambda i: (0, 0)), row, row, vec],
        out_shape=[jax.ShapeDtypeStruct((1, LANES), F32), jax.ShapeDtypeStruct((t, d), F32),
                   jax.ShapeDtypeStruct((t, d), MXU_DTYPE), jax.ShapeDtypeStruct((1, d), F32)],
        compiler_params=_params("arbitrary"),
    )(x, g.reshape(1, d), target)
    return loss, dx, dxb, dg.reshape(d)


def _conv_taps(x_ext, w, rows):
    kk = w.shape[0]
    y = x_ext[HALO:] * w[kk - 1:kk, :]
    for j in range(kk - 1):
        y = y + pltpu.roll(x_ext, kk - 1 - j, axis=0)[HALO:] * w[j:j + 1, :]
    return y


def _col_specs(tm, tn, col0, t_rows):
    assert col0 % tn == 0 and tm % HALO == 0
    c0 = col0 // tn
    per, last = tm // HALO, t_rows // HALO - 1
    tile = pl.BlockSpec((tm, tn), lambda j, i: (i, c0 + j))
    prev = pl.BlockSpec((HALO, tn), lambda j, i: (jnp.maximum(i * per - 1, 0), c0 + j))
    nxt = pl.BlockSpec((HALO, tn), lambda j, i: (jnp.minimum((i + 1) * per, last), c0 + j))
    return tile, prev, nxt


def _conv_fwd(name, xa, xa_col, w, w_col, ncols, out_dtype, xb=None, xb_col=0, gate=None, gate_col=0):
    t = xa.shape[0]
    kk = w.shape[0]
    tm, tn = _tile(t, 512, HALO), _tile(ncols, 512)
    nrow = t // tm
    has_b, has_g = xb is not None, gate is not None

    def body(*refs):
        refs = list(refs)
        xa_ref, xap_ref = refs.pop(0), refs.pop(0)
        xb_ref, xbp_ref = (refs.pop(0), refs.pop(0)) if has_b else (None, None)
        w_ref = refs.pop(0)
        g_ref = refs.pop(0) if has_g else None
        o_ref = refs.pop(0)
        i = pl.program_id(1)
        x, xp = xa_ref[...].astype(F32), xap_ref[...].astype(F32)
        if has_b:
            x, xp = x * xb_ref[...].astype(F32), xp * xbp_ref[...].astype(F32)
        xp = jnp.where(i == 0, 0.0, xp)
        y = _conv_taps(jnp.concatenate([xp, x], axis=0), w_ref[...], tm)
        if has_g:
            y = y * g_ref[...].astype(F32)
        o_ref[...] = y.astype(o_ref.dtype)

    a_tile, a_prev, _ = _col_specs(tm, tn, xa_col, t)
    ins, specs = [xa, xa], [a_tile, a_prev]
    if has_b:
        b_tile, b_prev, _ = _col_specs(tm, tn, xb_col, t)
        ins, specs = ins + [xb, xb], specs + [b_tile, b_prev]
    assert w_col % tn == 0
    ins, specs = ins + [w], specs + [pl.BlockSpec((kk, tn), lambda j, i: (0, w_col // tn + j))]
    if has_g:
        ins, specs = ins + [gate], specs + [_col_specs(tm, tn, gate_col, t)[0]]
    return pl.pallas_call(
        body, name=name, grid=(ncols // tn, nrow), in_specs=specs,
        out_specs=pl.BlockSpec((tm, tn), lambda j, i: (i, j)),
        out_shape=jax.ShapeDtypeStruct((t, ncols), out_dtype), compiler_params=_params("parallel", "parallel"),
    )(*ins)


def _conv_bwd(name, xa, xa_col, w, w_col, dy, dy_col, ncols, dx_dtype, xb=None, xb_col=0, gate=None, gate_col=0):
    t = xa.shape[0]
    kk = w.shape[0]
    tm, tn = _tile(t, 512, HALO), _tile(ncols, 512)
    nrow = t // tm
    has_b, has_g = xb is not None, gate is not None

    def body(*refs):
        refs = list(refs)
        xa_ref, xap_ref = refs.pop(0), refs.pop(0)
        xb_ref, xbp_ref = (refs.pop(0), refs.pop(0)) if has_b else (None, None)
        w_ref = refs.pop(0)
        dy_ref, dyn_ref = refs.pop(0), refs.pop(0)
        g_ref, gn_ref = (refs.pop(0), refs.pop(0)) if has_g else (None, None)
        dxa_ref = refs.pop(0)
        dxb_ref = refs.pop(0) if has_b else None
        dg_ref = refs.pop(0) if has_g else None
        dw_ref = refs.pop(0)
        i = pl.program_id(1)
        wv = w_ref[...]
        xa_t, xa_p = xa_ref[...].astype(F32), xap_ref[...].astype(F32)
        x, xp = xa_t, xa_p
        if has_b:
            xb_t = xb_ref[...].astype(F32)
            x, xp = x * xb_t, xp * xbp_ref[...].astype(F32)
        xp = jnp.where(i == 0, 0.0, xp)
        x_ext = jnp.concatenate([xp, x], axis=0)
        dyv, dyn = dy_ref[...].astype(F32), dyn_ref[...].astype(F32)
        if has_g:
            dg_ref[...] = (dyv * _conv_taps(x_ext, wv, tm)).astype(dg_ref.dtype)
            dyv, dyn = dyv * g_ref[...].astype(F32), dyn * gn_ref[...].astype(F32)
        dyn = jnp.where(i == nrow - 1, 0.0, dyn)
        dy_ext = jnp.concatenate([dyv, dyn], axis=0)
        dx = dyv * wv[kk - 1:kk, :]
        row8 = lax.broadcasted_iota(jnp.int32, (8, tn), 0)
        dw = jnp.where(row8 == kk - 1, jnp.sum(dyv * x, axis=0, keepdims=True), 0.0)
        for j in range(kk - 1):
            s = kk - 1 - j
            dx = dx + pltpu.roll(dy_ext, tm + HALO - s, axis=0)[:tm] * wv[j:j + 1, :]
            dwj = jnp.sum(dyv * pltpu.roll(x_ext, s, axis=0)[HALO:], axis=0, keepdims=True)
            dw = dw + jnp.where(row8 == j, dwj, 0.0)
        if has_b:
            dxa_ref[...] = (dx * xb_t).astype(dxa_ref.dtype)
            dxb_ref[...] = (dx * xa_t).astype(dxb_ref.dtype)
        else:
            dxa_ref[...] = dx.astype(dxa_ref.dtype)

        @pl.when(i == 0)
        def _():
            dw_ref[...] = dw

        @pl.when(i > 0)
        def _():
            dw_ref[...] += dw

    a_tile, a_prev, _ = _col_specs(tm, tn, xa_col, t)
    ins, specs = [xa, xa], [a_tile, a_prev]
    if has_b:
        b_tile, b_prev, _ = _col_specs(tm, tn, xb_col, t)
        ins, specs = ins + [xb, xb], specs + [b_tile, b_prev]
    assert w_col % tn == 0
    ins, specs = ins + [w], specs + [pl.BlockSpec((kk, tn), lambda j, i: (0, w_col // tn + j))]
    d_tile, _, d_next = _col_specs(tm, tn, dy_col, t)
    ins, specs = ins + [dy, dy], specs + [d_tile, d_next]
    if has_g:
        g_tile, _, g_next = _col_specs(tm, tn, gate_col, t)
        ins, specs = ins + [gate, gate], specs + [g_tile, g_next]
    out_tile = pl.BlockSpec((tm, tn), lambda j, i: (i, j))
    shapes, ospecs = [jax.ShapeDtypeStruct((t, ncols), dx_dtype)], [out_tile]
    if has_b:
        shapes, ospecs = shapes + [jax.ShapeDtypeStruct((t, ncols), dx_dtype)], ospecs + [out_tile]
    if has_g:
        shapes, ospecs = shapes + [jax.ShapeDtypeStruct((t, ncols), dx_dtype)], ospecs + [out_tile]
    shapes, ospecs = shapes + [jax.ShapeDtypeStruct((8, ncols), F32)], ospecs + [pl.BlockSpec((8, tn), lambda j, i: (0, j))]
    outs = list(pl.pallas_call(
        body, name=name, grid=(ncols // tn, nrow), in_specs=specs, out_specs=ospecs, out_shape=shapes,
        compiler_params=_params("parallel", "arbitrary"),
    )(*ins))
    dxa = outs.pop(0)
    dxb = outs.pop(0) if has_b else None
    dgate = outs.pop(0) if has_g else None
    return dxa, dxb, dgate, outs.pop(0)[:kk]


def _swiglu_fwd(u):
    t, f2 = u.shape
    f = f2 // 2
    tm, tn = _tile(t, 512, 16), _tile(f, 512)
    nf = f // tn

    def body(g_ref, u_ref, o_ref):
        o_ref[...] = (_silu(g_ref[...].astype(F32)) * u_ref[...].astype(F32)).astype(o_ref.dtype)

    return pl.pallas_call(
        body, name="swiglu_fwd", grid=(t // tm, nf),
        in_specs=[pl.BlockSpec((tm, tn), lambda i, j: (i, j)), pl.BlockSpec((tm, tn), lambda i, j: (i, nf + j))],
        out_specs=pl.BlockSpec((tm, tn), lambda i, j: (i, j)),
        out_shape=jax.ShapeDtypeStruct((t, f), MXU_DTYPE), compiler_params=_params("parallel", "parallel"),
    )(u, u)


def _swiglu_bwd(u, da):
    t, f2 = u.shape
    f = f2 // 2
    tm, tn = _tile(t, 512, 16), _tile(f, 512)
    nf = f // tn

    def body(g_ref, u_ref, da_ref, o_ref):
        g, d = g_ref[...].astype(F32), da_ref[...].astype(F32)
        sg = _sigmoid(g)
        gate_half = pl.program_id(1) < nf

        @pl.when(gate_half)
        def _():
            o_ref[...] = (d * u_ref[...].astype(F32) * (sg * (1.0 + g * (1.0 - sg)))).astype(o_ref.dtype)

        @pl.when(jnp.logical_not(gate_half))
        def _():
            o_ref[...] = (d * (g * sg)).astype(o_ref.dtype)

    return pl.pallas_call(
        body, name="swiglu_bwd", grid=(t // tm, 2 * nf),
        in_specs=[pl.BlockSpec((tm, tn), lambda i, j: (i, j % nf)),
                  pl.BlockSpec((tm, tn), lambda i, j: (i, nf + j % nf)),
                  pl.BlockSpec((tm, tn), lambda i, j: (i, j % nf))],
        out_specs=pl.BlockSpec((tm, tn), lambda i, j: (i, j)),
        out_shape=jax.ShapeDtypeStruct((t, f2), MXU_DTYPE), compiler_params=_params("parallel", "parallel"),
    )(u, u, da)


def _gdn_prep(ops, qc, kc, vc, b_col, a_col, a_log, dt_bias):
    c, dh = qc.shape[-2:]
    q, k, v = _silu(qc), _silu(kc), _silu(vc)
    q = q * lax.rsqrt(jnp.sum(q * q, axis=-1, keepdims=True) + EPS) * (dh ** -0.5)
    k = k * lax.rsqrt(jnp.sum(k * k, axis=-1, keepdims=True) + EPS)
    beta = _sigmoid(b_col)
    g_col = -jnp.exp(a_log) * _softplus(a_col + dt_bias)
    r = lax.broadcasted_iota(jnp.int32, (c, c), 0)
    s = lax.broadcasted_iota(jnp.int32, (c, c), 1)
    g_row = jnp.sum(jnp.where(r == s, g_col, 0.0), axis=-2, keepdims=True)
    gc_col = jnp.sum(jnp.where(s <= r, g_row, 0.0), axis=-1, keepdims=True)
    gc_row = jnp.sum(jnp.where(r <= s, g_col, 0.0), axis=-2, keepdims=True)
    decay = jnp.exp(jnp.where(s <= r, gc_col - gc_row, -1e30))
    kb = k * beta
    a = jnp.where(s < r, ops.mm(kb, k, "nt") * decay, 0.0)
    tinv = ops.tri_inv(a)
    e_col = jnp.exp(gc_col)
    u = ops.mmh(tinv, v * beta)
    w = ops.mmh(tinv, kb * e_col)
    attn = ops.mm(q, k, "nt") * decay
    g_last = jnp.sum(g_col, axis=-2, keepdims=True)
    return u, w, attn, q * e_col, k * jnp.exp(g_last - gc_col), g_last


def _gdn_step(ops, state, u, w, attn, q_dec, k_dec, g_last):
    v_new = u - ops.mm(w, state)
    o = ops.mm(q_dec, state) + ops.mm(attn, v_new)
    return o, state * jnp.exp(g_last) + ops.mm(k_dec, v_new, "tn")


PREP_HEADS, SCAN_HEADS = 2, 8


def _gdn_blocks(t, heads, hb_pref):
    tc = _tile(t, 256, CHUNK)
    hb = max(h for h in range(1, hb_pref + 1) if heads % h == 0)
    return tc, hb


def _to_chunks(ref, hb, dh):
    tc = ref.shape[0]
    return jnp.concatenate([ref[:, h * dh:(h + 1) * dh].astype(F32).reshape(tc // CHUNK, CHUNK, dh)
                            for h in range(hb)], axis=0)


def _from_chunks(ref, val, hb, dh):
    tc = ref.shape[0]
    ncb = tc // CHUNK
    for h in range(hb):
        ref[:, h * dh:(h + 1) * dh] = val[h * ncb:(h + 1) * ncb].reshape(tc, dh).astype(ref.dtype)


def _per_chunk(s, ncb):
    hb = s.shape[0]
    return jnp.broadcast_to(s[:, None], (hb, ncb, 1, 1)).reshape(hb * ncb, 1, 1)


def _gdn_prep_fwd(qkv, b_col, a_col, a_log, dt_bias, heads, dh):
    t = qkv.shape[0]
    tc, hb = _gdn_blocks(t, heads, PREP_HEADS)
    ncb, nhb, width = tc // CHUNK, heads // hb, heads * dh
    nc = t // CHUNK

    def body(q_ref, k_ref, v_ref, b_ref, a_ref, al_ref, dt_ref, u_ref, w_ref, p_ref, qd_ref, kd_ref, gl_ref):
        u, w, p, qd, kd, gl = _gdn_prep(
            _PLAIN, _to_chunks(q_ref, hb, dh), _to_chunks(k_ref, hb, dh), _to_chunks(v_ref, hb, dh),
            b_ref[...].reshape(hb * ncb, CHUNK, 1), a_ref[...].reshape(hb * ncb, CHUNK, 1),
            _per_chunk(al_ref[...], ncb), _per_chunk(dt_ref[...], ncb))
        _from_chunks(u_ref, u, hb, dh)
        _from_chunks(w_ref, w, hb, dh)
        _from_chunks(qd_ref, qd, hb, dh)
        _from_chunks(kd_ref, kd, hb, dh)
        p_ref[...] = p.reshape(hb, tc, CHUNK).astype(p_ref.dtype)
        gl_ref[...] = gl.reshape(hb, ncb, 1, 1)

    def tok(off):
        return pl.BlockSpec((tc, hb * dh), lambda i, j: (i, off * nhb + j))

    gate = pl.BlockSpec((hb, ncb, CHUNK, 1), lambda i, j: (j, i, 0, 0))
    scal = pl.BlockSpec((hb, 1, 1), lambda i, j: (j, 0, 0))
    return pl.pallas_call(
        body, name="gdn_prep_fwd", grid=(t // tc, nhb),
        in_specs=[tok(0), tok(1), tok(2), gate, gate, scal, scal],
        out_specs=[tok(0), tok(0), pl.BlockSpec((hb, tc, CHUNK), lambda i, j: (j, i, 0)), tok(0), tok(0),
                   pl.BlockSpec((hb, ncb, 1, 1), lambda i, j: (j, i, 0, 0))],
        out_shape=[jax.ShapeDtypeStruct((t, width), F32), jax.ShapeDtypeStruct((t, width), MXU_DTYPE),
                   jax.ShapeDtypeStruct((heads, t, CHUNK), MXU_DTYPE), jax.ShapeDtypeStruct((t, width), MXU_DTYPE),
                   jax.ShapeDtypeStruct((t, width), MXU_DTYPE), jax.ShapeDtypeStruct((heads, nc, 1, 1), F32)],
        compiler_params=_params("parallel", "parallel"),
    )(qkv, qkv, qkv, b_col, a_col, a_log, dt_bias)


def _gdn_prep_bwd(qkv, b_col, a_col, a_log, dt_bias, du, dw, dp, dqd, dkd, dgl, heads, dh):
    t = qkv.shape[0]
    tc, hb = _gdn_blocks(t, heads, PREP_HEADS)
    ncb, nhb, width = tc // CHUNK, heads // hb, heads * dh
    nc = t // CHUNK

    def body(q_ref, k_ref, v_ref, b_ref, a_ref, al_ref, dt_ref, du_ref, dw_ref, dp_ref, dqd_ref, dkd_ref, dgl_ref,
             dq_ref, dk_ref, dv_ref, db_ref, da_ref, dal_ref, ddt_ref):
        first = pl.program_id(1) == 0

        def prep(q, k, v, b, a, al, dt):
            return _gdn_prep(_DIFF, q, k, v, b, a, _per_chunk(al, ncb), _per_chunk(dt, ncb))

        _, vjp = jax.vjp(prep, _to_chunks(q_ref, hb, dh), _to_chunks(k_ref, hb, dh), _to_chunks(v_ref, hb, dh),
                         b_ref[...].reshape(hb * ncb, CHUNK, 1), a_ref[...].reshape(hb * ncb, CHUNK, 1),
                         al_ref[...], dt_ref[...])
        dq, dk, dv, db, da, dal, ddt = vjp((
            _to_chunks(du_ref, hb, dh), _to_chunks(dw_ref, hb, dh), dp_ref[...].reshape(hb * ncb, CHUNK, CHUNK),
            _to_chunks(dqd_ref, hb, dh), _to_chunks(dkd_ref, hb, dh), dgl_ref[...].reshape(hb * ncb, 1, 1)))
        _from_chunks(dq_ref, dq, hb, dh)
        _from_chunks(dk_ref, dk, hb, dh)
        _from_chunks(dv_ref, dv, hb, dh)
        db_ref[...] = db.reshape(hb, ncb, CHUNK, 1)
        da_ref[...] = da.reshape(hb, ncb, CHUNK, 1)

        @pl.when(first)
        def _():
            dal_ref[...] = dal
            ddt_ref[...] = ddt

        @pl.when(jnp.logical_not(first))
        def _():
            dal_ref[...] += dal
            ddt_ref[...] += ddt

    def tok(off):
        return pl.BlockSpec((tc, hb * dh), lambda j, i: (i, off * nhb + j))

    gate = pl.BlockSpec((hb, ncb, CHUNK, 1), lambda j, i: (j, i, 0, 0))
    scal = pl.BlockSpec((hb, 1, 1), lambda j, i: (j, 0, 0))
    pspec = pl.BlockSpec((hb, tc, CHUNK), lambda j, i: (j, i, 0))
    glspec = pl.BlockSpec((hb, ncb, 1, 1), lambda j, i: (j, i, 0, 0))
    tokf = jax.ShapeDtypeStruct((t, width), F32)
    gatef = jax.ShapeDtypeStruct((heads, nc, CHUNK, 1), F32)
    scalf = jax.ShapeDtypeStruct((heads, 1, 1), F32)
    return pl.pallas_call(
        body, name="gdn_prep_bwd", grid=(nhb, t // tc),
        in_specs=[tok(0), tok(1), tok(2), gate, gate, scal, scal, tok(0), tok(0), pspec, tok(0), tok(0), glspec],
        out_specs=[tok(0), tok(0), tok(0), gate, gate, scal, scal],
        out_shape=[tokf, tokf, tokf, gatef, gatef, scalf, scalf],
        compiler_params=_params("arbitrary", "arbitrary"),
    )(qkv, qkv, qkv, b_col, a_col, a_log, dt_bias, du, dw, dp, dqd, dkd, dgl)


def _heads(ref, rows, hb, dh):
    return jnp.stack([ref[rows, h * dh:(h + 1) * dh].astype(F32) for h in range(hb)])


def _put_heads(ref, rows, val, dh):
    for h in range(val.shape[0]):
        ref[rows, h * dh:(h + 1) * dh] = val[h].astype(ref.dtype)


def _gdn_scan_fwd(u, w, p, qd, kd, gl, heads, dh):
    t = u.shape[0]
    tc, hb = _gdn_blocks(t, heads, SCAN_HEADS)
    ncb, nhb = tc // CHUNK, heads // hb
    nc = t // CHUNK

    def body(u_ref, w_ref, p_ref, qd_ref, kd_ref, gl_ref, o_ref, s_ref, state):
        @pl.when(pl.program_id(1) == 0)
        def _():
            state[...] = jnp.zeros_like(state)

        for c in range(ncb):
            rs = slice(c * CHUNK, (c + 1) * CHUNK)
            s_in = state[...]
            s_ref[:, c] = s_in
            o, s_out = _gdn_step(_PLAIN, s_in, _heads(u_ref, rs, hb, dh), _heads(w_ref, rs, hb, dh), p_ref[:, rs, :],
                                 _heads(qd_ref, rs, hb, dh), _heads(kd_ref, rs, hb, dh), gl_ref[:, c])
            _put_heads(o_ref, rs, o, dh)
            state[...] = s_out

    tok = pl.BlockSpec((tc, hb * dh), lambda j, i: (i, j))
    pspec = pl.BlockSpec((hb, tc, CHUNK), lambda j, i: (j, i, 0))
    glspec = pl.BlockSpec((hb, ncb, 1, 1), lambda j, i: (j, i, 0, 0))
    return pl.pallas_call(
        body, name="gdn_scan_fwd", grid=(nhb, t // tc),
        in_specs=[tok, tok, pspec, tok, tok, glspec],
        out_specs=[tok, pl.BlockSpec((hb, ncb, dh, dh), lambda j, i: (j, i, 0, 0))],
        out_shape=[jax.ShapeDtypeStruct((t, heads * dh), F32), jax.ShapeDtypeStruct((heads, nc, dh, dh), F32)],
        scratch_shapes=[pltpu.VMEM((hb, dh, dh), F32)],
        compiler_params=_params("arbitrary", "arbitrary"),
    )(u, w, p, qd, kd, gl)


def _gdn_scan_bwd(u, w, p, qd, kd, gl, states, do, heads, dh):
    t = u.shape[0]
    tc, hb = _gdn_blocks(t, heads, SCAN_HEADS)
    ncb, nhb = tc // CHUNK, heads // hb
    nc, nt = t // CHUNK, t // tc

    def body(u_ref, w_ref, p_ref, qd_ref, kd_ref, gl_ref, s_ref, do_ref,
             du_ref, dw_ref, dp_ref, dqd_ref, dkd_ref, dgl_ref, dstate):
        @pl.when(pl.program_id(1) == 0)
        def _():
            dstate[...] = jnp.zeros_like(dstate)

        for c in reversed(range(ncb)):
            rs = slice(c * CHUNK, (c + 1) * CHUNK)
            _, vjp = jax.vjp(functools.partial(_gdn_step, _DIFF), s_ref[:, c], _heads(u_ref, rs, hb, dh),
                             _heads(w_ref, rs, hb, dh), p_ref[:, rs, :].astype(F32), _heads(qd_ref, rs, hb, dh),
                             _heads(kd_ref, rs, hb, dh), gl_ref[:, c])
            ds, du, dw, dp, dqd, dkd, dgl = vjp((_heads(do_ref, rs, hb, dh), dstate[...]))
            dstate[...] = ds
            _put_heads(du_ref, rs, du, dh)
            _put_heads(dw_ref, rs, dw, dh)
            _put_heads(dqd_ref, rs, dqd, dh)
            _put_heads(dkd_ref, rs, dkd, dh)
            dp_ref[:, rs, :] = dp
            dgl_ref[:, c] = dgl

    tok = pl.BlockSpec((tc, hb * dh), lambda j, i: (nt - 1 - i, j))
    pspec = pl.BlockSpec((hb, tc, CHUNK), lambda j, i: (j, nt - 1 - i, 0))
    glspec = pl.BlockSpec((hb, ncb, 1, 1), lambda j, i: (j, nt - 1 - i, 0, 0))
    sspec = pl.BlockSpec((hb, ncb, dh, dh), lambda j, i: (j, nt - 1 - i, 0, 0))
    tokf = jax.ShapeDtypeStruct((t, heads * dh), F32)
    return pl.pallas_call(
        body, name="gdn_scan_bwd", grid=(nhb, nt),
        in_specs=[tok, tok, pspec, tok, tok, glspec, sspec, tok],
        out_specs=[tok, tok, pspec, tok, tok, glspec],
        out_shape=[tokf, tokf, jax.ShapeDtypeStruct((heads, t, CHUNK), F32), tokf, tokf,
                   jax.ShapeDtypeStruct((heads, nc, 1, 1), F32)],
        scratch_shapes=[pltpu.VMEM((hb, dh, dh), F32)],
        compiler_params=_params("arbitrary", "arbitrary"),
    )(u, w, p, qd, kd, gl, states, do)


def _gdn_post(o, z, gain):
    return _rms(o, gain) * _silu(z)


def _gdn_post_fwd(o, pm, z_col, gain, heads, dh):
    t = o.shape[0]
    tm = _tile(t, 512, 16)
    z0 = z_col // dh

    def body(o_ref, z_ref, g_ref, y_ref):
        y_ref[...] = _gdn_post(o_ref[...], z_ref[...], g_ref[...]).astype(y_ref.dtype)

    return pl.pallas_call(
        body, name="gdn_post_fwd", grid=(t // tm, heads),
        in_specs=[pl.BlockSpec((tm, dh), lambda i, h: (i, h)), pl.BlockSpec((tm, dh), lambda i, h: (i, z0 + h)),
                  pl.BlockSpec((1, dh), lambda i, h: (0, 0))],
        out_specs=pl.BlockSpec((tm, dh), lambda i, h: (i, h)),
        out_shape=jax.ShapeDtypeStruct((t, heads * dh), MXU_DTYPE), compiler_params=_params("parallel", "parallel"),
    )(o, pm, gain.reshape(1, dh))


def _gdn_post_bwd(o, pm, z_col, gain, dy, heads, dh):
    t = o.shape[0]
    tm = _tile(t, 512, 16)
    z0 = z_col // dh

    def body(o_ref, z_ref, g_ref, dy_ref, do_ref, dz_ref, dg_ref):
        _, vjp = jax.vjp(_gdn_post, o_ref[...], z_ref[...], g_ref[...])
        do, dz, dg = vjp(dy_ref[...])
        do_ref[...] = do
        dz_ref[...] = dz.astype(dz_ref.dtype)
        first = jnp.logical_and(pl.program_id(0) == 0, pl.program_id(1) == 0)

        @pl.when(first)
        def _():
            dg_ref[...] = dg

        @pl.when(jnp.logical_not(first))
        def _():
            dg_ref[...] += dg

    blk = pl.BlockSpec((tm, dh), lambda i, h: (i, h))
    vec = pl.BlockSpec((1, dh), lambda i, h: (0, 0))
    do, dz, dg = pl.pallas_call(
        body, name="gdn_post_bwd", grid=(t // tm, heads),
        in_specs=[blk, pl.BlockSpec((tm, dh), lambda i, h: (i, z0 + h)), vec, blk],
        out_specs=[blk, blk, vec],
        out_shape=[jax.ShapeDtypeStruct((t, heads * dh), F32), jax.ShapeDtypeStruct((t, heads * dh), MXU_DTYPE),
                   jax.ShapeDtypeStruct((1, dh), F32)],
        compiler_params=_params("arbitrary", "arbitrary"),
    )(o, pm, gain.reshape(1, dh), dy)
    return do, dz, dg.reshape(dh)


def _attn(ops, q, kv):
    d = q.shape[1]
    hd = d // XATTN_HEADS
    outs = []
    for h in range(XATTN_HEADS):
        qh, kh, vh = q[:, h * hd:(h + 1) * hd], kv[:, h * hd:(h + 1) * hd], kv[:, d + h * hd:d + (h + 1) * hd]
        s = ops.mm(qh, kh, "nt") * (hd ** -0.5)
        e = jnp.exp(s - lax.stop_gradient(jnp.max(s, axis=-1, keepdims=True)))
        outs.append(ops.mm(e / jnp.sum(e, axis=-1, keepdims=True), vh))
    return jnp.concatenate(outs, axis=1)


def _attn_fwd(q, kv):
    t, d = q.shape
    nm = kv.shape[0]
    tm = _tile(t, 512, 16)

    def body(q_ref, kv_ref, o_ref):
        o_ref[...] = _attn(_PLAIN, q_ref[...], kv_ref[...]).astype(o_ref.dtype)

    return pl.pallas_call(
        body, name="xattn_fwd", grid=(t // tm,),
        in_specs=[pl.BlockSpec((tm, d), lambda i: (i, 0)), pl.BlockSpec((nm, 2 * d), lambda i: (0, 0))],
        out_specs=pl.BlockSpec((tm, d), lambda i: (i, 0)),
        out_shape=jax.ShapeDtypeStruct((t, d), MXU_DTYPE), compiler_params=_params("parallel"),
    )(q, kv)


def _attn_bwd(q, kv, do):
    t, d = q.shape
    nm = kv.shape[0]
    tm = _tile(t, 256, 16)

    def body(q_ref, kv_ref, do_ref, dq_ref, dkv_ref):
        _, vjp = jax.vjp(functools.partial(_attn, _DIFF), q_ref[...].astype(F32), kv_ref[...].astype(F32))
        dq, dkv = vjp(do_ref[...].astype(F32))
        dq_ref[...] = dq.astype(dq_ref.dtype)
        first = pl.program_id(0) == 0

        @pl.when(first)
        def _():
            dkv_ref[...] = dkv

        @pl.when(jnp.logical_not(first))
        def _():
            dkv_ref[...] += dkv

    row = pl.BlockSpec((tm, d), lambda i: (i, 0))
    full = pl.BlockSpec((nm, 2 * d), lambda i: (0, 0))
    return pl.pallas_call(
        body, name="xattn_bwd", grid=(t // tm,), in_specs=[row, full, row], out_specs=[row, full],
        out_shape=[jax.ShapeDtypeStruct((t, d), MXU_DTYPE), jax.ShapeDtypeStruct((nm, 2 * d), F32)],
        compiler_params=_params("arbitrary"),
    )(q, kv, do)


def _adamw(name, w, g, m, v):
    shape = w.shape
    cols = shape[-1]
    rows = w.size // cols
    w2, g2, m2, v2 = (a.reshape(rows, cols) for a in (w, g, m, v))
    tr = _tile(rows, max(8, (1 << 18) // cols // 8 * 8), 8)

    def body(w_ref, g_ref, m_ref, v_ref, d_ref, nm_ref, nv_ref):
        gv = g_ref[...]
        nm = ADAM_B1 * m_ref[...] + (1.0 - ADAM_B1) * gv
        nv = ADAM_B2 * v_ref[...] + (1.0 - ADAM_B2) * jnp.square(gv)
        m_hat = nm / (1.0 - ADAM_B1 ** ADAM_STEP)
        v_hat = nv / (1.0 - ADAM_B2 ** ADAM_STEP)
        d_ref[...] = -ADAM_LR * (m_hat / (jnp.sqrt(v_hat) + ADAM_EPS) + ADAM_WD * w_ref[...])
        nm_ref[...] = nm
        nv_ref[...] = nv

    blk = pl.BlockSpec((tr, cols), lambda i: (i, 0))
    out = jax.ShapeDtypeStruct((rows, cols), F32)
    d, nm, nv = pl.pallas_call(
        body, name=name, grid=(rows // tr,), in_specs=[blk] * 4, out_specs=[blk] * 3, out_shape=[out] * 3,
        compiler_params=_params("parallel"),
    )(w2, g2, m2, v2)
    return d.reshape(shape), nm.reshape(shape), nv.reshape(shape)


def _gate_cols(pba, heads):
    t = pba.shape[0]
    to_col = lambda a: a.T.reshape(heads, t // CHUNK, CHUNK, 1)
    return to_col(pba[:, :heads]), to_col(pba[:, heads:2 * heads])


def _layer_fwd(x, mem, p, heads, dh):
    wid = heads * dh
    sc = x.shape[1] - wid
    s = {"x0": x}
    s["h1"] = _rms_fwd("rms_mix", x, p["mix_norm"])
    s["pm"] = pm = _matmul("mm_mix_in", s["h1"], p["wmain"], "nn", F32)
    pba = _matmul("mm_mix_ba", s["h1"], p["wba"], "nn", F32)
    s["qkv"] = _conv_fwd("conv_gdn", pm, 0, p["gdn_conv"], 0, 3 * wid, F32)
    s["b_col"], s["a_col"] = _gate_cols(pba, heads)
    s["prep"] = _gdn_prep_fwd(s["qkv"], s["b_col"], s["a_col"], p["a_log"], p["dt_bias"], heads, dh)
    s["o"], s["states"] = _gdn_scan_fwd(*s["prep"], heads, dh)
    y_gdn = _gdn_post_fwd(s["o"], pm, 3 * wid, p["gdn_out_norm"], heads, dh)
    y_sc = _conv_fwd("conv_sc", pm, 4 * wid + sc, p["sc_conv"], 0, sc, MXU_DTYPE, xb=pm, xb_col=4 * wid + 2 * sc,
                     gate=pm, gate_col=4 * wid)
    s["ycat"] = jnp.concatenate([y_gdn, y_sc], axis=1)
    s["x1"] = x1 = _matmul("mm_mix_out", s["ycat"], p["wout"], "nn", F32, add=x)
    s["h2"] = _rms_fwd("rms_xattn", x1, p["xattn_norm"])
    s["q"] = _matmul("mm_xq", s["h2"], p["wq"], "nn", MXU_DTYPE)
    s["memn"] = _rms_fwd("rms_mem", mem, p["mem_norm"])
    s["kv"] = _matmul("mm_xkv", s["memn"], p["wkv"], "nn", MXU_DTYPE)
    s["ao"] = _attn_fwd(s["q"], s["kv"])
    s["x2"] = x2 = _matmul("mm_xo", s["ao"], p["wo"], "nn", F32, add=x1)
    s["h3"] = _rms_fwd("rms_ffn", x2, p["ffn_norm"])
    s["upre"] = _matmul("mm_ffn_up", s["h3"], p["wup"], "nn", MXU_DTYPE)
    s["uc"] = _conv_fwd("conv_ffn", s["upre"], 0, p["ffn_conv"], 0, s["upre"].shape[1], MXU_DTYPE)
    s["act"] = _swiglu_fwd(s["uc"])
    return _matmul("mm_ffn_down", s["act"], p["wdown"], "nn", F32, add=x2), s


def _layer_bwd(dx3, dx3b, mem, s, p, heads, dh):
    wid = heads * dh
    sc = dx3.shape[1] - wid
    t = dx3.shape[0]
    pm = s["pm"]
    g = {}
    da = _matmul("mm_ffn_down_dx", dx3b, p["wdown"], "nt", MXU_DTYPE)
    g["wdown"] = _matmul("mm_ffn_down_dw", s["act"], dx3b, "tn", WIRE_DTYPE)
    du = _swiglu_bwd(s["uc"], da)
    dupre, _, _, g["ffn_conv"] = _conv_bwd("conv_ffn_bwd", s["upre"], 0, p["ffn_conv"], 0, du, 0, du.shape[1],
                                           MXU_DTYPE)
    dh3 = _matmul("mm_ffn_up_dx", dupre, p["wup"], "nt", F32)
    g["wup"] = _matmul("mm_ffn_up_dw", s["h3"], dupre, "tn", WIRE_DTYPE)
    dx2, dx2b, g["ffn_norm"] = _rms_bwd("rms_ffn_bwd", s["x2"], p["ffn_norm"], dh3, dx3)
    dao = _matmul("mm_xo_dx", dx2b, p["wo"], "nt", MXU_DTYPE)
    g["wo"] = _matmul("mm_xo_dw", s["ao"], dx2b, "tn", WIRE_DTYPE)
    dq, dkv = _attn_bwd(s["q"], s["kv"], dao)
    dh2 = _matmul("mm_xq_dx", dq, p["wq"], "nt", F32)
    g["wq"] = _matmul("mm_xq_dw", s["h2"], dq, "tn", WIRE_DTYPE)
    dmemn = _matmul("mm_xkv_dx", dkv, p["wkv"], "nt", F32)
    g["wkv"] = _matmul("mm_xkv_dw", s["memn"], dkv, "tn", WIRE_DTYPE)
    _, _, g["mem_norm"] = _rms_bwd("rms_mem_bwd", mem, p["mem_norm"], dmemn)
    dx1, dx1b, g["xattn_norm"] = _rms_bwd("rms_xattn_bwd", s["x1"], p["xattn_norm"], dh2, dx2)
    dycat = _matmul("mm_mix_out_dx", dx1b, p["wout"], "nt", F32)
    g["wout"] = _matmul("mm_mix_out_dw", s["ycat"], dx1b, "tn", WIRE_DTYPE)
    d_c, d_h, d_b, g["sc_conv"] = _conv_bwd("conv_sc_bwd", pm, 4 * wid + sc, p["sc_conv"], 0, dycat, wid, sc,
                                             MXU_DTYPE, xb=pm, xb_col=4 * wid + 2 * sc, gate=pm, gate_col=4 * wid)
    do, dz, g["gdn_out_norm"] = _gdn_post_bwd(s["o"], pm, 3 * wid, p["gdn_out_norm"], dycat, heads, dh)
    dprep = _gdn_scan_bwd(*s["prep"], s["states"], do, heads, dh)
    dqc, dkc, dvc, db_col, da_col, g["a_log"], g["dt_bias"] = _gdn_prep_bwd(
        s["qkv"], s["b_col"], s["a_col"], p["a_log"], p["dt_bias"], *dprep, heads, dh)
    dqkv, _, _, g["gdn_conv"] = _conv_bwd("conv_gdn_bwd", pm, 0, p["gdn_conv"], 0,
                                          jnp.concatenate([dqc, dkc, dvc], axis=1), 0, 3 * wid, MXU_DTYPE)
    dpm = jnp.concatenate([dqkv, dz, d_b, d_c, d_h], axis=1)
    from_col = lambda a: a.reshape(heads, t).T
    dpba = jnp.concatenate([from_col(db_col), from_col(da_col), jnp.zeros((t, LANES - 2 * heads), F32)],
                           axis=1).astype(MXU_DTYPE)
    dh1 = _matmul("mm_mix_in_dx", dpm, p["wmain"], "nt", F32)
    dh1 = _matmul("mm_mix_ba_dx", dpba, p["wba"], "nt", F32, add=dh1)
    g["wmain"] = _matmul("mm_mix_in_dw", s["h1"], dpm, "tn", WIRE_DTYPE)
    g["wba"] = _matmul("mm_mix_ba_dw", s["h1"], dpba, "tn", WIRE_DTYPE)
    dx0, dx0b, g["mix_norm"] = _rms_bwd("rms_mix_bwd", s["x0"], p["mix_norm"], dh1, dx1)
    return dx0, dx0b, g


def _layer_params(win, wout, wq, wk, wv, wo, wup, wdown, small, heads, dh):
    wid = heads * dh
    p = dict(small)
    p.update({
        "a_log": small["a_log"].reshape(heads, 1, 1), "dt_bias": small["dt_bias"].reshape(heads, 1, 1),
        "wmain": jnp.concatenate([win[:, :4 * wid], win[:, 4 * wid + 2 * heads:]], axis=1),
        "wba": jnp.pad(win[:, 4 * wid:4 * wid + 2 * heads], ((0, 0), (0, LANES - 2 * heads))),
        "wout": wout, "wq": wq, "wkv": jnp.concatenate([wk, wv], axis=1), "wo": wo, "wup": wup, "wdown": wdown,
    })
    return p


def _local_step(x, mem, target, layers, final_norm, heads, dh):
    saved = []
    for p in layers:
        x, s = _layer_fwd(x, mem, p, heads, dh)
        saved.append(s)
    loss, dx, dxb, g_final = _final_loss(x, final_norm, target)
    per_layer = []
    for p, s in zip(reversed(layers), reversed(saved)):
        dx, dxb, g = _layer_bwd(dx, dxb, mem, s, p, heads, dh)
        per_layer.append(g)
    per_layer.reverse()
    return loss, dx, per_layer, g_final


_ANY =pl.BlockSpec(memory_space=pl.ANY)
_VMEM = pl.BlockSpec(memory_space=pltpu.VMEM)


def _mesh_pos():
    return lax.axis_index("x"), lax.axis_index("y"), lax.axis_index("c")


def _other_chips(x, y):
    return [(1 - x, y), (x, 1 - y), (1 - x, 1 - y)]


def _push(src, dst, sems, k, to):
    return pltpu.make_async_remote_copy(src_ref=src, dst_ref=dst, send_sem=sems[0].at[k], recv_sem=sems[1].at[k],
                                        device_id=to, device_id_type=MESH)


def _sem_pairs(n):
    return [pltpu.SemaphoreType.DMA((n,)), pltpu.SemaphoreType.DMA((n,))]


def _allgather_chips(srcs):
    n = len(srcs)

    def body(*refs):
        src, out, sems = refs[:n], refs[n:2 * n], refs[2 * n:]
        x, y, c = _mesh_pos()
        me, sibling, chips = 2 * x + y, (x, y, 1 - c), _other_chips(x, y)
        started, halves = [], []
        for t in range(n):
            half = src[t].shape[0] // 2
            mine, other = pl.ds(c * half, half), pl.ds((1 - c) * half, half)
            halves.append((mine, other))
            started.append(_push(src[t], out[t].at[me], sems, 7 * t + 6, sibling))
            started += [_push(src[t].at[mine], out[t].at[me, mine], sems, 7 * t + k, (cx, cy, c))
                        for k, (cx, cy) in enumerate(chips)]
        for cp in started:
            cp.start()
        for t in range(n):
            for k, (cx, cy) in enumerate(chips):
                landed = out[t].at[2 * cx + cy, halves[t][0]]
                _push(landed, landed, sems, 7 * t + k, sibling).wait_recv()
                fwd = _push(landed, landed, sems, 7 * t + 3 + k, sibling)
                fwd.start()
                started.append(fwd)
        for t in range(n):
            for k, (cx, cy) in enumerate(chips):
                landed = out[t].at[2 * cx + cy, halves[t][1]]
                _push(landed, landed, sems, 7 * t + 3 + k, sibling).wait_recv()
            _push(src[t], out[t].at[me], sems, 7 * t + 6, sibling).wait_recv()
        for cp in started:
            cp.wait_send()

    return pl.pallas_call(
        body, name="allgather_weights", in_specs=[_ANY] * n, out_specs=[_ANY] * n,
        out_shape=[jax.ShapeDtypeStruct((N_CHIPS,) + s.shape, s.dtype) for s in srcs], scratch_shapes=_sem_pairs(7 * n),
    )(*srcs)


def _sibling_exchange(bufs):
    n = len(bufs)

    def body(*refs):
        src, out, sems = refs[:n], refs[n:2 * n], refs[2 * n:]
        x, y, c = _mesh_pos()
        copies = [_push(src[t].at[1 - c], out[t], sems, t, (x, y, 1 - c)) for t in range(n)]
        for cp in copies:
            cp.start()
        for cp in copies:
            cp.wait()

    return pl.pallas_call(
        body, name="rs_sibling_exchange", in_specs=[_ANY] * n, out_specs=[_ANY] * n,
        out_shape=[jax.ShapeDtypeStruct(b.shape[1:], b.dtype) for b in bufs], scratch_shapes=_sem_pairs(n),
    )(*bufs)


def _chip_exchange(bufs):
    n = len(bufs)

    def body(*refs):
        src, out, sems = refs[:n], refs[n:2 * n], refs[2 * n:]
        x, y, c = _mesh_pos()
        copies = [_push(src[t].at[2 * cx + cy], out[t].at[k], sems, 3 * t + k, (cx, cy, c))
                  for t in range(n) for k, (cx, cy) in enumerate(_other_chips(x, y))]
        for cp in copies:
            cp.start()
        for cp in copies:
            cp.wait()

    return pl.pallas_call(
        body, name="rs_chip_exchange", in_specs=[_ANY] * n, out_specs=[_ANY] * n,
        out_shape=[jax.ShapeDtypeStruct((3,) + b.shape[1:], b.dtype) for b in bufs], scratch_shapes=_sem_pairs(3 * n),
    )(*bufs)


def _sibling_share(bufs):
    n = len(bufs)

    def body(*refs):
        src, out, sems = refs[:n], refs[n:2 * n], refs[2 * n:]
        x, y, c = _mesh_pos()
        copies = [_push(src[t].at[c], out[t].at[c], sems, t, (x, y, 1 - c)) for t in range(n)]
        for cp in copies:
            cp.start()
        for cp in copies:
            cp.wait()

    return pl.pallas_call(
        body, name="rs_sibling_share", in_specs=[_ANY] * n, out_specs=[_ANY] * n,
        out_shape=[jax.ShapeDtypeStruct(b.shape, b.dtype) for b in bufs], scratch_shapes=_sem_pairs(n),
        input_output_aliases={t: t for t in range(n)},
    )(*bufs)


def _allreduce_small(v):
    r, lanes = v.shape

    def body(v_ref, sum_ref, gath, send_sems, recv_sems):
        x, y, c = _mesh_pos()
        me = 4 * x + 2 * y + c
        gath[me] = v_ref[...]
        copies = []
        for rel in range(1, N_DEV):
            peer = tuple(1 - p if (rel >> b) & 1 else p for p, b in ((x, 2), (y, 1), (c, 0)))
            copies.append(pltpu.make_async_remote_copy(
                src_ref=v_ref, dst_ref=gath.at[me], send_sem=send_sems.at[rel - 1], recv_sem=recv_sems.at[rel - 1],
                device_id=peer, device_id_type=MESH))
        for cp in copies:
            cp.start()
        for cp in copies:
            cp.wait()
        total = gath[0]
        for k in range(1, N_DEV):
            total = total + gath[k]
        sum_ref[...] = total

    return pl.pallas_call(
        body, name="allreduce_small", in_specs=[_VMEM], out_specs=_VMEM,
        out_shape=jax.ShapeDtypeStruct((r, lanes), F32),
        scratch_shapes=[pltpu.VMEM((N_DEV, r, lanes), F32)] + _sem_pairs(N_DEV - 1),
        compiler_params=pltpu.CompilerParams(vmem_limit_bytes=VMEM_LIMIT),
    )(v)


def _sum_tile(rows, width):
    return _tile(rows, max(16, (1 << 19) // width // 16 * 16), 16)


def _sum_sibling(x, recv, core):
    _, n, w = x.shape
    tr = _sum_tile(n, w)

    def body(idx_ref, x_ref, r_ref, o_ref):
        o_ref[...] = (x_ref[...].astype(F32) + r_ref[...].astype(F32)).astype(o_ref.dtype)

    row = pl.BlockSpec((tr, w), lambda i, idx: (i, 0))
    return pl.pallas_call(
        body, name="rs_sum_sibling",
        grid_spec=pltpu.PrefetchScalarGridSpec(
            num_scalar_prefetch=1, grid=(n // tr,),
            in_specs=[pl.BlockSpec((None, tr, w), lambda i, idx: (idx[0], i, 0)), row], out_specs=row),
        out_shape=jax.ShapeDtypeStruct((n, w), x.dtype), compiler_params=_params("parallel"),
    )(core.reshape(1), x, recv)


def _sum_chips(s, recv, chip, core):
    _, m, w = s.shape
    tr = _sum_tile(m, w)

    def body(idx_ref, s_ref, r0_ref, r1_ref, r2_ref, o_ref):
        o_ref[...] = ((s_ref[...].astype(F32) + r0_ref[...].astype(F32)) + r1_ref[...].astype(F32)) \
            + r2_ref[...].astype(F32)

    def got(k):
        return pl.BlockSpec((None, tr, w), lambda i, idx: (k, i, 0))

    return pl.pallas_call(
        body, name="rs_sum_chips",
        grid_spec=pltpu.PrefetchScalarGridSpec(
            num_scalar_prefetch=1, grid=(m // tr,),
            in_specs=[pl.BlockSpec((None, tr, w), lambda i, idx: (idx[0], i, 0)), got(0), got(1), got(2)],
            out_specs=pl.BlockSpec((None, tr, w), lambda i, idx: (idx[1], i, 0))),
        out_shape=jax.ShapeDtypeStruct((2, m, w), F32), compiler_params=_params("parallel"),
    )(jnp.stack([chip, core]), s, recv, recv, recv)


_ROWS = ("w_mix_out", "w_xq", "w_xk", "w_xv", "w_xo", "w_ffn_down")
_CONVS = ("gdn_conv", "sc_conv", "ffn_conv")
_REPLICATED = ("mix_norm", "gdn_a_log", "gdn_dt_bias", "gdn_out_norm", "xattn_norm", "mem_norm", "ffn_norm",
               "final_norm")
_WEIGHTS = ("mix_norm", "w_mix_in", "gdn_conv", "gdn_a_log", "gdn_dt_bias", "gdn_out_norm", "sc_conv", "w_mix_out",
            "xattn_norm", "mem_norm", "w_xq", "w_xk", "w_xv", "w_xo", "ffn_norm", "w_ffn_up", "ffn_conv",
            "w_ffn_down", "final_norm")


def _pad_rows(flat, groups):
    unit = groups * 16 * LANES
    p = flat.shape[-1]
    pad = -p % unit
    if pad:
        flat = jnp.pad(flat, [(0, 0)] * (flat.ndim - 1) + [(0, pad)])
    return flat.reshape(flat.shape[:-1] + (groups, (p + pad) // (groups * LANES), LANES))


def _split_flat(flat, shapes):
    out, off = [], 0
    for shp in shapes:
        size = 1
        for n in shp:
            size *= n
        out.append(flat[..., off:off + size].reshape(flat.shape[:-1] + tuple(shp)))
        off += size
    return out


def _by_chip(g, axis):
    rows, cols = g.shape
    if axis == 0:
        return g.reshape(N_CHIPS, rows // N_CHIPS, cols)
    return g.reshape(rows, N_CHIPS, cols // N_CHIPS).transpose(1, 0, 2)


def _halves_by_chip(per_layer):
    depth = len(per_layer)
    _, rows, w = per_layer[0].shape
    x = jnp.stack(per_layer).astype(WIRE_DTYPE).reshape(2, depth // 2, N_CHIPS, rows, w)
    return x.transpose(0, 2, 1, 3, 4).reshape(2, N_CHIPS, depth // 2 * rows, w)


def kernel(x, mem, mix_norm, w_mix_in, gdn_conv, gdn_a_log, gdn_dt_bias, gdn_out_norm, sc_conv, w_mix_out, xattn_norm, mem_norm, w_xq, w_xk, w_xv, w_xo, ffn_norm, w_ffn_up, ffn_conv, w_ffn_down, final_norm, loss_target, m_mix_norm, m_w_mix_in, m_gdn_conv, m_gdn_a_log, m_gdn_dt_bias, m_gdn_out_norm, m_sc_conv, m_w_mix_out, m_xattn_norm, m_mem_norm, m_w_xq, m_w_xk, m_w_xv, m_w_xo, m_ffn_norm, m_w_ffn_up, m_ffn_conv, m_w_ffn_down, m_final_norm, v_mix_norm, v_w_mix_in, v_gdn_conv, v_gdn_a_log, v_gdn_dt_bias, v_gdn_out_norm, v_sc_conv, v_w_mix_out, v_xattn_norm, v_mem_norm, v_w_xq, v_w_xk, v_w_xv, v_w_xo, v_ffn_norm, v_w_ffn_up, v_ffn_conv, v_w_ffn_down, v_final_norm):
    w = dict(zip(_WEIGHTS, (mix_norm, w_mix_in, gdn_conv, gdn_a_log, gdn_dt_bias, gdn_out_norm, sc_conv, w_mix_out,
                            xattn_norm, mem_norm, w_xq, w_xk, w_xv, w_xo, ffn_norm, w_ffn_up, ffn_conv, w_ffn_down,
                            final_norm)))
    m = dict(zip(_WEIGHTS, (m_mix_norm, m_w_mix_in, m_gdn_conv, m_gdn_a_log, m_gdn_dt_bias, m_gdn_out_norm, m_sc_conv,
                            m_w_mix_out, m_xattn_norm, m_mem_norm, m_w_xq, m_w_xk, m_w_xv, m_w_xo, m_ffn_norm,
                            m_w_ffn_up, m_ffn_conv, m_w_ffn_down, m_final_norm)))
    v = dict(zip(_WEIGHTS, (v_mix_norm, v_w_mix_in, v_gdn_conv, v_gdn_a_log, v_gdn_dt_bias, v_gdn_out_norm, v_sc_conv,
                            v_w_mix_out, v_xattn_norm, v_mem_norm, v_w_xq, v_w_xk, v_w_xv, v_w_xo, v_ffn_norm,
                            v_w_ffn_up, v_ffn_conv, v_w_ffn_down, v_final_norm)))
    core = lax.axis_index("c")
    chip = 2 * lax.axis_index("x") + lax.axis_index("y")
    depth, heads = gdn_a_log.shape
    dh = gdn_out_norm.shape[1]
    d, wid = x.shape[2], heads * dh

    row_sizes = [w[n].shape[1] for n in _ROWS]
    row_offs = [sum(row_sizes[:k]) for k in range(len(_ROWS))]
    src_rows = jnp.concatenate([w[n] for n in _ROWS], axis=1).astype(WIRE_DTYPE)
    src_convs = _pad_rows(jnp.concatenate([w[n].reshape(-1) for n in _CONVS]), 2)
    g_in, g_up, g_rows, g_convs = _allgather_chips(
        [w_mix_in.astype(WIRE_DTYPE), w_ffn_up.astype(WIRE_DTYPE), src_rows, src_convs])
    conv_full = {n: jnp.moveaxis(part, 0, 2).reshape(depth, part.shape[2], -1)
                 for n, part in zip(_CONVS, _split_flat(g_convs.reshape(N_CHIPS, -1), [w[n].shape for n in _CONVS]))}
    layers = []
    for l in range(depth):
        cols = lambda g: jnp.concatenate([g[j, l] for j in range(N_CHIPS)], axis=1)
        rows = lambda k: jnp.concatenate(
            [g_rows[j, l, row_offs[k]:row_offs[k] + row_sizes[k]] for j in range(N_CHIPS)], axis=0)
        small = {"mix_norm": mix_norm[l], "xattn_norm": xattn_norm[l], "mem_norm": mem_norm[l],
                 "ffn_norm": ffn_norm[l], "gdn_out_norm": gdn_out_norm[l], "a_log": gdn_a_log[l],
                 "dt_bias": gdn_dt_bias[l], "gdn_conv": conv_full["gdn_conv"][l], "sc_conv": conv_full["sc_conv"][l],
                 "ffn_conv": conv_full["ffn_conv"][l]}
        layers.append(_layer_params(cols(g_in), rows(0), rows(1), rows(2), rows(3), rows(4), cols(g_up), rows(5),
                                    small, heads, dh))

    loss_row, dx, per_layer, g_final = _local_step(x[0], mem[0], loss_target[0], layers, final_norm, heads, dh)

    def in_by_chip(g):
        return _by_chip(jnp.concatenate([g["wmain"][:, :4 * wid], g["wba"][:, :2 * heads], g["wmain"][:, 4 * wid:]],
                                        axis=1), 1)

    def rows_by_chip(g):
        parts = (g["wout"], g["wq"], g["wkv"][:, :d], g["wkv"][:, d:], g["wo"], g["wdown"])
        return jnp.concatenate([_by_chip(p, 0) for p in parts], axis=1)

    bufs = [_halves_by_chip([in_by_chip(g) for g in per_layer]),
            _halves_by_chip([_by_chip(g["wup"], 1) for g in per_layer]),
            _halves_by_chip([rows_by_chip(g) for g in per_layer])]
    from_sibling = _sibling_exchange(bufs)
    chip_sums = [_sum_sibling(b.reshape(2, -1, b.shape[-1]), r.reshape(-1, r.shape[-1]), core).reshape(r.shape)
                 for b, r in zip(bufs, from_sibling)]
    from_chips = _chip_exchange(chip_sums)
    reduced = _sibling_share([_sum_chips(s, r, chip, core) for s, r in zip(chip_sums, from_chips)])
    g_in_s, g_up_s, g_rows_s = (r.reshape(depth, -1, r.shape[-1]) for r in reduced)
    grad = {"w_mix_in": g_in_s, "w_ffn_up": g_up_s}
    for n, off, size in zip(_ROWS, row_offs, row_sizes):
        grad[n] = g_rows_s[:, off:off + size]

    stack = lambda k: jnp.stack([g[k] for g in per_layer])
    small_g = {"mix_norm": stack("mix_norm"), "gdn_a_log": stack("a_log").reshape(depth, heads),
               "gdn_dt_bias": stack("dt_bias").reshape(depth, heads), "gdn_out_norm": stack("gdn_out_norm"),
               "xattn_norm": stack("xattn_norm"), "mem_norm": stack("mem_norm"), "ffn_norm": stack("ffn_norm"),
               "final_norm": g_final, "gdn_conv": stack("gdn_conv"), "sc_conv": stack("sc_conv"),
               "ffn_conv": stack("ffn_conv")}
    names = _REPLICATED + _CONVS
    small = jnp.concatenate([small_g[n].reshape(-1) for n in names] + [loss_row[0, :1]])
    small_sum = _allreduce_small(_pad_rows(small, 1)[0]).reshape(-1)
    parts = _split_flat(small_sum, [small_g[n].shape for n in names] + [(1,)])
    g_rep = dict(zip(_REPLICATED, parts[:len(_REPLICATED)]))
    for n, part in zip(_CONVS, parts[len(_REPLICATED):-1]):
        grad[n] = lax.dynamic_slice_in_dim(part, chip * w[n].shape[2], w[n].shape[2], axis=2)
    loss = parts[-1][0]

    delta, new_m, new_v = {}, {}, {}
    for n in ("w_mix_in", "w_ffn_up") + _ROWS + _CONVS:
        delta[n], new_m[n], new_v[n] = _adamw("adamw_" + n, w[n], grad[n], m[n], v[n])
    pack_rep = lambda t: _pad_rows(jnp.concatenate([t[n].reshape(-1) for n in _REPLICATED]), 1)[0]
    outs = _adamw("adamw_replicated", pack_rep(w), pack_rep(g_rep), pack_rep(m), pack_rep(v))
    shapes = [w[n].shape for n in _REPLICATED]
    for tgt, packed_out in zip((delta, new_m, new_v), outs):
        tgt.update(zip(_REPLICATED, _split_flat(packed_out.reshape(-1), shapes)))
    grad.update(g_rep)
    return (loss, dx[None], *[grad[n] for n in _WEIGHTS], *[delta[n] for n in _WEIGHTS],
            *[new_m[n] for n in _WEIGHTS], *[new_v[n] for n in _WEIGHTS])
```

```python
import functools

import jax
import jax.numpy as jnp
from jax import lax
from jax.experimental import pallas as pl
from jax.experimental.pallas import tpu as pltpu

F32 = jnp.float32
MXU_DTYPE = jnp.bfloat16
WIRE_DTYPE = jnp.bfloat16
SOLVE_PRECISION = lax.Precision.HIGH
EPS = 1e-6
CHUNK = 64
XATTN_HEADS = 4
LANES = 128
HALO = 16
VMEM_LIMIT = 52 * 1024 * 1024
ADAM_LR, ADAM_B1, ADAM_B2, ADAM_EPS, ADAM_WD, ADAM_STEP = 0.001, 0.9, 0.999, 1e-08, 0.01, 10
MESH = pl.DeviceIdType.MESH
N_CHIPS = 4
N_DEV = 8

_DIMS = {
    "nn": (((1,), (0,)), ((), ())),
    "nt": (((1,), (1,)), ((), ())),
    "tn": (((0,), (0,)), ((), ())),
}


def _tile(n, pref, align=LANES):
    if n <= pref:
        return n
    t = (pref // align) * align
    while t >= align:
        if n % t == 0:
            return t
        t -= align
    return n


def _params(*sem):
    return pltpu.CompilerParams(dimension_semantics=sem, vmem_limit_bytes=VMEM_LIMIT)


def _dot(a, b, form, hi=False):
    (ca, cb), _ = _DIMS[form]
    dims = (((ca[0] + 1,), (cb[0] + 1,)), ((0,), (0,))) if a.ndim == 3 else _DIMS[form]
    if hi:
        return lax.dot_general(a.astype(F32), b.astype(F32), dims, precision=SOLVE_PRECISION,
                               preferred_element_type=F32)
    return lax.dot_general(a.astype(MXU_DTYPE), b.astype(MXU_DTYPE), dims, preferred_element_type=F32)


@functools.partial(jax.custom_vjp, nondiff_argnums=(2, 3))
def _dot_d(a, b, form, hi):
    return _dot(a, b, form, hi)


def _dot_d_fwd(a, b, form, hi):
    return _dot(a, b, form, hi), (a, b)


def _dot_d_bwd(form, hi, res, g):
    a, b = res
    if form == "nn":
        da, db = _dot_d(g, b, "nt", hi), _dot_d(a, g, "tn", hi)
    elif form == "nt":
        da, db = _dot_d(g, b, "nn", hi), _dot_d(g, a, "tn", hi)
    else:
        da, db = _dot_d(b, g, "nt", hi), _dot_d(a, g, "nn", hi)
    return da.astype(a.dtype), db.astype(b.dtype)


_dot_d.defvjp(_dot_d_fwd, _dot_d_bwd)


def _tri_inv_impl(a, mmh):
    c = a.shape[-1]
    r = lax.broadcasted_iota(jnp.int32, (c, c), 0)
    s = lax.broadcasted_iota(jnp.int32, (c, c), 1)
    eye = (r == s).astype(F32)
    diag_blk = (r // 16) == (s // 16)
    d = jnp.where(diag_blk, a, 0.0)
    low = a - d
    d2 = mmh(d, d)
    d4 = mmh(d2, d2)
    d8 = mmh(d4, d4)
    td = mmh(mmh(mmh(eye - d, eye + d2), eye + d4), eye + d8)
    n = mmh(td, low)
    acc = eye - n
    p = n
    pw = 1
    while 2 * pw < c // 16:
        p = mmh(p, p)
        pw *= 2
        acc = mmh(acc, eye + p)
    return mmh(acc, td)


def _mmh_plain(a, b):
    return _dot(a, b, "nn", True)


@jax.custom_vjp
def _tri_inv_d(a):
    return _tri_inv_impl(a, _mmh_plain)


def _tri_inv_d_fwd(a):
    t = _tri_inv_impl(a, _mmh_plain)
    return t, t


def _tri_inv_d_bwd(t, g):
    return (-_dot(_dot(t, g, "tn", True), t, "nt", True),)


_tri_inv_d.defvjp(_tri_inv_d_fwd, _tri_inv_d_bwd)


class _Ops:
    def __init__(self, diff):
        self.diff = diff

    def mm(self, a, b, form="nn"):
        return _dot_d(a, b, form, False) if self.diff else _dot(a, b, form, False)

    def mmh(self, a, b, form="nn"):
        return _dot_d(a, b, form, True) if self.diff else _dot(a, b, form, True)

    def tri_inv(self, a):
        return _tri_inv_d(a) if self.diff else _tri_inv_impl(a, _mmh_plain)


_PLAIN = _Ops(False)
_DIFF = _Ops(True)


def _sigmoid(x):
    return 1.0 / (1.0 + jnp.exp(-x))


def _silu(x):
    return x * _sigmoid(x)


def _softplus(x):
    return jnp.maximum(x, 0.0) + jnp.log(1.0 + jnp.exp(-jnp.abs(x)))


def _rms(x, g):
    return x * lax.rsqrt(jnp.mean(x * x, axis=-1, keepdims=True) + EPS) * g


def _matmul_tiles(m, n, k):
    if k <= 2048:
        return _tile(m, 1024), _tile(n, 1408), k
    if k <= 8192:
        return _tile(m, 512), _tile(n, 512), k
    return _tile(m, 1024), _tile(n, 1024), _tile(k, 2816)


def _matmul(name, a, b, form, out_dtype, add=None, comm=None):
    if form == "nn":
        (m, k), (k2, n) = a.shape, b.shape
    elif form == "nt":
        (m, k), (n, k2) = a.shape, b.shape
    else:
        (k, m), (k2, n) = a.shape, b.shape
    assert k == k2, (name, a.shape, b.shape, form)
    tm, tn, tk = _matmul_tiles(m, n, k)
    nk = k // tk
    out_bytes = tm * tn * (jnp.dtype(out_dtype).itemsize + (4 if add is not None else 0))
    vmem = 2 * (tm * tk * a.dtype.itemsize + tk * tn * b.dtype.itemsize + out_bytes) + (tm * tn * 4 if nk > 1 else 0)
    assert vmem <= VMEM_LIMIT, (name, tm, tn, tk, vmem)
    if form == "nn":
        a_spec = pl.BlockSpec((tm, tk), lambda i, j, kk: (i, kk))
        b_spec = pl.BlockSpec((tk, tn), lambda i, j, kk: (kk, j))
    elif form == "nt":
        a_spec = pl.BlockSpec((tm, tk), lambda i, j, kk: (i, kk))
        b_spec = pl.BlockSpec((tn, tk), lambda i, j, kk: (j, kk))
    else:
        a_spec = pl.BlockSpec((tk, tm), lambda i, j, kk: (kk, i))
        b_spec = pl.BlockSpec((tk, tn), lambda i, j, kk: (kk, j))
    o_spec = pl.BlockSpec((tm, tn), lambda i, j, kk: (i, j))
    has_add = add is not None
    grid = (m // tm, n // tn, nk)
    n_in = 3 if has_add else 2
    c_in, c_out = (len(comm.ins), len(comm.outs)) if comm is not None else (0, 0)

    def body(*refs):
        a_ref, b_ref = refs[0], refs[1]
        add_ref = refs[2] if has_add else None
        o_ref = refs[n_in + c_in]
        pids = [pl.program_id(ax) for ax in range(3)]
        if comm is not None:
            comm_refs = (refs[n_in:n_in + c_in], refs[n_in + c_in + 1:n_in + c_in + 1 + c_out], refs[-2:])

            @pl.when(jnp.logical_and(jnp.logical_and(pids[0] == 0, pids[1] == 0), pids[2] == 0))
            def _():
                comm.start(*comm_refs)

        def finish(acc):
            if has_add:
                acc = acc + add_ref[...].astype(F32)
            o_ref[...] = acc.astype(o_ref.dtype)

        p = _dot(a_ref[...], b_ref[...], form)
        if nk == 1:
            finish(p)
        else:
            acc_ref = refs[n_in + c_in + 1 + c_out]

            @pl.when(pids[2] == 0)
            def _():
                acc_ref[...] = p

            @pl.when(pids[2] > 0)
            def _():
                acc_ref[...] += p

            @pl.when(pids[2] == nk - 1)
            def _():
                finish(acc_ref[...])

        if comm is not None:
            @pl.when(jnp.logical_and(jnp.logical_and(pids[0] == grid[0] - 1, pids[1] == grid[1] - 1),
                                     pids[2] == grid[2] - 1))
            def _():
                comm.finish(*comm_refs)

    acc_scratch = [pltpu.VMEM((tm, tn), F32)] if nk > 1 else []
    if comm is None:
        return pl.pallas_call(
            body, name=name, grid=grid, in_specs=[a_spec, b_spec] + ([o_spec] if has_add else []), out_specs=o_spec,
            out_shape=jax.ShapeDtypeStruct((m, n), out_dtype), scratch_shapes=acc_scratch,
            compiler_params=_params("parallel", "parallel", "arbitrary"),
        )(*((a, b, add) if has_add else (a, b)))
    outs = pl.pallas_call(
        body, name=name, grid=grid, in_specs=[a_spec, b_spec] + ([o_spec] if has_add else []) + [_ANY] * c_in,
        out_specs=[o_spec] + [_ANY] * c_out, out_shape=[jax.ShapeDtypeStruct((m, n), out_dtype)] + list(comm.outs),
        scratch_shapes=acc_scratch + _sem_pairs(comm.n_sems),
        input_output_aliases={n_in + i: 1 + o for i, o in comm.aliases.items()},
        compiler_params=_params("arbitrary", "arbitrary", "arbitrary"),
    )(*((a, b, add) if has_add else (a, b)), *comm.ins)
    return outs[0], list(outs[1:])


def _rms_fwd(name, x, g):
    t, d = x.shape
    tm = _tile(t, 512, 16)

    def body(x_ref, g_ref, o_ref):
        o_ref[...] = _rms(x_ref[...], g_ref[...]).astype(o_ref.dtype)

    return pl.pallas_call(
        body, name=name, grid=(t // tm,),
        in_specs=[pl.BlockSpec((tm, d), lambda i: (i, 0)), pl.BlockSpec((1, d), lambda i: (0, 0))],
        out_specs=pl.BlockSpec((tm, d), lambda i: (i, 0)),
        out_shape=jax.ShapeDtypeStruct((t, d), MXU_DTYPE), compiler_params=_params("parallel"),
    )(x, g.reshape(1, d))


def _rms_bwd(name, x, g, dh, dres=None):
    t, d = x.shape
    tm = _tile(t, 256, 16)
    has_res = dres is not None

    def body(*refs):
        x_ref, g_ref, dh_ref = refs[:3]
        dres_ref = refs[3] if has_res else None
        dx_ref, dxb_ref, dg_ref = refs[-3:]
        _, vjp = jax.vjp(_rms, x_ref[...], g_ref[...])
        dx, dg = vjp(dh_ref[...].astype(F32))
        if has_res:
            dx = dx + dres_ref[...]
        dx_ref[...] = dx
        dxb_ref[...] = dx.astype(dxb_ref.dtype)
        first = pl.program_id(0) == 0

        @pl.when(first)
        def _():
            dg_ref[...] = dg

        @pl.when(jnp.logical_not(first))
        def _():
            dg_ref[...] += dg

    row = pl.BlockSpec((tm, d), lambda i: (i, 0))
    vec = pl.BlockSpec((1, d), lambda i: (0, 0))
    dx, dxb, dg = pl.pallas_call(
        body, name=name, grid=(t // tm,),
        in_specs=[row, vec, row] + ([row] if has_res else []), out_specs=[row, row, vec],
        out_shape=[jax.ShapeDtypeStruct((t, d), F32), jax.ShapeDtypeStruct((t, d), MXU_DTYPE),
                   jax.ShapeDtypeStruct((1, d), F32)],
        compiler_params=_params("arbitrary"),
    )(*((x, g.reshape(1, d), dh) + ((dres,) if has_res else ())))
    return dx, dxb, dg.reshape(d)


def _final_loss(x, g, target):
    t, d = x.shape
    tm = _tile(t, 256, 16)

    def body(x_ref, g_ref, t_ref, loss_ref, dx_ref, dxb_ref, dg_ref):
        y, vjp = jax.vjp(_rms, x_ref[...], g_ref[...])
        err = y - t_ref[...]
        dx, dg = vjp(err * (1.0 / d))
        dx_ref[...] = dx
        dxb_ref[...] = dx.astype(dxb_ref.dtype)
        part = jnp.zeros((1, LANES), F32) + 0.5 * jnp.sum(jnp.mean(err * err, axis=-1, keepdims=True))
        first = pl.program_id(0) == 0

        @pl.when(first)
        def _():
            dg_ref[...] = dg
            loss_ref[...] = part

        @pl.when(jnp.logical_not(first))
        def _():
            dg_ref[...] += dg
            loss_ref[...] += part

    row = pl.BlockSpec((tm, d), lambda i: (i, 0))
    vec = pl.BlockSpec((1, d), lambda i: (0, 0))
    loss, dx, dxb, dg = pl.pallas_call(
        body, name="final_loss", grid=(t // tm,), in_specs=[row, vec, row],
        out_specs=[pl.BlockSpec((1, LANES), lambda i: (0, 0)), row, row, vec],
        out_shape=[jax.ShapeDtypeStruct((1, LANES), F32), jax.ShapeDtypeStruct((t, d), F32),
                   jax.ShapeDtypeStruct((t, d), MXU_DTYPE), jax.ShapeDtypeStruct((1, d), F32)],
        compiler_params=_params("arbitrary"),
    )(x, g.reshape(1, d), target)
    return loss, dx, dxb, dg.reshape(d)


def _conv_taps(x_ext, w, rows):
    kk = w.shape[0]
    y = x_ext[HALO:] * w[kk - 1:kk, :]
    for j in range(kk - 1):
        y = y + pltpu.roll(x_ext, kk - 1 - j, axis=0)[HALO:] * w[j:j + 1, :]
    return y


def _col_specs(tm, tn, col0, t_rows):
    assert col0 % tn == 0 and tm % HALO == 0
    c0 = col0 // tn
    per, last = tm // HALO, t_rows // HALO - 1
    tile = pl.BlockSpec((tm, tn), lambda j, i: (i, c0 + j))
    prev = pl.BlockSpec((HALO, tn), lambda j, i: (jnp.maximum(i * per - 1, 0), c0 + j))
    nxt = pl.BlockSpec((HALO, tn), lambda j, i: (jnp.minimum((i + 1) * per, last), c0 + j))
    return tile, prev, nxt


def _conv_fwd(name, xa, xa_col, w, w_col, ncols, out_dtype, xb=None, xb_col=0, gate=None, gate_col=0):
    t = xa.shape[0]
    kk = w.shape[0]
    tm, tn = _tile(t, 512, HALO), _tile(ncols, 512)
    nrow = t // tm
    has_b, has_g = xb is not None, gate is not None

    def body(*refs):
        refs = list(refs)
        xa_ref, xap_ref = refs.pop(0), refs.pop(0)
        xb_ref, xbp_ref = (refs.pop(0), refs.pop(0)) if has_b else (None, None)
        w_ref = refs.pop(0)
        g_ref = refs.pop(0) if has_g else None
        o_ref = refs.pop(0)
        i = pl.program_id(1)
        x, xp = xa_ref[...].astype(F32), xap_ref[...].astype(F32)
        if has_b:
            x, xp = x * xb_ref[...].astype(F32), xp * xbp_ref[...].astype(F32)
        xp = jnp.where(i == 0, 0.0, xp)
        y = _conv_taps(jnp.concatenate([xp, x], axis=0), w_ref[...], tm)
        if has_g:
            y = y * g_ref[...].astype(F32)
        o_ref[...] = y.astype(o_ref.dtype)

    a_tile, a_prev, _ = _col_specs(tm, tn, xa_col, t)
    ins, specs = [xa, xa], [a_tile, a_prev]
    if has_b:
        b_tile, b_prev, _ = _col_specs(tm, tn, xb_col, t)
        ins, specs = ins + [xb, xb], specs + [b_tile, b_prev]
    assert w_col % tn == 0
    ins, specs = ins + [w], specs + [pl.BlockSpec((kk, tn), lambda j, i: (0, w_col // tn + j))]
    if has_g:
        ins, specs = ins + [gate], specs + [_col_specs(tm, tn, gate_col, t)[0]]
    return pl.pallas_call(
        body, name=name, grid=(ncols // tn, nrow), in_specs=specs,
        out_specs=pl.BlockSpec((tm, tn), lambda j, i: (i, j)),
        out_shape=jax.ShapeDtypeStruct((t, ncols), out_dtype), compiler_params=_params("parallel", "parallel"),
    )(*ins)


def _conv_bwd(name, xa, xa_col, w, w_col, dy, dy_col, ncols, dx_dtype, xb=None, xb_col=0, gate=None, gate_col=0):
    t = xa.shape[0]
    kk = w.shape[0]
    tm, tn = _tile(t, 512, HALO), _tile(ncols, 512)
    nrow = t // tm
    has_b, has_g = xb is not None, gate is not None

    def body(*refs):
        refs = list(refs)
        xa_ref, xap_ref = refs.pop(0), refs.pop(0)
        xb_ref, xbp_ref = (refs.pop(0), refs.pop(0)) if has_b else (None, None)
        w_ref = refs.pop(0)
        dy_ref, dyn_ref = refs.pop(0), refs.pop(0)
        g_ref, gn_ref = (refs.pop(0), refs.pop(0)) if has_g else (None, None)
        dxa_ref = refs.pop(0)
        dxb_ref = refs.pop(0) if has_b else None
        dg_ref = refs.pop(0) if has_g else None
        dw_ref = refs.pop(0)
        i = pl.program_id(1)
        wv = w_ref[...]
        xa_t, xa_p = xa_ref[...].astype(F32), xap_ref[...].astype(F32)
        x, xp = xa_t, xa_p
        if has_b:
            xb_t = xb_ref[...].astype(F32)
            x, xp = x * xb_t, xp * xbp_ref[...].astype(F32)
        xp = jnp.where(i == 0, 0.0, xp)
        x_ext = jnp.concatenate([xp, x], axis=0)
        dyv, dyn = dy_ref[...].astype(F32), dyn_ref[...].astype(F32)
        if has_g:
            dg_ref[...] = (dyv * _conv_taps(x_ext, wv, tm)).astype(dg_ref.dtype)
            dyv, dyn = dyv * g_ref[...].astype(F32), dyn * gn_ref[...].astype(F32)
        dyn = jnp.where(i == nrow - 1, 0.0, dyn)
        dy_ext = jnp.concatenate([dyv, dyn], axis=0)
        dx = dyv * wv[kk - 1:kk, :]
        row8 = lax.broadcasted_iota(jnp.int32, (8, tn), 0)
        dw = jnp.where(row8 == kk - 1, jnp.sum(dyv * x, axis=0, keepdims=True), 0.0)
        for j in range(kk - 1):
            s = kk - 1 - j
            dx = dx + pltpu.roll(dy_ext, tm + HALO - s, axis=0)[:tm] * wv[j:j + 1, :]
            dwj = jnp.sum(dyv * pltpu.roll(x_ext, s, axis=0)[HALO:], axis=0, keepdims=True)
            dw = dw + jnp.where(row8 == j, dwj, 0.0)
        if has_b:
            dxa_ref[...] = (dx * xb_t).astype(dxa_ref.dtype)
            dxb_ref[...] = (dx * xa_t).astype(dxb_ref.dtype)
        else:
            dxa_ref[...] = dx.astype(dxa_ref.dtype)

        @pl.when(i == 0)
        def _():
            dw_ref[...] = dw

        @pl.when(i > 0)
        def _():
            dw_ref[...] += dw

    a_tile, a_prev, _ = _col_specs(tm, tn, xa_col, t)
    ins, specs = [xa, xa], [a_tile, a_prev]
    if has_b:
        b_tile, b_prev, _ = _col_specs(tm, tn, xb_col, t)
        ins, specs = ins + [xb, xb], specs + [b_tile, b_prev]
    assert w_col % tn == 0
    ins, specs = ins + [w], specs + [pl.BlockSpec((kk, tn), lambda j, i: (0, w_col // tn + j))]
    d_tile, _, d_next = _col_specs(tm, tn, dy_col, t)
    ins, specs = ins + [dy, dy], specs + [d_tile, d_next]
    if has_g:
        g_tile, _, g_next = _col_specs(tm, tn, gate_col, t)
        ins, specs = ins + [gate, gate], specs + [g_tile, g_next]
    out_tile = pl.BlockSpec((tm, tn), lambda j, i: (i, j))
    shapes, ospecs = [jax.ShapeDtypeStruct((t, ncols), dx_dtype)], [out_tile]
    if has_b:
        shapes, ospecs = shapes + [jax.ShapeDtypeStruct((t, ncols), dx_dtype)], ospecs + [out_tile]
    if has_g:
        shapes, ospecs = shapes + [jax.ShapeDtypeStruct((t, ncols), dx_dtype)], ospecs + [out_tile]
    shapes, ospecs = shapes + [jax.ShapeDtypeStruct((8, ncols), F32)], ospecs + [pl.BlockSpec((8, tn), lambda j, i: (0, j))]
    outs = list(pl.pallas_call(
        body, name=name, grid=(ncols // tn, nrow), in_specs=specs, out_specs=ospecs, out_shape=shapes,
        compiler_params=_params("parallel", "arbitrary"),
    )(*ins))
    dxa = outs.pop(0)
    dxb = outs.pop(0) if has_b else None
    dgate = outs.pop(0) if has_g else None
    return dxa, dxb, dgate, outs.pop(0)[:kk]


def _swiglu_fwd(u):
    t, f2 = u.shape
    f = f2 // 2
    tm, tn = _tile(t, 512, 16), _tile(f, 512)
    nf = f // tn

    def body(g_ref, u_ref, o_ref):
        o_ref[...] = (_silu(g_ref[...].astype(F32)) * u_ref[...].astype(F32)).astype(o_ref.dtype)

    return pl.pallas_call(
        body, name="swiglu_fwd", grid=(t // tm, nf),
        in_specs=[pl.BlockSpec((tm, tn), lambda i, j: (i, j)), pl.BlockSpec((tm, tn), lambda i, j: (i, nf + j))],
        out_specs=pl.BlockSpec((tm, tn), lambda i, j: (i, j)),
        out_shape=jax.ShapeDtypeStruct((t, f), MXU_DTYPE), compiler_params=_params("parallel", "parallel"),
    )(u, u)


def _swiglu_bwd(u, da):
    t, f2 = u.shape
    f = f2 // 2
    tm, tn = _tile(t, 512, 16), _tile(f, 512)
    nf = f // tn

    def body(g_ref, u_ref, da_ref, o_ref):
        g, d = g_ref[...].astype(F32), da_ref[...].astype(F32)
        sg = _sigmoid(g)
        gate_half = pl.program_id(1) < nf

        @pl.when(gate_half)
        def _():
            o_ref[...] = (d * u_ref[...].astype(F32) * (sg * (1.0 + g * (1.0 - sg)))).astype(o_ref.dtype)

        @pl.when(jnp.logical_not(gate_half))
        def _():
            o_ref[...] = (d * (g * sg)).astype(o_ref.dtype)

    return pl.pallas_call(
        body, name="swiglu_bwd", grid=(t // tm, 2 * nf),
        in_specs=[pl.BlockSpec((tm, tn), lambda i, j: (i, j % nf)),
                  pl.BlockSpec((tm, tn), lambda i, j: (i, nf + j % nf)),
                  pl.BlockSpec((tm, tn), lambda i, j: (i, j % nf))],
        out_specs=pl.BlockSpec((tm, tn), lambda i, j: (i, j)),
        out_shape=jax.ShapeDtypeStruct((t, f2), MXU_DTYPE), compiler_params=_params("parallel", "parallel"),
    )(u, u, da)


def _gdn_prep(ops, qc, kc, vc, b_col, a_col, a_log, dt_bias):
    c, dh = qc.shape[-2:]
    q, k, v = _silu(qc), _silu(kc), _silu(vc)
    q = q * lax.rsqrt(jnp.sum(q * q, axis=-1, keepdims=True) + EPS) * (dh ** -0.5)
    k = k * lax.rsqrt(jnp.sum(k * k, axis=-1, keepdims=True) + EPS)
    beta = _sigmoid(b_col)
    g_col = -jnp.exp(a_log) * _softplus(a_col + dt_bias)
    r = lax.broadcasted_iota(jnp.int32, (c, c), 0)
    s = lax.broadcasted_iota(jnp.int32, (c, c), 1)
    g_row = jnp.sum(jnp.where(r == s, g_col, 0.0), axis=-2, keepdims=True)
    gc_col = jnp.sum(jnp.where(s <= r, g_row, 0.0), axis=-1, keepdims=True)
    gc_row = jnp.sum(jnp.where(r <= s, g_col, 0.0), axis=-2, keepdims=True)
    decay = jnp.exp(jnp.where(s <= r, gc_col - gc_row, -1e30))
    kb = k * beta
    a = jnp.where(s < r, ops.mm(kb, k, "nt") * decay, 0.0)
    tinv = ops.tri_inv(a)
    e_col = jnp.exp(gc_col)
    u = ops.mmh(tinv, v * beta)
    w = ops.mmh(tinv, kb * e_col)
    attn = ops.mm(q, k, "nt") * decay
    g_last = jnp.sum(g_col, axis=-2, keepdims=True)
    return u, w, attn, q * e_col, k * jnp.exp(g_last - gc_col), g_last


def _gdn_step(ops, state, u, w, attn, q_dec, k_dec, g_last):
    v_new = u - ops.mm(w, state)
    o = ops.mm(q_dec, state) + ops.mm(attn, v_new)
    return o, state * jnp.exp(g_last) + ops.mm(k_dec, v_new, "tn")


PREP_HEADS, SCAN_HEADS = 2, 8


def _gdn_blocks(t, heads, hb_pref):
    tc = _tile(t, 256, CHUNK)
    hb = max(h for h in range(1, hb_pref + 1) if heads % h == 0)
    return tc, hb


def _to_chunks(ref, hb, dh):
    tc = ref.shape[0]
    return jnp.concatenate([ref[:, h * dh:(h + 1) * dh].astype(F32).reshape(tc // CHUNK, CHUNK, dh)
                            for h in range(hb)], axis=0)


def _from_chunks(ref, val, hb, dh):
    tc = ref.shape[0]
    ncb = tc // CHUNK
    for h in range(hb):
        ref[:, h * dh:(h + 1) * dh] = val[h * ncb:(h + 1) * ncb].reshape(tc, dh).astype(ref.dtype)


def _per_chunk(s, ncb):
    hb = s.shape[0]
    return jnp.broadcast_to(s[:, None], (hb, ncb, 1, 1)).reshape(hb * ncb, 1, 1)


def _gate_columns(pba, first_head, hb, heads):
    tc = pba.shape[0]
    lane = lax.broadcasted_iota(jnp.int32, pba.shape, 1)
    pick = lambda k: jnp.sum(jnp.where(lane == k, pba, 0.0), axis=1, keepdims=True).reshape(tc // CHUNK, CHUNK, 1)
    return (jnp.concatenate([pick(first_head + h) for h in range(hb)], axis=0),
            jnp.concatenate([pick(heads + first_head + h) for h in range(hb)], axis=0))


def _gdn_prep_fwd(qkv, pba, a_log, dt_bias, heads, dh):
    t = qkv.shape[0]
    tc, hb = _gdn_blocks(t, heads, PREP_HEADS)
    ncb, nhb, width = tc // CHUNK, heads // hb, heads * dh
    nc = t // CHUNK

    def body(q_ref, k_ref, v_ref, g_ref, al_ref, dt_ref, u_ref, w_ref, p_ref, qd_ref, kd_ref, gl_ref):
        b_col, a_col = _gate_columns(g_ref[...], pl.program_id(1) * hb, hb, heads)
        u, w, p, qd, kd, gl = _gdn_prep(
            _PLAIN, _to_chunks(q_ref, hb, dh), _to_chunks(k_ref, hb, dh), _to_chunks(v_ref, hb, dh), b_col, a_col,
            _per_chunk(al_ref[...], ncb), _per_chunk(dt_ref[...], ncb))
        _from_chunks(u_ref, u, hb, dh)
        _from_chunks(w_ref, w, hb, dh)
        _from_chunks(qd_ref, qd, hb, dh)
        _from_chunks(kd_ref, kd, hb, dh)
        p_ref[...] = p.reshape(hb, tc, CHUNK).astype(p_ref.dtype)
        gl_ref[...] = gl.reshape(hb, ncb, 1, 1)

    def tok(off):
        return pl.BlockSpec((tc, hb * dh), lambda i, j: (i, off * nhb + j))

    gate = pl.BlockSpec((tc, LANES), lambda i, j: (i, 0))
    scal = pl.BlockSpec((hb, 1, 1), lambda i, j: (j, 0, 0))
    return pl.pallas_call(
        body, name="gdn_prep_fwd", grid=(t // tc, nhb),
        in_specs=[tok(0), tok(1), tok(2), gate, scal, scal],
        out_specs=[tok(0), tok(0), pl.BlockSpec((hb, tc, CHUNK), lambda i, j: (j, i, 0)), tok(0), tok(0),
                   pl.BlockSpec((hb, ncb, 1, 1), lambda i, j: (j, i, 0, 0))],
        out_shape=[jax.ShapeDtypeStruct((t, width), F32), jax.ShapeDtypeStruct((t, width), MXU_DTYPE),
                   jax.ShapeDtypeStruct((heads, t, CHUNK), MXU_DTYPE), jax.ShapeDtypeStruct((t, width), MXU_DTYPE),
                   jax.ShapeDtypeStruct((t, width), MXU_DTYPE), jax.ShapeDtypeStruct((heads, nc, 1, 1), F32)],
        compiler_params=_params("parallel", "parallel"),
    )(qkv, qkv, qkv, pba, a_log, dt_bias)


def _gdn_prep_bwd(qkv, pba, a_log, dt_bias, du, dw, dp, dqd, dkd, dgl, heads, dh):
    t = qkv.shape[0]
    tc, hb = _gdn_blocks(t, heads, PREP_HEADS)
    ncb, nhb, width = tc // CHUNK, heads // hb, heads * dh

    def body(q_ref, k_ref, v_ref, g_ref, al_ref, dt_ref, du_ref, dw_ref, dp_ref, dqd_ref, dkd_ref, dgl_ref,
             dq_ref, dk_ref, dv_ref, dg_ref, dal_ref, ddt_ref):
        first_head = pl.program_id(1) * hb
        b_col, a_col = _gate_columns(g_ref[...], first_head, hb, heads)

        def prep(q, k, v, b, a, al, dt):
            return _gdn_prep(_DIFF, q, k, v, b, a, _per_chunk(al, ncb), _per_chunk(dt, ncb))

        _, vjp = jax.vjp(prep, _to_chunks(q_ref, hb, dh), _to_chunks(k_ref, hb, dh), _to_chunks(v_ref, hb, dh),
                         b_col, a_col, al_ref[...], dt_ref[...])
        dq, dk, dv, db, da, dal, ddt = vjp((
            _to_chunks(du_ref, hb, dh), _to_chunks(dw_ref, hb, dh), dp_ref[...].reshape(hb * ncb, CHUNK, CHUNK),
            _to_chunks(dqd_ref, hb, dh), _to_chunks(dkd_ref, hb, dh), dgl_ref[...].reshape(hb * ncb, 1, 1)))
        _from_chunks(dq_ref, dq, hb, dh)
        _from_chunks(dk_ref, dk, hb, dh)
        _from_chunks(dv_ref, dv, hb, dh)
        dal_ref[...] = dal[None]
        ddt_ref[...] = ddt[None]
        lane = lax.broadcasted_iota(jnp.int32, (tc, LANES), 1)
        dgates = jnp.zeros((tc, LANES), F32)
        for h in range(hb):
            rows = slice(h * ncb, (h + 1) * ncb)
            dgates = dgates + jnp.where(lane == first_head + h, db[rows].reshape(tc, 1), 0.0) \
                + jnp.where(lane == heads + first_head + h, da[rows].reshape(tc, 1), 0.0)

        @pl.when(first_head == 0)
        def _():
            dg_ref[...] = dgates

        @pl.when(first_head > 0)
        def _():
            dg_ref[...] += dgates

    def tok(off):
        return pl.BlockSpec((tc, hb * dh), lambda i, j: (i, off * nhb + j))

    gate = pl.BlockSpec((tc, LANES), lambda i, j: (i, 0))
    scal = pl.BlockSpec((hb, 1, 1), lambda i, j: (j, 0, 0))
    part = pl.BlockSpec((1, hb, 1, 1), lambda i, j: (i, j, 0, 0))
    pspec = pl.BlockSpec((hb, tc, CHUNK), lambda i, j: (j, i, 0))
    glspec = pl.BlockSpec((hb, ncb, 1, 1), lambda i, j: (j, i, 0, 0))
    tokf = jax.ShapeDtypeStruct((t, width), F32)
    partf = jax.ShapeDtypeStruct((t // tc, heads, 1, 1), F32)
    return pl.pallas_call(
        body, name="gdn_prep_bwd", grid=(t // tc, nhb),
        in_specs=[tok(0), tok(1), tok(2), gate, scal, scal, tok(0), tok(0), pspec, tok(0), tok(0), glspec],
        out_specs=[tok(0), tok(0), tok(0), gate, part, part],
        out_shape=[tokf, tokf, tokf, jax.ShapeDtypeStruct((t, LANES), F32), partf, partf],
        compiler_params=_params("parallel", "arbitrary"),
    )(qkv, qkv, qkv, pba, a_log, dt_bias, du, dw, dp, dqd, dkd, dgl)


def _heads(ref, rows, hb, dh):
    return jnp.stack([ref[rows, h * dh:(h + 1) * dh].astype(F32) for h in range(hb)])


def _put_heads(ref, rows, val, dh):
    for h in range(val.shape[0]):
        ref[rows, h * dh:(h + 1) * dh] = val[h].astype(ref.dtype)


def _gdn_scan_fwd(u, w, p, qd, kd, gl, heads, dh):
    t = u.shape[0]
    tc, hb = _gdn_blocks(t, heads, SCAN_HEADS)
    ncb, nhb = tc // CHUNK, heads // hb
    nc = t // CHUNK

    def body(u_ref, w_ref, p_ref, qd_ref, kd_ref, gl_ref, o_ref, s_ref, state):
        @pl.when(pl.program_id(1) == 0)
        def _():
            state[...] = jnp.zeros_like(state)

        for c in range(ncb):
            rs = slice(c * CHUNK, (c + 1) * CHUNK)
            s_in = state[...]
            s_ref[:, c] = s_in
            o, s_out = _gdn_step(_PLAIN, s_in, _heads(u_ref, rs, hb, dh), _heads(w_ref, rs, hb, dh), p_ref[:, rs, :],
                                 _heads(qd_ref, rs, hb, dh), _heads(kd_ref, rs, hb, dh), gl_ref[:, c])
            _put_heads(o_ref, rs, o, dh)
            state[...] = s_out

    tok = pl.BlockSpec((tc, hb * dh), lambda j, i: (i, j))
    pspec = pl.BlockSpec((hb, tc, CHUNK), lambda j, i: (j, i, 0))
    glspec = pl.BlockSpec((hb, ncb, 1, 1), lambda j, i: (j, i, 0, 0))
    return pl.pallas_call(
        body, name="gdn_scan_fwd", grid=(nhb, t // tc),
        in_specs=[tok, tok, pspec, tok, tok, glspec],
        out_specs=[tok, pl.BlockSpec((hb, ncb, dh, dh), lambda j, i: (j, i, 0, 0))],
        out_shape=[jax.ShapeDtypeStruct((t, heads * dh), F32), jax.ShapeDtypeStruct((heads, nc, dh, dh), F32)],
        scratch_shapes=[pltpu.VMEM((hb, dh, dh), F32)],
        compiler_params=_params("arbitrary", "arbitrary"),
    )(u, w, p, qd, kd, gl)


def _gdn_scan_bwd(u, w, p, qd, kd, gl, states, do, heads, dh):
    t = u.shape[0]
    tc, hb = _gdn_blocks(t, heads, SCAN_HEADS)
    ncb, nhb = tc // CHUNK, heads // hb
    nc, nt = t // CHUNK, t // tc

    def body(u_ref, w_ref, p_ref, qd_ref, kd_ref, gl_ref, s_ref, do_ref,
             du_ref, dw_ref, dp_ref, dqd_ref, dkd_ref, dgl_ref, dstate):
        @pl.when(pl.program_id(1) == 0)
        def _():
            dstate[...] = jnp.zeros_like(dstate)

        for c in reversed(range(ncb)):
            rs = slice(c * CHUNK, (c + 1) * CHUNK)
            _, vjp = jax.vjp(functools.partial(_gdn_step, _DIFF), s_ref[:, c], _heads(u_ref, rs, hb, dh),
                             _heads(w_ref, rs, hb, dh), p_ref[:, rs, :].astype(F32), _heads(qd_ref, rs, hb, dh),
                             _heads(kd_ref, rs, hb, dh), gl_ref[:, c])
            ds, du, dw, dp, dqd, dkd, dgl = vjp((_heads(do_ref, rs, hb, dh), dstate[...]))
            dstate[...] = ds
            _put_heads(du_ref, rs, du, dh)
            _put_heads(dw_ref, rs, dw, dh)
            _put_heads(dqd_ref, rs, dqd, dh)
            _put_heads(dkd_ref, rs, dkd, dh)
            dp_ref[:, rs, :] = dp
            dgl_ref[:, c] = dgl

    tok = pl.BlockSpec((tc, hb * dh), lambda j, i: (nt - 1 - i, j))
    pspec = pl.BlockSpec((hb, tc, CHUNK), lambda j, i: (j, nt - 1 - i, 0))
    glspec = pl.BlockSpec((hb, ncb, 1, 1), lambda j, i: (j, nt - 1 - i, 0, 0))
    sspec = pl.BlockSpec((hb, ncb, dh, dh), lambda j, i: (j, nt - 1 - i, 0, 0))
    tokf = jax.ShapeDtypeStruct((t, heads * dh), F32)
    return pl.pallas_call(
        body, name="gdn_scan_bwd", grid=(nhb, nt),
        in_specs=[tok, tok, pspec, tok, tok, glspec, sspec, tok],
        out_specs=[tok, tok, pspec, tok, tok, glspec],
        out_shape=[tokf, tokf, jax.ShapeDtypeStruct((heads, t, CHUNK), F32), tokf, tokf,
                   jax.ShapeDtypeStruct((heads, nc, 1, 1), F32)],
        scratch_shapes=[pltpu.VMEM((hb, dh, dh), F32)],
        compiler_params=_params("arbitrary", "arbitrary"),
    )(u, w, p, qd, kd, gl, states, do)


def _gdn_post(o, z, gain):
    return _rms(o, gain) * _silu(z)


def _gdn_post_fwd(o, pm, z_col, gain, heads, dh):
    t = o.shape[0]
    tm = _tile(t, 512, 16)
    z0 = z_col // dh

    def body(o_ref, z_ref, g_ref, y_ref):
        y_ref[...] = _gdn_post(o_ref[...], z_ref[...], g_ref[...]).astype(y_ref.dtype)

    return pl.pallas_call(
        body, name="gdn_post_fwd", grid=(t // tm, heads),
        in_specs=[pl.BlockSpec((tm, dh), lambda i, h: (i, h)), pl.BlockSpec((tm, dh), lambda i, h: (i, z0 + h)),
                  pl.BlockSpec((1, dh), lambda i, h: (0, 0))],
        out_specs=pl.BlockSpec((tm, dh), lambda i, h: (i, h)),
        out_shape=jax.ShapeDtypeStruct((t, heads * dh), MXU_DTYPE), compiler_params=_params("parallel", "parallel"),
    )(o, pm, gain.reshape(1, dh))


def _gdn_post_bwd(o, pm, z_col, gain, dy, heads, dh):
    t = o.shape[0]
    tm = _tile(t, 512, 16)
    z0 = z_col // dh

    def body(o_ref, z_ref, g_ref, dy_ref, do_ref, dz_ref, dg_ref):
        _, vjp = jax.vjp(_gdn_post, o_ref[...], z_ref[...], g_ref[...])
        do, dz, dg = vjp(dy_ref[...])
        do_ref[...] = do
        dz_ref[...] = dz.astype(dz_ref.dtype)
        first = jnp.logical_and(pl.program_id(0) == 0, pl.program_id(1) == 0)

        @pl.when(first)
        def _():
            dg_ref[...] = dg

        @pl.when(jnp.logical_not(first))
        def _():
            dg_ref[...] += dg

    blk = pl.BlockSpec((tm, dh), lambda i, h: (i, h))
    vec = pl.BlockSpec((1, dh), lambda i, h: (0, 0))
    do, dz, dg = pl.pallas_call(
        body, name="gdn_post_bwd", grid=(t // tm, heads),
        in_specs=[blk, pl.BlockSpec((tm, dh), lambda i, h: (i, z0 + h)), vec, blk],
        out_specs=[blk, blk, vec],
        out_shape=[jax.ShapeDtypeStruct((t, heads * dh), F32), jax.ShapeDtypeStruct((t, heads * dh), MXU_DTYPE),
                   jax.ShapeDtypeStruct((1, dh), F32)],
        compiler_params=_params("arbitrary", "arbitrary"),
    )(o, pm, gain.reshape(1, dh), dy)
    return do, dz, dg.reshape(dh)


def _attn(ops, q, kv):
    d = q.shape[1]
    hd = d // XATTN_HEADS
    outs = []
    for h in range(XATTN_HEADS):
        qh, kh, vh = q[:, h * hd:(h + 1) * hd], kv[:, h * hd:(h + 1) * hd], kv[:, d + h * hd:d + (h + 1) * hd]
        s = ops.mm(qh, kh, "nt") * (hd ** -0.5)
        e = jnp.exp(s - lax.stop_gradient(jnp.max(s, axis=-1, keepdims=True)))
        outs.append(ops.mm(e / jnp.sum(e, axis=-1, keepdims=True), vh))
    return jnp.concatenate(outs, axis=1)


def _attn_fwd(q, kv):
    t, d = q.shape
    nm = kv.shape[0]
    tm = _tile(t, 512, 16)

    def body(q_ref, kv_ref, o_ref):
        o_ref[...] = _attn(_PLAIN, q_ref[...], kv_ref[...]).astype(o_ref.dtype)

    return pl.pallas_call(
        body, name="xattn_fwd", grid=(t // tm,),
        in_specs=[pl.BlockSpec((tm, d), lambda i: (i, 0)), pl.BlockSpec((nm, 2 * d), lambda i: (0, 0))],
        out_specs=pl.BlockSpec((tm, d), lambda i: (i, 0)),
        out_shape=jax.ShapeDtypeStruct((t, d), MXU_DTYPE), compiler_params=_params("parallel"),
    )(q, kv)


def _attn_bwd(q, kv, do):
    t, d = q.shape
    nm = kv.shape[0]
    tm = _tile(t, 256, 16)

    def body(q_ref, kv_ref, do_ref, dq_ref, dkv_ref):
        _, vjp = jax.vjp(functools.partial(_attn, _DIFF), q_ref[...].astype(F32), kv_ref[...].astype(F32))
        dq, dkv = vjp(do_ref[...].astype(F32))
        dq_ref[...] = dq.astype(dq_ref.dtype)
        first = pl.program_id(0) == 0

        @pl.when(first)
        def _():
            dkv_ref[...] = dkv

        @pl.when(jnp.logical_not(first))
        def _():
            dkv_ref[...] += dkv

    row = pl.BlockSpec((tm, d), lambda i: (i, 0))
    full = pl.BlockSpec((nm, 2 * d), lambda i: (0, 0))
    return pl.pallas_call(
        body, name="xattn_bwd", grid=(t // tm,), in_specs=[row, full, row], out_specs=[row, full],
        out_shape=[jax.ShapeDtypeStruct((t, d), MXU_DTYPE), jax.ShapeDtypeStruct((nm, 2 * d), F32)],
        compiler_params=_params("arbitrary"),
    )(q, kv, do)


def _adamw(name, w, g, m, v):
    shape = w.shape
    cols = shape[-1]
    rows = w.size // cols
    w2, g2, m2, v2 = (a.reshape(rows, cols) for a in (w, g, m, v))
    tr = _tile(rows, max(8, (1 << 18) // cols // 8 * 8), 8)

    def body(w_ref, g_ref, m_ref, v_ref, d_ref, nm_ref, nv_ref):
        gv = g_ref[...]
        nm = ADAM_B1 * m_ref[...] + (1.0 - ADAM_B1) * gv
        nv = ADAM_B2 * v_ref[...] + (1.0 - ADAM_B2) * jnp.square(gv)
        m_hat = nm / (1.0 - ADAM_B1 ** ADAM_STEP)
        v_hat = nv / (1.0 - ADAM_B2 ** ADAM_STEP)
        d_ref[...] = -ADAM_LR * (m_hat / (jnp.sqrt(v_hat) + ADAM_EPS) + ADAM_WD * w_ref[...])
        nm_ref[...] = nm
        nv_ref[...] = nv

    blk = pl.BlockSpec((tr, cols), lambda i: (i, 0))
    out = jax.ShapeDtypeStruct((rows, cols), F32)
    d, nm, nv = pl.pallas_call(
        body, name=name, grid=(rows // tr,), in_specs=[blk] * 4, out_specs=[blk] * 3, out_shape=[out] * 3,
        compiler_params=_params("parallel"),
    )(w2, g2, m2, v2)
    return d.reshape(shape), nm.reshape(shape), nv.reshape(shape)


def _carried(carry, landed):
    def mm(name, *args, **kwargs):
        if name not in carry:
            return _matmul(name, *args, **kwargs)
        out, landed[name] = _matmul(name, *args, comm=carry[name], **kwargs)
        return out

    return mm


def _layer_fwd(x, mem, p, heads, dh, carry):
    wid = heads * dh
    sc = x.shape[1] - wid
    s, landed = {"x0": x}, {}
    mm = _carried(carry, landed)
    s["h1"] = _rms_fwd("rms_mix", x, p["mix_norm"])
    s["pm"] = pm = mm("mm_mix_in", s["h1"], p["wmain"], "nn", F32)
    s["pba"] = mm("mm_mix_ba", s["h1"], p["wba"], "nn", F32)
    s["qkv"] = _conv_fwd("conv_gdn", pm, 0, p["gdn_conv"], 0, 3 * wid, F32)
    s["prep"] = _gdn_prep_fwd(s["qkv"], s["pba"], p["a_log"], p["dt_bias"], heads, dh)
    s["o"], s["states"] = _gdn_scan_fwd(*s["prep"], heads, dh)
    y_gdn = _gdn_post_fwd(s["o"], pm, 3 * wid, p["gdn_out_norm"], heads, dh)
    y_sc = _conv_fwd("conv_sc", pm, 4 * wid + sc, p["sc_conv"], 0, sc, MXU_DTYPE, xb=pm, xb_col=4 * wid + 2 * sc,
                     gate=pm, gate_col=4 * wid)
    s["ycat"] = jnp.concatenate([y_gdn, y_sc], axis=1)
    s["x1"] = x1 = mm("mm_mix_out", s["ycat"], p["wout"], "nn", F32, add=x)
    s["h2"] = _rms_fwd("rms_xattn", x1, p["xattn_norm"])
    s["q"] = mm("mm_xq", s["h2"], p["wq"], "nn", MXU_DTYPE)
    s["memn"] = _rms_fwd("rms_mem", mem, p["mem_norm"])
    s["kv"] = mm("mm_xkv", s["memn"], p["wkv"], "nn", MXU_DTYPE)
    s["ao"] = _attn_fwd(s["q"], s["kv"])
    s["x2"] = x2 = mm("mm_xo", s["ao"], p["wo"], "nn", F32, add=x1)
    s["h3"] = _rms_fwd("rms_ffn", x2, p["ffn_norm"])
    s["upre"] = mm("mm_ffn_up", s["h3"], p["wup"], "nn", MXU_DTYPE)
    s["uc"] = _conv_fwd("conv_ffn", s["upre"], 0, p["ffn_conv"], 0, s["upre"].shape[1], MXU_DTYPE)
    s["act"] = _swiglu_fwd(s["uc"])
    return mm("mm_ffn_down", s["act"], p["wdown"], "nn", F32, add=x2), s, landed


def _layer_bwd(dx3, dx3b, mem, s, p, heads, dh, reduce):
    wid = heads * dh
    sc = dx3.shape[1] - wid
    pm = s["pm"]
    g = {}

    mm = reduce.carried if reduce is not None else _matmul
    da = mm("mm_ffn_down_dx", dx3b, p["wdown"], "nt", MXU_DTYPE)
    g["wdown"] = mm("mm_ffn_down_dw", s["act"], dx3b, "tn", WIRE_DTYPE)
    du = _swiglu_bwd(s["uc"], da)
    dupre, _, _, g["ffn_conv"] = _conv_bwd("conv_ffn_bwd", s["upre"], 0, p["ffn_conv"], 0, du, 0, du.shape[1],
                                           MXU_DTYPE)
    dh3 = mm("mm_ffn_up_dx", dupre, p["wup"], "nt", F32)
    g["wup"] = mm("mm_ffn_up_dw", s["h3"], dupre, "tn", WIRE_DTYPE)
    dx2, dx2b, g["ffn_norm"] = _rms_bwd("rms_ffn_bwd", s["x2"], p["ffn_norm"], dh3, dx3)
    dao = mm("mm_xo_dx", dx2b, p["wo"], "nt", MXU_DTYPE)
    g["wo"] = mm("mm_xo_dw", s["ao"], dx2b, "tn", WIRE_DTYPE)
    dq, dkv = _attn_bwd(s["q"], s["kv"], dao)
    dh2 = mm("mm_xq_dx", dq, p["wq"], "nt", F32)
    g["wq"] = mm("mm_xq_dw", s["h2"], dq, "tn", WIRE_DTYPE)
    dmemn = mm("mm_xkv_dx", dkv, p["wkv"], "nt", F32)
    g["wkv"] = mm("mm_xkv_dw", s["memn"], dkv, "tn", WIRE_DTYPE)
    _, _, g["mem_norm"] = _rms_bwd("rms_mem_bwd", mem, p["mem_norm"], dmemn)
    dx1, dx1b, g["xattn_norm"] = _rms_bwd("rms_xattn_bwd", s["x1"], p["xattn_norm"], dh2, dx2)
    dycat = mm("mm_mix_out_dx", dx1b, p["wout"], "nt", F32)
    g["wout"] = mm("mm_mix_out_dw", s["ycat"], dx1b, "tn", WIRE_DTYPE)
    d_c, d_h, d_b, g["sc_conv"] = _conv_bwd("conv_sc_bwd", pm, 4 * wid + sc, p["sc_conv"], 0, dycat, wid, sc,
                                             MXU_DTYPE, xb=pm, xb_col=4 * wid + 2 * sc, gate=pm, gate_col=4 * wid)
    do, dz, g["gdn_out_norm"] = _gdn_post_bwd(s["o"], pm, 3 * wid, p["gdn_out_norm"], dycat, heads, dh)
    dprep = _gdn_scan_bwd(*s["prep"], s["states"], do, heads, dh)
    dqc, dkc, dvc, dpba, dal, ddt = _gdn_prep_bwd(s["qkv"], s["pba"], p["a_log"], p["dt_bias"], *dprep, heads, dh)
    g["a_log"], g["dt_bias"] = jnp.sum(dal, axis=0), jnp.sum(ddt, axis=0)
    dqkv, _, _, g["gdn_conv"] = _conv_bwd("conv_gdn_bwd", pm, 0, p["gdn_conv"], 0,
                                          jnp.concatenate([dqc, dkc, dvc], axis=1), 0, 3 * wid, MXU_DTYPE)
    dpm = jnp.concatenate([dqkv, dz, d_b, d_c, d_h], axis=1)
    dpba = dpba.astype(MXU_DTYPE)
    dh1 = mm("mm_mix_in_dx", dpm, p["wmain"], "nt", F32)
    dh1 = mm("mm_mix_ba_dx", dpba, p["wba"], "nt", F32, add=dh1)
    g["wmain"] = mm("mm_mix_in_dw", s["h1"], dpm, "tn", WIRE_DTYPE)
    g["wba"] = mm("mm_mix_ba_dw", s["h1"], dpba, "tn", WIRE_DTYPE)
    dx0, dx0b, g["mix_norm"] = _rms_bwd("rms_mix_bwd", s["x0"], p["mix_norm"], dh1, dx1)
    return dx0, dx0b, g


def _layer_params(win, wout, wq, wk, wv, wo, wup, wdown, small, heads, dh):
    wid = heads * dh
    p = dict(small)
    p.update({
        "a_log": small["a_log"].reshape(heads, 1, 1), "dt_bias": small["dt_bias"].reshape(heads, 1, 1),
        "wmain": jnp.concatenate([win[:, :4 * wid], win[:, 4 * wid + 2 * heads:]], axis=1),
        "wba": jnp.pad(win[:, 4 * wid:4 * wid + 2 * heads], ((0, 0), (0, LANES - 2 * heads))),
        "wout": wout, "wq": wq, "wkv": jnp.concatenate([wk, wv], axis=1), "wo": wo, "wup": wup, "wdown": wdown,
    })
    return p


_ANY =pl.BlockSpec(memory_space=pl.ANY)
_VMEM = pl.BlockSpec(memory_space=pltpu.VMEM)


def _mesh_pos():
    return lax.axis_index("x"), lax.axis_index("y"), lax.axis_index("c")


def _other_chips(x, y):
    return [(1 - x, y), (x, 1 - y), (1 - x, 1 - y)]


def _push(src, dst, sems, k, to):
    return pltpu.make_async_remote_copy(src_ref=src, dst_ref=dst, send_sem=sems[0].at[k], recv_sem=sems[1].at[k],
                                        device_id=to, device_id_type=MESH)


def _sem_pairs(n):
    return [pltpu.SemaphoreType.DMA((n,)), pltpu.SemaphoreType.DMA((n,))]


class _Comm:
    def __init__(self, ins, outs, n_sems, start, finish, aliases=None):
        self.ins, self.outs, self.n_sems, self.start, self.finish = list(ins), list(outs), n_sems, start, finish
        self.aliases = aliases or {}


def _run_comm(name, comm):
    n_in, n_out = len(comm.ins), len(comm.outs)

    def body(*refs):
        parts = (refs[:n_in], refs[n_in:n_in + n_out], refs[n_in + n_out:])
        comm.start(*parts)
        comm.finish(*parts)

    return pl.pallas_call(
        body, name=name, in_specs=[_ANY] * n_in, out_specs=[_ANY] * n_out, out_shape=comm.outs,
        scratch_shapes=_sem_pairs(comm.n_sems), input_output_aliases=comm.aliases,
    )(*comm.ins)


def _allgather_comm(srcs):
    n = len(srcs)

    def first(src, out, sems):
        x, y, c = _mesh_pos()
        own, sends = [], []
        for t in range(n):
            half = src[t].shape[0] // 2
            mine = pl.ds(c * half, half)
            own.append(_push(src[t], out[t].at[2 * x + y], sems, 7 * t + 6, (x, y, 1 - c)))
            sends += [_push(src[t].at[mine], out[t].at[2 * x + y, mine], sems, 7 * t + k, (cx, cy, c))
                      for k, (cx, cy) in enumerate(_other_chips(x, y))]
        return own, sends

    def start(src, out, sems):
        own, sends = first(src, out, sems)
        for cp in own + sends:
            cp.start()

    def finish(src, out, sems):
        x, y, c = _mesh_pos()
        sibling = (x, y, 1 - c)
        own, sends = first(src, out, sems)
        fwds, relayed = [], []
        for t in range(n):
            half = src[t].shape[0] // 2
            for k, (cx, cy) in enumerate(_other_chips(x, y)):
                here = out[t].at[2 * cx + cy, pl.ds(c * half, half)]
                there = out[t].at[2 * cx + cy, pl.ds((1 - c) * half, half)]
                _push(here, here, sems, 7 * t + k, sibling).wait_recv()
                fwds.append(_push(here, here, sems, 7 * t + 3 + k, sibling))
                fwds[-1].start()
                relayed.append(_push(there, there, sems, 7 * t + 3 + k, sibling))
        for cp in relayed + own:
            cp.wait_recv()
        for cp in own + sends + fwds:
            cp.wait_send()

    return _Comm(srcs, [jax.ShapeDtypeStruct((N_CHIPS,) + s.shape, s.dtype) for s in srcs], 7 * n, start, finish)


def _start_wait(build):
    def start(src, out, sems):
        for cp in build(src, out, sems):
            cp.start()

    def finish(src, out, sems):
        for cp in build(src, out, sems):
            cp.wait()

    return start, finish


def _sibling_exchange_comm(bufs):
    def build(src, out, sems):
        x, y, c = _mesh_pos()
        return [_push(src[t].at[1 - c], out[t], sems, t, (x, y, 1 - c)) for t in range(len(bufs))]

    start, finish = _start_wait(build)
    return _Comm(bufs, [jax.ShapeDtypeStruct(b.shape[1:], b.dtype) for b in bufs], len(bufs), start, finish)


def _chip_exchange_comm(bufs):
    def build(src, out, sems):
        x, y, c = _mesh_pos()
        return [_push(src[t].at[2 * cx + cy], out[t].at[k], sems, 3 * t + k, (cx, cy, c))
                for t in range(len(bufs)) for k, (cx, cy) in enumerate(_other_chips(x, y))]

    start, finish = _start_wait(build)
    return _Comm(bufs, [jax.ShapeDtypeStruct((3,) + b.shape[1:], b.dtype) for b in bufs], 3 * len(bufs), start, finish)


def _sibling_share_comm(bufs):
    def build(src, out, sems):
        x, y, c = _mesh_pos()
        return [_push(src[t].at[c], out[t].at[c], sems, t, (x, y, 1 - c)) for t in range(len(bufs))]

    start, finish = _start_wait(build)
    return _Comm(bufs, [jax.ShapeDtypeStruct(b.shape, b.dtype) for b in bufs], len(bufs), start, finish,
                 aliases={t: t for t in range(len(bufs))})


def _allreduce_small(v):
    r, lanes = v.shape

    def body(v_ref, sum_ref, gath, send_sems, recv_sems):
        x, y, c = _mesh_pos()
        me = 4 * x + 2 * y + c
        gath[me] = v_ref[...]
        copies = []
        for rel in range(1, N_DEV):
            peer = tuple(1 - p if (rel >> b) & 1 else p for p, b in ((x, 2), (y, 1), (c, 0)))
            copies.append(pltpu.make_async_remote_copy(
                src_ref=v_ref, dst_ref=gath.at[me], send_sem=send_sems.at[rel - 1], recv_sem=recv_sems.at[rel - 1],
                device_id=peer, device_id_type=MESH))
        for cp in copies:
            cp.start()
        for cp in copies:
            cp.wait()
        total = gath[0]
        for k in range(1, N_DEV):
            total = total + gath[k]
        sum_ref[...] = total

    return pl.pallas_call(
        body, name="allreduce_small", in_specs=[_VMEM], out_specs=_VMEM,
        out_shape=jax.ShapeDtypeStruct((r, lanes), F32),
        scratch_shapes=[pltpu.VMEM((N_DEV, r, lanes), F32)] + _sem_pairs(N_DEV - 1),
        compiler_params=pltpu.CompilerParams(vmem_limit_bytes=VMEM_LIMIT),
    )(v)


def _sum_tile(rows, width):
    return _tile(rows, max(16, (1 << 19) // width // 16 * 16), 16)


def _sum_sibling(x, recv, core):
    _, n, w = x.shape
    tr = _sum_tile(n, w)

    def body(idx_ref, x_ref, r_ref, o_ref):
        o_ref[...] = (x_ref[...].astype(F32) + r_ref[...].astype(F32)).astype(o_ref.dtype)

    row = pl.BlockSpec((tr, w), lambda i, idx: (i, 0))
    return pl.pallas_call(
        body, name="rs_sum_sibling",
        grid_spec=pltpu.PrefetchScalarGridSpec(
            num_scalar_prefetch=1, grid=(n // tr,),
            in_specs=[pl.BlockSpec((None, tr, w), lambda i, idx: (idx[0], i, 0)), row], out_specs=row),
        out_shape=jax.ShapeDtypeStruct((n, w), x.dtype), compiler_params=_params("parallel"),
    )(core.reshape(1), x, recv)


def _sum_chips(s, recv, chip, core):
    _, m, w = s.shape
    tr = _sum_tile(m, w)

    def body(idx_ref, s_ref, r0_ref, r1_ref, r2_ref, o_ref):
        o_ref[...] = ((s_ref[...].astype(F32) + r0_ref[...].astype(F32)) + r1_ref[...].astype(F32)) \
            + r2_ref[...].astype(F32)

    def got(k):
        return pl.BlockSpec((None, tr, w), lambda i, idx: (k, i, 0))

    return pl.pallas_call(
        body, name="rs_sum_chips",
        grid_spec=pltpu.PrefetchScalarGridSpec(
            num_scalar_prefetch=1, grid=(m // tr,),
            in_specs=[pl.BlockSpec((None, tr, w), lambda i, idx: (idx[0], i, 0)), got(0), got(1), got(2)],
            out_specs=pl.BlockSpec((None, tr, w), lambda i, idx: (idx[1], i, 0))),
        out_shape=jax.ShapeDtypeStruct((2, m, w), F32), compiler_params=_params("parallel"),
    )(jnp.stack([chip, core]), s, recv, recv, recv)


_ROWS = ("w_mix_out", "w_xq", "w_xk", "w_xv", "w_xo", "w_ffn_down")
_CONVS = ("gdn_conv", "sc_conv", "ffn_conv")
_REPLICATED = ("mix_norm", "gdn_a_log", "gdn_dt_bias", "gdn_out_norm", "xattn_norm", "mem_norm", "ffn_norm",
               "final_norm")
_WEIGHTS = ("mix_norm", "w_mix_in", "gdn_conv", "gdn_a_log", "gdn_dt_bias", "gdn_out_norm", "sc_conv", "w_mix_out",
            "xattn_norm", "mem_norm", "w_xq", "w_xk", "w_xv", "w_xo", "ffn_norm", "w_ffn_up", "ffn_conv",
            "w_ffn_down", "final_norm")


def _pad_rows(flat, groups):
    unit = groups * 16 * LANES
    p = flat.shape[-1]
    pad = -p % unit
    if pad:
        flat = jnp.pad(flat, [(0, 0)] * (flat.ndim - 1) + [(0, pad)])
    return flat.reshape(flat.shape[:-1] + (groups, (p + pad) // (groups * LANES), LANES))


def _split_flat(flat, shapes):
    out, off = [], 0
    for shp in shapes:
        size = 1
        for n in shp:
            size *= n
        out.append(flat[..., off:off + size].reshape(flat.shape[:-1] + tuple(shp)))
        off += size
    return out


def _by_chip(g, axis):
    rows, cols = g.shape
    if axis == 0:
        return g.reshape(N_CHIPS, rows // N_CHIPS, cols)
    return g.reshape(rows, N_CHIPS, cols // N_CHIPS).transpose(1, 0, 2)


def _halves_by_chip(g):
    _, rows, w = g.shape
    return g.astype(WIRE_DTYPE).reshape(N_CHIPS, 2, rows // 2, w).transpose(1, 0, 2, 3)


class _ReduceScatter:
    STAGES = ("mm_ffn_down_dx", "mm_ffn_up_dx", "mm_ffn_up_dw", "mm_mix_in_dx")

    def __init__(self, bufs, chip, core):
        self.bufs, self.chip, self.core = list(bufs), chip, core
        self.sums = self.from_chips = self.reduced = self.result = None

    def comm(self, stage):
        if stage == self.STAGES[0]:
            return _sibling_exchange_comm(self.bufs)
        if stage == self.STAGES[1]:
            return _chip_exchange_comm(self.sums[-1:])
        if stage == self.STAGES[2]:
            return _chip_exchange_comm(self.sums[:-1])
        return _sibling_share_comm(self.reduced)

    def landed(self, stage, outs):
        if stage == self.STAGES[0]:
            self.sums = [_sum_sibling(b.reshape(2, -1, b.shape[-1]), r.reshape(-1, r.shape[-1]), self.core)
                         .reshape(r.shape) for b, r in zip(self.bufs, outs)]
        elif stage == self.STAGES[1]:
            self.from_chips = list(outs)
        elif stage == self.STAGES[2]:
            self.reduced = [_sum_chips(s, r, self.chip, self.core)
                            for s, r in zip(self.sums, list(outs) + self.from_chips)]
        else:
            self.result = list(outs)

    def carried(self, name, *args, **kwargs):
        if name not in self.STAGES:
            return _matmul(name, *args, **kwargs)
        out, outs = _matmul(name, *args, comm=self.comm(name), **kwargs)
        self.landed(name, outs)
        return out

    def run_alone(self):
        for stage, name in zip(self.STAGES, ("rs_sibling_exchange", "rs_chip_exchange_rows", "rs_chip_exchange_cols",
                                             "rs_sibling_share")):
            self.landed(stage, _run_comm(name, self.comm(stage)))


def kernel(x, mem, mix_norm, w_mix_in, gdn_conv, gdn_a_log, gdn_dt_bias, gdn_out_norm, sc_conv, w_mix_out, xattn_norm, mem_norm, w_xq, w_xk, w_xv, w_xo, ffn_norm, w_ffn_up, ffn_conv, w_ffn_down, final_norm, loss_target, m_mix_norm, m_w_mix_in, m_gdn_conv, m_gdn_a_log, m_gdn_dt_bias, m_gdn_out_norm, m_sc_conv, m_w_mix_out, m_xattn_norm, m_mem_norm, m_w_xq, m_w_xk, m_w_xv, m_w_xo, m_ffn_norm, m_w_ffn_up, m_ffn_conv, m_w_ffn_down, m_final_norm, v_mix_norm, v_w_mix_in, v_gdn_conv, v_gdn_a_log, v_gdn_dt_bias, v_gdn_out_norm, v_sc_conv, v_w_mix_out, v_xattn_norm, v_mem_norm, v_w_xq, v_w_xk, v_w_xv, v_w_xo, v_ffn_norm, v_w_ffn_up, v_ffn_conv, v_w_ffn_down, v_final_norm):
    w = dict(zip(_WEIGHTS, (mix_norm, w_mix_in, gdn_conv, gdn_a_log, gdn_dt_bias, gdn_out_norm, sc_conv, w_mix_out,
                            xattn_norm, mem_norm, w_xq, w_xk, w_xv, w_xo, ffn_norm, w_ffn_up, ffn_conv, w_ffn_down,
                            final_norm)))
    m = dict(zip(_WEIGHTS, (m_mix_norm, m_w_mix_in, m_gdn_conv, m_gdn_a_log, m_gdn_dt_bias, m_gdn_out_norm, m_sc_conv,
                            m_w_mix_out, m_xattn_norm, m_mem_norm, m_w_xq, m_w_xk, m_w_xv, m_w_xo, m_ffn_norm,
                            m_w_ffn_up, m_ffn_conv, m_w_ffn_down, m_final_norm)))
    v = dict(zip(_WEIGHTS, (v_mix_norm, v_w_mix_in, v_gdn_conv, v_gdn_a_log, v_gdn_dt_bias, v_gdn_out_norm, v_sc_conv,
                            v_w_mix_out, v_xattn_norm, v_mem_norm, v_w_xq, v_w_xk, v_w_xv, v_w_xo, v_ffn_norm,
                            v_w_ffn_up, v_ffn_conv, v_w_ffn_down, v_final_norm)))
    core = lax.axis_index("c")
    chip = 2 * lax.axis_index("x") + lax.axis_index("y")
    depth, heads = gdn_a_log.shape
    dh = gdn_out_norm.shape[1]
    d, wid = x.shape[2], heads * dh

    row_sizes = [w[n].shape[1] for n in _ROWS]
    row_offs = [sum(row_sizes[:k]) for k in range(len(_ROWS))]
    src_in, src_up = w_mix_in.astype(WIRE_DTYPE), w_ffn_up.astype(WIRE_DTYPE)
    src_rows = jnp.concatenate([w[n] for n in _ROWS], axis=1).astype(WIRE_DTYPE)
    src_convs = _pad_rows(jnp.concatenate([w[n].reshape(-1) for n in _CONVS]), 2)
    g_in, g_up, g_rows, g_convs = _run_comm(
        "allgather_first", _allgather_comm([src_in[0], src_up[0], src_rows[0], src_convs]))
    conv_full = {n: jnp.moveaxis(part, 0, 2).reshape(depth, part.shape[2], -1)
                 for n, part in zip(_CONVS, _split_flat(g_convs.reshape(N_CHIPS, -1), [w[n].shape for n in _CONVS]))}

    def params(l, g_in, g_up, g_rows):
        cols = lambda g: jnp.concatenate([g[j] for j in range(N_CHIPS)], axis=1)
        rows = lambda k: jnp.concatenate(
            [g_rows[j, row_offs[k]:row_offs[k] + row_sizes[k]] for j in range(N_CHIPS)], axis=0)
        small = {"mix_norm": mix_norm[l], "xattn_norm": xattn_norm[l], "mem_norm": mem_norm[l],
                 "ffn_norm": ffn_norm[l], "gdn_out_norm": gdn_out_norm[l], "a_log": gdn_a_log[l],
                 "dt_bias": gdn_dt_bias[l], "gdn_conv": conv_full["gdn_conv"][l], "sc_conv": conv_full["sc_conv"][l],
                 "ffn_conv": conv_full["ffn_conv"][l]}
        return _layer_params(cols(g_in), rows(0), rows(1), rows(2), rows(3), rows(4), cols(g_up), rows(5), small,
                             heads, dh)

    xl, mem_l = x[0], mem[0]
    layers, saved = [], []
    for l in range(depth):
        layers.append(params(l, g_in, g_up, g_rows))
        carry = {}
        if l + 1 < depth:
            carry = {"mm_mix_in": _allgather_comm([src_up[l + 1]]), "mm_ffn_up": _allgather_comm([src_rows[l + 1]]),
                     "mm_ffn_down": _allgather_comm([src_in[l + 1]])}
        xl, s, landed = _layer_fwd(xl, mem_l, layers[l], heads, dh, carry)
        saved.append(s)
        if carry:
            (g_up,), (g_rows,), (g_in,) = landed["mm_mix_in"], landed["mm_ffn_up"], landed["mm_ffn_down"]
    loss_row, dx, dxb, g_final = _final_loss(xl, final_norm, loss_target[0])

    def by_chip(g):
        g_win = jnp.concatenate([g["wmain"][:, :4 * wid], g["wba"][:, :2 * heads], g["wmain"][:, 4 * wid:]], axis=1)
        parts = (g["wout"], g["wq"], g["wkv"][:, :d], g["wkv"][:, d:], g["wo"], g["wdown"])
        return [_halves_by_chip(_by_chip(g_win, 1)), _halves_by_chip(_by_chip(g["wup"], 1)),
                _halves_by_chip(jnp.concatenate([_by_chip(p, 0) for p in parts], axis=1))]

    per_layer, shards, reduce = [None] * depth, [None] * depth, None
    for l in reversed(range(depth)):
        dx, dxb, per_layer[l] = _layer_bwd(dx, dxb, mem_l, saved[l], layers[l], heads, dh, reduce)
        if reduce is not None:
            shards[l + 1] = reduce.result
        reduce = _ReduceScatter(by_chip(per_layer[l]), chip, core)
    reduce.run_alone()
    shards[0] = reduce.result
    g_in_s, g_up_s, g_rows_s = (jnp.stack([s[t].reshape(-1, s[t].shape[-1]) for s in shards]) for t in range(3))
    grad = {"w_mix_in": g_in_s, "w_ffn_up": g_up_s}
    for n, off, size in zip(_ROWS, row_offs, row_sizes):
        grad[n] = g_rows_s[:, off:off + size]

    stack = lambda k: jnp.stack([g[k] for g in per_layer])
    small_g = {"mix_norm": stack("mix_norm"), "gdn_a_log": stack("a_log").reshape(depth, heads),
               "gdn_dt_bias": stack("dt_bias").reshape(depth, heads), "gdn_out_norm": stack("gdn_out_norm"),
               "xattn_norm": stack("xattn_norm"), "mem_norm": stack("mem_norm"), "ffn_norm": stack("ffn_norm"),
               "final_norm": g_final, "gdn_conv": stack("gdn_conv"), "sc_conv": stack("sc_conv"),
               "ffn_conv": stack("ffn_conv")}
    names = _REPLICATED + _CONVS
    small = jnp.concatenate([small_g[n].reshape(-1) for n in names] + [loss_row[0, :1]])
    small_sum = _allreduce_small(_pad_rows(small, 1)[0]).reshape(-1)
    parts = _split_flat(small_sum, [small_g[n].shape for n in names] + [(1,)])
    g_rep = dict(zip(_REPLICATED, parts[:len(_REPLICATED)]))
    for n, part in zip(_CONVS, parts[len(_REPLICATED):-1]):
        grad[n] = lax.dynamic_slice_in_dim(part, chip * w[n].shape[2], w[n].shape[2], axis=2)
    loss = parts[-1][0]

    delta, new_m, new_v = {}, {}, {}
    for n in ("w_mix_in", "w_ffn_up") + _ROWS + _CONVS:
        delta[n], new_m[n], new_v[n] = _adamw("adamw_" + n, w[n], grad[n], m[n], v[n])
    pack_rep = lambda t: _pad_rows(jnp.concatenate([t[n].reshape(-1) for n in _REPLICATED]), 1)[0]
    outs = _adamw("adamw_replicated", pack_rep(w), pack_rep(g_rep), pack_rep(m), pack_rep(v))
    shapes = [w[n].shape for n in _REPLICATED]
    for tgt, packed_out in zip((delta, new_m, new_v), outs):
        tgt.update(zip(_REPLICATED, _split_flat(packed_out.reshape(-1), shapes)))
    grad.update(g_rep)
    return (loss, dx[None], *[grad[n] for n in _WEIGHTS], *[delta[n] for n in _WEIGHTS],
            *[new_m[n] for n in _WEIGHTS], *[new_v[n] for n in _WEIGHTS])
```

```python
import functools

import jax
import jax.numpy as jnp
from jax import lax
from jax.experimental import pallas as pl
from jax.experimental.pallas import tpu as pltpu

F32 = jnp.float32
MXU_DTYPE = jnp.bfloat16
WIRE_DTYPE = jnp.bfloat16
SOLVE_PRECISION = lax.Precision.HIGH
EPS = 1e-6
CHUNK = 64
XATTN_HEADS = 4
LANES = 128
HALO = 16
EW_ROWS, EW_COLS = 256, 2816
VMEM_LIMIT = 52 * 1024 * 1024
ADAM_LR, ADAM_B1, ADAM_B2, ADAM_EPS, ADAM_WD, ADAM_STEP = 0.001, 0.9, 0.999, 1e-08, 0.01, 10
MESH = pl.DeviceIdType.MESH
N_CHIPS = 4
N_DEV = 8

_DIMS = {
    "nn": (((1,), (0,)), ((), ())),
    "nt": (((1,), (1,)), ((), ())),
    "tn": (((0,), (0,)), ((), ())),
}


def _tile(n, pref, align=LANES):
    if n <= pref:
        return n
    t = (pref // align) * align
    while t >= align:
        if n % t == 0:
            return t
        t -= align
    return n


def _params(*sem):
    return pltpu.CompilerParams(dimension_semantics=sem, vmem_limit_bytes=VMEM_LIMIT)


def _dot(a, b, form, hi=False):
    (ca, cb), _ = _DIMS[form]
    dims = (((ca[0] + 1,), (cb[0] + 1,)), ((0,), (0,))) if a.ndim == 3 else _DIMS[form]
    if hi:
        return lax.dot_general(a.astype(F32), b.astype(F32), dims, precision=SOLVE_PRECISION,
                               preferred_element_type=F32)
    return lax.dot_general(a.astype(MXU_DTYPE), b.astype(MXU_DTYPE), dims, preferred_element_type=F32)


@functools.partial(jax.custom_vjp, nondiff_argnums=(2, 3))
def _dot_d(a, b, form, hi):
    return _dot(a, b, form, hi)


def _dot_d_fwd(a, b, form, hi):
    return _dot(a, b, form, hi), (a, b)


def _dot_d_bwd(form, hi, res, g):
    a, b = res
    if form == "nn":
        da, db = _dot_d(g, b, "nt", hi), _dot_d(a, g, "tn", hi)
    elif form == "nt":
        da, db = _dot_d(g, b, "nn", hi), _dot_d(g, a, "tn", hi)
    else:
        da, db = _dot_d(b, g, "nt", hi), _dot_d(a, g, "nn", hi)
    return da.astype(a.dtype), db.astype(b.dtype)


_dot_d.defvjp(_dot_d_fwd, _dot_d_bwd)


def _tri_inv_impl(a, mmh):
    c = a.shape[-1]
    r = lax.broadcasted_iota(jnp.int32, (c, c), 0)
    s = lax.broadcasted_iota(jnp.int32, (c, c), 1)
    eye = (r == s).astype(F32)
    diag_blk = (r // 16) == (s // 16)
    d = jnp.where(diag_blk, a, 0.0)
    low = a - d
    d2 = mmh(d, d)
    d4 = mmh(d2, d2)
    d8 = mmh(d4, d4)
    td = mmh(mmh(mmh(eye - d, eye + d2), eye + d4), eye + d8)
    n = mmh(td, low)
    acc = eye - n
    p = n
    pw = 1
    while 2 * pw < c // 16:
        p = mmh(p, p)
        pw *= 2
        acc = mmh(acc, eye + p)
    return mmh(acc, td)


def _mmh_plain(a, b):
    return _dot(a, b, "nn", True)


@jax.custom_vjp
def _tri_inv_known(a, t):
    return t


def _tri_inv_known_fwd(a, t):
    return t, t


def _tri_inv_known_bwd(t, g):
    return -_dot(_dot(t, g, "tn", True), t, "nt", True), jnp.zeros_like(t)


_tri_inv_known.defvjp(_tri_inv_known_fwd, _tri_inv_known_bwd)


class _Ops:
    def __init__(self, diff, tinv=None):
        self.diff, self.tinv = diff, tinv

    def mm(self, a, b, form="nn"):
        return _dot_d(a, b, form, False) if self.diff else _dot(a, b, form, False)

    def mmh(self, a, b, form="nn"):
        return _dot_d(a, b, form, True) if self.diff else _dot(a, b, form, True)

    def tri_inv(self, a):
        return _tri_inv_known(a, self.tinv) if self.diff else _tri_inv_impl(a, _mmh_plain)


_PLAIN = _Ops(False)
_DIFF = _Ops(True)


def _sigmoid(x):
    return 1.0 / (1.0 + jnp.exp(-x))


def _silu(x):
    return x * _sigmoid(x)


def _softplus(x):
    return jnp.maximum(x, 0.0) + jnp.log(1.0 + jnp.exp(-jnp.abs(x)))


def _rms(x, g):
    return x * lax.rsqrt(jnp.mean(x * x, axis=-1, keepdims=True) + EPS) * g


def _matmul_tiles(m, n, k):
    if k <= 2048:
        return _tile(m, 1024), _tile(n, 1408), k
    if k <= 8192:
        return _tile(m, 512), _tile(n, 512), k
    return _tile(m, 1024), _tile(n, 1024), _tile(k, 2816)


def _matmul(name, a, b, form, out_dtype, add=None, comm=None):
    if form == "nn":
        (m, k), (k2, n) = a.shape, b.shape
    elif form == "nt":
        (m, k), (n, k2) = a.shape, b.shape
    else:
        (k, m), (k2, n) = a.shape, b.shape
    assert k == k2, (name, a.shape, b.shape, form)
    tm, tn, tk = _matmul_tiles(m, n, k)
    nk = k // tk
    out_bytes = tm * tn * (jnp.dtype(out_dtype).itemsize + (4 if add is not None else 0))
    vmem = 2 * (tm * tk * a.dtype.itemsize + tk * tn * b.dtype.itemsize + out_bytes) + (tm * tn * 4 if nk > 1 else 0)
    assert vmem <= VMEM_LIMIT, (name, tm, tn, tk, vmem)
    if form == "nn":
        a_spec = pl.BlockSpec((tm, tk), lambda i, j, kk: (i, kk))
        b_spec = pl.BlockSpec((tk, tn), lambda i, j, kk: (kk, j))
    elif form == "nt":
        a_spec = pl.BlockSpec((tm, tk), lambda i, j, kk: (i, kk))
        b_spec = pl.BlockSpec((tn, tk), lambda i, j, kk: (j, kk))
    else:
        a_spec = pl.BlockSpec((tk, tm), lambda i, j, kk: (kk, i))
        b_spec = pl.BlockSpec((tk, tn), lambda i, j, kk: (kk, j))
    o_spec = pl.BlockSpec((tm, tn), lambda i, j, kk: (i, j))
    has_add = add is not None
    grid = (m // tm, n // tn, nk)
    n_in = 3 if has_add else 2
    c_in, c_out = (len(comm.ins), len(comm.outs)) if comm is not None else (0, 0)

    def body(*refs):
        a_ref, b_ref = refs[0], refs[1]
        add_ref = refs[2] if has_add else None
        o_ref = refs[n_in + c_in]
        pids = [pl.program_id(ax) for ax in range(3)]
        if comm is not None:
            comm_refs = (refs[n_in:n_in + c_in], refs[n_in + c_in + 1:n_in + c_in + 1 + c_out], refs[-2:])

            @pl.when(jnp.logical_and(jnp.logical_and(pids[0] == 0, pids[1] == 0), pids[2] == 0))
            def _():
                comm.start(*comm_refs)

        def finish(acc):
            if has_add:
                acc = acc + add_ref[...].astype(F32)
            o_ref[...] = acc.astype(o_ref.dtype)

        p = _dot(a_ref[...], b_ref[...], form)
        if nk == 1:
            finish(p)
        else:
            acc_ref = refs[n_in + c_in + 1 + c_out]

            @pl.when(pids[2] == 0)
            def _():
                acc_ref[...] = p

            @pl.when(pids[2] > 0)
            def _():
                acc_ref[...] += p

            @pl.when(pids[2] == nk - 1)
            def _():
                finish(acc_ref[...])

        if comm is not None:
            @pl.when(jnp.logical_and(jnp.logical_and(pids[0] == grid[0] - 1, pids[1] == grid[1] - 1),
                                     pids[2] == grid[2] - 1))
            def _():
                comm.finish(*comm_refs)

    acc_scratch = [pltpu.VMEM((tm, tn), F32)] if nk > 1 else []
    if comm is None:
        return pl.pallas_call(
            body, name=name, grid=grid, in_specs=[a_spec, b_spec] + ([o_spec] if has_add else []), out_specs=o_spec,
            out_shape=jax.ShapeDtypeStruct((m, n), out_dtype), scratch_shapes=acc_scratch,
            compiler_params=_params("parallel", "parallel", "arbitrary"),
        )(*((a, b, add) if has_add else (a, b)))
    outs = pl.pallas_call(
        body, name=name, grid=grid, in_specs=[a_spec, b_spec] + ([o_spec] if has_add else []) + [_ANY] * c_in,
        out_specs=[o_spec] + [_ANY] * c_out, out_shape=[jax.ShapeDtypeStruct((m, n), out_dtype)] + list(comm.outs),
        scratch_shapes=acc_scratch + _sem_pairs(comm.n_sems),
        input_output_aliases={n_in + i: 1 + o for i, o in comm.aliases.items()},
        compiler_params=_params("arbitrary", "arbitrary", "arbitrary"),
    )(*((a, b, add) if has_add else (a, b)), *comm.ins)
    return outs[0], list(outs[1:])


def _rms_fwd(name, x, g):
    t, d = x.shape
    tm = _tile(t, 512, 16)

    def body(x_ref, g_ref, o_ref):
        o_ref[...] = _rms(x_ref[...], g_ref[...]).astype(o_ref.dtype)

    return pl.pallas_call(
        body, name=name, grid=(t // tm,),
        in_specs=[pl.BlockSpec((tm, d), lambda i: (i, 0)), pl.BlockSpec((1, d), lambda i: (0, 0))],
        out_specs=pl.BlockSpec((tm, d), lambda i: (i, 0)),
        out_shape=jax.ShapeDtypeStruct((t, d), MXU_DTYPE), compiler_params=_params("parallel"),
    )(x, g.reshape(1, d))


def _rms_bwd(name, x, g, dh, dres=None):
    t, d = x.shape
    tm = _tile(t, 256, 16)
    has_res = dres is not None

    def body(*refs):
        x_ref, g_ref, dh_ref = refs[:3]
        dres_ref = refs[3] if has_res else None
        dx_ref, dxb_ref, dg_ref = refs[-3:]
        _, vjp = jax.vjp(_rms, x_ref[...], g_ref[...])
        dx, dg = vjp(dh_ref[...].astype(F32))
        if has_res:
            dx = dx + dres_ref[...]
        dx_ref[...] = dx
        dxb_ref[...] = dx.astype(dxb_ref.dtype)
        first = pl.program_id(0) == 0

        @pl.when(first)
        def _():
            dg_ref[...] = dg

        @pl.when(jnp.logical_not(first))
        def _():
            dg_ref[...] += dg

    row = pl.BlockSpec((tm, d), lambda i: (i, 0))
    vec = pl.BlockSpec((1, d), lambda i: (0, 0))
    dx, dxb, dg = pl.pallas_call(
        body, name=name, grid=(t // tm,),
        in_specs=[row, vec, row] + ([row] if has_res else []), out_specs=[row, row, vec],
        out_shape=[jax.ShapeDtypeStruct((t, d), F32), jax.ShapeDtypeStruct((t, d), MXU_DTYPE),
                   jax.ShapeDtypeStruct((1, d), F32)],
        compiler_params=_params("arbitrary"),
    )(*((x, g.reshape(1, d), dh) + ((dres,) if has_res else ())))
    return dx, dxb, dg.reshape(d)


def _final_loss(x, g, target):
    t, d = x.shape
    tm = _tile(t, 256, 16)

    def body(x_ref, g_ref, t_ref, loss_ref, dx_ref, dxb_ref, dg_ref):
        y, vjp = jax.vjp(_rms, x_ref[...], g_ref[...])
        err = y - t_ref[...]
        dx, dg = vjp(err * (1.0 / d))
        dx_ref[...] = dx
        dxb_ref[...] = dx.astype(dxb_ref.dtype)
        part = jnp.zeros((1, LANES), F32) + 0.5 * jnp.sum(jnp.mean(err * err, axis=-1, keepdims=True))
        first = pl.program_id(0) == 0

        @pl.when(first)
        def _():
            dg_ref[...] = dg
            loss_ref[...] = part

        @pl.when(jnp.logical_not(first))
        def _():
            dg_ref[...] += dg
            loss_ref[...] += part

    row = pl.BlockSpec((tm, d), lambda i: (i, 0))
    vec = pl.BlockSpec((1, d), lambda i: (0, 0))
    loss, dx, dxb, dg = pl.pallas_call(
        body, name="final_loss", grid=(t // tm,), in_specs=[row, vec, row],
        out_specs=[pl.BlockSpec((1, LANES), lambda i: (0, 0)), row, row, vec],
        out_shape=[jax.ShapeDtypeStruct((1, LANES), F32), jax.ShapeDtypeStruct((t, d), F32),
                   jax.ShapeDtypeStruct((t, d), MXU_DTYPE), jax.ShapeDtypeStruct((1, d), F32)],
        compiler_params=_params("arbitrary"),
    )(x, g.reshape(1, d), target)
    return loss, dx, dxb, dg.reshape(d)


def _conv_taps(x_ext, w, rows):
    kk = w.shape[0]
    y = x_ext[HALO:] * w[kk - 1:kk, :]
    for j in range(kk - 1):
        y = y + pltpu.roll(x_ext, kk - 1 - j, axis=0)[HALO:] * w[j:j + 1, :]
    return y


def _col_specs(tm, tn, col0, t_rows):
    assert col0 % tn == 0 and tm % HALO == 0
    c0 = col0 // tn
    per, last = tm // HALO, t_rows // HALO - 1
    tile = pl.BlockSpec((tm, tn), lambda j, i: (i, c0 + j))
    prev = pl.BlockSpec((HALO, tn), lambda j, i: (jnp.maximum(i * per - 1, 0), c0 + j))
    nxt = pl.BlockSpec((HALO, tn), lambda j, i: (jnp.minimum((i + 1) * per, last), c0 + j))
    return tile, prev, nxt


def _conv_fwd(name, xa, xa_col, w, w_col, ncols, out_dtype, xb=None, xb_col=0, gate=None, gate_col=0):
    t = xa.shape[0]
    kk = w.shape[0]
    tm, tn = _tile(t, EW_ROWS, HALO), _tile(ncols, EW_COLS)
    nrow = t // tm
    has_b, has_g = xb is not None, gate is not None

    def body(*refs):
        refs = list(refs)
        xa_ref, xap_ref = refs.pop(0), refs.pop(0)
        xb_ref, xbp_ref = (refs.pop(0), refs.pop(0)) if has_b else (None, None)
        w_ref = refs.pop(0)
        g_ref = refs.pop(0) if has_g else None
        o_ref = refs.pop(0)
        i = pl.program_id(1)
        x, xp = xa_ref[...].astype(F32), xap_ref[...].astype(F32)
        if has_b:
            x, xp = x * xb_ref[...].astype(F32), xp * xbp_ref[...].astype(F32)
        xp = jnp.where(i == 0, 0.0, xp)
        y = _conv_taps(jnp.concatenate([xp, x], axis=0), w_ref[...], tm)
        if has_g:
            y = y * g_ref[...].astype(F32)
        o_ref[...] = y.astype(o_ref.dtype)

    a_tile, a_prev, _ = _col_specs(tm, tn, xa_col, t)
    ins, specs = [xa, xa], [a_tile, a_prev]
    if has_b:
        b_tile, b_prev, _ = _col_specs(tm, tn, xb_col, t)
        ins, specs = ins + [xb, xb], specs + [b_tile, b_prev]
    assert w_col % tn == 0
    ins, specs = ins + [w], specs + [pl.BlockSpec((kk, tn), lambda j, i: (0, w_col // tn + j))]
    if has_g:
        ins, specs = ins + [gate], specs + [_col_specs(tm, tn, gate_col, t)[0]]
    return pl.pallas_call(
        body, name=name, grid=(ncols // tn, nrow), in_specs=specs,
        out_specs=pl.BlockSpec((tm, tn), lambda j, i: (i, j)),
        out_shape=jax.ShapeDtypeStruct((t, ncols), out_dtype), compiler_params=_params("parallel", "parallel"),
    )(*ins)


def _conv_bwd(name, xa, xa_col, w, w_col, dy, dy_col, ncols, dx_dtype, xb=None, xb_col=0, gate=None, gate_col=0):
    t = xa.shape[0]
    kk = w.shape[0]
    tm, tn = _tile(t, EW_ROWS, HALO), _tile(ncols, EW_COLS)
    nrow = t // tm
    has_b, has_g = xb is not None, gate is not None

    def body(*refs):
        refs = list(refs)
        xa_ref, xap_ref = refs.pop(0), refs.pop(0)
        xb_ref, xbp_ref = (refs.pop(0), refs.pop(0)) if has_b else (None, None)
        w_ref = refs.pop(0)
        dy_ref, dyn_ref = refs.pop(0), refs.pop(0)
        g_ref, gn_ref = (refs.pop(0), refs.pop(0)) if has_g else (None, None)
        dxa_ref = refs.pop(0)
        dxb_ref = refs.pop(0) if has_b else None
        dg_ref = refs.pop(0) if has_g else None
        dw_ref = refs.pop(0)
        i = pl.program_id(1)
        wv = w_ref[...]
        xa_t, xa_p = xa_ref[...].astype(F32), xap_ref[...].astype(F32)
        x, xp = xa_t, xa_p
        if has_b:
            xb_t = xb_ref[...].astype(F32)
            x, xp = x * xb_t, xp * xbp_ref[...].astype(F32)
        xp = jnp.where(i == 0, 0.0, xp)
        x_ext = jnp.concatenate([xp, x], axis=0)
        dyv, dyn = dy_ref[...].astype(F32), dyn_ref[...].astype(F32)
        if has_g:
            dg_ref[...] = (dyv * _conv_taps(x_ext, wv, tm)).astype(dg_ref.dtype)
            dyv, dyn = dyv * g_ref[...].astype(F32), dyn * gn_ref[...].astype(F32)
        dyn = jnp.where(i == nrow - 1, 0.0, dyn)
        dy_ext = jnp.concatenate([dyv, dyn], axis=0)
        dx = dyv * wv[kk - 1:kk, :]
        row8 = lax.broadcasted_iota(jnp.int32, (8, tn), 0)
        dw = jnp.where(row8 == kk - 1, jnp.sum(dyv * x, axis=0, keepdims=True), 0.0)
        for j in range(kk - 1):
            s = kk - 1 - j
            dx = dx + pltpu.roll(dy_ext, tm + HALO - s, axis=0)[:tm] * wv[j:j + 1, :]
            dwj = jnp.sum(dyv * pltpu.roll(x_ext, s, axis=0)[HALO:], axis=0, keepdims=True)
            dw = dw + jnp.where(row8 == j, dwj, 0.0)
        if has_b:
            dxa_ref[...] = (dx * xb_t).astype(dxa_ref.dtype)
            dxb_ref[...] = (dx * xa_t).astype(dxb_ref.dtype)
        else:
            dxa_ref[...] = dx.astype(dxa_ref.dtype)

        @pl.when(i == 0)
        def _():
            dw_ref[...] = dw

        @pl.when(i > 0)
        def _():
            dw_ref[...] += dw

    a_tile, a_prev, _ = _col_specs(tm, tn, xa_col, t)
    ins, specs = [xa, xa], [a_tile, a_prev]
    if has_b:
        b_tile, b_prev, _ = _col_specs(tm, tn, xb_col, t)
        ins, specs = ins + [xb, xb], specs + [b_tile, b_prev]
    assert w_col % tn == 0
    ins, specs = ins + [w], specs + [pl.BlockSpec((kk, tn), lambda j, i: (0, w_col // tn + j))]
    d_tile, _, d_next = _col_specs(tm, tn, dy_col, t)
    ins, specs = ins + [dy, dy], specs + [d_tile, d_next]
    if has_g:
        g_tile, _, g_next = _col_specs(tm, tn, gate_col, t)
        ins, specs = ins + [gate, gate], specs + [g_tile, g_next]
    out_tile = pl.BlockSpec((tm, tn), lambda j, i: (i, j))
    shapes, ospecs = [jax.ShapeDtypeStruct((t, ncols), dx_dtype)], [out_tile]
    if has_b:
        shapes, ospecs = shapes + [jax.ShapeDtypeStruct((t, ncols), dx_dtype)], ospecs + [out_tile]
    if has_g:
        shapes, ospecs = shapes + [jax.ShapeDtypeStruct((t, ncols), dx_dtype)], ospecs + [out_tile]
    shapes, ospecs = shapes + [jax.ShapeDtypeStruct((8, ncols), F32)], ospecs + [pl.BlockSpec((8, tn), lambda j, i: (0, j))]
    outs = list(pl.pallas_call(
        body, name=name, grid=(ncols // tn, nrow), in_specs=specs, out_specs=ospecs, out_shape=shapes,
        compiler_params=_params("parallel", "arbitrary"),
    )(*ins))
    dxa = outs.pop(0)
    dxb = outs.pop(0) if has_b else None
    dgate = outs.pop(0) if has_g else None
    return dxa, dxb, dgate, outs.pop(0)[:kk]


def _swiglu_fwd(u):
    t, f2 = u.shape
    f = f2 // 2
    tm, tn = _tile(t, EW_ROWS, 16), _tile(f, EW_COLS)
    nf = f // tn

    def body(g_ref, u_ref, o_ref):
        o_ref[...] = (_silu(g_ref[...].astype(F32)) * u_ref[...].astype(F32)).astype(o_ref.dtype)

    return pl.pallas_call(
        body, name="swiglu_fwd", grid=(t // tm, nf),
        in_specs=[pl.BlockSpec((tm, tn), lambda i, j: (i, j)), pl.BlockSpec((tm, tn), lambda i, j: (i, nf + j))],
        out_specs=pl.BlockSpec((tm, tn), lambda i, j: (i, j)),
        out_shape=jax.ShapeDtypeStruct((t, f), MXU_DTYPE), compiler_params=_params("parallel", "parallel"),
    )(u, u)


def _swiglu_bwd(u, da):
    t, f2 = u.shape
    f = f2 // 2
    tm, tn = _tile(t, EW_ROWS, 16), _tile(f, EW_COLS)
    nf = f // tn

    def body(g_ref, u_ref, da_ref, o_ref):
        g, d = g_ref[...].astype(F32), da_ref[...].astype(F32)
        sg = _sigmoid(g)
        gate_half = pl.program_id(1) < nf

        @pl.when(gate_half)
        def _():
            o_ref[...] = (d * u_ref[...].astype(F32) * (sg * (1.0 + g * (1.0 - sg)))).astype(o_ref.dtype)

        @pl.when(jnp.logical_not(gate_half))
        def _():
            o_ref[...] = (d * (g * sg)).astype(o_ref.dtype)

    return pl.pallas_call(
        body, name="swiglu_bwd", grid=(t // tm, 2 * nf),
        in_specs=[pl.BlockSpec((tm, tn), lambda i, j: (i, j % nf)),
                  pl.BlockSpec((tm, tn), lambda i, j: (i, nf + j % nf)),
                  pl.BlockSpec((tm, tn), lambda i, j: (i, j % nf))],
        out_specs=pl.BlockSpec((tm, tn), lambda i, j: (i, j)),
        out_shape=jax.ShapeDtypeStruct((t, f2), MXU_DTYPE), compiler_params=_params("parallel", "parallel"),
    )(u, u, da)


def _gdn_prep(ops, qc, kc, vc, b_col, a_col, a_log, dt_bias):
    c, dh = qc.shape[-2:]
    q, k, v = _silu(qc), _silu(kc), _silu(vc)
    q = q * lax.rsqrt(jnp.sum(q * q, axis=-1, keepdims=True) + EPS) * (dh ** -0.5)
    k = k * lax.rsqrt(jnp.sum(k * k, axis=-1, keepdims=True) + EPS)
    beta = _sigmoid(b_col)
    g_col = -jnp.exp(a_log) * _softplus(a_col + dt_bias)
    r = lax.broadcasted_iota(jnp.int32, (c, c), 0)
    s = lax.broadcasted_iota(jnp.int32, (c, c), 1)
    g_row = jnp.sum(jnp.where(r == s, g_col, 0.0), axis=-2, keepdims=True)
    gc_col = jnp.sum(jnp.where(s <= r, g_row, 0.0), axis=-1, keepdims=True)
    gc_row = jnp.sum(jnp.where(r <= s, g_col, 0.0), axis=-2, keepdims=True)
    decay = jnp.exp(jnp.where(s <= r, gc_col - gc_row, -1e30))
    kb = k * beta
    a = jnp.where(s < r, ops.mm(kb, k, "nt") * decay, 0.0)
    tinv = ops.tri_inv(a)
    e_col = jnp.exp(gc_col)
    uw = ops.mmh(tinv, jnp.concatenate([v * beta, kb * e_col], axis=-1))
    u, w = uw[..., :dh], uw[..., dh:]
    attn = ops.mm(q, k, "nt") * decay
    g_last = jnp.sum(g_col, axis=-2, keepdims=True)
    return u, w, attn, q * e_col, k * jnp.exp(g_last - gc_col), g_last, tinv


def _gdn_step(ops, state, u, w, attn, q_dec, k_dec, g_last):
    v_new = u - ops.mm(w, state)
    o = ops.mm(q_dec, state) + ops.mm(attn, v_new)
    return o, state * jnp.exp(g_last) + ops.mm(k_dec, v_new, "tn")


PREP_HEADS, SCAN_HEADS = 2, 8


def _gdn_blocks(t, heads, hb_pref):
    tc = _tile(t, 256, CHUNK)
    hb = max(h for h in range(1, hb_pref + 1) if heads % h == 0)
    return tc, hb


def _to_chunks(ref, hb, dh):
    tc = ref.shape[0]
    return jnp.concatenate([ref[:, h * dh:(h + 1) * dh].astype(F32).reshape(tc // CHUNK, CHUNK, dh)
                            for h in range(hb)], axis=0)


def _from_chunks(ref, val, hb, dh):
    tc = ref.shape[0]
    ncb = tc // CHUNK
    for h in range(hb):
        ref[:, h * dh:(h + 1) * dh] = val[h * ncb:(h + 1) * ncb].reshape(tc, dh).astype(ref.dtype)


def _per_chunk(s, ncb):
    hb = s.shape[0]
    return jnp.broadcast_to(s[:, None], (hb, ncb, 1, 1)).reshape(hb * ncb, 1, 1)


def _gate_columns(pba, first_head, hb, heads):
    tc = pba.shape[0]
    lane = lax.broadcasted_iota(jnp.int32, pba.shape, 1)
    pick = lambda k: jnp.sum(jnp.where(lane == k, pba, 0.0), axis=1, keepdims=True).reshape(tc // CHUNK, CHUNK, 1)
    return (jnp.concatenate([pick(first_head + h) for h in range(hb)], axis=0),
            jnp.concatenate([pick(heads + first_head + h) for h in range(hb)], axis=0))


def _gdn_prep_fwd(qkv, pba, a_log, dt_bias, heads, dh):
    t = qkv.shape[0]
    tc, hb = _gdn_blocks(t, heads, PREP_HEADS)
    ncb, nhb, width = tc // CHUNK, heads // hb, heads * dh
    nc = t // CHUNK

    def body(q_ref, k_ref, v_ref, g_ref, al_ref, dt_ref, u_ref, w_ref, p_ref, qd_ref, kd_ref, gl_ref, ti_ref):
        b_col, a_col = _gate_columns(g_ref[...], pl.program_id(1) * hb, hb, heads)
        u, w, p, qd, kd, gl, tinv = _gdn_prep(
            _PLAIN, _to_chunks(q_ref, hb, dh), _to_chunks(k_ref, hb, dh), _to_chunks(v_ref, hb, dh), b_col, a_col,
            _per_chunk(al_ref[...], ncb), _per_chunk(dt_ref[...], ncb))
        _from_chunks(u_ref, u, hb, dh)
        _from_chunks(w_ref, w, hb, dh)
        _from_chunks(qd_ref, qd, hb, dh)
        _from_chunks(kd_ref, kd, hb, dh)
        p_ref[...] = p.reshape(hb, tc, CHUNK).astype(p_ref.dtype)
        gl_ref[...] = gl.reshape(hb, ncb, 1, 1)
        ti_ref[...] = tinv.reshape(hb, tc, CHUNK)

    def tok(off):
        return pl.BlockSpec((tc, hb * dh), lambda i, j: (i, off * nhb + j))

    gate = pl.BlockSpec((tc, LANES), lambda i, j: (i, 0))
    scal = pl.BlockSpec((hb, 1, 1), lambda i, j: (j, 0, 0))
    square = pl.BlockSpec((hb, tc, CHUNK), lambda i, j: (j, i, 0))
    *prep, tinv = pl.pallas_call(
        body, name="gdn_prep_fwd", grid=(t // tc, nhb),
        in_specs=[tok(0), tok(1), tok(2), gate, scal, scal],
        out_specs=[tok(0), tok(0), square, tok(0), tok(0), pl.BlockSpec((hb, ncb, 1, 1), lambda i, j: (j, i, 0, 0)),
                   square],
        out_shape=[jax.ShapeDtypeStruct((t, width), F32), jax.ShapeDtypeStruct((t, width), MXU_DTYPE),
                   jax.ShapeDtypeStruct((heads, t, CHUNK), MXU_DTYPE), jax.ShapeDtypeStruct((t, width), MXU_DTYPE),
                   jax.ShapeDtypeStruct((t, width), MXU_DTYPE), jax.ShapeDtypeStruct((heads, nc, 1, 1), F32),
                   jax.ShapeDtypeStruct((heads, t, CHUNK), F32)],
        compiler_params=_params("parallel", "parallel"),
    )(qkv, qkv, qkv, pba, a_log, dt_bias)
    return tuple(prep), tinv


def _gdn_prep_bwd(qkv, pba, a_log, dt_bias, tinv, du, dw, dp, dqd, dkd, dgl, heads, dh):
    t = qkv.shape[0]
    tc, hb = _gdn_blocks(t, heads, PREP_HEADS)
    ncb, nhb, width = tc // CHUNK, heads // hb, heads * dh

    def body(q_ref, k_ref, v_ref, g_ref, al_ref, dt_ref, ti_ref, du_ref, dw_ref, dp_ref, dqd_ref, dkd_ref, dgl_ref,
             dq_ref, dk_ref, dv_ref, dg_ref, dal_ref, ddt_ref):
        first_head = pl.program_id(1) * hb
        b_col, a_col = _gate_columns(g_ref[...], first_head, hb, heads)
        ops = _Ops(True, ti_ref[...].reshape(hb * ncb, CHUNK, CHUNK))

        def prep(q, k, v, b, a, al, dt):
            return _gdn_prep(ops, q, k, v, b, a, _per_chunk(al, ncb), _per_chunk(dt, ncb))[:6]

        _, vjp = jax.vjp(prep, _to_chunks(q_ref, hb, dh), _to_chunks(k_ref, hb, dh), _to_chunks(v_ref, hb, dh),
                         b_col, a_col, al_ref[...], dt_ref[...])
        dq, dk, dv, db, da, dal, ddt = vjp((
            _to_chunks(du_ref, hb, dh), _to_chunks(dw_ref, hb, dh), dp_ref[...].reshape(hb * ncb, CHUNK, CHUNK),
            _to_chunks(dqd_ref, hb, dh), _to_chunks(dkd_ref, hb, dh), dgl_ref[...].reshape(hb * ncb, 1, 1)))
        _from_chunks(dq_ref, dq, hb, dh)
        _from_chunks(dk_ref, dk, hb, dh)
        _from_chunks(dv_ref, dv, hb, dh)
        dal_ref[...] = dal[None]
        ddt_ref[...] = ddt[None]
        lane = lax.broadcasted_iota(jnp.int32, (tc, LANES), 1)
        dgates = jnp.zeros((tc, LANES), F32)
        for h in range(hb):
            rows = slice(h * ncb, (h + 1) * ncb)
            dgates = dgates + jnp.where(lane == first_head + h, db[rows].reshape(tc, 1), 0.0) \
                + jnp.where(lane == heads + first_head + h, da[rows].reshape(tc, 1), 0.0)

        @pl.when(first_head == 0)
        def _():
            dg_ref[...] = dgates

        @pl.when(first_head > 0)
        def _():
            dg_ref[...] += dgates

    def tok(off):
        return pl.BlockSpec((tc, hb * dh), lambda i, j: (i, off * nhb + j))

    gate = pl.BlockSpec((tc, LANES), lambda i, j: (i, 0))
    scal = pl.BlockSpec((hb, 1, 1), lambda i, j: (j, 0, 0))
    part = pl.BlockSpec((1, hb, 1, 1), lambda i, j: (i, j, 0, 0))
    pspec = pl.BlockSpec((hb, tc, CHUNK), lambda i, j: (j, i, 0))
    glspec = pl.BlockSpec((hb, ncb, 1, 1), lambda i, j: (j, i, 0, 0))
    tokf = jax.ShapeDtypeStruct((t, width), F32)
    partf = jax.ShapeDtypeStruct((t // tc, heads, 1, 1), F32)
    return pl.pallas_call(
        body, name="gdn_prep_bwd", grid=(t // tc, nhb),
        in_specs=[tok(0), tok(1), tok(2), gate, scal, scal, pspec, tok(0), tok(0), pspec, tok(0), tok(0), glspec],
        out_specs=[tok(0), tok(0), tok(0), gate, part, part],
        out_shape=[tokf, tokf, tokf, jax.ShapeDtypeStruct((t, LANES), F32), partf, partf],
        compiler_params=_params("parallel", "arbitrary"),
    )(qkv, qkv, qkv, pba, a_log, dt_bias, tinv, du, dw, dp, dqd, dkd, dgl)


def _heads(ref, rows, hb, dh):
    return jnp.stack([ref[rows, h * dh:(h + 1) * dh].astype(F32) for h in range(hb)])


def _put_heads(ref, rows, val, dh):
    for h in range(val.shape[0]):
        ref[rows, h * dh:(h + 1) * dh] = val[h].astype(ref.dtype)


def _gdn_scan_fwd(u, w, p, qd, kd, gl, heads, dh):
    t = u.shape[0]
    tc, hb = _gdn_blocks(t, heads, SCAN_HEADS)
    ncb, nhb = tc // CHUNK, heads // hb
    nc = t // CHUNK

    def body(u_ref, w_ref, p_ref, qd_ref, kd_ref, gl_ref, o_ref, s_ref, state):
        @pl.when(pl.program_id(1) == 0)
        def _():
            state[...] = jnp.zeros_like(state)

        for c in range(ncb):
            rs = slice(c * CHUNK, (c + 1) * CHUNK)
            s_in = state[...]
            s_ref[:, c] = s_in
            o, s_out = _gdn_step(_PLAIN, s_in, _heads(u_ref, rs, hb, dh), _heads(w_ref, rs, hb, dh), p_ref[:, rs, :],
                                 _heads(qd_ref, rs, hb, dh), _heads(kd_ref, rs, hb, dh), gl_ref[:, c])
            _put_heads(o_ref, rs, o, dh)
            state[...] = s_out

    tok = pl.BlockSpec((tc, hb * dh), lambda j, i: (i, j))
    pspec = pl.BlockSpec((hb, tc, CHUNK), lambda j, i: (j, i, 0))
    glspec = pl.BlockSpec((hb, ncb, 1, 1), lambda j, i: (j, i, 0, 0))
    return pl.pallas_call(
        body, name="gdn_scan_fwd", grid=(nhb, t // tc),
        in_specs=[tok, tok, pspec, tok, tok, glspec],
        out_specs=[tok, pl.BlockSpec((hb, ncb, dh, dh), lambda j, i: (j, i, 0, 0))],
        out_shape=[jax.ShapeDtypeStruct((t, heads * dh), F32), jax.ShapeDtypeStruct((heads, nc, dh, dh), F32)],
        scratch_shapes=[pltpu.VMEM((hb, dh, dh), F32)],
        compiler_params=_params("arbitrary", "arbitrary"),
    )(u, w, p, qd, kd, gl)


def _gdn_scan_bwd(u, w, p, qd, kd, gl, states, do, heads, dh):
    t = u.shape[0]
    tc, hb = _gdn_blocks(t, heads, SCAN_HEADS)
    ncb, nhb = tc // CHUNK, heads // hb
    nc, nt = t // CHUNK, t // tc

    def body(u_ref, w_ref, p_ref, qd_ref, kd_ref, gl_ref, s_ref, do_ref,
             du_ref, dw_ref, dp_ref, dqd_ref, dkd_ref, dgl_ref, dstate):
        @pl.when(pl.program_id(1) == 0)
        def _():
            dstate[...] = jnp.zeros_like(dstate)

        for c in reversed(range(ncb)):
            rs = slice(c * CHUNK, (c + 1) * CHUNK)
            _, vjp = jax.vjp(functools.partial(_gdn_step, _DIFF), s_ref[:, c], _heads(u_ref, rs, hb, dh),
                             _heads(w_ref, rs, hb, dh), p_ref[:, rs, :].astype(F32), _heads(qd_ref, rs, hb, dh),
                             _heads(kd_ref, rs, hb, dh), gl_ref[:, c])
            ds, du, dw, dp, dqd, dkd, dgl = vjp((_heads(do_ref, rs, hb, dh), dstate[...]))
            dstate[...] = ds
            _put_heads(du_ref, rs, du, dh)
            _put_heads(dw_ref, rs, dw, dh)
            _put_heads(dqd_ref, rs, dqd, dh)
            _put_heads(dkd_ref, rs, dkd, dh)
            dp_ref[:, rs, :] = dp
            dgl_ref[:, c] = dgl

    tok = pl.BlockSpec((tc, hb * dh), lambda j, i: (nt - 1 - i, j))
    pspec = pl.BlockSpec((hb, tc, CHUNK), lambda j, i: (j, nt - 1 - i, 0))
    glspec = pl.BlockSpec((hb, ncb, 1, 1), lambda j, i: (j, nt - 1 - i, 0, 0))
    sspec = pl.BlockSpec((hb, ncb, dh, dh), lambda j, i: (j, nt - 1 - i, 0, 0))
    tokf = jax.ShapeDtypeStruct((t, heads * dh), F32)
    return pl.pallas_call(
        body, name="gdn_scan_bwd", grid=(nhb, nt),
        in_specs=[tok, tok, pspec, tok, tok, glspec, sspec, tok],
        out_specs=[tok, tok, pspec, tok, tok, glspec],
        out_shape=[tokf, tokf, jax.ShapeDtypeStruct((heads, t, CHUNK), F32), tokf, tokf,
                   jax.ShapeDtypeStruct((heads, nc, 1, 1), F32)],
        scratch_shapes=[pltpu.VMEM((hb, dh, dh), F32)],
        compiler_params=_params("arbitrary", "arbitrary"),
    )(u, w, p, qd, kd, gl, states, do)


def _gdn_post(o, z, gain):
    return _rms(o, gain) * _silu(z)


def _gdn_post_fwd(o, pm, z_col, gain, heads, dh):
    t, wid = o.shape
    tm = _tile(t, 256, 16)
    assert z_col % wid == 0

    def body(o_ref, z_ref, g_ref, y_ref):
        for h in range(heads):
            ls = slice(h * dh, (h + 1) * dh)
            y_ref[:, ls] = _gdn_post(o_ref[:, ls], z_ref[:, ls], g_ref[...]).astype(y_ref.dtype)

    blk = pl.BlockSpec((tm, wid), lambda i: (i, 0))
    return pl.pallas_call(
        body, name="gdn_post_fwd", grid=(t // tm,),
        in_specs=[blk, pl.BlockSpec((tm, wid), lambda i: (i, z_col // wid)), pl.BlockSpec((1, dh), lambda i: (0, 0))],
        out_specs=blk, out_shape=jax.ShapeDtypeStruct((t, wid), MXU_DTYPE), compiler_params=_params("parallel"),
    )(o, pm, gain.reshape(1, dh))


def _gdn_post_bwd(o, pm, z_col, gain, dy, heads, dh):
    t, wid = o.shape
    tm = _tile(t, 256, 16)
    assert z_col % wid == 0

    def body(o_ref, z_ref, g_ref, dy_ref, do_ref, dz_ref, dg_ref):
        dg = jnp.zeros((1, dh), F32)
        for h in range(heads):
            ls = slice(h * dh, (h + 1) * dh)
            _, vjp = jax.vjp(_gdn_post, o_ref[:, ls], z_ref[:, ls], g_ref[...])
            do, dz, dg_h = vjp(dy_ref[:, ls])
            do_ref[:, ls] = do
            dz_ref[:, ls] = dz.astype(dz_ref.dtype)
            dg = dg + dg_h
        first = pl.program_id(0) == 0

        @pl.when(first)
        def _():
            dg_ref[...] = dg

        @pl.when(jnp.logical_not(first))
        def _():
            dg_ref[...] += dg

    blk = pl.BlockSpec((tm, wid), lambda i: (i, 0))
    vec = pl.BlockSpec((1, dh), lambda i: (0, 0))
    do, dz, dg = pl.pallas_call(
        body, name="gdn_post_bwd", grid=(t // tm,),
        in_specs=[blk, pl.BlockSpec((tm, wid), lambda i: (i, z_col // wid)), vec, blk], out_specs=[blk, blk, vec],
        out_shape=[jax.ShapeDtypeStruct((t, wid), F32), jax.ShapeDtypeStruct((t, wid), MXU_DTYPE),
                   jax.ShapeDtypeStruct((1, dh), F32)],
        compiler_params=_params("arbitrary"),
    )(o, pm, gain.reshape(1, dh), dy)
    return do, dz, dg.reshape(dh)


def _attn(ops, q, kv):
    d = q.shape[1]
    hd = d // XATTN_HEADS
    outs = []
    for h in range(XATTN_HEADS):
        qh, kh, vh = q[:, h * hd:(h + 1) * hd], kv[:, h * hd:(h + 1) * hd], kv[:, d + h * hd:d + (h + 1) * hd]
        s = ops.mm(qh, kh, "nt") * (hd ** -0.5)
        e = jnp.exp(s - lax.stop_gradient(jnp.max(s, axis=-1, keepdims=True)))
        outs.append(ops.mm(e / jnp.sum(e, axis=-1, keepdims=True), vh))
    return jnp.concatenate(outs, axis=1)


def _attn_fwd(q, kv):
    t, d = q.shape
    nm = kv.shape[0]
    tm = _tile(t, 512, 16)

    def body(q_ref, kv_ref, o_ref):
        o_ref[...] = _attn(_PLAIN, q_ref[...], kv_ref[...]).astype(o_ref.dtype)

    return pl.pallas_call(
        body, name="xattn_fwd", grid=(t // tm,),
        in_specs=[pl.BlockSpec((tm, d), lambda i: (i, 0)), pl.BlockSpec((nm, 2 * d), lambda i: (0, 0))],
        out_specs=pl.BlockSpec((tm, d), lambda i: (i, 0)),
        out_shape=jax.ShapeDtypeStruct((t, d), MXU_DTYPE), compiler_params=_params("parallel"),
    )(q, kv)


def _attn_bwd(q, kv, do):
    t, d = q.shape
    nm = kv.shape[0]
    tm = _tile(t, 256, 16)

    def body(q_ref, kv_ref, do_ref, dq_ref, dkv_ref):
        _, vjp = jax.vjp(functools.partial(_attn, _DIFF), q_ref[...].astype(F32), kv_ref[...].astype(F32))
        dq, dkv = vjp(do_ref[...].astype(F32))
        dq_ref[...] = dq.astype(dq_ref.dtype)
        first = pl.program_id(0) == 0

        @pl.when(first)
        def _():
            dkv_ref[...] = dkv

        @pl.when(jnp.logical_not(first))
        def _():
            dkv_ref[...] += dkv

    row = pl.BlockSpec((tm, d), lambda i: (i, 0))
    full = pl.BlockSpec((nm, 2 * d), lambda i: (0, 0))
    return pl.pallas_call(
        body, name="xattn_bwd", grid=(t // tm,), in_specs=[row, full, row], out_specs=[row, full],
        out_shape=[jax.ShapeDtypeStruct((t, d), MXU_DTYPE), jax.ShapeDtypeStruct((nm, 2 * d), F32)],
        compiler_params=_params("arbitrary"),
    )(q, kv, do)


def _adamw(name, w, g, m, v):
    shape = w.shape
    cols = shape[-1]
    rows = w.size // cols
    w2, g2, m2, v2 = (a.reshape(rows, cols) for a in (w, g, m, v))
    tr = _tile(rows, max(8, (1 << 18) // cols // 8 * 8), 8)

    def body(w_ref, g_ref, m_ref, v_ref, d_ref, nm_ref, nv_ref):
        gv = g_ref[...]
        nm = ADAM_B1 * m_ref[...] + (1.0 - ADAM_B1) * gv
        nv = ADAM_B2 * v_ref[...] + (1.0 - ADAM_B2) * jnp.square(gv)
        m_hat = nm / (1.0 - ADAM_B1 ** ADAM_STEP)
        v_hat = nv / (1.0 - ADAM_B2 ** ADAM_STEP)
        d_ref[...] = -ADAM_LR * (m_hat / (jnp.sqrt(v_hat) + ADAM_EPS) + ADAM_WD * w_ref[...])
        nm_ref[...] = nm
        nv_ref[...] = nv

    blk = pl.BlockSpec((tr, cols), lambda i: (i, 0))
    out = jax.ShapeDtypeStruct((rows, cols), F32)
    d, nm, nv = pl.pallas_call(
        body, name=name, grid=(rows // tr,), in_specs=[blk] * 4, out_specs=[blk] * 3, out_shape=[out] * 3,
        compiler_params=_params("parallel"),
    )(w2, g2, m2, v2)
    return d.reshape(shape), nm.reshape(shape), nv.reshape(shape)


def _carried(carry, landed):
    def mm(name, *args, **kwargs):
        if name not in carry:
            return _matmul(name, *args, **kwargs)
        out, landed[name] = _matmul(name, *args, comm=carry[name], **kwargs)
        return out

    return mm


def _layer_fwd(x, mem, p, heads, dh, carry):
    wid = heads * dh
    sc = x.shape[1] - wid
    s, landed = {"x0": x}, {}
    mm = _carried(carry, landed)
    s["h1"] = _rms_fwd("rms_mix", x, p["mix_norm"])
    s["pm"] = pm = mm("mm_mix_in", s["h1"], p["wmain"], "nn", F32)
    s["pba"] = mm("mm_mix_ba", s["h1"], p["wba"], "nn", F32)
    s["qkv"] = _conv_fwd("conv_gdn", pm, 0, p["gdn_conv"], 0, 3 * wid, F32)
    s["prep"], s["tinv"] = _gdn_prep_fwd(s["qkv"], s["pba"], p["a_log"], p["dt_bias"], heads, dh)
    s["o"], s["states"] = _gdn_scan_fwd(*s["prep"], heads, dh)
    y_gdn = _gdn_post_fwd(s["o"], pm, 3 * wid, p["gdn_out_norm"], heads, dh)
    y_sc = _conv_fwd("conv_sc", pm, 4 * wid + sc, p["sc_conv"], 0, sc, MXU_DTYPE, xb=pm, xb_col=4 * wid + 2 * sc,
                     gate=pm, gate_col=4 * wid)
    s["ycat"] = jnp.concatenate([y_gdn, y_sc], axis=1)
    s["x1"] = x1 = mm("mm_mix_out", s["ycat"], p["wout"], "nn", F32, add=x)
    s["h2"] = _rms_fwd("rms_xattn", x1, p["xattn_norm"])
    s["q"] = mm("mm_xq", s["h2"], p["wq"], "nn", MXU_DTYPE)
    s["memn"] = _rms_fwd("rms_mem", mem, p["mem_norm"])
    s["kv"] = mm("mm_xkv", s["memn"], p["wkv"], "nn", MXU_DTYPE)
    s["ao"] = _attn_fwd(s["q"], s["kv"])
    s["x2"] = x2 = mm("mm_xo", s["ao"], p["wo"], "nn", F32, add=x1)
    s["h3"] = _rms_fwd("rms_ffn", x2, p["ffn_norm"])
    s["upre"] = mm("mm_ffn_up", s["h3"], p["wup"], "nn", MXU_DTYPE)
    s["uc"] = _conv_fwd("conv_ffn", s["upre"], 0, p["ffn_conv"], 0, s["upre"].shape[1], MXU_DTYPE)
    s["act"] = _swiglu_fwd(s["uc"])
    return mm("mm_ffn_down", s["act"], p["wdown"], "nn", F32, add=x2), s, landed


def _layer_bwd(dx3, dx3b, mem, s, p, heads, dh, reduce):
    wid = heads * dh
    sc = dx3.shape[1] - wid
    pm = s["pm"]
    g = {}

    mm = reduce.carried if reduce is not None else _matmul
    da = mm("mm_ffn_down_dx", dx3b, p["wdown"], "nt", MXU_DTYPE)
    g["wdown"] = mm("mm_ffn_down_dw", s["act"], dx3b, "tn", WIRE_DTYPE)
    du = _swiglu_bwd(s["uc"], da)
    dupre, _, _, g["ffn_conv"] = _conv_bwd("conv_ffn_bwd", s["upre"], 0, p["ffn_conv"], 0, du, 0, du.shape[1],
                                           MXU_DTYPE)
    dh3 = mm("mm_ffn_up_dx", dupre, p["wup"], "nt", F32)
    g["wup"] = mm("mm_ffn_up_dw", s["h3"], dupre, "tn", WIRE_DTYPE)
    dx2, dx2b, g["ffn_norm"] = _rms_bwd("rms_ffn_bwd", s["x2"], p["ffn_norm"], dh3, dx3)
    dao = mm("mm_xo_dx", dx2b, p["wo"], "nt", MXU_DTYPE)
    g["wo"] = mm("mm_xo_dw", s["ao"], dx2b, "tn", WIRE_DTYPE)
    dq, dkv = _attn_bwd(s["q"], s["kv"], dao)
    dh2 = mm("mm_xq_dx", dq, p["wq"], "nt", F32)
    g["wq"] = mm("mm_xq_dw", s["h2"], dq, "tn", WIRE_DTYPE)
    dmemn = mm("mm_xkv_dx", dkv, p["wkv"], "nt", F32)
    g["wkv"] = mm("mm_xkv_dw", s["memn"], dkv, "tn", WIRE_DTYPE)
    _, _, g["mem_norm"] = _rms_bwd("rms_mem_bwd", mem, p["mem_norm"], dmemn)
    dx1, dx1b, g["xattn_norm"] = _rms_bwd("rms_xattn_bwd", s["x1"], p["xattn_norm"], dh2, dx2)
    dycat = mm("mm_mix_out_dx", dx1b, p["wout"], "nt", F32)
    g["wout"] = mm("mm_mix_out_dw", s["ycat"], dx1b, "tn", WIRE_DTYPE)
    d_c, d_h, d_b, g["sc_conv"] = _conv_bwd("conv_sc_bwd", pm, 4 * wid + sc, p["sc_conv"], 0, dycat, wid, sc,
                                             MXU_DTYPE, xb=pm, xb_col=4 * wid + 2 * sc, gate=pm, gate_col=4 * wid)
    do, dz, g["gdn_out_norm"] = _gdn_post_bwd(s["o"], pm, 3 * wid, p["gdn_out_norm"], dycat, heads, dh)
    dprep = _gdn_scan_bwd(*s["prep"], s["states"], do, heads, dh)
    dqc, dkc, dvc, dpba, dal, ddt = _gdn_prep_bwd(s["qkv"], s["pba"], p["a_log"], p["dt_bias"], s["tinv"], *dprep,
                                                  heads, dh)
    g["a_log"], g["dt_bias"] = jnp.sum(dal, axis=0), jnp.sum(ddt, axis=0)
    dqkv, _, _, g["gdn_conv"] = _conv_bwd("conv_gdn_bwd", pm, 0, p["gdn_conv"], 0,
                                          jnp.concatenate([dqc, dkc, dvc], axis=1), 0, 3 * wid, MXU_DTYPE)
    dpm = jnp.concatenate([dqkv, dz, d_b, d_c, d_h], axis=1)
    dpba = dpba.astype(MXU_DTYPE)
    dh1 = mm("mm_mix_in_dx", dpm, p["wmain"], "nt", F32)
    dh1 = mm("mm_mix_ba_dx", dpba, p["wba"], "nt", F32, add=dh1)
    g["wmain"] = mm("mm_mix_in_dw", s["h1"], dpm, "tn", WIRE_DTYPE)
    g["wba"] = mm("mm_mix_ba_dw", s["h1"], dpba, "tn", WIRE_DTYPE)
    dx0, dx0b, g["mix_norm"] = _rms_bwd("rms_mix_bwd", s["x0"], p["mix_norm"], dh1, dx1)
    return dx0, dx0b, g


def _layer_params(win, wout, wq, wk, wv, wo, wup, wdown, small, heads, dh):
    wid = heads * dh
    p = dict(small)
    p.update({
        "a_log": small["a_log"].reshape(heads, 1, 1), "dt_bias": small["dt_bias"].reshape(heads, 1, 1),
        "wmain": jnp.concatenate([win[:, :4 * wid], win[:, 4 * wid + 2 * heads:]], axis=1),
        "wba": jnp.pad(win[:, 4 * wid:4 * wid + 2 * heads], ((0, 0), (0, LANES - 2 * heads))),
        "wout": wout, "wq": wq, "wkv": jnp.concatenate([wk, wv], axis=1), "wo": wo, "wup": wup, "wdown": wdown,
    })
    return p


_ANY =pl.BlockSpec(memory_space=pl.ANY)
_VMEM = pl.BlockSpec(memory_space=pltpu.VMEM)


def _mesh_pos():
    return lax.axis_index("x"), lax.axis_index("y"), lax.axis_index("c")


def _other_chips(x, y):
    return [(1 - x, y), (x, 1 - y), (1 - x, 1 - y)]


def _push(src, dst, sems, k, to):
    return pltpu.make_async_remote_copy(src_ref=src, dst_ref=dst, send_sem=sems[0].at[k], recv_sem=sems[1].at[k],
                                        device_id=to, device_id_type=MESH)


def _sem_pairs(n):
    return [pltpu.SemaphoreType.DMA((n,)), pltpu.SemaphoreType.DMA((n,))]


class _Comm:
    def __init__(self, ins, outs, n_sems, start, finish, aliases=None):
        self.ins, self.outs, self.n_sems, self.start, self.finish = list(ins), list(outs), n_sems, start, finish
        self.aliases = aliases or {}


def _run_comm(name, comm):
    n_in, n_out = len(comm.ins), len(comm.outs)

    def body(*refs):
        parts = (refs[:n_in], refs[n_in:n_in + n_out], refs[n_in + n_out:])
        comm.start(*parts)
        comm.finish(*parts)

    return pl.pallas_call(
        body, name=name, in_specs=[_ANY] * n_in, out_specs=[_ANY] * n_out, out_shape=comm.outs,
        scratch_shapes=_sem_pairs(comm.n_sems), input_output_aliases=comm.aliases,
    )(*comm.ins)


def _allgather_comm(srcs):
    n = len(srcs)

    def first(src, out, sems):
        x, y, c = _mesh_pos()
        own, sends = [], []
        for t in range(n):
            half = src[t].shape[0] // 2
            mine = pl.ds(c * half, half)
            own.append(_push(src[t], out[t].at[2 * x + y], sems, 7 * t + 6, (x, y, 1 - c)))
            sends += [_push(src[t].at[mine], out[t].at[2 * x + y, mine], sems, 7 * t + k, (cx, cy, c))
                      for k, (cx, cy) in enumerate(_other_chips(x, y))]
        return own, sends

    def start(src, out, sems):
        own, sends = first(src, out, sems)
        for cp in own + sends:
            cp.start()

    def finish(src, out, sems):
        x, y, c = _mesh_pos()
        sibling = (x, y, 1 - c)
        own, sends = first(src, out, sems)
        fwds, relayed = [], []
        for t in range(n):
            half = src[t].shape[0] // 2
            for k, (cx, cy) in enumerate(_other_chips(x, y)):
                here = out[t].at[2 * cx + cy, pl.ds(c * half, half)]
                there = out[t].at[2 * cx + cy, pl.ds((1 - c) * half, half)]
                _push(here, here, sems, 7 * t + k, sibling).wait_recv()
                fwds.append(_push(here, here, sems, 7 * t + 3 + k, sibling))
                fwds[-1].start()
                relayed.append(_push(there, there, sems, 7 * t + 3 + k, sibling))
        for cp in relayed + own:
            cp.wait_recv()
        for cp in own + sends + fwds:
            cp.wait_send()

    return _Comm(srcs, [jax.ShapeDtypeStruct((N_CHIPS,) + s.shape, s.dtype) for s in srcs], 7 * n, start, finish)


def _start_wait(build):
    def start(src, out, sems):
        for cp in build(src, out, sems):
            cp.start()

    def finish(src, out, sems):
        for cp in build(src, out, sems):
            cp.wait()

    return start, finish


def _sibling_exchange_comm(bufs):
    def build(src, out, sems):
        x, y, c = _mesh_pos()
        return [_push(src[t].at[1 - c], out[t], sems, t, (x, y, 1 - c)) for t in range(len(bufs))]

    start, finish = _start_wait(build)
    return _Comm(bufs, [jax.ShapeDtypeStruct(b.shape[1:], b.dtype) for b in bufs], len(bufs), start, finish)


def _chip_exchange_comm(bufs):
    def build(src, out, sems):
        x, y, c = _mesh_pos()
        return [_push(src[t].at[2 * cx + cy], out[t].at[k], sems, 3 * t + k, (cx, cy, c))
                for t in range(len(bufs)) for k, (cx, cy) in enumerate(_other_chips(x, y))]

    start, finish = _start_wait(build)
    return _Comm(bufs, [jax.ShapeDtypeStruct((3,) + b.shape[1:], b.dtype) for b in bufs], 3 * len(bufs), start, finish)


def _sibling_share_comm(bufs):
    def build(src, out, sems):
        x, y, c = _mesh_pos()
        return [_push(src[t].at[c], out[t].at[c], sems, t, (x, y, 1 - c)) for t in range(len(bufs))]

    start, finish = _start_wait(build)
    return _Comm(bufs, [jax.ShapeDtypeStruct(b.shape, b.dtype) for b in bufs], len(bufs), start, finish,
                 aliases={t: t for t in range(len(bufs))})


def _allreduce_small(v):
    r, lanes = v.shape

    def body(v_ref, sum_ref, gath, send_sems, recv_sems):
        x, y, c = _mesh_pos()
        me = 4 * x + 2 * y + c
        gath[me] = v_ref[...]
        copies = []
        for rel in range(1, N_DEV):
            peer = tuple(1 - p if (rel >> b) & 1 else p for p, b in ((x, 2), (y, 1), (c, 0)))
            copies.append(pltpu.make_async_remote_copy(
                src_ref=v_ref, dst_ref=gath.at[me], send_sem=send_sems.at[rel - 1], recv_sem=recv_sems.at[rel - 1],
                device_id=peer, device_id_type=MESH))
        for cp in copies:
            cp.start()
        for cp in copies:
            cp.wait()
        total = gath[0]
        for k in range(1, N_DEV):
            total = total + gath[k]
        sum_ref[...] = total

    return pl.pallas_call(
        body, name="allreduce_small", in_specs=[_VMEM], out_specs=_VMEM,
        out_shape=jax.ShapeDtypeStruct((r, lanes), F32),
        scratch_shapes=[pltpu.VMEM((N_DEV, r, lanes), F32)] + _sem_pairs(N_DEV - 1),
        compiler_params=pltpu.CompilerParams(vmem_limit_bytes=VMEM_LIMIT),
    )(v)


def _sum_tile(rows, width):
    return _tile(rows, max(16, (1 << 19) // width // 16 * 16), 16)


def _sum_sibling(x, recv, core):
    _, n, w = x.shape
    tr = _sum_tile(n, w)

    def body(idx_ref, x_ref, r_ref, o_ref):
        o_ref[...] = (x_ref[...].astype(F32) + r_ref[...].astype(F32)).astype(o_ref.dtype)

    row = pl.BlockSpec((tr, w), lambda i, idx: (i, 0))
    return pl.pallas_call(
        body, name="rs_sum_sibling",
        grid_spec=pltpu.PrefetchScalarGridSpec(
            num_scalar_prefetch=1, grid=(n // tr,),
            in_specs=[pl.BlockSpec((None, tr, w), lambda i, idx: (idx[0], i, 0)), row], out_specs=row),
        out_shape=jax.ShapeDtypeStruct((n, w), x.dtype), compiler_params=_params("parallel"),
    )(core.reshape(1), x, recv)


def _sum_chips(s, recv, chip, core):
    _, m, w = s.shape
    tr = _sum_tile(m, w)

    def body(idx_ref, s_ref, r0_ref, r1_ref, r2_ref, o_ref):
        o_ref[...] = ((s_ref[...].astype(F32) + r0_ref[...].astype(F32)) + r1_ref[...].astype(F32)) \
            + r2_ref[...].astype(F32)

    def got(k):
        return pl.BlockSpec((None, tr, w), lambda i, idx: (k, i, 0))

    return pl.pallas_call(
        body, name="rs_sum_chips",
        grid_spec=pltpu.PrefetchScalarGridSpec(
            num_scalar_prefetch=1, grid=(m // tr,),
            in_specs=[pl.BlockSpec((None, tr, w), lambda i, idx: (idx[0], i, 0)), got(0), got(1), got(2)],
            out_specs=pl.BlockSpec((None, tr, w), lambda i, idx: (idx[1], i, 0))),
        out_shape=jax.ShapeDtypeStruct((2, m, w), F32), compiler_params=_params("parallel"),
    )(jnp.stack([chip, core]), s, recv, recv, recv)


_ROWS = ("w_mix_out", "w_xq", "w_xk", "w_xv", "w_xo", "w_ffn_down")
_CONVS = ("gdn_conv", "sc_conv", "ffn_conv")
_REPLICATED = ("mix_norm", "gdn_a_log", "gdn_dt_bias", "gdn_out_norm", "xattn_norm", "mem_norm", "ffn_norm",
               "final_norm")
_WEIGHTS = ("mix_norm", "w_mix_in", "gdn_conv", "gdn_a_log", "gdn_dt_bias", "gdn_out_norm", "sc_conv", "w_mix_out",
            "xattn_norm", "mem_norm", "w_xq", "w_xk", "w_xv", "w_xo", "ffn_norm", "w_ffn_up", "ffn_conv",
            "w_ffn_down", "final_norm")


def _pad_rows(flat, groups):
    unit = groups * 16 * LANES
    p = flat.shape[-1]
    pad = -p % unit
    if pad:
        flat = jnp.pad(flat, [(0, 0)] * (flat.ndim - 1) + [(0, pad)])
    return flat.reshape(flat.shape[:-1] + (groups, (p + pad) // (groups * LANES), LANES))


def _split_flat(flat, shapes):
    out, off = [], 0
    for shp in shapes:
        size = 1
        for n in shp:
            size *= n
        out.append(flat[..., off:off + size].reshape(flat.shape[:-1] + tuple(shp)))
        off += size
    return out


def _by_chip(g, axis):
    rows, cols = g.shape
    if axis == 0:
        return g.reshape(N_CHIPS, rows // N_CHIPS, cols)
    return g.reshape(rows, N_CHIPS, cols // N_CHIPS).transpose(1, 0, 2)


def _halves_by_chip(g):
    _, rows, w = g.shape
    return g.astype(WIRE_DTYPE).reshape(N_CHIPS, 2, rows // 2, w).transpose(1, 0, 2, 3)


class _ReduceScatter:
    STAGES = ("mm_ffn_down_dx", "mm_ffn_up_dx", "mm_ffn_up_dw", "mm_mix_in_dx")

    def __init__(self, bufs, chip, core):
        self.bufs, self.chip, self.core = list(bufs), chip, core
        self.sums = self.from_chips = self.reduced = self.result = None

    def comm(self, stage):
        if stage == self.STAGES[0]:
            return _sibling_exchange_comm(self.bufs)
        if stage == self.STAGES[1]:
            return _chip_exchange_comm(self.sums[-1:])
        if stage == self.STAGES[2]:
            return _chip_exchange_comm(self.sums[:-1])
        return _sibling_share_comm(self.reduced)

    def landed(self, stage, outs):
        if stage == self.STAGES[0]:
            self.sums = [_sum_sibling(b.reshape(2, -1, b.shape[-1]), r.reshape(-1, r.shape[-1]), self.core)
                         .reshape(r.shape) for b, r in zip(self.bufs, outs)]
        elif stage == self.STAGES[1]:
            self.from_chips = list(outs)
        elif stage == self.STAGES[2]:
            self.reduced = [_sum_chips(s, r, self.chip, self.core)
                            for s, r in zip(self.sums, list(outs) + self.from_chips)]
        else:
            self.result = list(outs)

    def carried(self, name, *args, **kwargs):
        if name not in self.STAGES:
            return _matmul(name, *args, **kwargs)
        out, outs = _matmul(name, *args, comm=self.comm(name), **kwargs)
        self.landed(name, outs)
        return out

    def run_alone(self):
        for stage, name in zip(self.STAGES, ("rs_sibling_exchange", "rs_chip_exchange_rows", "rs_chip_exchange_cols",
                                             "rs_sibling_share")):
            self.landed(stage, _run_comm(name, self.comm(stage)))


def kernel(x, mem, mix_norm, w_mix_in, gdn_conv, gdn_a_log, gdn_dt_bias, gdn_out_norm, sc_conv, w_mix_out, xattn_norm, mem_norm, w_xq, w_xk, w_xv, w_xo, ffn_norm, w_ffn_up, ffn_conv, w_ffn_down, final_norm, loss_target, m_mix_norm, m_w_mix_in, m_gdn_conv, m_gdn_a_log, m_gdn_dt_bias, m_gdn_out_norm, m_sc_conv, m_w_mix_out, m_xattn_norm, m_mem_norm, m_w_xq, m_w_xk, m_w_xv, m_w_xo, m_ffn_norm, m_w_ffn_up, m_ffn_conv, m_w_ffn_down, m_final_norm, v_mix_norm, v_w_mix_in, v_gdn_conv, v_gdn_a_log, v_gdn_dt_bias, v_gdn_out_norm, v_sc_conv, v_w_mix_out, v_xattn_norm, v_mem_norm, v_w_xq, v_w_xk, v_w_xv, v_w_xo, v_ffn_norm, v_w_ffn_up, v_ffn_conv, v_w_ffn_down, v_final_norm):
    w = dict(zip(_WEIGHTS, (mix_norm, w_mix_in, gdn_conv, gdn_a_log, gdn_dt_bias, gdn_out_norm, sc_conv, w_mix_out,
                            xattn_norm, mem_norm, w_xq, w_xk, w_xv, w_xo, ffn_norm, w_ffn_up, ffn_conv, w_ffn_down,
                            final_norm)))
    m = dict(zip(_WEIGHTS, (m_mix_norm, m_w_mix_in, m_gdn_conv, m_gdn_a_log, m_gdn_dt_bias, m_gdn_out_norm, m_sc_conv,
                            m_w_mix_out, m_xattn_norm, m_mem_norm, m_w_xq, m_w_xk, m_w_xv, m_w_xo, m_ffn_norm,
                            m_w_ffn_up, m_ffn_conv, m_w_ffn_down, m_final_norm)))
    v = dict(zip(_WEIGHTS, (v_mix_norm, v_w_mix_in, v_gdn_conv, v_gdn_a_log, v_gdn_dt_bias, v_gdn_out_norm, v_sc_conv,
                            v_w_mix_out, v_xattn_norm, v_mem_norm, v_w_xq, v_w_xk, v_w_xv, v_w_xo, v_ffn_norm,
                            v_w_ffn_up, v_ffn_conv, v_w_ffn_down, v_final_norm)))
    core = lax.axis_index("c")
    chip = 2 * lax.axis_index("x") + lax.axis_index("y")
    depth, heads = gdn_a_log.shape
    dh = gdn_out_norm.shape[1]
    d, wid = x.shape[2], heads * dh

    row_sizes = [w[n].shape[1] for n in _ROWS]
    row_offs = [sum(row_sizes[:k]) for k in range(len(_ROWS))]
    src_in, src_up = w_mix_in.astype(WIRE_DTYPE), w_ffn_up.astype(WIRE_DTYPE)
    src_rows = jnp.concatenate([w[n] for n in _ROWS], axis=1).astype(WIRE_DTYPE)
    src_convs = _pad_rows(jnp.concatenate([w[n].reshape(-1) for n in _CONVS]), 2)
    g_in, g_up, g_rows, g_convs = _run_comm(
        "allgather_first", _allgather_comm([src_in[0], src_up[0], src_rows[0], src_convs]))
    conv_full = {n: jnp.moveaxis(part, 0, 2).reshape(depth, part.shape[2], -1)
                 for n, part in zip(_CONVS, _split_flat(g_convs.reshape(N_CHIPS, -1), [w[n].shape for n in _CONVS]))}

    def params(l, g_in, g_up, g_rows):
        cols = lambda g: jnp.concatenate([g[j] for j in range(N_CHIPS)], axis=1)
        rows = lambda k: jnp.concatenate(
            [g_rows[j, row_offs[k]:row_offs[k] + row_sizes[k]] for j in range(N_CHIPS)], axis=0)
        small = {"mix_norm": mix_norm[l], "xattn_norm": xattn_norm[l], "mem_norm": mem_norm[l],
                 "ffn_norm": ffn_norm[l], "gdn_out_norm": gdn_out_norm[l], "a_log": gdn_a_log[l],
                 "dt_bias": gdn_dt_bias[l], "gdn_conv": conv_full["gdn_conv"][l], "sc_conv": conv_full["sc_conv"][l],
                 "ffn_conv": conv_full["ffn_conv"][l]}
        return _layer_params(cols(g_in), rows(0), rows(1), rows(2), rows(3), rows(4), cols(g_up), rows(5), small,
                             heads, dh)

    xl, mem_l = x[0], mem[0]
    layers, saved = [], []
    for l in range(depth):
        layers.append(params(l, g_in, g_up, g_rows))
        carry = {}
        if l + 1 < depth:
            carry = {"mm_mix_in": _allgather_comm([src_up[l + 1]]), "mm_ffn_up": _allgather_comm([src_rows[l + 1]]),
                     "mm_ffn_down": _allgather_comm([src_in[l + 1]])}
        xl, s, landed = _layer_fwd(xl, mem_l, layers[l], heads, dh, carry)
        saved.append(s)
        if carry:
            (g_up,), (g_rows,), (g_in,) = landed["mm_mix_in"], landed["mm_ffn_up"], landed["mm_ffn_down"]
    loss_row, dx, dxb, g_final = _final_loss(xl, final_norm, loss_target[0])

    def by_chip(g):
        g_win = jnp.concatenate([g["wmain"][:, :4 * wid], g["wba"][:, :2 * heads], g["wmain"][:, 4 * wid:]], axis=1)
        parts = (g["wout"], g["wq"], g["wkv"][:, :d], g["wkv"][:, d:], g["wo"], g["wdown"])
        return [_halves_by_chip(_by_chip(g_win, 1)), _halves_by_chip(_by_chip(g["wup"], 1)),
                _halves_by_chip(jnp.concatenate([_by_chip(p, 0) for p in parts], axis=1))]

    per_layer, shards, reduce = [None] * depth, [None] * depth, None
    for l in reversed(range(depth)):
        dx, dxb, per_layer[l] = _layer_bwd(dx, dxb, mem_l, saved[l], layers[l], heads, dh, reduce)
        if reduce is not None:
            shards[l + 1] = reduce.result
        reduce = _ReduceScatter(by_chip(per_layer[l]), chip, core)
    reduce.run_alone()
    shards[0] = reduce.result
    g_in_s, g_up_s, g_rows_s = (jnp.stack([s[t].reshape(-1, s[t].shape[-1]) for s in shards]) for t in range(3))
    grad = {"w_mix_in": g_in_s, "w_ffn_up": g_up_s}
    for n, off, size in zip(_ROWS, row_offs, row_sizes):
        grad[n] = g_rows_s[:, off:off + size]

    stack = lambda k: jnp.stack([g[k] for g in per_layer])
    small_g = {"mix_norm": stack("mix_norm"), "gdn_a_log": stack("a_log").reshape(depth, heads),
               "gdn_dt_bias": stack("dt_bias").reshape(depth, heads), "gdn_out_norm": stack("gdn_out_norm"),
               "xattn_norm": stack("xattn_norm"), "mem_norm": stack("mem_norm"), "ffn_norm": stack("ffn_norm"),
               "final_norm": g_final, "gdn_conv": stack("gdn_conv"), "sc_conv": stack("sc_conv"),
               "ffn_conv": stack("ffn_conv")}
    names = _REPLICATED + _CONVS
    small = jnp.concatenate([small_g[n].reshape(-1) for n in names] + [loss_row[0, :1]])
    small_sum = _allreduce_small(_pad_rows(small, 1)[0]).reshape(-1)
    parts = _split_flat(small_sum, [small_g[n].shape for n in names] + [(1,)])
    g_rep = dict(zip(_REPLICATED, parts[:len(_REPLICATED)]))
    for n, part in zip(_CONVS, parts[len(_REPLICATED):-1]):
        grad[n] = lax.dynamic_slice_in_dim(part, chip * w[n].shape[2], w[n].shape[2], axis=2)
    loss = parts[-1][0]

    delta, new_m, new_v = {}, {}, {}
    for n in ("w_mix_in", "w_ffn_up") + _ROWS + _CONVS:
        delta[n], new_m[n], new_v[n] = _adamw("adamw_" + n, w[n], grad[n], m[n], v[n])
    pack_rep = lambda t: _pad_rows(jnp.concatenate([t[n].reshape(-1) for n in _REPLICATED]), 1)[0]
    outs = _adamw("adamw_replicated", pack_rep(w), pack_rep(g_rep), pack_rep(m), pack_rep(v))
    shapes = [w[n].shape for n in _REPLICATED]
    for tgt, packed_out in zip((delta, new_m, new_v), outs):
        tgt.update(zip(_REPLICATED, _split_flat(packed_out.reshape(-1), shapes)))
    grad.update(g_rep)
    return (loss, dx[None], *[grad[n] for n in _WEIGHTS], *[delta[n] for n in _WEIGHTS],
            *[new_m[n] for n in _WEIGHTS], *[new_v[n] for n in _WEIGHTS])
```

```python
import functools

import jax
import jax.numpy as jnp
from jax import lax
from jax.experimental import pallas as pl
from jax.experimental.pallas import tpu as pltpu

F32 = jnp.float32
MXU_DTYPE = jnp.bfloat16
WIRE_DTYPE = jnp.bfloat16
SOLVE_PRECISION = lax.Precision.HIGH
EPS = 1e-6
CHUNK = 64
XATTN_HEADS = 4
LANES = 128
HALO = 16
EW_ROWS, EW_COLS = 256, 2816
VMEM_LIMIT = 52 * 1024 * 1024
ADAM_LR, ADAM_B1, ADAM_B2, ADAM_EPS, ADAM_WD, ADAM_STEP = 0.001, 0.9, 0.999, 1e-08, 0.01, 10
MESH = pl.DeviceIdType.MESH
N_CHIPS = 4
N_DEV = 8

_DIMS = {
    "nn": (((1,), (0,)), ((), ())),
    "nt": (((1,), (1,)), ((), ())),
    "tn": (((0,), (0,)), ((), ())),
}


def _tile(n, pref, align=LANES):
    if n <= pref:
        return n
    t = (pref // align) * align
    while t >= align:
        if n % t == 0:
            return t
        t -= align
    return n


def _params(*sem):
    return pltpu.CompilerParams(dimension_semantics=sem, vmem_limit_bytes=VMEM_LIMIT)


def _dot(a, b, form, hi=False):
    (ca, cb), _ = _DIMS[form]
    dims = (((ca[0] + 1,), (cb[0] + 1,)), ((0,), (0,))) if a.ndim == 3 else _DIMS[form]
    if hi:
        return lax.dot_general(a.astype(F32), b.astype(F32), dims, precision=SOLVE_PRECISION,
                               preferred_element_type=F32)
    return lax.dot_general(a.astype(MXU_DTYPE), b.astype(MXU_DTYPE), dims, preferred_element_type=F32)


@functools.partial(jax.custom_vjp, nondiff_argnums=(2, 3))
def _dot_d(a, b, form, hi):
    return _dot(a, b, form, hi)


def _dot_d_fwd(a, b, form, hi):
    return _dot(a, b, form, hi), (a, b)


def _dot_d_bwd(form, hi, res, g):
    a, b = res
    if form == "nn":
        da, db = _dot_d(g, b, "nt", hi), _dot_d(a, g, "tn", hi)
    elif form == "nt":
        da, db = _dot_d(g, b, "nn", hi), _dot_d(g, a, "tn", hi)
    else:
        da, db = _dot_d(b, g, "nt", hi), _dot_d(a, g, "nn", hi)
    return da.astype(a.dtype), db.astype(b.dtype)


_dot_d.defvjp(_dot_d_fwd, _dot_d_bwd)


def _tri_inv_impl(a, mmh):
    c = a.shape[-1]
    r = lax.broadcasted_iota(jnp.int32, (c, c), 0)
    s = lax.broadcasted_iota(jnp.int32, (c, c), 1)
    eye = (r == s).astype(F32)
    diag_blk = (r // 16) == (s // 16)
    d = jnp.where(diag_blk, a, 0.0)
    low = a - d
    d2 = mmh(d, d)
    d4 = mmh(d2, d2)
    d8 = mmh(d4, d4)
    td = mmh(mmh(mmh(eye - d, eye + d2), eye + d4), eye + d8)
    n = mmh(td, low)
    acc = eye - n
    p = n
    pw = 1
    while 2 * pw < c // 16:
        p = mmh(p, p)
        pw *= 2
        acc = mmh(acc, eye + p)
    return mmh(acc, td)


def _mmh_plain(a, b):
    return _dot(a, b, "nn", True)


@jax.custom_vjp
def _tri_inv_known(a, t):
    return t


def _tri_inv_known_fwd(a, t):
    return t, t


def _tri_inv_known_bwd(t, g):
    return -_dot(_dot(t, g, "tn", True), t, "nt", True), jnp.zeros_like(t)


_tri_inv_known.defvjp(_tri_inv_known_fwd, _tri_inv_known_bwd)


class _Ops:
    def __init__(self, diff, tinv=None):
        self.diff, self.tinv = diff, tinv

    def mm(self, a, b, form="nn"):
        return _dot_d(a, b, form, False) if self.diff else _dot(a, b, form, False)

    def mmh(self, a, b, form="nn"):
        return _dot_d(a, b, form, True) if self.diff else _dot(a, b, form, True)

    def tri_inv(self, a):
        return _tri_inv_known(a, self.tinv) if self.diff else _tri_inv_impl(a, _mmh_plain)


_PLAIN = _Ops(False)
_DIFF = _Ops(True)


def _sigmoid(x):
    return 1.0 / (1.0 + jnp.exp(-x))


def _silu(x):
    return x * _sigmoid(x)


def _softplus(x):
    return jnp.maximum(x, 0.0) + jnp.log(1.0 + jnp.exp(-jnp.abs(x)))


def _rms(x, g):
    return x * lax.rsqrt(jnp.mean(x * x, axis=-1, keepdims=True) + EPS) * g


def _matmul_tiles(m, n, k, form):
    if k <= 2048:
        return _tile(m, 1024), _tile(n, 1408), k
    if k <= 8192:
        return _tile(m, 1024 if form == "nn" else 512), _tile(n, 512), k
    return _tile(m, 1024), _tile(n, 1024), _tile(k, 2816)


def _matmul(name, a, b, form, out_dtype, add=None, comm=None):
    if form == "nn":
        (m, k), (k2, n) = a.shape, b.shape
    elif form == "nt":
        (m, k), (n, k2) = a.shape, b.shape
    else:
        (k, m), (k2, n) = a.shape, b.shape
    assert k == k2, (name, a.shape, b.shape, form)
    tm, tn, tk = _matmul_tiles(m, n, k, form)
    nk = k // tk
    out_bytes = tm * tn * (jnp.dtype(out_dtype).itemsize + (4 if add is not None else 0))
    vmem = 2 * (tm * tk * a.dtype.itemsize + tk * tn * b.dtype.itemsize + out_bytes) + (tm * tn * 4 if nk > 1 else 0)
    assert vmem <= VMEM_LIMIT, (name, tm, tn, tk, vmem)
    if form == "nn":
        a_spec = pl.BlockSpec((tm, tk), lambda i, j, kk: (i, kk))
        b_spec = pl.BlockSpec((tk, tn), lambda i, j, kk: (kk, j))
    elif form == "nt":
        a_spec = pl.BlockSpec((tm, tk), lambda i, j, kk: (i, kk))
        b_spec = pl.BlockSpec((tn, tk), lambda i, j, kk: (j, kk))
    else:
        a_spec = pl.BlockSpec((tk, tm), lambda i, j, kk: (kk, i))
        b_spec = pl.BlockSpec((tk, tn), lambda i, j, kk: (kk, j))
    o_spec = pl.BlockSpec((tm, tn), lambda i, j, kk: (i, j))
    has_add = add is not None
    grid = (m // tm, n // tn, nk)
    n_in = 3 if has_add else 2
    c_in, c_out = (len(comm.ins), len(comm.outs)) if comm is not None else (0, 0)

    def body(*refs):
        a_ref, b_ref = refs[0], refs[1]
        add_ref = refs[2] if has_add else None
        o_ref = refs[n_in + c_in]
        pids = [pl.program_id(ax) for ax in range(3)]
        if comm is not None:
            comm_refs = (refs[n_in:n_in + c_in], refs[n_in + c_in + 1:n_in + c_in + 1 + c_out], refs[-2:])

            @pl.when(jnp.logical_and(jnp.logical_and(pids[0] == 0, pids[1] == 0), pids[2] == 0))
            def _():
                comm.start(*comm_refs)

        def finish(acc):
            if has_add:
                acc = acc + add_ref[...].astype(F32)
            o_ref[...] = acc.astype(o_ref.dtype)

        p = _dot(a_ref[...], b_ref[...], form)
        if nk == 1:
            finish(p)
        else:
            acc_ref = refs[n_in + c_in + 1 + c_out]

            @pl.when(pids[2] == 0)
            def _():
                acc_ref[...] = p

            @pl.when(pids[2] > 0)
            def _():
                acc_ref[...] += p

            @pl.when(pids[2] == nk - 1)
            def _():
                finish(acc_ref[...])

        if comm is not None:
            @pl.when(jnp.logical_and(jnp.logical_and(pids[0] == grid[0] - 1, pids[1] == grid[1] - 1),
                                     pids[2] == grid[2] - 1))
            def _():
                comm.finish(*comm_refs)

    acc_scratch = [pltpu.VMEM((tm, tn), F32)] if nk > 1 else []
    if comm is None:
        return pl.pallas_call(
            body, name=name, grid=grid, in_specs=[a_spec, b_spec] + ([o_spec] if has_add else []), out_specs=o_spec,
            out_shape=jax.ShapeDtypeStruct((m, n), out_dtype), scratch_shapes=acc_scratch,
            compiler_params=_params("parallel", "parallel", "arbitrary"),
        )(*((a, b, add) if has_add else (a, b)))
    outs = pl.pallas_call(
        body, name=name, grid=grid, in_specs=[a_spec, b_spec] + ([o_spec] if has_add else []) + [_ANY] * c_in,
        out_specs=[o_spec] + [_ANY] * c_out, out_shape=[jax.ShapeDtypeStruct((m, n), out_dtype)] + list(comm.outs),
        scratch_shapes=acc_scratch + _sem_pairs(comm.n_sems),
        input_output_aliases={n_in + i: 1 + o for i, o in comm.aliases.items()},
        compiler_params=_params("arbitrary", "arbitrary", "arbitrary"),
    )(*((a, b, add) if has_add else (a, b)), *comm.ins)
    return outs[0], list(outs[1:])


def _rms_fwd(name, x, g):
    t, d = x.shape
    tm = _tile(t, 512, 16)

    def body(x_ref, g_ref, o_ref):
        o_ref[...] = _rms(x_ref[...], g_ref[...]).astype(o_ref.dtype)

    return pl.pallas_call(
        body, name=name, grid=(t // tm,),
        in_specs=[pl.BlockSpec((tm, d), lambda i: (i, 0)), pl.BlockSpec((1, d), lambda i: (0, 0))],
        out_specs=pl.BlockSpec((tm, d), lambda i: (i, 0)),
        out_shape=jax.ShapeDtypeStruct((t, d), MXU_DTYPE), compiler_params=_params("parallel"),
    )(x, g.reshape(1, d))


def _rms_bwd(name, x, g, dh, dres=None):
    t, d = x.shape
    tm = _tile(t, 256, 16)
    has_res = dres is not None

    def body(*refs):
        x_ref, g_ref, dh_ref = refs[:3]
        dres_ref = refs[3] if has_res else None
        dx_ref, dxb_ref, dg_ref = refs[-3:]
        _, vjp = jax.vjp(_rms, x_ref[...], g_ref[...])
        dx, dg = vjp(dh_ref[...].astype(F32))
        if has_res:
            dx = dx + dres_ref[...]
        dx_ref[...] = dx
        dxb_ref[...] = dx.astype(dxb_ref.dtype)
        first = pl.program_id(0) == 0

        @pl.when(first)
        def _():
            dg_ref[...] = dg

        @pl.when(jnp.logical_not(first))
        def _():
            dg_ref[...] += dg

    row = pl.BlockSpec((tm, d), lambda i: (i, 0))
    vec = pl.BlockSpec((1, d), lambda i: (0, 0))
    dx, dxb, dg = pl.pallas_call(
        body, name=name, grid=(t // tm,),
        in_specs=[row, vec, row] + ([row] if has_res else []), out_specs=[row, row, vec],
        out_shape=[jax.ShapeDtypeStruct((t, d), F32), jax.ShapeDtypeStruct((t, d), MXU_DTYPE),
                   jax.ShapeDtypeStruct((1, d), F32)],
        compiler_params=_params("arbitrary"),
    )(*((x, g.reshape(1, d), dh) + ((dres,) if has_res else ())))
    return dx, dxb, dg.reshape(d)


def _final_loss(x, g, target):
    t, d = x.shape
    tm = _tile(t, 256, 16)

    def body(x_ref, g_ref, t_ref, loss_ref, dx_ref, dxb_ref, dg_ref):
        y, vjp = jax.vjp(_rms, x_ref[...], g_ref[...])
        err = y - t_ref[...]
        dx, dg = vjp(err * (1.0 / d))
        dx_ref[...] = dx
        dxb_ref[...] = dx.astype(dxb_ref.dtype)
        part = jnp.zeros((1, LANES), F32) + 0.5 * jnp.sum(jnp.mean(err * err, axis=-1, keepdims=True))
        first = pl.program_id(0) == 0

        @pl.when(first)
        def _():
            dg_ref[...] = dg
            loss_ref[...] = part

        @pl.when(jnp.logical_not(first))
        def _():
            dg_ref[...] += dg
            loss_ref[...] += part

    row = pl.BlockSpec((tm, d), lambda i: (i, 0))
    vec = pl.BlockSpec((1, d), lambda i: (0, 0))
    loss, dx, dxb, dg = pl.pallas_call(
        body, name="final_loss", grid=(t // tm,), in_specs=[row, vec, row],
        out_specs=[pl.BlockSpec((1, LANES), lambda i: (0, 0)), row, row, vec],
        out_shape=[jax.ShapeDtypeStruct((1, LANES), F32), jax.ShapeDtypeStruct((t, d), F32),
                   jax.ShapeDtypeStruct((t, d), MXU_DTYPE), jax.ShapeDtypeStruct((1, d), F32)],
        compiler_params=_params("arbitrary"),
    )(x, g.reshape(1, d), target)
    return loss, dx, dxb, dg.reshape(d)


def _conv_taps(x_ext, w, rows):
    kk = w.shape[0]
    y = x_ext[HALO:] * w[kk - 1:kk, :]
    for j in range(kk - 1):
        y = y + pltpu.roll(x_ext, kk - 1 - j, axis=0)[HALO:] * w[j:j + 1, :]
    return y


def _col_specs(tm, tn, col0, t_rows):
    assert col0 % tn == 0 and tm % HALO == 0
    c0 = col0 // tn
    per, last = tm // HALO, t_rows // HALO - 1
    tile = pl.BlockSpec((tm, tn), lambda j, i: (i, c0 + j))
    prev = pl.BlockSpec((HALO, tn), lambda j, i: (jnp.maximum(i * per - 1, 0), c0 + j))
    nxt = pl.BlockSpec((HALO, tn), lambda j, i: (jnp.minimum((i + 1) * per, last), c0 + j))
    return tile, prev, nxt


def _conv_fwd(name, xa, xa_col, w, w_col, ncols, out_dtype, xb=None, xb_col=0, gate=None, gate_col=0):
    t = xa.shape[0]
    kk = w.shape[0]
    tm, tn = _tile(t, EW_ROWS, HALO), _tile(ncols, EW_COLS)
    nrow = t // tm
    has_b, has_g = xb is not None, gate is not None

    def body(*refs):
        refs = list(refs)
        xa_ref, xap_ref = refs.pop(0), refs.pop(0)
        xb_ref, xbp_ref = (refs.pop(0), refs.pop(0)) if has_b else (None, None)
        w_ref = refs.pop(0)
        g_ref = refs.pop(0) if has_g else None
        o_ref = refs.pop(0)
        i = pl.program_id(1)
        x, xp = xa_ref[...].astype(F32), xap_ref[...].astype(F32)
        if has_b:
            x, xp = x * xb_ref[...].astype(F32), xp * xbp_ref[...].astype(F32)
        xp = jnp.where(i == 0, 0.0, xp)
        y = _conv_taps(jnp.concatenate([xp, x], axis=0), w_ref[...], tm)
        if has_g:
            y = y * g_ref[...].astype(F32)
        o_ref[...] = y.astype(o_ref.dtype)

    a_tile, a_prev, _ = _col_specs(tm, tn, xa_col, t)
    ins, specs = [xa, xa], [a_tile, a_prev]
    if has_b:
        b_tile, b_prev, _ = _col_specs(tm, tn, xb_col, t)
        ins, specs = ins + [xb, xb], specs + [b_tile, b_prev]
    assert w_col % tn == 0
    ins, specs = ins + [w], specs + [pl.BlockSpec((kk, tn), lambda j, i: (0, w_col // tn + j))]
    if has_g:
        ins, specs = ins + [gate], specs + [_col_specs(tm, tn, gate_col, t)[0]]
    return pl.pallas_call(
        body, name=name, grid=(ncols // tn, nrow), in_specs=specs,
        out_specs=pl.BlockSpec((tm, tn), lambda j, i: (i, j)),
        out_shape=jax.ShapeDtypeStruct((t, ncols), out_dtype), compiler_params=_params("parallel", "parallel"),
    )(*ins)


def _conv_bwd(name, xa, xa_col, w, w_col, dy, dy_col, ncols, dx_dtype, xb=None, xb_col=0, gate=None, gate_col=0):
    t = xa.shape[0]
    kk = w.shape[0]
    tm, tn = _tile(t, EW_ROWS, HALO), _tile(ncols, EW_COLS)
    nrow = t // tm
    has_b, has_g = xb is not None, gate is not None

    def body(*refs):
        refs = list(refs)
        xa_ref, xap_ref = refs.pop(0), refs.pop(0)
        xb_ref, xbp_ref = (refs.pop(0), refs.pop(0)) if has_b else (None, None)
        w_ref = refs.pop(0)
        dy_ref, dyn_ref = refs.pop(0), refs.pop(0)
        g_ref, gn_ref = (refs.pop(0), refs.pop(0)) if has_g else (None, None)
        dxa_ref = refs.pop(0)
        dxb_ref = refs.pop(0) if has_b else None
        dg_ref = refs.pop(0) if has_g else None
        dw_ref = refs.pop(0)
        i = pl.program_id(1)
        wv = w_ref[...]
        xa_t, xa_p = xa_ref[...].astype(F32), xap_ref[...].astype(F32)
        x, xp = xa_t, xa_p
        if has_b:
            xb_t = xb_ref[...].astype(F32)
            x, xp = x * xb_t, xp * xbp_ref[...].astype(F32)
        xp = jnp.where(i == 0, 0.0, xp)
        x_ext = jnp.concatenate([xp, x], axis=0)
        dyv, dyn = dy_ref[...].astype(F32), dyn_ref[...].astype(F32)
        if has_g:
            dg_ref[...] = (dyv * _conv_taps(x_ext, wv, tm)).astype(dg_ref.dtype)
            dyv, dyn = dyv * g_ref[...].astype(F32), dyn * gn_ref[...].astype(F32)
        dyn = jnp.where(i == nrow - 1, 0.0, dyn)
        dy_ext = jnp.concatenate([dyv, dyn], axis=0)
        dx = dyv * wv[kk - 1:kk, :]
        row8 = lax.broadcasted_iota(jnp.int32, (8, tn), 0)
        dw = jnp.where(row8 == kk - 1, jnp.sum(dyv * x, axis=0, keepdims=True), 0.0)
        for j in range(kk - 1):
            s = kk - 1 - j
            dx = dx + pltpu.roll(dy_ext, tm + HALO - s, axis=0)[:tm] * wv[j:j + 1, :]
            dwj = jnp.sum(dyv * pltpu.roll(x_ext, s, axis=0)[HALO:], axis=0, keepdims=True)
            dw = dw + jnp.where(row8 == j, dwj, 0.0)
        if has_b:
            dxa_ref[...] = (dx * xb_t).astype(dxa_ref.dtype)
            dxb_ref[...] = (dx * xa_t).astype(dxb_ref.dtype)
        else:
            dxa_ref[...] = dx.astype(dxa_ref.dtype)

        @pl.when(i == 0)
        def _():
            dw_ref[...] = dw

        @pl.when(i > 0)
        def _():
            dw_ref[...] += dw

    a_tile, a_prev, _ = _col_specs(tm, tn, xa_col, t)
    ins, specs = [xa, xa], [a_tile, a_prev]
    if has_b:
        b_tile, b_prev, _ = _col_specs(tm, tn, xb_col, t)
        ins, specs = ins + [xb, xb], specs + [b_tile, b_prev]
    assert w_col % tn == 0
    ins, specs = ins + [w], specs + [pl.BlockSpec((kk, tn), lambda j, i: (0, w_col // tn + j))]
    d_tile, _, d_next = _col_specs(tm, tn, dy_col, t)
    ins, specs = ins + [dy, dy], specs + [d_tile, d_next]
    if has_g:
        g_tile, _, g_next = _col_specs(tm, tn, gate_col, t)
        ins, specs = ins + [gate, gate], specs + [g_tile, g_next]
    out_tile = pl.BlockSpec((tm, tn), lambda j, i: (i, j))
    shapes, ospecs = [jax.ShapeDtypeStruct((t, ncols), dx_dtype)], [out_tile]
    if has_b:
        shapes, ospecs = shapes + [jax.ShapeDtypeStruct((t, ncols), dx_dtype)], ospecs + [out_tile]
    if has_g:
        shapes, ospecs = shapes + [jax.ShapeDtypeStruct((t, ncols), dx_dtype)], ospecs + [out_tile]
    shapes, ospecs = shapes + [jax.ShapeDtypeStruct((8, ncols), F32)], ospecs + [pl.BlockSpec((8, tn), lambda j, i: (0, j))]
    outs = list(pl.pallas_call(
        body, name=name, grid=(ncols // tn, nrow), in_specs=specs, out_specs=ospecs, out_shape=shapes,
        compiler_params=_params("parallel", "arbitrary"),
    )(*ins))
    dxa = outs.pop(0)
    dxb = outs.pop(0) if has_b else None
    dgate = outs.pop(0) if has_g else None
    return dxa, dxb, dgate, outs.pop(0)[:kk]


def _swiglu_fwd(u):
    t, f2 = u.shape
    f = f2 // 2
    tm, tn = _tile(t, EW_ROWS, 16), _tile(f, EW_COLS)
    nf = f // tn

    def body(g_ref, u_ref, o_ref):
        o_ref[...] = (_silu(g_ref[...].astype(F32)) * u_ref[...].astype(F32)).astype(o_ref.dtype)

    return pl.pallas_call(
        body, name="swiglu_fwd", grid=(t // tm, nf),
        in_specs=[pl.BlockSpec((tm, tn), lambda i, j: (i, j)), pl.BlockSpec((tm, tn), lambda i, j: (i, nf + j))],
        out_specs=pl.BlockSpec((tm, tn), lambda i, j: (i, j)),
        out_shape=jax.ShapeDtypeStruct((t, f), MXU_DTYPE), compiler_params=_params("parallel", "parallel"),
    )(u, u)


def _swiglu_bwd(u, da):
    t, f2 = u.shape
    f = f2 // 2
    tm, tn = _tile(t, EW_ROWS, 16), _tile(f, EW_COLS)
    nf = f // tn

    def body(g_ref, u_ref, da_ref, o_ref):
        g, d = g_ref[...].astype(F32), da_ref[...].astype(F32)
        sg = _sigmoid(g)
        gate_half = pl.program_id(1) < nf

        @pl.when(gate_half)
        def _():
            o_ref[...] = (d * u_ref[...].astype(F32) * (sg * (1.0 + g * (1.0 - sg)))).astype(o_ref.dtype)

        @pl.when(jnp.logical_not(gate_half))
        def _():
            o_ref[...] = (d * (g * sg)).astype(o_ref.dtype)

    return pl.pallas_call(
        body, name="swiglu_bwd", grid=(t // tm, 2 * nf),
        in_specs=[pl.BlockSpec((tm, tn), lambda i, j: (i, j % nf)),
                  pl.BlockSpec((tm, tn), lambda i, j: (i, nf + j % nf)),
                  pl.BlockSpec((tm, tn), lambda i, j: (i, j % nf))],
        out_specs=pl.BlockSpec((tm, tn), lambda i, j: (i, j)),
        out_shape=jax.ShapeDtypeStruct((t, f2), MXU_DTYPE), compiler_params=_params("parallel", "parallel"),
    )(u, u, da)


def _gdn_prep(ops, qc, kc, vc, b_col, a_col, a_log, dt_bias):
    c, dh = qc.shape[-2:]
    q, k, v = _silu(qc), _silu(kc), _silu(vc)
    q = q * lax.rsqrt(jnp.sum(q * q, axis=-1, keepdims=True) + EPS) * (dh ** -0.5)
    k = k * lax.rsqrt(jnp.sum(k * k, axis=-1, keepdims=True) + EPS)
    beta = _sigmoid(b_col)
    g_col = -jnp.exp(a_log) * _softplus(a_col + dt_bias)
    r = lax.broadcasted_iota(jnp.int32, (c, c), 0)
    s = lax.broadcasted_iota(jnp.int32, (c, c), 1)
    g_row = jnp.sum(jnp.where(r == s, g_col, 0.0), axis=-2, keepdims=True)
    gc_col = jnp.sum(jnp.where(s <= r, g_row, 0.0), axis=-1, keepdims=True)
    gc_row = jnp.sum(jnp.where(r <= s, g_col, 0.0), axis=-2, keepdims=True)
    decay = jnp.exp(jnp.where(s <= r, gc_col - gc_row, -1e30))
    kb = k * beta
    a = jnp.where(s < r, ops.mm(kb, k, "nt") * decay, 0.0)
    tinv = ops.tri_inv(a)
    e_col = jnp.exp(gc_col)
    uw = ops.mmh(tinv, jnp.concatenate([v * beta, kb * e_col], axis=-1))
    u, w = uw[..., :dh], uw[..., dh:]
    attn = ops.mm(q, k, "nt") * decay
    g_last = jnp.sum(g_col, axis=-2, keepdims=True)
    return u, w, attn, q * e_col, k * jnp.exp(g_last - gc_col), g_last, tinv


def _gdn_step(ops, state, u, w, attn, q_dec, k_dec, g_last):
    v_new = u - ops.mm(w, state)
    o = ops.mm(q_dec, state) + ops.mm(attn, v_new)
    return o, state * jnp.exp(g_last) + ops.mm(k_dec, v_new, "tn")


PREP_HEADS, SCAN_HEADS = 4, 8


def _gdn_blocks(t, heads, hb_pref):
    tc = _tile(t, 256, CHUNK)
    hb = max(h for h in range(1, hb_pref + 1) if heads % h == 0)
    return tc, hb


def _to_chunks(ref, hb, dh):
    tc = ref.shape[0]
    return jnp.concatenate([ref[:, h * dh:(h + 1) * dh].astype(F32).reshape(tc // CHUNK, CHUNK, dh)
                            for h in range(hb)], axis=0)


def _from_chunks(ref, val, hb, dh):
    tc = ref.shape[0]
    ncb = tc // CHUNK
    for h in range(hb):
        ref[:, h * dh:(h + 1) * dh] = val[h * ncb:(h + 1) * ncb].reshape(tc, dh).astype(ref.dtype)


def _per_chunk(s, ncb):
    hb = s.shape[0]
    return jnp.broadcast_to(s[:, None], (hb, ncb, 1, 1)).reshape(hb * ncb, 1, 1)


def _gate_columns(pba, first_head, hb, heads):
    tc = pba.shape[0]
    lane = lax.broadcasted_iota(jnp.int32, pba.shape, 1)

    def pick(k):
        return jnp.sum(jnp.where(lane == k, pba, 0.0), axis=1, keepdims=True).reshape(tc // CHUNK, CHUNK, 1)

    return (jnp.concatenate([pick(first_head + h) for h in range(hb)], axis=0),
            jnp.concatenate([pick(heads + first_head + h) for h in range(hb)], axis=0))


def _gdn_prep_fwd(qkv, pba, a_log, dt_bias, heads, dh, comm=None):
    t = qkv.shape[0]
    tc, hb = _gdn_blocks(t, heads, PREP_HEADS)
    ncb, nhb, width = tc // CHUNK, heads // hb, heads * dh
    nc = t // CHUNK
    grid = (t // tc, nhb)
    c_in, c_out = (len(comm.ins), len(comm.outs)) if comm is not None else (0, 0)

    def body(*refs):
        q_ref, k_ref, v_ref, g_ref, al_ref, dt_ref = refs[:6]
        u_ref, w_ref, p_ref, qd_ref, kd_ref, gl_ref, ti_ref = refs[6 + c_in:13 + c_in]
        if comm is not None:
            comm_refs = (refs[6:6 + c_in], refs[13 + c_in:13 + c_in + c_out], refs[-2:])

            @pl.when(jnp.logical_and(pl.program_id(0) == 0, pl.program_id(1) == 0))
            def _():
                comm.start(*comm_refs)

        b_col, a_col = _gate_columns(g_ref[...], pl.program_id(1) * hb, hb, heads)
        u, w, p, qd, kd, gl, tinv = _gdn_prep(
            _PLAIN, _to_chunks(q_ref, hb, dh), _to_chunks(k_ref, hb, dh), _to_chunks(v_ref, hb, dh), b_col, a_col,
            _per_chunk(al_ref[...], ncb), _per_chunk(dt_ref[...], ncb))
        _from_chunks(u_ref, u, hb, dh)
        _from_chunks(w_ref, w, hb, dh)
        _from_chunks(qd_ref, qd, hb, dh)
        _from_chunks(kd_ref, kd, hb, dh)
        p_ref[...] = p.reshape(hb, tc, CHUNK).astype(p_ref.dtype)
        gl_ref[...] = gl.reshape(hb, ncb, 1, 1)
        ti_ref[...] = tinv.reshape(hb, tc, CHUNK)
        if comm is not None:
            @pl.when(jnp.logical_and(pl.program_id(0) == grid[0] - 1, pl.program_id(1) == grid[1] - 1))
            def _():
                comm.finish(*comm_refs)

    def tok(off):
        return pl.BlockSpec((tc, hb * dh), lambda i, j: (i, off * nhb + j))

    gate = pl.BlockSpec((tc, LANES), lambda i, j: (i, 0))
    scal = pl.BlockSpec((hb, 1, 1), lambda i, j: (j, 0, 0))
    square = pl.BlockSpec((hb, tc, CHUNK), lambda i, j: (j, i, 0))
    outs = pl.pallas_call(
        body, name="gdn_prep_fwd", grid=grid,
        in_specs=[tok(0), tok(1), tok(2), gate, scal, scal] + [_ANY] * c_in,
        out_specs=[tok(0), tok(0), square, tok(0), tok(0), pl.BlockSpec((hb, ncb, 1, 1), lambda i, j: (j, i, 0, 0)),
                   square] + [_ANY] * c_out,
        out_shape=[jax.ShapeDtypeStruct((t, width), F32), jax.ShapeDtypeStruct((t, width), MXU_DTYPE),
                   jax.ShapeDtypeStruct((heads, t, CHUNK), MXU_DTYPE), jax.ShapeDtypeStruct((t, width), MXU_DTYPE),
                   jax.ShapeDtypeStruct((t, width), MXU_DTYPE), jax.ShapeDtypeStruct((heads, nc, 1, 1), F32),
                   jax.ShapeDtypeStruct((heads, t, CHUNK), F32)] + (list(comm.outs) if comm is not None else []),
        scratch_shapes=_sem_pairs(comm.n_sems) if comm is not None else [],
        compiler_params=_params("arbitrary", "arbitrary") if comm is not None else _params("parallel", "parallel"),
    )(qkv, qkv, qkv, pba, a_log, dt_bias, *(comm.ins if comm is not None else ()))
    return tuple(outs[:6]), outs[6], list(outs[7:])


def _gdn_prep_bwd(qkv, pba, a_log, dt_bias, tinv, du, dw, dp, dqd, dkd, dgl, heads, dh):
    t = qkv.shape[0]
    tc, hb = _gdn_blocks(t, heads, PREP_HEADS)
    ncb, nhb, width = tc // CHUNK, heads // hb, heads * dh

    def body(q_ref, k_ref, v_ref, g_ref, al_ref, dt_ref, ti_ref, du_ref, dw_ref, dp_ref, dqd_ref, dkd_ref, dgl_ref,
             dq_ref, dk_ref, dv_ref, dg_ref, dal_ref, ddt_ref):
        first_head = pl.program_id(1) * hb
        b_col, a_col = _gate_columns(g_ref[...], first_head, hb, heads)
        ops = _Ops(True, ti_ref[...].reshape(hb * ncb, CHUNK, CHUNK))

        def prep(q, k, v, b, a, al, dt):
            return _gdn_prep(ops, q, k, v, b, a, _per_chunk(al, ncb), _per_chunk(dt, ncb))[:6]

        _, vjp = jax.vjp(prep, _to_chunks(q_ref, hb, dh), _to_chunks(k_ref, hb, dh), _to_chunks(v_ref, hb, dh),
                         b_col, a_col, al_ref[...], dt_ref[...])
        dq, dk, dv, db, da, dal, ddt = vjp((
            _to_chunks(du_ref, hb, dh), _to_chunks(dw_ref, hb, dh), dp_ref[...].reshape(hb * ncb, CHUNK, CHUNK),
            _to_chunks(dqd_ref, hb, dh), _to_chunks(dkd_ref, hb, dh), dgl_ref[...].reshape(hb * ncb, 1, 1)))
        _from_chunks(dq_ref, dq, hb, dh)
        _from_chunks(dk_ref, dk, hb, dh)
        _from_chunks(dv_ref, dv, hb, dh)
        dal_ref[...] = dal[None]
        ddt_ref[...] = ddt[None]
        lane = lax.broadcasted_iota(jnp.int32, (tc, LANES), 1)
        dgates = jnp.zeros((tc, LANES), F32)
        for h in range(hb):
            rows = slice(h * ncb, (h + 1) * ncb)
            dgates = dgates + jnp.where(lane == first_head + h, db[rows].reshape(tc, 1), 0.0) \
                + jnp.where(lane == heads + first_head + h, da[rows].reshape(tc, 1), 0.0)

        @pl.when(first_head == 0)
        def _():
            dg_ref[...] = dgates

        @pl.when(first_head > 0)
        def _():
            dg_ref[...] += dgates

    def tok(off):
        return pl.BlockSpec((tc, hb * dh), lambda i, j: (i, off * nhb + j))

    gate = pl.BlockSpec((tc, LANES), lambda i, j: (i, 0))
    scal = pl.BlockSpec((hb, 1, 1), lambda i, j: (j, 0, 0))
    part = pl.BlockSpec((1, hb, 1, 1), lambda i, j: (i, j, 0, 0))
    pspec = pl.BlockSpec((hb, tc, CHUNK), lambda i, j: (j, i, 0))
    glspec = pl.BlockSpec((hb, ncb, 1, 1), lambda i, j: (j, i, 0, 0))
    tokf = jax.ShapeDtypeStruct((t, width), F32)
    partf = jax.ShapeDtypeStruct((t // tc, heads, 1, 1), F32)
    return pl.pallas_call(
        body, name="gdn_prep_bwd", grid=(t // tc, nhb),
        in_specs=[tok(0), tok(1), tok(2), gate, scal, scal, pspec, tok(0), tok(0), pspec, tok(0), tok(0), glspec],
        out_specs=[tok(0), tok(0), tok(0), gate, part, part],
        out_shape=[tokf, tokf, tokf, jax.ShapeDtypeStruct((t, LANES), F32), partf, partf],
        compiler_params=_params("parallel", "arbitrary"),
    )(qkv, qkv, qkv, pba, a_log, dt_bias, tinv, du, dw, dp, dqd, dkd, dgl)


def _heads(ref, rows, hb, dh):
    return jnp.stack([ref[rows, h * dh:(h + 1) * dh].astype(F32) for h in range(hb)])


def _put_heads(ref, rows, val, dh):
    for h in range(val.shape[0]):
        ref[rows, h * dh:(h + 1) * dh] = val[h].astype(ref.dtype)


def _gdn_scan_fwd(u, w, p, qd, kd, gl, heads, dh):
    t = u.shape[0]
    tc, hb = _gdn_blocks(t, heads, SCAN_HEADS)
    ncb, nhb = tc // CHUNK, heads // hb
    nc = t // CHUNK

    def body(u_ref, w_ref, p_ref, qd_ref, kd_ref, gl_ref, o_ref, s_ref, state):
        @pl.when(pl.program_id(1) == 0)
        def _():
            state[...] = jnp.zeros_like(state)

        for c in range(ncb):
            rs = slice(c * CHUNK, (c + 1) * CHUNK)
            s_in = state[...]
            s_ref[:, c] = s_in
            o, s_out = _gdn_step(_PLAIN, s_in, _heads(u_ref, rs, hb, dh), _heads(w_ref, rs, hb, dh), p_ref[:, rs, :],
                                 _heads(qd_ref, rs, hb, dh), _heads(kd_ref, rs, hb, dh), gl_ref[:, c])
            _put_heads(o_ref, rs, o, dh)
            state[...] = s_out

    tok = pl.BlockSpec((tc, hb * dh), lambda j, i: (i, j))
    pspec = pl.BlockSpec((hb, tc, CHUNK), lambda j, i: (j, i, 0))
    glspec = pl.BlockSpec((hb, ncb, 1, 1), lambda j, i: (j, i, 0, 0))
    return pl.pallas_call(
        body, name="gdn_scan_fwd", grid=(nhb, t // tc),
        in_specs=[tok, tok, pspec, tok, tok, glspec],
        out_specs=[tok, pl.BlockSpec((hb, ncb, dh, dh), lambda j, i: (j, i, 0, 0))],
        out_shape=[jax.ShapeDtypeStruct((t, heads * dh), F32), jax.ShapeDtypeStruct((heads, nc, dh, dh), F32)],
        scratch_shapes=[pltpu.VMEM((hb, dh, dh), F32)],
        compiler_params=_params("arbitrary", "arbitrary"),
    )(u, w, p, qd, kd, gl)


def _gdn_scan_bwd(u, w, p, qd, kd, gl, states, do, heads, dh):
    t = u.shape[0]
    tc, hb = _gdn_blocks(t, heads, SCAN_HEADS)
    ncb, nhb = tc // CHUNK, heads // hb
    nc, nt = t // CHUNK, t // tc

    def body(u_ref, w_ref, p_ref, qd_ref, kd_ref, gl_ref, s_ref, do_ref,
             du_ref, dw_ref, dp_ref, dqd_ref, dkd_ref, dgl_ref, dstate):
        @pl.when(pl.program_id(1) == 0)
        def _():
            dstate[...] = jnp.zeros_like(dstate)

        for c in reversed(range(ncb)):
            rs = slice(c * CHUNK, (c + 1) * CHUNK)
            _, vjp = jax.vjp(functools.partial(_gdn_step, _DIFF), s_ref[:, c], _heads(u_ref, rs, hb, dh),
                             _heads(w_ref, rs, hb, dh), p_ref[:, rs, :].astype(F32), _heads(qd_ref, rs, hb, dh),
                             _heads(kd_ref, rs, hb, dh), gl_ref[:, c])
            ds, du, dw, dp, dqd, dkd, dgl = vjp((_heads(do_ref, rs, hb, dh), dstate[...]))
            dstate[...] = ds
            _put_heads(du_ref, rs, du, dh)
            _put_heads(dw_ref, rs, dw, dh)
            _put_heads(dqd_ref, rs, dqd, dh)
            _put_heads(dkd_ref, rs, dkd, dh)
            dp_ref[:, rs, :] = dp
            dgl_ref[:, c] = dgl

    tok = pl.BlockSpec((tc, hb * dh), lambda j, i: (nt - 1 - i, j))
    pspec = pl.BlockSpec((hb, tc, CHUNK), lambda j, i: (j, nt - 1 - i, 0))
    glspec = pl.BlockSpec((hb, ncb, 1, 1), lambda j, i: (j, nt - 1 - i, 0, 0))
    sspec = pl.BlockSpec((hb, ncb, dh, dh), lambda j, i: (j, nt - 1 - i, 0, 0))
    tokf = jax.ShapeDtypeStruct((t, heads * dh), F32)
    return pl.pallas_call(
        body, name="gdn_scan_bwd", grid=(nhb, nt),
        in_specs=[tok, tok, pspec, tok, tok, glspec, sspec, tok],
        out_specs=[tok, tok, pspec, tok, tok, glspec],
        out_shape=[tokf, tokf, jax.ShapeDtypeStruct((heads, t, CHUNK), F32), tokf, tokf,
                   jax.ShapeDtypeStruct((heads, nc, 1, 1), F32)],
        scratch_shapes=[pltpu.VMEM((hb, dh, dh), F32)],
        compiler_params=_params("arbitrary", "arbitrary"),
    )(u, w, p, qd, kd, gl, states, do)


def _gdn_post(o, z, gain):
    return _rms(o, gain) * _silu(z)


def _gdn_post_fwd(o, pm, z_col, gain, heads, dh):
    t, wid = o.shape
    tm = _tile(t, 256, 16)
    assert z_col % wid == 0

    def body(o_ref, z_ref, g_ref, y_ref):
        for h in range(heads):
            ls = slice(h * dh, (h + 1) * dh)
            y_ref[:, ls] = _gdn_post(o_ref[:, ls], z_ref[:, ls], g_ref[...]).astype(y_ref.dtype)

    blk = pl.BlockSpec((tm, wid), lambda i: (i, 0))
    return pl.pallas_call(
        body, name="gdn_post_fwd", grid=(t // tm,),
        in_specs=[blk, pl.BlockSpec((tm, wid), lambda i: (i, z_col // wid)), pl.BlockSpec((1, dh), lambda i: (0, 0))],
        out_specs=blk, out_shape=jax.ShapeDtypeStruct((t, wid), MXU_DTYPE), compiler_params=_params("parallel"),
    )(o, pm, gain.reshape(1, dh))


def _gdn_post_bwd(o, pm, z_col, gain, dy, heads, dh):
    t, wid = o.shape
    tm = _tile(t, 256, 16)
    assert z_col % wid == 0

    def body(o_ref, z_ref, g_ref, dy_ref, do_ref, dz_ref, dg_ref):
        dg = jnp.zeros((1, dh), F32)
        for h in range(heads):
            ls = slice(h * dh, (h + 1) * dh)
            _, vjp = jax.vjp(_gdn_post, o_ref[:, ls], z_ref[:, ls], g_ref[...])
            do, dz, dg_h = vjp(dy_ref[:, ls])
            do_ref[:, ls] = do
            dz_ref[:, ls] = dz.astype(dz_ref.dtype)
            dg = dg + dg_h
        first = pl.program_id(0) == 0

        @pl.when(first)
        def _():
            dg_ref[...] = dg

        @pl.when(jnp.logical_not(first))
        def _():
            dg_ref[...] += dg

    blk = pl.BlockSpec((tm, wid), lambda i: (i, 0))
    vec = pl.BlockSpec((1, dh), lambda i: (0, 0))
    do, dz, dg = pl.pallas_call(
        body, name="gdn_post_bwd", grid=(t // tm,),
        in_specs=[blk, pl.BlockSpec((tm, wid), lambda i: (i, z_col // wid)), vec, blk], out_specs=[blk, blk, vec],
        out_shape=[jax.ShapeDtypeStruct((t, wid), F32), jax.ShapeDtypeStruct((t, wid), MXU_DTYPE),
                   jax.ShapeDtypeStruct((1, dh), F32)],
        compiler_params=_params("arbitrary"),
    )(o, pm, gain.reshape(1, dh), dy)
    return do, dz, dg.reshape(dh)


def _attn(ops, q, kv):
    d = q.shape[1]
    hd = d // XATTN_HEADS
    outs = []
    for h in range(XATTN_HEADS):
        qh, kh, vh = q[:, h * hd:(h + 1) * hd], kv[:, h * hd:(h + 1) * hd], kv[:, d + h * hd:d + (h + 1) * hd]
        s = ops.mm(qh, kh, "nt") * (hd ** -0.5)
        e = jnp.exp(s - lax.stop_gradient(jnp.max(s, axis=-1, keepdims=True)))
        outs.append(ops.mm(e / jnp.sum(e, axis=-1, keepdims=True), vh))
    return jnp.concatenate(outs, axis=1)


def _attn_fwd(q, kv):
    t, d = q.shape
    nm = kv.shape[0]
    tm = _tile(t, 512, 16)

    def body(q_ref, kv_ref, o_ref):
        o_ref[...] = _attn(_PLAIN, q_ref[...], kv_ref[...]).astype(o_ref.dtype)

    return pl.pallas_call(
        body, name="xattn_fwd", grid=(t // tm,),
        in_specs=[pl.BlockSpec((tm, d), lambda i: (i, 0)), pl.BlockSpec((nm, 2 * d), lambda i: (0, 0))],
        out_specs=pl.BlockSpec((tm, d), lambda i: (i, 0)),
        out_shape=jax.ShapeDtypeStruct((t, d), MXU_DTYPE), compiler_params=_params("parallel"),
    )(q, kv)


def _attn_bwd(q, kv, do):
    t, d = q.shape
    nm = kv.shape[0]
    tm = _tile(t, 256, 16)

    def body(q_ref, kv_ref, do_ref, dq_ref, dkv_ref):
        _, vjp = jax.vjp(functools.partial(_attn, _DIFF), q_ref[...].astype(F32), kv_ref[...].astype(F32))
        dq, dkv = vjp(do_ref[...].astype(F32))
        dq_ref[...] = dq.astype(dq_ref.dtype)
        first = pl.program_id(0) == 0

        @pl.when(first)
        def _():
            dkv_ref[...] = dkv

        @pl.when(jnp.logical_not(first))
        def _():
            dkv_ref[...] += dkv

    row = pl.BlockSpec((tm, d), lambda i: (i, 0))
    full = pl.BlockSpec((nm, 2 * d), lambda i: (0, 0))
    return pl.pallas_call(
        body, name="xattn_bwd", grid=(t // tm,), in_specs=[row, full, row], out_specs=[row, full],
        out_shape=[jax.ShapeDtypeStruct((t, d), MXU_DTYPE), jax.ShapeDtypeStruct((nm, 2 * d), F32)],
        compiler_params=_params("arbitrary"),
    )(q, kv, do)


def _adamw(name, w, g, m, v):
    shape = w.shape
    cols = shape[-1]
    rows = w.size // cols
    w2, g2, m2, v2 = (a.reshape(rows, cols) for a in (w, g, m, v))
    tr = _tile(rows, max(8, (1 << 18) // cols // 8 * 8), 8)

    def body(w_ref, g_ref, m_ref, v_ref, d_ref, nm_ref, nv_ref):
        gv = g_ref[...]
        nm = ADAM_B1 * m_ref[...] + (1.0 - ADAM_B1) * gv
        nv = ADAM_B2 * v_ref[...] + (1.0 - ADAM_B2) * jnp.square(gv)
        m_hat = nm / (1.0 - ADAM_B1 ** ADAM_STEP)
        v_hat = nv / (1.0 - ADAM_B2 ** ADAM_STEP)
        d_ref[...] = -ADAM_LR * (m_hat / (jnp.sqrt(v_hat) + ADAM_EPS) + ADAM_WD * w_ref[...])
        nm_ref[...] = nm
        nv_ref[...] = nv

    blk = pl.BlockSpec((tr, cols), lambda i: (i, 0))
    out = jax.ShapeDtypeStruct((rows, cols), F32)
    d, nm, nv = pl.pallas_call(
        body, name=name, grid=(rows // tr,), in_specs=[blk] * 4, out_specs=[blk] * 3, out_shape=[out] * 3,
        compiler_params=_params("parallel"),
    )(w2, g2, m2, v2)
    return d.reshape(shape), nm.reshape(shape), nv.reshape(shape)


def _layer_fwd(x, mem, p, heads, dh, carry, late):
    wid = heads * dh
    sc = x.shape[1] - wid
    p, s, landed = dict(p), {"x0": x}, {}

    def arrived(name, brought):
        landed[name] = brought
        if name in late:
            p.update(late[name](brought))

    def mm(name, *args, **kwargs):
        if name not in carry:
            return _matmul(name, *args, **kwargs)
        out, brought = _matmul(name, *args, comm=carry[name], **kwargs)
        arrived(name, brought)
        return out

    s["h1"] = _rms_fwd("rms_mix", x, p["mix_norm"])
    s["pm"] = pm = mm("mm_mix_in", s["h1"], p["wmain"], "nn", F32)
    s["pba"] = mm("mm_mix_ba", s["h1"], p["wba"], "nn", F32)
    s["qkv"] = _conv_fwd("conv_gdn", pm, 0, p["gdn_conv"], 0, 3 * wid, F32)
    s["prep"], s["tinv"], brought = _gdn_prep_fwd(s["qkv"], s["pba"], p["a_log"], p["dt_bias"], heads, dh,
                                                  comm=carry.get("gdn_prep_fwd"))
    if brought:
        arrived("gdn_prep_fwd", brought)
    s["o"], s["states"] = _gdn_scan_fwd(*s["prep"], heads, dh)
    y_gdn = _gdn_post_fwd(s["o"], pm, 3 * wid, p["gdn_out_norm"], heads, dh)
    y_sc = _conv_fwd("conv_sc", pm, 4 * wid + sc, p["sc_conv"], 0, sc, MXU_DTYPE, xb=pm, xb_col=4 * wid + 2 * sc,
                     gate=pm, gate_col=4 * wid)
    s["ycat"] = jnp.concatenate([y_gdn, y_sc], axis=1)
    s["x1"] = x1 = mm("mm_mix_out", s["ycat"], p["wout"], "nn", F32, add=x)
    s["h2"] = _rms_fwd("rms_xattn", x1, p["xattn_norm"])
    s["q"] = mm("mm_xq", s["h2"], p["wq"], "nn", MXU_DTYPE)
    s["memn"] = _rms_fwd("rms_mem", mem, p["mem_norm"])
    s["kv"] = mm("mm_xkv", s["memn"], p["wkv"], "nn", MXU_DTYPE)
    s["ao"] = _attn_fwd(s["q"], s["kv"])
    s["x2"] = x2 = mm("mm_xo", s["ao"], p["wo"], "nn", F32, add=x1)
    s["h3"] = _rms_fwd("rms_ffn", x2, p["ffn_norm"])
    s["upre"] = mm("mm_ffn_up", s["h3"], p["wup"], "nn", MXU_DTYPE)
    s["uc"] = _conv_fwd("conv_ffn", s["upre"], 0, p["ffn_conv"], 0, s["upre"].shape[1], MXU_DTYPE)
    s["act"] = _swiglu_fwd(s["uc"])
    return mm("mm_ffn_down", s["act"], p["wdown"], "nn", F32, add=x2), s, landed, p


def _layer_bwd(dx3, dx3b, mem, s, p, heads, dh, reduce):
    wid = heads * dh
    sc = dx3.shape[1] - wid
    pm = s["pm"]
    g = {}

    mm = reduce.carried if reduce is not None else _matmul
    da = mm("mm_ffn_down_dx", dx3b, p["wdown"], "nt", MXU_DTYPE)
    g["wdown"] = mm("mm_ffn_down_dw", s["act"], dx3b, "tn", WIRE_DTYPE)
    du = _swiglu_bwd(s["uc"], da)
    dupre, _, _, g["ffn_conv"] = _conv_bwd("conv_ffn_bwd", s["upre"], 0, p["ffn_conv"], 0, du, 0, du.shape[1],
                                           MXU_DTYPE)
    dh3 = mm("mm_ffn_up_dx", dupre, p["wup"], "nt", F32)
    g["wup"] = mm("mm_ffn_up_dw", s["h3"], dupre, "tn", WIRE_DTYPE)
    dx2, dx2b, g["ffn_norm"] = _rms_bwd("rms_ffn_bwd", s["x2"], p["ffn_norm"], dh3, dx3)
    dao = mm("mm_xo_dx", dx2b, p["wo"], "nt", MXU_DTYPE)
    g["wo"] = mm("mm_xo_dw", s["ao"], dx2b, "tn", WIRE_DTYPE)
    dq, dkv = _attn_bwd(s["q"], s["kv"], dao)
    dh2 = mm("mm_xq_dx", dq, p["wq"], "nt", F32)
    g["wq"] = mm("mm_xq_dw", s["h2"], dq, "tn", WIRE_DTYPE)
    dmemn = mm("mm_xkv_dx", dkv, p["wkv"], "nt", F32)
    g["wkv"] = mm("mm_xkv_dw", s["memn"], dkv, "tn", WIRE_DTYPE)
    _, _, g["mem_norm"] = _rms_bwd("rms_mem_bwd", mem, p["mem_norm"], dmemn)
    dx1, dx1b, g["xattn_norm"] = _rms_bwd("rms_xattn_bwd", s["x1"], p["xattn_norm"], dh2, dx2)
    dycat = mm("mm_mix_out_dx", dx1b, p["wout"], "nt", F32)
    g["wout"] = mm("mm_mix_out_dw", s["ycat"], dx1b, "tn", WIRE_DTYPE)
    d_c, d_h, d_b, g["sc_conv"] = _conv_bwd("conv_sc_bwd", pm, 4 * wid + sc, p["sc_conv"], 0, dycat, wid, sc,
                                             MXU_DTYPE, xb=pm, xb_col=4 * wid + 2 * sc, gate=pm, gate_col=4 * wid)
    do, dz, g["gdn_out_norm"] = _gdn_post_bwd(s["o"], pm, 3 * wid, p["gdn_out_norm"], dycat, heads, dh)
    dprep = _gdn_scan_bwd(*s["prep"], s["states"], do, heads, dh)
    dqc, dkc, dvc, dpba, dal, ddt = _gdn_prep_bwd(s["qkv"], s["pba"], p["a_log"], p["dt_bias"], s["tinv"], *dprep,
                                                  heads, dh)
    g["a_log"], g["dt_bias"] = jnp.sum(dal, axis=0), jnp.sum(ddt, axis=0)
    dqkv, _, _, g["gdn_conv"] = _conv_bwd("conv_gdn_bwd", pm, 0, p["gdn_conv"], 0,
                                          jnp.concatenate([dqc, dkc, dvc], axis=1), 0, 3 * wid, MXU_DTYPE)
    dpm = jnp.concatenate([dqkv, dz, d_b, d_c, d_h], axis=1)
    dpba = dpba.astype(MXU_DTYPE)
    dh1 = mm("mm_mix_in_dx", dpm, p["wmain"], "nt", F32)
    dh1 = mm("mm_mix_ba_dx", dpba, p["wba"], "nt", F32, add=dh1)
    g["wmain"] = mm("mm_mix_in_dw", s["h1"], dpm, "tn", WIRE_DTYPE)
    g["wba"] = mm("mm_mix_ba_dw", s["h1"], dpba, "tn", WIRE_DTYPE)
    dx0, dx0b, g["mix_norm"] = _rms_bwd("rms_mix_bwd", s["x0"], p["mix_norm"], dh1, dx1)
    return dx0, dx0b, g


def _input_projection(win, heads, dh):
    wid = heads * dh
    return {"wmain": jnp.concatenate([win[:, :4 * wid], win[:, 4 * wid + 2 * heads:]], axis=1),
            "wba": jnp.pad(win[:, 4 * wid:4 * wid + 2 * heads], ((0, 0), (0, LANES - 2 * heads)))}


def _square_projections(wout, wq, wk, wv, wo, wdown):
    return {"wout": wout, "wq": wq, "wkv": jnp.concatenate([wk, wv], axis=1), "wo": wo, "wdown": wdown}


_ANY = pl.BlockSpec(memory_space=pl.ANY)
_VMEM = pl.BlockSpec(memory_space=pltpu.VMEM)


def _mesh_pos():
    return lax.axis_index("x"), lax.axis_index("y"), lax.axis_index("c")


def _other_chips(x, y):
    return [(1 - x, y), (x, 1 - y), (1 - x, 1 - y)]


def _push(src, dst, sems, k, to):
    return pltpu.make_async_remote_copy(src_ref=src, dst_ref=dst, send_sem=sems[0].at[k], recv_sem=sems[1].at[k],
                                        device_id=to, device_id_type=MESH)


def _sem_pairs(n):
    return [pltpu.SemaphoreType.DMA((n,)), pltpu.SemaphoreType.DMA((n,))]


class _Comm:
    def __init__(self, ins, outs, n_sems, start, finish, aliases=None):
        self.ins, self.outs, self.n_sems, self.start, self.finish = list(ins), list(outs), n_sems, start, finish
        self.aliases = aliases or {}


def _run_comm(name, comm):
    n_in, n_out = len(comm.ins), len(comm.outs)

    def body(*refs):
        parts = (refs[:n_in], refs[n_in:n_in + n_out], refs[n_in + n_out:])
        comm.start(*parts)
        comm.finish(*parts)

    return pl.pallas_call(
        body, name=name, in_specs=[_ANY] * n_in, out_specs=[_ANY] * n_out, out_shape=comm.outs,
        scratch_shapes=_sem_pairs(comm.n_sems), input_output_aliases=comm.aliases,
    )(*comm.ins)


def _allgather_comm(srcs):
    n = len(srcs)

    def first(src, out, sems):
        x, y, c = _mesh_pos()
        own, sends = [], []
        for t in range(n):
            half = src[t].shape[0] // 2
            mine = pl.ds(c * half, half)
            own.append(_push(src[t], out[t].at[2 * x + y], sems, 7 * t + 6, (x, y, 1 - c)))
            sends += [_push(src[t].at[mine], out[t].at[2 * x + y, mine], sems, 7 * t + k, (cx, cy, c))
                      for k, (cx, cy) in enumerate(_other_chips(x, y))]
        return own, sends

    def start(src, out, sems):
        own, sends = first(src, out, sems)
        for cp in own + sends:
            cp.start()

    def finish(src, out, sems):
        x, y, c = _mesh_pos()
        sibling = (x, y, 1 - c)
        own, sends = first(src, out, sems)
        fwds, relayed = [], []
        for t in range(n):
            half = src[t].shape[0] // 2
            for k, (cx, cy) in enumerate(_other_chips(x, y)):
                here = out[t].at[2 * cx + cy, pl.ds(c * half, half)]
                there = out[t].at[2 * cx + cy, pl.ds((1 - c) * half, half)]
                _push(here, here, sems, 7 * t + k, sibling).wait_recv()
                fwds.append(_push(here, here, sems, 7 * t + 3 + k, sibling))
                fwds[-1].start()
                relayed.append(_push(there, there, sems, 7 * t + 3 + k, sibling))
        for cp in relayed + own:
            cp.wait_recv()
        for cp in own + sends + fwds:
            cp.wait_send()

    return _Comm(srcs, [jax.ShapeDtypeStruct((N_CHIPS,) + s.shape, s.dtype) for s in srcs], 7 * n, start, finish)


def _start_wait(build):
    def start(src, out, sems):
        for cp in build(src, out, sems):
            cp.start()

    def finish(src, out, sems):
        for cp in build(src, out, sems):
            cp.wait()

    return start, finish


def _sibling_exchange_comm(bufs):
    def build(src, out, sems):
        x, y, c = _mesh_pos()
        return [_push(src[t].at[1 - c], out[t], sems, t, (x, y, 1 - c)) for t in range(len(bufs))]

    start, finish = _start_wait(build)
    return _Comm(bufs, [jax.ShapeDtypeStruct(b.shape[1:], b.dtype) for b in bufs], len(bufs), start, finish)


def _chip_exchange_comm(bufs):
    def build(src, out, sems):
        x, y, c = _mesh_pos()
        return [_push(src[t].at[2 * cx + cy], out[t].at[k], sems, 3 * t + k, (cx, cy, c))
                for t in range(len(bufs)) for k, (cx, cy) in enumerate(_other_chips(x, y))]

    start, finish = _start_wait(build)
    return _Comm(bufs, [jax.ShapeDtypeStruct((3,) + b.shape[1:], b.dtype) for b in bufs], 3 * len(bufs), start, finish)


def _sibling_share_comm(bufs):
    def build(src, out, sems):
        x, y, c = _mesh_pos()
        return [_push(src[t].at[c], out[t].at[c], sems, t, (x, y, 1 - c)) for t in range(len(bufs))]

    start, finish = _start_wait(build)
    return _Comm(bufs, [jax.ShapeDtypeStruct(b.shape, b.dtype) for b in bufs], len(bufs), start, finish,
                 aliases={t: t for t in range(len(bufs))})


def _allreduce_small(v):
    r, lanes = v.shape

    def body(v_ref, sum_ref, gath, send_sems, recv_sems):
        x, y, c = _mesh_pos()
        me = 4 * x + 2 * y + c
        gath[me] = v_ref[...]
        copies = []
        for rel in range(1, N_DEV):
            peer = tuple(1 - p if (rel >> b) & 1 else p for p, b in ((x, 2), (y, 1), (c, 0)))
            copies.append(pltpu.make_async_remote_copy(
                src_ref=v_ref, dst_ref=gath.at[me], send_sem=send_sems.at[rel - 1], recv_sem=recv_sems.at[rel - 1],
                device_id=peer, device_id_type=MESH))
        for cp in copies:
            cp.start()
        for cp in copies:
            cp.wait()
        total = gath[0]
        for k in range(1, N_DEV):
            total = total + gath[k]
        sum_ref[...] = total

    return pl.pallas_call(
        body, name="allreduce_small", in_specs=[_VMEM], out_specs=_VMEM,
        out_shape=jax.ShapeDtypeStruct((r, lanes), F32),
        scratch_shapes=[pltpu.VMEM((N_DEV, r, lanes), F32)] + _sem_pairs(N_DEV - 1),
        compiler_params=pltpu.CompilerParams(vmem_limit_bytes=VMEM_LIMIT),
    )(v)


def _sum_tile(rows, width):
    return _tile(rows, max(16, (1 << 19) // width // 16 * 16), 16)


def _sum_sibling(x, recv, core):
    _, n, w = x.shape
    tr = _sum_tile(n, w)

    def body(idx_ref, x_ref, r_ref, o_ref):
        o_ref[...] = (x_ref[...].astype(F32) + r_ref[...].astype(F32)).astype(o_ref.dtype)

    row = pl.BlockSpec((tr, w), lambda i, idx: (i, 0))
    return pl.pallas_call(
        body, name="rs_sum_sibling",
        grid_spec=pltpu.PrefetchScalarGridSpec(
            num_scalar_prefetch=1, grid=(n // tr,),
            in_specs=[pl.BlockSpec((None, tr, w), lambda i, idx: (idx[0], i, 0)), row], out_specs=row),
        out_shape=jax.ShapeDtypeStruct((n, w), x.dtype), compiler_params=_params("parallel"),
    )(core.reshape(1), x, recv)


def _sum_chips(s, recv, chip, core):
    _, m, w = s.shape
    tr = _sum_tile(m, w)

    def body(idx_ref, s_ref, r0_ref, r1_ref, r2_ref, o_ref):
        o_ref[...] = ((s_ref[...].astype(F32) + r0_ref[...].astype(F32)) + r1_ref[...].astype(F32)) \
            + r2_ref[...].astype(F32)

    def got(k):
        return pl.BlockSpec((None, tr, w), lambda i, idx: (k, i, 0))

    return pl.pallas_call(
        body, name="rs_sum_chips",
        grid_spec=pltpu.PrefetchScalarGridSpec(
            num_scalar_prefetch=1, grid=(m // tr,),
            in_specs=[pl.BlockSpec((None, tr, w), lambda i, idx: (idx[0], i, 0)), got(0), got(1), got(2)],
            out_specs=pl.BlockSpec((None, tr, w), lambda i, idx: (idx[1], i, 0))),
        out_shape=jax.ShapeDtypeStruct((2, m, w), F32), compiler_params=_params("parallel"),
    )(jnp.stack([chip, core]), s, recv, recv, recv)


_ROWS = ("w_mix_out", "w_xq", "w_xk", "w_xv", "w_xo", "w_ffn_down")
_CONVS = ("gdn_conv", "sc_conv", "ffn_conv")
_REPLICATED = ("mix_norm", "gdn_a_log", "gdn_dt_bias", "gdn_out_norm", "xattn_norm", "mem_norm", "ffn_norm",
               "final_norm")
_WEIGHTS = ("mix_norm", "w_mix_in", "gdn_conv", "gdn_a_log", "gdn_dt_bias", "gdn_out_norm", "sc_conv", "w_mix_out",
            "xattn_norm", "mem_norm", "w_xq", "w_xk", "w_xv", "w_xo", "ffn_norm", "w_ffn_up", "ffn_conv",
            "w_ffn_down", "final_norm")


def _pad_rows(flat, groups):
    unit = groups * 16 * LANES
    p = flat.shape[-1]
    pad = -p % unit
    if pad:
        flat = jnp.pad(flat, [(0, 0)] * (flat.ndim - 1) + [(0, pad)])
    return flat.reshape(flat.shape[:-1] + (groups, (p + pad) // (groups * LANES), LANES))


def _split_flat(flat, shapes):
    out, off = [], 0
    for shp in shapes:
        size = 1
        for n in shp:
            size *= n
        out.append(flat[..., off:off + size].reshape(flat.shape[:-1] + tuple(shp)))
        off += size
    return out


def _by_chip(g, axis):
    rows, cols = g.shape
    if axis == 0:
        return g.reshape(N_CHIPS, rows // N_CHIPS, cols)
    return g.reshape(rows, N_CHIPS, cols // N_CHIPS).transpose(1, 0, 2)


def _halves_by_chip(g):
    _, rows, w = g.shape
    return g.astype(WIRE_DTYPE).reshape(N_CHIPS, 2, rows // 2, w).transpose(1, 0, 2, 3)


class _ReduceScatter:
    STAGES = ("mm_ffn_down_dx", "mm_ffn_up_dx", "mm_ffn_up_dw", "mm_mix_in_dx")

    def __init__(self, bufs, chip, core):
        self.bufs, self.chip, self.core = list(bufs), chip, core
        self.sums = self.from_chips = self.reduced = self.result = None

    def comm(self, stage):
        if stage == self.STAGES[0]:
            return _sibling_exchange_comm(self.bufs)
        if stage == self.STAGES[1]:
            return _chip_exchange_comm(self.sums[-1:])
        if stage == self.STAGES[2]:
            return _chip_exchange_comm(self.sums[:-1])
        return _sibling_share_comm(self.reduced)

    def landed(self, stage, outs):
        if stage == self.STAGES[0]:
            self.sums = [_sum_sibling(b.reshape(2, -1, b.shape[-1]), r.reshape(-1, r.shape[-1]), self.core)
                         .reshape(r.shape) for b, r in zip(self.bufs, outs)]
        elif stage == self.STAGES[1]:
            self.from_chips = list(outs)
        elif stage == self.STAGES[2]:
            self.reduced = [_sum_chips(s, r, self.chip, self.core)
                            for s, r in zip(self.sums, list(outs) + self.from_chips)]
        else:
            self.result = list(outs)

    def carried(self, name, *args, **kwargs):
        if name not in self.STAGES:
            return _matmul(name, *args, **kwargs)
        out, outs = _matmul(name, *args, comm=self.comm(name), **kwargs)
        self.landed(name, outs)
        return out

    def run_alone(self):
        for stage, name in zip(self.STAGES, ("rs_sibling_exchange", "rs_chip_exchange_rows", "rs_chip_exchange_cols",
                                             "rs_sibling_share")):
            self.landed(stage, _run_comm(name, self.comm(stage)))


def kernel(x, mem, mix_norm, w_mix_in, gdn_conv, gdn_a_log, gdn_dt_bias, gdn_out_norm, sc_conv, w_mix_out, xattn_norm, mem_norm, w_xq, w_xk, w_xv, w_xo, ffn_norm, w_ffn_up, ffn_conv, w_ffn_down, final_norm, loss_target, m_mix_norm, m_w_mix_in, m_gdn_conv, m_gdn_a_log, m_gdn_dt_bias, m_gdn_out_norm, m_sc_conv, m_w_mix_out, m_xattn_norm, m_mem_norm, m_w_xq, m_w_xk, m_w_xv, m_w_xo, m_ffn_norm, m_w_ffn_up, m_ffn_conv, m_w_ffn_down, m_final_norm, v_mix_norm, v_w_mix_in, v_gdn_conv, v_gdn_a_log, v_gdn_dt_bias, v_gdn_out_norm, v_sc_conv, v_w_mix_out, v_xattn_norm, v_mem_norm, v_w_xq, v_w_xk, v_w_xv, v_w_xo, v_ffn_norm, v_w_ffn_up, v_ffn_conv, v_w_ffn_down, v_final_norm):
    w = dict(zip(_WEIGHTS, (mix_norm, w_mix_in, gdn_conv, gdn_a_log, gdn_dt_bias, gdn_out_norm, sc_conv, w_mix_out,
                            xattn_norm, mem_norm, w_xq, w_xk, w_xv, w_xo, ffn_norm, w_ffn_up, ffn_conv, w_ffn_down,
                            final_norm)))
    m = dict(zip(_WEIGHTS, (m_mix_norm, m_w_mix_in, m_gdn_conv, m_gdn_a_log, m_gdn_dt_bias, m_gdn_out_norm, m_sc_conv,
                            m_w_mix_out, m_xattn_norm, m_mem_norm, m_w_xq, m_w_xk, m_w_xv, m_w_xo, m_ffn_norm,
                            m_w_ffn_up, m_ffn_conv, m_w_ffn_down, m_final_norm)))
    v = dict(zip(_WEIGHTS, (v_mix_norm, v_w_mix_in, v_gdn_conv, v_gdn_a_log, v_gdn_dt_bias, v_gdn_out_norm, v_sc_conv,
                            v_w_mix_out, v_xattn_norm, v_mem_norm, v_w_xq, v_w_xk, v_w_xv, v_w_xo, v_ffn_norm,
                            v_w_ffn_up, v_ffn_conv, v_w_ffn_down, v_final_norm)))
    core = lax.axis_index("c")
    chip = 2 * lax.axis_index("x") + lax.axis_index("y")
    depth, heads = gdn_a_log.shape
    dh = gdn_out_norm.shape[1]
    d, wid = x.shape[2], heads * dh

    row_sizes = [w[n].shape[1] for n in _ROWS]
    row_offs = [sum(row_sizes[:k]) for k in range(len(_ROWS))]
    src_in, src_up = w_mix_in.astype(WIRE_DTYPE), w_ffn_up.astype(WIRE_DTYPE)
    src_rows = jnp.concatenate([w[n] for n in _ROWS], axis=1).astype(WIRE_DTYPE)
    src_convs = _pad_rows(jnp.concatenate([w[n].reshape(-1) for n in _CONVS]), 2)
    g_in, g_convs = _run_comm("allgather_first", _allgather_comm([src_in[0], src_convs]))
    conv_full = {n: jnp.moveaxis(part, 0, 2).reshape(depth, part.shape[2], -1)
                 for n, part in zip(_CONVS, _split_flat(g_convs.reshape(N_CHIPS, -1), [w[n].shape for n in _CONVS]))}
    side_by_side = lambda g: jnp.concatenate([g[j] for j in range(N_CHIPS)], axis=1)

    def from_rows(brought):
        g_rows, = brought
        return _square_projections(*[jnp.concatenate([g_rows[j, off:off + size] for j in range(N_CHIPS)], axis=0)
                                     for off, size in zip(row_offs, row_sizes)])

    xl, mem_l = x[0], mem[0]
    layers, saved, g_rows = [], [], None
    for l in range(depth):
        p = {"mix_norm": mix_norm[l], "xattn_norm": xattn_norm[l], "mem_norm": mem_norm[l], "ffn_norm": ffn_norm[l],
             "gdn_out_norm": gdn_out_norm[l], "a_log": gdn_a_log[l].reshape(heads, 1, 1),
             "dt_bias": gdn_dt_bias[l].reshape(heads, 1, 1), "gdn_conv": conv_full["gdn_conv"][l],
             "sc_conv": conv_full["sc_conv"][l], "ffn_conv": conv_full["ffn_conv"][l]}
        p.update(_input_projection(side_by_side(g_in), heads, dh))
        carry = {"gdn_prep_fwd": _allgather_comm([src_up[l]])}
        late = {"gdn_prep_fwd": lambda brought: {"wup": side_by_side(brought[0])}}
        if l == 0:
            carry["mm_mix_in"], late["mm_mix_in"] = _allgather_comm([src_rows[0]]), from_rows
        else:
            p.update(from_rows(g_rows))
        if l + 1 < depth:
            carry["mm_ffn_up"] = _allgather_comm([src_rows[l + 1]])
            carry["mm_ffn_down"] = _allgather_comm([src_in[l + 1]])
        xl, s, landed, p = _layer_fwd(xl, mem_l, p, heads, dh, carry, late)
        layers.append(p)
        saved.append(s)
        if l + 1 < depth:
            g_rows, (g_in,) = landed["mm_ffn_up"], landed["mm_ffn_down"]
    loss_row, dx, dxb, g_final = _final_loss(xl, final_norm, loss_target[0])

    def by_chip(g):
        g_win = jnp.concatenate([g["wmain"][:, :4 * wid], g["wba"][:, :2 * heads], g["wmain"][:, 4 * wid:]], axis=1)
        parts = (g["wout"], g["wq"], g["wkv"][:, :d], g["wkv"][:, d:], g["wo"], g["wdown"])
        return [_halves_by_chip(_by_chip(g_win, 1)), _halves_by_chip(_by_chip(g["wup"], 1)),
                _halves_by_chip(jnp.concatenate([_by_chip(p, 0) for p in parts], axis=1))]

    per_layer, shards, reduce = [None] * depth, [None] * depth, None
    for l in reversed(range(depth)):
        dx, dxb, per_layer[l] = _layer_bwd(dx, dxb, mem_l, saved[l], layers[l], heads, dh, reduce)
        if reduce is not None:
            shards[l + 1] = reduce.result
        reduce = _ReduceScatter(by_chip(per_layer[l]), chip, core)
    reduce.run_alone()
    shards[0] = reduce.result
    g_in_s, g_up_s, g_rows_s = (jnp.stack([s[t].reshape(-1, s[t].shape[-1]) for s in shards]) for t in range(3))
    grad = {"w_mix_in": g_in_s, "w_ffn_up": g_up_s}
    for n, off, size in zip(_ROWS, row_offs, row_sizes):
        grad[n] = g_rows_s[:, off:off + size]

    stack = lambda k: jnp.stack([g[k] for g in per_layer])
    small_g = {"mix_norm": stack("mix_norm"), "gdn_a_log": stack("a_log").reshape(depth, heads),
               "gdn_dt_bias": stack("dt_bias").reshape(depth, heads), "gdn_out_norm": stack("gdn_out_norm"),
               "xattn_norm": stack("xattn_norm"), "mem_norm": stack("mem_norm"), "ffn_norm": stack("ffn_norm"),
               "final_norm": g_final, "gdn_conv": stack("gdn_conv"), "sc_conv": stack("sc_conv"),
               "ffn_conv": stack("ffn_conv")}
    names = _REPLICATED + _CONVS
    small = jnp.concatenate([small_g[n].reshape(-1) for n in names] + [loss_row[0, :1]])
    small_sum = _allreduce_small(_pad_rows(small, 1)[0]).reshape(-1)
    parts = _split_flat(small_sum, [small_g[n].shape for n in names] + [(1,)])
    g_rep = dict(zip(_REPLICATED, parts[:len(_REPLICATED)]))
    for n, part in zip(_CONVS, parts[len(_REPLICATED):-1]):
        grad[n] = lax.dynamic_slice_in_dim(part, chip * w[n].shape[2], w[n].shape[2], axis=2)
    loss = parts[-1][0]

    delta, new_m, new_v = {}, {}, {}
    for n in ("w_mix_in", "w_ffn_up") + _ROWS + _CONVS:
        delta[n], new_m[n], new_v[n] = _adamw("adamw_" + n, w[n], grad[n], m[n], v[n])
    pack_rep = lambda t: _pad_rows(jnp.concatenate([t[n].reshape(-1) for n in _REPLICATED]), 1)[0]
    outs = _adamw("adamw_replicated", pack_rep(w), pack_rep(g_rep), pack_rep(m), pack_rep(v))
    shapes = [w[n].shape for n in _REPLICATED]
    for tgt, packed_out in zip((delta, new_m, new_v), outs):
        tgt.update(zip(_REPLICATED, _split_flat(packed_out.reshape(-1), shapes)))
    grad.update(g_rep)
    return (loss, dx[None], *[grad[n] for n in _WEIGHTS], *[delta[n] for n in _WEIGHTS],
            *[new_m[n] for n in _WEIGHTS], *[new_v[n] for n in _WEIGHTS])
```

```python
import functools

import jax
import jax.numpy as jnp
from jax import lax
from jax.experimental import pallas as pl
from jax.experimental.pallas import tpu as pltpu

F32 = jnp.float32
MXU_DTYPE = jnp.bfloat16
WIRE_DTYPE = jnp.bfloat16
SOLVE_PRECISION = lax.Precision.HIGH
EPS = 1e-6
CHUNK = 64
XATTN_HEADS = 4
LANES = 128
HALO = 16
EW_ROWS, EW_COLS = 256, 2816
VMEM_LIMIT = 52 * 1024 * 1024
ADAM_LR, ADAM_B1, ADAM_B2, ADAM_EPS, ADAM_WD, ADAM_STEP = 0.001, 0.9, 0.999, 1e-08, 0.01, 10
MESH = pl.DeviceIdType.MESH
N_CHIPS = 4
N_DEV = 8

_DIMS = {
    "nn": (((1,), (0,)), ((), ())),
    "nt": (((1,), (1,)), ((), ())),
    "tn": (((0,), (0,)), ((), ())),
}


def _tile(n, pref, align=LANES):
    if n <= pref:
        return n
    t = (pref // align) * align
    while t >= align:
        if n % t == 0:
            return t
        t -= align
    return n


def _params(*sem):
    return pltpu.CompilerParams(dimension_semantics=sem, vmem_limit_bytes=VMEM_LIMIT)


def _dot(a, b, form, hi=False):
    (ca, cb), _ = _DIMS[form]
    dims = (((ca[0] + 1,), (cb[0] + 1,)), ((0,), (0,))) if a.ndim == 3 else _DIMS[form]
    if hi:
        return lax.dot_general(a.astype(F32), b.astype(F32), dims, precision=SOLVE_PRECISION,
                               preferred_element_type=F32)
    return lax.dot_general(a.astype(MXU_DTYPE), b.astype(MXU_DTYPE), dims, preferred_element_type=F32)


@functools.partial(jax.custom_vjp, nondiff_argnums=(2, 3))
def _dot_d(a, b, form, hi):
    return _dot(a, b, form, hi)


def _dot_d_fwd(a, b, form, hi):
    return _dot(a, b, form, hi), (a, b)


def _dot_d_bwd(form, hi, res, g):
    a, b = res
    if form == "nn":
        da, db = _dot_d(g, b, "nt", hi), _dot_d(a, g, "tn", hi)
    elif form == "nt":
        da, db = _dot_d(g, b, "nn", hi), _dot_d(g, a, "tn", hi)
    else:
        da, db = _dot_d(b, g, "nt", hi), _dot_d(a, g, "nn", hi)
    return da.astype(a.dtype), db.astype(b.dtype)


_dot_d.defvjp(_dot_d_fwd, _dot_d_bwd)


def _tri_inv_impl(a, mmh):
    c = a.shape[-1]
    r = lax.broadcasted_iota(jnp.int32, (c, c), 0)
    s = lax.broadcasted_iota(jnp.int32, (c, c), 1)
    eye = (r == s).astype(F32)
    diag_blk = (r // 16) == (s // 16)
    d = jnp.where(diag_blk, a, 0.0)
    low = a - d
    d2 = mmh(d, d)
    d4 = mmh(d2, d2)
    d8 = mmh(d4, d4)
    td = mmh(mmh(mmh(eye - d, eye + d2), eye + d4), eye + d8)
    n = mmh(td, low)
    acc = eye - n
    p = n
    pw = 1
    while 2 * pw < c // 16:
        p = mmh(p, p)
        pw *= 2
        acc = mmh(acc, eye + p)
    return mmh(acc, td)


def _mmh_plain(a, b):
    return _dot(a, b, "nn", True)


@jax.custom_vjp
def _tri_inv_known(a, t):
    return t


def _tri_inv_known_fwd(a, t):
    return t, t


def _tri_inv_known_bwd(t, g):
    return -_dot(_dot(t, g, "tn", True), t, "nt", True), jnp.zeros_like(t)


_tri_inv_known.defvjp(_tri_inv_known_fwd, _tri_inv_known_bwd)


class _Ops:
    def __init__(self, diff, tinv=None):
        self.diff, self.tinv = diff, tinv

    def mm(self, a, b, form="nn"):
        return _dot_d(a, b, form, False) if self.diff else _dot(a, b, form, False)

    def mmh(self, a, b, form="nn"):
        return _dot_d(a, b, form, True) if self.diff else _dot(a, b, form, True)

    def tri_inv(self, a):
        return _tri_inv_known(a, self.tinv) if self.diff else _tri_inv_impl(a, _mmh_plain)


_PLAIN = _Ops(False)
_DIFF = _Ops(True)


def _sigmoid(x):
    return 1.0 / (1.0 + jnp.exp(-x))


def _silu(x):
    return x * _sigmoid(x)


def _softplus(x):
    return jnp.maximum(x, 0.0) + jnp.log(1.0 + jnp.exp(-jnp.abs(x)))


def _rms(x, g):
    return x * lax.rsqrt(jnp.mean(x * x, axis=-1, keepdims=True) + EPS) * g


def _matmul_tiles(m, n, k, form):
    if k <= 2048:
        return _tile(m, 1024), _tile(n, 1408), k
    if k <= 8192:
        return _tile(m, 1024 if form == "nn" else 512), _tile(n, 512), k
    return _tile(m, 1024), _tile(n, 1024), _tile(k, 2816)


def _matmul(name, a, b, form, out_dtype, add=None, comm=None):
    b_shape = b.shape if b.ndim == 2 else (b.shape[1], N_CHIPS * b.shape[2])
    if form == "nn":
        (m, k), (k2, n) = a.shape, b_shape
    elif form == "nt":
        (m, k), (n, k2) = a.shape, b_shape
    else:
        (k, m), (k2, n) = a.shape, b_shape
    assert k == k2, (name, a.shape, b.shape, form)
    tm, tn, tk = _matmul_tiles(m, n, k, form)
    if b.ndim == 3:
        assert form != "tn", name
        tn, tk = (_tile(b.shape[2], tn), tk) if form == "nn" else (tn, _tile(b.shape[2], tk))
    nk = k // tk
    out_bytes = tm * tn * (jnp.dtype(out_dtype).itemsize + (4 if add is not None else 0))
    vmem = 2 * (tm * tk * a.dtype.itemsize + tk * tn * b.dtype.itemsize + out_bytes) + (tm * tn * 4 if nk > 1 else 0)
    assert vmem <= VMEM_LIMIT, (name, tm, tn, tk, vmem)
    if form == "nn":
        a_spec = pl.BlockSpec((tm, tk), lambda i, j, kk: (i, kk))
        b_spec = pl.BlockSpec((tk, tn), lambda i, j, kk: (kk, j))
    elif form == "nt":
        a_spec = pl.BlockSpec((tm, tk), lambda i, j, kk: (i, kk))
        b_spec = pl.BlockSpec((tn, tk), lambda i, j, kk: (j, kk))
    else:
        a_spec = pl.BlockSpec((tk, tm), lambda i, j, kk: (kk, i))
        b_spec = pl.BlockSpec((tk, tn), lambda i, j, kk: (kk, j))
    if b.ndim == 3:
        per = b.shape[2] // (tn if form == "nn" else tk)
        if form == "nn":
            b_spec = pl.BlockSpec((None, tk, tn), lambda i, j, kk: (j // per, kk, j % per))
        else:
            b_spec = pl.BlockSpec((None, tn, tk), lambda i, j, kk: (kk // per, j, kk % per))
    o_spec = pl.BlockSpec((tm, tn), lambda i, j, kk: (i, j))
    has_add = add is not None
    grid = (m // tm, n // tn, nk)
    n_in = 3 if has_add else 2
    c_in, c_out = (len(comm.ins), len(comm.outs)) if comm is not None else (0, 0)

    def body(*refs):
        a_ref, b_ref = refs[0], refs[1]
        add_ref = refs[2] if has_add else None
        o_ref = refs[n_in + c_in]
        pids = [pl.program_id(ax) for ax in range(3)]
        if comm is not None:
            comm_refs = (refs[n_in:n_in + c_in], refs[n_in + c_in + 1:n_in + c_in + 1 + c_out], refs[-2:])

            @pl.when(jnp.logical_and(jnp.logical_and(pids[0] == 0, pids[1] == 0), pids[2] == 0))
            def _():
                comm.start(*comm_refs)

        def finish(acc):
            if has_add:
                acc = acc + add_ref[...].astype(F32)
            o_ref[...] = acc.astype(o_ref.dtype)

        p = _dot(a_ref[...], b_ref[...], form)
        if nk == 1:
            finish(p)
        else:
            acc_ref = refs[n_in + c_in + 1 + c_out]

            @pl.when(pids[2] == 0)
            def _():
                acc_ref[...] = p

            @pl.when(pids[2] > 0)
            def _():
                acc_ref[...] += p

            @pl.when(pids[2] == nk - 1)
            def _():
                finish(acc_ref[...])

        if comm is not None:
            @pl.when(jnp.logical_and(jnp.logical_and(pids[0] == grid[0] - 1, pids[1] == grid[1] - 1),
                                     pids[2] == grid[2] - 1))
            def _():
                comm.finish(*comm_refs)

    acc_scratch = [pltpu.VMEM((tm, tn), F32)] if nk > 1 else []
    if comm is None:
        return pl.pallas_call(
            body, name=name, grid=grid, in_specs=[a_spec, b_spec] + ([o_spec] if has_add else []), out_specs=o_spec,
            out_shape=jax.ShapeDtypeStruct((m, n), out_dtype), scratch_shapes=acc_scratch,
            compiler_params=_params("parallel", "parallel", "arbitrary"),
        )(*((a, b, add) if has_add else (a, b)))
    outs = pl.pallas_call(
        body, name=name, grid=grid, in_specs=[a_spec, b_spec] + ([o_spec] if has_add else []) + [_ANY] * c_in,
        out_specs=[o_spec] + [_ANY] * c_out, out_shape=[jax.ShapeDtypeStruct((m, n), out_dtype)] + list(comm.outs),
        scratch_shapes=acc_scratch + _sem_pairs(comm.n_sems),
        input_output_aliases={n_in + i: 1 + o for i, o in comm.aliases.items()},
        compiler_params=_params("arbitrary", "arbitrary", "arbitrary"),
    )(*((a, b, add) if has_add else (a, b)), *comm.ins)
    return outs[0], list(outs[1:])


def _rms_fwd(name, x, g):
    t, d = x.shape
    tm = _tile(t, 512, 16)

    def body(x_ref, g_ref, o_ref):
        o_ref[...] = _rms(x_ref[...], g_ref[...]).astype(o_ref.dtype)

    return pl.pallas_call(
        body, name=name, grid=(t // tm,),
        in_specs=[pl.BlockSpec((tm, d), lambda i: (i, 0)), pl.BlockSpec((1, d), lambda i: (0, 0))],
        out_specs=pl.BlockSpec((tm, d), lambda i: (i, 0)),
        out_shape=jax.ShapeDtypeStruct((t, d), MXU_DTYPE), compiler_params=_params("parallel"),
    )(x, g.reshape(1, d))


def _rms_bwd(name, x, g, dh, dres=None):
    t, d = x.shape
    tm = _tile(t, 256, 16)
    has_res = dres is not None

    def body(*refs):
        x_ref, g_ref, dh_ref = refs[:3]
        dres_ref = refs[3] if has_res else None
        dx_ref, dxb_ref, dg_ref = refs[-3:]
        _, vjp = jax.vjp(_rms, x_ref[...], g_ref[...])
        dx, dg = vjp(dh_ref[...].astype(F32))
        if has_res:
            dx = dx + dres_ref[...]
        dx_ref[...] = dx
        dxb_ref[...] = dx.astype(dxb_ref.dtype)
        first = pl.program_id(0) == 0

        @pl.when(first)
        def _():
            dg_ref[...] = dg

        @pl.when(jnp.logical_not(first))
        def _():
            dg_ref[...] += dg

    row = pl.BlockSpec((tm, d), lambda i: (i, 0))
    vec = pl.BlockSpec((1, d), lambda i: (0, 0))
    dx, dxb, dg = pl.pallas_call(
        body, name=name, grid=(t // tm,),
        in_specs=[row, vec, row] + ([row] if has_res else []), out_specs=[row, row, vec],
        out_shape=[jax.ShapeDtypeStruct((t, d), F32), jax.ShapeDtypeStruct((t, d), MXU_DTYPE),
                   jax.ShapeDtypeStruct((1, d), F32)],
        compiler_params=_params("arbitrary"),
    )(*((x, g.reshape(1, d), dh) + ((dres,) if has_res else ())))
    return dx, dxb, dg.reshape(d)


def _final_loss(x, g, target):
    t, d = x.shape
    tm = _tile(t, 256, 16)

    def body(x_ref, g_ref, t_ref, loss_ref, dx_ref, dxb_ref, dg_ref):
        y, vjp = jax.vjp(_rms, x_ref[...], g_ref[...])
        err = y - t_ref[...]
        dx, dg = vjp(err * (1.0 / d))
        dx_ref[...] = dx
        dxb_ref[...] = dx.astype(dxb_ref.dtype)
        part = jnp.zeros((1, LANES), F32) + 0.5 * jnp.sum(jnp.mean(err * err, axis=-1, keepdims=True))
        first = pl.program_id(0) == 0

        @pl.when(first)
        def _():
            dg_ref[...] = dg
            loss_ref[...] = part

        @pl.when(jnp.logical_not(first))
        def _():
            dg_ref[...] += dg
            loss_ref[...] += part

    row = pl.BlockSpec((tm, d), lambda i: (i, 0))
    vec = pl.BlockSpec((1, d), lambda i: (0, 0))
    loss, dx, dxb, dg = pl.pallas_call(
        body, name="final_loss", grid=(t // tm,), in_specs=[row, vec, row],
        out_specs=[pl.BlockSpec((1, LANES), lambda i: (0, 0)), row, row, vec],
        out_shape=[jax.ShapeDtypeStruct((1, LANES), F32), jax.ShapeDtypeStruct((t, d), F32),
                   jax.ShapeDtypeStruct((t, d), MXU_DTYPE), jax.ShapeDtypeStruct((1, d), F32)],
        compiler_params=_params("arbitrary"),
    )(x, g.reshape(1, d), target)
    return loss, dx, dxb, dg.reshape(d)


def _conv_taps(x_ext, w, rows):
    kk = w.shape[0]
    y = x_ext[HALO:] * w[kk - 1:kk, :]
    for j in range(kk - 1):
        y = y + pltpu.roll(x_ext, kk - 1 - j, axis=0)[HALO:] * w[j:j + 1, :]
    return y


def _col_specs(tm, tn, col0, t_rows):
    assert col0 % tn == 0 and tm % HALO == 0
    c0 = col0 // tn
    per, last = tm // HALO, t_rows // HALO - 1
    tile = pl.BlockSpec((tm, tn), lambda j, i: (i, c0 + j))
    prev = pl.BlockSpec((HALO, tn), lambda j, i: (jnp.maximum(i * per - 1, 0), c0 + j))
    nxt = pl.BlockSpec((HALO, tn), lambda j, i: (jnp.minimum((i + 1) * per, last), c0 + j))
    return tile, prev, nxt


def _conv_fwd(name, xa, xa_col, w, w_col, ncols, out_dtype, xb=None, xb_col=0, gate=None, gate_col=0):
    t = xa.shape[0]
    kk = w.shape[0]
    tm, tn = _tile(t, EW_ROWS, HALO), _tile(ncols, EW_COLS)
    nrow = t // tm
    has_b, has_g = xb is not None, gate is not None

    def body(*refs):
        refs = list(refs)
        xa_ref, xap_ref = refs.pop(0), refs.pop(0)
        xb_ref, xbp_ref = (refs.pop(0), refs.pop(0)) if has_b else (None, None)
        w_ref = refs.pop(0)
        g_ref = refs.pop(0) if has_g else None
        o_ref = refs.pop(0)
        i = pl.program_id(1)
        x, xp = xa_ref[...].astype(F32), xap_ref[...].astype(F32)
        if has_b:
            x, xp = x * xb_ref[...].astype(F32), xp * xbp_ref[...].astype(F32)
        xp = jnp.where(i == 0, 0.0, xp)
        y = _conv_taps(jnp.concatenate([xp, x], axis=0), w_ref[...], tm)
        if has_g:
            y = y * g_ref[...].astype(F32)
        o_ref[...] = y.astype(o_ref.dtype)

    a_tile, a_prev, _ = _col_specs(tm, tn, xa_col, t)
    ins, specs = [xa, xa], [a_tile, a_prev]
    if has_b:
        b_tile, b_prev, _ = _col_specs(tm, tn, xb_col, t)
        ins, specs = ins + [xb, xb], specs + [b_tile, b_prev]
    assert w_col % tn == 0
    ins, specs = ins + [w], specs + [pl.BlockSpec((kk, tn), lambda j, i: (0, w_col // tn + j))]
    if has_g:
        ins, specs = ins + [gate], specs + [_col_specs(tm, tn, gate_col, t)[0]]
    return pl.pallas_call(
        body, name=name, grid=(ncols // tn, nrow), in_specs=specs,
        out_specs=pl.BlockSpec((tm, tn), lambda j, i: (i, j)),
        out_shape=jax.ShapeDtypeStruct((t, ncols), out_dtype), compiler_params=_params("parallel", "parallel"),
    )(*ins)


def _conv_bwd(name, xa, xa_col, w, w_col, dy, dy_col, ncols, dx_dtype, xb=None, xb_col=0, gate=None, gate_col=0):
    t = xa.shape[0]
    kk = w.shape[0]
    tm, tn = _tile(t, EW_ROWS, HALO), _tile(ncols, EW_COLS)
    nrow = t // tm
    has_b, has_g = xb is not None, gate is not None

    def body(*refs):
        refs = list(refs)
        xa_ref, xap_ref = refs.pop(0), refs.pop(0)
        xb_ref, xbp_ref = (refs.pop(0), refs.pop(0)) if has_b else (None, None)
        w_ref = refs.pop(0)
        dy_ref, dyn_ref = refs.pop(0), refs.pop(0)
        g_ref, gn_ref = (refs.pop(0), refs.pop(0)) if has_g else (None, None)
        dxa_ref = refs.pop(0)
        dxb_ref = refs.pop(0) if has_b else None
        dg_ref = refs.pop(0) if has_g else None
        dw_ref = refs.pop(0)
        i = pl.program_id(1)
        wv = w_ref[...]
        xa_t, xa_p = xa_ref[...].astype(F32), xap_ref[...].astype(F32)
        x, xp = xa_t, xa_p
        if has_b:
            xb_t = xb_ref[...].astype(F32)
            x, xp = x * xb_t, xp * xbp_ref[...].astype(F32)
        xp = jnp.where(i == 0, 0.0, xp)
        x_ext = jnp.concatenate([xp, x], axis=0)
        dyv, dyn = dy_ref[...].astype(F32), dyn_ref[...].astype(F32)
        if has_g:
            dg_ref[...] = (dyv * _conv_taps(x_ext, wv, tm)).astype(dg_ref.dtype)
            dyv, dyn = dyv * g_ref[...].astype(F32), dyn * gn_ref[...].astype(F32)
        dyn = jnp.where(i == nrow - 1, 0.0, dyn)
        dy_ext = jnp.concatenate([dyv, dyn], axis=0)
        dx = dyv * wv[kk - 1:kk, :]
        row8 = lax.broadcasted_iota(jnp.int32, (8, tn), 0)
        dw = jnp.where(row8 == kk - 1, jnp.sum(dyv * x, axis=0, keepdims=True), 0.0)
        for j in range(kk - 1):
            s = kk - 1 - j
            dx = dx + pltpu.roll(dy_ext, tm + HALO - s, axis=0)[:tm] * wv[j:j + 1, :]
            dwj = jnp.sum(dyv * pltpu.roll(x_ext, s, axis=0)[HALO:], axis=0, keepdims=True)
            dw = dw + jnp.where(row8 == j, dwj, 0.0)
        if has_b:
            dxa_ref[...] = (dx * xb_t).astype(dxa_ref.dtype)
            dxb_ref[...] = (dx * xa_t).astype(dxb_ref.dtype)
        else:
            dxa_ref[...] = dx.astype(dxa_ref.dtype)

        @pl.when(i == 0)
        def _():
            dw_ref[...] = dw

        @pl.when(i > 0)
        def _():
            dw_ref[...] += dw

    a_tile, a_prev, _ = _col_specs(tm, tn, xa_col, t)
    ins, specs = [xa, xa], [a_tile, a_prev]
    if has_b:
        b_tile, b_prev, _ = _col_specs(tm, tn, xb_col, t)
        ins, specs = ins + [xb, xb], specs + [b_tile, b_prev]
    assert w_col % tn == 0
    ins, specs = ins + [w], specs + [pl.BlockSpec((kk, tn), lambda j, i: (0, w_col // tn + j))]
    d_tile, _, d_next = _col_specs(tm, tn, dy_col, t)
    ins, specs = ins + [dy, dy], specs + [d_tile, d_next]
    if has_g:
        g_tile, _, g_next = _col_specs(tm, tn, gate_col, t)
        ins, specs = ins + [gate, gate], specs + [g_tile, g_next]
    out_tile = pl.BlockSpec((tm, tn), lambda j, i: (i, j))
    shapes, ospecs = [jax.ShapeDtypeStruct((t, ncols), dx_dtype)], [out_tile]
    if has_b:
        shapes, ospecs = shapes + [jax.ShapeDtypeStruct((t, ncols), dx_dtype)], ospecs + [out_tile]
    if has_g:
        shapes, ospecs = shapes + [jax.ShapeDtypeStruct((t, ncols), dx_dtype)], ospecs + [out_tile]
    shapes, ospecs = shapes + [jax.ShapeDtypeStruct((8, ncols), F32)], ospecs + [pl.BlockSpec((8, tn), lambda j, i: (0, j))]
    outs = list(pl.pallas_call(
        body, name=name, grid=(ncols // tn, nrow), in_specs=specs, out_specs=ospecs, out_shape=shapes,
        compiler_params=_params("parallel", "arbitrary"),
    )(*ins))
    dxa = outs.pop(0)
    dxb = outs.pop(0) if has_b else None
    dgate = outs.pop(0) if has_g else None
    return dxa, dxb, dgate, outs.pop(0)[:kk]


def _swiglu_fwd(u):
    t, f2 = u.shape
    f = f2 // 2
    tm, tn = _tile(t, EW_ROWS, 16), _tile(f, EW_COLS)
    nf = f // tn

    def body(g_ref, u_ref, o_ref):
        o_ref[...] = (_silu(g_ref[...].astype(F32)) * u_ref[...].astype(F32)).astype(o_ref.dtype)

    return pl.pallas_call(
        body, name="swiglu_fwd", grid=(t // tm, nf),
        in_specs=[pl.BlockSpec((tm, tn), lambda i, j: (i, j)), pl.BlockSpec((tm, tn), lambda i, j: (i, nf + j))],
        out_specs=pl.BlockSpec((tm, tn), lambda i, j: (i, j)),
        out_shape=jax.ShapeDtypeStruct((t, f), MXU_DTYPE), compiler_params=_params("parallel", "parallel"),
    )(u, u)


def _swiglu_bwd(u, da):
    t, f2 = u.shape
    f = f2 // 2
    tm, tn = _tile(t, EW_ROWS, 16), _tile(f, EW_COLS)
    nf = f // tn

    def body(g_ref, u_ref, da_ref, o_ref):
        g, d = g_ref[...].astype(F32), da_ref[...].astype(F32)
        sg = _sigmoid(g)
        gate_half = pl.program_id(1) < nf

        @pl.when(gate_half)
        def _():
            o_ref[...] = (d * u_ref[...].astype(F32) * (sg * (1.0 + g * (1.0 - sg)))).astype(o_ref.dtype)

        @pl.when(jnp.logical_not(gate_half))
        def _():
            o_ref[...] = (d * (g * sg)).astype(o_ref.dtype)

    return pl.pallas_call(
        body, name="swiglu_bwd", grid=(t // tm, 2 * nf),
        in_specs=[pl.BlockSpec((tm, tn), lambda i, j: (i, j % nf)),
                  pl.BlockSpec((tm, tn), lambda i, j: (i, nf + j % nf)),
                  pl.BlockSpec((tm, tn), lambda i, j: (i, j % nf))],
        out_specs=pl.BlockSpec((tm, tn), lambda i, j: (i, j)),
        out_shape=jax.ShapeDtypeStruct((t, f2), MXU_DTYPE), compiler_params=_params("parallel", "parallel"),
    )(u, u, da)


def _gdn_prep(ops, qc, kc, vc, b_col, a_col, a_log, dt_bias):
    c, dh = qc.shape[-2:]
    q, k, v = _silu(qc), _silu(kc), _silu(vc)
    q = q * lax.rsqrt(jnp.sum(q * q, axis=-1, keepdims=True) + EPS) * (dh ** -0.5)
    k = k * lax.rsqrt(jnp.sum(k * k, axis=-1, keepdims=True) + EPS)
    beta = _sigmoid(b_col)
    g_col = -jnp.exp(a_log) * _softplus(a_col + dt_bias)
    r = lax.broadcasted_iota(jnp.int32, (c, c), 0)
    s = lax.broadcasted_iota(jnp.int32, (c, c), 1)
    g_row = jnp.sum(jnp.where(r == s, g_col, 0.0), axis=-2, keepdims=True)
    gc_col = jnp.sum(jnp.where(s <= r, g_row, 0.0), axis=-1, keepdims=True)
    gc_row = jnp.sum(jnp.where(r <= s, g_col, 0.0), axis=-2, keepdims=True)
    decay = jnp.exp(jnp.where(s <= r, gc_col - gc_row, -1e30))
    kb = k * beta
    a = jnp.where(s < r, ops.mm(kb, k, "nt") * decay, 0.0)
    tinv = ops.tri_inv(a)
    e_col = jnp.exp(gc_col)
    uw = ops.mmh(tinv, jnp.concatenate([v * beta, kb * e_col], axis=-1))
    u, w = uw[..., :dh], uw[..., dh:]
    attn = ops.mm(q, k, "nt") * decay
    g_last = jnp.sum(g_col, axis=-2, keepdims=True)
    return u, w, attn, q * e_col, k * jnp.exp(g_last - gc_col), g_last, tinv


def _gdn_step(ops, state, u, w, attn, q_dec, k_dec, g_last):
    v_new = u - ops.mm(w, state)
    o = ops.mm(q_dec, state) + ops.mm(attn, v_new)
    return o, state * jnp.exp(g_last) + ops.mm(k_dec, v_new, "tn")


PREP_HEADS, SCAN_HEADS = 4, 8


def _gdn_blocks(t, heads, hb_pref):
    tc = _tile(t, 256, CHUNK)
    hb = max(h for h in range(1, hb_pref + 1) if heads % h == 0)
    return tc, hb


def _to_chunks(ref, hb, dh):
    tc = ref.shape[0]
    return jnp.concatenate([ref[:, h * dh:(h + 1) * dh].astype(F32).reshape(tc // CHUNK, CHUNK, dh)
                            for h in range(hb)], axis=0)


def _from_chunks(ref, val, hb, dh):
    tc = ref.shape[0]
    ncb = tc // CHUNK
    for h in range(hb):
        ref[:, h * dh:(h + 1) * dh] = val[h * ncb:(h + 1) * ncb].reshape(tc, dh).astype(ref.dtype)


def _per_chunk(s, ncb):
    hb = s.shape[0]
    return jnp.broadcast_to(s[:, None], (hb, ncb, 1, 1)).reshape(hb * ncb, 1, 1)


def _gate_columns(pba, first_head, hb, heads):
    tc = pba.shape[0]
    lane = lax.broadcasted_iota(jnp.int32, pba.shape, 1)

    def pick(k):
        return jnp.sum(jnp.where(lane == k, pba, 0.0), axis=1, keepdims=True).reshape(tc // CHUNK, CHUNK, 1)

    return (jnp.concatenate([pick(first_head + h) for h in range(hb)], axis=0),
            jnp.concatenate([pick(heads + first_head + h) for h in range(hb)], axis=0))


def _gdn_prep_fwd(qkv, pba, a_log, dt_bias, heads, dh, comm=None):
    t = qkv.shape[0]
    tc, hb = _gdn_blocks(t, heads, PREP_HEADS)
    ncb, nhb, width = tc // CHUNK, heads // hb, heads * dh
    nc = t // CHUNK
    grid = (t // tc, nhb)
    c_in, c_out = (len(comm.ins), len(comm.outs)) if comm is not None else (0, 0)

    def body(*refs):
        q_ref, k_ref, v_ref, g_ref, al_ref, dt_ref = refs[:6]
        u_ref, w_ref, p_ref, qd_ref, kd_ref, gl_ref, ti_ref = refs[6 + c_in:13 + c_in]
        if comm is not None:
            comm_refs = (refs[6:6 + c_in], refs[13 + c_in:13 + c_in + c_out], refs[-2:])

            @pl.when(jnp.logical_and(pl.program_id(0) == 0, pl.program_id(1) == 0))
            def _():
                comm.start(*comm_refs)

        b_col, a_col = _gate_columns(g_ref[...], pl.program_id(1) * hb, hb, heads)
        u, w, p, qd, kd, gl, tinv = _gdn_prep(
            _PLAIN, _to_chunks(q_ref, hb, dh), _to_chunks(k_ref, hb, dh), _to_chunks(v_ref, hb, dh), b_col, a_col,
            _per_chunk(al_ref[...], ncb), _per_chunk(dt_ref[...], ncb))
        _from_chunks(u_ref, u, hb, dh)
        _from_chunks(w_ref, w, hb, dh)
        _from_chunks(qd_ref, qd, hb, dh)
        _from_chunks(kd_ref, kd, hb, dh)
        p_ref[...] = p.reshape(hb, tc, CHUNK).astype(p_ref.dtype)
        gl_ref[...] = gl.reshape(hb, ncb, 1, 1)
        ti_ref[...] = tinv.reshape(hb, tc, CHUNK)
        if comm is not None:
            @pl.when(jnp.logical_and(pl.program_id(0) == grid[0] - 1, pl.program_id(1) == grid[1] - 1))
            def _():
                comm.finish(*comm_refs)

    def tok(off):
        return pl.BlockSpec((tc, hb * dh), lambda i, j: (i, off * nhb + j))

    gate = pl.BlockSpec((tc, LANES), lambda i, j: (i, 0))
    scal = pl.BlockSpec((hb, 1, 1), lambda i, j: (j, 0, 0))
    square = pl.BlockSpec((hb, tc, CHUNK), lambda i, j: (j, i, 0))
    outs = pl.pallas_call(
        body, name="gdn_prep_fwd", grid=grid,
        in_specs=[tok(0), tok(1), tok(2), gate, scal, scal] + [_ANY] * c_in,
        out_specs=[tok(0), tok(0), square, tok(0), tok(0), pl.BlockSpec((hb, ncb, 1, 1), lambda i, j: (j, i, 0, 0)),
                   square] + [_ANY] * c_out,
        out_shape=[jax.ShapeDtypeStruct((t, width), F32), jax.ShapeDtypeStruct((t, width), MXU_DTYPE),
                   jax.ShapeDtypeStruct((heads, t, CHUNK), MXU_DTYPE), jax.ShapeDtypeStruct((t, width), MXU_DTYPE),
                   jax.ShapeDtypeStruct((t, width), MXU_DTYPE), jax.ShapeDtypeStruct((heads, nc, 1, 1), F32),
                   jax.ShapeDtypeStruct((heads, t, CHUNK), F32)] + (list(comm.outs) if comm is not None else []),
        scratch_shapes=_sem_pairs(comm.n_sems) if comm is not None else [],
        compiler_params=_params("arbitrary", "arbitrary") if comm is not None else _params("parallel", "parallel"),
    )(qkv, qkv, qkv, pba, a_log, dt_bias, *(comm.ins if comm is not None else ()))
    return tuple(outs[:6]), outs[6], list(outs[7:])


def _gdn_prep_bwd(qkv, pba, a_log, dt_bias, tinv, du, dw, dp, dqd, dkd, dgl, heads, dh):
    t = qkv.shape[0]
    tc, hb = _gdn_blocks(t, heads, PREP_HEADS)
    ncb, nhb, width = tc // CHUNK, heads // hb, heads * dh

    def body(q_ref, k_ref, v_ref, g_ref, al_ref, dt_ref, ti_ref, du_ref, dw_ref, dp_ref, dqd_ref, dkd_ref, dgl_ref,
             dq_ref, dk_ref, dv_ref, dg_ref, dal_ref, ddt_ref):
        first_head = pl.program_id(1) * hb
        b_col, a_col = _gate_columns(g_ref[...], first_head, hb, heads)
        ops = _Ops(True, ti_ref[...].reshape(hb * ncb, CHUNK, CHUNK))

        def prep(q, k, v, b, a, al, dt):
            return _gdn_prep(ops, q, k, v, b, a, _per_chunk(al, ncb), _per_chunk(dt, ncb))[:6]

        _, vjp = jax.vjp(prep, _to_chunks(q_ref, hb, dh), _to_chunks(k_ref, hb, dh), _to_chunks(v_ref, hb, dh),
                         b_col, a_col, al_ref[...], dt_ref[...])
        dq, dk, dv, db, da, dal, ddt = vjp((
            _to_chunks(du_ref, hb, dh), _to_chunks(dw_ref, hb, dh), dp_ref[...].reshape(hb * ncb, CHUNK, CHUNK),
            _to_chunks(dqd_ref, hb, dh), _to_chunks(dkd_ref, hb, dh), dgl_ref[...].reshape(hb * ncb, 1, 1)))
        _from_chunks(dq_ref, dq, hb, dh)
        _from_chunks(dk_ref, dk, hb, dh)
        _from_chunks(dv_ref, dv, hb, dh)
        dal_ref[...] = dal[None]
        ddt_ref[...] = ddt[None]
        lane = lax.broadcasted_iota(jnp.int32, (tc, LANES), 1)
        dgates = jnp.zeros((tc, LANES), F32)
        for h in range(hb):
            rows = slice(h * ncb, (h + 1) * ncb)
            dgates = dgates + jnp.where(lane == first_head + h, db[rows].reshape(tc, 1), 0.0) \
                + jnp.where(lane == heads + first_head + h, da[rows].reshape(tc, 1), 0.0)

        @pl.when(first_head == 0)
        def _():
            dg_ref[...] = dgates

        @pl.when(first_head > 0)
        def _():
            dg_ref[...] += dgates

    def tok(off):
        return pl.BlockSpec((tc, hb * dh), lambda i, j: (i, off * nhb + j))

    gate = pl.BlockSpec((tc, LANES), lambda i, j: (i, 0))
    scal = pl.BlockSpec((hb, 1, 1), lambda i, j: (j, 0, 0))
    part = pl.BlockSpec((1, hb, 1, 1), lambda i, j: (i, j, 0, 0))
    pspec = pl.BlockSpec((hb, tc, CHUNK), lambda i, j: (j, i, 0))
    glspec = pl.BlockSpec((hb, ncb, 1, 1), lambda i, j: (j, i, 0, 0))
    tokf = jax.ShapeDtypeStruct((t, width), F32)
    partf = jax.ShapeDtypeStruct((t // tc, heads, 1, 1), F32)
    return pl.pallas_call(
        body, name="gdn_prep_bwd", grid=(t // tc, nhb),
        in_specs=[tok(0), tok(1), tok(2), gate, scal, scal, pspec, tok(0), tok(0), pspec, tok(0), tok(0), glspec],
        out_specs=[tok(0), tok(0), tok(0), gate, part, part],
        out_shape=[tokf, tokf, tokf, jax.ShapeDtypeStruct((t, LANES), F32), partf, partf],
        compiler_params=_params("parallel", "arbitrary"),
    )(qkv, qkv, qkv, pba, a_log, dt_bias, tinv, du, dw, dp, dqd, dkd, dgl)


def _heads(ref, rows, hb, dh):
    return jnp.stack([ref[rows, h * dh:(h + 1) * dh].astype(F32) for h in range(hb)])


def _put_heads(ref, rows, val, dh):
    for h in range(val.shape[0]):
        ref[rows, h * dh:(h + 1) * dh] = val[h].astype(ref.dtype)


def _gdn_scan_fwd(u, w, p, qd, kd, gl, heads, dh):
    t = u.shape[0]
    tc, hb = _gdn_blocks(t, heads, SCAN_HEADS)
    ncb, nhb = tc // CHUNK, heads // hb
    nc = t // CHUNK

    def body(u_ref, w_ref, p_ref, qd_ref, kd_ref, gl_ref, o_ref, s_ref, state):
        @pl.when(pl.program_id(1) == 0)
        def _():
            state[...] = jnp.zeros_like(state)

        for c in range(ncb):
            rs = slice(c * CHUNK, (c + 1) * CHUNK)
            s_in = state[...]
            s_ref[:, c] = s_in
            o, s_out = _gdn_step(_PLAIN, s_in, _heads(u_ref, rs, hb, dh), _heads(w_ref, rs, hb, dh), p_ref[:, rs, :],
                                 _heads(qd_ref, rs, hb, dh), _heads(kd_ref, rs, hb, dh), gl_ref[:, c])
            _put_heads(o_ref, rs, o, dh)
            state[...] = s_out

    tok = pl.BlockSpec((tc, hb * dh), lambda j, i: (i, j))
    pspec = pl.BlockSpec((hb, tc, CHUNK), lambda j, i: (j, i, 0))
    glspec = pl.BlockSpec((hb, ncb, 1, 1), lambda j, i: (j, i, 0, 0))
    return pl.pallas_call(
        body, name="gdn_scan_fwd", grid=(nhb, t // tc),
        in_specs=[tok, tok, pspec, tok, tok, glspec],
        out_specs=[tok, pl.BlockSpec((hb, ncb, dh, dh), lambda j, i: (j, i, 0, 0))],
        out_shape=[jax.ShapeDtypeStruct((t, heads * dh), F32), jax.ShapeDtypeStruct((heads, nc, dh, dh), F32)],
        scratch_shapes=[pltpu.VMEM((hb, dh, dh), F32)],
        compiler_params=_params("arbitrary", "arbitrary"),
    )(u, w, p, qd, kd, gl)


def _gdn_scan_bwd(u, w, p, qd, kd, gl, states, do, heads, dh):
    t = u.shape[0]
    tc, hb = _gdn_blocks(t, heads, SCAN_HEADS)
    ncb, nhb = tc // CHUNK, heads // hb
    nc, nt = t // CHUNK, t // tc

    def body(u_ref, w_ref, p_ref, qd_ref, kd_ref, gl_ref, s_ref, do_ref,
             du_ref, dw_ref, dp_ref, dqd_ref, dkd_ref, dgl_ref, dstate):
        @pl.when(pl.program_id(1) == 0)
        def _():
            dstate[...] = jnp.zeros_like(dstate)

        for c in reversed(range(ncb)):
            rs = slice(c * CHUNK, (c + 1) * CHUNK)
            _, vjp = jax.vjp(functools.partial(_gdn_step, _DIFF), s_ref[:, c], _heads(u_ref, rs, hb, dh),
                             _heads(w_ref, rs, hb, dh), p_ref[:, rs, :].astype(F32), _heads(qd_ref, rs, hb, dh),
                             _heads(kd_ref, rs, hb, dh), gl_ref[:, c])
            ds, du, dw, dp, dqd, dkd, dgl = vjp((_heads(do_ref, rs, hb, dh), dstate[...]))
            dstate[...] = ds
            _put_heads(du_ref, rs, du, dh)
            _put_heads(dw_ref, rs, dw, dh)
            _put_heads(dqd_ref, rs, dqd, dh)
            _put_heads(dkd_ref, rs, dkd, dh)
            dp_ref[:, rs, :] = dp
            dgl_ref[:, c] = dgl

    tok = pl.BlockSpec((tc, hb * dh), lambda j, i: (nt - 1 - i, j))
    pspec = pl.BlockSpec((hb, tc, CHUNK), lambda j, i: (j, nt - 1 - i, 0))
    glspec = pl.BlockSpec((hb, ncb, 1, 1), lambda j, i: (j, nt - 1 - i, 0, 0))
    sspec = pl.BlockSpec((hb, ncb, dh, dh), lambda j, i: (j, nt - 1 - i, 0, 0))
    tokf = jax.ShapeDtypeStruct((t, heads * dh), F32)
    return pl.pallas_call(
        body, name="gdn_scan_bwd", grid=(nhb, nt),
        in_specs=[tok, tok, pspec, tok, tok, glspec, sspec, tok],
        out_specs=[tok, tok, pspec, tok, tok, glspec],
        out_shape=[tokf, tokf, jax.ShapeDtypeStruct((heads, t, CHUNK), F32), tokf, tokf,
                   jax.ShapeDtypeStruct((heads, nc, 1, 1), F32)],
        scratch_shapes=[pltpu.VMEM((hb, dh, dh), F32)],
        compiler_params=_params("arbitrary", "arbitrary"),
    )(u, w, p, qd, kd, gl, states, do)


def _gdn_post(o, z, gain):
    return _rms(o, gain) * _silu(z)


def _gdn_post_fwd(o, pm, z_col, gain, heads, dh):
    t, wid = o.shape
    tm = _tile(t, 256, 16)
    assert z_col % wid == 0

    def body(o_ref, z_ref, g_ref, y_ref):
        for h in range(heads):
            ls = slice(h * dh, (h + 1) * dh)
            y_ref[:, ls] = _gdn_post(o_ref[:, ls], z_ref[:, ls], g_ref[...]).astype(y_ref.dtype)

    blk = pl.BlockSpec((tm, wid), lambda i: (i, 0))
    return pl.pallas_call(
        body, name="gdn_post_fwd", grid=(t // tm,),
        in_specs=[blk, pl.BlockSpec((tm, wid), lambda i: (i, z_col // wid)), pl.BlockSpec((1, dh), lambda i: (0, 0))],
        out_specs=blk, out_shape=jax.ShapeDtypeStruct((t, wid), MXU_DTYPE), compiler_params=_params("parallel"),
    )(o, pm, gain.reshape(1, dh))


def _gdn_post_bwd(o, pm, z_col, gain, dy, heads, dh):
    t, wid = o.shape
    tm = _tile(t, 256, 16)
    assert z_col % wid == 0

    def body(o_ref, z_ref, g_ref, dy_ref, do_ref, dz_ref, dg_ref):
        dg = jnp.zeros((1, dh), F32)
        for h in range(heads):
            ls = slice(h * dh, (h + 1) * dh)
            _, vjp = jax.vjp(_gdn_post, o_ref[:, ls], z_ref[:, ls], g_ref[...])
            do, dz, dg_h = vjp(dy_ref[:, ls])
            do_ref[:, ls] = do
            dz_ref[:, ls] = dz.astype(dz_ref.dtype)
            dg = dg + dg_h
        first = pl.program_id(0) == 0

        @pl.when(first)
        def _():
            dg_ref[...] = dg

        @pl.when(jnp.logical_not(first))
        def _():
            dg_ref[...] += dg

    blk = pl.BlockSpec((tm, wid), lambda i: (i, 0))
    vec = pl.BlockSpec((1, dh), lambda i: (0, 0))
    do, dz, dg = pl.pallas_call(
        body, name="gdn_post_bwd", grid=(t // tm,),
        in_specs=[blk, pl.BlockSpec((tm, wid), lambda i: (i, z_col // wid)), vec, blk], out_specs=[blk, blk, vec],
        out_shape=[jax.ShapeDtypeStruct((t, wid), F32), jax.ShapeDtypeStruct((t, wid), MXU_DTYPE),
                   jax.ShapeDtypeStruct((1, dh), F32)],
        compiler_params=_params("arbitrary"),
    )(o, pm, gain.reshape(1, dh), dy)
    return do, dz, dg.reshape(dh)


def _attn(ops, q, kv):
    d = q.shape[1]
    hd = d // XATTN_HEADS
    outs = []
    for h in range(XATTN_HEADS):
        qh, kh, vh = q[:, h * hd:(h + 1) * hd], kv[:, h * hd:(h + 1) * hd], kv[:, d + h * hd:d + (h + 1) * hd]
        s = ops.mm(qh, kh, "nt") * (hd ** -0.5)
        e = jnp.exp(s - lax.stop_gradient(jnp.max(s, axis=-1, keepdims=True)))
        outs.append(ops.mm(e / jnp.sum(e, axis=-1, keepdims=True), vh))
    return jnp.concatenate(outs, axis=1)


def _attn_fwd(q, kv):
    t, d = q.shape
    nm = kv.shape[0]
    tm = _tile(t, 512, 16)

    def body(q_ref, kv_ref, o_ref):
        o_ref[...] = _attn(_PLAIN, q_ref[...], kv_ref[...]).astype(o_ref.dtype)

    return pl.pallas_call(
        body, name="xattn_fwd", grid=(t // tm,),
        in_specs=[pl.BlockSpec((tm, d), lambda i: (i, 0)), pl.BlockSpec((nm, 2 * d), lambda i: (0, 0))],
        out_specs=pl.BlockSpec((tm, d), lambda i: (i, 0)),
        out_shape=jax.ShapeDtypeStruct((t, d), MXU_DTYPE), compiler_params=_params("parallel"),
    )(q, kv)


def _attn_bwd(q, kv, do):
    t, d = q.shape
    nm = kv.shape[0]
    tm = _tile(t, 256, 16)

    def body(q_ref, kv_ref, do_ref, dq_ref, dkv_ref):
        _, vjp = jax.vjp(functools.partial(_attn, _DIFF), q_ref[...].astype(F32), kv_ref[...].astype(F32))
        dq, dkv = vjp(do_ref[...].astype(F32))
        dq_ref[...] = dq.astype(dq_ref.dtype)
        first = pl.program_id(0) == 0

        @pl.when(first)
        def _():
            dkv_ref[...] = dkv

        @pl.when(jnp.logical_not(first))
        def _():
            dkv_ref[...] += dkv

    row = pl.BlockSpec((tm, d), lambda i: (i, 0))
    full = pl.BlockSpec((nm, 2 * d), lambda i: (0, 0))
    return pl.pallas_call(
        body, name="xattn_bwd", grid=(t // tm,), in_specs=[row, full, row], out_specs=[row, full],
        out_shape=[jax.ShapeDtypeStruct((t, d), MXU_DTYPE), jax.ShapeDtypeStruct((nm, 2 * d), F32)],
        compiler_params=_params("arbitrary"),
    )(q, kv, do)


def _adamw(name, w, g, m, v):
    shape = w.shape
    cols = shape[-1]
    rows = w.size // cols
    w2, g2, m2, v2 = (a.reshape(rows, cols) for a in (w, g, m, v))
    tr = _tile(rows, max(8, (1 << 18) // cols // 8 * 8), 8)

    def body(w_ref, g_ref, m_ref, v_ref, d_ref, nm_ref, nv_ref):
        gv = g_ref[...]
        nm = ADAM_B1 * m_ref[...] + (1.0 - ADAM_B1) * gv
        nv = ADAM_B2 * v_ref[...] + (1.0 - ADAM_B2) * jnp.square(gv)
        m_hat = nm / (1.0 - ADAM_B1 ** ADAM_STEP)
        v_hat = nv / (1.0 - ADAM_B2 ** ADAM_STEP)
        d_ref[...] = -ADAM_LR * (m_hat / (jnp.sqrt(v_hat) + ADAM_EPS) + ADAM_WD * w_ref[...])
        nm_ref[...] = nm
        nv_ref[...] = nv

    blk = pl.BlockSpec((tr, cols), lambda i: (i, 0))
    out = jax.ShapeDtypeStruct((rows, cols), F32)
    d, nm, nv = pl.pallas_call(
        body, name=name, grid=(rows // tr,), in_specs=[blk] * 4, out_specs=[blk] * 3, out_shape=[out] * 3,
        compiler_params=_params("parallel"),
    )(w2, g2, m2, v2)
    return d.reshape(shape), nm.reshape(shape), nv.reshape(shape)


def _layer_fwd(x, mem, p, heads, dh, carry, late):
    wid = heads * dh
    sc = x.shape[1] - wid
    p, s, landed = dict(p), {"x0": x}, {}

    def arrived(name, brought):
        landed[name] = brought
        if name in late:
            p.update(late[name](brought))

    def mm(name, *args, **kwargs):
        if name not in carry:
            return _matmul(name, *args, **kwargs)
        out, brought = _matmul(name, *args, comm=carry[name], **kwargs)
        arrived(name, brought)
        return out

    s["h1"] = _rms_fwd("rms_mix", x, p["mix_norm"])
    s["pm"] = pm = mm("mm_mix_in", s["h1"], p["wmain"], "nn", F32)
    s["pba"] = mm("mm_mix_ba", s["h1"], p["wba"], "nn", F32)
    s["qkv"] = _conv_fwd("conv_gdn", pm, 0, p["gdn_conv"], 0, 3 * wid, F32)
    s["prep"], s["tinv"], brought = _gdn_prep_fwd(s["qkv"], s["pba"], p["a_log"], p["dt_bias"], heads, dh,
                                                  comm=carry.get("gdn_prep_fwd"))
    if brought:
        arrived("gdn_prep_fwd", brought)
    s["o"], s["states"] = _gdn_scan_fwd(*s["prep"], heads, dh)
    y_gdn = _gdn_post_fwd(s["o"], pm, 3 * wid, p["gdn_out_norm"], heads, dh)
    y_sc = _conv_fwd("conv_sc", pm, 4 * wid + sc, p["sc_conv"], 0, sc, MXU_DTYPE, xb=pm, xb_col=4 * wid + 2 * sc,
                     gate=pm, gate_col=4 * wid)
    s["ycat"] = jnp.concatenate([y_gdn, y_sc], axis=1)
    s["x1"] = x1 = mm("mm_mix_out", s["ycat"], p["wout"], "nn", F32, add=x)
    s["h2"] = _rms_fwd("rms_xattn", x1, p["xattn_norm"])
    s["q"] = mm("mm_xq", s["h2"], p["wq"], "nn", MXU_DTYPE)
    s["memn"] = _rms_fwd("rms_mem", mem, p["mem_norm"])
    s["kv"] = mm("mm_xkv", s["memn"], p["wkv"], "nn", MXU_DTYPE)
    s["ao"] = _attn_fwd(s["q"], s["kv"])
    s["x2"] = x2 = mm("mm_xo", s["ao"], p["wo"], "nn", F32, add=x1)
    s["h3"] = _rms_fwd("rms_ffn", x2, p["ffn_norm"])
    s["upre"] = mm("mm_ffn_up", s["h3"], p["wup"], "nn", MXU_DTYPE)
    s["uc"] = _conv_fwd("conv_ffn", s["upre"], 0, p["ffn_conv"], 0, s["upre"].shape[1], MXU_DTYPE)
    s["act"] = _swiglu_fwd(s["uc"])
    return mm("mm_ffn_down", s["act"], p["wdown"], "nn", F32, add=x2), s, landed, p


def _layer_bwd(dx3, dx3b, mem, s, p, heads, dh, reduce):
    wid = heads * dh
    sc = dx3.shape[1] - wid
    pm = s["pm"]
    g = {}

    mm = reduce.carried if reduce is not None else _matmul
    da = mm("mm_ffn_down_dx", dx3b, p["wdown"], "nt", MXU_DTYPE)
    g["wdown"] = mm("mm_ffn_down_dw", s["act"], dx3b, "tn", WIRE_DTYPE)
    du = _swiglu_bwd(s["uc"], da)
    dupre, _, _, g["ffn_conv"] = _conv_bwd("conv_ffn_bwd", s["upre"], 0, p["ffn_conv"], 0, du, 0, du.shape[1],
                                           MXU_DTYPE)
    dh3 = mm("mm_ffn_up_dx", dupre, p["wup"], "nt", F32)
    g["wup"] = mm("mm_ffn_up_dw", s["h3"], dupre, "tn", WIRE_DTYPE)
    dx2, dx2b, g["ffn_norm"] = _rms_bwd("rms_ffn_bwd", s["x2"], p["ffn_norm"], dh3, dx3)
    dao = mm("mm_xo_dx", dx2b, p["wo"], "nt", MXU_DTYPE)
    g["wo"] = mm("mm_xo_dw", s["ao"], dx2b, "tn", WIRE_DTYPE)
    dq, dkv = _attn_bwd(s["q"], s["kv"], dao)
    dh2 = mm("mm_xq_dx", dq, p["wq"], "nt", F32)
    g["wq"] = mm("mm_xq_dw", s["h2"], dq, "tn", WIRE_DTYPE)
    dmemn = mm("mm_xkv_dx", dkv, p["wkv"], "nt", F32)
    g["wkv"] = mm("mm_xkv_dw", s["memn"], dkv, "tn", WIRE_DTYPE)
    _, _, g["mem_norm"] = _rms_bwd("rms_mem_bwd", mem, p["mem_norm"], dmemn)
    dx1, dx1b, g["xattn_norm"] = _rms_bwd("rms_xattn_bwd", s["x1"], p["xattn_norm"], dh2, dx2)
    dycat = mm("mm_mix_out_dx", dx1b, p["wout"], "nt", F32)
    g["wout"] = mm("mm_mix_out_dw", s["ycat"], dx1b, "tn", WIRE_DTYPE)
    d_c, d_h, d_b, g["sc_conv"] = _conv_bwd("conv_sc_bwd", pm, 4 * wid + sc, p["sc_conv"], 0, dycat, wid, sc,
                                             MXU_DTYPE, xb=pm, xb_col=4 * wid + 2 * sc, gate=pm, gate_col=4 * wid)
    do, dz, g["gdn_out_norm"] = _gdn_post_bwd(s["o"], pm, 3 * wid, p["gdn_out_norm"], dycat, heads, dh)
    dprep = _gdn_scan_bwd(*s["prep"], s["states"], do, heads, dh)
    dqc, dkc, dvc, dpba, dal, ddt = _gdn_prep_bwd(s["qkv"], s["pba"], p["a_log"], p["dt_bias"], s["tinv"], *dprep,
                                                  heads, dh)
    g["a_log"], g["dt_bias"] = jnp.sum(dal, axis=0), jnp.sum(ddt, axis=0)
    dqkv, _, _, g["gdn_conv"] = _conv_bwd("conv_gdn_bwd", pm, 0, p["gdn_conv"], 0,
                                          jnp.concatenate([dqc, dkc, dvc], axis=1), 0, 3 * wid, MXU_DTYPE)
    dpm = jnp.concatenate([dqkv, dz, d_b, d_c, d_h], axis=1)
    dpba = dpba.astype(MXU_DTYPE)
    dh1 = mm("mm_mix_in_dx", dpm, p["wmain"], "nt", F32)
    dh1 = mm("mm_mix_ba_dx", dpba, p["wba"], "nt", F32, add=dh1)
    g["wmain"] = mm("mm_mix_in_dw", s["h1"], dpm, "tn", WIRE_DTYPE)
    g["wba"] = mm("mm_mix_ba_dw", s["h1"], dpba, "tn", WIRE_DTYPE)
    dx0, dx0b, g["mix_norm"] = _rms_bwd("rms_mix_bwd", s["x0"], p["mix_norm"], dh1, dx1)
    return dx0, dx0b, g


def _input_projection(win, heads, dh):
    wid = heads * dh
    return {"wmain": jnp.concatenate([win[:, :4 * wid], win[:, 4 * wid + 2 * heads:]], axis=1),
            "wba": jnp.pad(win[:, 4 * wid:4 * wid + 2 * heads], ((0, 0), (0, LANES - 2 * heads)))}


def _square_projections(wout, wq, wk, wv, wo, wdown):
    return {"wout": wout, "wq": wq, "wkv": jnp.concatenate([wk, wv], axis=1), "wo": wo, "wdown": wdown}


_ANY = pl.BlockSpec(memory_space=pl.ANY)
_VMEM = pl.BlockSpec(memory_space=pltpu.VMEM)


def _mesh_pos():
    return lax.axis_index("x"), lax.axis_index("y"), lax.axis_index("c")


def _other_chips(x, y):
    return [(1 - x, y), (x, 1 - y), (1 - x, 1 - y)]


def _push(src, dst, sems, k, to):
    return pltpu.make_async_remote_copy(src_ref=src, dst_ref=dst, send_sem=sems[0].at[k], recv_sem=sems[1].at[k],
                                        device_id=to, device_id_type=MESH)


def _sem_pairs(n):
    return [pltpu.SemaphoreType.DMA((n,)), pltpu.SemaphoreType.DMA((n,))]


class _Comm:
    def __init__(self, ins, outs, n_sems, start, finish, aliases=None):
        self.ins, self.outs, self.n_sems, self.start, self.finish = list(ins), list(outs), n_sems, start, finish
        self.aliases = aliases or {}


def _run_comm(name, comm):
    n_in, n_out = len(comm.ins), len(comm.outs)

    def body(*refs):
        parts = (refs[:n_in], refs[n_in:n_in + n_out], refs[n_in + n_out:])
        comm.start(*parts)
        comm.finish(*parts)

    return pl.pallas_call(
        body, name=name, in_specs=[_ANY] * n_in, out_specs=[_ANY] * n_out, out_shape=comm.outs,
        scratch_shapes=_sem_pairs(comm.n_sems), input_output_aliases=comm.aliases,
    )(*comm.ins)


def _allgather_comm(srcs):
    n = len(srcs)

    def first(src, out, sems):
        x, y, c = _mesh_pos()
        own, sends = [], []
        for t in range(n):
            half = src[t].shape[0] // 2
            mine = pl.ds(c * half, half)
            own.append(_push(src[t], out[t].at[2 * x + y], sems, 7 * t + 6, (x, y, 1 - c)))
            sends += [_push(src[t].at[mine], out[t].at[2 * x + y, mine], sems, 7 * t + k, (cx, cy, c))
                      for k, (cx, cy) in enumerate(_other_chips(x, y))]
        return own, sends

    def start(src, out, sems):
        own, sends = first(src, out, sems)
        for cp in own + sends:
            cp.start()

    def finish(src, out, sems):
        x, y, c = _mesh_pos()
        sibling = (x, y, 1 - c)
        own, sends = first(src, out, sems)
        fwds, relayed = [], []
        for t in range(n):
            half = src[t].shape[0] // 2
            for k, (cx, cy) in enumerate(_other_chips(x, y)):
                here = out[t].at[2 * cx + cy, pl.ds(c * half, half)]
                there = out[t].at[2 * cx + cy, pl.ds((1 - c) * half, half)]
                _push(here, here, sems, 7 * t + k, sibling).wait_recv()
                fwds.append(_push(here, here, sems, 7 * t + 3 + k, sibling))
                fwds[-1].start()
                relayed.append(_push(there, there, sems, 7 * t + 3 + k, sibling))
        for cp in relayed + own:
            cp.wait_recv()
        for cp in own + sends + fwds:
            cp.wait_send()

    return _Comm(srcs, [jax.ShapeDtypeStruct((N_CHIPS,) + s.shape, s.dtype) for s in srcs], 7 * n, start, finish)


def _start_wait(build):
    def start(src, out, sems):
        for cp in build(src, out, sems):
            cp.start()

    def finish(src, out, sems):
        for cp in build(src, out, sems):
            cp.wait()

    return start, finish


def _sibling_exchange_comm(bufs):
    def build(src, out, sems):
        x, y, c = _mesh_pos()
        return [_push(src[t].at[1 - c], out[t], sems, t, (x, y, 1 - c)) for t in range(len(bufs))]

    start, finish = _start_wait(build)
    return _Comm(bufs, [jax.ShapeDtypeStruct(b.shape[1:], b.dtype) for b in bufs], len(bufs), start, finish)


def _chip_exchange_comm(bufs):
    def build(src, out, sems):
        x, y, c = _mesh_pos()
        return [_push(src[t].at[2 * cx + cy], out[t].at[k], sems, 3 * t + k, (cx, cy, c))
                for t in range(len(bufs)) for k, (cx, cy) in enumerate(_other_chips(x, y))]

    start, finish = _start_wait(build)
    return _Comm(bufs, [jax.ShapeDtypeStruct((3,) + b.shape[1:], b.dtype) for b in bufs], 3 * len(bufs), start, finish)


def _sibling_share_comm(bufs):
    def build(src, out, sems):
        x, y, c = _mesh_pos()
        return [_push(src[t].at[c], out[t].at[c], sems, t, (x, y, 1 - c)) for t in range(len(bufs))]

    start, finish = _start_wait(build)
    return _Comm(bufs, [jax.ShapeDtypeStruct(b.shape, b.dtype) for b in bufs], len(bufs), start, finish,
                 aliases={t: t for t in range(len(bufs))})


def _allreduce_small(v):
    r, lanes = v.shape

    def body(v_ref, sum_ref, gath, send_sems, recv_sems):
        x, y, c = _mesh_pos()
        me = 4 * x + 2 * y + c
        gath[me] = v_ref[...]
        copies = []
        for rel in range(1, N_DEV):
            peer = tuple(1 - p if (rel >> b) & 1 else p for p, b in ((x, 2), (y, 1), (c, 0)))
            copies.append(pltpu.make_async_remote_copy(
                src_ref=v_ref, dst_ref=gath.at[me], send_sem=send_sems.at[rel - 1], recv_sem=recv_sems.at[rel - 1],
                device_id=peer, device_id_type=MESH))
        for cp in copies:
            cp.start()
        for cp in copies:
            cp.wait()
        total = gath[0]
        for k in range(1, N_DEV):
            total = total + gath[k]
        sum_ref[...] = total

    return pl.pallas_call(
        body, name="allreduce_small", in_specs=[_VMEM], out_specs=_VMEM,
        out_shape=jax.ShapeDtypeStruct((r, lanes), F32),
        scratch_shapes=[pltpu.VMEM((N_DEV, r, lanes), F32)] + _sem_pairs(N_DEV - 1),
        compiler_params=pltpu.CompilerParams(vmem_limit_bytes=VMEM_LIMIT),
    )(v)


def _sum_tile(rows, width):
    return _tile(rows, max(16, (1 << 19) // width // 16 * 16), 16)


def _sum_sibling(x, recv, core):
    _, n, w = x.shape
    tr = _sum_tile(n, w)

    def body(idx_ref, x_ref, r_ref, o_ref):
        o_ref[...] = (x_ref[...].astype(F32) + r_ref[...].astype(F32)).astype(o_ref.dtype)

    row = pl.BlockSpec((tr, w), lambda i, idx: (i, 0))
    return pl.pallas_call(
        body, name="rs_sum_sibling",
        grid_spec=pltpu.PrefetchScalarGridSpec(
            num_scalar_prefetch=1, grid=(n // tr,),
            in_specs=[pl.BlockSpec((None, tr, w), lambda i, idx: (idx[0], i, 0)), row], out_specs=row),
        out_shape=jax.ShapeDtypeStruct((n, w), x.dtype), compiler_params=_params("parallel"),
    )(core.reshape(1), x, recv)


def _sum_chips(s, recv, chip, core):
    _, m, w = s.shape
    tr = _sum_tile(m, w)

    def body(idx_ref, s_ref, r0_ref, r1_ref, r2_ref, o_ref):
        o_ref[...] = ((s_ref[...].astype(F32) + r0_ref[...].astype(F32)) + r1_ref[...].astype(F32)) \
            + r2_ref[...].astype(F32)

    def got(k):
        return pl.BlockSpec((None, tr, w), lambda i, idx: (k, i, 0))

    return pl.pallas_call(
        body, name="rs_sum_chips",
        grid_spec=pltpu.PrefetchScalarGridSpec(
            num_scalar_prefetch=1, grid=(m // tr,),
            in_specs=[pl.BlockSpec((None, tr, w), lambda i, idx: (idx[0], i, 0)), got(0), got(1), got(2)],
            out_specs=pl.BlockSpec((None, tr, w), lambda i, idx: (idx[1], i, 0))),
        out_shape=jax.ShapeDtypeStruct((2, m, w), F32), compiler_params=_params("parallel"),
    )(jnp.stack([chip, core]), s, recv, recv, recv)


_ROWS = ("w_mix_out", "w_xq", "w_xk", "w_xv", "w_xo", "w_ffn_down")
_CONVS = ("gdn_conv", "sc_conv", "ffn_conv")
_REPLICATED = ("mix_norm", "gdn_a_log", "gdn_dt_bias", "gdn_out_norm", "xattn_norm", "mem_norm", "ffn_norm",
               "final_norm")
_WEIGHTS = ("mix_norm", "w_mix_in", "gdn_conv", "gdn_a_log", "gdn_dt_bias", "gdn_out_norm", "sc_conv", "w_mix_out",
            "xattn_norm", "mem_norm", "w_xq", "w_xk", "w_xv", "w_xo", "ffn_norm", "w_ffn_up", "ffn_conv",
            "w_ffn_down", "final_norm")


def _pad_rows(flat, groups):
    unit = groups * 16 * LANES
    p = flat.shape[-1]
    pad = -p % unit
    if pad:
        flat = jnp.pad(flat, [(0, 0)] * (flat.ndim - 1) + [(0, pad)])
    return flat.reshape(flat.shape[:-1] + (groups, (p + pad) // (groups * LANES), LANES))


def _split_flat(flat, shapes):
    out, off = [], 0
    for shp in shapes:
        size = 1
        for n in shp:
            size *= n
        out.append(flat[..., off:off + size].reshape(flat.shape[:-1] + tuple(shp)))
        off += size
    return out


def _by_chip(g, axis):
    rows, cols = g.shape
    if axis == 0:
        return g.reshape(N_CHIPS, rows // N_CHIPS, cols)
    return g.reshape(rows, N_CHIPS, cols // N_CHIPS).transpose(1, 0, 2)


def _halves_by_chip(g):
    _, rows, w = g.shape
    return g.astype(WIRE_DTYPE).reshape(N_CHIPS, 2, rows // 2, w).transpose(1, 0, 2, 3)


class _ReduceScatter:
    STAGES = ("mm_ffn_down_dx", "mm_ffn_up_dx", "mm_ffn_up_dw", "mm_mix_in_dx")

    def __init__(self, bufs, chip, core):
        self.bufs, self.chip, self.core = list(bufs), chip, core
        self.sums = self.from_chips = self.reduced = self.result = None

    def comm(self, stage):
        if stage == self.STAGES[0]:
            return _sibling_exchange_comm(self.bufs)
        if stage == self.STAGES[1]:
            return _chip_exchange_comm(self.sums[-1:])
        if stage == self.STAGES[2]:
            return _chip_exchange_comm(self.sums[:-1])
        return _sibling_share_comm(self.reduced)

    def landed(self, stage, outs):
        if stage == self.STAGES[0]:
            self.sums = [_sum_sibling(b.reshape(2, -1, b.shape[-1]), r.reshape(-1, r.shape[-1]), self.core)
                         .reshape(r.shape) for b, r in zip(self.bufs, outs)]
        elif stage == self.STAGES[1]:
            self.from_chips = list(outs)
        elif stage == self.STAGES[2]:
            self.reduced = [_sum_chips(s, r, self.chip, self.core)
                            for s, r in zip(self.sums, list(outs) + self.from_chips)]
        else:
            self.result = list(outs)

    def carried(self, name, *args, **kwargs):
        if name not in self.STAGES:
            return _matmul(name, *args, **kwargs)
        out, outs = _matmul(name, *args, comm=self.comm(name), **kwargs)
        self.landed(name, outs)
        return out

    def run_alone(self):
        for stage, name in zip(self.STAGES, ("rs_sibling_exchange", "rs_chip_exchange_rows", "rs_chip_exchange_cols",
                                             "rs_sibling_share")):
            self.landed(stage, _run_comm(name, self.comm(stage)))


def kernel(x, mem, mix_norm, w_mix_in, gdn_conv, gdn_a_log, gdn_dt_bias, gdn_out_norm, sc_conv, w_mix_out, xattn_norm, mem_norm, w_xq, w_xk, w_xv, w_xo, ffn_norm, w_ffn_up, ffn_conv, w_ffn_down, final_norm, loss_target, m_mix_norm, m_w_mix_in, m_gdn_conv, m_gdn_a_log, m_gdn_dt_bias, m_gdn_out_norm, m_sc_conv, m_w_mix_out, m_xattn_norm, m_mem_norm, m_w_xq, m_w_xk, m_w_xv, m_w_xo, m_ffn_norm, m_w_ffn_up, m_ffn_conv, m_w_ffn_down, m_final_norm, v_mix_norm, v_w_mix_in, v_gdn_conv, v_gdn_a_log, v_gdn_dt_bias, v_gdn_out_norm, v_sc_conv, v_w_mix_out, v_xattn_norm, v_mem_norm, v_w_xq, v_w_xk, v_w_xv, v_w_xo, v_ffn_norm, v_w_ffn_up, v_ffn_conv, v_w_ffn_down, v_final_norm):
    w = dict(zip(_WEIGHTS, (mix_norm, w_mix_in, gdn_conv, gdn_a_log, gdn_dt_bias, gdn_out_norm, sc_conv, w_mix_out,
                            xattn_norm, mem_norm, w_xq, w_xk, w_xv, w_xo, ffn_norm, w_ffn_up, ffn_conv, w_ffn_down,
                            final_norm)))
    m = dict(zip(_WEIGHTS, (m_mix_norm, m_w_mix_in, m_gdn_conv, m_gdn_a_log, m_gdn_dt_bias, m_gdn_out_norm, m_sc_conv,
                            m_w_mix_out, m_xattn_norm, m_mem_norm, m_w_xq, m_w_xk, m_w_xv, m_w_xo, m_ffn_norm,
                            m_w_ffn_up, m_ffn_conv, m_w_ffn_down, m_final_norm)))
    v = dict(zip(_WEIGHTS, (v_mix_norm, v_w_mix_in, v_gdn_conv, v_gdn_a_log, v_gdn_dt_bias, v_gdn_out_norm, v_sc_conv,
                            v_w_mix_out, v_xattn_norm, v_mem_norm, v_w_xq, v_w_xk, v_w_xv, v_w_xo, v_ffn_norm,
                            v_w_ffn_up, v_ffn_conv, v_w_ffn_down, v_final_norm)))
    core = lax.axis_index("c")
    chip = 2 * lax.axis_index("x") + lax.axis_index("y")
    depth, heads = gdn_a_log.shape
    dh = gdn_out_norm.shape[1]
    d, wid = x.shape[2], heads * dh

    row_sizes = [w[n].shape[1] for n in _ROWS]
    row_offs = [sum(row_sizes[:k]) for k in range(len(_ROWS))]
    src_in, src_up = w_mix_in.astype(WIRE_DTYPE), w_ffn_up.astype(WIRE_DTYPE)
    src_rows = jnp.concatenate([w[n] for n in _ROWS], axis=1).astype(WIRE_DTYPE)
    src_convs = _pad_rows(jnp.concatenate([w[n].reshape(-1) for n in _CONVS]), 2)
    g_in, g_convs = _run_comm("allgather_first", _allgather_comm([src_in[0], src_convs]))
    conv_full = {n: jnp.moveaxis(part, 0, 2).reshape(depth, part.shape[2], -1)
                 for n, part in zip(_CONVS, _split_flat(g_convs.reshape(N_CHIPS, -1), [w[n].shape for n in _CONVS]))}
    side_by_side = lambda g: jnp.concatenate([g[j] for j in range(N_CHIPS)], axis=1)

    def from_rows(brought):
        g_rows, = brought
        return _square_projections(*[jnp.concatenate([g_rows[j, off:off + size] for j in range(N_CHIPS)], axis=0)
                                     for off, size in zip(row_offs, row_sizes)])

    xl, mem_l = x[0], mem[0]
    layers, saved, g_rows = [], [], None
    for l in range(depth):
        p = {"mix_norm": mix_norm[l], "xattn_norm": xattn_norm[l], "mem_norm": mem_norm[l], "ffn_norm": ffn_norm[l],
             "gdn_out_norm": gdn_out_norm[l], "a_log": gdn_a_log[l].reshape(heads, 1, 1),
             "dt_bias": gdn_dt_bias[l].reshape(heads, 1, 1), "gdn_conv": conv_full["gdn_conv"][l],
             "sc_conv": conv_full["sc_conv"][l], "ffn_conv": conv_full["ffn_conv"][l]}
        p.update(_input_projection(side_by_side(g_in), heads, dh))
        carry = {"gdn_prep_fwd": _allgather_comm([src_up[l]])}
        late = {"gdn_prep_fwd": lambda brought: {"wup": brought[0]}}
        if l == 0:
            carry["mm_mix_in"], late["mm_mix_in"] = _allgather_comm([src_rows[0]]), from_rows
        else:
            p.update(from_rows(g_rows))
        if l + 1 < depth:
            carry["mm_ffn_up"] = _allgather_comm([src_rows[l + 1]])
            carry["mm_ffn_down"] = _allgather_comm([src_in[l + 1]])
        xl, s, landed, p = _layer_fwd(xl, mem_l, p, heads, dh, carry, late)
        layers.append(p)
        saved.append(s)
        if l + 1 < depth:
            g_rows, (g_in,) = landed["mm_ffn_up"], landed["mm_ffn_down"]
    loss_row, dx, dxb, g_final = _final_loss(xl, final_norm, loss_target[0])

    def by_chip(g):
        g_win = jnp.concatenate([g["wmain"][:, :4 * wid], g["wba"][:, :2 * heads], g["wmain"][:, 4 * wid:]], axis=1)
        parts = (g["wout"], g["wq"], g["wkv"][:, :d], g["wkv"][:, d:], g["wo"], g["wdown"])
        return [_halves_by_chip(_by_chip(g_win, 1)), _halves_by_chip(_by_chip(g["wup"], 1)),
                _halves_by_chip(jnp.concatenate([_by_chip(p, 0) for p in parts], axis=1))]

    per_layer, shards, reduce = [None] * depth, [None] * depth, None
    for l in reversed(range(depth)):
        dx, dxb, per_layer[l] = _layer_bwd(dx, dxb, mem_l, saved[l], layers[l], heads, dh, reduce)
        if reduce is not None:
            shards[l + 1] = reduce.result
        reduce = _ReduceScatter(by_chip(per_layer[l]), chip, core)
    reduce.run_alone()
    shards[0] = reduce.result
    by_layer = [[s[t].reshape(-1, s[t].shape[-1]) for s in shards] for t in range(3)]
    grad = {"w_mix_in": jnp.stack(by_layer[0]), "w_ffn_up": jnp.stack(by_layer[1])}
    for n, off, size in zip(_ROWS, row_offs, row_sizes):
        grad[n] = jnp.stack([r[off:off + size] for r in by_layer[2]])

    stack = lambda k: jnp.stack([g[k] for g in per_layer])
    small_g = {"mix_norm": stack("mix_norm"), "gdn_a_log": stack("a_log").reshape(depth, heads),
               "gdn_dt_bias": stack("dt_bias").reshape(depth, heads), "gdn_out_norm": stack("gdn_out_norm"),
               "xattn_norm": stack("xattn_norm"), "mem_norm": stack("mem_norm"), "ffn_norm": stack("ffn_norm"),
               "final_norm": g_final, "gdn_conv": stack("gdn_conv"), "sc_conv": stack("sc_conv"),
               "ffn_conv": stack("ffn_conv")}
    names = _REPLICATED + _CONVS
    small = jnp.concatenate([small_g[n].reshape(-1) for n in names] + [loss_row[0, :1]])
    small_sum = _allreduce_small(_pad_rows(small, 1)[0]).reshape(-1)
    parts = _split_flat(small_sum, [small_g[n].shape for n in names] + [(1,)])
    g_rep = dict(zip(_REPLICATED, parts[:len(_REPLICATED)]))
    for n, part in zip(_CONVS, parts[len(_REPLICATED):-1]):
        grad[n] = lax.dynamic_slice_in_dim(part, chip * w[n].shape[2], w[n].shape[2], axis=2)
    loss = parts[-1][0]

    delta, new_m, new_v = {}, {}, {}
    for n in ("w_mix_in", "w_ffn_up") + _ROWS + _CONVS:
        delta[n], new_m[n], new_v[n] = _adamw("adamw_" + n, w[n], grad[n], m[n], v[n])
    pack_rep = lambda t: _pad_rows(jnp.concatenate([t[n].reshape(-1) for n in _REPLICATED]), 1)[0]
    outs = _adamw("adamw_replicated", pack_rep(w), pack_rep(g_rep), pack_rep(m), pack_rep(v))
    shapes = [w[n].shape for n in _REPLICATED]
    for tgt, packed_out in zip((delta, new_m, new_v), outs):
        tgt.update(zip(_REPLICATED, _split_flat(packed_out.reshape(-1), shapes)))
    grad.update(g_rep)
    return (loss, dx[None], *[grad[n] for n in _WEIGHTS], *[delta[n] for n in _WEIGHTS],
            *[new_m[n] for n in _WEIGHTS], *[new_v[n] for n in _WEIGHTS])
```

```python
import functools

import jax
import jax.numpy as jnp
from jax import lax
from jax.experimental import pallas as pl
from jax.experimental.pallas import tpu as pltpu

F32 = jnp.float32
MXU_DTYPE = jnp.bfloat16
WIRE_DTYPE = jnp.bfloat16
SOLVE_PRECISION = lax.Precision.HIGH
EPS = 1e-6
CHUNK = 64
XATTN_HEADS = 4
LANES = 128
HALO = 16
EW_ROWS, EW_COLS = 256, 2816
VMEM_LIMIT = 52 * 1024 * 1024
ADAM_LR, ADAM_B1, ADAM_B2, ADAM_EPS, ADAM_WD, ADAM_STEP = 0.001, 0.9, 0.999, 1e-08, 0.01, 10
MESH = pl.DeviceIdType.MESH
N_CHIPS = 4
N_DEV = 8

_DIMS = {
    "nn": (((1,), (0,)), ((), ())),
    "nt": (((1,), (1,)), ((), ())),
    "tn": (((0,), (0,)), ((), ())),
}


def _tile(n, pref, align=LANES):
    if n <= pref:
        return n
    t = (pref // align) * align
    while t >= align:
        if n % t == 0:
            return t
        t -= align
    return n


def _params(*sem):
    return pltpu.CompilerParams(dimension_semantics=sem, vmem_limit_bytes=VMEM_LIMIT)


def _dot(a, b, form, hi=False):
    (ca, cb), _ = _DIMS[form]
    dims = (((ca[0] + 1,), (cb[0] + 1,)), ((0,), (0,))) if a.ndim == 3 else _DIMS[form]
    if hi:
        return lax.dot_general(a.astype(F32), b.astype(F32), dims, precision=SOLVE_PRECISION,
                               preferred_element_type=F32)
    return lax.dot_general(a.astype(MXU_DTYPE), b.astype(MXU_DTYPE), dims, preferred_element_type=F32)


@functools.partial(jax.custom_vjp, nondiff_argnums=(2, 3))
def _dot_d(a, b, form, hi):
    return _dot(a, b, form, hi)


def _dot_d_fwd(a, b, form, hi):
    return _dot(a, b, form, hi), (a, b)


def _dot_d_bwd(form, hi, res, g):
    a, b = res
    if form == "nn":
        da, db = _dot_d(g, b, "nt", hi), _dot_d(a, g, "tn", hi)
    elif form == "nt":
        da, db = _dot_d(g, b, "nn", hi), _dot_d(g, a, "tn", hi)
    else:
        da, db = _dot_d(b, g, "nt", hi), _dot_d(a, g, "nn", hi)
    return da.astype(a.dtype), db.astype(b.dtype)


_dot_d.defvjp(_dot_d_fwd, _dot_d_bwd)


def _tri_inv_impl(a, mmh):
    c = a.shape[-1]
    r = lax.broadcasted_iota(jnp.int32, (c, c), 0)
    s = lax.broadcasted_iota(jnp.int32, (c, c), 1)
    eye = (r == s).astype(F32)
    diag_blk = (r // 16) == (s // 16)
    d = jnp.where(diag_blk, a, 0.0)
    low = a - d
    d2 = mmh(d, d)
    d4 = mmh(d2, d2)
    d8 = mmh(d4, d4)
    td = mmh(mmh(mmh(eye - d, eye + d2), eye + d4), eye + d8)
    n = mmh(td, low)
    acc = eye - n
    p = n
    pw = 1
    while 2 * pw < c // 16:
        p = mmh(p, p)
        pw *= 2
        acc = mmh(acc, eye + p)
    return mmh(acc, td)


def _mmh_plain(a, b):
    return _dot(a, b, "nn", True)


@jax.custom_vjp
def _tri_inv_known(a, t):
    return t


def _tri_inv_known_fwd(a, t):
    return t, t


def _tri_inv_known_bwd(t, g):
    return -_dot(_dot(t, g, "tn", True), t, "nt", True), jnp.zeros_like(t)


_tri_inv_known.defvjp(_tri_inv_known_fwd, _tri_inv_known_bwd)


class _Ops:
    def __init__(self, diff, tinv=None):
        self.diff, self.tinv = diff, tinv

    def mm(self, a, b, form="nn"):
        return _dot_d(a, b, form, False) if self.diff else _dot(a, b, form, False)

    def mmh(self, a, b, form="nn"):
        return _dot_d(a, b, form, True) if self.diff else _dot(a, b, form, True)

    def tri_inv(self, a):
        return _tri_inv_known(a, self.tinv) if self.diff else _tri_inv_impl(a, _mmh_plain)


_PLAIN = _Ops(False)
_DIFF = _Ops(True)


def _sigmoid(x):
    return 1.0 / (1.0 + jnp.exp(-x))


def _silu(x):
    return x * _sigmoid(x)


def _softplus(x):
    return jnp.maximum(x, 0.0) + jnp.log(1.0 + jnp.exp(-jnp.abs(x)))


def _rms(x, g):
    return x * lax.rsqrt(jnp.mean(x * x, axis=-1, keepdims=True) + EPS) * g


def _matmul_tiles(m, n, k, form):
    if k <= 2048:
        return _tile(m, 1024), _tile(n, 1408), k
    if k <= 8192:
        return _tile(m, 1024 if form == "nn" else 512), _tile(n, 512), k
    return _tile(m, 1024), _tile(n, 1024), _tile(k, 2816)


def _matmul(name, a, b, form, out_dtype, add=None, comm=None):
    b_shape = b.shape if b.ndim == 2 else (b.shape[1], N_CHIPS * b.shape[2])
    if form == "nn":
        (m, k), (k2, n) = a.shape, b_shape
    elif form == "nt":
        (m, k), (n, k2) = a.shape, b_shape
    else:
        (k, m), (k2, n) = a.shape, b_shape
    assert k == k2, (name, a.shape, b.shape, form)
    tm, tn, tk = _matmul_tiles(m, n, k, form)
    if b.ndim == 3:
        assert form != "tn", name
        tn, tk = (_tile(b.shape[2], tn), tk) if form == "nn" else (tn, _tile(b.shape[2], tk))
    nk = k // tk
    out_bytes = tm * tn * (jnp.dtype(out_dtype).itemsize + (4 if add is not None else 0))
    vmem = 2 * (tm * tk * a.dtype.itemsize + tk * tn * b.dtype.itemsize + out_bytes) + (tm * tn * 4 if nk > 1 else 0)
    assert vmem <= VMEM_LIMIT, (name, tm, tn, tk, vmem)
    if form == "nn":
        a_spec = pl.BlockSpec((tm, tk), lambda i, j, kk: (i, kk))
        b_spec = pl.BlockSpec((tk, tn), lambda i, j, kk: (kk, j))
    elif form == "nt":
        a_spec = pl.BlockSpec((tm, tk), lambda i, j, kk: (i, kk))
        b_spec = pl.BlockSpec((tn, tk), lambda i, j, kk: (j, kk))
    else:
        a_spec = pl.BlockSpec((tk, tm), lambda i, j, kk: (kk, i))
        b_spec = pl.BlockSpec((tk, tn), lambda i, j, kk: (kk, j))
    if b.ndim == 3:
        per = b.shape[2] // (tn if form == "nn" else tk)
        if form == "nn":
            b_spec = pl.BlockSpec((None, tk, tn), lambda i, j, kk: (j // per, kk, j % per))
        else:
            b_spec = pl.BlockSpec((None, tn, tk), lambda i, j, kk: (kk // per, j, kk % per))
    o_spec = pl.BlockSpec((tm, tn), lambda i, j, kk: (i, j))
    has_add = add is not None
    grid = (m // tm, n // tn, nk)
    n_in = 3 if has_add else 2
    c_in, c_out = (len(comm.ins), len(comm.outs)) if comm is not None else (0, 0)

    def body(*refs):
        a_ref, b_ref = refs[0], refs[1]
        add_ref = refs[2] if has_add else None
        o_ref = refs[n_in + c_in]
        pids = [pl.program_id(ax) for ax in range(3)]
        if comm is not None:
            comm_refs = (refs[n_in:n_in + c_in], refs[n_in + c_in + 1:n_in + c_in + 1 + c_out], refs[-2:])

            @pl.when(jnp.logical_and(jnp.logical_and(pids[0] == 0, pids[1] == 0), pids[2] == 0))
            def _():
                comm.start(*comm_refs)

        def finish(acc):
            if has_add:
                acc = acc + add_ref[...].astype(F32)
            o_ref[...] = acc.astype(o_ref.dtype)

        p = _dot(a_ref[...], b_ref[...], form)
        if nk == 1:
            finish(p)
        else:
            acc_ref = refs[n_in + c_in + 1 + c_out]

            @pl.when(pids[2] == 0)
            def _():
                acc_ref[...] = p

            @pl.when(pids[2] > 0)
            def _():
                acc_ref[...] += p

            @pl.when(pids[2] == nk - 1)
            def _():
                finish(acc_ref[...])

        if comm is not None:
            @pl.when(jnp.logical_and(jnp.logical_and(pids[0] == grid[0] - 1, pids[1] == grid[1] - 1),
                                     pids[2] == grid[2] - 1))
            def _():
                comm.finish(*comm_refs)

    acc_scratch = [pltpu.VMEM((tm, tn), F32)] if nk > 1 else []
    if comm is None:
        return pl.pallas_call(
            body, name=name, grid=grid, in_specs=[a_spec, b_spec] + ([o_spec] if has_add else []), out_specs=o_spec,
            out_shape=jax.ShapeDtypeStruct((m, n), out_dtype), scratch_shapes=acc_scratch,
            compiler_params=_params("parallel", "parallel", "arbitrary"),
        )(*((a, b, add) if has_add else (a, b)))
    outs = pl.pallas_call(
        body, name=name, grid=grid, in_specs=[a_spec, b_spec] + ([o_spec] if has_add else []) + [_ANY] * c_in,
        out_specs=[o_spec] + [_ANY] * c_out, out_shape=[jax.ShapeDtypeStruct((m, n), out_dtype)] + list(comm.outs),
        scratch_shapes=acc_scratch + _sem_pairs(comm.n_sems),
        input_output_aliases={n_in + i: 1 + o for i, o in comm.aliases.items()},
        compiler_params=_params("arbitrary", "arbitrary", "arbitrary"),
    )(*((a, b, add) if has_add else (a, b)), *comm.ins)
    return outs[0], list(outs[1:])


def _rms_fwd(name, x, g):
    t, d = x.shape
    tm = _tile(t, 512, 16)

    def body(x_ref, g_ref, o_ref):
        o_ref[...] = _rms(x_ref[...], g_ref[...]).astype(o_ref.dtype)

    return pl.pallas_call(
        body, name=name, grid=(t // tm,),
        in_specs=[pl.BlockSpec((tm, d), lambda i: (i, 0)), pl.BlockSpec((1, d), lambda i: (0, 0))],
        out_specs=pl.BlockSpec((tm, d), lambda i: (i, 0)),
        out_shape=jax.ShapeDtypeStruct((t, d), MXU_DTYPE), compiler_params=_params("parallel"),
    )(x, g.reshape(1, d))


def _rms_bwd(name, x, g, dh, dres=None):
    t, d = x.shape
    tm = _tile(t, 256, 16)
    has_res = dres is not None

    def body(*refs):
        x_ref, g_ref, dh_ref = refs[:3]
        dres_ref = refs[3] if has_res else None
        dx_ref, dxb_ref, dg_ref = refs[-3:]
        _, vjp = jax.vjp(_rms, x_ref[...], g_ref[...])
        dx, dg = vjp(dh_ref[...].astype(F32))
        if has_res:
            dx = dx + dres_ref[...]
        dx_ref[...] = dx
        dxb_ref[...] = dx.astype(dxb_ref.dtype)
        first = pl.program_id(0) == 0

        @pl.when(first)
        def _():
            dg_ref[...] = dg

        @pl.when(jnp.logical_not(first))
        def _():
            dg_ref[...] += dg

    row = pl.BlockSpec((tm, d), lambda i: (i, 0))
    vec = pl.BlockSpec((1, d), lambda i: (0, 0))
    dx, dxb, dg = pl.pallas_call(
        body, name=name, grid=(t // tm,),
        in_specs=[row, vec, row] + ([row] if has_res else []), out_specs=[row, row, vec],
        out_shape=[jax.ShapeDtypeStruct((t, d), F32), jax.ShapeDtypeStruct((t, d), MXU_DTYPE),
                   jax.ShapeDtypeStruct((1, d), F32)],
        compiler_params=_params("arbitrary"),
    )(*((x, g.reshape(1, d), dh) + ((dres,) if has_res else ())))
    return dx, dxb, dg.reshape(d)


def _final_loss(x, g, target):
    t, d = x.shape
    tm = _tile(t, 256, 16)

    def body(x_ref, g_ref, t_ref, loss_ref, dx_ref, dxb_ref, dg_ref):
        y, vjp = jax.vjp(_rms, x_ref[...], g_ref[...])
        err = y - t_ref[...]
        dx, dg = vjp(err * (1.0 / d))
        dx_ref[...] = dx
        dxb_ref[...] = dx.astype(dxb_ref.dtype)
        part = jnp.zeros((1, LANES), F32) + 0.5 * jnp.sum(jnp.mean(err * err, axis=-1, keepdims=True))
        first = pl.program_id(0) == 0

        @pl.when(first)
        def _():
            dg_ref[...] = dg
            loss_ref[...] = part

        @pl.when(jnp.logical_not(first))
        def _():
            dg_ref[...] += dg
            loss_ref[...] += part

    row = pl.BlockSpec((tm, d), lambda i: (i, 0))
    vec = pl.BlockSpec((1, d), lambda i: (0, 0))
    loss, dx, dxb, dg = pl.pallas_call(
        body, name="final_loss", grid=(t // tm,), in_specs=[row, vec, row],
        out_specs=[pl.BlockSpec((1, LANES), lambda i: (0, 0)), row, row, vec],
        out_shape=[jax.ShapeDtypeStruct((1, LANES), F32), jax.ShapeDtypeStruct((t, d), F32),
                   jax.ShapeDtypeStruct((t, d), MXU_DTYPE), jax.ShapeDtypeStruct((1, d), F32)],
        compiler_params=_params("arbitrary"),
    )(x, g.reshape(1, d), target)
    return loss, dx, dxb, dg.reshape(d)


def _conv_taps(x_ext, w, rows):
    kk = w.shape[0]
    y = x_ext[HALO:] * w[kk - 1:kk, :]
    for j in range(kk - 1):
        y = y + pltpu.roll(x_ext, kk - 1 - j, axis=0)[HALO:] * w[j:j + 1, :]
    return y


def _col_specs(tm, tn, col0, t_rows):
    assert col0 % tn == 0 and tm % HALO == 0
    c0 = col0 // tn
    per, last = tm // HALO, t_rows // HALO - 1
    tile = pl.BlockSpec((tm, tn), lambda j, i: (i, c0 + j))
    prev = pl.BlockSpec((HALO, tn), lambda j, i: (jnp.maximum(i * per - 1, 0), c0 + j))
    nxt = pl.BlockSpec((HALO, tn), lambda j, i: (jnp.minimum((i + 1) * per, last), c0 + j))
    return tile, prev, nxt


def _conv_fwd(name, xa, xa_col, w, w_col, ncols, out_dtype, xb=None, xb_col=0, gate=None, gate_col=0):
    t = xa.shape[0]
    kk = w.shape[0]
    tm, tn = _tile(t, EW_ROWS, HALO), _tile(ncols, EW_COLS)
    nrow = t // tm
    has_b, has_g = xb is not None, gate is not None

    def body(*refs):
        refs = list(refs)
        xa_ref, xap_ref = refs.pop(0), refs.pop(0)
        xb_ref, xbp_ref = (refs.pop(0), refs.pop(0)) if has_b else (None, None)
        w_ref = refs.pop(0)
        g_ref = refs.pop(0) if has_g else None
        o_ref = refs.pop(0)
        i = pl.program_id(1)
        x, xp = xa_ref[...].astype(F32), xap_ref[...].astype(F32)
        if has_b:
            x, xp = x * xb_ref[...].astype(F32), xp * xbp_ref[...].astype(F32)
        xp = jnp.where(i == 0, 0.0, xp)
        y = _conv_taps(jnp.concatenate([xp, x], axis=0), w_ref[...], tm)
        if has_g:
            y = y * g_ref[...].astype(F32)
        o_ref[...] = y.astype(o_ref.dtype)

    a_tile, a_prev, _ = _col_specs(tm, tn, xa_col, t)
    ins, specs = [xa, xa], [a_tile, a_prev]
    if has_b:
        b_tile, b_prev, _ = _col_specs(tm, tn, xb_col, t)
        ins, specs = ins + [xb, xb], specs + [b_tile, b_prev]
    assert w_col % tn == 0
    ins, specs = ins + [w], specs + [pl.BlockSpec((kk, tn), lambda j, i: (0, w_col // tn + j))]
    if has_g:
        ins, specs = ins + [gate], specs + [_col_specs(tm, tn, gate_col, t)[0]]
    return pl.pallas_call(
        body, name=name, grid=(ncols // tn, nrow), in_specs=specs,
        out_specs=pl.BlockSpec((tm, tn), lambda j, i: (i, j)),
        out_shape=jax.ShapeDtypeStruct((t, ncols), out_dtype), compiler_params=_params("parallel", "parallel"),
    )(*ins)


def _conv_bwd(name, xa, xa_col, w, w_col, dy, dy_col, ncols, dx_dtype, xb=None, xb_col=0, gate=None, gate_col=0):
    t = xa.shape[0]
    kk = w.shape[0]
    tm, tn = _tile(t, EW_ROWS, HALO), _tile(ncols, EW_COLS)
    nrow = t // tm
    has_b, has_g = xb is not None, gate is not None

    def body(*refs):
        refs = list(refs)
        xa_ref, xap_ref = refs.pop(0), refs.pop(0)
        xb_ref, xbp_ref = (refs.pop(0), refs.pop(0)) if has_b else (None, None)
        w_ref = refs.pop(0)
        dy_ref, dyn_ref = refs.pop(0), refs.pop(0)
        g_ref, gn_ref = (refs.pop(0), refs.pop(0)) if has_g else (None, None)
        dxa_ref = refs.pop(0)
        dxb_ref = refs.pop(0) if has_b else None
        dg_ref = refs.pop(0) if has_g else None
        dw_ref = refs.pop(0)
        i = pl.program_id(1)
        wv = w_ref[...]
        xa_t, xa_p = xa_ref[...].astype(F32), xap_ref[...].astype(F32)
        x, xp = xa_t, xa_p
        if has_b:
            xb_t = xb_ref[...].astype(F32)
            x, xp = x * xb_t, xp * xbp_ref[...].astype(F32)
        xp = jnp.where(i == 0, 0.0, xp)
        x_ext = jnp.concatenate([xp, x], axis=0)
        dyv, dyn = dy_ref[...].astype(F32), dyn_ref[...].astype(F32)
        if has_g:
            dg_ref[...] = (dyv * _conv_taps(x_ext, wv, tm)).astype(dg_ref.dtype)
            dyv, dyn = dyv * g_ref[...].astype(F32), dyn * gn_ref[...].astype(F32)
        dyn = jnp.where(i == nrow - 1, 0.0, dyn)
        dy_ext = jnp.concatenate([dyv, dyn], axis=0)
        dx = dyv * wv[kk - 1:kk, :]
        row8 = lax.broadcasted_iota(jnp.int32, (8, tn), 0)
        dw = jnp.where(row8 == kk - 1, jnp.sum(dyv * x, axis=0, keepdims=True), 0.0)
        for j in range(kk - 1):
            s = kk - 1 - j
            dx = dx + pltpu.roll(dy_ext, tm + HALO - s, axis=0)[:tm] * wv[j:j + 1, :]
            dwj = jnp.sum(dyv * pltpu.roll(x_ext, s, axis=0)[HALO:], axis=0, keepdims=True)
            dw = dw + jnp.where(row8 == j, dwj, 0.0)
        if has_b:
            dxa_ref[...] = (dx * xb_t).astype(dxa_ref.dtype)
            dxb_ref[...] = (dx * xa_t).astype(dxb_ref.dtype)
        else:
            dxa_ref[...] = dx.astype(dxa_ref.dtype)

        @pl.when(i == 0)
        def _():
            dw_ref[...] = dw

        @pl.when(i > 0)
        def _():
            dw_ref[...] += dw

    a_tile, a_prev, _ = _col_specs(tm, tn, xa_col, t)
    ins, specs = [xa, xa], [a_tile, a_prev]
    if has_b:
        b_tile, b_prev, _ = _col_specs(tm, tn, xb_col, t)
        ins, specs = ins + [xb, xb], specs + [b_tile, b_prev]
    assert w_col % tn == 0
    ins, specs = ins + [w], specs + [pl.BlockSpec((kk, tn), lambda j, i: (0, w_col // tn + j))]
    d_tile, _, d_next = _col_specs(tm, tn, dy_col, t)
    ins, specs = ins + [dy, dy], specs + [d_tile, d_next]
    if has_g:
        g_tile, _, g_next = _col_specs(tm, tn, gate_col, t)
        ins, specs = ins + [gate, gate], specs + [g_tile, g_next]
    out_tile = pl.BlockSpec((tm, tn), lambda j, i: (i, j))
    shapes, ospecs = [jax.ShapeDtypeStruct((t, ncols), dx_dtype)], [out_tile]
    if has_b:
        shapes, ospecs = shapes + [jax.ShapeDtypeStruct((t, ncols), dx_dtype)], ospecs + [out_tile]
    if has_g:
        shapes, ospecs = shapes + [jax.ShapeDtypeStruct((t, ncols), dx_dtype)], ospecs + [out_tile]
    shapes, ospecs = shapes + [jax.ShapeDtypeStruct((8, ncols), F32)], ospecs + [pl.BlockSpec((8, tn), lambda j, i: (0, j))]
    outs = list(pl.pallas_call(
        body, name=name, grid=(ncols // tn, nrow), in_specs=specs, out_specs=ospecs, out_shape=shapes,
        compiler_params=_params("parallel", "arbitrary"),
    )(*ins))
    dxa = outs.pop(0)
    dxb = outs.pop(0) if has_b else None
    dgate = outs.pop(0) if has_g else None
    return dxa, dxb, dgate, outs.pop(0)[:kk]


def _ffn_act_fwd(upre, w):
    t, f2 = upre.shape
    f, kk = f2 // 2, w.shape[0]
    tm, tn = _tile(t, EW_ROWS, HALO), _tile(f, EW_COLS // 2)
    nf = f // tn

    def body(g_ref, gp_ref, u_ref, up_ref, wg_ref, wu_ref, cg_ref, cu_ref, a_ref):
        first = pl.program_id(1) == 0

        def conv(x_ref, prev_ref, w_ref):
            prev = jnp.where(first, 0.0, prev_ref[...].astype(F32))
            return _conv_taps(jnp.concatenate([prev, x_ref[...].astype(F32)], axis=0), w_ref[...], tm)

        cg, cu = conv(g_ref, gp_ref, wg_ref), conv(u_ref, up_ref, wu_ref)
        cg_ref[...] = cg.astype(cg_ref.dtype)
        cu_ref[...] = cu.astype(cu_ref.dtype)
        a_ref[...] = (_silu(cg) * cu).astype(a_ref.dtype)

    g_tile, g_prev, _ = _col_specs(tm, tn, 0, t)
    u_tile, u_prev, _ = _col_specs(tm, tn, f, t)
    out = pl.BlockSpec((tm, tn), lambda j, i: (i, j))
    return pl.pallas_call(
        body, name="ffn_act_fwd", grid=(nf, t // tm),
        in_specs=[g_tile, g_prev, u_tile, u_prev, pl.BlockSpec((kk, tn), lambda j, i: (0, j)),
                  pl.BlockSpec((kk, tn), lambda j, i: (0, nf + j))],
        out_specs=[out, out, out], out_shape=[jax.ShapeDtypeStruct((t, f), MXU_DTYPE)] * 3,
        compiler_params=_params("parallel", "parallel"),
    )(upre, upre, upre, upre, w, w)


def _swiglu_bwd(ug, uu, da):
    t, f = ug.shape
    f2 = 2 * f
    tm, tn = _tile(t, EW_ROWS, 16), _tile(f, EW_COLS)
    nf = f // tn

    def body(g_ref, u_ref, da_ref, o_ref):
        g, d = g_ref[...].astype(F32), da_ref[...].astype(F32)
        sg = _sigmoid(g)
        gate_half = pl.program_id(1) < nf

        @pl.when(gate_half)
        def _():
            o_ref[...] = (d * u_ref[...].astype(F32) * (sg * (1.0 + g * (1.0 - sg)))).astype(o_ref.dtype)

        @pl.when(jnp.logical_not(gate_half))
        def _():
            o_ref[...] = (d * (g * sg)).astype(o_ref.dtype)

    return pl.pallas_call(
        body, name="swiglu_bwd", grid=(t // tm, 2 * nf),
        in_specs=[pl.BlockSpec((tm, tn), lambda i, j: (i, j % nf))] * 3,
        out_specs=pl.BlockSpec((tm, tn), lambda i, j: (i, j)),
        out_shape=jax.ShapeDtypeStruct((t, f2), MXU_DTYPE), compiler_params=_params("parallel", "parallel"),
    )(ug, uu, da)


def _gdn_prep(ops, qc, kc, vc, b_col, a_col, a_log, dt_bias):
    c, dh = qc.shape[-2:]
    q, k, v = _silu(qc), _silu(kc), _silu(vc)
    q = q * lax.rsqrt(jnp.sum(q * q, axis=-1, keepdims=True) + EPS) * (dh ** -0.5)
    k = k * lax.rsqrt(jnp.sum(k * k, axis=-1, keepdims=True) + EPS)
    beta = _sigmoid(b_col)
    g_col = -jnp.exp(a_log) * _softplus(a_col + dt_bias)
    r = lax.broadcasted_iota(jnp.int32, (c, c), 0)
    s = lax.broadcasted_iota(jnp.int32, (c, c), 1)
    g_row = jnp.sum(jnp.where(r == s, g_col, 0.0), axis=-2, keepdims=True)
    gc_col = jnp.sum(jnp.where(s <= r, g_row, 0.0), axis=-1, keepdims=True)
    gc_row = jnp.sum(jnp.where(r <= s, g_col, 0.0), axis=-2, keepdims=True)
    decay = jnp.exp(jnp.where(s <= r, gc_col - gc_row, -1e30))
    kb = k * beta
    a = jnp.where(s < r, ops.mm(kb, k, "nt") * decay, 0.0)
    tinv = ops.tri_inv(a)
    e_col = jnp.exp(gc_col)
    uw = ops.mmh(tinv, jnp.concatenate([v * beta, kb * e_col], axis=-1))
    u, w = uw[..., :dh], uw[..., dh:]
    attn = ops.mm(q, k, "nt") * decay
    g_last = jnp.sum(g_col, axis=-2, keepdims=True)
    return u, w, attn, q * e_col, k * jnp.exp(g_last - gc_col), g_last, tinv


def _gdn_step(ops, state, u, w, attn, q_dec, k_dec, g_last):
    v_new = u - ops.mm(w, state)
    o = ops.mm(q_dec, state) + ops.mm(attn, v_new)
    return o, state * jnp.exp(g_last) + ops.mm(k_dec, v_new, "tn")


PREP_HEADS, SCAN_HEADS = 4, 8


def _gdn_blocks(t, heads, hb_pref):
    tc = _tile(t, 256, CHUNK)
    hb = max(h for h in range(1, hb_pref + 1) if heads % h == 0)
    return tc, hb


def _to_chunks(ref, hb, dh):
    tc = ref.shape[0]
    return jnp.concatenate([ref[:, h * dh:(h + 1) * dh].astype(F32).reshape(tc // CHUNK, CHUNK, dh)
                            for h in range(hb)], axis=0)


def _from_chunks(ref, val, hb, dh):
    tc = ref.shape[0]
    ncb = tc // CHUNK
    for h in range(hb):
        ref[:, h * dh:(h + 1) * dh] = val[h * ncb:(h + 1) * ncb].reshape(tc, dh).astype(ref.dtype)


def _per_chunk(s, ncb):
    hb = s.shape[0]
    return jnp.broadcast_to(s[:, None], (hb, ncb, 1, 1)).reshape(hb * ncb, 1, 1)


def _gate_columns(pba, first_head, hb, heads):
    tc = pba.shape[0]
    lane = lax.broadcasted_iota(jnp.int32, pba.shape, 1)

    def pick(k):
        return jnp.sum(jnp.where(lane == k, pba, 0.0), axis=1, keepdims=True).reshape(tc // CHUNK, CHUNK, 1)

    return (jnp.concatenate([pick(first_head + h) for h in range(hb)], axis=0),
            jnp.concatenate([pick(heads + first_head + h) for h in range(hb)], axis=0))


def _gdn_prep_fwd(qkv, pba, a_log, dt_bias, heads, dh, comm=None):
    t = qkv.shape[0]
    tc, hb = _gdn_blocks(t, heads, PREP_HEADS)
    ncb, nhb, width = tc // CHUNK, heads // hb, heads * dh
    nc = t // CHUNK
    grid = (t // tc, nhb)
    c_in, c_out = (len(comm.ins), len(comm.outs)) if comm is not None else (0, 0)

    def body(*refs):
        q_ref, k_ref, v_ref, g_ref, al_ref, dt_ref = refs[:6]
        u_ref, w_ref, p_ref, qd_ref, kd_ref, gl_ref, ti_ref = refs[6 + c_in:13 + c_in]
        if comm is not None:
            comm_refs = (refs[6:6 + c_in], refs[13 + c_in:13 + c_in + c_out], refs[-2:])

            @pl.when(jnp.logical_and(pl.program_id(0) == 0, pl.program_id(1) == 0))
            def _():
                comm.start(*comm_refs)

        b_col, a_col = _gate_columns(g_ref[...], pl.program_id(1) * hb, hb, heads)
        u, w, p, qd, kd, gl, tinv = _gdn_prep(
            _PLAIN, _to_chunks(q_ref, hb, dh), _to_chunks(k_ref, hb, dh), _to_chunks(v_ref, hb, dh), b_col, a_col,
            _per_chunk(al_ref[...], ncb), _per_chunk(dt_ref[...], ncb))
        _from_chunks(u_ref, u, hb, dh)
        _from_chunks(w_ref, w, hb, dh)
        _from_chunks(qd_ref, qd, hb, dh)
        _from_chunks(kd_ref, kd, hb, dh)
        p_ref[...] = p.reshape(hb, tc, CHUNK).astype(p_ref.dtype)
        gl_ref[...] = gl.reshape(hb, ncb, 1, 1)
        ti_ref[...] = tinv.reshape(hb, tc, CHUNK)
        if comm is not None:
            @pl.when(jnp.logical_and(pl.program_id(0) == grid[0] - 1, pl.program_id(1) == grid[1] - 1))
            def _():
                comm.finish(*comm_refs)

    def tok(off):
        return pl.BlockSpec((tc, hb * dh), lambda i, j: (i, off * nhb + j))

    gate = pl.BlockSpec((tc, LANES), lambda i, j: (i, 0))
    scal = pl.BlockSpec((hb, 1, 1), lambda i, j: (j, 0, 0))
    square = pl.BlockSpec((hb, tc, CHUNK), lambda i, j: (j, i, 0))
    outs = pl.pallas_call(
        body, name="gdn_prep_fwd", grid=grid,
        in_specs=[tok(0), tok(1), tok(2), gate, scal, scal] + [_ANY] * c_in,
        out_specs=[tok(0), tok(0), square, tok(0), tok(0), pl.BlockSpec((hb, ncb, 1, 1), lambda i, j: (j, i, 0, 0)),
                   square] + [_ANY] * c_out,
        out_shape=[jax.ShapeDtypeStruct((t, width), F32), jax.ShapeDtypeStruct((t, width), MXU_DTYPE),
                   jax.ShapeDtypeStruct((heads, t, CHUNK), MXU_DTYPE), jax.ShapeDtypeStruct((t, width), MXU_DTYPE),
                   jax.ShapeDtypeStruct((t, width), MXU_DTYPE), jax.ShapeDtypeStruct((heads, nc, 1, 1), F32),
                   jax.ShapeDtypeStruct((heads, t, CHUNK), F32)] + (list(comm.outs) if comm is not None else []),
        scratch_shapes=_sem_pairs(comm.n_sems) if comm is not None else [],
        compiler_params=_params("arbitrary", "arbitrary") if comm is not None else _params("parallel", "parallel"),
    )(qkv, qkv, qkv, pba, a_log, dt_bias, *(comm.ins if comm is not None else ()))
    return tuple(outs[:6]), outs[6], list(outs[7:])


def _gdn_prep_bwd(qkv, pba, a_log, dt_bias, tinv, du, dw, dp, dqd, dkd, dgl, heads, dh):
    t = qkv.shape[0]
    tc, hb = _gdn_blocks(t, heads, PREP_HEADS)
    ncb, nhb, width = tc // CHUNK, heads // hb, heads * dh

    def body(q_ref, k_ref, v_ref, g_ref, al_ref, dt_ref, ti_ref, du_ref, dw_ref, dp_ref, dqd_ref, dkd_ref, dgl_ref,
             dq_ref, dk_ref, dv_ref, dg_ref, dal_ref, ddt_ref):
        first_head = pl.program_id(1) * hb
        b_col, a_col = _gate_columns(g_ref[...], first_head, hb, heads)
        ops = _Ops(True, ti_ref[...].reshape(hb * ncb, CHUNK, CHUNK))

        def prep(q, k, v, b, a, al, dt):
            return _gdn_prep(ops, q, k, v, b, a, _per_chunk(al, ncb), _per_chunk(dt, ncb))[:6]

        _, vjp = jax.vjp(prep, _to_chunks(q_ref, hb, dh), _to_chunks(k_ref, hb, dh), _to_chunks(v_ref, hb, dh),
                         b_col, a_col, al_ref[...], dt_ref[...])
        dq, dk, dv, db, da, dal, ddt = vjp((
            _to_chunks(du_ref, hb, dh), _to_chunks(dw_ref, hb, dh), dp_ref[...].reshape(hb * ncb, CHUNK, CHUNK),
            _to_chunks(dqd_ref, hb, dh), _to_chunks(dkd_ref, hb, dh), dgl_ref[...].reshape(hb * ncb, 1, 1)))
        _from_chunks(dq_ref, dq, hb, dh)
        _from_chunks(dk_ref, dk, hb, dh)
        _from_chunks(dv_ref, dv, hb, dh)
        dal_ref[...] = dal[None]
        ddt_ref[...] = ddt[None]
        lane = lax.broadcasted_iota(jnp.int32, (tc, LANES), 1)
        dgates = jnp.zeros((tc, LANES), F32)
        for h in range(hb):
            rows = slice(h * ncb, (h + 1) * ncb)
            dgates = dgates + jnp.where(lane == first_head + h, db[rows].reshape(tc, 1), 0.0) \
                + jnp.where(lane == heads + first_head + h, da[rows].reshape(tc, 1), 0.0)

        @pl.when(first_head == 0)
        def _():
            dg_ref[...] = dgates

        @pl.when(first_head > 0)
        def _():
            dg_ref[...] += dgates

    def tok(off):
        return pl.BlockSpec((tc, hb * dh), lambda i, j: (i, off * nhb + j))

    gate = pl.BlockSpec((tc, LANES), lambda i, j: (i, 0))
    scal = pl.BlockSpec((hb, 1, 1), lambda i, j: (j, 0, 0))
    part = pl.BlockSpec((1, hb, 1, 1), lambda i, j: (i, j, 0, 0))
    pspec = pl.BlockSpec((hb, tc, CHUNK), lambda i, j: (j, i, 0))
    glspec = pl.BlockSpec((hb, ncb, 1, 1), lambda i, j: (j, i, 0, 0))
    tokf = jax.ShapeDtypeStruct((t, width), F32)
    partf = jax.ShapeDtypeStruct((t // tc, heads, 1, 1), F32)
    return pl.pallas_call(
        body, name="gdn_prep_bwd", grid=(t // tc, nhb),
        in_specs=[tok(0), tok(1), tok(2), gate, scal, scal, pspec, tok(0), tok(0), pspec, tok(0), tok(0), glspec],
        out_specs=[tok(0), tok(0), tok(0), gate, part, part],
        out_shape=[tokf, tokf, tokf, jax.ShapeDtypeStruct((t, LANES), F32), partf, partf],
        compiler_params=_params("parallel", "arbitrary"),
    )(qkv, qkv, qkv, pba, a_log, dt_bias, tinv, du, dw, dp, dqd, dkd, dgl)


def _heads(ref, rows, hb, dh):
    return jnp.stack([ref[rows, h * dh:(h + 1) * dh].astype(F32) for h in range(hb)])


def _put_heads(ref, rows, val, dh):
    for h in range(val.shape[0]):
        ref[rows, h * dh:(h + 1) * dh] = val[h].astype(ref.dtype)


def _gdn_scan_fwd(u, w, p, qd, kd, gl, heads, dh):
    t = u.shape[0]
    tc, hb = _gdn_blocks(t, heads, SCAN_HEADS)
    ncb, nhb = tc // CHUNK, heads // hb
    nc = t // CHUNK

    def body(u_ref, w_ref, p_ref, qd_ref, kd_ref, gl_ref, o_ref, s_ref, state):
        @pl.when(pl.program_id(1) == 0)
        def _():
            state[...] = jnp.zeros_like(state)

        for c in range(ncb):
            rs = slice(c * CHUNK, (c + 1) * CHUNK)
            s_in = state[...]
            s_ref[:, c] = s_in
            o, s_out = _gdn_step(_PLAIN, s_in, _heads(u_ref, rs, hb, dh), _heads(w_ref, rs, hb, dh), p_ref[:, rs, :],
                                 _heads(qd_ref, rs, hb, dh), _heads(kd_ref, rs, hb, dh), gl_ref[:, c])
            _put_heads(o_ref, rs, o, dh)
            state[...] = s_out

    tok = pl.BlockSpec((tc, hb * dh), lambda j, i: (i, j))
    pspec = pl.BlockSpec((hb, tc, CHUNK), lambda j, i: (j, i, 0))
    glspec = pl.BlockSpec((hb, ncb, 1, 1), lambda j, i: (j, i, 0, 0))
    return pl.pallas_call(
        body, name="gdn_scan_fwd", grid=(nhb, t // tc),
        in_specs=[tok, tok, pspec, tok, tok, glspec],
        out_specs=[tok, pl.BlockSpec((hb, ncb, dh, dh), lambda j, i: (j, i, 0, 0))],
        out_shape=[jax.ShapeDtypeStruct((t, heads * dh), F32), jax.ShapeDtypeStruct((heads, nc, dh, dh), F32)],
        scratch_shapes=[pltpu.VMEM((hb, dh, dh), F32)],
        compiler_params=_params("arbitrary", "arbitrary"),
    )(u, w, p, qd, kd, gl)


def _gdn_scan_bwd(u, w, p, qd, kd, gl, states, do, heads, dh):
    t = u.shape[0]
    tc, hb = _gdn_blocks(t, heads, SCAN_HEADS)
    ncb, nhb = tc // CHUNK, heads // hb
    nc, nt = t // CHUNK, t // tc

    def body(u_ref, w_ref, p_ref, qd_ref, kd_ref, gl_ref, s_ref, do_ref,
             du_ref, dw_ref, dp_ref, dqd_ref, dkd_ref, dgl_ref, dstate):
        @pl.when(pl.program_id(1) == 0)
        def _():
            dstate[...] = jnp.zeros_like(dstate)

        for c in reversed(range(ncb)):
            rs = slice(c * CHUNK, (c + 1) * CHUNK)
            _, vjp = jax.vjp(functools.partial(_gdn_step, _DIFF), s_ref[:, c], _heads(u_ref, rs, hb, dh),
                             _heads(w_ref, rs, hb, dh), p_ref[:, rs, :].astype(F32), _heads(qd_ref, rs, hb, dh),
                             _heads(kd_ref, rs, hb, dh), gl_ref[:, c])
            ds, du, dw, dp, dqd, dkd, dgl = vjp((_heads(do_ref, rs, hb, dh), dstate[...]))
            dstate[...] = ds
            _put_heads(du_ref, rs, du, dh)
            _put_heads(dw_ref, rs, dw, dh)
            _put_heads(dqd_ref, rs, dqd, dh)
            _put_heads(dkd_ref, rs, dkd, dh)
            dp_ref[:, rs, :] = dp
            dgl_ref[:, c] = dgl

    tok = pl.BlockSpec((tc, hb * dh), lambda j, i: (nt - 1 - i, j))
    pspec = pl.BlockSpec((hb, tc, CHUNK), lambda j, i: (j, nt - 1 - i, 0))
    glspec = pl.BlockSpec((hb, ncb, 1, 1), lambda j, i: (j, nt - 1 - i, 0, 0))
    sspec = pl.BlockSpec((hb, ncb, dh, dh), lambda j, i: (j, nt - 1 - i, 0, 0))
    tokf = jax.ShapeDtypeStruct((t, heads * dh), F32)
    return pl.pallas_call(
        body, name="gdn_scan_bwd", grid=(nhb, nt),
        in_specs=[tok, tok, pspec, tok, tok, glspec, sspec, tok],
        out_specs=[tok, tok, pspec, tok, tok, glspec],
        out_shape=[tokf, tokf, jax.ShapeDtypeStruct((heads, t, CHUNK), F32), tokf, tokf,
                   jax.ShapeDtypeStruct((heads, nc, 1, 1), F32)],
        scratch_shapes=[pltpu.VMEM((hb, dh, dh), F32)],
        compiler_params=_params("arbitrary", "arbitrary"),
    )(u, w, p, qd, kd, gl, states, do)


def _gdn_post(o, z, gain):
    return _rms(o, gain) * _silu(z)


def _gdn_post_fwd(o, pm, z_col, gain, heads, dh):
    t, wid = o.shape
    tm = _tile(t, 256, 16)
    assert z_col % wid == 0

    def body(o_ref, z_ref, g_ref, y_ref):
        for h in range(heads):
            ls = slice(h * dh, (h + 1) * dh)
            y_ref[:, ls] = _gdn_post(o_ref[:, ls], z_ref[:, ls], g_ref[...]).astype(y_ref.dtype)

    blk = pl.BlockSpec((tm, wid), lambda i: (i, 0))
    return pl.pallas_call(
        body, name="gdn_post_fwd", grid=(t // tm,),
        in_specs=[blk, pl.BlockSpec((tm, wid), lambda i: (i, z_col // wid)), pl.BlockSpec((1, dh), lambda i: (0, 0))],
        out_specs=blk, out_shape=jax.ShapeDtypeStruct((t, wid), MXU_DTYPE), compiler_params=_params("parallel"),
    )(o, pm, gain.reshape(1, dh))


def _gdn_post_bwd(o, pm, z_col, gain, dy, heads, dh):
    t, wid = o.shape
    tm = _tile(t, 256, 16)
    assert z_col % wid == 0

    def body(o_ref, z_ref, g_ref, dy_ref, do_ref, dz_ref, dg_ref):
        dg = jnp.zeros((1, dh), F32)
        for h in range(heads):
            ls = slice(h * dh, (h + 1) * dh)
            _, vjp = jax.vjp(_gdn_post, o_ref[:, ls], z_ref[:, ls], g_ref[...])
            do, dz, dg_h = vjp(dy_ref[:, ls])
            do_ref[:, ls] = do
            dz_ref[:, ls] = dz.astype(dz_ref.dtype)
            dg = dg + dg_h
        first = pl.program_id(0) == 0

        @pl.when(first)
        def _():
            dg_ref[...] = dg

        @pl.when(jnp.logical_not(first))
        def _():
            dg_ref[...] += dg

    blk = pl.BlockSpec((tm, wid), lambda i: (i, 0))
    vec = pl.BlockSpec((1, dh), lambda i: (0, 0))
    do, dz, dg = pl.pallas_call(
        body, name="gdn_post_bwd", grid=(t // tm,),
        in_specs=[blk, pl.BlockSpec((tm, wid), lambda i: (i, z_col // wid)), vec, blk], out_specs=[blk, blk, vec],
        out_shape=[jax.ShapeDtypeStruct((t, wid), F32), jax.ShapeDtypeStruct((t, wid), MXU_DTYPE),
                   jax.ShapeDtypeStruct((1, dh), F32)],
        compiler_params=_params("arbitrary"),
    )(o, pm, gain.reshape(1, dh), dy)
    return do, dz, dg.reshape(dh)


def _attn(ops, q, kv):
    d = q.shape[1]
    hd = d // XATTN_HEADS
    outs = []
    for h in range(XATTN_HEADS):
        qh, kh, vh = q[:, h * hd:(h + 1) * hd], kv[:, h * hd:(h + 1) * hd], kv[:, d + h * hd:d + (h + 1) * hd]
        s = ops.mm(qh, kh, "nt") * (hd ** -0.5)
        e = jnp.exp(s - lax.stop_gradient(jnp.max(s, axis=-1, keepdims=True)))
        outs.append(ops.mm(e / jnp.sum(e, axis=-1, keepdims=True), vh))
    return jnp.concatenate(outs, axis=1)


def _attn_fwd(q, kv):
    t, d = q.shape
    nm = kv.shape[0]
    tm = _tile(t, 512, 16)

    def body(q_ref, kv_ref, o_ref):
        o_ref[...] = _attn(_PLAIN, q_ref[...], kv_ref[...]).astype(o_ref.dtype)

    return pl.pallas_call(
        body, name="xattn_fwd", grid=(t // tm,),
        in_specs=[pl.BlockSpec((tm, d), lambda i: (i, 0)), pl.BlockSpec((nm, 2 * d), lambda i: (0, 0))],
        out_specs=pl.BlockSpec((tm, d), lambda i: (i, 0)),
        out_shape=jax.ShapeDtypeStruct((t, d), MXU_DTYPE), compiler_params=_params("parallel"),
    )(q, kv)


def _attn_bwd(q, kv, do):
    t, d = q.shape
    nm = kv.shape[0]
    tm = _tile(t, 256, 16)

    def body(q_ref, kv_ref, do_ref, dq_ref, dkv_ref):
        _, vjp = jax.vjp(functools.partial(_attn, _DIFF), q_ref[...].astype(F32), kv_ref[...].astype(F32))
        dq, dkv = vjp(do_ref[...].astype(F32))
        dq_ref[...] = dq.astype(dq_ref.dtype)
        first = pl.program_id(0) == 0

        @pl.when(first)
        def _():
            dkv_ref[...] = dkv

        @pl.when(jnp.logical_not(first))
        def _():
            dkv_ref[...] += dkv

    row = pl.BlockSpec((tm, d), lambda i: (i, 0))
    full = pl.BlockSpec((nm, 2 * d), lambda i: (0, 0))
    return pl.pallas_call(
        body, name="xattn_bwd", grid=(t // tm,), in_specs=[row, full, row], out_specs=[row, full],
        out_shape=[jax.ShapeDtypeStruct((t, d), MXU_DTYPE), jax.ShapeDtypeStruct((nm, 2 * d), F32)],
        compiler_params=_params("arbitrary"),
    )(q, kv, do)


def _adamw(name, w, g, m, v):
    shape = w.shape
    cols = shape[-1]
    rows = w.size // cols
    w2, g2, m2, v2 = (a.reshape(rows, cols) for a in (w, g, m, v))
    tr = _tile(rows, max(8, (1 << 18) // cols // 8 * 8), 8)

    def body(w_ref, g_ref, m_ref, v_ref, d_ref, nm_ref, nv_ref):
        gv = g_ref[...]
        nm = ADAM_B1 * m_ref[...] + (1.0 - ADAM_B1) * gv
        nv = ADAM_B2 * v_ref[...] + (1.0 - ADAM_B2) * jnp.square(gv)
        m_hat = nm / (1.0 - ADAM_B1 ** ADAM_STEP)
        v_hat = nv / (1.0 - ADAM_B2 ** ADAM_STEP)
        d_ref[...] = -ADAM_LR * (m_hat / (jnp.sqrt(v_hat) + ADAM_EPS) + ADAM_WD * w_ref[...])
        nm_ref[...] = nm
        nv_ref[...] = nv

    blk = pl.BlockSpec((tr, cols), lambda i: (i, 0))
    out = jax.ShapeDtypeStruct((rows, cols), F32)
    d, nm, nv = pl.pallas_call(
        body, name=name, grid=(rows // tr,), in_specs=[blk] * 4, out_specs=[blk] * 3, out_shape=[out] * 3,
        compiler_params=_params("parallel"),
    )(w2, g2, m2, v2)
    return d.reshape(shape), nm.reshape(shape), nv.reshape(shape)


def _layer_fwd(x, mem, p, heads, dh, carry, late):
    wid = heads * dh
    sc = x.shape[1] - wid
    p, s, landed = dict(p), {"x0": x}, {}

    def arrived(name, brought):
        landed[name] = brought
        if name in late:
            p.update(late[name](brought))

    def mm(name, *args, **kwargs):
        if name not in carry:
            return _matmul(name, *args, **kwargs)
        out, brought = _matmul(name, *args, comm=carry[name], **kwargs)
        arrived(name, brought)
        return out

    s["h1"] = _rms_fwd("rms_mix", x, p["mix_norm"])
    s["pm"] = pm = mm("mm_mix_in", s["h1"], p["wmain"], "nn", F32)
    s["pba"] = mm("mm_mix_ba", s["h1"], p["wba"], "nn", F32)
    s["qkv"] = _conv_fwd("conv_gdn", pm, 0, p["gdn_conv"], 0, 3 * wid, F32)
    s["prep"], s["tinv"], brought = _gdn_prep_fwd(s["qkv"], s["pba"], p["a_log"], p["dt_bias"], heads, dh,
                                                  comm=carry.get("gdn_prep_fwd"))
    if brought:
        arrived("gdn_prep_fwd", brought)
    s["o"], s["states"] = _gdn_scan_fwd(*s["prep"], heads, dh)
    y_gdn = _gdn_post_fwd(s["o"], pm, 3 * wid, p["gdn_out_norm"], heads, dh)
    y_sc = _conv_fwd("conv_sc", pm, 4 * wid + sc, p["sc_conv"], 0, sc, MXU_DTYPE, xb=pm, xb_col=4 * wid + 2 * sc,
                     gate=pm, gate_col=4 * wid)
    s["ycat"] = jnp.concatenate([y_gdn, y_sc], axis=1)
    s["x1"] = x1 = mm("mm_mix_out", s["ycat"], p["wout"], "nn", F32, add=x)
    s["h2"] = _rms_fwd("rms_xattn", x1, p["xattn_norm"])
    s["q"] = mm("mm_xq", s["h2"], p["wq"], "nn", MXU_DTYPE)
    s["memn"] = _rms_fwd("rms_mem", mem, p["mem_norm"])
    s["kv"] = mm("mm_xkv", s["memn"], p["wkv"], "nn", MXU_DTYPE)
    s["ao"] = _attn_fwd(s["q"], s["kv"])
    s["x2"] = x2 = mm("mm_xo", s["ao"], p["wo"], "nn", F32, add=x1)
    s["h3"] = _rms_fwd("rms_ffn", x2, p["ffn_norm"])
    s["upre"] = mm("mm_ffn_up", s["h3"], p["wup"], "nn", MXU_DTYPE)
    s["ug"], s["uu"], s["act"] = _ffn_act_fwd(s["upre"], p["ffn_conv"])
    return mm("mm_ffn_down", s["act"], p["wdown"], "nn", F32, add=x2), s, landed, p


def _layer_bwd(dx3, dx3b, mem, s, p, heads, dh, reduce):
    wid = heads * dh
    sc = dx3.shape[1] - wid
    pm = s["pm"]
    g = {}

    mm = reduce.carried if reduce is not None else _matmul
    da = mm("mm_ffn_down_dx", dx3b, p["wdown"], "nt", MXU_DTYPE)
    g["wdown"] = mm("mm_ffn_down_dw", s["act"], dx3b, "tn", WIRE_DTYPE)
    du = _swiglu_bwd(s["ug"], s["uu"], da)
    dupre, _, _, g["ffn_conv"] = _conv_bwd("conv_ffn_bwd", s["upre"], 0, p["ffn_conv"], 0, du, 0, du.shape[1],
                                           MXU_DTYPE)
    dh3 = mm("mm_ffn_up_dx", dupre, p["wup"], "nt", F32)
    g["wup"] = mm("mm_ffn_up_dw", s["h3"], dupre, "tn", WIRE_DTYPE)
    dx2, dx2b, g["ffn_norm"] = _rms_bwd("rms_ffn_bwd", s["x2"], p["ffn_norm"], dh3, dx3)
    dao = mm("mm_xo_dx", dx2b, p["wo"], "nt", MXU_DTYPE)
    g["wo"] = mm("mm_xo_dw", s["ao"], dx2b, "tn", WIRE_DTYPE)
    dq, dkv = _attn_bwd(s["q"], s["kv"], dao)
    dh2 = mm("mm_xq_dx", dq, p["wq"], "nt", F32)
    g["wq"] = mm("mm_xq_dw", s["h2"], dq, "tn", WIRE_DTYPE)
    dmemn = mm("mm_xkv_dx", dkv, p["wkv"], "nt", F32)
    g["wkv"] = mm("mm_xkv_dw", s["memn"], dkv, "tn", WIRE_DTYPE)
    _, _, g["mem_norm"] = _rms_bwd("rms_mem_bwd", mem, p["mem_norm"], dmemn)
    dx1, dx1b, g["xattn_norm"] = _rms_bwd("rms_xattn_bwd", s["x1"], p["xattn_norm"], dh2, dx2)
    dycat = mm("mm_mix_out_dx", dx1b, p["wout"], "nt", F32)
    g["wout"] = mm("mm_mix_out_dw", s["ycat"], dx1b, "tn", WIRE_DTYPE)
    d_c, d_h, d_b, g["sc_conv"] = _conv_bwd("conv_sc_bwd", pm, 4 * wid + sc, p["sc_conv"], 0, dycat, wid, sc,
                                             MXU_DTYPE, xb=pm, xb_col=4 * wid + 2 * sc, gate=pm, gate_col=4 * wid)
    do, dz, g["gdn_out_norm"] = _gdn_post_bwd(s["o"], pm, 3 * wid, p["gdn_out_norm"], dycat, heads, dh)
    dprep = _gdn_scan_bwd(*s["prep"], s["states"], do, heads, dh)
    dqc, dkc, dvc, dpba, dal, ddt = _gdn_prep_bwd(s["qkv"], s["pba"], p["a_log"], p["dt_bias"], s["tinv"], *dprep,
                                                  heads, dh)
    g["a_log"], g["dt_bias"] = jnp.sum(dal, axis=0), jnp.sum(ddt, axis=0)
    dqkv, _, _, g["gdn_conv"] = _conv_bwd("conv_gdn_bwd", pm, 0, p["gdn_conv"], 0,
                                          jnp.concatenate([dqc, dkc, dvc], axis=1), 0, 3 * wid, MXU_DTYPE)
    dpm = jnp.concatenate([dqkv, dz, d_b, d_c, d_h], axis=1)
    dpba = dpba.astype(MXU_DTYPE)
    dh1 = mm("mm_mix_in_dx", dpm, p["wmain"], "nt", F32)
    dh1 = mm("mm_mix_ba_dx", dpba, p["wba"], "nt", F32, add=dh1)
    g["wmain"] = mm("mm_mix_in_dw", s["h1"], dpm, "tn", WIRE_DTYPE)
    g["wba"] = mm("mm_mix_ba_dw", s["h1"], dpba, "tn", WIRE_DTYPE)
    dx0, dx0b, g["mix_norm"] = _rms_bwd("rms_mix_bwd", s["x0"], p["mix_norm"], dh1, dx1)
    return dx0, dx0b, g


def _input_projection(win, heads, dh):
    wid = heads * dh
    return {"wmain": jnp.concatenate([win[:, :4 * wid], win[:, 4 * wid + 2 * heads:]], axis=1),
            "wba": jnp.pad(win[:, 4 * wid:4 * wid + 2 * heads], ((0, 0), (0, LANES - 2 * heads)))}


def _square_projections(wout, wq, wk, wv, wo, wdown):
    return {"wout": wout, "wq": wq, "wkv": jnp.concatenate([wk, wv], axis=1), "wo": wo, "wdown": wdown}


_ANY = pl.BlockSpec(memory_space=pl.ANY)
_VMEM = pl.BlockSpec(memory_space=pltpu.VMEM)


def _mesh_pos():
    return lax.axis_index("x"), lax.axis_index("y"), lax.axis_index("c")


def _other_chips(x, y):
    return [(1 - x, y), (x, 1 - y), (1 - x, 1 - y)]


def _push(src, dst, sems, k, to):
    return pltpu.make_async_remote_copy(src_ref=src, dst_ref=dst, send_sem=sems[0].at[k], recv_sem=sems[1].at[k],
                                        device_id=to, device_id_type=MESH)


def _sem_pairs(n):
    return [pltpu.SemaphoreType.DMA((n,)), pltpu.SemaphoreType.DMA((n,))]


class _Comm:
    def __init__(self, ins, outs, n_sems, start, finish, aliases=None):
        self.ins, self.outs, self.n_sems, self.start, self.finish = list(ins), list(outs), n_sems, start, finish
        self.aliases = aliases or {}


def _run_comm(name, comm):
    n_in, n_out = len(comm.ins), len(comm.outs)

    def body(*refs):
        parts = (refs[:n_in], refs[n_in:n_in + n_out], refs[n_in + n_out:])
        comm.start(*parts)
        comm.finish(*parts)

    return pl.pallas_call(
        body, name=name, in_specs=[_ANY] * n_in, out_specs=[_ANY] * n_out, out_shape=comm.outs,
        scratch_shapes=_sem_pairs(comm.n_sems), input_output_aliases=comm.aliases,
    )(*comm.ins)


def _allgather_comm(srcs):
    n = len(srcs)

    def first(src, out, sems):
        x, y, c = _mesh_pos()
        own, sends = [], []
        for t in range(n):
            half = src[t].shape[0] // 2
            mine = pl.ds(c * half, half)
            own.append(_push(src[t], out[t].at[2 * x + y], sems, 7 * t + 6, (x, y, 1 - c)))
            sends += [_push(src[t].at[mine], out[t].at[2 * x + y, mine], sems, 7 * t + k, (cx, cy, c))
                      for k, (cx, cy) in enumerate(_other_chips(x, y))]
        return own, sends

    def start(src, out, sems):
        own, sends = first(src, out, sems)
        for cp in own + sends:
            cp.start()

    def finish(src, out, sems):
        x, y, c = _mesh_pos()
        sibling = (x, y, 1 - c)
        own, sends = first(src, out, sems)
        fwds, relayed = [], []
        for t in range(n):
            half = src[t].shape[0] // 2
            for k, (cx, cy) in enumerate(_other_chips(x, y)):
                here = out[t].at[2 * cx + cy, pl.ds(c * half, half)]
                there = out[t].at[2 * cx + cy, pl.ds((1 - c) * half, half)]
                _push(here, here, sems, 7 * t + k, sibling).wait_recv()
                fwds.append(_push(here, here, sems, 7 * t + 3 + k, sibling))
                fwds[-1].start()
                relayed.append(_push(there, there, sems, 7 * t + 3 + k, sibling))
        for cp in relayed + own:
            cp.wait_recv()
        for cp in own + sends + fwds:
            cp.wait_send()

    return _Comm(srcs, [jax.ShapeDtypeStruct((N_CHIPS,) + s.shape, s.dtype) for s in srcs], 7 * n, start, finish)


def _start_wait(build):
    def start(src, out, sems):
        for cp in build(src, out, sems):
            cp.start()

    def finish(src, out, sems):
        for cp in build(src, out, sems):
            cp.wait()

    return start, finish


def _sibling_exchange_comm(bufs):
    def build(src, out, sems):
        x, y, c = _mesh_pos()
        return [_push(src[t].at[1 - c], out[t], sems, t, (x, y, 1 - c)) for t in range(len(bufs))]

    start, finish = _start_wait(build)
    return _Comm(bufs, [jax.ShapeDtypeStruct(b.shape[1:], b.dtype) for b in bufs], len(bufs), start, finish)


def _chip_exchange_comm(bufs):
    def build(src, out, sems):
        x, y, c = _mesh_pos()
        return [_push(src[t].at[2 * cx + cy], out[t].at[k], sems, 3 * t + k, (cx, cy, c))
                for t in range(len(bufs)) for k, (cx, cy) in enumerate(_other_chips(x, y))]

    start, finish = _start_wait(build)
    return _Comm(bufs, [jax.ShapeDtypeStruct((3,) + b.shape[1:], b.dtype) for b in bufs], 3 * len(bufs), start, finish)


def _sibling_share_comm(bufs):
    def build(src, out, sems):
        x, y, c = _mesh_pos()
        return [_push(src[t].at[c], out[t].at[c], sems, t, (x, y, 1 - c)) for t in range(len(bufs))]

    start, finish = _start_wait(build)
    return _Comm(bufs, [jax.ShapeDtypeStruct(b.shape, b.dtype) for b in bufs], len(bufs), start, finish,
                 aliases={t: t for t in range(len(bufs))})


def _allreduce_small(v):
    r, lanes = v.shape

    def body(v_ref, sum_ref, gath, send_sems, recv_sems):
        x, y, c = _mesh_pos()
        me = 4 * x + 2 * y + c
        gath[me] = v_ref[...]
        copies = []
        for rel in range(1, N_DEV):
            peer = tuple(1 - p if (rel >> b) & 1 else p for p, b in ((x, 2), (y, 1), (c, 0)))
            copies.append(pltpu.make_async_remote_copy(
                src_ref=v_ref, dst_ref=gath.at[me], send_sem=send_sems.at[rel - 1], recv_sem=recv_sems.at[rel - 1],
                device_id=peer, device_id_type=MESH))
        for cp in copies:
            cp.start()
        for cp in copies:
            cp.wait()
        total = gath[0]
        for k in range(1, N_DEV):
            total = total + gath[k]
        sum_ref[...] = total

    return pl.pallas_call(
        body, name="allreduce_small", in_specs=[_VMEM], out_specs=_VMEM,
        out_shape=jax.ShapeDtypeStruct((r, lanes), F32),
        scratch_shapes=[pltpu.VMEM((N_DEV, r, lanes), F32)] + _sem_pairs(N_DEV - 1),
        compiler_params=pltpu.CompilerParams(vmem_limit_bytes=VMEM_LIMIT),
    )(v)


def _sum_tile(rows, width):
    return _tile(rows, max(16, (1 << 19) // width // 16 * 16), 16)


def _sum_sibling(x, recv, core):
    _, n, w = x.shape
    tr = _sum_tile(n, w)

    def body(idx_ref, x_ref, r_ref, o_ref):
        o_ref[...] = (x_ref[...].astype(F32) + r_ref[...].astype(F32)).astype(o_ref.dtype)

    row = pl.BlockSpec((tr, w), lambda i, idx: (i, 0))
    return pl.pallas_call(
        body, name="rs_sum_sibling",
        grid_spec=pltpu.PrefetchScalarGridSpec(
            num_scalar_prefetch=1, grid=(n // tr,),
            in_specs=[pl.BlockSpec((None, tr, w), lambda i, idx: (idx[0], i, 0)), row], out_specs=row),
        out_shape=jax.ShapeDtypeStruct((n, w), x.dtype), compiler_params=_params("parallel"),
    )(core.reshape(1), x, recv)


def _sum_chips(s, recv, chip, core):
    _, m, w = s.shape
    tr = _sum_tile(m, w)

    def body(idx_ref, s_ref, r0_ref, r1_ref, r2_ref, o_ref):
        o_ref[...] = ((s_ref[...].astype(F32) + r0_ref[...].astype(F32)) + r1_ref[...].astype(F32)) \
            + r2_ref[...].astype(F32)

    def got(k):
        return pl.BlockSpec((None, tr, w), lambda i, idx: (k, i, 0))

    return pl.pallas_call(
        body, name="rs_sum_chips",
        grid_spec=pltpu.PrefetchScalarGridSpec(
            num_scalar_prefetch=1, grid=(m // tr,),
            in_specs=[pl.BlockSpec((None, tr, w), lambda i, idx: (idx[0], i, 0)), got(0), got(1), got(2)],
            out_specs=pl.BlockSpec((None, tr, w), lambda i, idx: (idx[1], i, 0))),
        out_shape=jax.ShapeDtypeStruct((2, m, w), F32), compiler_params=_params("parallel"),
    )(jnp.stack([chip, core]), s, recv, recv, recv)


_ROWS = ("w_mix_out", "w_xq", "w_xk", "w_xv", "w_xo", "w_ffn_down")
_CONVS = ("gdn_conv", "sc_conv", "ffn_conv")
_REPLICATED = ("mix_norm", "gdn_a_log", "gdn_dt_bias", "gdn_out_norm", "xattn_norm", "mem_norm", "ffn_norm",
               "final_norm")
_WEIGHTS = ("mix_norm", "w_mix_in", "gdn_conv", "gdn_a_log", "gdn_dt_bias", "gdn_out_norm", "sc_conv", "w_mix_out",
            "xattn_norm", "mem_norm", "w_xq", "w_xk", "w_xv", "w_xo", "ffn_norm", "w_ffn_up", "ffn_conv",
            "w_ffn_down", "final_norm")


def _pad_rows(flat, groups):
    unit = groups * 16 * LANES
    p = flat.shape[-1]
    pad = -p % unit
    if pad:
        flat = jnp.pad(flat, [(0, 0)] * (flat.ndim - 1) + [(0, pad)])
    return flat.reshape(flat.shape[:-1] + (groups, (p + pad) // (groups * LANES), LANES))


def _split_flat(flat, shapes):
    out, off = [], 0
    for shp in shapes:
        size = 1
        for n in shp:
            size *= n
        out.append(flat[..., off:off + size].reshape(flat.shape[:-1] + tuple(shp)))
        off += size
    return out


def _by_chip(g, axis):
    rows, cols = g.shape
    if axis == 0:
        return g.reshape(N_CHIPS, rows // N_CHIPS, cols)
    return g.reshape(rows, N_CHIPS, cols // N_CHIPS).transpose(1, 0, 2)


def _halves_by_chip(g):
    _, rows, w = g.shape
    return g.astype(WIRE_DTYPE).reshape(N_CHIPS, 2, rows // 2, w).transpose(1, 0, 2, 3)


class _ReduceScatter:
    STAGES = ("mm_ffn_down_dx", "mm_ffn_up_dx", "mm_ffn_up_dw", "mm_mix_in_dx")

    def __init__(self, bufs, chip, core):
        self.bufs, self.chip, self.core = list(bufs), chip, core
        self.sums = self.from_chips = self.reduced = self.result = None

    def comm(self, stage):
        if stage == self.STAGES[0]:
            return _sibling_exchange_comm(self.bufs)
        if stage == self.STAGES[1]:
            return _chip_exchange_comm(self.sums[-1:])
        if stage == self.STAGES[2]:
            return _chip_exchange_comm(self.sums[:-1])
        return _sibling_share_comm(self.reduced)

    def landed(self, stage, outs):
        if stage == self.STAGES[0]:
            self.sums = [_sum_sibling(b.reshape(2, -1, b.shape[-1]), r.reshape(-1, r.shape[-1]), self.core)
                         .reshape(r.shape) for b, r in zip(self.bufs, outs)]
        elif stage == self.STAGES[1]:
            self.from_chips = list(outs)
        elif stage == self.STAGES[2]:
            self.reduced = [_sum_chips(s, r, self.chip, self.core)
                            for s, r in zip(self.sums, list(outs) + self.from_chips)]
        else:
            self.result = list(outs)

    def carried(self, name, *args, **kwargs):
        if name not in self.STAGES:
            return _matmul(name, *args, **kwargs)
        out, outs = _matmul(name, *args, comm=self.comm(name), **kwargs)
        self.landed(name, outs)
        return out

    def run_alone(self):
        for stage, name in zip(self.STAGES, ("rs_sibling_exchange", "rs_chip_exchange_rows", "rs_chip_exchange_cols",
                                             "rs_sibling_share")):
            self.landed(stage, _run_comm(name, self.comm(stage)))


def kernel(x, mem, mix_norm, w_mix_in, gdn_conv, gdn_a_log, gdn_dt_bias, gdn_out_norm, sc_conv, w_mix_out, xattn_norm, mem_norm, w_xq, w_xk, w_xv, w_xo, ffn_norm, w_ffn_up, ffn_conv, w_ffn_down, final_norm, loss_target, m_mix_norm, m_w_mix_in, m_gdn_conv, m_gdn_a_log, m_gdn_dt_bias, m_gdn_out_norm, m_sc_conv, m_w_mix_out, m_xattn_norm, m_mem_norm, m_w_xq, m_w_xk, m_w_xv, m_w_xo, m_ffn_norm, m_w_ffn_up, m_ffn_conv, m_w_ffn_down, m_final_norm, v_mix_norm, v_w_mix_in, v_gdn_conv, v_gdn_a_log, v_gdn_dt_bias, v_gdn_out_norm, v_sc_conv, v_w_mix_out, v_xattn_norm, v_mem_norm, v_w_xq, v_w_xk, v_w_xv, v_w_xo, v_ffn_norm, v_w_ffn_up, v_ffn_conv, v_w_ffn_down, v_final_norm):
    w = dict(zip(_WEIGHTS, (mix_norm, w_mix_in, gdn_conv, gdn_a_log, gdn_dt_bias, gdn_out_norm, sc_conv, w_mix_out,
                            xattn_norm, mem_norm, w_xq, w_xk, w_xv, w_xo, ffn_norm, w_ffn_up, ffn_conv, w_ffn_down,
                            final_norm)))
    m = dict(zip(_WEIGHTS, (m_mix_norm, m_w_mix_in, m_gdn_conv, m_gdn_a_log, m_gdn_dt_bias, m_gdn_out_norm, m_sc_conv,
                            m_w_mix_out, m_xattn_norm, m_mem_norm, m_w_xq, m_w_xk, m_w_xv, m_w_xo, m_ffn_norm,
                            m_w_ffn_up, m_ffn_conv, m_w_ffn_down, m_final_norm)))
    v = dict(zip(_WEIGHTS, (v_mix_norm, v_w_mix_in, v_gdn_conv, v_gdn_a_log, v_gdn_dt_bias, v_gdn_out_norm, v_sc_conv,
                            v_w_mix_out, v_xattn_norm, v_mem_norm, v_w_xq, v_w_xk, v_w_xv, v_w_xo, v_ffn_norm,
                            v_w_ffn_up, v_ffn_conv, v_w_ffn_down, v_final_norm)))
    core = lax.axis_index("c")
    chip = 2 * lax.axis_index("x") + lax.axis_index("y")
    depth, heads = gdn_a_log.shape
    dh = gdn_out_norm.shape[1]
    d, wid = x.shape[2], heads * dh

    row_sizes = [w[n].shape[1] for n in _ROWS]
    row_offs = [sum(row_sizes[:k]) for k in range(len(_ROWS))]
    src_in, src_up = w_mix_in.astype(WIRE_DTYPE), w_ffn_up.astype(WIRE_DTYPE)
    src_rows = jnp.concatenate([w[n] for n in _ROWS], axis=1).astype(WIRE_DTYPE)
    src_convs = _pad_rows(jnp.concatenate([w[n].reshape(-1) for n in _CONVS]), 2)
    g_in, g_convs = _run_comm("allgather_first", _allgather_comm([src_in[0], src_convs]))
    conv_full = {n: jnp.moveaxis(part, 0, 2).reshape(depth, part.shape[2], -1)
                 for n, part in zip(_CONVS, _split_flat(g_convs.reshape(N_CHIPS, -1), [w[n].shape for n in _CONVS]))}
    side_by_side = lambda g: jnp.concatenate([g[j] for j in range(N_CHIPS)], axis=1)

    def from_rows(brought):
        g_rows, = brought
        return _square_projections(*[jnp.concatenate([g_rows[j, off:off + size] for j in range(N_CHIPS)], axis=0)
                                     for off, size in zip(row_offs, row_sizes)])

    xl, mem_l = x[0], mem[0]
    layers, saved, g_rows = [], [], None
    for l in range(depth):
        p = {"mix_norm": mix_norm[l], "xattn_norm": xattn_norm[l], "mem_norm": mem_norm[l], "ffn_norm": ffn_norm[l],
             "gdn_out_norm": gdn_out_norm[l], "a_log": gdn_a_log[l].reshape(heads, 1, 1),
             "dt_bias": gdn_dt_bias[l].reshape(heads, 1, 1), "gdn_conv": conv_full["gdn_conv"][l],
             "sc_conv": conv_full["sc_conv"][l], "ffn_conv": conv_full["ffn_conv"][l]}
        p.update(_input_projection(side_by_side(g_in), heads, dh))
        carry = {"gdn_prep_fwd": _allgather_comm([src_up[l]])}
        late = {"gdn_prep_fwd": lambda brought: {"wup": brought[0]}}
        if l == 0:
            carry["mm_mix_in"], late["mm_mix_in"] = _allgather_comm([src_rows[0]]), from_rows
        else:
            p.update(from_rows(g_rows))
        if l + 1 < depth:
            carry["mm_ffn_up"] = _allgather_comm([src_rows[l + 1]])
            carry["mm_ffn_down"] = _allgather_comm([src_in[l + 1]])
        xl, s, landed, p = _layer_fwd(xl, mem_l, p, heads, dh, carry, late)
        layers.append(p)
        saved.append(s)
        if l + 1 < depth:
            g_rows, (g_in,) = landed["mm_ffn_up"], landed["mm_ffn_down"]
    loss_row, dx, dxb, g_final = _final_loss(xl, final_norm, loss_target[0])

    def by_chip(g):
        g_win = jnp.concatenate([g["wmain"][:, :4 * wid], g["wba"][:, :2 * heads], g["wmain"][:, 4 * wid:]], axis=1)
        parts = (g["wout"], g["wq"], g["wkv"][:, :d], g["wkv"][:, d:], g["wo"], g["wdown"])
        return [_halves_by_chip(_by_chip(g_win, 1)), _halves_by_chip(_by_chip(g["wup"], 1)),
                _halves_by_chip(jnp.concatenate([_by_chip(p, 0) for p in parts], axis=1))]

    per_layer, shards, reduce = [None] * depth, [None] * depth, None
    for l in reversed(range(depth)):
        dx, dxb, per_layer[l] = _layer_bwd(dx, dxb, mem_l, saved[l], layers[l], heads, dh, reduce)
        if reduce is not None:
            shards[l + 1] = reduce.result
        reduce = _ReduceScatter(by_chip(per_layer[l]), chip, core)
    reduce.run_alone()
    shards[0] = reduce.result
    by_layer = [[s[t].reshape(-1, s[t].shape[-1]) for s in shards] for t in range(3)]
    grad = {"w_mix_in": jnp.stack(by_layer[0]), "w_ffn_up": jnp.stack(by_layer[1])}
    for n, off, size in zip(_ROWS, row_offs, row_sizes):
        grad[n] = jnp.stack([r[off:off + size] for r in by_layer[2]])

    stack = lambda k: jnp.stack([g[k] for g in per_layer])
    small_g = {"mix_norm": stack("mix_norm"), "gdn_a_log": stack("a_log").reshape(depth, heads),
               "gdn_dt_bias": stack("dt_bias").reshape(depth, heads), "gdn_out_norm": stack("gdn_out_norm"),
               "xattn_norm": stack("xattn_norm"), "mem_norm": stack("mem_norm"), "ffn_norm": stack("ffn_norm"),
               "final_norm": g_final, "gdn_conv": stack("gdn_conv"), "sc_conv": stack("sc_conv"),
               "ffn_conv": stack("ffn_conv")}
    names = _REPLICATED + _CONVS
    small = jnp.concatenate([small_g[n].reshape(-1) for n in names] + [loss_row[0, :1]])
    small_sum = _allreduce_small(_pad_rows(small, 1)[0]).reshape(-1)
    parts = _split_flat(small_sum, [small_g[n].shape for n in names] + [(1,)])
    g_rep = dict(zip(_REPLICATED, parts[:len(_REPLICATED)]))
    for n, part in zip(_CONVS, parts[len(_REPLICATED):-1]):
        grad[n] = lax.dynamic_slice_in_dim(part, chip * w[n].shape[2], w[n].shape[2], axis=2)
    loss = parts[-1][0]

    delta, new_m, new_v = {}, {}, {}
    for n in ("w_mix_in", "w_ffn_up") + _ROWS + _CONVS:
        delta[n], new_m[n], new_v[n] = _adamw("adamw_" + n, w[n], grad[n], m[n], v[n])
    pack_rep = lambda t: _pad_rows(jnp.concatenate([t[n].reshape(-1) for n in _REPLICATED]), 1)[0]
    outs = _adamw("adamw_replicated", pack_rep(w), pack_rep(g_rep), pack_rep(m), pack_rep(v))
    shapes = [w[n].shape for n in _REPLICATED]
    for tgt, packed_out in zip((delta, new_m, new_v), outs):
        tgt.update(zip(_REPLICATED, _split_flat(packed_out.reshape(-1), shapes)))
    grad.update(g_rep)
    return (loss, dx[None], *[grad[n] for n in _WEIGHTS], *[delta[n] for n in _WEIGHTS],
            *[new_m[n] for n in _WEIGHTS], *[new_v[n] for n in _WEIGHTS])
```

```python
import functools

import jax
import jax.numpy as jnp
from jax import lax
from jax.experimental import pallas as pl
from jax.experimental.pallas import tpu as pltpu

F32 = jnp.float32
MXU_DTYPE = jnp.bfloat16
WIRE_DTYPE = jnp.bfloat16
SOLVE_PRECISION = lax.Precision.HIGH
EPS = 1e-6
CHUNK = 64
XATTN_HEADS = 4
LANES = 128
HALO = 16
EW_ROWS, EW_COLS = 256, 2816
VMEM_LIMIT = 52 * 1024 * 1024
ADAM_LR, ADAM_B1, ADAM_B2, ADAM_EPS, ADAM_WD, ADAM_STEP = 0.001, 0.9, 0.999, 1e-08, 0.01, 10
MESH = pl.DeviceIdType.MESH
N_CHIPS = 4
N_DEV = 8

_DIMS = {
    "nn": (((1,), (0,)), ((), ())),
    "nt": (((1,), (1,)), ((), ())),
    "tn": (((0,), (0,)), ((), ())),
}


def _tile(n, pref, align=LANES):
    if n <= pref:
        return n
    t = (pref // align) * align
    while t >= align:
        if n % t == 0:
            return t
        t -= align
    return n


def _params(*sem):
    return pltpu.CompilerParams(dimension_semantics=sem, vmem_limit_bytes=VMEM_LIMIT)


def _dot(a, b, form, hi=False):
    (ca, cb), _ = _DIMS[form]
    dims = (((ca[0] + 1,), (cb[0] + 1,)), ((0,), (0,))) if a.ndim == 3 else _DIMS[form]
    if hi:
        return lax.dot_general(a.astype(F32), b.astype(F32), dims, precision=SOLVE_PRECISION,
                               preferred_element_type=F32)
    return lax.dot_general(a.astype(MXU_DTYPE), b.astype(MXU_DTYPE), dims, preferred_element_type=F32)


@functools.partial(jax.custom_vjp, nondiff_argnums=(2, 3))
def _dot_d(a, b, form, hi):
    return _dot(a, b, form, hi)


def _dot_d_fwd(a, b, form, hi):
    return _dot(a, b, form, hi), (a, b)


def _dot_d_bwd(form, hi, res, g):
    a, b = res
    if form == "nn":
        da, db = _dot_d(g, b, "nt", hi), _dot_d(a, g, "tn", hi)
    elif form == "nt":
        da, db = _dot_d(g, b, "nn", hi), _dot_d(g, a, "tn", hi)
    else:
        da, db = _dot_d(b, g, "nt", hi), _dot_d(a, g, "nn", hi)
    return da.astype(a.dtype), db.astype(b.dtype)


_dot_d.defvjp(_dot_d_fwd, _dot_d_bwd)


def _tri_inv_impl(a, mmh):
    c = a.shape[-1]
    r = lax.broadcasted_iota(jnp.int32, (c, c), 0)
    s = lax.broadcasted_iota(jnp.int32, (c, c), 1)
    eye = (r == s).astype(F32)
    diag_blk = (r // 16) == (s // 16)
    d = jnp.where(diag_blk, a, 0.0)
    low = a - d
    d2 = mmh(d, d)
    d4 = mmh(d2, d2)
    d8 = mmh(d4, d4)
    td = mmh(mmh(mmh(eye - d, eye + d2), eye + d4), eye + d8)
    n = mmh(td, low)
    acc = eye - n
    p = n
    pw = 1
    while 2 * pw < c // 16:
        p = mmh(p, p)
        pw *= 2
        acc = mmh(acc, eye + p)
    return mmh(acc, td)


def _mmh_plain(a, b):
    return _dot(a, b, "nn", True)


@jax.custom_vjp
def _tri_inv_known(a, t):
    return t


def _tri_inv_known_fwd(a, t):
    return t, t


def _tri_inv_known_bwd(t, g):
    return -_dot(_dot(t, g, "tn", True), t, "nt", True), jnp.zeros_like(t)


_tri_inv_known.defvjp(_tri_inv_known_fwd, _tri_inv_known_bwd)


class _Ops:
    def __init__(self, diff, tinv=None):
        self.diff, self.tinv = diff, tinv

    def mm(self, a, b, form="nn"):
        return _dot_d(a, b, form, False) if self.diff else _dot(a, b, form, False)

    def mmh(self, a, b, form="nn"):
        return _dot_d(a, b, form, True) if self.diff else _dot(a, b, form, True)

    def tri_inv(self, a):
        return _tri_inv_known(a, self.tinv) if self.diff else _tri_inv_impl(a, _mmh_plain)


_PLAIN = _Ops(False)
_DIFF = _Ops(True)


def _sigmoid(x):
    return 1.0 / (1.0 + jnp.exp(-x))


def _silu(x):
    return x * _sigmoid(x)


def _softplus(x):
    return jnp.maximum(x, 0.0) + jnp.log(1.0 + jnp.exp(-jnp.abs(x)))


def _rms(x, g):
    return x * lax.rsqrt(jnp.mean(x * x, axis=-1, keepdims=True) + EPS) * g


def _matmul_tiles(m, n, k, form):
    if k <= 2048:
        return _tile(m, 1024), _tile(n, 1408), k
    if k <= 8192:
        return _tile(m, 1024 if form == "nn" else 512), _tile(n, 512), k
    return _tile(m, 1024), _tile(n, 1024), _tile(k, 2816)


def _matmul(name, a, b, form, out_dtype, add=None, comm=None):
    b_shape = b.shape if b.ndim == 2 else (b.shape[1], N_CHIPS * b.shape[2])
    if form == "nn":
        (m, k), (k2, n) = a.shape, b_shape
    elif form == "nt":
        (m, k), (n, k2) = a.shape, b_shape
    else:
        (k, m), (k2, n) = a.shape, b_shape
    assert k == k2, (name, a.shape, b.shape, form)
    tm, tn, tk = _matmul_tiles(m, n, k, form)
    if b.ndim == 3:
        assert form != "tn", name
        tn, tk = (_tile(b.shape[2], tn), tk) if form == "nn" else (tn, _tile(b.shape[2], tk))
    nk = k // tk
    out_bytes = tm * tn * (jnp.dtype(out_dtype).itemsize + (4 if add is not None else 0))
    vmem = 2 * (tm * tk * a.dtype.itemsize + tk * tn * b.dtype.itemsize + out_bytes) + (tm * tn * 4 if nk > 1 else 0)
    assert vmem <= VMEM_LIMIT, (name, tm, tn, tk, vmem)
    if form == "nn":
        a_spec = pl.BlockSpec((tm, tk), lambda i, j, kk: (i, kk))
        b_spec = pl.BlockSpec((tk, tn), lambda i, j, kk: (kk, j))
    elif form == "nt":
        a_spec = pl.BlockSpec((tm, tk), lambda i, j, kk: (i, kk))
        b_spec = pl.BlockSpec((tn, tk), lambda i, j, kk: (j, kk))
    else:
        a_spec = pl.BlockSpec((tk, tm), lambda i, j, kk: (kk, i))
        b_spec = pl.BlockSpec((tk, tn), lambda i, j, kk: (kk, j))
    if b.ndim == 3:
        per = b.shape[2] // (tn if form == "nn" else tk)
        if form == "nn":
            b_spec = pl.BlockSpec((None, tk, tn), lambda i, j, kk: (j // per, kk, j % per))
        else:
            b_spec = pl.BlockSpec((None, tn, tk), lambda i, j, kk: (kk // per, j, kk % per))
    o_spec = pl.BlockSpec((tm, tn), lambda i, j, kk: (i, j))
    has_add = add is not None
    grid = (m // tm, n // tn, nk)
    n_in = 3 if has_add else 2
    c_in, c_out = (len(comm.ins), len(comm.outs)) if comm is not None else (0, 0)

    def body(*refs):
        a_ref, b_ref = refs[0], refs[1]
        add_ref = refs[2] if has_add else None
        o_ref = refs[n_in + c_in]
        pids = [pl.program_id(ax) for ax in range(3)]
        if comm is not None:
            comm_refs = (refs[n_in:n_in + c_in], refs[n_in + c_in + 1:n_in + c_in + 1 + c_out], refs[-2:])

            @pl.when(jnp.logical_and(jnp.logical_and(pids[0] == 0, pids[1] == 0), pids[2] == 0))
            def _():
                comm.start(*comm_refs)

        def finish(acc):
            if has_add:
                acc = acc + add_ref[...].astype(F32)
            o_ref[...] = acc.astype(o_ref.dtype)

        p = _dot(a_ref[...], b_ref[...], form)
        if nk == 1:
            finish(p)
        else:
            acc_ref = refs[n_in + c_in + 1 + c_out]

            @pl.when(pids[2] == 0)
            def _():
                acc_ref[...] = p

            @pl.when(pids[2] > 0)
            def _():
                acc_ref[...] += p

            @pl.when(pids[2] == nk - 1)
            def _():
                finish(acc_ref[...])

        if comm is not None:
            @pl.when(jnp.logical_and(jnp.logical_and(pids[0] == grid[0] - 1, pids[1] == grid[1] - 1),
                                     pids[2] == grid[2] - 1))
            def _():
                comm.finish(*comm_refs)

    acc_scratch = [pltpu.VMEM((tm, tn), F32)] if nk > 1 else []
    if comm is None:
        return pl.pallas_call(
            body, name=name, grid=grid, in_specs=[a_spec, b_spec] + ([o_spec] if has_add else []), out_specs=o_spec,
            out_shape=jax.ShapeDtypeStruct((m, n), out_dtype), scratch_shapes=acc_scratch,
            compiler_params=_params("parallel", "parallel", "arbitrary"),
        )(*((a, b, add) if has_add else (a, b)))
    outs = pl.pallas_call(
        body, name=name, grid=grid, in_specs=[a_spec, b_spec] + ([o_spec] if has_add else []) + [_ANY] * c_in,
        out_specs=[o_spec] + [_ANY] * c_out, out_shape=[jax.ShapeDtypeStruct((m, n), out_dtype)] + list(comm.outs),
        scratch_shapes=acc_scratch + _sem_pairs(comm.n_sems),
        input_output_aliases={n_in + i: 1 + o for i, o in comm.aliases.items()},
        compiler_params=_params("arbitrary", "arbitrary", "arbitrary"),
    )(*((a, b, add) if has_add else (a, b)), *comm.ins)
    return outs[0], list(outs[1:])


def _rms_fwd(name, x, g):
    t, d = x.shape
    tm = _tile(t, 512, 16)

    def body(x_ref, g_ref, o_ref):
        o_ref[...] = _rms(x_ref[...], g_ref[...]).astype(o_ref.dtype)

    return pl.pallas_call(
        body, name=name, grid=(t // tm,),
        in_specs=[pl.BlockSpec((tm, d), lambda i: (i, 0)), pl.BlockSpec((1, d), lambda i: (0, 0))],
        out_specs=pl.BlockSpec((tm, d), lambda i: (i, 0)),
        out_shape=jax.ShapeDtypeStruct((t, d), MXU_DTYPE), compiler_params=_params("parallel"),
    )(x, g.reshape(1, d))


def _rms_bwd(name, x, g, dh, dres=None):
    t, d = x.shape
    tm = _tile(t, 256, 16)
    has_res = dres is not None

    def body(*refs):
        x_ref, g_ref, dh_ref = refs[:3]
        dres_ref = refs[3] if has_res else None
        dx_ref, dxb_ref, dg_ref = refs[-3:]
        _, vjp = jax.vjp(_rms, x_ref[...], g_ref[...])
        dx, dg = vjp(dh_ref[...].astype(F32))
        if has_res:
            dx = dx + dres_ref[...]
        dx_ref[...] = dx
        dxb_ref[...] = dx.astype(dxb_ref.dtype)
        first = pl.program_id(0) == 0

        @pl.when(first)
        def _():
            dg_ref[...] = dg

        @pl.when(jnp.logical_not(first))
        def _():
            dg_ref[...] += dg

    row = pl.BlockSpec((tm, d), lambda i: (i, 0))
    vec = pl.BlockSpec((1, d), lambda i: (0, 0))
    dx, dxb, dg = pl.pallas_call(
        body, name=name, grid=(t // tm,),
        in_specs=[row, vec, row] + ([row] if has_res else []), out_specs=[row, row, vec],
        out_shape=[jax.ShapeDtypeStruct((t, d), F32), jax.ShapeDtypeStruct((t, d), MXU_DTYPE),
                   jax.ShapeDtypeStruct((1, d), F32)],
        compiler_params=_params("arbitrary"),
    )(*((x, g.reshape(1, d), dh) + ((dres,) if has_res else ())))
    return dx, dxb, dg.reshape(d)


def _final_loss(x, g, target):
    t, d = x.shape
    tm = _tile(t, 256, 16)

    def body(x_ref, g_ref, t_ref, loss_ref, dx_ref, dxb_ref, dg_ref):
        y, vjp = jax.vjp(_rms, x_ref[...], g_ref[...])
        err = y - t_ref[...]
        dx, dg = vjp(err * (1.0 / d))
        dx_ref[...] = dx
        dxb_ref[...] = dx.astype(dxb_ref.dtype)
        part = jnp.zeros((1, LANES), F32) + 0.5 * jnp.sum(jnp.mean(err * err, axis=-1, keepdims=True))
        first = pl.program_id(0) == 0

        @pl.when(first)
        def _():
            dg_ref[...] = dg
            loss_ref[...] = part

        @pl.when(jnp.logical_not(first))
        def _():
            dg_ref[...] += dg
            loss_ref[...] += part

    row = pl.BlockSpec((tm, d), lambda i: (i, 0))
    vec = pl.BlockSpec((1, d), lambda i: (0, 0))
    loss, dx, dxb, dg = pl.pallas_call(
        body, name="final_loss", grid=(t // tm,), in_specs=[row, vec, row],
        out_specs=[pl.BlockSpec((1, LANES), lambda i: (0, 0)), row, row, vec],
        out_shape=[jax.ShapeDtypeStruct((1, LANES), F32), jax.ShapeDtypeStruct((t, d), F32),
                   jax.ShapeDtypeStruct((t, d), MXU_DTYPE), jax.ShapeDtypeStruct((1, d), F32)],
        compiler_params=_params("arbitrary"),
    )(x, g.reshape(1, d), target)
    return loss, dx, dxb, dg.reshape(d)


def _conv_taps(x_ext, w, rows):
    kk = w.shape[0]
    y = x_ext[HALO:] * w[kk - 1:kk, :]
    for j in range(kk - 1):
        y = y + pltpu.roll(x_ext, kk - 1 - j, axis=0)[HALO:] * w[j:j + 1, :]
    return y


def _col_specs(tm, tn, col0, t_rows):
    assert col0 % tn == 0 and tm % HALO == 0
    c0 = col0 // tn
    per, last = tm // HALO, t_rows // HALO - 1
    tile = pl.BlockSpec((tm, tn), lambda j, i: (i, c0 + j))
    prev = pl.BlockSpec((HALO, tn), lambda j, i: (jnp.maximum(i * per - 1, 0), c0 + j))
    nxt = pl.BlockSpec((HALO, tn), lambda j, i: (jnp.minimum((i + 1) * per, last), c0 + j))
    return tile, prev, nxt


def _conv_fwd(name, xa, xa_col, w, w_col, ncols, out_dtype, xb=None, xb_col=0, gate=None, gate_col=0):
    t = xa.shape[0]
    kk = w.shape[0]
    tm, tn = _tile(t, EW_ROWS, HALO), _tile(ncols, EW_COLS)
    nrow = t // tm
    has_b, has_g = xb is not None, gate is not None

    def body(*refs):
        refs = list(refs)
        xa_ref, xap_ref = refs.pop(0), refs.pop(0)
        xb_ref, xbp_ref = (refs.pop(0), refs.pop(0)) if has_b else (None, None)
        w_ref = refs.pop(0)
        g_ref = refs.pop(0) if has_g else None
        o_ref = refs.pop(0)
        i = pl.program_id(1)
        x, xp = xa_ref[...].astype(F32), xap_ref[...].astype(F32)
        if has_b:
            x, xp = x * xb_ref[...].astype(F32), xp * xbp_ref[...].astype(F32)
        xp = jnp.where(i == 0, 0.0, xp)
        y = _conv_taps(jnp.concatenate([xp, x], axis=0), w_ref[...], tm)
        if has_g:
            y = y * g_ref[...].astype(F32)
        o_ref[...] = y.astype(o_ref.dtype)

    a_tile, a_prev, _ = _col_specs(tm, tn, xa_col, t)
    ins, specs = [xa, xa], [a_tile, a_prev]
    if has_b:
        b_tile, b_prev, _ = _col_specs(tm, tn, xb_col, t)
        ins, specs = ins + [xb, xb], specs + [b_tile, b_prev]
    assert w_col % tn == 0
    ins, specs = ins + [w], specs + [pl.BlockSpec((kk, tn), lambda j, i: (0, w_col // tn + j))]
    if has_g:
        ins, specs = ins + [gate], specs + [_col_specs(tm, tn, gate_col, t)[0]]
    return pl.pallas_call(
        body, name=name, grid=(ncols // tn, nrow), in_specs=specs,
        out_specs=pl.BlockSpec((tm, tn), lambda j, i: (i, j)),
        out_shape=jax.ShapeDtypeStruct((t, ncols), out_dtype), compiler_params=_params("parallel", "parallel"),
    )(*ins)


def _conv_bwd(name, xa, xa_col, w, w_col, dy, dy_col, ncols, dx_dtype, xb=None, xb_col=0, gate=None, gate_col=0):
    t = xa.shape[0]
    kk = w.shape[0]
    tm, tn = _tile(t, EW_ROWS, HALO), _tile(ncols, EW_COLS)
    nrow = t // tm
    has_b, has_g = xb is not None, gate is not None

    def body(*refs):
        refs = list(refs)
        xa_ref, xap_ref = refs.pop(0), refs.pop(0)
        xb_ref, xbp_ref = (refs.pop(0), refs.pop(0)) if has_b else (None, None)
        w_ref = refs.pop(0)
        dy_ref, dyn_ref = refs.pop(0), refs.pop(0)
        g_ref, gn_ref = (refs.pop(0), refs.pop(0)) if has_g else (None, None)
        dxa_ref = refs.pop(0)
        dxb_ref = refs.pop(0) if has_b else None
        dg_ref = refs.pop(0) if has_g else None
        dw_ref = refs.pop(0)
        i = pl.program_id(1)
        wv = w_ref[...]
        xa_t, xa_p = xa_ref[...].astype(F32), xap_ref[...].astype(F32)
        x, xp = xa_t, xa_p
        if has_b:
            xb_t = xb_ref[...].astype(F32)
            x, xp = x * xb_t, xp * xbp_ref[...].astype(F32)
        xp = jnp.where(i == 0, 0.0, xp)
        x_ext = jnp.concatenate([xp, x], axis=0)
        dyv, dyn = dy_ref[...].astype(F32), dyn_ref[...].astype(F32)
        if has_g:
            dg_ref[...] = (dyv * _conv_taps(x_ext, wv, tm)).astype(dg_ref.dtype)
            dyv, dyn = dyv * g_ref[...].astype(F32), dyn * gn_ref[...].astype(F32)
        dyn = jnp.where(i == nrow - 1, 0.0, dyn)
        dy_ext = jnp.concatenate([dyv, dyn], axis=0)
        dx = dyv * wv[kk - 1:kk, :]
        row8 = lax.broadcasted_iota(jnp.int32, (8, tn), 0)
        dw = jnp.where(row8 == kk - 1, jnp.sum(dyv * x, axis=0, keepdims=True), 0.0)
        for j in range(kk - 1):
            s = kk - 1 - j
            dx = dx + pltpu.roll(dy_ext, tm + HALO - s, axis=0)[:tm] * wv[j:j + 1, :]
            dwj = jnp.sum(dyv * pltpu.roll(x_ext, s, axis=0)[HALO:], axis=0, keepdims=True)
            dw = dw + jnp.where(row8 == j, dwj, 0.0)
        if has_b:
            dxa_ref[...] = (dx * xb_t).astype(dxa_ref.dtype)
            dxb_ref[...] = (dx * xa_t).astype(dxb_ref.dtype)
        else:
            dxa_ref[...] = dx.astype(dxa_ref.dtype)

        @pl.when(i == 0)
        def _():
            dw_ref[...] = dw

        @pl.when(i > 0)
        def _():
            dw_ref[...] += dw

    a_tile, a_prev, _ = _col_specs(tm, tn, xa_col, t)
    ins, specs = [xa, xa], [a_tile, a_prev]
    if has_b:
        b_tile, b_prev, _ = _col_specs(tm, tn, xb_col, t)
        ins, specs = ins + [xb, xb], specs + [b_tile, b_prev]
    assert w_col % tn == 0
    ins, specs = ins + [w], specs + [pl.BlockSpec((kk, tn), lambda j, i: (0, w_col // tn + j))]
    if dy.ndim == 3:
        nh, per, last = ncols // 2 // tn, tm // HALO, t // HALO - 1
        assert dy_col == 0 and nh * tn * 2 == ncols and not has_g, (name, dy.shape, tn)
        dy = dy.reshape(2 * t, ncols // 2)
        d_tile = pl.BlockSpec((tm, tn), lambda j, i: (j // nh * nrow + i, j % nh))
        d_next = pl.BlockSpec(
            (HALO, tn), lambda j, i: (j // nh * (last + 1) + jnp.minimum((i + 1) * per, last), j % nh))
    else:
        d_tile, _, d_next = _col_specs(tm, tn, dy_col, t)
    ins, specs = ins + [dy, dy], specs + [d_tile, d_next]
    if has_g:
        g_tile, _, g_next = _col_specs(tm, tn, gate_col, t)
        ins, specs = ins + [gate, gate], specs + [g_tile, g_next]
    out_tile = pl.BlockSpec((tm, tn), lambda j, i: (i, j))
    shapes, ospecs = [jax.ShapeDtypeStruct((t, ncols), dx_dtype)], [out_tile]
    if has_b:
        shapes, ospecs = shapes + [jax.ShapeDtypeStruct((t, ncols), dx_dtype)], ospecs + [out_tile]
    if has_g:
        shapes, ospecs = shapes + [jax.ShapeDtypeStruct((t, ncols), dx_dtype)], ospecs + [out_tile]
    shapes = shapes + [jax.ShapeDtypeStruct((8, ncols), F32)]
    ospecs = ospecs + [pl.BlockSpec((8, tn), lambda j, i: (0, j))]
    outs = list(pl.pallas_call(
        body, name=name, grid=(ncols // tn, nrow), in_specs=specs, out_specs=ospecs, out_shape=shapes,
        compiler_params=_params("parallel", "arbitrary"),
    )(*ins))
    dxa = outs.pop(0)
    dxb = outs.pop(0) if has_b else None
    dgate = outs.pop(0) if has_g else None
    return dxa, dxb, dgate, outs.pop(0)[:kk]


def _ffn_act_fwd(upre, w):
    t, f2 = upre.shape
    f, kk = f2 // 2, w.shape[0]
    tm, tn = _tile(t, EW_ROWS, HALO), _tile(f, EW_COLS // 2)
    nf = f // tn

    def body(g_ref, gp_ref, u_ref, up_ref, wg_ref, wu_ref, cg_ref, cu_ref, a_ref):
        first = pl.program_id(1) == 0

        def conv(x_ref, prev_ref, w_ref):
            prev = jnp.where(first, 0.0, prev_ref[...].astype(F32))
            return _conv_taps(jnp.concatenate([prev, x_ref[...].astype(F32)], axis=0), w_ref[...], tm)

        cg, cu = conv(g_ref, gp_ref, wg_ref), conv(u_ref, up_ref, wu_ref)
        cg_ref[...] = cg.astype(cg_ref.dtype)
        cu_ref[...] = cu.astype(cu_ref.dtype)
        a_ref[...] = (_silu(cg) * cu).astype(a_ref.dtype)

    g_tile, g_prev, _ = _col_specs(tm, tn, 0, t)
    u_tile, u_prev, _ = _col_specs(tm, tn, f, t)
    out = pl.BlockSpec((tm, tn), lambda j, i: (i, j))
    return pl.pallas_call(
        body, name="ffn_act_fwd", grid=(nf, t // tm),
        in_specs=[g_tile, g_prev, u_tile, u_prev, pl.BlockSpec((kk, tn), lambda j, i: (0, j)),
                  pl.BlockSpec((kk, tn), lambda j, i: (0, nf + j))],
        out_specs=[out, out, out], out_shape=[jax.ShapeDtypeStruct((t, f), MXU_DTYPE)] * 3,
        compiler_params=_params("parallel", "parallel"),
    )(upre, upre, upre, upre, w, w)


def _swiglu_bwd(ug, uu, da):
    t, f = ug.shape
    tm, tn = _tile(t, EW_ROWS, 16), _tile(f, EW_COLS)
    nf = f // tn

    def body(g_ref, u_ref, da_ref, o_ref):
        g, d = g_ref[...].astype(F32), da_ref[...].astype(F32)
        sg = _sigmoid(g)
        o_ref[0] = (d * u_ref[...].astype(F32) * (sg * (1.0 + g * (1.0 - sg)))).astype(o_ref.dtype)
        o_ref[1] = (d * (g * sg)).astype(o_ref.dtype)

    tile = pl.BlockSpec((tm, tn), lambda i, j: (i, j))
    return pl.pallas_call(
        body, name="swiglu_bwd", grid=(t // tm, nf), in_specs=[tile] * 3,
        out_specs=pl.BlockSpec((2, tm, tn), lambda i, j: (0, i, j)),
        out_shape=jax.ShapeDtypeStruct((2, t, f), MXU_DTYPE), compiler_params=_params("parallel", "parallel"),
    )(ug, uu, da)


def _gdn_prep(ops, qc, kc, vc, b_col, a_col, a_log, dt_bias):
    c, dh = qc.shape[-2:]
    q, k, v = _silu(qc), _silu(kc), _silu(vc)
    q = q * lax.rsqrt(jnp.sum(q * q, axis=-1, keepdims=True) + EPS) * (dh ** -0.5)
    k = k * lax.rsqrt(jnp.sum(k * k, axis=-1, keepdims=True) + EPS)
    beta = _sigmoid(b_col)
    g_col = -jnp.exp(a_log) * _softplus(a_col + dt_bias)
    r = lax.broadcasted_iota(jnp.int32, (c, c), 0)
    s = lax.broadcasted_iota(jnp.int32, (c, c), 1)
    g_row = jnp.sum(jnp.where(r == s, g_col, 0.0), axis=-2, keepdims=True)
    gc_col = jnp.sum(jnp.where(s <= r, g_row, 0.0), axis=-1, keepdims=True)
    gc_row = jnp.sum(jnp.where(r <= s, g_col, 0.0), axis=-2, keepdims=True)
    decay = jnp.exp(jnp.where(s <= r, gc_col - gc_row, -1e30))
    kb = k * beta
    a = jnp.where(s < r, ops.mm(kb, k, "nt") * decay, 0.0)
    tinv = ops.tri_inv(a)
    e_col = jnp.exp(gc_col)
    uw = ops.mmh(tinv, jnp.concatenate([v * beta, kb * e_col], axis=-1))
    u, w = uw[..., :dh], uw[..., dh:]
    attn = ops.mm(q, k, "nt") * decay
    g_last = jnp.sum(g_col, axis=-2, keepdims=True)
    return u, w, attn, q * e_col, k * jnp.exp(g_last - gc_col), g_last, tinv


def _gdn_step(ops, state, u, w, attn, q_dec, k_dec, g_last):
    v_new = u - ops.mm(w, state)
    o = ops.mm(q_dec, state) + ops.mm(attn, v_new)
    return o, state * jnp.exp(g_last) + ops.mm(k_dec, v_new, "tn")


PREP_HEADS, SCAN_HEADS = 4, 8


def _gdn_blocks(t, heads, hb_pref):
    tc = _tile(t, 256, CHUNK)
    hb = max(h for h in range(1, hb_pref + 1) if heads % h == 0)
    return tc, hb


def _to_chunks(ref, hb, dh):
    tc = ref.shape[0]
    return jnp.concatenate([ref[:, h * dh:(h + 1) * dh].astype(F32).reshape(tc // CHUNK, CHUNK, dh)
                            for h in range(hb)], axis=0)


def _from_chunks(ref, val, hb, dh):
    tc = ref.shape[0]
    ncb = tc // CHUNK
    for h in range(hb):
        ref[:, h * dh:(h + 1) * dh] = val[h * ncb:(h + 1) * ncb].reshape(tc, dh).astype(ref.dtype)


def _per_chunk(s, ncb):
    hb = s.shape[0]
    return jnp.broadcast_to(s[:, None], (hb, ncb, 1, 1)).reshape(hb * ncb, 1, 1)


def _gate_columns(pba, first_head, hb, heads):
    tc = pba.shape[0]
    lane = lax.broadcasted_iota(jnp.int32, pba.shape, 1)

    def pick(k):
        return jnp.sum(jnp.where(lane == k, pba, 0.0), axis=1, keepdims=True).reshape(tc // CHUNK, CHUNK, 1)

    return (jnp.concatenate([pick(first_head + h) for h in range(hb)], axis=0),
            jnp.concatenate([pick(heads + first_head + h) for h in range(hb)], axis=0))


def _gdn_prep_fwd(qkv, pba, a_log, dt_bias, heads, dh, comm=None):
    t = qkv.shape[0]
    tc, hb = _gdn_blocks(t, heads, PREP_HEADS)
    ncb, nhb, width = tc // CHUNK, heads // hb, heads * dh
    nc = t // CHUNK
    grid = (t // tc, nhb)
    c_in, c_out = (len(comm.ins), len(comm.outs)) if comm is not None else (0, 0)

    def body(*refs):
        q_ref, k_ref, v_ref, g_ref, al_ref, dt_ref = refs[:6]
        u_ref, w_ref, p_ref, qd_ref, kd_ref, gl_ref, ti_ref = refs[6 + c_in:13 + c_in]
        if comm is not None:
            comm_refs = (refs[6:6 + c_in], refs[13 + c_in:13 + c_in + c_out], refs[-2:])

            @pl.when(jnp.logical_and(pl.program_id(0) == 0, pl.program_id(1) == 0))
            def _():
                comm.start(*comm_refs)

        b_col, a_col = _gate_columns(g_ref[...], pl.program_id(1) * hb, hb, heads)
        u, w, p, qd, kd, gl, tinv = _gdn_prep(
            _PLAIN, _to_chunks(q_ref, hb, dh), _to_chunks(k_ref, hb, dh), _to_chunks(v_ref, hb, dh), b_col, a_col,
            _per_chunk(al_ref[...], ncb), _per_chunk(dt_ref[...], ncb))
        _from_chunks(u_ref, u, hb, dh)
        _from_chunks(w_ref, w, hb, dh)
        _from_chunks(qd_ref, qd, hb, dh)
        _from_chunks(kd_ref, kd, hb, dh)
        p_ref[...] = p.reshape(hb, tc, CHUNK).astype(p_ref.dtype)
        gl_ref[...] = gl.reshape(hb, ncb, 1, 1)
        ti_ref[...] = tinv.reshape(hb, tc, CHUNK)
        if comm is not None:
            @pl.when(jnp.logical_and(pl.program_id(0) == grid[0] - 1, pl.program_id(1) == grid[1] - 1))
            def _():
                comm.finish(*comm_refs)

    def tok(off):
        return pl.BlockSpec((tc, hb * dh), lambda i, j: (i, off * nhb + j))

    gate = pl.BlockSpec((tc, LANES), lambda i, j: (i, 0))
    scal = pl.BlockSpec((hb, 1, 1), lambda i, j: (j, 0, 0))
    square = pl.BlockSpec((hb, tc, CHUNK), lambda i, j: (j, i, 0))
    outs = pl.pallas_call(
        body, name="gdn_prep_fwd", grid=grid,
        in_specs=[tok(0), tok(1), tok(2), gate, scal, scal] + [_ANY] * c_in,
        out_specs=[tok(0), tok(0), square, tok(0), tok(0), pl.BlockSpec((hb, ncb, 1, 1), lambda i, j: (j, i, 0, 0)),
                   square] + [_ANY] * c_out,
        out_shape=[jax.ShapeDtypeStruct((t, width), F32), jax.ShapeDtypeStruct((t, width), MXU_DTYPE),
                   jax.ShapeDtypeStruct((heads, t, CHUNK), MXU_DTYPE), jax.ShapeDtypeStruct((t, width), MXU_DTYPE),
                   jax.ShapeDtypeStruct((t, width), MXU_DTYPE), jax.ShapeDtypeStruct((heads, nc, 1, 1), F32),
                   jax.ShapeDtypeStruct((heads, t, CHUNK), F32)] + (list(comm.outs) if comm is not None else []),
        scratch_shapes=_sem_pairs(comm.n_sems) if comm is not None else [],
        compiler_params=_params("arbitrary", "arbitrary") if comm is not None else _params("parallel", "parallel"),
    )(qkv, qkv, qkv, pba, a_log, dt_bias, *(comm.ins if comm is not None else ()))
    return tuple(outs[:6]), outs[6], list(outs[7:])


def _gdn_prep_bwd(qkv, pba, a_log, dt_bias, tinv, du, dw, dp, dqd, dkd, dgl, heads, dh):
    t = qkv.shape[0]
    tc, hb = _gdn_blocks(t, heads, PREP_HEADS)
    ncb, nhb, width = tc // CHUNK, heads // hb, heads * dh

    def body(q_ref, k_ref, v_ref, g_ref, al_ref, dt_ref, ti_ref, du_ref, dw_ref, dp_ref, dqd_ref, dkd_ref, dgl_ref,
             dq_ref, dk_ref, dv_ref, dg_ref, dal_ref, ddt_ref):
        first_head = pl.program_id(1) * hb
        b_col, a_col = _gate_columns(g_ref[...], first_head, hb, heads)
        ops = _Ops(True, ti_ref[...].reshape(hb * ncb, CHUNK, CHUNK))

        def prep(q, k, v, b, a, al, dt):
            return _gdn_prep(ops, q, k, v, b, a, _per_chunk(al, ncb), _per_chunk(dt, ncb))[:6]

        _, vjp = jax.vjp(prep, _to_chunks(q_ref, hb, dh), _to_chunks(k_ref, hb, dh), _to_chunks(v_ref, hb, dh),
                         b_col, a_col, al_ref[...], dt_ref[...])
        dq, dk, dv, db, da, dal, ddt = vjp((
            _to_chunks(du_ref, hb, dh), _to_chunks(dw_ref, hb, dh), dp_ref[...].reshape(hb * ncb, CHUNK, CHUNK),
            _to_chunks(dqd_ref, hb, dh), _to_chunks(dkd_ref, hb, dh), dgl_ref[...].reshape(hb * ncb, 1, 1)))
        _from_chunks(dq_ref, dq, hb, dh)
        _from_chunks(dk_ref, dk, hb, dh)
        _from_chunks(dv_ref, dv, hb, dh)
        dal_ref[...] = dal[None]
        ddt_ref[...] = ddt[None]
        lane = lax.broadcasted_iota(jnp.int32, (tc, LANES), 1)
        dgates = jnp.zeros((tc, LANES), F32)
        for h in range(hb):
            rows = slice(h * ncb, (h + 1) * ncb)
            dgates = dgates + jnp.where(lane == first_head + h, db[rows].reshape(tc, 1), 0.0) \
                + jnp.where(lane == heads + first_head + h, da[rows].reshape(tc, 1), 0.0)

        @pl.when(first_head == 0)
        def _():
            dg_ref[...] = dgates

        @pl.when(first_head > 0)
        def _():
            dg_ref[...] += dgates

    def tok(off):
        return pl.BlockSpec((tc, hb * dh), lambda i, j: (i, off * nhb + j))

    gate = pl.BlockSpec((tc, LANES), lambda i, j: (i, 0))
    scal = pl.BlockSpec((hb, 1, 1), lambda i, j: (j, 0, 0))
    part = pl.BlockSpec((1, hb, 1, 1), lambda i, j: (i, j, 0, 0))
    pspec = pl.BlockSpec((hb, tc, CHUNK), lambda i, j: (j, i, 0))
    glspec = pl.BlockSpec((hb, ncb, 1, 1), lambda i, j: (j, i, 0, 0))
    tokf = jax.ShapeDtypeStruct((t, width), F32)
    partf = jax.ShapeDtypeStruct((t // tc, heads, 1, 1), F32)
    return pl.pallas_call(
        body, name="gdn_prep_bwd", grid=(t // tc, nhb),
        in_specs=[tok(0), tok(1), tok(2), gate, scal, scal, pspec, tok(0), tok(0), pspec, tok(0), tok(0), glspec],
        out_specs=[tok(0), tok(0), tok(0), gate, part, part],
        out_shape=[tokf, tokf, tokf, jax.ShapeDtypeStruct((t, LANES), F32), partf, partf],
        compiler_params=_params("parallel", "arbitrary"),
    )(qkv, qkv, qkv, pba, a_log, dt_bias, tinv, du, dw, dp, dqd, dkd, dgl)


def _heads(ref, rows, hb, dh):
    return jnp.stack([ref[rows, h * dh:(h + 1) * dh].astype(F32) for h in range(hb)])


def _put_heads(ref, rows, val, dh):
    for h in range(val.shape[0]):
        ref[rows, h * dh:(h + 1) * dh] = val[h].astype(ref.dtype)


def _gdn_scan_fwd(u, w, p, qd, kd, gl, heads, dh):
    t = u.shape[0]
    tc, hb = _gdn_blocks(t, heads, SCAN_HEADS)
    ncb, nhb = tc // CHUNK, heads // hb
    nc = t // CHUNK

    def body(u_ref, w_ref, p_ref, qd_ref, kd_ref, gl_ref, o_ref, s_ref, state):
        @pl.when(pl.program_id(1) == 0)
        def _():
            state[...] = jnp.zeros_like(state)

        for c in range(ncb):
            rs = slice(c * CHUNK, (c + 1) * CHUNK)
            s_in = state[...]
            s_ref[:, c] = s_in
            o, s_out = _gdn_step(_PLAIN, s_in, _heads(u_ref, rs, hb, dh), _heads(w_ref, rs, hb, dh), p_ref[:, rs, :],
                                 _heads(qd_ref, rs, hb, dh), _heads(kd_ref, rs, hb, dh), gl_ref[:, c])
            _put_heads(o_ref, rs, o, dh)
            state[...] = s_out

    tok = pl.BlockSpec((tc, hb * dh), lambda j, i: (i, j))
    pspec = pl.BlockSpec((hb, tc, CHUNK), lambda j, i: (j, i, 0))
    glspec = pl.BlockSpec((hb, ncb, 1, 1), lambda j, i: (j, i, 0, 0))
    return pl.pallas_call(
        body, name="gdn_scan_fwd", grid=(nhb, t // tc),
        in_specs=[tok, tok, pspec, tok, tok, glspec],
        out_specs=[tok, pl.BlockSpec((hb, ncb, dh, dh), lambda j, i: (j, i, 0, 0))],
        out_shape=[jax.ShapeDtypeStruct((t, heads * dh), F32), jax.ShapeDtypeStruct((heads, nc, dh, dh), F32)],
        scratch_shapes=[pltpu.VMEM((hb, dh, dh), F32)],
        compiler_params=_params("arbitrary", "arbitrary"),
    )(u, w, p, qd, kd, gl)


def _gdn_scan_bwd(u, w, p, qd, kd, gl, states, do, heads, dh):
    t = u.shape[0]
    tc, hb = _gdn_blocks(t, heads, SCAN_HEADS)
    ncb, nhb = tc // CHUNK, heads // hb
    nc, nt = t // CHUNK, t // tc

    def body(u_ref, w_ref, p_ref, qd_ref, kd_ref, gl_ref, s_ref, do_ref,
             du_ref, dw_ref, dp_ref, dqd_ref, dkd_ref, dgl_ref, dstate):
        @pl.when(pl.program_id(1) == 0)
        def _():
            dstate[...] = jnp.zeros_like(dstate)

        for c in reversed(range(ncb)):
            rs = slice(c * CHUNK, (c + 1) * CHUNK)
            _, vjp = jax.vjp(functools.partial(_gdn_step, _DIFF), s_ref[:, c], _heads(u_ref, rs, hb, dh),
                             _heads(w_ref, rs, hb, dh), p_ref[:, rs, :].astype(F32), _heads(qd_ref, rs, hb, dh),
                             _heads(kd_ref, rs, hb, dh), gl_ref[:, c])
            ds, du, dw, dp, dqd, dkd, dgl = vjp((_heads(do_ref, rs, hb, dh), dstate[...]))
            dstate[...] = ds
            _put_heads(du_ref, rs, du, dh)
            _put_heads(dw_ref, rs, dw, dh)
            _put_heads(dqd_ref, rs, dqd, dh)
            _put_heads(dkd_ref, rs, dkd, dh)
            dp_ref[:, rs, :] = dp
            dgl_ref[:, c] = dgl

    tok = pl.BlockSpec((tc, hb * dh), lambda j, i: (nt - 1 - i, j))
    pspec = pl.BlockSpec((hb, tc, CHUNK), lambda j, i: (j, nt - 1 - i, 0))
    glspec = pl.BlockSpec((hb, ncb, 1, 1), lambda j, i: (j, nt - 1 - i, 0, 0))
    sspec = pl.BlockSpec((hb, ncb, dh, dh), lambda j, i: (j, nt - 1 - i, 0, 0))
    tokf = jax.ShapeDtypeStruct((t, heads * dh), F32)
    return pl.pallas_call(
        body, name="gdn_scan_bwd", grid=(nhb, nt),
        in_specs=[tok, tok, pspec, tok, tok, glspec, sspec, tok],
        out_specs=[tok, tok, pspec, tok, tok, glspec],
        out_shape=[tokf, tokf, jax.ShapeDtypeStruct((heads, t, CHUNK), F32), tokf, tokf,
                   jax.ShapeDtypeStruct((heads, nc, 1, 1), F32)],
        scratch_shapes=[pltpu.VMEM((hb, dh, dh), F32)],
        compiler_params=_params("arbitrary", "arbitrary"),
    )(u, w, p, qd, kd, gl, states, do)


def _gdn_post(o, z, gain):
    return _rms(o, gain) * _silu(z)


def _gdn_post_fwd(o, pm, z_col, gain, heads, dh):
    t, wid = o.shape
    tm = _tile(t, 256, 16)
    assert z_col % wid == 0

    def body(o_ref, z_ref, g_ref, y_ref):
        for h in range(heads):
            ls = slice(h * dh, (h + 1) * dh)
            y_ref[:, ls] = _gdn_post(o_ref[:, ls], z_ref[:, ls], g_ref[...]).astype(y_ref.dtype)

    blk = pl.BlockSpec((tm, wid), lambda i: (i, 0))
    return pl.pallas_call(
        body, name="gdn_post_fwd", grid=(t // tm,),
        in_specs=[blk, pl.BlockSpec((tm, wid), lambda i: (i, z_col // wid)), pl.BlockSpec((1, dh), lambda i: (0, 0))],
        out_specs=blk, out_shape=jax.ShapeDtypeStruct((t, wid), MXU_DTYPE), compiler_params=_params("parallel"),
    )(o, pm, gain.reshape(1, dh))


def _gdn_post_bwd(o, pm, z_col, gain, dy, heads, dh):
    t, wid = o.shape
    tm = _tile(t, 256, 16)
    assert z_col % wid == 0

    def body(o_ref, z_ref, g_ref, dy_ref, do_ref, dz_ref, dg_ref):
        dg = jnp.zeros((1, dh), F32)
        for h in range(heads):
            ls = slice(h * dh, (h + 1) * dh)
            _, vjp = jax.vjp(_gdn_post, o_ref[:, ls], z_ref[:, ls], g_ref[...])
            do, dz, dg_h = vjp(dy_ref[:, ls])
            do_ref[:, ls] = do
            dz_ref[:, ls] = dz.astype(dz_ref.dtype)
            dg = dg + dg_h
        first = pl.program_id(0) == 0

        @pl.when(first)
        def _():
            dg_ref[...] = dg

        @pl.when(jnp.logical_not(first))
        def _():
            dg_ref[...] += dg

    blk = pl.BlockSpec((tm, wid), lambda i: (i, 0))
    vec = pl.BlockSpec((1, dh), lambda i: (0, 0))
    do, dz, dg = pl.pallas_call(
        body, name="gdn_post_bwd", grid=(t // tm,),
        in_specs=[blk, pl.BlockSpec((tm, wid), lambda i: (i, z_col // wid)), vec, blk], out_specs=[blk, blk, vec],
        out_shape=[jax.ShapeDtypeStruct((t, wid), F32), jax.ShapeDtypeStruct((t, wid), MXU_DTYPE),
                   jax.ShapeDtypeStruct((1, dh), F32)],
        compiler_params=_params("arbitrary"),
    )(o, pm, gain.reshape(1, dh), dy)
    return do, dz, dg.reshape(dh)


def _attn(ops, q, kv):
    d = q.shape[1]
    hd = d // XATTN_HEADS
    outs = []
    for h in range(XATTN_HEADS):
        qh, kh, vh = q[:, h * hd:(h + 1) * hd], kv[:, h * hd:(h + 1) * hd], kv[:, d + h * hd:d + (h + 1) * hd]
        s = ops.mm(qh, kh, "nt") * (hd ** -0.5)
        e = jnp.exp(s - lax.stop_gradient(jnp.max(s, axis=-1, keepdims=True)))
        outs.append(ops.mm(e / jnp.sum(e, axis=-1, keepdims=True), vh))
    return jnp.concatenate(outs, axis=1)


def _attn_fwd(q, kv):
    t, d = q.shape
    nm = kv.shape[0]
    tm = _tile(t, 512, 16)

    def body(q_ref, kv_ref, o_ref):
        o_ref[...] = _attn(_PLAIN, q_ref[...], kv_ref[...]).astype(o_ref.dtype)

    return pl.pallas_call(
        body, name="xattn_fwd", grid=(t // tm,),
        in_specs=[pl.BlockSpec((tm, d), lambda i: (i, 0)), pl.BlockSpec((nm, 2 * d), lambda i: (0, 0))],
        out_specs=pl.BlockSpec((tm, d), lambda i: (i, 0)),
        out_shape=jax.ShapeDtypeStruct((t, d), MXU_DTYPE), compiler_params=_params("parallel"),
    )(q, kv)


def _attn_bwd(q, kv, do):
    t, d = q.shape
    nm = kv.shape[0]
    tm = _tile(t, 256, 16)

    def body(q_ref, kv_ref, do_ref, dq_ref, dkv_ref):
        _, vjp = jax.vjp(functools.partial(_attn, _DIFF), q_ref[...].astype(F32), kv_ref[...].astype(F32))
        dq, dkv = vjp(do_ref[...].astype(F32))
        dq_ref[...] = dq.astype(dq_ref.dtype)
        first = pl.program_id(0) == 0

        @pl.when(first)
        def _():
            dkv_ref[...] = dkv

        @pl.when(jnp.logical_not(first))
        def _():
            dkv_ref[...] += dkv

    row = pl.BlockSpec((tm, d), lambda i: (i, 0))
    full = pl.BlockSpec((nm, 2 * d), lambda i: (0, 0))
    return pl.pallas_call(
        body, name="xattn_bwd", grid=(t // tm,), in_specs=[row, full, row], out_specs=[row, full],
        out_shape=[jax.ShapeDtypeStruct((t, d), MXU_DTYPE), jax.ShapeDtypeStruct((nm, 2 * d), F32)],
        compiler_params=_params("arbitrary"),
    )(q, kv, do)


def _adamw(name, w, g, m, v):
    shape = w.shape
    lead, (rows, cols) = shape[:-2], shape[-2:]
    assert len(lead) <= 1, shape
    tr = _tile(rows, max(8, (1 << 18) // cols // 8 * 8), 8)

    def body(w_ref, g_ref, m_ref, v_ref, d_ref, nm_ref, nv_ref):
        gv = g_ref[...]
        nm = ADAM_B1 * m_ref[...] + (1.0 - ADAM_B1) * gv
        nv = ADAM_B2 * v_ref[...] + (1.0 - ADAM_B2) * jnp.square(gv)
        m_hat = nm / (1.0 - ADAM_B1 ** ADAM_STEP)
        v_hat = nv / (1.0 - ADAM_B2 ** ADAM_STEP)
        d_ref[...] = -ADAM_LR * (m_hat / (jnp.sqrt(v_hat) + ADAM_EPS) + ADAM_WD * w_ref[...])
        nm_ref[...] = nm
        nv_ref[...] = nv

    if lead:
        blk, grid = pl.BlockSpec((None, tr, cols), lambda l, i: (l, i, 0)), (lead[0], rows // tr)
        sem = ("parallel", "parallel")
    else:
        blk, grid, sem = pl.BlockSpec((tr, cols), lambda i: (i, 0)), (rows // tr,), ("parallel",)
    return pl.pallas_call(
        body, name=name, grid=grid, in_specs=[blk] * 4, out_specs=[blk] * 3,
        out_shape=[jax.ShapeDtypeStruct(shape, F32)] * 3, compiler_params=_params(*sem),
    )(w, g, m, v)


def _layer_fwd(x, mem, p, heads, dh, carry, late):
    wid = heads * dh
    sc = x.shape[1] - wid
    p, s, landed = dict(p), {"x0": x}, {}

    def arrived(name, brought):
        landed[name] = brought
        if name in late:
            p.update(late[name](brought))

    def mm(name, *args, **kwargs):
        if name not in carry:
            return _matmul(name, *args, **kwargs)
        out, brought = _matmul(name, *args, comm=carry[name], **kwargs)
        arrived(name, brought)
        return out

    s["h1"] = _rms_fwd("rms_mix", x, p["mix_norm"])
    s["pm"] = pm = mm("mm_mix_in", s["h1"], p["wmain"], "nn", F32)
    s["pba"] = mm("mm_mix_ba", s["h1"], p["wba"], "nn", F32)
    s["qkv"] = _conv_fwd("conv_gdn", pm, 0, p["gdn_conv"], 0, 3 * wid, F32)
    s["prep"], s["tinv"], brought = _gdn_prep_fwd(s["qkv"], s["pba"], p["a_log"], p["dt_bias"], heads, dh,
                                                  comm=carry.get("gdn_prep_fwd"))
    if brought:
        arrived("gdn_prep_fwd", brought)
    s["o"], s["states"] = _gdn_scan_fwd(*s["prep"], heads, dh)
    y_gdn = _gdn_post_fwd(s["o"], pm, 3 * wid, p["gdn_out_norm"], heads, dh)
    y_sc = _conv_fwd("conv_sc", pm, 4 * wid + sc, p["sc_conv"], 0, sc, MXU_DTYPE, xb=pm, xb_col=4 * wid + 2 * sc,
                     gate=pm, gate_col=4 * wid)
    s["ycat"] = jnp.concatenate([y_gdn, y_sc], axis=1)
    s["x1"] = x1 = mm("mm_mix_out", s["ycat"], p["wout"], "nn", F32, add=x)
    s["h2"] = _rms_fwd("rms_xattn", x1, p["xattn_norm"])
    s["q"] = mm("mm_xq", s["h2"], p["wq"], "nn", MXU_DTYPE)
    s["memn"] = _rms_fwd("rms_mem", mem, p["mem_norm"])
    s["kv"] = mm("mm_xkv", s["memn"], p["wkv"], "nn", MXU_DTYPE)
    s["ao"] = _attn_fwd(s["q"], s["kv"])
    s["x2"] = x2 = mm("mm_xo", s["ao"], p["wo"], "nn", F32, add=x1)
    s["h3"] = _rms_fwd("rms_ffn", x2, p["ffn_norm"])
    s["upre"] = mm("mm_ffn_up", s["h3"], p["wup"], "nn", MXU_DTYPE)
    s["ug"], s["uu"], s["act"] = _ffn_act_fwd(s["upre"], p["ffn_conv"])
    return mm("mm_ffn_down", s["act"], p["wdown"], "nn", F32, add=x2), s, landed, p


def _layer_bwd(dx3, dx3b, mem, s, p, heads, dh, reduce):
    wid = heads * dh
    sc = dx3.shape[1] - wid
    pm = s["pm"]
    g = {}

    mm = reduce.carried if reduce is not None else _matmul
    da = mm("mm_ffn_down_dx", dx3b, p["wdown"], "nt", MXU_DTYPE)
    g["wdown"] = mm("mm_ffn_down_dw", s["act"], dx3b, "tn", WIRE_DTYPE)
    du = _swiglu_bwd(s["ug"], s["uu"], da)
    dupre, _, _, g["ffn_conv"] = _conv_bwd("conv_ffn_bwd", s["upre"], 0, p["ffn_conv"], 0, du, 0,
                                           s["upre"].shape[1], MXU_DTYPE)
    dh3 = mm("mm_ffn_up_dx", dupre, p["wup"], "nt", F32)
    g["wup"] = mm("mm_ffn_up_dw", s["h3"], dupre, "tn", WIRE_DTYPE)
    dx2, dx2b, g["ffn_norm"] = _rms_bwd("rms_ffn_bwd", s["x2"], p["ffn_norm"], dh3, dx3)
    dao = mm("mm_xo_dx", dx2b, p["wo"], "nt", MXU_DTYPE)
    g["wo"] = mm("mm_xo_dw", s["ao"], dx2b, "tn", WIRE_DTYPE)
    dq, dkv = _attn_bwd(s["q"], s["kv"], dao)
    dh2 = mm("mm_xq_dx", dq, p["wq"], "nt", F32)
    g["wq"] = mm("mm_xq_dw", s["h2"], dq, "tn", WIRE_DTYPE)
    dmemn = mm("mm_xkv_dx", dkv, p["wkv"], "nt", F32)
    g["wkv"] = mm("mm_xkv_dw", s["memn"], dkv, "tn", WIRE_DTYPE)
    _, _, g["mem_norm"] = _rms_bwd("rms_mem_bwd", mem, p["mem_norm"], dmemn)
    dx1, dx1b, g["xattn_norm"] = _rms_bwd("rms_xattn_bwd", s["x1"], p["xattn_norm"], dh2, dx2)
    dycat = mm("mm_mix_out_dx", dx1b, p["wout"], "nt", F32)
    g["wout"] = mm("mm_mix_out_dw", s["ycat"], dx1b, "tn", WIRE_DTYPE)
    d_c, d_h, d_b, g["sc_conv"] = _conv_bwd("conv_sc_bwd", pm, 4 * wid + sc, p["sc_conv"], 0, dycat, wid, sc,
                                             MXU_DTYPE, xb=pm, xb_col=4 * wid + 2 * sc, gate=pm, gate_col=4 * wid)
    do, dz, g["gdn_out_norm"] = _gdn_post_bwd(s["o"], pm, 3 * wid, p["gdn_out_norm"], dycat, heads, dh)
    dprep = _gdn_scan_bwd(*s["prep"], s["states"], do, heads, dh)
    dqc, dkc, dvc, dpba, dal, ddt = _gdn_prep_bwd(s["qkv"], s["pba"], p["a_log"], p["dt_bias"], s["tinv"], *dprep,
                                                  heads, dh)
    g["a_log"], g["dt_bias"] = jnp.sum(dal, axis=0), jnp.sum(ddt, axis=0)
    dqkv, _, _, g["gdn_conv"] = _conv_bwd("conv_gdn_bwd", pm, 0, p["gdn_conv"], 0,
                                          jnp.concatenate([dqc, dkc, dvc], axis=1), 0, 3 * wid, MXU_DTYPE)
    dpm = jnp.concatenate([dqkv, dz, d_b, d_c, d_h], axis=1)
    dpba = dpba.astype(MXU_DTYPE)
    dh1 = mm("mm_mix_in_dx", dpm, p["wmain"], "nt", F32)
    dh1 = mm("mm_mix_ba_dx", dpba, p["wba"], "nt", F32, add=dh1)
    g["wmain"] = mm("mm_mix_in_dw", s["h1"], dpm, "tn", WIRE_DTYPE)
    g["wba"] = mm("mm_mix_ba_dw", s["h1"], dpba, "tn", WIRE_DTYPE)
    dx0, dx0b, g["mix_norm"] = _rms_bwd("rms_mix_bwd", s["x0"], p["mix_norm"], dh1, dx1)
    return dx0, dx0b, g


def _input_projection(win, heads, dh):
    wid = heads * dh
    return {"wmain": jnp.concatenate([win[:, :4 * wid], win[:, 4 * wid + 2 * heads:]], axis=1),
            "wba": jnp.pad(win[:, 4 * wid:4 * wid + 2 * heads], ((0, 0), (0, LANES - 2 * heads)))}


def _square_projections(wout, wq, wk, wv, wo, wdown):
    return {"wout": wout, "wq": wq, "wkv": jnp.concatenate([wk, wv], axis=1), "wo": wo, "wdown": wdown}


_ANY = pl.BlockSpec(memory_space=pl.ANY)
_VMEM = pl.BlockSpec(memory_space=pltpu.VMEM)


def _mesh_pos():
    return lax.axis_index("x"), lax.axis_index("y"), lax.axis_index("c")


def _other_chips(x, y):
    return [(1 - x, y), (x, 1 - y), (1 - x, 1 - y)]


def _push(src, dst, sems, k, to):
    return pltpu.make_async_remote_copy(src_ref=src, dst_ref=dst, send_sem=sems[0].at[k], recv_sem=sems[1].at[k],
                                        device_id=to, device_id_type=MESH)


def _sem_pairs(n):
    return [pltpu.SemaphoreType.DMA((n,)), pltpu.SemaphoreType.DMA((n,))]


class _Comm:
    def __init__(self, ins, outs, n_sems, start, finish, aliases=None):
        self.ins, self.outs, self.n_sems, self.start, self.finish = list(ins), list(outs), n_sems, start, finish
        self.aliases = aliases or {}


def _run_comm(name, comm):
    n_in, n_out = len(comm.ins), len(comm.outs)

    def body(*refs):
        parts = (refs[:n_in], refs[n_in:n_in + n_out], refs[n_in + n_out:])
        comm.start(*parts)
        comm.finish(*parts)

    return pl.pallas_call(
        body, name=name, in_specs=[_ANY] * n_in, out_specs=[_ANY] * n_out, out_shape=comm.outs,
        scratch_shapes=_sem_pairs(comm.n_sems), input_output_aliases=comm.aliases,
    )(*comm.ins)


def _allgather_comm(srcs):
    n = len(srcs)

    def first(src, out, sems):
        x, y, c = _mesh_pos()
        own, sends = [], []
        for t in range(n):
            half = src[t].shape[0] // 2
            mine = pl.ds(c * half, half)
            own.append(_push(src[t], out[t].at[2 * x + y], sems, 7 * t + 6, (x, y, 1 - c)))
            sends += [_push(src[t].at[mine], out[t].at[2 * x + y, mine], sems, 7 * t + k, (cx, cy, c))
                      for k, (cx, cy) in enumerate(_other_chips(x, y))]
        return own, sends

    def start(src, out, sems):
        own, sends = first(src, out, sems)
        for cp in own + sends:
            cp.start()

    def finish(src, out, sems):
        x, y, c = _mesh_pos()
        sibling = (x, y, 1 - c)
        own, sends = first(src, out, sems)
        fwds, relayed = [], []
        for t in range(n):
            half = src[t].shape[0] // 2
            for k, (cx, cy) in enumerate(_other_chips(x, y)):
                here = out[t].at[2 * cx + cy, pl.ds(c * half, half)]
                there = out[t].at[2 * cx + cy, pl.ds((1 - c) * half, half)]
                _push(here, here, sems, 7 * t + k, sibling).wait_recv()
                fwds.append(_push(here, here, sems, 7 * t + 3 + k, sibling))
                fwds[-1].start()
                relayed.append(_push(there, there, sems, 7 * t + 3 + k, sibling))
        for cp in relayed + own:
            cp.wait_recv()
        for cp in own + sends + fwds:
            cp.wait_send()

    return _Comm(srcs, [jax.ShapeDtypeStruct((N_CHIPS,) + s.shape, s.dtype) for s in srcs], 7 * n, start, finish)


def _start_wait(build):
    def start(src, out, sems):
        for cp in build(src, out, sems):
            cp.start()

    def finish(src, out, sems):
        for cp in build(src, out, sems):
            cp.wait()

    return start, finish


def _sibling_exchange_comm(bufs):
    def build(src, out, sems):
        x, y, c = _mesh_pos()
        return [_push(src[t].at[1 - c], out[t], sems, t, (x, y, 1 - c)) for t in range(len(bufs))]

    start, finish = _start_wait(build)
    return _Comm(bufs, [jax.ShapeDtypeStruct(b.shape[1:], b.dtype) for b in bufs], len(bufs), start, finish)


def _chip_exchange_comm(bufs):
    def build(src, out, sems):
        x, y, c = _mesh_pos()
        return [_push(src[t].at[2 * cx + cy], out[t].at[k], sems, 3 * t + k, (cx, cy, c))
                for t in range(len(bufs)) for k, (cx, cy) in enumerate(_other_chips(x, y))]

    start, finish = _start_wait(build)
    return _Comm(bufs, [jax.ShapeDtypeStruct((3,) + b.shape[1:], b.dtype) for b in bufs], 3 * len(bufs), start, finish)


def _sibling_share_comm(bufs):
    def build(src, out, sems):
        x, y, c = _mesh_pos()
        return [_push(src[t].at[c], out[t].at[c], sems, t, (x, y, 1 - c)) for t in range(len(bufs))]

    start, finish = _start_wait(build)
    return _Comm(bufs, [jax.ShapeDtypeStruct(b.shape, b.dtype) for b in bufs], len(bufs), start, finish,
                 aliases={t: t for t in range(len(bufs))})


def _allreduce_small(v):
    r, lanes = v.shape

    def body(v_ref, sum_ref, gath, send_sems, recv_sems):
        x, y, c = _mesh_pos()
        me = 4 * x + 2 * y + c
        gath[me] = v_ref[...]
        copies = []
        for rel in range(1, N_DEV):
            peer = tuple(1 - p if (rel >> b) & 1 else p for p, b in ((x, 2), (y, 1), (c, 0)))
            copies.append(pltpu.make_async_remote_copy(
                src_ref=v_ref, dst_ref=gath.at[me], send_sem=send_sems.at[rel - 1], recv_sem=recv_sems.at[rel - 1],
                device_id=peer, device_id_type=MESH))
        for cp in copies:
            cp.start()
        for cp in copies:
            cp.wait()
        total = gath[0]
        for k in range(1, N_DEV):
            total = total + gath[k]
        sum_ref[...] = total

    return pl.pallas_call(
        body, name="allreduce_small", in_specs=[_VMEM], out_specs=_VMEM,
        out_shape=jax.ShapeDtypeStruct((r, lanes), F32),
        scratch_shapes=[pltpu.VMEM((N_DEV, r, lanes), F32)] + _sem_pairs(N_DEV - 1),
        compiler_params=pltpu.CompilerParams(vmem_limit_bytes=VMEM_LIMIT),
    )(v)


def _sum_tile(rows, width):
    return _tile(rows, max(16, (1 << 19) // width // 16 * 16), 16)


def _sum_sibling(x, recv, core):
    _, n, w = x.shape
    tr = _sum_tile(n, w)

    def body(idx_ref, x_ref, r_ref, o_ref):
        o_ref[...] = (x_ref[...].astype(F32) + r_ref[...].astype(F32)).astype(o_ref.dtype)

    row = pl.BlockSpec((tr, w), lambda i, idx: (i, 0))
    return pl.pallas_call(
        body, name="rs_sum_sibling",
        grid_spec=pltpu.PrefetchScalarGridSpec(
            num_scalar_prefetch=1, grid=(n // tr,),
            in_specs=[pl.BlockSpec((None, tr, w), lambda i, idx: (idx[0], i, 0)), row], out_specs=row),
        out_shape=jax.ShapeDtypeStruct((n, w), x.dtype), compiler_params=_params("parallel"),
    )(core.reshape(1), x, recv)


def _sum_chips(s, recv, chip, core):
    _, m, w = s.shape
    tr = _sum_tile(m, w)

    def body(idx_ref, s_ref, r0_ref, r1_ref, r2_ref, o_ref):
        o_ref[...] = ((s_ref[...].astype(F32) + r0_ref[...].astype(F32)) + r1_ref[...].astype(F32)) \
            + r2_ref[...].astype(F32)

    def got(k):
        return pl.BlockSpec((None, tr, w), lambda i, idx: (k, i, 0))

    return pl.pallas_call(
        body, name="rs_sum_chips",
        grid_spec=pltpu.PrefetchScalarGridSpec(
            num_scalar_prefetch=1, grid=(m // tr,),
            in_specs=[pl.BlockSpec((None, tr, w), lambda i, idx: (idx[0], i, 0)), got(0), got(1), got(2)],
            out_specs=pl.BlockSpec((None, tr, w), lambda i, idx: (idx[1], i, 0))),
        out_shape=jax.ShapeDtypeStruct((2, m, w), F32), compiler_params=_params("parallel"),
    )(jnp.stack([chip, core]), s, recv, recv, recv)


_ROWS = ("w_mix_out", "w_xq", "w_xk", "w_xv", "w_xo", "w_ffn_down")
_CONVS = ("gdn_conv", "sc_conv", "ffn_conv")
_REPLICATED = ("mix_norm", "gdn_a_log", "gdn_dt_bias", "gdn_out_norm", "xattn_norm", "mem_norm", "ffn_norm",
               "final_norm")
_WEIGHTS = ("mix_norm", "w_mix_in", "gdn_conv", "gdn_a_log", "gdn_dt_bias", "gdn_out_norm", "sc_conv", "w_mix_out",
            "xattn_norm", "mem_norm", "w_xq", "w_xk", "w_xv", "w_xo", "ffn_norm", "w_ffn_up", "ffn_conv",
            "w_ffn_down", "final_norm")


def _pad_rows(flat, groups):
    unit = groups * 16 * LANES
    p = flat.shape[-1]
    pad = -p % unit
    if pad:
        flat = jnp.pad(flat, [(0, 0)] * (flat.ndim - 1) + [(0, pad)])
    return flat.reshape(flat.shape[:-1] + (groups, (p + pad) // (groups * LANES), LANES))


def _split_flat(flat, shapes):
    out, off = [], 0
    for shp in shapes:
        size = 1
        for n in shp:
            size *= n
        out.append(flat[..., off:off + size].reshape(flat.shape[:-1] + tuple(shp)))
        off += size
    return out


def _halves_by_chip(g):
    _, rows, w = g.shape
    return g.astype(WIRE_DTYPE).reshape(N_CHIPS, 2, rows // 2, w).transpose(1, 0, 2, 3)


def _halves_by_chip_columns(g):
    rows, cols = g.shape
    hr, n = rows // 2, cols // N_CHIPS
    return jnp.stack([jnp.stack([g[h * hr:(h + 1) * hr, j * n:(j + 1) * n] for j in range(N_CHIPS)])
                      for h in range(2)]).astype(WIRE_DTYPE)


class _ReduceScatter:
    STAGES = ("mm_ffn_down_dx", "mm_ffn_up_dx", "mm_ffn_up_dw", "mm_mix_in_dx")

    def __init__(self, bufs, chip, core):
        self.bufs, self.chip, self.core = list(bufs), chip, core
        self.sums = self.from_chips = self.reduced = self.result = None

    def comm(self, stage):
        if stage == self.STAGES[0]:
            return _sibling_exchange_comm(self.bufs)
        if stage == self.STAGES[1]:
            return _chip_exchange_comm(self.sums[-1:])
        if stage == self.STAGES[2]:
            return _chip_exchange_comm(self.sums[:-1])
        return _sibling_share_comm(self.reduced)

    def landed(self, stage, outs):
        if stage == self.STAGES[0]:
            self.sums = [_sum_sibling(b.reshape(2, -1, b.shape[-1]), r.reshape(-1, r.shape[-1]), self.core)
                         .reshape(r.shape) for b, r in zip(self.bufs, outs)]
        elif stage == self.STAGES[1]:
            self.from_chips = list(outs)
        elif stage == self.STAGES[2]:
            self.reduced = [_sum_chips(s, r, self.chip, self.core)
                            for s, r in zip(self.sums, list(outs) + self.from_chips)]
        else:
            self.result = list(outs)

    def carried(self, name, *args, **kwargs):
        if name not in self.STAGES:
            return _matmul(name, *args, **kwargs)
        out, outs = _matmul(name, *args, comm=self.comm(name), **kwargs)
        self.landed(name, outs)
        return out

    def run_alone(self):
        for stage, name in zip(self.STAGES, ("rs_sibling_exchange", "rs_chip_exchange_rows", "rs_chip_exchange_cols",
                                             "rs_sibling_share")):
            self.landed(stage, _run_comm(name, self.comm(stage)))


def kernel(x, mem, mix_norm, w_mix_in, gdn_conv, gdn_a_log, gdn_dt_bias, gdn_out_norm, sc_conv, w_mix_out, xattn_norm, mem_norm, w_xq, w_xk, w_xv, w_xo, ffn_norm, w_ffn_up, ffn_conv, w_ffn_down, final_norm, loss_target, m_mix_norm, m_w_mix_in, m_gdn_conv, m_gdn_a_log, m_gdn_dt_bias, m_gdn_out_norm, m_sc_conv, m_w_mix_out, m_xattn_norm, m_mem_norm, m_w_xq, m_w_xk, m_w_xv, m_w_xo, m_ffn_norm, m_w_ffn_up, m_ffn_conv, m_w_ffn_down, m_final_norm, v_mix_norm, v_w_mix_in, v_gdn_conv, v_gdn_a_log, v_gdn_dt_bias, v_gdn_out_norm, v_sc_conv, v_w_mix_out, v_xattn_norm, v_mem_norm, v_w_xq, v_w_xk, v_w_xv, v_w_xo, v_ffn_norm, v_w_ffn_up, v_ffn_conv, v_w_ffn_down, v_final_norm):
    w = dict(zip(_WEIGHTS, (mix_norm, w_mix_in, gdn_conv, gdn_a_log, gdn_dt_bias, gdn_out_norm, sc_conv, w_mix_out,
                            xattn_norm, mem_norm, w_xq, w_xk, w_xv, w_xo, ffn_norm, w_ffn_up, ffn_conv, w_ffn_down,
                            final_norm)))
    m = dict(zip(_WEIGHTS, (m_mix_norm, m_w_mix_in, m_gdn_conv, m_gdn_a_log, m_gdn_dt_bias, m_gdn_out_norm, m_sc_conv,
                            m_w_mix_out, m_xattn_norm, m_mem_norm, m_w_xq, m_w_xk, m_w_xv, m_w_xo, m_ffn_norm,
                            m_w_ffn_up, m_ffn_conv, m_w_ffn_down, m_final_norm)))
    v = dict(zip(_WEIGHTS, (v_mix_norm, v_w_mix_in, v_gdn_conv, v_gdn_a_log, v_gdn_dt_bias, v_gdn_out_norm, v_sc_conv,
                            v_w_mix_out, v_xattn_norm, v_mem_norm, v_w_xq, v_w_xk, v_w_xv, v_w_xo, v_ffn_norm,
                            v_w_ffn_up, v_ffn_conv, v_w_ffn_down, v_final_norm)))
    core = lax.axis_index("c")
    chip = 2 * lax.axis_index("x") + lax.axis_index("y")
    depth, heads = gdn_a_log.shape
    dh = gdn_out_norm.shape[1]
    d, wid = x.shape[2], heads * dh

    row_sizes = [w[n].shape[1] for n in _ROWS]
    row_offs = [sum(row_sizes[:k]) for k in range(len(_ROWS))]
    src_in, src_up = w_mix_in.astype(WIRE_DTYPE), w_ffn_up.astype(WIRE_DTYPE)
    src_rows = jnp.concatenate([w[n] for n in _ROWS], axis=1).astype(WIRE_DTYPE)
    src_convs = _pad_rows(jnp.concatenate([w[n].reshape(-1) for n in _CONVS]), 2)
    g_in, g_convs = _run_comm("allgather_first", _allgather_comm([src_in[0], src_convs]))
    conv_full = {n: jnp.moveaxis(part, 0, 2).reshape(depth, part.shape[2], -1)
                 for n, part in zip(_CONVS, _split_flat(g_convs.reshape(N_CHIPS, -1), [w[n].shape for n in _CONVS]))}
    side_by_side = lambda g: jnp.concatenate([g[j] for j in range(N_CHIPS)], axis=1)

    def from_rows(brought):
        g_rows, = brought
        return _square_projections(*[jnp.concatenate([g_rows[j, off:off + size] for j in range(N_CHIPS)], axis=0)
                                     for off, size in zip(row_offs, row_sizes)])

    xl, mem_l = x[0], mem[0]
    layers, saved, g_rows = [], [], None
    for l in range(depth):
        p = {"mix_norm": mix_norm[l], "xattn_norm": xattn_norm[l], "mem_norm": mem_norm[l], "ffn_norm": ffn_norm[l],
             "gdn_out_norm": gdn_out_norm[l], "a_log": gdn_a_log[l].reshape(heads, 1, 1),
             "dt_bias": gdn_dt_bias[l].reshape(heads, 1, 1), "gdn_conv": conv_full["gdn_conv"][l],
             "sc_conv": conv_full["sc_conv"][l], "ffn_conv": conv_full["ffn_conv"][l]}
        p.update(_input_projection(side_by_side(g_in), heads, dh))
        carry = {"gdn_prep_fwd": _allgather_comm([src_up[l]])}
        late = {"gdn_prep_fwd": lambda brought: {"wup": brought[0]}}
        if l == 0:
            carry["mm_mix_in"], late["mm_mix_in"] = _allgather_comm([src_rows[0]]), from_rows
        else:
            p.update(from_rows(g_rows))
        if l + 1 < depth:
            carry["mm_ffn_up"] = _allgather_comm([src_rows[l + 1]])
            carry["mm_ffn_down"] = _allgather_comm([src_in[l + 1]])
        xl, s, landed, p = _layer_fwd(xl, mem_l, p, heads, dh, carry, late)
        layers.append(p)
        saved.append(s)
        if l + 1 < depth:
            g_rows, (g_in,) = landed["mm_ffn_up"], landed["mm_ffn_down"]
    loss_row, dx, dxb, g_final = _final_loss(xl, final_norm, loss_target[0])

    def by_chip(g):
        g_win = jnp.concatenate([g["wmain"][:, :4 * wid], g["wba"][:, :2 * heads], g["wmain"][:, 4 * wid:]], axis=1)
        parts = (g["wout"], g["wq"], g["wkv"][:, :d], g["wkv"][:, d:], g["wo"], g["wdown"])
        by_rows = [p.reshape(N_CHIPS, p.shape[0] // N_CHIPS, p.shape[1]) for p in parts]
        return [_halves_by_chip_columns(g_win), _halves_by_chip_columns(g["wup"]),
                _halves_by_chip(jnp.concatenate(by_rows, axis=1))]

    per_layer, shards, reduce = [None] * depth, [None] * depth, None
    for l in reversed(range(depth)):
        dx, dxb, per_layer[l] = _layer_bwd(dx, dxb, mem_l, saved[l], layers[l], heads, dh, reduce)
        if reduce is not None:
            shards[l + 1] = reduce.result
        reduce = _ReduceScatter(by_chip(per_layer[l]), chip, core)
    reduce.run_alone()
    shards[0] = reduce.result
    by_layer = [[s[t].reshape(-1, s[t].shape[-1]) for s in shards] for t in range(3)]
    grad = {"w_mix_in": jnp.stack(by_layer[0]), "w_ffn_up": jnp.stack(by_layer[1])}
    for n, off, size in zip(_ROWS, row_offs, row_sizes):
        grad[n] = jnp.stack([r[off:off + size] for r in by_layer[2]])

    stack = lambda k: jnp.stack([g[k] for g in per_layer])
    small_g = {"mix_norm": stack("mix_norm"), "gdn_a_log": stack("a_log").reshape(depth, heads),
               "gdn_dt_bias": stack("dt_bias").reshape(depth, heads), "gdn_out_norm": stack("gdn_out_norm"),
               "xattn_norm": stack("xattn_norm"), "mem_norm": stack("mem_norm"), "ffn_norm": stack("ffn_norm"),
               "final_norm": g_final, "gdn_conv": stack("gdn_conv"), "sc_conv": stack("sc_conv"),
               "ffn_conv": stack("ffn_conv")}
    names = _REPLICATED + _CONVS
    small = jnp.concatenate([small_g[n].reshape(-1) for n in names] + [loss_row[0, :1]])
    small_sum = _allreduce_small(_pad_rows(small, 1)[0]).reshape(-1)
    parts = _split_flat(small_sum, [small_g[n].shape for n in names] + [(1,)])
    g_rep = dict(zip(_REPLICATED, parts[:len(_REPLICATED)]))
    for n, part in zip(_CONVS, parts[len(_REPLICATED):-1]):
        grad[n] = lax.dynamic_slice_in_dim(part, chip * w[n].shape[2], w[n].shape[2], axis=2)
    loss = parts[-1][0]

    delta, new_m, new_v = {}, {}, {}
    for n in ("w_mix_in", "w_ffn_up") + _ROWS + _CONVS:
        delta[n], new_m[n], new_v[n] = _adamw("adamw_" + n, w[n], grad[n], m[n], v[n])
    pack_rep = lambda t: _pad_rows(jnp.concatenate([t[n].reshape(-1) for n in _REPLICATED]), 1)[0]
    outs = _adamw("adamw_replicated", pack_rep(w), pack_rep(g_rep), pack_rep(m), pack_rep(v))
    shapes = [w[n].shape for n in _REPLICATED]
    for tgt, packed_out in zip((delta, new_m, new_v), outs):
        tgt.update(zip(_REPLICATED, _split_flat(packed_out.reshape(-1), shapes)))
    grad.update(g_rep)
    return (loss, dx[None], *[grad[n] for n in _WEIGHTS], *[delta[n] for n in _WEIGHTS],
            *[new_m[n] for n in _WEIGHTS], *[new_v[n] for n in _WEIGHTS])
```

```python
import functools

import jax
import jax.numpy as jnp
from jax import lax
from jax.experimental import pallas as pl
from jax.experimental.pallas import tpu as pltpu

F32 = jnp.float32
MXU_DTYPE = jnp.bfloat16
WIRE_DTYPE = jnp.bfloat16
SOLVE_PRECISION = lax.Precision.HIGH
EPS = 1e-6
CHUNK = 64
XATTN_HEADS = 4
LANES = 128
HALO = 16
EW_ROWS, EW_COLS = 256, 2816
VMEM_LIMIT = 52 * 1024 * 1024
ADAM_LR, ADAM_B1, ADAM_B2, ADAM_EPS, ADAM_WD, ADAM_STEP = 0.001, 0.9, 0.999, 1e-08, 0.01, 10
MESH = pl.DeviceIdType.MESH
N_CHIPS = 4
N_DEV = 8

_DIMS = {
    "nn": (((1,), (0,)), ((), ())),
    "nt": (((1,), (1,)), ((), ())),
    "tn": (((0,), (0,)), ((), ())),
}


def _tile(n, pref, align=LANES):
    if n <= pref:
        return n
    t = (pref // align) * align
    while t >= align:
        if n % t == 0:
            return t
        t -= align
    return n


def _params(*sem):
    return pltpu.CompilerParams(dimension_semantics=sem, vmem_limit_bytes=VMEM_LIMIT)


def _dot(a, b, form, hi=False):
    (ca, cb), _ = _DIMS[form]
    dims = (((ca[0] + 1,), (cb[0] + 1,)), ((0,), (0,))) if a.ndim == 3 else _DIMS[form]
    if hi:
        return lax.dot_general(a.astype(F32), b.astype(F32), dims, precision=SOLVE_PRECISION,
                               preferred_element_type=F32)
    return lax.dot_general(a.astype(MXU_DTYPE), b.astype(MXU_DTYPE), dims, preferred_element_type=F32)


@functools.partial(jax.custom_vjp, nondiff_argnums=(2, 3))
def _dot_d(a, b, form, hi):
    return _dot(a, b, form, hi)


def _dot_d_fwd(a, b, form, hi):
    return _dot(a, b, form, hi), (a, b)


def _dot_d_bwd(form, hi, res, g):
    a, b = res
    if form == "nn":
        da, db = _dot_d(g, b, "nt", hi), _dot_d(a, g, "tn", hi)
    elif form == "nt":
        da, db = _dot_d(g, b, "nn", hi), _dot_d(g, a, "tn", hi)
    else:
        da, db = _dot_d(b, g, "nt", hi), _dot_d(a, g, "nn", hi)
    return da.astype(a.dtype), db.astype(b.dtype)


_dot_d.defvjp(_dot_d_fwd, _dot_d_bwd)


def _tri_inv_impl(a, mmh):
    c = a.shape[-1]
    r = lax.broadcasted_iota(jnp.int32, (c, c), 0)
    s = lax.broadcasted_iota(jnp.int32, (c, c), 1)
    eye = (r == s).astype(F32)
    diag_blk = (r // 16) == (s // 16)
    d = jnp.where(diag_blk, a, 0.0)
    low = a - d
    d2 = mmh(d, d)
    d4 = mmh(d2, d2)
    d8 = mmh(d4, d4)
    td = mmh(mmh(mmh(eye - d, eye + d2), eye + d4), eye + d8)
    n = mmh(td, low)
    acc = eye - n
    p = n
    pw = 1
    while 2 * pw < c // 16:
        p = mmh(p, p)
        pw *= 2
        acc = mmh(acc, eye + p)
    return mmh(acc, td)


def _mmh_plain(a, b):
    return _dot(a, b, "nn", True)


@jax.custom_vjp
def _tri_inv_known(a, t):
    return t


def _tri_inv_known_fwd(a, t):
    return t, t


def _tri_inv_known_bwd(t, g):
    return -_dot(_dot(t, g, "tn", True), t, "nt", True), jnp.zeros_like(t)


_tri_inv_known.defvjp(_tri_inv_known_fwd, _tri_inv_known_bwd)


class _Ops:
    def __init__(self, diff, tinv=None):
        self.diff, self.tinv = diff, tinv

    def mm(self, a, b, form="nn"):
        return _dot_d(a, b, form, False) if self.diff else _dot(a, b, form, False)

    def mmh(self, a, b, form="nn"):
        return _dot_d(a, b, form, True) if self.diff else _dot(a, b, form, True)

    def tri_inv(self, a):
        return _tri_inv_known(a, self.tinv) if self.diff else _tri_inv_impl(a, _mmh_plain)


_PLAIN = _Ops(False)
_DIFF = _Ops(True)


def _sigmoid(x):
    return 1.0 / (1.0 + jnp.exp(-x))


def _silu(x):
    return x * _sigmoid(x)


def _softplus(x):
    return jnp.maximum(x, 0.0) + jnp.log(1.0 + jnp.exp(-jnp.abs(x)))


def _rms(x, g):
    return x * lax.rsqrt(jnp.mean(x * x, axis=-1, keepdims=True) + EPS) * g


def _matmul_tiles(m, n, k, form):
    if k <= 2048:
        return _tile(m, 1024), _tile(n, 1408), k
    if k <= 8192:
        return _tile(m, 1024 if form == "nn" else 512), _tile(n, 512), k
    return _tile(m, 1024), _tile(n, 1024), _tile(k, 2816)


def _matmul(name, a, b, form, out_dtype, add=None, comm=None):
    b_shape = b.shape if b.ndim == 2 else (b.shape[1], N_CHIPS * b.shape[2])
    if form == "nn":
        (m, k), (k2, n) = a.shape, b_shape
    elif form == "nt":
        (m, k), (n, k2) = a.shape, b_shape
    else:
        (k, m), (k2, n) = a.shape, b_shape
    assert k == k2, (name, a.shape, b.shape, form)
    tm, tn, tk = _matmul_tiles(m, n, k, form)
    if b.ndim == 3:
        assert form != "tn", name
        tn, tk = (_tile(b.shape[2], tn), tk) if form == "nn" else (tn, _tile(b.shape[2], tk))
    nk = k // tk
    out_bytes = tm * tn * (jnp.dtype(out_dtype).itemsize + (4 if add is not None else 0))
    vmem = 2 * (tm * tk * a.dtype.itemsize + tk * tn * b.dtype.itemsize + out_bytes) + (tm * tn * 4 if nk > 1 else 0)
    assert vmem <= VMEM_LIMIT, (name, tm, tn, tk, vmem)
    if form == "nn":
        a_spec = pl.BlockSpec((tm, tk), lambda i, j, kk: (i, kk))
        b_spec = pl.BlockSpec((tk, tn), lambda i, j, kk: (kk, j))
    elif form == "nt":
        a_spec = pl.BlockSpec((tm, tk), lambda i, j, kk: (i, kk))
        b_spec = pl.BlockSpec((tn, tk), lambda i, j, kk: (j, kk))
    else:
        a_spec = pl.BlockSpec((tk, tm), lambda i, j, kk: (kk, i))
        b_spec = pl.BlockSpec((tk, tn), lambda i, j, kk: (kk, j))
    if b.ndim == 3:
        per = b.shape[2] // (tn if form == "nn" else tk)
        if form == "nn":
            b_spec = pl.BlockSpec((None, tk, tn), lambda i, j, kk: (j // per, kk, j % per))
        else:
            b_spec = pl.BlockSpec((None, tn, tk), lambda i, j, kk: (kk // per, j, kk % per))
    o_spec = pl.BlockSpec((tm, tn), lambda i, j, kk: (i, j))
    has_add = add is not None
    grid = (m // tm, n // tn, nk)
    n_in = 3 if has_add else 2
    c_in, c_out = (len(comm.ins), len(comm.outs)) if comm is not None else (0, 0)

    def body(*refs):
        a_ref, b_ref = refs[0], refs[1]
        add_ref = refs[2] if has_add else None
        o_ref = refs[n_in + c_in]
        pids = [pl.program_id(ax) for ax in range(3)]
        if comm is not None:
            comm_refs = (refs[n_in:n_in + c_in], refs[n_in + c_in + 1:n_in + c_in + 1 + c_out], refs[-2:])

            @pl.when(jnp.logical_and(jnp.logical_and(pids[0] == 0, pids[1] == 0), pids[2] == 0))
            def _():
                comm.start(*comm_refs)

        def finish(acc):
            if has_add:
                acc = acc + add_ref[...].astype(F32)
            o_ref[...] = acc.astype(o_ref.dtype)

        p = _dot(a_ref[...], b_ref[...], form)
        if nk == 1:
            finish(p)
        else:
            acc_ref = refs[n_in + c_in + 1 + c_out]

            @pl.when(pids[2] == 0)
            def _():
                acc_ref[...] = p

            @pl.when(pids[2] > 0)
            def _():
                acc_ref[...] += p

            @pl.when(pids[2] == nk - 1)
            def _():
                finish(acc_ref[...])

        if comm is not None:
            @pl.when(jnp.logical_and(jnp.logical_and(pids[0] == grid[0] - 1, pids[1] == grid[1] - 1),
                                     pids[2] == grid[2] - 1))
            def _():
                comm.finish(*comm_refs)

    acc_scratch = [pltpu.VMEM((tm, tn), F32)] if nk > 1 else []
    if comm is None:
        return pl.pallas_call(
            body, name=name, grid=grid, in_specs=[a_spec, b_spec] + ([o_spec] if has_add else []), out_specs=o_spec,
            out_shape=jax.ShapeDtypeStruct((m, n), out_dtype), scratch_shapes=acc_scratch,
            compiler_params=_params("parallel", "parallel", "arbitrary"),
        )(*((a, b, add) if has_add else (a, b)))
    outs = pl.pallas_call(
        body, name=name, grid=grid, in_specs=[a_spec, b_spec] + ([o_spec] if has_add else []) + [_ANY] * c_in,
        out_specs=[o_spec] + [_ANY] * c_out, out_shape=[jax.ShapeDtypeStruct((m, n), out_dtype)] + list(comm.outs),
        scratch_shapes=acc_scratch + _sem_pairs(comm.n_sems),
        input_output_aliases={n_in + i: 1 + o for i, o in comm.aliases.items()},
        compiler_params=_params("arbitrary", "arbitrary", "arbitrary"),
    )(*((a, b, add) if has_add else (a, b)), *comm.ins)
    return outs[0], list(outs[1:])


def _rms_fwd(name, x, g):
    t, d = x.shape
    tm = _tile(t, 512, 16)

    def body(x_ref, g_ref, o_ref):
        o_ref[...] = _rms(x_ref[...], g_ref[...]).astype(o_ref.dtype)

    return pl.pallas_call(
        body, name=name, grid=(t // tm,),
        in_specs=[pl.BlockSpec((tm, d), lambda i: (i, 0)), pl.BlockSpec((1, d), lambda i: (0, 0))],
        out_specs=pl.BlockSpec((tm, d), lambda i: (i, 0)),
        out_shape=jax.ShapeDtypeStruct((t, d), MXU_DTYPE), compiler_params=_params("parallel"),
    )(x, g.reshape(1, d))


def _rms_bwd(name, x, g, dh, dres=None):
    t, d = x.shape
    tm = _tile(t, 256, 16)
    has_res = dres is not None

    def body(*refs):
        x_ref, g_ref, dh_ref = refs[:3]
        dres_ref = refs[3] if has_res else None
        dx_ref, dxb_ref, dg_ref = refs[-3:]
        _, vjp = jax.vjp(_rms, x_ref[...], g_ref[...])
        dx, dg = vjp(dh_ref[...].astype(F32))
        if has_res:
            dx = dx + dres_ref[...]
        dx_ref[...] = dx
        dxb_ref[...] = dx.astype(dxb_ref.dtype)
        first = pl.program_id(0) == 0

        @pl.when(first)
        def _():
            dg_ref[...] = dg

        @pl.when(jnp.logical_not(first))
        def _():
            dg_ref[...] += dg

    row = pl.BlockSpec((tm, d), lambda i: (i, 0))
    vec = pl.BlockSpec((1, d), lambda i: (0, 0))
    dx, dxb, dg = pl.pallas_call(
        body, name=name, grid=(t // tm,),
        in_specs=[row, vec, row] + ([row] if has_res else []), out_specs=[row, row, vec],
        out_shape=[jax.ShapeDtypeStruct((t, d), F32), jax.ShapeDtypeStruct((t, d), MXU_DTYPE),
                   jax.ShapeDtypeStruct((1, d), F32)],
        compiler_params=_params("arbitrary"),
    )(*((x, g.reshape(1, d), dh) + ((dres,) if has_res else ())))
    return dx, dxb, dg.reshape(d)


def _final_loss(x, g, target):
    t, d = x.shape
    tm = _tile(t, 256, 16)

    def body(x_ref, g_ref, t_ref, loss_ref, dx_ref, dxb_ref, dg_ref):
        y, vjp = jax.vjp(_rms, x_ref[...], g_ref[...])
        err = y - t_ref[...]
        dx, dg = vjp(err * (1.0 / d))
        dx_ref[...] = dx
        dxb_ref[...] = dx.astype(dxb_ref.dtype)
        part = jnp.zeros((1, LANES), F32) + 0.5 * jnp.sum(jnp.mean(err * err, axis=-1, keepdims=True))
        first = pl.program_id(0) == 0

        @pl.when(first)
        def _():
            dg_ref[...] = dg
            loss_ref[...] = part

        @pl.when(jnp.logical_not(first))
        def _():
            dg_ref[...] += dg
            loss_ref[...] += part

    row = pl.BlockSpec((tm, d), lambda i: (i, 0))
    vec = pl.BlockSpec((1, d), lambda i: (0, 0))
    loss, dx, dxb, dg = pl.pallas_call(
        body, name="final_loss", grid=(t // tm,), in_specs=[row, vec, row],
        out_specs=[pl.BlockSpec((1, LANES), lambda i: (0, 0)), row, row, vec],
        out_shape=[jax.ShapeDtypeStruct((1, LANES), F32), jax.ShapeDtypeStruct((t, d), F32),
                   jax.ShapeDtypeStruct((t, d), MXU_DTYPE), jax.ShapeDtypeStruct((1, d), F32)],
        compiler_params=_params("arbitrary"),
    )(x, g.reshape(1, d), target)
    return loss, dx, dxb, dg.reshape(d)


def _conv_taps(x_ext, w, rows):
    kk = w.shape[0]
    y = x_ext[HALO:] * w[kk - 1:kk, :]
    for j in range(kk - 1):
        y = y + pltpu.roll(x_ext, kk - 1 - j, axis=0)[HALO:] * w[j:j + 1, :]
    return y


def _col_specs(tm, tn, col0, t_rows):
    assert col0 % tn == 0 and tm % HALO == 0
    c0 = col0 // tn
    per, last = tm // HALO, t_rows // HALO - 1
    tile = pl.BlockSpec((tm, tn), lambda j, i: (i, c0 + j))
    prev = pl.BlockSpec((HALO, tn), lambda j, i: (jnp.maximum(i * per - 1, 0), c0 + j))
    nxt = pl.BlockSpec((HALO, tn), lambda j, i: (jnp.minimum((i + 1) * per, last), c0 + j))
    return tile, prev, nxt


def _conv_fwd(name, xa, xa_col, w, w_col, ncols, out_dtype, xb=None, xb_col=0, gate=None, gate_col=0):
    t = xa.shape[0]
    kk = w.shape[0]
    tm, tn = _tile(t, EW_ROWS, HALO), _tile(ncols, EW_COLS)
    nrow = t // tm
    has_b, has_g = xb is not None, gate is not None

    def body(*refs):
        refs = list(refs)
        xa_ref, xap_ref = refs.pop(0), refs.pop(0)
        xb_ref, xbp_ref = (refs.pop(0), refs.pop(0)) if has_b else (None, None)
        w_ref = refs.pop(0)
        g_ref = refs.pop(0) if has_g else None
        o_ref = refs.pop(0)
        i = pl.program_id(1)
        x, xp = xa_ref[...].astype(F32), xap_ref[...].astype(F32)
        if has_b:
            x, xp = x * xb_ref[...].astype(F32), xp * xbp_ref[...].astype(F32)
        xp = jnp.where(i == 0, 0.0, xp)
        y = _conv_taps(jnp.concatenate([xp, x], axis=0), w_ref[...], tm)
        if has_g:
            y = y * g_ref[...].astype(F32)
        o_ref[...] = y.astype(o_ref.dtype)

    a_tile, a_prev, _ = _col_specs(tm, tn, xa_col, t)
    ins, specs = [xa, xa], [a_tile, a_prev]
    if has_b:
        b_tile, b_prev, _ = _col_specs(tm, tn, xb_col, t)
        ins, specs = ins + [xb, xb], specs + [b_tile, b_prev]
    assert w_col % tn == 0
    ins, specs = ins + [w], specs + [pl.BlockSpec((kk, tn), lambda j, i: (0, w_col // tn + j))]
    if has_g:
        ins, specs = ins + [gate], specs + [_col_specs(tm, tn, gate_col, t)[0]]
    return pl.pallas_call(
        body, name=name, grid=(ncols // tn, nrow), in_specs=specs,
        out_specs=pl.BlockSpec((tm, tn), lambda j, i: (i, j)),
        out_shape=jax.ShapeDtypeStruct((t, ncols), out_dtype), compiler_params=_params("parallel", "parallel"),
    )(*ins)


def _conv_bwd(name, xa, xa_col, w, w_col, dy, dy_col, ncols, dx_dtype, xb=None, xb_col=0, gate=None, gate_col=0):
    t = xa.shape[0]
    kk = w.shape[0]
    tm, tn = _tile(t, EW_ROWS, HALO), _tile(ncols, EW_COLS)
    nrow = t // tm
    has_b, has_g = xb is not None, gate is not None

    def body(*refs):
        refs = list(refs)
        xa_ref, xap_ref = refs.pop(0), refs.pop(0)
        xb_ref, xbp_ref = (refs.pop(0), refs.pop(0)) if has_b else (None, None)
        w_ref = refs.pop(0)
        dy_ref, dyn_ref = refs.pop(0), refs.pop(0)
        g_ref, gn_ref = (refs.pop(0), refs.pop(0)) if has_g else (None, None)
        dxa_ref = refs.pop(0)
        dxb_ref = refs.pop(0) if has_b else None
        dg_ref = refs.pop(0) if has_g else None
        dw_ref = refs.pop(0)
        i = pl.program_id(1)
        wv = w_ref[...]
        xa_t, xa_p = xa_ref[...].astype(F32), xap_ref[...].astype(F32)
        x, xp = xa_t, xa_p
        if has_b:
            xb_t = xb_ref[...].astype(F32)
            x, xp = x * xb_t, xp * xbp_ref[...].astype(F32)
        xp = jnp.where(i == 0, 0.0, xp)
        x_ext = jnp.concatenate([xp, x], axis=0)
        dyv, dyn = dy_ref[...].astype(F32), dyn_ref[...].astype(F32)
        if has_g:
            dg_ref[...] = (dyv * _conv_taps(x_ext, wv, tm)).astype(dg_ref.dtype)
            dyv, dyn = dyv * g_ref[...].astype(F32), dyn * gn_ref[...].astype(F32)
        dyn = jnp.where(i == nrow - 1, 0.0, dyn)
        dy_ext = jnp.concatenate([dyv, dyn], axis=0)
        dx = dyv * wv[kk - 1:kk, :]
        row8 = lax.broadcasted_iota(jnp.int32, (8, tn), 0)
        dw = jnp.where(row8 == kk - 1, jnp.sum(dyv * x, axis=0, keepdims=True), 0.0)
        for j in range(kk - 1):
            s = kk - 1 - j
            dx = dx + pltpu.roll(dy_ext, tm + HALO - s, axis=0)[:tm] * wv[j:j + 1, :]
            dwj = jnp.sum(dyv * pltpu.roll(x_ext, s, axis=0)[HALO:], axis=0, keepdims=True)
            dw = dw + jnp.where(row8 == j, dwj, 0.0)
        if has_b:
            dxa_ref[...] = (dx * xb_t).astype(dxa_ref.dtype)
            dxb_ref[...] = (dx * xa_t).astype(dxb_ref.dtype)
        else:
            dxa_ref[...] = dx.astype(dxa_ref.dtype)

        @pl.when(i == 0)
        def _():
            dw_ref[...] = dw

        @pl.when(i > 0)
        def _():
            dw_ref[...] += dw

    a_tile, a_prev, _ = _col_specs(tm, tn, xa_col, t)
    ins, specs = [xa, xa], [a_tile, a_prev]
    if has_b:
        b_tile, b_prev, _ = _col_specs(tm, tn, xb_col, t)
        ins, specs = ins + [xb, xb], specs + [b_tile, b_prev]
    assert w_col % tn == 0
    ins, specs = ins + [w], specs + [pl.BlockSpec((kk, tn), lambda j, i: (0, w_col // tn + j))]
    if dy.ndim == 3:
        nh, per, last = ncols // 2 // tn, tm // HALO, t // HALO - 1
        assert dy_col == 0 and nh * tn * 2 == ncols and not has_g, (name, dy.shape, tn)
        dy = dy.reshape(2 * t, ncols // 2)
        d_tile = pl.BlockSpec((tm, tn), lambda j, i: (j // nh * nrow + i, j % nh))
        d_next = pl.BlockSpec(
            (HALO, tn), lambda j, i: (j // nh * (last + 1) + jnp.minimum((i + 1) * per, last), j % nh))
    else:
        d_tile, _, d_next = _col_specs(tm, tn, dy_col, t)
    ins, specs = ins + [dy, dy], specs + [d_tile, d_next]
    if has_g:
        g_tile, _, g_next = _col_specs(tm, tn, gate_col, t)
        ins, specs = ins + [gate, gate], specs + [g_tile, g_next]
    out_tile = pl.BlockSpec((tm, tn), lambda j, i: (i, j))
    shapes, ospecs = [jax.ShapeDtypeStruct((t, ncols), dx_dtype)], [out_tile]
    if has_b:
        shapes, ospecs = shapes + [jax.ShapeDtypeStruct((t, ncols), dx_dtype)], ospecs + [out_tile]
    if has_g:
        shapes, ospecs = shapes + [jax.ShapeDtypeStruct((t, ncols), dx_dtype)], ospecs + [out_tile]
    shapes = shapes + [jax.ShapeDtypeStruct((8, ncols), F32)]
    ospecs = ospecs + [pl.BlockSpec((8, tn), lambda j, i: (0, j))]
    outs = list(pl.pallas_call(
        body, name=name, grid=(ncols // tn, nrow), in_specs=specs, out_specs=ospecs, out_shape=shapes,
        compiler_params=_params("parallel", "arbitrary"),
    )(*ins))
    dxa = outs.pop(0)
    dxb = outs.pop(0) if has_b else None
    dgate = outs.pop(0) if has_g else None
    return dxa, dxb, dgate, outs.pop(0)[:kk]


def _ffn_act_fwd(upre, w):
    t, f2 = upre.shape
    f, kk = f2 // 2, w.shape[0]
    tm, tn = _tile(t, EW_ROWS, HALO), _tile(f, EW_COLS // 2)
    nf = f // tn

    def body(g_ref, gp_ref, u_ref, up_ref, wg_ref, wu_ref, cg_ref, cu_ref, a_ref):
        first = pl.program_id(1) == 0

        def conv(x_ref, prev_ref, w_ref):
            prev = jnp.where(first, 0.0, prev_ref[...].astype(F32))
            return _conv_taps(jnp.concatenate([prev, x_ref[...].astype(F32)], axis=0), w_ref[...], tm)

        cg, cu = conv(g_ref, gp_ref, wg_ref), conv(u_ref, up_ref, wu_ref)
        cg_ref[...] = cg.astype(cg_ref.dtype)
        cu_ref[...] = cu.astype(cu_ref.dtype)
        a_ref[...] = (_silu(cg) * cu).astype(a_ref.dtype)

    g_tile, g_prev, _ = _col_specs(tm, tn, 0, t)
    u_tile, u_prev, _ = _col_specs(tm, tn, f, t)
    out = pl.BlockSpec((tm, tn), lambda j, i: (i, j))
    return pl.pallas_call(
        body, name="ffn_act_fwd", grid=(nf, t // tm),
        in_specs=[g_tile, g_prev, u_tile, u_prev, pl.BlockSpec((kk, tn), lambda j, i: (0, j)),
                  pl.BlockSpec((kk, tn), lambda j, i: (0, nf + j))],
        out_specs=[out, out, out], out_shape=[jax.ShapeDtypeStruct((t, f), MXU_DTYPE)] * 3,
        compiler_params=_params("parallel", "parallel"),
    )(upre, upre, upre, upre, w, w)


def _swiglu_bwd(ug, uu, da):
    t, f = ug.shape
    tm, tn = _tile(t, EW_ROWS, 16), _tile(f, EW_COLS)
    nf = f // tn

    def body(g_ref, u_ref, da_ref, o_ref):
        g, d = g_ref[...].astype(F32), da_ref[...].astype(F32)
        sg = _sigmoid(g)
        o_ref[0] = (d * u_ref[...].astype(F32) * (sg * (1.0 + g * (1.0 - sg)))).astype(o_ref.dtype)
        o_ref[1] = (d * (g * sg)).astype(o_ref.dtype)

    tile = pl.BlockSpec((tm, tn), lambda i, j: (i, j))
    return pl.pallas_call(
        body, name="swiglu_bwd", grid=(t // tm, nf), in_specs=[tile] * 3,
        out_specs=pl.BlockSpec((2, tm, tn), lambda i, j: (0, i, j)),
        out_shape=jax.ShapeDtypeStruct((2, t, f), MXU_DTYPE), compiler_params=_params("parallel", "parallel"),
    )(ug, uu, da)


def _gdn_prep(ops, qc, kc, vc, b_col, a_col, a_log, dt_bias):
    c, dh = qc.shape[-2:]
    q, k, v = _silu(qc), _silu(kc), _silu(vc)
    q = q * lax.rsqrt(jnp.sum(q * q, axis=-1, keepdims=True) + EPS) * (dh ** -0.5)
    k = k * lax.rsqrt(jnp.sum(k * k, axis=-1, keepdims=True) + EPS)
    beta = _sigmoid(b_col)
    g_col = -jnp.exp(a_log) * _softplus(a_col + dt_bias)
    r = lax.broadcasted_iota(jnp.int32, (c, c), 0)
    s = lax.broadcasted_iota(jnp.int32, (c, c), 1)
    g_row = jnp.sum(jnp.where(r == s, g_col, 0.0), axis=-2, keepdims=True)
    gc_col = jnp.sum(jnp.where(s <= r, g_row, 0.0), axis=-1, keepdims=True)
    gc_row = jnp.sum(jnp.where(r <= s, g_col, 0.0), axis=-2, keepdims=True)
    decay = jnp.exp(jnp.where(s <= r, gc_col - gc_row, -1e30))
    kb = k * beta
    a = jnp.where(s < r, ops.mm(kb, k, "nt") * decay, 0.0)
    tinv = ops.tri_inv(a)
    e_col = jnp.exp(gc_col)
    uw = ops.mmh(tinv, jnp.concatenate([v * beta, kb * e_col], axis=-1))
    u, w = uw[..., :dh], uw[..., dh:]
    attn = ops.mm(q, k, "nt") * decay
    g_last = jnp.sum(g_col, axis=-2, keepdims=True)
    return u, w, attn, q * e_col, k * jnp.exp(g_last - gc_col), g_last, tinv


def _gdn_step(ops, state, u, w, attn, q_dec, k_dec, g_last):
    v_new = u - ops.mm(w, state)
    o = ops.mm(q_dec, state) + ops.mm(attn, v_new)
    return o, state * jnp.exp(g_last) + ops.mm(k_dec, v_new, "tn")


PREP_HEADS, SCAN_HEADS = 4, 8


def _gdn_blocks(t, heads, hb_pref):
    tc = _tile(t, 256, CHUNK)
    hb = max(h for h in range(1, hb_pref + 1) if heads % h == 0)
    return tc, hb


def _to_chunks(ref, hb, dh):
    tc = ref.shape[0]
    return jnp.concatenate([ref[:, h * dh:(h + 1) * dh].astype(F32).reshape(tc // CHUNK, CHUNK, dh)
                            for h in range(hb)], axis=0)


def _from_chunks(ref, val, hb, dh):
    tc = ref.shape[0]
    ncb = tc // CHUNK
    for h in range(hb):
        ref[:, h * dh:(h + 1) * dh] = val[h * ncb:(h + 1) * ncb].reshape(tc, dh).astype(ref.dtype)


def _per_chunk(s, ncb):
    hb = s.shape[0]
    return jnp.broadcast_to(s[:, None], (hb, ncb, 1, 1)).reshape(hb * ncb, 1, 1)


def _gate_columns(pba, first_head, hb, heads):
    tc = pba.shape[0]
    lane = lax.broadcasted_iota(jnp.int32, pba.shape, 1)

    def pick(k):
        return jnp.sum(jnp.where(lane == k, pba, 0.0), axis=1, keepdims=True).reshape(tc // CHUNK, CHUNK, 1)

    return (jnp.concatenate([pick(first_head + h) for h in range(hb)], axis=0),
            jnp.concatenate([pick(heads + first_head + h) for h in range(hb)], axis=0))


def _gdn_prep_fwd(qkv, pba, a_log, dt_bias, heads, dh, comm=None):
    t = qkv.shape[0]
    tc, hb = _gdn_blocks(t, heads, PREP_HEADS)
    ncb, nhb, width = tc // CHUNK, heads // hb, heads * dh
    nc = t // CHUNK
    grid = (t // tc, nhb)
    c_in, c_out = (len(comm.ins), len(comm.outs)) if comm is not None else (0, 0)

    def body(*refs):
        q_ref, k_ref, v_ref, g_ref, al_ref, dt_ref = refs[:6]
        u_ref, w_ref, p_ref, qd_ref, kd_ref, gl_ref, ti_ref = refs[6 + c_in:13 + c_in]
        if comm is not None:
            comm_refs = (refs[6:6 + c_in], refs[13 + c_in:13 + c_in + c_out], refs[-2:])

            @pl.when(jnp.logical_and(pl.program_id(0) == 0, pl.program_id(1) == 0))
            def _():
                comm.start(*comm_refs)

        b_col, a_col = _gate_columns(g_ref[...], pl.program_id(1) * hb, hb, heads)
        u, w, p, qd, kd, gl, tinv = _gdn_prep(
            _PLAIN, _to_chunks(q_ref, hb, dh), _to_chunks(k_ref, hb, dh), _to_chunks(v_ref, hb, dh), b_col, a_col,
            _per_chunk(al_ref[...], ncb), _per_chunk(dt_ref[...], ncb))
        _from_chunks(u_ref, u, hb, dh)
        _from_chunks(w_ref, w, hb, dh)
        _from_chunks(qd_ref, qd, hb, dh)
        _from_chunks(kd_ref, kd, hb, dh)
        p_ref[...] = p.reshape(hb, tc, CHUNK).astype(p_ref.dtype)
        gl_ref[...] = gl.reshape(hb, ncb, 1, 1)
        ti_ref[...] = tinv.reshape(hb, tc, CHUNK)
        if comm is not None:
            @pl.when(jnp.logical_and(pl.program_id(0) == grid[0] - 1, pl.program_id(1) == grid[1] - 1))
            def _():
                comm.finish(*comm_refs)

    def tok(off):
        return pl.BlockSpec((tc, hb * dh), lambda i, j: (i, off * nhb + j))

    gate = pl.BlockSpec((tc, LANES), lambda i, j: (i, 0))
    scal = pl.BlockSpec((hb, 1, 1), lambda i, j: (j, 0, 0))
    square = pl.BlockSpec((hb, tc, CHUNK), lambda i, j: (j, i, 0))
    outs = pl.pallas_call(
        body, name="gdn_prep_fwd", grid=grid,
        in_specs=[tok(0), tok(1), tok(2), gate, scal, scal] + [_ANY] * c_in,
        out_specs=[tok(0), tok(0), square, tok(0), tok(0), pl.BlockSpec((hb, ncb, 1, 1), lambda i, j: (j, i, 0, 0)),
                   square] + [_ANY] * c_out,
        out_shape=[jax.ShapeDtypeStruct((t, width), F32), jax.ShapeDtypeStruct((t, width), MXU_DTYPE),
                   jax.ShapeDtypeStruct((heads, t, CHUNK), MXU_DTYPE), jax.ShapeDtypeStruct((t, width), MXU_DTYPE),
                   jax.ShapeDtypeStruct((t, width), MXU_DTYPE), jax.ShapeDtypeStruct((heads, nc, 1, 1), F32),
                   jax.ShapeDtypeStruct((heads, t, CHUNK), F32)] + (list(comm.outs) if comm is not None else []),
        scratch_shapes=_sem_pairs(comm.n_sems) if comm is not None else [],
        compiler_params=_params("arbitrary", "arbitrary") if comm is not None else _params("parallel", "parallel"),
    )(qkv, qkv, qkv, pba, a_log, dt_bias, *(comm.ins if comm is not None else ()))
    return tuple(outs[:6]), outs[6], list(outs[7:])


def _gdn_prep_bwd(qkv, pba, a_log, dt_bias, tinv, du, dw, dp, dqd, dkd, dgl, heads, dh):
    t = qkv.shape[0]
    tc, hb = _gdn_blocks(t, heads, PREP_HEADS)
    ncb, nhb, width = tc // CHUNK, heads // hb, heads * dh

    def body(q_ref, k_ref, v_ref, g_ref, al_ref, dt_ref, ti_ref, du_ref, dw_ref, dp_ref, dqd_ref, dkd_ref, dgl_ref,
             dq_ref, dk_ref, dv_ref, dg_ref, dal_ref, ddt_ref):
        first_head = pl.program_id(1) * hb
        b_col, a_col = _gate_columns(g_ref[...], first_head, hb, heads)
        ops = _Ops(True, ti_ref[...].reshape(hb * ncb, CHUNK, CHUNK))

        def prep(q, k, v, b, a, al, dt):
            return _gdn_prep(ops, q, k, v, b, a, _per_chunk(al, ncb), _per_chunk(dt, ncb))[:6]

        _, vjp = jax.vjp(prep, _to_chunks(q_ref, hb, dh), _to_chunks(k_ref, hb, dh), _to_chunks(v_ref, hb, dh),
                         b_col, a_col, al_ref[...], dt_ref[...])
        dq, dk, dv, db, da, dal, ddt = vjp((
            _to_chunks(du_ref, hb, dh), _to_chunks(dw_ref, hb, dh), dp_ref[...].reshape(hb * ncb, CHUNK, CHUNK),
            _to_chunks(dqd_ref, hb, dh), _to_chunks(dkd_ref, hb, dh), dgl_ref[...].reshape(hb * ncb, 1, 1)))
        _from_chunks(dq_ref, dq, hb, dh)
        _from_chunks(dk_ref, dk, hb, dh)
        _from_chunks(dv_ref, dv, hb, dh)
        dal_ref[...] = dal[None]
        ddt_ref[...] = ddt[None]
        lane = lax.broadcasted_iota(jnp.int32, (tc, LANES), 1)
        dgates = jnp.zeros((tc, LANES), F32)
        for h in range(hb):
            rows = slice(h * ncb, (h + 1) * ncb)
            dgates = dgates + jnp.where(lane == first_head + h, db[rows].reshape(tc, 1), 0.0) \
                + jnp.where(lane == heads + first_head + h, da[rows].reshape(tc, 1), 0.0)

        @pl.when(first_head == 0)
        def _():
            dg_ref[...] = dgates

        @pl.when(first_head > 0)
        def _():
            dg_ref[...] += dgates

    def tok(off):
        return pl.BlockSpec((tc, hb * dh), lambda i, j: (i, off * nhb + j))

    gate = pl.BlockSpec((tc, LANES), lambda i, j: (i, 0))
    scal = pl.BlockSpec((hb, 1, 1), lambda i, j: (j, 0, 0))
    part = pl.BlockSpec((1, hb, 1, 1), lambda i, j: (i, j, 0, 0))
    pspec = pl.BlockSpec((hb, tc, CHUNK), lambda i, j: (j, i, 0))
    glspec = pl.BlockSpec((hb, ncb, 1, 1), lambda i, j: (j, i, 0, 0))
    tokf = jax.ShapeDtypeStruct((t, width), F32)
    partf = jax.ShapeDtypeStruct((t // tc, heads, 1, 1), F32)
    return pl.pallas_call(
        body, name="gdn_prep_bwd", grid=(t // tc, nhb),
        in_specs=[tok(0), tok(1), tok(2), gate, scal, scal, pspec, tok(0), tok(0), pspec, tok(0), tok(0), glspec],
        out_specs=[tok(0), tok(0), tok(0), gate, part, part],
        out_shape=[tokf, tokf, tokf, jax.ShapeDtypeStruct((t, LANES), F32), partf, partf],
        compiler_params=_params("parallel", "arbitrary"),
    )(qkv, qkv, qkv, pba, a_log, dt_bias, tinv, du, dw, dp, dqd, dkd, dgl)


def _heads(ref, rows, hb, dh):
    return jnp.stack([ref[rows, h * dh:(h + 1) * dh].astype(F32) for h in range(hb)])


def _put_heads(ref, rows, val, dh):
    for h in range(val.shape[0]):
        ref[rows, h * dh:(h + 1) * dh] = val[h].astype(ref.dtype)


def _gdn_scan_fwd(u, w, p, qd, kd, gl, heads, dh):
    t = u.shape[0]
    tc, hb = _gdn_blocks(t, heads, SCAN_HEADS)
    ncb, nhb = tc // CHUNK, heads // hb
    nc = t // CHUNK

    def body(u_ref, w_ref, p_ref, qd_ref, kd_ref, gl_ref, o_ref, s_ref, state):
        @pl.when(pl.program_id(1) == 0)
        def _():
            state[...] = jnp.zeros_like(state)

        for c in range(ncb):
            rs = slice(c * CHUNK, (c + 1) * CHUNK)
            s_in = state[...]
            s_ref[:, c] = s_in
            o, s_out = _gdn_step(_PLAIN, s_in, _heads(u_ref, rs, hb, dh), _heads(w_ref, rs, hb, dh), p_ref[:, rs, :],
                                 _heads(qd_ref, rs, hb, dh), _heads(kd_ref, rs, hb, dh), gl_ref[:, c])
            _put_heads(o_ref, rs, o, dh)
            state[...] = s_out

    tok = pl.BlockSpec((tc, hb * dh), lambda j, i: (i, j))
    pspec = pl.BlockSpec((hb, tc, CHUNK), lambda j, i: (j, i, 0))
    glspec = pl.BlockSpec((hb, ncb, 1, 1), lambda j, i: (j, i, 0, 0))
    return pl.pallas_call(
        body, name="gdn_scan_fwd", grid=(nhb, t // tc),
        in_specs=[tok, tok, pspec, tok, tok, glspec],
        out_specs=[tok, pl.BlockSpec((hb, ncb, dh, dh), lambda j, i: (j, i, 0, 0))],
        out_shape=[jax.ShapeDtypeStruct((t, heads * dh), F32), jax.ShapeDtypeStruct((heads, nc, dh, dh), F32)],
        scratch_shapes=[pltpu.VMEM((hb, dh, dh), F32)],
        compiler_params=_params("arbitrary", "arbitrary"),
    )(u, w, p, qd, kd, gl)


def _gdn_scan_bwd(u, w, p, qd, kd, gl, states, do, heads, dh):
    t = u.shape[0]
    tc, hb = _gdn_blocks(t, heads, SCAN_HEADS)
    ncb, nhb = tc // CHUNK, heads // hb
    nc, nt = t // CHUNK, t // tc

    def body(u_ref, w_ref, p_ref, qd_ref, kd_ref, gl_ref, s_ref, do_ref,
             du_ref, dw_ref, dp_ref, dqd_ref, dkd_ref, dgl_ref, dstate):
        @pl.when(pl.program_id(1) == 0)
        def _():
            dstate[...] = jnp.zeros_like(dstate)

        for c in reversed(range(ncb)):
            rs = slice(c * CHUNK, (c + 1) * CHUNK)
            _, vjp = jax.vjp(functools.partial(_gdn_step, _DIFF), s_ref[:, c], _heads(u_ref, rs, hb, dh),
                             _heads(w_ref, rs, hb, dh), p_ref[:, rs, :].astype(F32), _heads(qd_ref, rs, hb, dh),
                             _heads(kd_ref, rs, hb, dh), gl_ref[:, c])
            ds, du, dw, dp, dqd, dkd, dgl = vjp((_heads(do_ref, rs, hb, dh), dstate[...]))
            dstate[...] = ds
            _put_heads(du_ref, rs, du, dh)
            _put_heads(dw_ref, rs, dw, dh)
            _put_heads(dqd_ref, rs, dqd, dh)
            _put_heads(dkd_ref, rs, dkd, dh)
            dp_ref[:, rs, :] = dp
            dgl_ref[:, c] = dgl

    tok = pl.BlockSpec((tc, hb * dh), lambda j, i: (nt - 1 - i, j))
    pspec = pl.BlockSpec((hb, tc, CHUNK), lambda j, i: (j, nt - 1 - i, 0))
    glspec = pl.BlockSpec((hb, ncb, 1, 1), lambda j, i: (j, nt - 1 - i, 0, 0))
    sspec = pl.BlockSpec((hb, ncb, dh, dh), lambda j, i: (j, nt - 1 - i, 0, 0))
    tokf = jax.ShapeDtypeStruct((t, heads * dh), F32)
    return pl.pallas_call(
        body, name="gdn_scan_bwd", grid=(nhb, nt),
        in_specs=[tok, tok, pspec, tok, tok, glspec, sspec, tok],
        out_specs=[tok, tok, pspec, tok, tok, glspec],
        out_shape=[tokf, tokf, jax.ShapeDtypeStruct((heads, t, CHUNK), F32), tokf, tokf,
                   jax.ShapeDtypeStruct((heads, nc, 1, 1), F32)],
        scratch_shapes=[pltpu.VMEM((hb, dh, dh), F32)],
        compiler_params=_params("arbitrary", "arbitrary"),
    )(u, w, p, qd, kd, gl, states, do)


def _gdn_post(o, z, gain):
    return _rms(o, gain) * _silu(z)


def _gdn_post_fwd(o, pm, z_col, gain, heads, dh):
    t, wid = o.shape
    tm = _tile(t, 256, 16)
    assert z_col % wid == 0

    def body(o_ref, z_ref, g_ref, y_ref):
        for h in range(heads):
            ls = slice(h * dh, (h + 1) * dh)
            y_ref[:, ls] = _gdn_post(o_ref[:, ls], z_ref[:, ls], g_ref[...]).astype(y_ref.dtype)

    blk = pl.BlockSpec((tm, wid), lambda i: (i, 0))
    return pl.pallas_call(
        body, name="gdn_post_fwd", grid=(t // tm,),
        in_specs=[blk, pl.BlockSpec((tm, wid), lambda i: (i, z_col // wid)), pl.BlockSpec((1, dh), lambda i: (0, 0))],
        out_specs=blk, out_shape=jax.ShapeDtypeStruct((t, wid), MXU_DTYPE), compiler_params=_params("parallel"),
    )(o, pm, gain.reshape(1, dh))


def _gdn_post_bwd(o, pm, z_col, gain, dy, heads, dh):
    t, wid = o.shape
    tm = _tile(t, 256, 16)
    assert z_col % wid == 0

    def body(o_ref, z_ref, g_ref, dy_ref, do_ref, dz_ref, dg_ref):
        dg = jnp.zeros((1, dh), F32)
        for h in range(heads):
            ls = slice(h * dh, (h + 1) * dh)
            _, vjp = jax.vjp(_gdn_post, o_ref[:, ls], z_ref[:, ls], g_ref[...])
            do, dz, dg_h = vjp(dy_ref[:, ls])
            do_ref[:, ls] = do
            dz_ref[:, ls] = dz.astype(dz_ref.dtype)
            dg = dg + dg_h
        first = pl.program_id(0) == 0

        @pl.when(first)
        def _():
            dg_ref[...] = dg

        @pl.when(jnp.logical_not(first))
        def _():
            dg_ref[...] += dg

    blk = pl.BlockSpec((tm, wid), lambda i: (i, 0))
    vec = pl.BlockSpec((1, dh), lambda i: (0, 0))
    do, dz, dg = pl.pallas_call(
        body, name="gdn_post_bwd", grid=(t // tm,),
        in_specs=[blk, pl.BlockSpec((tm, wid), lambda i: (i, z_col // wid)), vec, blk], out_specs=[blk, blk, vec],
        out_shape=[jax.ShapeDtypeStruct((t, wid), F32), jax.ShapeDtypeStruct((t, wid), MXU_DTYPE),
                   jax.ShapeDtypeStruct((1, dh), F32)],
        compiler_params=_params("arbitrary"),
    )(o, pm, gain.reshape(1, dh), dy)
    return do, dz, dg.reshape(dh)


def _attn(ops, q, kv):
    d = q.shape[1]
    hd = d // XATTN_HEADS
    outs = []
    for h in range(XATTN_HEADS):
        qh, kh, vh = q[:, h * hd:(h + 1) * hd], kv[:, h * hd:(h + 1) * hd], kv[:, d + h * hd:d + (h + 1) * hd]
        s = ops.mm(qh, kh, "nt") * (hd ** -0.5)
        e = jnp.exp(s - lax.stop_gradient(jnp.max(s, axis=-1, keepdims=True)))
        outs.append(ops.mm(e / jnp.sum(e, axis=-1, keepdims=True), vh))
    return jnp.concatenate(outs, axis=1)


def _attn_fwd(q, kv):
    t, d = q.shape
    nm = kv.shape[0]
    tm = _tile(t, 512, 16)

    def body(q_ref, kv_ref, o_ref):
        o_ref[...] = _attn(_PLAIN, q_ref[...], kv_ref[...]).astype(o_ref.dtype)

    return pl.pallas_call(
        body, name="xattn_fwd", grid=(t // tm,),
        in_specs=[pl.BlockSpec((tm, d), lambda i: (i, 0)), pl.BlockSpec((nm, 2 * d), lambda i: (0, 0))],
        out_specs=pl.BlockSpec((tm, d), lambda i: (i, 0)),
        out_shape=jax.ShapeDtypeStruct((t, d), MXU_DTYPE), compiler_params=_params("parallel"),
    )(q, kv)


def _attn_bwd(q, kv, do):
    t, d = q.shape
    nm = kv.shape[0]
    tm = _tile(t, 256, 16)

    def body(q_ref, kv_ref, do_ref, dq_ref, dkv_ref):
        _, vjp = jax.vjp(functools.partial(_attn, _DIFF), q_ref[...].astype(F32), kv_ref[...].astype(F32))
        dq, dkv = vjp(do_ref[...].astype(F32))
        dq_ref[...] = dq.astype(dq_ref.dtype)
        first = pl.program_id(0) == 0

        @pl.when(first)
        def _():
            dkv_ref[...] = dkv

        @pl.when(jnp.logical_not(first))
        def _():
            dkv_ref[...] += dkv

    row = pl.BlockSpec((tm, d), lambda i: (i, 0))
    full = pl.BlockSpec((nm, 2 * d), lambda i: (0, 0))
    return pl.pallas_call(
        body, name="xattn_bwd", grid=(t // tm,), in_specs=[row, full, row], out_specs=[row, full],
        out_shape=[jax.ShapeDtypeStruct((t, d), MXU_DTYPE), jax.ShapeDtypeStruct((nm, 2 * d), F32)],
        compiler_params=_params("arbitrary"),
    )(q, kv, do)


def _adamw(name, w, g, m, v):
    shape = w.shape
    cols = shape[-1]
    rows = w.size // cols
    w2, g2, m2, v2 = (a.reshape(rows, cols) for a in (w, g, m, v))
    tr = _tile(rows, max(8, (1 << 18) // cols // 8 * 8), 8)

    def body(w_ref, g_ref, m_ref, v_ref, d_ref, nm_ref, nv_ref):
        gv = g_ref[...]
        nm = ADAM_B1 * m_ref[...] + (1.0 - ADAM_B1) * gv
        nv = ADAM_B2 * v_ref[...] + (1.0 - ADAM_B2) * jnp.square(gv)
        m_hat = nm / (1.0 - ADAM_B1 ** ADAM_STEP)
        v_hat = nv / (1.0 - ADAM_B2 ** ADAM_STEP)
        d_ref[...] = -ADAM_LR * (m_hat / (jnp.sqrt(v_hat) + ADAM_EPS) + ADAM_WD * w_ref[...])
        nm_ref[...] = nm
        nv_ref[...] = nv

    blk = pl.BlockSpec((tr, cols), lambda i: (i, 0))
    out = jax.ShapeDtypeStruct((rows, cols), F32)
    d, nm, nv = pl.pallas_call(
        body, name=name, grid=(rows // tr,), in_specs=[blk] * 4, out_specs=[blk] * 3, out_shape=[out] * 3,
        compiler_params=_params("parallel"),
    )(w2, g2, m2, v2)
    return d.reshape(shape), nm.reshape(shape), nv.reshape(shape)


def _layer_fwd(x, mem, p, heads, dh, carry, late):
    wid = heads * dh
    sc = x.shape[1] - wid
    p, s, landed = dict(p), {"x0": x}, {}

    def arrived(name, brought):
        landed[name] = brought
        if name in late:
            p.update(late[name](brought))

    def mm(name, *args, **kwargs):
        if name not in carry:
            return _matmul(name, *args, **kwargs)
        out, brought = _matmul(name, *args, comm=carry[name], **kwargs)
        arrived(name, brought)
        return out

    s["h1"] = _rms_fwd("rms_mix", x, p["mix_norm"])
    s["pm"] = pm = mm("mm_mix_in", s["h1"], p["wmain"], "nn", F32)
    s["pba"] = mm("mm_mix_ba", s["h1"], p["wba"], "nn", F32)
    s["qkv"] = _conv_fwd("conv_gdn", pm, 0, p["gdn_conv"], 0, 3 * wid, F32)
    s["prep"], s["tinv"], brought = _gdn_prep_fwd(s["qkv"], s["pba"], p["a_log"], p["dt_bias"], heads, dh,
                                                  comm=carry.get("gdn_prep_fwd"))
    if brought:
        arrived("gdn_prep_fwd", brought)
    s["o"], s["states"] = _gdn_scan_fwd(*s["prep"], heads, dh)
    y_gdn = _gdn_post_fwd(s["o"], pm, 3 * wid, p["gdn_out_norm"], heads, dh)
    y_sc = _conv_fwd("conv_sc", pm, 4 * wid + sc, p["sc_conv"], 0, sc, MXU_DTYPE, xb=pm, xb_col=4 * wid + 2 * sc,
                     gate=pm, gate_col=4 * wid)
    s["ycat"] = jnp.concatenate([y_gdn, y_sc], axis=1)
    s["x1"] = x1 = mm("mm_mix_out", s["ycat"], p["wout"], "nn", F32, add=x)
    s["h2"] = _rms_fwd("rms_xattn", x1, p["xattn_norm"])
    s["q"] = mm("mm_xq", s["h2"], p["wq"], "nn", MXU_DTYPE)
    s["memn"] = _rms_fwd("rms_mem", mem, p["mem_norm"])
    s["kv"] = mm("mm_xkv", s["memn"], p["wkv"], "nn", MXU_DTYPE)
    s["ao"] = _attn_fwd(s["q"], s["kv"])
    s["x2"] = x2 = mm("mm_xo", s["ao"], p["wo"], "nn", F32, add=x1)
    s["h3"] = _rms_fwd("rms_ffn", x2, p["ffn_norm"])
    s["upre"] = mm("mm_ffn_up", s["h3"], p["wup"], "nn", MXU_DTYPE)
    s["ug"], s["uu"], s["act"] = _ffn_act_fwd(s["upre"], p["ffn_conv"])
    return mm("mm_ffn_down", s["act"], p["wdown"], "nn", F32, add=x2), s, landed, p


def _layer_bwd(dx3, dx3b, mem, s, p, heads, dh, reduce):
    wid = heads * dh
    sc = dx3.shape[1] - wid
    pm = s["pm"]
    g = {}

    mm = reduce.carried if reduce is not None else _matmul
    da = mm("mm_ffn_down_dx", dx3b, p["wdown"], "nt", MXU_DTYPE)
    g["wdown"] = mm("mm_ffn_down_dw", s["act"], dx3b, "tn", WIRE_DTYPE)
    du = _swiglu_bwd(s["ug"], s["uu"], da)
    dupre, _, _, g["ffn_conv"] = _conv_bwd("conv_ffn_bwd", s["upre"], 0, p["ffn_conv"], 0, du, 0,
                                           s["upre"].shape[1], MXU_DTYPE)
    dh3 = mm("mm_ffn_up_dx", dupre, p["wup"], "nt", F32)
    g["wup"] = mm("mm_ffn_up_dw", s["h3"], dupre, "tn", WIRE_DTYPE)
    dx2, dx2b, g["ffn_norm"] = _rms_bwd("rms_ffn_bwd", s["x2"], p["ffn_norm"], dh3, dx3)
    dao = mm("mm_xo_dx", dx2b, p["wo"], "nt", MXU_DTYPE)
    g["wo"] = mm("mm_xo_dw", s["ao"], dx2b, "tn", WIRE_DTYPE)
    dq, dkv = _attn_bwd(s["q"], s["kv"], dao)
    dh2 = mm("mm_xq_dx", dq, p["wq"], "nt", F32)
    g["wq"] = mm("mm_xq_dw", s["h2"], dq, "tn", WIRE_DTYPE)
    dmemn = mm("mm_xkv_dx", dkv, p["wkv"], "nt", F32)
    g["wkv"] = mm("mm_xkv_dw", s["memn"], dkv, "tn", WIRE_DTYPE)
    _, _, g["mem_norm"] = _rms_bwd("rms_mem_bwd", mem, p["mem_norm"], dmemn)
    dx1, dx1b, g["xattn_norm"] = _rms_bwd("rms_xattn_bwd", s["x1"], p["xattn_norm"], dh2, dx2)
    dycat = mm("mm_mix_out_dx", dx1b, p["wout"], "nt", F32)
    g["wout"] = mm("mm_mix_out_dw", s["ycat"], dx1b, "tn", WIRE_DTYPE)
    d_c, d_h, d_b, g["sc_conv"] = _conv_bwd("conv_sc_bwd", pm, 4 * wid + sc, p["sc_conv"], 0, dycat, wid, sc,
                                             MXU_DTYPE, xb=pm, xb_col=4 * wid + 2 * sc, gate=pm, gate_col=4 * wid)
    do, dz, g["gdn_out_norm"] = _gdn_post_bwd(s["o"], pm, 3 * wid, p["gdn_out_norm"], dycat, heads, dh)
    dprep = _gdn_scan_bwd(*s["prep"], s["states"], do, heads, dh)
    dqc, dkc, dvc, dpba, dal, ddt = _gdn_prep_bwd(s["qkv"], s["pba"], p["a_log"], p["dt_bias"], s["tinv"], *dprep,
                                                  heads, dh)
    g["a_log"], g["dt_bias"] = jnp.sum(dal, axis=0), jnp.sum(ddt, axis=0)
    dqkv, _, _, g["gdn_conv"] = _conv_bwd("conv_gdn_bwd", pm, 0, p["gdn_conv"], 0,
                                          jnp.concatenate([dqc, dkc, dvc], axis=1), 0, 3 * wid, MXU_DTYPE)
    dpm = jnp.concatenate([dqkv, dz, d_b, d_c, d_h], axis=1)
    dpba = dpba.astype(MXU_DTYPE)
    dh1 = mm("mm_mix_in_dx", dpm, p["wmain"], "nt", F32)
    dh1 = mm("mm_mix_ba_dx", dpba, p["wba"], "nt", F32, add=dh1)
    g["wmain"] = mm("mm_mix_in_dw", s["h1"], dpm, "tn", WIRE_DTYPE)
    g["wba"] = mm("mm_mix_ba_dw", s["h1"], dpba, "tn", WIRE_DTYPE)
    dx0, dx0b, g["mix_norm"] = _rms_bwd("rms_mix_bwd", s["x0"], p["mix_norm"], dh1, dx1)
    return dx0, dx0b, g


def _input_projection(win, heads, dh):
    wid = heads * dh
    return {"wmain": jnp.concatenate([win[:, :4 * wid], win[:, 4 * wid + 2 * heads:]], axis=1),
            "wba": jnp.pad(win[:, 4 * wid:4 * wid + 2 * heads], ((0, 0), (0, LANES - 2 * heads)))}


def _square_projections(wout, wq, wk, wv, wo, wdown):
    return {"wout": wout, "wq": wq, "wkv": jnp.concatenate([wk, wv], axis=1), "wo": wo, "wdown": wdown}


_ANY = pl.BlockSpec(memory_space=pl.ANY)
_VMEM = pl.BlockSpec(memory_space=pltpu.VMEM)


def _mesh_pos():
    return lax.axis_index("x"), lax.axis_index("y"), lax.axis_index("c")


def _other_chips(x, y):
    return [(1 - x, y), (x, 1 - y), (1 - x, 1 - y)]


def _push(src, dst, sems, k, to):
    return pltpu.make_async_remote_copy(src_ref=src, dst_ref=dst, send_sem=sems[0].at[k], recv_sem=sems[1].at[k],
                                        device_id=to, device_id_type=MESH)


def _sem_pairs(n):
    return [pltpu.SemaphoreType.DMA((n,)), pltpu.SemaphoreType.DMA((n,))]


class _Comm:
    def __init__(self, ins, outs, n_sems, start, finish, aliases=None):
        self.ins, self.outs, self.n_sems, self.start, self.finish = list(ins), list(outs), n_sems, start, finish
        self.aliases = aliases or {}


def _run_comm(name, comm):
    n_in, n_out = len(comm.ins), len(comm.outs)

    def body(*refs):
        parts = (refs[:n_in], refs[n_in:n_in + n_out], refs[n_in + n_out:])
        comm.start(*parts)
        comm.finish(*parts)

    return pl.pallas_call(
        body, name=name, in_specs=[_ANY] * n_in, out_specs=[_ANY] * n_out, out_shape=comm.outs,
        scratch_shapes=_sem_pairs(comm.n_sems), input_output_aliases=comm.aliases,
    )(*comm.ins)


def _allgather_comm(srcs):
    n = len(srcs)

    def first(src, out, sems):
        x, y, c = _mesh_pos()
        own, sends = [], []
        for t in range(n):
            half = src[t].shape[0] // 2
            mine = pl.ds(c * half, half)
            own.append(_push(src[t], out[t].at[2 * x + y], sems, 7 * t + 6, (x, y, 1 - c)))
            sends += [_push(src[t].at[mine], out[t].at[2 * x + y, mine], sems, 7 * t + k, (cx, cy, c))
                      for k, (cx, cy) in enumerate(_other_chips(x, y))]
        return own, sends

    def start(src, out, sems):
        own, sends = first(src, out, sems)
        for cp in own + sends:
            cp.start()

    def finish(src, out, sems):
        x, y, c = _mesh_pos()
        sibling = (x, y, 1 - c)
        own, sends = first(src, out, sems)
        fwds, relayed = [], []
        for t in range(n):
            half = src[t].shape[0] // 2
            for k, (cx, cy) in enumerate(_other_chips(x, y)):
                here = out[t].at[2 * cx + cy, pl.ds(c * half, half)]
                there = out[t].at[2 * cx + cy, pl.ds((1 - c) * half, half)]
                _push(here, here, sems, 7 * t + k, sibling).wait_recv()
                fwds.append(_push(here, here, sems, 7 * t + 3 + k, sibling))
                fwds[-1].start()
                relayed.append(_push(there, there, sems, 7 * t + 3 + k, sibling))
        for cp in relayed + own:
            cp.wait_recv()
        for cp in own + sends + fwds:
            cp.wait_send()

    return _Comm(srcs, [jax.ShapeDtypeStruct((N_CHIPS,) + s.shape, s.dtype) for s in srcs], 7 * n, start, finish)


def _start_wait(build):
    def start(src, out, sems):
        for cp in build(src, out, sems):
            cp.start()

    def finish(src, out, sems):
        for cp in build(src, out, sems):
            cp.wait()

    return start, finish


def _sibling_exchange_comm(bufs):
    def build(src, out, sems):
        x, y, c = _mesh_pos()
        return [_push(src[t].at[1 - c], out[t], sems, t, (x, y, 1 - c)) for t in range(len(bufs))]

    start, finish = _start_wait(build)
    return _Comm(bufs, [jax.ShapeDtypeStruct(b.shape[1:], b.dtype) for b in bufs], len(bufs), start, finish)


def _chip_exchange_comm(bufs):
    def build(src, out, sems):
        x, y, c = _mesh_pos()
        return [_push(src[t].at[2 * cx + cy], out[t].at[k], sems, 3 * t + k, (cx, cy, c))
                for t in range(len(bufs)) for k, (cx, cy) in enumerate(_other_chips(x, y))]

    start, finish = _start_wait(build)
    return _Comm(bufs, [jax.ShapeDtypeStruct((3,) + b.shape[1:], b.dtype) for b in bufs], 3 * len(bufs), start, finish)


def _sibling_share_comm(bufs):
    def build(src, out, sems):
        x, y, c = _mesh_pos()
        return [_push(src[t].at[c], out[t].at[c], sems, t, (x, y, 1 - c)) for t in range(len(bufs))]

    start, finish = _start_wait(build)
    return _Comm(bufs, [jax.ShapeDtypeStruct(b.shape, b.dtype) for b in bufs], len(bufs), start, finish,
                 aliases={t: t for t in range(len(bufs))})


def _allreduce_small(v):
    r, lanes = v.shape

    def body(v_ref, sum_ref, gath, send_sems, recv_sems):
        x, y, c = _mesh_pos()
        me = 4 * x + 2 * y + c
        gath[me] = v_ref[...]
        copies = []
        for rel in range(1, N_DEV):
            peer = tuple(1 - p if (rel >> b) & 1 else p for p, b in ((x, 2), (y, 1), (c, 0)))
            copies.append(pltpu.make_async_remote_copy(
                src_ref=v_ref, dst_ref=gath.at[me], send_sem=send_sems.at[rel - 1], recv_sem=recv_sems.at[rel - 1],
                device_id=peer, device_id_type=MESH))
        for cp in copies:
            cp.start()
        for cp in copies:
            cp.wait()
        total = gath[0]
        for k in range(1, N_DEV):
            total = total + gath[k]
        sum_ref[...] = total

    return pl.pallas_call(
        body, name="allreduce_small", in_specs=[_VMEM], out_specs=_VMEM,
        out_shape=jax.ShapeDtypeStruct((r, lanes), F32),
        scratch_shapes=[pltpu.VMEM((N_DEV, r, lanes), F32)] + _sem_pairs(N_DEV - 1),
        compiler_params=pltpu.CompilerParams(vmem_limit_bytes=VMEM_LIMIT),
    )(v)


def _sum_tile(rows, width):
    return _tile(rows, max(16, (1 << 19) // width // 16 * 16), 16)


def _sum_sibling(x, recv, core):
    _, n, w = x.shape
    tr = _sum_tile(n, w)

    def body(idx_ref, x_ref, r_ref, o_ref):
        o_ref[...] = (x_ref[...].astype(F32) + r_ref[...].astype(F32)).astype(o_ref.dtype)

    row = pl.BlockSpec((tr, w), lambda i, idx: (i, 0))
    return pl.pallas_call(
        body, name="rs_sum_sibling",
        grid_spec=pltpu.PrefetchScalarGridSpec(
            num_scalar_prefetch=1, grid=(n // tr,),
            in_specs=[pl.BlockSpec((None, tr, w), lambda i, idx: (idx[0], i, 0)), row], out_specs=row),
        out_shape=jax.ShapeDtypeStruct((n, w), x.dtype), compiler_params=_params("parallel"),
    )(core.reshape(1), x, recv)


def _sum_chips(s, recv, chip, core):
    _, m, w = s.shape
    tr = _sum_tile(m, w)

    def body(idx_ref, s_ref, r0_ref, r1_ref, r2_ref, o_ref):
        o_ref[...] = ((s_ref[...].astype(F32) + r0_ref[...].astype(F32)) + r1_ref[...].astype(F32)) \
            + r2_ref[...].astype(F32)

    def got(k):
        return pl.BlockSpec((None, tr, w), lambda i, idx: (k, i, 0))

    return pl.pallas_call(
        body, name="rs_sum_chips",
        grid_spec=pltpu.PrefetchScalarGridSpec(
            num_scalar_prefetch=1, grid=(m // tr,),
            in_specs=[pl.BlockSpec((None, tr, w), lambda i, idx: (idx[0], i, 0)), got(0), got(1), got(2)],
            out_specs=pl.BlockSpec((None, tr, w), lambda i, idx: (idx[1], i, 0))),
        out_shape=jax.ShapeDtypeStruct((2, m, w), F32), compiler_params=_params("parallel"),
    )(jnp.stack([chip, core]), s, recv, recv, recv)


_ROWS = ("w_mix_out", "w_xq", "w_xk", "w_xv", "w_xo", "w_ffn_down")
_CONVS = ("gdn_conv", "sc_conv", "ffn_conv")
_REPLICATED = ("mix_norm", "gdn_a_log", "gdn_dt_bias", "gdn_out_norm", "xattn_norm", "mem_norm", "ffn_norm",
               "final_norm")
_WEIGHTS = ("mix_norm", "w_mix_in", "gdn_conv", "gdn_a_log", "gdn_dt_bias", "gdn_out_norm", "sc_conv", "w_mix_out",
            "xattn_norm", "mem_norm", "w_xq", "w_xk", "w_xv", "w_xo", "ffn_norm", "w_ffn_up", "ffn_conv",
            "w_ffn_down", "final_norm")


def _pad_rows(flat, groups):
    unit = groups * 16 * LANES
    p = flat.shape[-1]
    pad = -p % unit
    if pad:
        flat = jnp.pad(flat, [(0, 0)] * (flat.ndim - 1) + [(0, pad)])
    return flat.reshape(flat.shape[:-1] + (groups, (p + pad) // (groups * LANES), LANES))


def _split_flat(flat, shapes):
    out, off = [], 0
    for shp in shapes:
        size = 1
        for n in shp:
            size *= n
        out.append(flat[..., off:off + size].reshape(flat.shape[:-1] + tuple(shp)))
        off += size
    return out


def _halves_by_chip(g):
    _, rows, w = g.shape
    return g.astype(WIRE_DTYPE).reshape(N_CHIPS, 2, rows // 2, w).transpose(1, 0, 2, 3)


def _by_chip_columns(g):
    rows, cols = g.shape
    return g.reshape(rows, N_CHIPS, cols // N_CHIPS).transpose(1, 0, 2)


class _ReduceScatter:
    STAGES = ("mm_ffn_down_dx", "mm_ffn_up_dx", "mm_ffn_up_dw", "mm_mix_in_dx")

    def __init__(self, bufs, chip, core):
        self.bufs, self.chip, self.core = list(bufs), chip, core
        self.sums = self.from_chips = self.reduced = self.result = None

    def comm(self, stage):
        if stage == self.STAGES[0]:
            return _sibling_exchange_comm(self.bufs)
        if stage == self.STAGES[1]:
            return _chip_exchange_comm(self.sums[-1:])
        if stage == self.STAGES[2]:
            return _chip_exchange_comm(self.sums[:-1])
        return _sibling_share_comm(self.reduced)

    def landed(self, stage, outs):
        if stage == self.STAGES[0]:
            self.sums = [_sum_sibling(b.reshape(2, -1, b.shape[-1]), r.reshape(-1, r.shape[-1]), self.core)
                         .reshape(r.shape) for b, r in zip(self.bufs, outs)]
        elif stage == self.STAGES[1]:
            self.from_chips = list(outs)
        elif stage == self.STAGES[2]:
            self.reduced = [_sum_chips(s, r, self.chip, self.core)
                            for s, r in zip(self.sums, list(outs) + self.from_chips)]
        else:
            self.result = list(outs)

    def carried(self, name, *args, **kwargs):
        if name not in self.STAGES:
            return _matmul(name, *args, **kwargs)
        out, outs = _matmul(name, *args, comm=self.comm(name), **kwargs)
        self.landed(name, outs)
        return out

    def run_alone(self):
        for stage, name in zip(self.STAGES, ("rs_sibling_exchange", "rs_chip_exchange_rows", "rs_chip_exchange_cols",
                                             "rs_sibling_share")):
            self.landed(stage, _run_comm(name, self.comm(stage)))


def kernel(x, mem, mix_norm, w_mix_in, gdn_conv, gdn_a_log, gdn_dt_bias, gdn_out_norm, sc_conv, w_mix_out, xattn_norm, mem_norm, w_xq, w_xk, w_xv, w_xo, ffn_norm, w_ffn_up, ffn_conv, w_ffn_down, final_norm, loss_target, m_mix_norm, m_w_mix_in, m_gdn_conv, m_gdn_a_log, m_gdn_dt_bias, m_gdn_out_norm, m_sc_conv, m_w_mix_out, m_xattn_norm, m_mem_norm, m_w_xq, m_w_xk, m_w_xv, m_w_xo, m_ffn_norm, m_w_ffn_up, m_ffn_conv, m_w_ffn_down, m_final_norm, v_mix_norm, v_w_mix_in, v_gdn_conv, v_gdn_a_log, v_gdn_dt_bias, v_gdn_out_norm, v_sc_conv, v_w_mix_out, v_xattn_norm, v_mem_norm, v_w_xq, v_w_xk, v_w_xv, v_w_xo, v_ffn_norm, v_w_ffn_up, v_ffn_conv, v_w_ffn_down, v_final_norm):
    w = dict(zip(_WEIGHTS, (mix_norm, w_mix_in, gdn_conv, gdn_a_log, gdn_dt_bias, gdn_out_norm, sc_conv, w_mix_out,
                            xattn_norm, mem_norm, w_xq, w_xk, w_xv, w_xo, ffn_norm, w_ffn_up, ffn_conv, w_ffn_down,
                            final_norm)))
    m = dict(zip(_WEIGHTS, (m_mix_norm, m_w_mix_in, m_gdn_conv, m_gdn_a_log, m_gdn_dt_bias, m_gdn_out_norm, m_sc_conv,
                            m_w_mix_out, m_xattn_norm, m_mem_norm, m_w_xq, m_w_xk, m_w_xv, m_w_xo, m_ffn_norm,
                            m_w_ffn_up, m_ffn_conv, m_w_ffn_down, m_final_norm)))
    v = dict(zip(_WEIGHTS, (v_mix_norm, v_w_mix_in, v_gdn_conv, v_gdn_a_log, v_gdn_dt_bias, v_gdn_out_norm, v_sc_conv,
                            v_w_mix_out, v_xattn_norm, v_mem_norm, v_w_xq, v_w_xk, v_w_xv, v_w_xo, v_ffn_norm,
                            v_w_ffn_up, v_ffn_conv, v_w_ffn_down, v_final_norm)))
    core = lax.axis_index("c")
    chip = 2 * lax.axis_index("x") + lax.axis_index("y")
    depth, heads = gdn_a_log.shape
    dh = gdn_out_norm.shape[1]
    d, wid = x.shape[2], heads * dh

    row_sizes = [w[n].shape[1] for n in _ROWS]
    row_offs = [sum(row_sizes[:k]) for k in range(len(_ROWS))]
    src_in, src_up = w_mix_in.astype(WIRE_DTYPE), w_ffn_up.astype(WIRE_DTYPE)
    src_rows = jnp.concatenate([w[n] for n in _ROWS], axis=1).astype(WIRE_DTYPE)
    src_convs = _pad_rows(jnp.concatenate([w[n].reshape(-1) for n in _CONVS]), 2)
    g_in, g_convs = _run_comm("allgather_first", _allgather_comm([src_in[0], src_convs]))
    conv_full = {n: jnp.moveaxis(part, 0, 2).reshape(depth, part.shape[2], -1)
                 for n, part in zip(_CONVS, _split_flat(g_convs.reshape(N_CHIPS, -1), [w[n].shape for n in _CONVS]))}
    side_by_side = lambda g: jnp.concatenate([g[j] for j in range(N_CHIPS)], axis=1)

    def from_rows(brought):
        g_rows, = brought
        return _square_projections(*[jnp.concatenate([g_rows[j, off:off + size] for j in range(N_CHIPS)], axis=0)
                                     for off, size in zip(row_offs, row_sizes)])

    xl, mem_l = x[0], mem[0]
    layers, saved, g_rows = [], [], None
    for l in range(depth):
        p = {"mix_norm": mix_norm[l], "xattn_norm": xattn_norm[l], "mem_norm": mem_norm[l], "ffn_norm": ffn_norm[l],
             "gdn_out_norm": gdn_out_norm[l], "a_log": gdn_a_log[l].reshape(heads, 1, 1),
             "dt_bias": gdn_dt_bias[l].reshape(heads, 1, 1), "gdn_conv": conv_full["gdn_conv"][l],
             "sc_conv": conv_full["sc_conv"][l], "ffn_conv": conv_full["ffn_conv"][l]}
        p.update(_input_projection(side_by_side(g_in), heads, dh))
        carry = {"gdn_prep_fwd": _allgather_comm([src_up[l]])}
        late = {"gdn_prep_fwd": lambda brought: {"wup": brought[0]}}
        if l == 0:
            carry["mm_mix_in"], late["mm_mix_in"] = _allgather_comm([src_rows[0]]), from_rows
        else:
            p.update(from_rows(g_rows))
        if l + 1 < depth:
            carry["mm_ffn_up"] = _allgather_comm([src_rows[l + 1]])
            carry["mm_ffn_down"] = _allgather_comm([src_in[l + 1]])
        xl, s, landed, p = _layer_fwd(xl, mem_l, p, heads, dh, carry, late)
        layers.append(p)
        saved.append(s)
        if l + 1 < depth:
            g_rows, (g_in,) = landed["mm_ffn_up"], landed["mm_ffn_down"]
    loss_row, dx, dxb, g_final = _final_loss(xl, final_norm, loss_target[0])

    def by_chip(g):
        g_win = jnp.concatenate([g["wmain"][:, :4 * wid], g["wba"][:, :2 * heads], g["wmain"][:, 4 * wid:]], axis=1)
        parts = (g["wout"], g["wq"], g["wkv"][:, :d], g["wkv"][:, d:], g["wo"], g["wdown"])
        by_rows = [p.reshape(N_CHIPS, p.shape[0] // N_CHIPS, p.shape[1]) for p in parts]
        return [_halves_by_chip(_by_chip_columns(g_win)), _halves_by_chip(_by_chip_columns(g["wup"])),
                _halves_by_chip(jnp.concatenate(by_rows, axis=1))]

    per_layer, shards, reduce = [None] * depth, [None] * depth, None
    for l in reversed(range(depth)):
        dx, dxb, per_layer[l] = _layer_bwd(dx, dxb, mem_l, saved[l], layers[l], heads, dh, reduce)
        if reduce is not None:
            shards[l + 1] = reduce.result
        reduce = _ReduceScatter(by_chip(per_layer[l]), chip, core)
    reduce.run_alone()
    shards[0] = reduce.result
    by_layer = [[s[t].reshape(-1, s[t].shape[-1]) for s in shards] for t in range(3)]
    grad = {"w_mix_in": jnp.stack(by_layer[0]), "w_ffn_up": jnp.stack(by_layer[1])}
    for n, off, size in zip(_ROWS, row_offs, row_sizes):
        grad[n] = jnp.stack([r[off:off + size] for r in by_layer[2]])

    stack = lambda k: jnp.stack([g[k] for g in per_layer])
    small_g = {"mix_norm": stack("mix_norm"), "gdn_a_log": stack("a_log").reshape(depth, heads),
               "gdn_dt_bias": stack("dt_bias").reshape(depth, heads), "gdn_out_norm": stack("gdn_out_norm"),
               "xattn_norm": stack("xattn_norm"), "mem_norm": stack("mem_norm"), "ffn_norm": stack("ffn_norm"),
               "final_norm": g_final, "gdn_conv": stack("gdn_conv"), "sc_conv": stack("sc_conv"),
               "ffn_conv": stack("ffn_conv")}
    names = _REPLICATED + _CONVS
    small = jnp.concatenate([small_g[n].reshape(-1) for n in names] + [loss_row[0, :1]])
    small_sum = _allreduce_small(_pad_rows(small, 1)[0]).reshape(-1)
    parts = _split_flat(small_sum, [small_g[n].shape for n in names] + [(1,)])
    g_rep = dict(zip(_REPLICATED, parts[:len(_REPLICATED)]))
    for n, part in zip(_CONVS, parts[len(_REPLICATED):-1]):
        grad[n] = lax.dynamic_slice_in_dim(part, chip * w[n].shape[2], w[n].shape[2], axis=2)
    loss = parts[-1][0]

    delta, new_m, new_v = {}, {}, {}
    for n in ("w_mix_in", "w_ffn_up") + _ROWS + _CONVS:
        delta[n], new_m[n], new_v[n] = _adamw("adamw_" + n, w[n], grad[n], m[n], v[n])
    pack_rep = lambda t: _pad_rows(jnp.concatenate([t[n].reshape(-1) for n in _REPLICATED]), 1)[0]
    outs = _adamw("adamw_replicated", pack_rep(w), pack_rep(g_rep), pack_rep(m), pack_rep(v))
    shapes = [w[n].shape for n in _REPLICATED]
    for tgt, packed_out in zip((delta, new_m, new_v), outs):
        tgt.update(zip(_REPLICATED, _split_flat(packed_out.reshape(-1), shapes)))
    grad.update(g_rep)
    return (loss, dx[None], *[grad[n] for n in _WEIGHTS], *[delta[n] for n in _WEIGHTS],
            *[new_m[n] for n in _WEIGHTS], *[new_v[n] for n in _WEIGHTS])
```

```python
import functools

import jax
import jax.numpy as jnp
from jax import lax
from jax.experimental import pallas as pl
from jax.experimental.pallas import tpu as pltpu

F32 = jnp.float32
MXU_DTYPE = jnp.bfloat16
WIRE_DTYPE = jnp.bfloat16
SOLVE_PRECISION = lax.Precision.HIGH
EPS = 1e-6
CHUNK = 64
XATTN_HEADS = 4
LANES = 128
HALO = 16
EW_ROWS, EW_COLS = 256, 2816
VMEM_LIMIT = 52 * 1024 * 1024
ADAM_LR, ADAM_B1, ADAM_B2, ADAM_EPS, ADAM_WD, ADAM_STEP = 0.001, 0.9, 0.999, 1e-08, 0.01, 10
MESH = pl.DeviceIdType.MESH
N_CHIPS = 4
N_DEV = 8

_DIMS = {
    "nn": (((1,), (0,)), ((), ())),
    "nt": (((1,), (1,)), ((), ())),
    "tn": (((0,), (0,)), ((), ())),
}


def _tile(n, pref, align=LANES):
    if n <= pref:
        return n
    t = (pref // align) * align
    while t >= align:
        if n % t == 0:
            return t
        t -= align
    return n


def _params(*sem):
    return pltpu.CompilerParams(dimension_semantics=sem, vmem_limit_bytes=VMEM_LIMIT)


def _dot(a, b, form, hi=False):
    (ca, cb), _ = _DIMS[form]
    dims = (((ca[0] + 1,), (cb[0] + 1,)), ((0,), (0,))) if a.ndim == 3 else _DIMS[form]
    if hi:
        return lax.dot_general(a.astype(F32), b.astype(F32), dims, precision=SOLVE_PRECISION,
                               preferred_element_type=F32)
    return lax.dot_general(a.astype(MXU_DTYPE), b.astype(MXU_DTYPE), dims, preferred_element_type=F32)


@functools.partial(jax.custom_vjp, nondiff_argnums=(2, 3))
def _dot_d(a, b, form, hi):
    return _dot(a, b, form, hi)


def _dot_d_fwd(a, b, form, hi):
    return _dot(a, b, form, hi), (a, b)


def _dot_d_bwd(form, hi, res, g):
    a, b = res
    if form == "nn":
        da, db = _dot_d(g, b, "nt", hi), _dot_d(a, g, "tn", hi)
    elif form == "nt":
        da, db = _dot_d(g, b, "nn", hi), _dot_d(g, a, "tn", hi)
    else:
        da, db = _dot_d(b, g, "nt", hi), _dot_d(a, g, "nn", hi)
    return da.astype(a.dtype), db.astype(b.dtype)


_dot_d.defvjp(_dot_d_fwd, _dot_d_bwd)


def _tri_inv_impl(a, mmh):
    c = a.shape[-1]
    r = lax.broadcasted_iota(jnp.int32, (c, c), 0)
    s = lax.broadcasted_iota(jnp.int32, (c, c), 1)
    eye = (r == s).astype(F32)
    diag_blk = (r // 16) == (s // 16)
    d = jnp.where(diag_blk, a, 0.0)
    low = a - d
    d2 = mmh(d, d)
    d4 = mmh(d2, d2)
    d8 = mmh(d4, d4)
    td = mmh(mmh(mmh(eye - d, eye + d2), eye + d4), eye + d8)
    n = mmh(td, low)
    acc = eye - n
    p = n
    pw = 1
    while 2 * pw < c // 16:
        p = mmh(p, p)
        pw *= 2
        acc = mmh(acc, eye + p)
    return mmh(acc, td)


def _mmh_plain(a, b):
    return _dot(a, b, "nn", True)


@jax.custom_vjp
def _tri_inv_known(a, t):
    return t


def _tri_inv_known_fwd(a, t):
    return t, t


def _tri_inv_known_bwd(t, g):
    return -_dot(_dot(t, g, "tn", True), t, "nt", True), jnp.zeros_like(t)


_tri_inv_known.defvjp(_tri_inv_known_fwd, _tri_inv_known_bwd)


class _Ops:
    def __init__(self, diff, tinv=None):
        self.diff, self.tinv = diff, tinv

    def mm(self, a, b, form="nn"):
        return _dot_d(a, b, form, False) if self.diff else _dot(a, b, form, False)

    def mmh(self, a, b, form="nn"):
        return _dot_d(a, b, form, True) if self.diff else _dot(a, b, form, True)

    def tri_inv(self, a):
        return _tri_inv_known(a, self.tinv) if self.diff else _tri_inv_impl(a, _mmh_plain)


_PLAIN = _Ops(False)
_DIFF = _Ops(True)


def _sigmoid(x):
    return 1.0 / (1.0 + jnp.exp(-x))


def _silu(x):
    return x * _sigmoid(x)


def _softplus(x):
    return jnp.maximum(x, 0.0) + jnp.log(1.0 + jnp.exp(-jnp.abs(x)))


def _rms(x, g):
    return x * lax.rsqrt(jnp.mean(x * x, axis=-1, keepdims=True) + EPS) * g


def _matmul_tiles(m, n, k, form):
    if k <= 2048:
        return _tile(m, 1024), _tile(n, 1408), k
    if k <= 8192:
        return _tile(m, 1024 if form == "nn" else 512), _tile(n, 512), k
    return _tile(m, 1024), _tile(n, 1024), _tile(k, 2816)


def _matmul(name, a, b, form, out_dtype, add=None, comm=None):
    b_shape = b.shape if b.ndim == 2 else (b.shape[1], N_CHIPS * b.shape[2])
    if form == "nn":
        (m, k), (k2, n) = a.shape, b_shape
    elif form == "nt":
        (m, k), (n, k2) = a.shape, b_shape
    else:
        (k, m), (k2, n) = a.shape, b_shape
    assert k == k2, (name, a.shape, b.shape, form)
    tm, tn, tk = _matmul_tiles(m, n, k, form)
    if b.ndim == 3:
        assert form != "tn", name
        tn, tk = (_tile(b.shape[2], tn), tk) if form == "nn" else (tn, _tile(b.shape[2], tk))
    nk = k // tk
    out_bytes = tm * tn * (jnp.dtype(out_dtype).itemsize + (4 if add is not None else 0))
    vmem = 2 * (tm * tk * a.dtype.itemsize + tk * tn * b.dtype.itemsize + out_bytes) + (tm * tn * 4 if nk > 1 else 0)
    assert vmem <= VMEM_LIMIT, (name, tm, tn, tk, vmem)
    if form == "nn":
        a_spec = pl.BlockSpec((tm, tk), lambda i, j, kk: (i, kk))
        b_spec = pl.BlockSpec((tk, tn), lambda i, j, kk: (kk, j))
    elif form == "nt":
        a_spec = pl.BlockSpec((tm, tk), lambda i, j, kk: (i, kk))
        b_spec = pl.BlockSpec((tn, tk), lambda i, j, kk: (j, kk))
    else:
        a_spec = pl.BlockSpec((tk, tm), lambda i, j, kk: (kk, i))
        b_spec = pl.BlockSpec((tk, tn), lambda i, j, kk: (kk, j))
    if b.ndim == 3:
        per = b.shape[2] // (tn if form == "nn" else tk)
        if form == "nn":
            b_spec = pl.BlockSpec((None, tk, tn), lambda i, j, kk: (j // per, kk, j % per))
        else:
            b_spec = pl.BlockSpec((None, tn, tk), lambda i, j, kk: (kk // per, j, kk % per))
    o_spec = pl.BlockSpec((tm, tn), lambda i, j, kk: (i, j))
    has_add = add is not None
    grid = (m // tm, n // tn, nk)
    n_in = 3 if has_add else 2
    c_in, c_out = (len(comm.ins), len(comm.outs)) if comm is not None else (0, 0)

    def body(*refs):
        a_ref, b_ref = refs[0], refs[1]
        add_ref = refs[2] if has_add else None
        o_ref = refs[n_in + c_in]
        pids = [pl.program_id(ax) for ax in range(3)]
        if comm is not None:
            comm_refs = (refs[n_in:n_in + c_in], refs[n_in + c_in + 1:n_in + c_in + 1 + c_out], refs[-2:])

            @pl.when(jnp.logical_and(jnp.logical_and(pids[0] == 0, pids[1] == 0), pids[2] == 0))
            def _():
                comm.start(*comm_refs)

        def finish(acc):
            if has_add:
                acc = acc + add_ref[...].astype(F32)
            o_ref[...] = acc.astype(o_ref.dtype)

        p = _dot(a_ref[...], b_ref[...], form)
        if nk == 1:
            finish(p)
        else:
            acc_ref = refs[n_in + c_in + 1 + c_out]

            @pl.when(pids[2] == 0)
            def _():
                acc_ref[...] = p

            @pl.when(pids[2] > 0)
            def _():
                acc_ref[...] += p

            @pl.when(pids[2] == nk - 1)
            def _():
                finish(acc_ref[...])

        if comm is not None:
            @pl.when(jnp.logical_and(jnp.logical_and(pids[0] == grid[0] - 1, pids[1] == grid[1] - 1),
                                     pids[2] == grid[2] - 1))
            def _():
                comm.finish(*comm_refs)

    acc_scratch = [pltpu.VMEM((tm, tn), F32)] if nk > 1 else []
    if comm is None:
        return pl.pallas_call(
            body, name=name, grid=grid, in_specs=[a_spec, b_spec] + ([o_spec] if has_add else []), out_specs=o_spec,
            out_shape=jax.ShapeDtypeStruct((m, n), out_dtype), scratch_shapes=acc_scratch,
            compiler_params=_params("parallel", "parallel", "arbitrary"),
        )(*((a, b, add) if has_add else (a, b)))
    outs = pl.pallas_call(
        body, name=name, grid=grid, in_specs=[a_spec, b_spec] + ([o_spec] if has_add else []) + [_ANY] * c_in,
        out_specs=[o_spec] + [_ANY] * c_out, out_shape=[jax.ShapeDtypeStruct((m, n), out_dtype)] + list(comm.outs),
        scratch_shapes=acc_scratch + _sem_pairs(comm.n_sems),
        input_output_aliases={n_in + i: 1 + o for i, o in comm.aliases.items()},
        compiler_params=_params("arbitrary", "arbitrary", "arbitrary"),
    )(*((a, b, add) if has_add else (a, b)), *comm.ins)
    return outs[0], list(outs[1:])


def _rms_fwd(name, x, g):
    t, d = x.shape
    tm = _tile(t, 512, 16)

    def body(x_ref, g_ref, o_ref):
        o_ref[...] = _rms(x_ref[...], g_ref[...]).astype(o_ref.dtype)

    return pl.pallas_call(
        body, name=name, grid=(t // tm,),
        in_specs=[pl.BlockSpec((tm, d), lambda i: (i, 0)), pl.BlockSpec((1, d), lambda i: (0, 0))],
        out_specs=pl.BlockSpec((tm, d), lambda i: (i, 0)),
        out_shape=jax.ShapeDtypeStruct((t, d), MXU_DTYPE), compiler_params=_params("parallel"),
    )(x, g.reshape(1, d))


def _rms_bwd(name, x, g, dh, dres=None):
    t, d = x.shape
    tm = _tile(t, 256, 16)
    has_res = dres is not None

    def body(*refs):
        x_ref, g_ref, dh_ref = refs[:3]
        dres_ref = refs[3] if has_res else None
        dx_ref, dxb_ref, dg_ref = refs[-3:]
        _, vjp = jax.vjp(_rms, x_ref[...], g_ref[...])
        dx, dg = vjp(dh_ref[...].astype(F32))
        if has_res:
            dx = dx + dres_ref[...]
        dx_ref[...] = dx
        dxb_ref[...] = dx.astype(dxb_ref.dtype)
        first = pl.program_id(0) == 0

        @pl.when(first)
        def _():
            dg_ref[...] = dg

        @pl.when(jnp.logical_not(first))
        def _():
            dg_ref[...] += dg

    row = pl.BlockSpec((tm, d), lambda i: (i, 0))
    vec = pl.BlockSpec((1, d), lambda i: (0, 0))
    dx, dxb, dg = pl.pallas_call(
        body, name=name, grid=(t // tm,),
        in_specs=[row, vec, row] + ([row] if has_res else []), out_specs=[row, row, vec],
        out_shape=[jax.ShapeDtypeStruct((t, d), F32), jax.ShapeDtypeStruct((t, d), MXU_DTYPE),
                   jax.ShapeDtypeStruct((1, d), F32)],
        compiler_params=_params("arbitrary"),
    )(*((x, g.reshape(1, d), dh) + ((dres,) if has_res else ())))
    return dx, dxb, dg.reshape(d)


def _final_loss(x, g, target):
    t, d = x.shape
    tm = _tile(t, 256, 16)

    def body(x_ref, g_ref, t_ref, loss_ref, dx_ref, dxb_ref, dg_ref):
        y, vjp = jax.vjp(_rms, x_ref[...], g_ref[...])
        err = y - t_ref[...]
        dx, dg = vjp(err * (1.0 / d))
        dx_ref[...] = dx
        dxb_ref[...] = dx.astype(dxb_ref.dtype)
        part = jnp.zeros((1, LANES), F32) + 0.5 * jnp.sum(jnp.mean(err * err, axis=-1, keepdims=True))
        first = pl.program_id(0) == 0

        @pl.when(first)
        def _():
            dg_ref[...] = dg
            loss_ref[...] = part

        @pl.when(jnp.logical_not(first))
        def _():
            dg_ref[...] += dg
            loss_ref[...] += part

    row = pl.BlockSpec((tm, d), lambda i: (i, 0))
    vec = pl.BlockSpec((1, d), lambda i: (0, 0))
    loss, dx, dxb, dg = pl.pallas_call(
        body, name="final_loss", grid=(t // tm,), in_specs=[row, vec, row],
        out_specs=[pl.BlockSpec((1, LANES), lambda i: (0, 0)), row, row, vec],
        out_shape=[jax.ShapeDtypeStruct((1, LANES), F32), jax.ShapeDtypeStruct((t, d), F32),
                   jax.ShapeDtypeStruct((t, d), MXU_DTYPE), jax.ShapeDtypeStruct((1, d), F32)],
        compiler_params=_params("arbitrary"),
    )(x, g.reshape(1, d), target)
    return loss, dx, dxb, dg.reshape(d)


def _conv_taps(x_ext, w, rows):
    kk = w.shape[0]
    y = x_ext[HALO:] * w[kk - 1:kk, :]
    for j in range(kk - 1):
        y = y + pltpu.roll(x_ext, kk - 1 - j, axis=0)[HALO:] * w[j:j + 1, :]
    return y


def _col_specs(tm, tn, col0, t_rows):
    assert col0 % tn == 0 and tm % HALO == 0
    c0 = col0 // tn
    per, last = tm // HALO, t_rows // HALO - 1
    tile = pl.BlockSpec((tm, tn), lambda j, i: (i, c0 + j))
    prev = pl.BlockSpec((HALO, tn), lambda j, i: (jnp.maximum(i * per - 1, 0), c0 + j))
    nxt = pl.BlockSpec((HALO, tn), lambda j, i: (jnp.minimum((i + 1) * per, last), c0 + j))
    return tile, prev, nxt


def _conv_fwd(name, xa, xa_col, w, w_col, ncols, out_dtype, xb=None, xb_col=0, gate=None, gate_col=0):
    t = xa.shape[0]
    kk = w.shape[0]
    tm, tn = _tile(t, EW_ROWS, HALO), _tile(ncols, EW_COLS)
    nrow = t // tm
    has_b, has_g = xb is not None, gate is not None

    def body(*refs):
        refs = list(refs)
        xa_ref, xap_ref = refs.pop(0), refs.pop(0)
        xb_ref, xbp_ref = (refs.pop(0), refs.pop(0)) if has_b else (None, None)
        w_ref = refs.pop(0)
        g_ref = refs.pop(0) if has_g else None
        o_ref = refs.pop(0)
        i = pl.program_id(1)
        x, xp = xa_ref[...].astype(F32), xap_ref[...].astype(F32)
        if has_b:
            x, xp = x * xb_ref[...].astype(F32), xp * xbp_ref[...].astype(F32)
        xp = jnp.where(i == 0, 0.0, xp)
        y = _conv_taps(jnp.concatenate([xp, x], axis=0), w_ref[...], tm)
        if has_g:
            y = y * g_ref[...].astype(F32)
        o_ref[...] = y.astype(o_ref.dtype)

    a_tile, a_prev, _ = _col_specs(tm, tn, xa_col, t)
    ins, specs = [xa, xa], [a_tile, a_prev]
    if has_b:
        b_tile, b_prev, _ = _col_specs(tm, tn, xb_col, t)
        ins, specs = ins + [xb, xb], specs + [b_tile, b_prev]
    assert w_col % tn == 0
    ins, specs = ins + [w], specs + [pl.BlockSpec((kk, tn), lambda j, i: (0, w_col // tn + j))]
    if has_g:
        ins, specs = ins + [gate], specs + [_col_specs(tm, tn, gate_col, t)[0]]
    return pl.pallas_call(
        body, name=name, grid=(ncols // tn, nrow), in_specs=specs,
        out_specs=pl.BlockSpec((tm, tn), lambda j, i: (i, j)),
        out_shape=jax.ShapeDtypeStruct((t, ncols), out_dtype), compiler_params=_params("parallel", "parallel"),
    )(*ins)


def _conv_bwd(name, xa, xa_col, w, w_col, dy, dy_col, ncols, dx_dtype, xb=None, xb_col=0, gate=None, gate_col=0):
    t = xa.shape[0]
    kk = w.shape[0]
    tm, tn = _tile(t, EW_ROWS, HALO), _tile(ncols, EW_COLS)
    nrow = t // tm
    has_b, has_g = xb is not None, gate is not None

    def body(*refs):
        refs = list(refs)
        xa_ref, xap_ref = refs.pop(0), refs.pop(0)
        xb_ref, xbp_ref = (refs.pop(0), refs.pop(0)) if has_b else (None, None)
        w_ref = refs.pop(0)
        dy_ref, dyn_ref = refs.pop(0), refs.pop(0)
        g_ref, gn_ref = (refs.pop(0), refs.pop(0)) if has_g else (None, None)
        dxa_ref = refs.pop(0)
        dxb_ref = refs.pop(0) if has_b else None
        dg_ref = refs.pop(0) if has_g else None
        dw_ref = refs.pop(0)
        i = pl.program_id(1)
        wv = w_ref[...]
        xa_t, xa_p = xa_ref[...].astype(F32), xap_ref[...].astype(F32)
        x, xp = xa_t, xa_p
        if has_b:
            xb_t = xb_ref[...].astype(F32)
            x, xp = x * xb_t, xp * xbp_ref[...].astype(F32)
        xp = jnp.where(i == 0, 0.0, xp)
        x_ext = jnp.concatenate([xp, x], axis=0)
        dyv, dyn = dy_ref[...].astype(F32), dyn_ref[...].astype(F32)
        if has_g:
            dg_ref[...] = (dyv * _conv_taps(x_ext, wv, tm)).astype(dg_ref.dtype)
            dyv, dyn = dyv * g_ref[...].astype(F32), dyn * gn_ref[...].astype(F32)
        dyn = jnp.where(i == nrow - 1, 0.0, dyn)
        dy_ext = jnp.concatenate([dyv, dyn], axis=0)
        dx = dyv * wv[kk - 1:kk, :]
        row8 = lax.broadcasted_iota(jnp.int32, (8, tn), 0)
        dw = jnp.where(row8 == kk - 1, jnp.sum(dyv * x, axis=0, keepdims=True), 0.0)
        for j in range(kk - 1):
            s = kk - 1 - j
            dx = dx + pltpu.roll(dy_ext, tm + HALO - s, axis=0)[:tm] * wv[j:j + 1, :]
            dwj = jnp.sum(dyv * pltpu.roll(x_ext, s, axis=0)[HALO:], axis=0, keepdims=True)
            dw = dw + jnp.where(row8 == j, dwj, 0.0)
        if has_b:
            dxa_ref[...] = (dx * xb_t).astype(dxa_ref.dtype)
            dxb_ref[...] = (dx * xa_t).astype(dxb_ref.dtype)
        else:
            dxa_ref[...] = dx.astype(dxa_ref.dtype)

        @pl.when(i == 0)
        def _():
            dw_ref[...] = dw

        @pl.when(i > 0)
        def _():
            dw_ref[...] += dw

    a_tile, a_prev, _ = _col_specs(tm, tn, xa_col, t)
    ins, specs = [xa, xa], [a_tile, a_prev]
    if has_b:
        b_tile, b_prev, _ = _col_specs(tm, tn, xb_col, t)
        ins, specs = ins + [xb, xb], specs + [b_tile, b_prev]
    assert w_col % tn == 0
    ins, specs = ins + [w], specs + [pl.BlockSpec((kk, tn), lambda j, i: (0, w_col // tn + j))]
    if dy.ndim == 3:
        nh, per, last = ncols // 2 // tn, tm // HALO, t // HALO - 1
        assert dy_col == 0 and nh * tn * 2 == ncols and not has_g, (name, dy.shape, tn)
        dy = dy.reshape(2 * t, ncols // 2)
        d_tile = pl.BlockSpec((tm, tn), lambda j, i: (j // nh * nrow + i, j % nh))
        d_next = pl.BlockSpec(
            (HALO, tn), lambda j, i: (j // nh * (last + 1) + jnp.minimum((i + 1) * per, last), j % nh))
    else:
        d_tile, _, d_next = _col_specs(tm, tn, dy_col, t)
    ins, specs = ins + [dy, dy], specs + [d_tile, d_next]
    if has_g:
        g_tile, _, g_next = _col_specs(tm, tn, gate_col, t)
        ins, specs = ins + [gate, gate], specs + [g_tile, g_next]
    out_tile = pl.BlockSpec((tm, tn), lambda j, i: (i, j))
    shapes, ospecs = [jax.ShapeDtypeStruct((t, ncols), dx_dtype)], [out_tile]
    if has_b:
        shapes, ospecs = shapes + [jax.ShapeDtypeStruct((t, ncols), dx_dtype)], ospecs + [out_tile]
    if has_g:
        shapes, ospecs = shapes + [jax.ShapeDtypeStruct((t, ncols), dx_dtype)], ospecs + [out_tile]
    shapes = shapes + [jax.ShapeDtypeStruct((8, ncols), F32)]
    ospecs = ospecs + [pl.BlockSpec((8, tn), lambda j, i: (0, j))]
    outs = list(pl.pallas_call(
        body, name=name, grid=(ncols // tn, nrow), in_specs=specs, out_specs=ospecs, out_shape=shapes,
        compiler_params=_params("parallel", "arbitrary"),
    )(*ins))
    dxa = outs.pop(0)
    dxb = outs.pop(0) if has_b else None
    dgate = outs.pop(0) if has_g else None
    return dxa, dxb, dgate, outs.pop(0)[:kk]


def _ffn_act_fwd(upre, w):
    t, f2 = upre.shape
    f, kk = f2 // 2, w.shape[0]
    tm, tn = _tile(t, EW_ROWS, HALO), _tile(f, EW_COLS)
    nf = f // tn

    def body(g_ref, gp_ref, u_ref, up_ref, wg_ref, wu_ref, cg_ref, cu_ref, a_ref):
        first = pl.program_id(1) == 0

        def conv(x_ref, prev_ref, w_ref):
            prev = jnp.where(first, 0.0, prev_ref[...].astype(F32))
            return _conv_taps(jnp.concatenate([prev, x_ref[...].astype(F32)], axis=0), w_ref[...], tm)

        cg, cu = conv(g_ref, gp_ref, wg_ref), conv(u_ref, up_ref, wu_ref)
        cg_ref[...] = cg.astype(cg_ref.dtype)
        cu_ref[...] = cu.astype(cu_ref.dtype)
        a_ref[...] = (_silu(cg) * cu).astype(a_ref.dtype)

    g_tile, g_prev, _ = _col_specs(tm, tn, 0, t)
    u_tile, u_prev, _ = _col_specs(tm, tn, f, t)
    out = pl.BlockSpec((tm, tn), lambda j, i: (i, j))
    return pl.pallas_call(
        body, name="ffn_act_fwd", grid=(nf, t // tm),
        in_specs=[g_tile, g_prev, u_tile, u_prev, pl.BlockSpec((kk, tn), lambda j, i: (0, j)),
                  pl.BlockSpec((kk, tn), lambda j, i: (0, nf + j))],
        out_specs=[out, out, out], out_shape=[jax.ShapeDtypeStruct((t, f), MXU_DTYPE)] * 3,
        compiler_params=_params("parallel", "parallel"),
    )(upre, upre, upre, upre, w, w)


def _swiglu_bwd(ug, uu, da):
    t, f = ug.shape
    tm, tn = _tile(t, EW_ROWS, 16), _tile(f, EW_COLS)
    nf = f // tn

    def body(g_ref, u_ref, da_ref, o_ref):
        g, d = g_ref[...].astype(F32), da_ref[...].astype(F32)
        sg = _sigmoid(g)
        o_ref[0] = (d * u_ref[...].astype(F32) * (sg * (1.0 + g * (1.0 - sg)))).astype(o_ref.dtype)
        o_ref[1] = (d * (g * sg)).astype(o_ref.dtype)

    tile = pl.BlockSpec((tm, tn), lambda i, j: (i, j))
    return pl.pallas_call(
        body, name="swiglu_bwd", grid=(t // tm, nf), in_specs=[tile] * 3,
        out_specs=pl.BlockSpec((2, tm, tn), lambda i, j: (0, i, j)),
        out_shape=jax.ShapeDtypeStruct((2, t, f), MXU_DTYPE), compiler_params=_params("parallel", "parallel"),
    )(ug, uu, da)


def _gdn_prep(ops, qc, kc, vc, b_col, a_col, a_log, dt_bias):
    c, dh = qc.shape[-2:]
    q, k, v = _silu(qc), _silu(kc), _silu(vc)
    q = q * lax.rsqrt(jnp.sum(q * q, axis=-1, keepdims=True) + EPS) * (dh ** -0.5)
    k = k * lax.rsqrt(jnp.sum(k * k, axis=-1, keepdims=True) + EPS)
    beta = _sigmoid(b_col)
    g_col = -jnp.exp(a_log) * _softplus(a_col + dt_bias)
    r = lax.broadcasted_iota(jnp.int32, (c, c), 0)
    s = lax.broadcasted_iota(jnp.int32, (c, c), 1)
    g_row = jnp.sum(jnp.where(r == s, g_col, 0.0), axis=-2, keepdims=True)
    gc_col = jnp.sum(jnp.where(s <= r, g_row, 0.0), axis=-1, keepdims=True)
    gc_row = jnp.sum(jnp.where(r <= s, g_col, 0.0), axis=-2, keepdims=True)
    decay = jnp.exp(jnp.where(s <= r, gc_col - gc_row, -1e30))
    kb = k * beta
    a = jnp.where(s < r, ops.mm(kb, k, "nt") * decay, 0.0)
    tinv = ops.tri_inv(a)
    e_col = jnp.exp(gc_col)
    uw = ops.mmh(tinv, jnp.concatenate([v * beta, kb * e_col], axis=-1))
    u, w = uw[..., :dh], uw[..., dh:]
    attn = ops.mm(q, k, "nt") * decay
    g_last = jnp.sum(g_col, axis=-2, keepdims=True)
    return u, w, attn, q * e_col, k * jnp.exp(g_last - gc_col), g_last, tinv


def _gdn_step(ops, state, u, w, attn, q_dec, k_dec, g_last):
    v_new = u - ops.mm(w, state)
    o = ops.mm(q_dec, state) + ops.mm(attn, v_new)
    return o, state * jnp.exp(g_last) + ops.mm(k_dec, v_new, "tn")


PREP_HEADS, SCAN_HEADS = 4, 8


def _gdn_blocks(t, heads, hb_pref):
    tc = _tile(t, 256, CHUNK)
    hb = max(h for h in range(1, hb_pref + 1) if heads % h == 0)
    return tc, hb


def _to_chunks(ref, hb, dh):
    tc = ref.shape[0]
    return jnp.concatenate([ref[:, h * dh:(h + 1) * dh].astype(F32).reshape(tc // CHUNK, CHUNK, dh)
                            for h in range(hb)], axis=0)


def _from_chunks(ref, val, hb, dh):
    tc = ref.shape[0]
    ncb = tc // CHUNK
    for h in range(hb):
        ref[:, h * dh:(h + 1) * dh] = val[h * ncb:(h + 1) * ncb].reshape(tc, dh).astype(ref.dtype)


def _per_chunk(s, ncb):
    hb = s.shape[0]
    return jnp.broadcast_to(s[:, None], (hb, ncb, 1, 1)).reshape(hb * ncb, 1, 1)


def _gate_columns(pba, first_head, hb, heads):
    tc = pba.shape[0]
    lane = lax.broadcasted_iota(jnp.int32, pba.shape, 1)

    def pick(k):
        return jnp.sum(jnp.where(lane == k, pba, 0.0), axis=1, keepdims=True).reshape(tc // CHUNK, CHUNK, 1)

    return (jnp.concatenate([pick(first_head + h) for h in range(hb)], axis=0),
            jnp.concatenate([pick(heads + first_head + h) for h in range(hb)], axis=0))


def _gdn_prep_fwd(qkv, pba, a_log, dt_bias, heads, dh, comm=None):
    t = qkv.shape[0]
    tc, hb = _gdn_blocks(t, heads, PREP_HEADS)
    ncb, nhb, width = tc // CHUNK, heads // hb, heads * dh
    nc = t // CHUNK
    grid = (t // tc, nhb)
    c_in, c_out = (len(comm.ins), len(comm.outs)) if comm is not None else (0, 0)

    def body(*refs):
        q_ref, k_ref, v_ref, g_ref, al_ref, dt_ref = refs[:6]
        u_ref, w_ref, p_ref, qd_ref, kd_ref, gl_ref, ti_ref = refs[6 + c_in:13 + c_in]
        if comm is not None:
            comm_refs = (refs[6:6 + c_in], refs[13 + c_in:13 + c_in + c_out], refs[-2:])

            @pl.when(jnp.logical_and(pl.program_id(0) == 0, pl.program_id(1) == 0))
            def _():
                comm.start(*comm_refs)

        b_col, a_col = _gate_columns(g_ref[...], pl.program_id(1) * hb, hb, heads)
        u, w, p, qd, kd, gl, tinv = _gdn_prep(
            _PLAIN, _to_chunks(q_ref, hb, dh), _to_chunks(k_ref, hb, dh), _to_chunks(v_ref, hb, dh), b_col, a_col,
            _per_chunk(al_ref[...], ncb), _per_chunk(dt_ref[...], ncb))
        _from_chunks(u_ref, u, hb, dh)
        _from_chunks(w_ref, w, hb, dh)
        _from_chunks(qd_ref, qd, hb, dh)
        _from_chunks(kd_ref, kd, hb, dh)
        p_ref[...] = p.reshape(hb, tc, CHUNK).astype(p_ref.dtype)
        gl_ref[...] = gl.reshape(hb, ncb, 1, 1)
        ti_ref[...] = tinv.reshape(hb, tc, CHUNK)
        if comm is not None:
            @pl.when(jnp.logical_and(pl.program_id(0) == grid[0] - 1, pl.program_id(1) == grid[1] - 1))
            def _():
                comm.finish(*comm_refs)

    def tok(off):
        return pl.BlockSpec((tc, hb * dh), lambda i, j: (i, off * nhb + j))

    gate = pl.BlockSpec((tc, LANES), lambda i, j: (i, 0))
    scal = pl.BlockSpec((hb, 1, 1), lambda i, j: (j, 0, 0))
    square = pl.BlockSpec((hb, tc, CHUNK), lambda i, j: (j, i, 0))
    outs = pl.pallas_call(
        body, name="gdn_prep_fwd", grid=grid,
        in_specs=[tok(0), tok(1), tok(2), gate, scal, scal] + [_ANY] * c_in,
        out_specs=[tok(0), tok(0), square, tok(0), tok(0), pl.BlockSpec((hb, ncb, 1, 1), lambda i, j: (j, i, 0, 0)),
                   square] + [_ANY] * c_out,
        out_shape=[jax.ShapeDtypeStruct((t, width), F32), jax.ShapeDtypeStruct((t, width), MXU_DTYPE),
                   jax.ShapeDtypeStruct((heads, t, CHUNK), MXU_DTYPE), jax.ShapeDtypeStruct((t, width), MXU_DTYPE),
                   jax.ShapeDtypeStruct((t, width), MXU_DTYPE), jax.ShapeDtypeStruct((heads, nc, 1, 1), F32),
                   jax.ShapeDtypeStruct((heads, t, CHUNK), F32)] + (list(comm.outs) if comm is not None else []),
        scratch_shapes=_sem_pairs(comm.n_sems) if comm is not None else [],
        compiler_params=_params("arbitrary", "arbitrary") if comm is not None else _params("parallel", "parallel"),
    )(qkv, qkv, qkv, pba, a_log, dt_bias, *(comm.ins if comm is not None else ()))
    return tuple(outs[:6]), outs[6], list(outs[7:])


def _gdn_prep_bwd(qkv, pba, a_log, dt_bias, tinv, du, dw, dp, dqd, dkd, dgl, heads, dh):
    t = qkv.shape[0]
    tc, hb = _gdn_blocks(t, heads, PREP_HEADS)
    ncb, nhb, width = tc // CHUNK, heads // hb, heads * dh

    def body(q_ref, k_ref, v_ref, g_ref, al_ref, dt_ref, ti_ref, du_ref, dw_ref, dp_ref, dqd_ref, dkd_ref, dgl_ref,
             dq_ref, dk_ref, dv_ref, dg_ref, dal_ref, ddt_ref):
        first_head = pl.program_id(1) * hb
        b_col, a_col = _gate_columns(g_ref[...], first_head, hb, heads)
        ops = _Ops(True, ti_ref[...].reshape(hb * ncb, CHUNK, CHUNK))

        def prep(q, k, v, b, a, al, dt):
            return _gdn_prep(ops, q, k, v, b, a, _per_chunk(al, ncb), _per_chunk(dt, ncb))[:6]

        _, vjp = jax.vjp(prep, _to_chunks(q_ref, hb, dh), _to_chunks(k_ref, hb, dh), _to_chunks(v_ref, hb, dh),
                         b_col, a_col, al_ref[...], dt_ref[...])
        dq, dk, dv, db, da, dal, ddt = vjp((
            _to_chunks(du_ref, hb, dh), _to_chunks(dw_ref, hb, dh), dp_ref[...].reshape(hb * ncb, CHUNK, CHUNK),
            _to_chunks(dqd_ref, hb, dh), _to_chunks(dkd_ref, hb, dh), dgl_ref[...].reshape(hb * ncb, 1, 1)))
        _from_chunks(dq_ref, dq, hb, dh)
        _from_chunks(dk_ref, dk, hb, dh)
        _from_chunks(dv_ref, dv, hb, dh)
        dal_ref[...] = dal[None]
        ddt_ref[...] = ddt[None]
        lane = lax.broadcasted_iota(jnp.int32, (tc, LANES), 1)
        dgates = jnp.zeros((tc, LANES), F32)
        for h in range(hb):
            rows = slice(h * ncb, (h + 1) * ncb)
            dgates = dgates + jnp.where(lane == first_head + h, db[rows].reshape(tc, 1), 0.0) \
                + jnp.where(lane == heads + first_head + h, da[rows].reshape(tc, 1), 0.0)

        @pl.when(first_head == 0)
        def _():
            dg_ref[...] = dgates

        @pl.when(first_head > 0)
        def _():
            dg_ref[...] += dgates

    def tok(off):
        return pl.BlockSpec((tc, hb * dh), lambda i, j: (i, off * nhb + j))

    gate = pl.BlockSpec((tc, LANES), lambda i, j: (i, 0))
    scal = pl.BlockSpec((hb, 1, 1), lambda i, j: (j, 0, 0))
    part = pl.BlockSpec((1, hb, 1, 1), lambda i, j: (i, j, 0, 0))
    pspec = pl.BlockSpec((hb, tc, CHUNK), lambda i, j: (j, i, 0))
    glspec = pl.BlockSpec((hb, ncb, 1, 1), lambda i, j: (j, i, 0, 0))
    tokf = jax.ShapeDtypeStruct((t, width), F32)
    partf = jax.ShapeDtypeStruct((t // tc, heads, 1, 1), F32)
    return pl.pallas_call(
        body, name="gdn_prep_bwd", grid=(t // tc, nhb),
        in_specs=[tok(0), tok(1), tok(2), gate, scal, scal, pspec, tok(0), tok(0), pspec, tok(0), tok(0), glspec],
        out_specs=[tok(0), tok(0), tok(0), gate, part, part],
        out_shape=[tokf, tokf, tokf, jax.ShapeDtypeStruct((t, LANES), F32), partf, partf],
        compiler_params=_params("parallel", "arbitrary"),
    )(qkv, qkv, qkv, pba, a_log, dt_bias, tinv, du, dw, dp, dqd, dkd, dgl)


def _heads(ref, rows, hb, dh):
    return jnp.stack([ref[rows, h * dh:(h + 1) * dh].astype(F32) for h in range(hb)])


def _put_heads(ref, rows, val, dh):
    for h in range(val.shape[0]):
        ref[rows, h * dh:(h + 1) * dh] = val[h].astype(ref.dtype)


def _gdn_scan_fwd(u, w, p, qd, kd, gl, heads, dh):
    t = u.shape[0]
    tc, hb = _gdn_blocks(t, heads, SCAN_HEADS)
    ncb, nhb = tc // CHUNK, heads // hb
    nc = t // CHUNK

    def body(u_ref, w_ref, p_ref, qd_ref, kd_ref, gl_ref, o_ref, s_ref, state):
        @pl.when(pl.program_id(1) == 0)
        def _():
            state[...] = jnp.zeros_like(state)

        for c in range(ncb):
            rs = slice(c * CHUNK, (c + 1) * CHUNK)
            s_in = state[...]
            s_ref[:, c] = s_in
            o, s_out = _gdn_step(_PLAIN, s_in, _heads(u_ref, rs, hb, dh), _heads(w_ref, rs, hb, dh), p_ref[:, rs, :],
                                 _heads(qd_ref, rs, hb, dh), _heads(kd_ref, rs, hb, dh), gl_ref[:, c])
            _put_heads(o_ref, rs, o, dh)
            state[...] = s_out

    tok = pl.BlockSpec((tc, hb * dh), lambda j, i: (i, j))
    pspec = pl.BlockSpec((hb, tc, CHUNK), lambda j, i: (j, i, 0))
    glspec = pl.BlockSpec((hb, ncb, 1, 1), lambda j, i: (j, i, 0, 0))
    return pl.pallas_call(
        body, name="gdn_scan_fwd", grid=(nhb, t // tc),
        in_specs=[tok, tok, pspec, tok, tok, glspec],
        out_specs=[tok, pl.BlockSpec((hb, ncb, dh, dh), lambda j, i: (j, i, 0, 0))],
        out_shape=[jax.ShapeDtypeStruct((t, heads * dh), F32), jax.ShapeDtypeStruct((heads, nc, dh, dh), F32)],
        scratch_shapes=[pltpu.VMEM((hb, dh, dh), F32)],
        compiler_params=_params("arbitrary", "arbitrary"),
    )(u, w, p, qd, kd, gl)


def _gdn_scan_bwd(u, w, p, qd, kd, gl, states, do, heads, dh):
    t = u.shape[0]
    tc, hb = _gdn_blocks(t, heads, SCAN_HEADS)
    ncb, nhb = tc // CHUNK, heads // hb
    nc, nt = t // CHUNK, t // tc

    def body(u_ref, w_ref, p_ref, qd_ref, kd_ref, gl_ref, s_ref, do_ref,
             du_ref, dw_ref, dp_ref, dqd_ref, dkd_ref, dgl_ref, dstate):
        @pl.when(pl.program_id(1) == 0)
        def _():
            dstate[...] = jnp.zeros_like(dstate)

        for c in reversed(range(ncb)):
            rs = slice(c * CHUNK, (c + 1) * CHUNK)
            _, vjp = jax.vjp(functools.partial(_gdn_step, _DIFF), s_ref[:, c], _heads(u_ref, rs, hb, dh),
                             _heads(w_ref, rs, hb, dh), p_ref[:, rs, :].astype(F32), _heads(qd_ref, rs, hb, dh),
                             _heads(kd_ref, rs, hb, dh), gl_ref[:, c])
            ds, du, dw, dp, dqd, dkd, dgl = vjp((_heads(do_ref, rs, hb, dh), dstate[...]))
            dstate[...] = ds
            _put_heads(du_ref, rs, du, dh)
            _put_heads(dw_ref, rs, dw, dh)
            _put_heads(dqd_ref, rs, dqd, dh)
            _put_heads(dkd_ref, rs, dkd, dh)
            dp_ref[:, rs, :] = dp
            dgl_ref[:, c] = dgl

    tok = pl.BlockSpec((tc, hb * dh), lambda j, i: (nt - 1 - i, j))
    pspec = pl.BlockSpec((hb, tc, CHUNK), lambda j, i: (j, nt - 1 - i, 0))
    glspec = pl.BlockSpec((hb, ncb, 1, 1), lambda j, i: (j, nt - 1 - i, 0, 0))
    sspec = pl.BlockSpec((hb, ncb, dh, dh), lambda j, i: (j, nt - 1 - i, 0, 0))
    tokf = jax.ShapeDtypeStruct((t, heads * dh), F32)
    return pl.pallas_call(
        body, name="gdn_scan_bwd", grid=(nhb, nt),
        in_specs=[tok, tok, pspec, tok, tok, glspec, sspec, tok],
        out_specs=[tok, tok, pspec, tok, tok, glspec],
        out_shape=[tokf, tokf, jax.ShapeDtypeStruct((heads, t, CHUNK), F32), tokf, tokf,
                   jax.ShapeDtypeStruct((heads, nc, 1, 1), F32)],
        scratch_shapes=[pltpu.VMEM((hb, dh, dh), F32)],
        compiler_params=_params("arbitrary", "arbitrary"),
    )(u, w, p, qd, kd, gl, states, do)


def _gdn_post(o, z, gain):
    return _rms(o, gain) * _silu(z)


def _gdn_post_fwd(o, pm, z_col, gain, heads, dh):
    t, wid = o.shape
    tm = _tile(t, 256, 16)
    assert z_col % wid == 0

    def body(o_ref, z_ref, g_ref, y_ref):
        for h in range(heads):
            ls = slice(h * dh, (h + 1) * dh)
            y_ref[:, ls] = _gdn_post(o_ref[:, ls], z_ref[:, ls], g_ref[...]).astype(y_ref.dtype)

    blk = pl.BlockSpec((tm, wid), lambda i: (i, 0))
    return pl.pallas_call(
        body, name="gdn_post_fwd", grid=(t // tm,),
        in_specs=[blk, pl.BlockSpec((tm, wid), lambda i: (i, z_col // wid)), pl.BlockSpec((1, dh), lambda i: (0, 0))],
        out_specs=blk, out_shape=jax.ShapeDtypeStruct((t, wid), MXU_DTYPE), compiler_params=_params("parallel"),
    )(o, pm, gain.reshape(1, dh))


def _gdn_post_bwd(o, pm, z_col, gain, dy, heads, dh):
    t, wid = o.shape
    tm = _tile(t, 256, 16)
    assert z_col % wid == 0

    def body(o_ref, z_ref, g_ref, dy_ref, do_ref, dz_ref, dg_ref):
        dg = jnp.zeros((1, dh), F32)
        for h in range(heads):
            ls = slice(h * dh, (h + 1) * dh)
            _, vjp = jax.vjp(_gdn_post, o_ref[:, ls], z_ref[:, ls], g_ref[...])
            do, dz, dg_h = vjp(dy_ref[:, ls])
            do_ref[:, ls] = do
            dz_ref[:, ls] = dz.astype(dz_ref.dtype)
            dg = dg + dg_h
        first = pl.program_id(0) == 0

        @pl.when(first)
        def _():
            dg_ref[...] = dg

        @pl.when(jnp.logical_not(first))
        def _():
            dg_ref[...] += dg

    blk = pl.BlockSpec((tm, wid), lambda i: (i, 0))
    vec = pl.BlockSpec((1, dh), lambda i: (0, 0))
    do, dz, dg = pl.pallas_call(
        body, name="gdn_post_bwd", grid=(t // tm,),
        in_specs=[blk, pl.BlockSpec((tm, wid), lambda i: (i, z_col // wid)), vec, blk], out_specs=[blk, blk, vec],
        out_shape=[jax.ShapeDtypeStruct((t, wid), F32), jax.ShapeDtypeStruct((t, wid), MXU_DTYPE),
                   jax.ShapeDtypeStruct((1, dh), F32)],
        compiler_params=_params("arbitrary"),
    )(o, pm, gain.reshape(1, dh), dy)
    return do, dz, dg.reshape(dh)


def _attn(ops, q, kv):
    d = q.shape[1]
    hd = d // XATTN_HEADS
    outs = []
    for h in range(XATTN_HEADS):
        qh, kh, vh = q[:, h * hd:(h + 1) * hd], kv[:, h * hd:(h + 1) * hd], kv[:, d + h * hd:d + (h + 1) * hd]
        s = ops.mm(qh, kh, "nt") * (hd ** -0.5)
        e = jnp.exp(s - lax.stop_gradient(jnp.max(s, axis=-1, keepdims=True)))
        outs.append(ops.mm(e / jnp.sum(e, axis=-1, keepdims=True), vh))
    return jnp.concatenate(outs, axis=1)


def _attn_fwd(q, kv):
    t, d = q.shape
    nm = kv.shape[0]
    tm = _tile(t, 512, 16)

    def body(q_ref, kv_ref, o_ref):
        o_ref[...] = _attn(_PLAIN, q_ref[...], kv_ref[...]).astype(o_ref.dtype)

    return pl.pallas_call(
        body, name="xattn_fwd", grid=(t // tm,),
        in_specs=[pl.BlockSpec((tm, d), lambda i: (i, 0)), pl.BlockSpec((nm, 2 * d), lambda i: (0, 0))],
        out_specs=pl.BlockSpec((tm, d), lambda i: (i, 0)),
        out_shape=jax.ShapeDtypeStruct((t, d), MXU_DTYPE), compiler_params=_params("parallel"),
    )(q, kv)


def _attn_bwd(q, kv, do):
    t, d = q.shape
    nm = kv.shape[0]
    tm = _tile(t, 256, 16)

    def body(q_ref, kv_ref, do_ref, dq_ref, dkv_ref):
        _, vjp = jax.vjp(functools.partial(_attn, _DIFF), q_ref[...].astype(F32), kv_ref[...].astype(F32))
        dq, dkv = vjp(do_ref[...].astype(F32))
        dq_ref[...] = dq.astype(dq_ref.dtype)
        first = pl.program_id(0) == 0

        @pl.when(first)
        def _():
            dkv_ref[...] = dkv

        @pl.when(jnp.logical_not(first))
        def _():
            dkv_ref[...] += dkv

    row = pl.BlockSpec((tm, d), lambda i: (i, 0))
    full = pl.BlockSpec((nm, 2 * d), lambda i: (0, 0))
    return pl.pallas_call(
        body, name="xattn_bwd", grid=(t // tm,), in_specs=[row, full, row], out_specs=[row, full],
        out_shape=[jax.ShapeDtypeStruct((t, d), MXU_DTYPE), jax.ShapeDtypeStruct((nm, 2 * d), F32)],
        compiler_params=_params("arbitrary"),
    )(q, kv, do)


def _adamw(name, w, g, m, v):
    shape = w.shape
    cols = shape[-1]
    rows = w.size // cols
    w2, g2, m2, v2 = (a.reshape(rows, cols) for a in (w, g, m, v))
    tr = _tile(rows, max(8, (1 << 18) // cols // 8 * 8), 8)

    def body(w_ref, g_ref, m_ref, v_ref, d_ref, nm_ref, nv_ref):
        gv = g_ref[...]
        nm = ADAM_B1 * m_ref[...] + (1.0 - ADAM_B1) * gv
        nv = ADAM_B2 * v_ref[...] + (1.0 - ADAM_B2) * jnp.square(gv)
        m_hat = nm / (1.0 - ADAM_B1 ** ADAM_STEP)
        v_hat = nv / (1.0 - ADAM_B2 ** ADAM_STEP)
        d_ref[...] = -ADAM_LR * (m_hat / (jnp.sqrt(v_hat) + ADAM_EPS) + ADAM_WD * w_ref[...])
        nm_ref[...] = nm
        nv_ref[...] = nv

    blk = pl.BlockSpec((tr, cols), lambda i: (i, 0))
    out = jax.ShapeDtypeStruct((rows, cols), F32)
    d, nm, nv = pl.pallas_call(
        body, name=name, grid=(rows // tr,), in_specs=[blk] * 4, out_specs=[blk] * 3, out_shape=[out] * 3,
        compiler_params=_params("parallel"),
    )(w2, g2, m2, v2)
    return d.reshape(shape), nm.reshape(shape), nv.reshape(shape)


def _layer_fwd(x, mem, p, heads, dh, carry, late):
    wid = heads * dh
    sc = x.shape[1] - wid
    p, s, landed = dict(p), {"x0": x}, {}

    def arrived(name, brought):
        landed[name] = brought
        if name in late:
            p.update(late[name](brought))

    def mm(name, *args, **kwargs):
        if name not in carry:
            return _matmul(name, *args, **kwargs)
        out, brought = _matmul(name, *args, comm=carry[name], **kwargs)
        arrived(name, brought)
        return out

    s["h1"] = _rms_fwd("rms_mix", x, p["mix_norm"])
    s["pm"] = pm = mm("mm_mix_in", s["h1"], p["wmain"], "nn", F32)
    s["pba"] = mm("mm_mix_ba", s["h1"], p["wba"], "nn", F32)
    s["qkv"] = _conv_fwd("conv_gdn", pm, 0, p["gdn_conv"], 0, 3 * wid, F32)
    s["prep"], s["tinv"], brought = _gdn_prep_fwd(s["qkv"], s["pba"], p["a_log"], p["dt_bias"], heads, dh,
                                                  comm=carry.get("gdn_prep_fwd"))
    if brought:
        arrived("gdn_prep_fwd", brought)
    s["o"], s["states"] = _gdn_scan_fwd(*s["prep"], heads, dh)
    y_gdn = _gdn_post_fwd(s["o"], pm, 3 * wid, p["gdn_out_norm"], heads, dh)
    y_sc = _conv_fwd("conv_sc", pm, 4 * wid + sc, p["sc_conv"], 0, sc, MXU_DTYPE, xb=pm, xb_col=4 * wid + 2 * sc,
                     gate=pm, gate_col=4 * wid)
    s["ycat"] = jnp.concatenate([y_gdn, y_sc], axis=1)
    s["x1"] = x1 = mm("mm_mix_out", s["ycat"], p["wout"], "nn", F32, add=x)
    s["h2"] = _rms_fwd("rms_xattn", x1, p["xattn_norm"])
    s["q"] = mm("mm_xq", s["h2"], p["wq"], "nn", MXU_DTYPE)
    s["memn"] = _rms_fwd("rms_mem", mem, p["mem_norm"])
    s["kv"] = mm("mm_xkv", s["memn"], p["wkv"], "nn", MXU_DTYPE)
    s["ao"] = _attn_fwd(s["q"], s["kv"])
    s["x2"] = x2 = mm("mm_xo", s["ao"], p["wo"], "nn", F32, add=x1)
    s["h3"] = _rms_fwd("rms_ffn", x2, p["ffn_norm"])
    s["upre"] = mm("mm_ffn_up", s["h3"], p["wup"], "nn", MXU_DTYPE)
    s["ug"], s["uu"], s["act"] = _ffn_act_fwd(s["upre"], p["ffn_conv"])
    return mm("mm_ffn_down", s["act"], p["wdown"], "nn", F32, add=x2), s, landed, p


def _layer_bwd(dx3, dx3b, mem, s, p, heads, dh, reduce):
    wid = heads * dh
    sc = dx3.shape[1] - wid
    pm = s["pm"]
    g = {}

    mm = reduce.carried if reduce is not None else _matmul
    da = mm("mm_ffn_down_dx", dx3b, p["wdown"], "nt", MXU_DTYPE)
    g["wdown"] = mm("mm_ffn_down_dw", s["act"], dx3b, "tn", WIRE_DTYPE)
    du = _swiglu_bwd(s["ug"], s["uu"], da)
    dupre, _, _, g["ffn_conv"] = _conv_bwd("conv_ffn_bwd", s["upre"], 0, p["ffn_conv"], 0, du, 0,
                                           s["upre"].shape[1], MXU_DTYPE)
    dh3 = mm("mm_ffn_up_dx", dupre, p["wup"], "nt", F32)
    g["wup"] = mm("mm_ffn_up_dw", s["h3"], dupre, "tn", WIRE_DTYPE)
    dx2, dx2b, g["ffn_norm"] = _rms_bwd("rms_ffn_bwd", s["x2"], p["ffn_norm"], dh3, dx3)
    dao = mm("mm_xo_dx", dx2b, p["wo"], "nt", MXU_DTYPE)
    g["wo"] = mm("mm_xo_dw", s["ao"], dx2b, "tn", WIRE_DTYPE)
    dq, dkv = _attn_bwd(s["q"], s["kv"], dao)
    dh2 = mm("mm_xq_dx", dq, p["wq"], "nt", F32)
    g["wq"] = mm("mm_xq_dw", s["h2"], dq, "tn", WIRE_DTYPE)
    dmemn = mm("mm_xkv_dx", dkv, p["wkv"], "nt", F32)
    g["wkv"] = mm("mm_xkv_dw", s["memn"], dkv, "tn", WIRE_DTYPE)
    _, _, g["mem_norm"] = _rms_bwd("rms_mem_bwd", mem, p["mem_norm"], dmemn)
    dx1, dx1b, g["xattn_norm"] = _rms_bwd("rms_xattn_bwd", s["x1"], p["xattn_norm"], dh2, dx2)
    dycat = mm("mm_mix_out_dx", dx1b, p["wout"], "nt", F32)
    g["wout"] = mm("mm_mix_out_dw", s["ycat"], dx1b, "tn", WIRE_DTYPE)
    d_c, d_h, d_b, g["sc_conv"] = _conv_bwd("conv_sc_bwd", pm, 4 * wid + sc, p["sc_conv"], 0, dycat, wid, sc,
                                             MXU_DTYPE, xb=pm, xb_col=4 * wid + 2 * sc, gate=pm, gate_col=4 * wid)
    do, dz, g["gdn_out_norm"] = _gdn_post_bwd(s["o"], pm, 3 * wid, p["gdn_out_norm"], dycat, heads, dh)
    dprep = _gdn_scan_bwd(*s["prep"], s["states"], do, heads, dh)
    dqc, dkc, dvc, dpba, dal, ddt = _gdn_prep_bwd(s["qkv"], s["pba"], p["a_log"], p["dt_bias"], s["tinv"], *dprep,
                                                  heads, dh)
    g["a_log"], g["dt_bias"] = jnp.sum(dal, axis=0), jnp.sum(ddt, axis=0)
    dqkv, _, _, g["gdn_conv"] = _conv_bwd("conv_gdn_bwd", pm, 0, p["gdn_conv"], 0,
                                          jnp.concatenate([dqc, dkc, dvc], axis=1), 0, 3 * wid, MXU_DTYPE)
    dpm = jnp.concatenate([dqkv, dz, d_b, d_c, d_h], axis=1)
    dpba = dpba.astype(MXU_DTYPE)
    dh1 = mm("mm_mix_in_dx", dpm, p["wmain"], "nt", F32)
    dh1 = mm("mm_mix_ba_dx", dpba, p["wba"], "nt", F32, add=dh1)
    g["wmain"] = mm("mm_mix_in_dw", s["h1"], dpm, "tn", WIRE_DTYPE)
    g["wba"] = mm("mm_mix_ba_dw", s["h1"], dpba, "tn", WIRE_DTYPE)
    dx0, dx0b, g["mix_norm"] = _rms_bwd("rms_mix_bwd", s["x0"], p["mix_norm"], dh1, dx1)
    return dx0, dx0b, g


def _input_projection(win, heads, dh):
    wid = heads * dh
    return {"wmain": jnp.concatenate([win[:, :4 * wid], win[:, 4 * wid + 2 * heads:]], axis=1),
            "wba": jnp.pad(win[:, 4 * wid:4 * wid + 2 * heads], ((0, 0), (0, LANES - 2 * heads)))}


def _square_projections(wout, wq, wk, wv, wo, wdown):
    return {"wout": wout, "wq": wq, "wkv": jnp.concatenate([wk, wv], axis=1), "wo": wo, "wdown": wdown}


_ANY = pl.BlockSpec(memory_space=pl.ANY)
_VMEM = pl.BlockSpec(memory_space=pltpu.VMEM)


def _mesh_pos():
    return lax.axis_index("x"), lax.axis_index("y"), lax.axis_index("c")


def _other_chips(x, y):
    return [(1 - x, y), (x, 1 - y), (1 - x, 1 - y)]


def _push(src, dst, sems, k, to):
    return pltpu.make_async_remote_copy(src_ref=src, dst_ref=dst, send_sem=sems[0].at[k], recv_sem=sems[1].at[k],
                                        device_id=to, device_id_type=MESH)


def _sem_pairs(n):
    return [pltpu.SemaphoreType.DMA((n,)), pltpu.SemaphoreType.DMA((n,))]


class _Comm:
    def __init__(self, ins, outs, n_sems, start, finish, aliases=None):
        self.ins, self.outs, self.n_sems, self.start, self.finish = list(ins), list(outs), n_sems, start, finish
        self.aliases = aliases or {}


def _run_comm(name, comm):
    n_in, n_out = len(comm.ins), len(comm.outs)

    def body(*refs):
        parts = (refs[:n_in], refs[n_in:n_in + n_out], refs[n_in + n_out:])
        comm.start(*parts)
        comm.finish(*parts)

    return pl.pallas_call(
        body, name=name, in_specs=[_ANY] * n_in, out_specs=[_ANY] * n_out, out_shape=comm.outs,
        scratch_shapes=_sem_pairs(comm.n_sems), input_output_aliases=comm.aliases,
    )(*comm.ins)


def _allgather_comm(srcs):
    n = len(srcs)

    def first(src, out, sems):
        x, y, c = _mesh_pos()
        own, sends = [], []
        for t in range(n):
            half = src[t].shape[0] // 2
            mine = pl.ds(c * half, half)
            own.append(_push(src[t], out[t].at[2 * x + y], sems, 7 * t + 6, (x, y, 1 - c)))
            sends += [_push(src[t].at[mine], out[t].at[2 * x + y, mine], sems, 7 * t + k, (cx, cy, c))
                      for k, (cx, cy) in enumerate(_other_chips(x, y))]
        return own, sends

    def start(src, out, sems):
        own, sends = first(src, out, sems)
        for cp in own + sends:
            cp.start()

    def finish(src, out, sems):
        x, y, c = _mesh_pos()
        sibling = (x, y, 1 - c)
        own, sends = first(src, out, sems)
        fwds, relayed = [], []
        for t in range(n):
            half = src[t].shape[0] // 2
            for k, (cx, cy) in enumerate(_other_chips(x, y)):
                here = out[t].at[2 * cx + cy, pl.ds(c * half, half)]
                there = out[t].at[2 * cx + cy, pl.ds((1 - c) * half, half)]
                _push(here, here, sems, 7 * t + k, sibling).wait_recv()
                fwds.append(_push(here, here, sems, 7 * t + 3 + k, sibling))
                fwds[-1].start()
                relayed.append(_push(there, there, sems, 7 * t + 3 + k, sibling))
        for cp in relayed + own:
            cp.wait_recv()
        for cp in own + sends + fwds:
            cp.wait_send()

    return _Comm(srcs, [jax.ShapeDtypeStruct((N_CHIPS,) + s.shape, s.dtype) for s in srcs], 7 * n, start, finish)


def _start_wait(build):
    def start(src, out, sems):
        for cp in build(src, out, sems):
            cp.start()

    def finish(src, out, sems):
        for cp in build(src, out, sems):
            cp.wait()

    return start, finish


def _sibling_exchange_comm(bufs):
    def build(src, out, sems):
        x, y, c = _mesh_pos()
        return [_push(src[t].at[1 - c], out[t], sems, t, (x, y, 1 - c)) for t in range(len(bufs))]

    start, finish = _start_wait(build)
    return _Comm(bufs, [jax.ShapeDtypeStruct(b.shape[1:], b.dtype) for b in bufs], len(bufs), start, finish)


def _chip_exchange_comm(bufs):
    def build(src, out, sems):
        x, y, c = _mesh_pos()
        return [_push(src[t].at[2 * cx + cy], out[t].at[k], sems, 3 * t + k, (cx, cy, c))
                for t in range(len(bufs)) for k, (cx, cy) in enumerate(_other_chips(x, y))]

    start, finish = _start_wait(build)
    return _Comm(bufs, [jax.ShapeDtypeStruct((3,) + b.shape[1:], b.dtype) for b in bufs], 3 * len(bufs), start, finish)


def _sibling_share_comm(bufs):
    def build(src, out, sems):
        x, y, c = _mesh_pos()
        return [_push(src[t].at[c], out[t].at[c], sems, t, (x, y, 1 - c)) for t in range(len(bufs))]

    start, finish = _start_wait(build)
    return _Comm(bufs, [jax.ShapeDtypeStruct(b.shape, b.dtype) for b in bufs], len(bufs), start, finish,
                 aliases={t: t for t in range(len(bufs))})


def _allreduce_small(v):
    r, lanes = v.shape

    def body(v_ref, sum_ref, gath, send_sems, recv_sems):
        x, y, c = _mesh_pos()
        me = 4 * x + 2 * y + c
        gath[me] = v_ref[...]
        copies = []
        for rel in range(1, N_DEV):
            peer = tuple(1 - p if (rel >> b) & 1 else p for p, b in ((x, 2), (y, 1), (c, 0)))
            copies.append(pltpu.make_async_remote_copy(
                src_ref=v_ref, dst_ref=gath.at[me], send_sem=send_sems.at[rel - 1], recv_sem=recv_sems.at[rel - 1],
                device_id=peer, device_id_type=MESH))
        for cp in copies:
            cp.start()
        for cp in copies:
            cp.wait()
        total = gath[0]
        for k in range(1, N_DEV):
            total = total + gath[k]
        sum_ref[...] = total

    return pl.pallas_call(
        body, name="allreduce_small", in_specs=[_VMEM], out_specs=_VMEM,
        out_shape=jax.ShapeDtypeStruct((r, lanes), F32),
        scratch_shapes=[pltpu.VMEM((N_DEV, r, lanes), F32)] + _sem_pairs(N_DEV - 1),
        compiler_params=pltpu.CompilerParams(vmem_limit_bytes=VMEM_LIMIT),
    )(v)


def _sum_tile(rows, width):
    return _tile(rows, max(16, (1 << 19) // width // 16 * 16), 16)


def _sum_sibling(x, recv, core):
    _, n, w = x.shape
    tr = _sum_tile(n, w)

    def body(idx_ref, x_ref, r_ref, o_ref):
        o_ref[...] = (x_ref[...].astype(F32) + r_ref[...].astype(F32)).astype(o_ref.dtype)

    row = pl.BlockSpec((tr, w), lambda i, idx: (i, 0))
    return pl.pallas_call(
        body, name="rs_sum_sibling",
        grid_spec=pltpu.PrefetchScalarGridSpec(
            num_scalar_prefetch=1, grid=(n // tr,),
            in_specs=[pl.BlockSpec((None, tr, w), lambda i, idx: (idx[0], i, 0)), row], out_specs=row),
        out_shape=jax.ShapeDtypeStruct((n, w), x.dtype), compiler_params=_params("parallel"),
    )(core.reshape(1), x, recv)


def _sum_chips(s, recv, chip, core):
    _, m, w = s.shape
    tr = _sum_tile(m, w)

    def body(idx_ref, s_ref, r0_ref, r1_ref, r2_ref, o_ref):
        o_ref[...] = ((s_ref[...].astype(F32) + r0_ref[...].astype(F32)) + r1_ref[...].astype(F32)) \
            + r2_ref[...].astype(F32)

    def got(k):
        return pl.BlockSpec((None, tr, w), lambda i, idx: (k, i, 0))

    return pl.pallas_call(
        body, name="rs_sum_chips",
        grid_spec=pltpu.PrefetchScalarGridSpec(
            num_scalar_prefetch=1, grid=(m // tr,),
            in_specs=[pl.BlockSpec((None, tr, w), lambda i, idx: (idx[0], i, 0)), got(0), got(1), got(2)],
            out_specs=pl.BlockSpec((None, tr, w), lambda i, idx: (idx[1], i, 0))),
        out_shape=jax.ShapeDtypeStruct((2, m, w), F32), compiler_params=_params("parallel"),
    )(jnp.stack([chip, core]), s, recv, recv, recv)


_ROWS = ("w_mix_out", "w_xq", "w_xk", "w_xv", "w_xo", "w_ffn_down")
_CONVS = ("gdn_conv", "sc_conv", "ffn_conv")
_REPLICATED = ("mix_norm", "gdn_a_log", "gdn_dt_bias", "gdn_out_norm", "xattn_norm", "mem_norm", "ffn_norm",
               "final_norm")
_WEIGHTS = ("mix_norm", "w_mix_in", "gdn_conv", "gdn_a_log", "gdn_dt_bias", "gdn_out_norm", "sc_conv", "w_mix_out",
            "xattn_norm", "mem_norm", "w_xq", "w_xk", "w_xv", "w_xo", "ffn_norm", "w_ffn_up", "ffn_conv",
            "w_ffn_down", "final_norm")


def _pad_rows(flat, groups):
    unit = groups * 16 * LANES
    p = flat.shape[-1]
    pad = -p % unit
    if pad:
        flat = jnp.pad(flat, [(0, 0)] * (flat.ndim - 1) + [(0, pad)])
    return flat.reshape(flat.shape[:-1] + (groups, (p + pad) // (groups * LANES), LANES))


def _split_flat(flat, shapes):
    out, off = [], 0
    for shp in shapes:
        size = 1
        for n in shp:
            size *= n
        out.append(flat[..., off:off + size].reshape(flat.shape[:-1] + tuple(shp)))
        off += size
    return out


def _halves_by_chip(g):
    _, rows, w = g.shape
    return g.astype(WIRE_DTYPE).reshape(N_CHIPS, 2, rows // 2, w).transpose(1, 0, 2, 3)


def _by_chip_columns(g):
    rows, cols = g.shape
    return g.reshape(rows, N_CHIPS, cols // N_CHIPS).transpose(1, 0, 2)


class _ReduceScatter:
    STAGES = ("mm_ffn_down_dx", "mm_ffn_up_dx", "mm_ffn_up_dw", "mm_mix_in_dx")

    def __init__(self, bufs, chip, core):
        self.bufs, self.chip, self.core = list(bufs), chip, core
        self.sums = self.from_chips = self.reduced = self.result = None

    def comm(self, stage):
        if stage == self.STAGES[0]:
            return _sibling_exchange_comm(self.bufs)
        if stage == self.STAGES[1]:
            return _chip_exchange_comm(self.sums[-1:])
        if stage == self.STAGES[2]:
            return _chip_exchange_comm(self.sums[:-1])
        return _sibling_share_comm(self.reduced)

    def landed(self, stage, outs):
        if stage == self.STAGES[0]:
            self.sums = [_sum_sibling(b.reshape(2, -1, b.shape[-1]), r.reshape(-1, r.shape[-1]), self.core)
                         .reshape(r.shape) for b, r in zip(self.bufs, outs)]
        elif stage == self.STAGES[1]:
            self.from_chips = list(outs)
        elif stage == self.STAGES[2]:
            self.reduced = [_sum_chips(s, r, self.chip, self.core)
                            for s, r in zip(self.sums, list(outs) + self.from_chips)]
        else:
            self.result = list(outs)

    def carried(self, name, *args, **kwargs):
        if name not in self.STAGES:
            return _matmul(name, *args, **kwargs)
        out, outs = _matmul(name, *args, comm=self.comm(name), **kwargs)
        self.landed(name, outs)
        return out

    def run_alone(self):
        for stage, name in zip(self.STAGES, ("rs_sibling_exchange", "rs_chip_exchange_rows", "rs_chip_exchange_cols",
                                             "rs_sibling_share")):
            self.landed(stage, _run_comm(name, self.comm(stage)))


def kernel(x, mem, mix_norm, w_mix_in, gdn_conv, gdn_a_log, gdn_dt_bias, gdn_out_norm, sc_conv, w_mix_out, xattn_norm, mem_norm, w_xq, w_xk, w_xv, w_xo, ffn_norm, w_ffn_up, ffn_conv, w_ffn_down, final_norm, loss_target, m_mix_norm, m_w_mix_in, m_gdn_conv, m_gdn_a_log, m_gdn_dt_bias, m_gdn_out_norm, m_sc_conv, m_w_mix_out, m_xattn_norm, m_mem_norm, m_w_xq, m_w_xk, m_w_xv, m_w_xo, m_ffn_norm, m_w_ffn_up, m_ffn_conv, m_w_ffn_down, m_final_norm, v_mix_norm, v_w_mix_in, v_gdn_conv, v_gdn_a_log, v_gdn_dt_bias, v_gdn_out_norm, v_sc_conv, v_w_mix_out, v_xattn_norm, v_mem_norm, v_w_xq, v_w_xk, v_w_xv, v_w_xo, v_ffn_norm, v_w_ffn_up, v_ffn_conv, v_w_ffn_down, v_final_norm):
    w = dict(zip(_WEIGHTS, (mix_norm, w_mix_in, gdn_conv, gdn_a_log, gdn_dt_bias, gdn_out_norm, sc_conv, w_mix_out,
                            xattn_norm, mem_norm, w_xq, w_xk, w_xv, w_xo, ffn_norm, w_ffn_up, ffn_conv, w_ffn_down,
                            final_norm)))
    m = dict(zip(_WEIGHTS, (m_mix_norm, m_w_mix_in, m_gdn_conv, m_gdn_a_log, m_gdn_dt_bias, m_gdn_out_norm, m_sc_conv,
                            m_w_mix_out, m_xattn_norm, m_mem_norm, m_w_xq, m_w_xk, m_w_xv, m_w_xo, m_ffn_norm,
                            m_w_ffn_up, m_ffn_conv, m_w_ffn_down, m_final_norm)))
    v = dict(zip(_WEIGHTS, (v_mix_norm, v_w_mix_in, v_gdn_conv, v_gdn_a_log, v_gdn_dt_bias, v_gdn_out_norm, v_sc_conv,
                            v_w_mix_out, v_xattn_norm, v_mem_norm, v_w_xq, v_w_xk, v_w_xv, v_w_xo, v_ffn_norm,
                            v_w_ffn_up, v_ffn_conv, v_w_ffn_down, v_final_norm)))
    core = lax.axis_index("c")
    chip = 2 * lax.axis_index("x") + lax.axis_index("y")
    depth, heads = gdn_a_log.shape
    dh = gdn_out_norm.shape[1]
    d, wid = x.shape[2], heads * dh

    row_sizes = [w[n].shape[1] for n in _ROWS]
    row_offs = [sum(row_sizes[:k]) for k in range(len(_ROWS))]
    src_in, src_up = w_mix_in.astype(WIRE_DTYPE), w_ffn_up.astype(WIRE_DTYPE)
    src_rows = jnp.concatenate([w[n] for n in _ROWS], axis=1).astype(WIRE_DTYPE)
    src_convs = _pad_rows(jnp.concatenate([w[n].reshape(-1) for n in _CONVS]), 2)
    g_in, g_convs = _run_comm("allgather_first", _allgather_comm([src_in[0], src_convs]))
    conv_full = {n: jnp.moveaxis(part, 0, 2).reshape(depth, part.shape[2], -1)
                 for n, part in zip(_CONVS, _split_flat(g_convs.reshape(N_CHIPS, -1), [w[n].shape for n in _CONVS]))}
    side_by_side = lambda g: jnp.concatenate([g[j] for j in range(N_CHIPS)], axis=1)

    def from_rows(brought):
        g_rows, = brought
        return _square_projections(*[jnp.concatenate([g_rows[j, off:off + size] for j in range(N_CHIPS)], axis=0)
                                     for off, size in zip(row_offs, row_sizes)])

    xl, mem_l = x[0], mem[0]
    layers, saved, g_rows = [], [], None
    for l in range(depth):
        p = {"mix_norm": mix_norm[l], "xattn_norm": xattn_norm[l], "mem_norm": mem_norm[l], "ffn_norm": ffn_norm[l],
             "gdn_out_norm": gdn_out_norm[l], "a_log": gdn_a_log[l].reshape(heads, 1, 1),
             "dt_bias": gdn_dt_bias[l].reshape(heads, 1, 1), "gdn_conv": conv_full["gdn_conv"][l],
             "sc_conv": conv_full["sc_conv"][l], "ffn_conv": conv_full["ffn_conv"][l]}
        p.update(_input_projection(side_by_side(g_in), heads, dh))
        carry = {"gdn_prep_fwd": _allgather_comm([src_up[l]])}
        late = {"gdn_prep_fwd": lambda brought: {"wup": brought[0]}}
        if l == 0:
            carry["mm_mix_in"], late["mm_mix_in"] = _allgather_comm([src_rows[0]]), from_rows
        else:
            p.update(from_rows(g_rows))
        if l + 1 < depth:
            carry["mm_ffn_up"] = _allgather_comm([src_rows[l + 1]])
            carry["mm_ffn_down"] = _allgather_comm([src_in[l + 1]])
        xl, s, landed, p = _layer_fwd(xl, mem_l, p, heads, dh, carry, late)
        layers.append(p)
        saved.append(s)
        if l + 1 < depth:
            g_rows, (g_in,) = landed["mm_ffn_up"], landed["mm_ffn_down"]
    loss_row, dx, dxb, g_final = _final_loss(xl, final_norm, loss_target[0])

    def by_chip(g):
        g_win = jnp.concatenate([g["wmain"][:, :4 * wid], g["wba"][:, :2 * heads], g["wmain"][:, 4 * wid:]], axis=1)
        parts = (g["wout"], g["wq"], g["wkv"][:, :d], g["wkv"][:, d:], g["wo"], g["wdown"])
        by_rows = [p.reshape(N_CHIPS, p.shape[0] // N_CHIPS, p.shape[1]) for p in parts]
        return [_halves_by_chip(_by_chip_columns(g_win)), _halves_by_chip(_by_chip_columns(g["wup"])),
                _halves_by_chip(jnp.concatenate(by_rows, axis=1))]

    per_layer, shards, reduce = [None] * depth, [None] * depth, None
    for l in reversed(range(depth)):
        dx, dxb, per_layer[l] = _layer_bwd(dx, dxb, mem_l, saved[l], layers[l], heads, dh, reduce)
        if reduce is not None:
            shards[l + 1] = reduce.result
        reduce = _ReduceScatter(by_chip(per_layer[l]), chip, core)
    reduce.run_alone()
    shards[0] = reduce.result
    by_layer = [[s[t].reshape(-1, s[t].shape[-1]) for s in shards] for t in range(3)]
    grad = {"w_mix_in": jnp.stack(by_layer[0]), "w_ffn_up": jnp.stack(by_layer[1])}
    for n, off, size in zip(_ROWS, row_offs, row_sizes):
        grad[n] = jnp.stack([r[off:off + size] for r in by_layer[2]])

    stack = lambda k: jnp.stack([g[k] for g in per_layer])
    small_g = {"mix_norm": stack("mix_norm"), "gdn_a_log": stack("a_log").reshape(depth, heads),
               "gdn_dt_bias": stack("dt_bias").reshape(depth, heads), "gdn_out_norm": stack("gdn_out_norm"),
               "xattn_norm": stack("xattn_norm"), "mem_norm": stack("mem_norm"), "ffn_norm": stack("ffn_norm"),
               "final_norm": g_final, "gdn_conv": stack("gdn_conv"), "sc_conv": stack("sc_conv"),
               "ffn_conv": stack("ffn_conv")}
    names = _REPLICATED + _CONVS
    small = jnp.concatenate([small_g[n].reshape(-1) for n in names] + [loss_row[0, :1]])
    small_sum = _allreduce_small(_pad_rows(small, 1)[0]).reshape(-1)
    parts = _split_flat(small_sum, [small_g[n].shape for n in names] + [(1,)])
    g_rep = dict(zip(_REPLICATED, parts[:len(_REPLICATED)]))
    for n, part in zip(_CONVS, parts[len(_REPLICATED):-1]):
        grad[n] = lax.dynamic_slice_in_dim(part, chip * w[n].shape[2], w[n].shape[2], axis=2)
    loss = parts[-1][0]

    delta, new_m, new_v = {}, {}, {}
    for n in ("w_mix_in", "w_ffn_up") + _ROWS + _CONVS:
        delta[n], new_m[n], new_v[n] = _adamw("adamw_" + n, w[n], grad[n], m[n], v[n])
    pack_rep = lambda t: _pad_rows(jnp.concatenate([t[n].reshape(-1) for n in _REPLICATED]), 1)[0]
    outs = _adamw("adamw_replicated", pack_rep(w), pack_rep(g_rep), pack_rep(m), pack_rep(v))
    shapes = [w[n].shape for n in _REPLICATED]
    for tgt, packed_out in zip((delta, new_m, new_v), outs):
        tgt.update(zip(_REPLICATED, _split_flat(packed_out.reshape(-1), shapes)))
    grad.update(g_rep)
    return (loss, dx[None], *[grad[n] for n in _WEIGHTS], *[delta[n] for n in _WEIGHTS],
            *[new_m[n] for n in _WEIGHTS], *[new_v[n] for n in _WEIGHTS])
```

```python
import functools

import jax
import jax.numpy as jnp
from jax import lax
from jax.experimental import pallas as pl
from jax.experimental.pallas import tpu as pltpu

F32 = jnp.float32
MXU_DTYPE = jnp.bfloat16
WIRE_DTYPE = jnp.bfloat16
SOLVE_PRECISION = lax.Precision.HIGH
EPS = 1e-6
CHUNK = 64
XATTN_HEADS = 4
LANES = 128
HALO = 16
EW_ROWS, EW_COLS = 256, 2816
VMEM_LIMIT = 52 * 1024 * 1024
ADAM_LR, ADAM_B1, ADAM_B2, ADAM_EPS, ADAM_WD, ADAM_STEP = 0.001, 0.9, 0.999, 1e-08, 0.01, 10
MESH = pl.DeviceIdType.MESH
N_CHIPS = 4
N_DEV = 8

_DIMS = {
    "nn": (((1,), (0,)), ((), ())),
    "nt": (((1,), (1,)), ((), ())),
    "tn": (((0,), (0,)), ((), ())),
}


def _tile(n, pref, align=LANES):
    if n <= pref:
        return n
    t = (pref // align) * align
    while t >= align:
        if n % t == 0:
            return t
        t -= align
    return n


def _params(*sem):
    return pltpu.CompilerParams(dimension_semantics=sem, vmem_limit_bytes=VMEM_LIMIT)


def _dot(a, b, form, hi=False):
    (ca, cb), _ = _DIMS[form]
    dims = (((ca[0] + 1,), (cb[0] + 1,)), ((0,), (0,))) if a.ndim == 3 else _DIMS[form]
    if hi:
        return lax.dot_general(a.astype(F32), b.astype(F32), dims, precision=SOLVE_PRECISION,
                               preferred_element_type=F32)
    return lax.dot_general(a.astype(MXU_DTYPE), b.astype(MXU_DTYPE), dims, preferred_element_type=F32)


@functools.partial(jax.custom_vjp, nondiff_argnums=(2, 3))
def _dot_d(a, b, form, hi):
    return _dot(a, b, form, hi)


def _dot_d_fwd(a, b, form, hi):
    return _dot(a, b, form, hi), (a, b)


def _dot_d_bwd(form, hi, res, g):
    a, b = res
    if form == "nn":
        da, db = _dot_d(g, b, "nt", hi), _dot_d(a, g, "tn", hi)
    elif form == "nt":
        da, db = _dot_d(g, b, "nn", hi), _dot_d(g, a, "tn", hi)
    else:
        da, db = _dot_d(b, g, "nt", hi), _dot_d(a, g, "nn", hi)
    return da.astype(a.dtype), db.astype(b.dtype)


_dot_d.defvjp(_dot_d_fwd, _dot_d_bwd)


def _tri_inv_impl(a, mmh):
    c = a.shape[-1]
    r = lax.broadcasted_iota(jnp.int32, (c, c), 0)
    s = lax.broadcasted_iota(jnp.int32, (c, c), 1)
    eye = (r == s).astype(F32)
    diag_blk = (r // 16) == (s // 16)
    d = jnp.where(diag_blk, a, 0.0)
    low = a - d
    d2 = mmh(d, d)
    d4 = mmh(d2, d2)
    d8 = mmh(d4, d4)
    td = mmh(mmh(mmh(eye - d, eye + d2), eye + d4), eye + d8)
    n = mmh(td, low)
    acc = eye - n
    p = n
    pw = 1
    while 2 * pw < c // 16:
        p = mmh(p, p)
        pw *= 2
        acc = mmh(acc, eye + p)
    return mmh(acc, td)


def _mmh_plain(a, b):
    return _dot(a, b, "nn", True)


@jax.custom_vjp
def _tri_inv_known(a, t):
    return t


def _tri_inv_known_fwd(a, t):
    return t, t


def _tri_inv_known_bwd(t, g):
    return -_dot(_dot(t, g, "tn", True), t, "nt", True), jnp.zeros_like(t)


_tri_inv_known.defvjp(_tri_inv_known_fwd, _tri_inv_known_bwd)


class _Ops:
    def __init__(self, diff, tinv=None):
        self.diff, self.tinv = diff, tinv

    def mm(self, a, b, form="nn"):
        return _dot_d(a, b, form, False) if self.diff else _dot(a, b, form, False)

    def mmh(self, a, b, form="nn"):
        return _dot_d(a, b, form, True) if self.diff else _dot(a, b, form, True)

    def tri_inv(self, a):
        return _tri_inv_known(a, self.tinv) if self.diff else _tri_inv_impl(a, _mmh_plain)


_PLAIN = _Ops(False)
_DIFF = _Ops(True)


def _sigmoid(x):
    return 1.0 / (1.0 + jnp.exp(-x))


def _silu(x):
    return x * _sigmoid(x)


def _softplus(x):
    return jnp.maximum(x, 0.0) + jnp.log(1.0 + jnp.exp(-jnp.abs(x)))


def _rms(x, g):
    return x * lax.rsqrt(jnp.mean(x * x, axis=-1, keepdims=True) + EPS) * g


def _matmul_tiles(m, n, k, form):
    if k <= 2048:
        return _tile(m, 1024), _tile(n, 1408), k
    if k <= 8192:
        return _tile(m, 512 if form == "tn" else 1024), _tile(n, 512), k
    return _tile(m, 1024), _tile(n, 1024), _tile(k, 2816)


def _matmul(name, a, b, form, out_dtype, add=None, comm=None):
    b_shape = b.shape if b.ndim == 2 else (b.shape[1], N_CHIPS * b.shape[2])
    if form == "nn":
        (m, k), (k2, n) = a.shape, b_shape
    elif form == "nt":
        (m, k), (n, k2) = a.shape, b_shape
    else:
        (k, m), (k2, n) = a.shape, b_shape
    assert k == k2, (name, a.shape, b.shape, form)
    tm, tn, tk = _matmul_tiles(m, n, k, form)
    if b.ndim == 3:
        assert form != "tn", name
        tn, tk = (_tile(b.shape[2], tn), tk) if form == "nn" else (tn, _tile(b.shape[2], tk))
    nk = k // tk
    out_bytes = tm * tn * (jnp.dtype(out_dtype).itemsize + (4 if add is not None else 0))
    vmem = 2 * (tm * tk * a.dtype.itemsize + tk * tn * b.dtype.itemsize + out_bytes) + (tm * tn * 4 if nk > 1 else 0)
    assert vmem <= VMEM_LIMIT, (name, tm, tn, tk, vmem)
    if form == "nn":
        a_spec = pl.BlockSpec((tm, tk), lambda i, j, kk: (i, kk))
        b_spec = pl.BlockSpec((tk, tn), lambda i, j, kk: (kk, j))
    elif form == "nt":
        a_spec = pl.BlockSpec((tm, tk), lambda i, j, kk: (i, kk))
        b_spec = pl.BlockSpec((tn, tk), lambda i, j, kk: (j, kk))
    else:
        a_spec = pl.BlockSpec((tk, tm), lambda i, j, kk: (kk, i))
        b_spec = pl.BlockSpec((tk, tn), lambda i, j, kk: (kk, j))
    if form != "tn" and nk == 1 and k > 2048:
        a_spec = pl.BlockSpec((tm, tk), lambda i, j, kk: (i, kk), pipeline_mode=pl.Buffered(1))
    if b.ndim == 3:
        per = b.shape[2] // (tn if form == "nn" else tk)
        if form == "nn":
            b_spec = pl.BlockSpec((None, tk, tn), lambda i, j, kk: (j // per, kk, j % per))
        else:
            b_spec = pl.BlockSpec((None, tn, tk), lambda i, j, kk: (kk // per, j, kk % per))
    o_spec = pl.BlockSpec((tm, tn), lambda i, j, kk: (i, j))
    has_add = add is not None
    grid = (m // tm, n // tn, nk)
    n_in = 3 if has_add else 2
    c_in, c_out = (len(comm.ins), len(comm.outs)) if comm is not None else (0, 0)

    def body(*refs):
        a_ref, b_ref = refs[0], refs[1]
        add_ref = refs[2] if has_add else None
        o_ref = refs[n_in + c_in]
        pids = [pl.program_id(ax) for ax in range(3)]
        if comm is not None:
            comm_refs = (refs[n_in:n_in + c_in], refs[n_in + c_in + 1:n_in + c_in + 1 + c_out], refs[-2:])

            @pl.when(jnp.logical_and(jnp.logical_and(pids[0] == 0, pids[1] == 0), pids[2] == 0))
            def _():
                comm.start(*comm_refs)

        def finish(acc):
            if has_add:
                acc = acc + add_ref[...].astype(F32)
            o_ref[...] = acc.astype(o_ref.dtype)

        p = _dot(a_ref[...], b_ref[...], form)
        if nk == 1:
            finish(p)
        else:
            acc_ref = refs[n_in + c_in + 1 + c_out]

            @pl.when(pids[2] == 0)
            def _():
                acc_ref[...] = p

            @pl.when(pids[2] > 0)
            def _():
                acc_ref[...] += p

            @pl.when(pids[2] == nk - 1)
            def _():
                finish(acc_ref[...])

        if comm is not None:
            @pl.when(jnp.logical_and(jnp.logical_and(pids[0] == grid[0] - 1, pids[1] == grid[1] - 1),
                                     pids[2] == grid[2] - 1))
            def _():
                comm.finish(*comm_refs)

    acc_scratch = [pltpu.VMEM((tm, tn), F32)] if nk > 1 else []
    if comm is None:
        return pl.pallas_call(
            body, name=name, grid=grid, in_specs=[a_spec, b_spec] + ([o_spec] if has_add else []), out_specs=o_spec,
            out_shape=jax.ShapeDtypeStruct((m, n), out_dtype), scratch_shapes=acc_scratch,
            compiler_params=_params("parallel", "parallel", "arbitrary"),
        )(*((a, b, add) if has_add else (a, b)))
    outs = pl.pallas_call(
        body, name=name, grid=grid, in_specs=[a_spec, b_spec] + ([o_spec] if has_add else []) + [_ANY] * c_in,
        out_specs=[o_spec] + [_ANY] * c_out, out_shape=[jax.ShapeDtypeStruct((m, n), out_dtype)] + list(comm.outs),
        scratch_shapes=acc_scratch + _sem_pairs(comm.n_sems),
        input_output_aliases={n_in + i: 1 + o for i, o in comm.aliases.items()},
        compiler_params=_params("arbitrary", "arbitrary", "arbitrary"),
    )(*((a, b, add) if has_add else (a, b)), *comm.ins)
    return outs[0], list(outs[1:])


def _rms_fwd(name, x, g):
    t, d = x.shape
    tm = _tile(t, 512, 16)

    def body(x_ref, g_ref, o_ref):
        o_ref[...] = _rms(x_ref[...], g_ref[...]).astype(o_ref.dtype)

    return pl.pallas_call(
        body, name=name, grid=(t // tm,),
        in_specs=[pl.BlockSpec((tm, d), lambda i: (i, 0)), pl.BlockSpec((1, d), lambda i: (0, 0))],
        out_specs=pl.BlockSpec((tm, d), lambda i: (i, 0)),
        out_shape=jax.ShapeDtypeStruct((t, d), MXU_DTYPE), compiler_params=_params("parallel"),
    )(x, g.reshape(1, d))


def _rms_bwd(name, x, g, dh, dres=None):
    t, d = x.shape
    tm = _tile(t, 256, 16)
    has_res = dres is not None

    def body(*refs):
        x_ref, g_ref, dh_ref = refs[:3]
        dres_ref = refs[3] if has_res else None
        dx_ref, dxb_ref, dg_ref = refs[-3:]
        _, vjp = jax.vjp(_rms, x_ref[...], g_ref[...])
        dx, dg = vjp(dh_ref[...].astype(F32))
        if has_res:
            dx = dx + dres_ref[...]
        dx_ref[...] = dx
        dxb_ref[...] = dx.astype(dxb_ref.dtype)
        first = pl.program_id(0) == 0

        @pl.when(first)
        def _():
            dg_ref[...] = dg

        @pl.when(jnp.logical_not(first))
        def _():
            dg_ref[...] += dg

    row = pl.BlockSpec((tm, d), lambda i: (i, 0))
    vec = pl.BlockSpec((1, d), lambda i: (0, 0))
    dx, dxb, dg = pl.pallas_call(
        body, name=name, grid=(t // tm,),
        in_specs=[row, vec, row] + ([row] if has_res else []), out_specs=[row, row, vec],
        out_shape=[jax.ShapeDtypeStruct((t, d), F32), jax.ShapeDtypeStruct((t, d), MXU_DTYPE),
                   jax.ShapeDtypeStruct((1, d), F32)],
        compiler_params=_params("arbitrary"),
    )(*((x, g.reshape(1, d), dh) + ((dres,) if has_res else ())))
    return dx, dxb, dg.reshape(d)


def _final_loss(x, g, target):
    t, d = x.shape
    tm = _tile(t, 256, 16)

    def body(x_ref, g_ref, t_ref, loss_ref, dx_ref, dxb_ref, dg_ref):
        y, vjp = jax.vjp(_rms, x_ref[...], g_ref[...])
        err = y - t_ref[...]
        dx, dg = vjp(err * (1.0 / d))
        dx_ref[...] = dx
        dxb_ref[...] = dx.astype(dxb_ref.dtype)
        part = jnp.zeros((1, LANES), F32) + 0.5 * jnp.sum(jnp.mean(err * err, axis=-1, keepdims=True))
        first = pl.program_id(0) == 0

        @pl.when(first)
        def _():
            dg_ref[...] = dg
            loss_ref[...] = part

        @pl.when(jnp.logical_not(first))
        def _():
            dg_ref[...] += dg
            loss_ref[...] += part

    row = pl.BlockSpec((tm, d), lambda i: (i, 0))
    vec = pl.BlockSpec((1, d), lambda i: (0, 0))
    loss, dx, dxb, dg = pl.pallas_call(
        body, name="final_loss", grid=(t // tm,), in_specs=[row, vec, row],
        out_specs=[pl.BlockSpec((1, LANES), lambda i: (0, 0)), row, row, vec],
        out_shape=[jax.ShapeDtypeStruct((1, LANES), F32), jax.ShapeDtypeStruct((t, d), F32),
                   jax.ShapeDtypeStruct((t, d), MXU_DTYPE), jax.ShapeDtypeStruct((1, d), F32)],
        compiler_params=_params("arbitrary"),
    )(x, g.reshape(1, d), target)
    return loss, dx, dxb, dg.reshape(d)


def _conv_taps(x_ext, w, rows):
    kk = w.shape[0]
    y = x_ext[HALO:] * w[kk - 1:kk, :]
    for j in range(kk - 1):
        y = y + pltpu.roll(x_ext, kk - 1 - j, axis=0)[HALO:] * w[j:j + 1, :]
    return y


def _col_specs(tm, tn, col0, t_rows):
    assert col0 % tn == 0 and tm % HALO == 0
    c0 = col0 // tn
    per, last = tm // HALO, t_rows // HALO - 1
    tile = pl.BlockSpec((tm, tn), lambda j, i: (i, c0 + j))
    prev = pl.BlockSpec((HALO, tn), lambda j, i: (jnp.maximum(i * per - 1, 0), c0 + j))
    nxt = pl.BlockSpec((HALO, tn), lambda j, i: (jnp.minimum((i + 1) * per, last), c0 + j))
    return tile, prev, nxt


def _conv_fwd(name, xa, xa_col, w, w_col, ncols, out_dtype, xb=None, xb_col=0, gate=None, gate_col=0):
    t = xa.shape[0]
    kk = w.shape[0]
    tm, tn = _tile(t, EW_ROWS, HALO), _tile(ncols, EW_COLS)
    nrow = t // tm
    has_b, has_g = xb is not None, gate is not None

    def body(*refs):
        refs = list(refs)
        xa_ref, xap_ref = refs.pop(0), refs.pop(0)
        xb_ref, xbp_ref = (refs.pop(0), refs.pop(0)) if has_b else (None, None)
        w_ref = refs.pop(0)
        g_ref = refs.pop(0) if has_g else None
        o_ref = refs.pop(0)
        i = pl.program_id(1)
        x, xp = xa_ref[...].astype(F32), xap_ref[...].astype(F32)
        if has_b:
            x, xp = x * xb_ref[...].astype(F32), xp * xbp_ref[...].astype(F32)
        xp = jnp.where(i == 0, 0.0, xp)
        y = _conv_taps(jnp.concatenate([xp, x], axis=0), w_ref[...], tm)
        if has_g:
            y = y * g_ref[...].astype(F32)
        o_ref[...] = y.astype(o_ref.dtype)

    a_tile, a_prev, _ = _col_specs(tm, tn, xa_col, t)
    ins, specs = [xa, xa], [a_tile, a_prev]
    if has_b:
        b_tile, b_prev, _ = _col_specs(tm, tn, xb_col, t)
        ins, specs = ins + [xb, xb], specs + [b_tile, b_prev]
    assert w_col % tn == 0
    ins, specs = ins + [w], specs + [pl.BlockSpec((kk, tn), lambda j, i: (0, w_col // tn + j))]
    if has_g:
        ins, specs = ins + [gate], specs + [_col_specs(tm, tn, gate_col, t)[0]]
    return pl.pallas_call(
        body, name=name, grid=(ncols // tn, nrow), in_specs=specs,
        out_specs=pl.BlockSpec((tm, tn), lambda j, i: (i, j)),
        out_shape=jax.ShapeDtypeStruct((t, ncols), out_dtype), compiler_params=_params("parallel", "parallel"),
    )(*ins)


def _conv_bwd(name, xa, xa_col, w, w_col, dy, dy_col, ncols, dx_dtype, xb=None, xb_col=0, gate=None, gate_col=0):
    t = xa.shape[0]
    kk = w.shape[0]
    tm, tn = _tile(t, EW_ROWS, HALO), _tile(ncols, EW_COLS)
    nrow = t // tm
    has_b, has_g = xb is not None, gate is not None

    def body(*refs):
        refs = list(refs)
        xa_ref, xap_ref = refs.pop(0), refs.pop(0)
        xb_ref, xbp_ref = (refs.pop(0), refs.pop(0)) if has_b else (None, None)
        w_ref = refs.pop(0)
        dy_ref, dyn_ref = refs.pop(0), refs.pop(0)
        g_ref, gn_ref = (refs.pop(0), refs.pop(0)) if has_g else (None, None)
        dxa_ref = refs.pop(0)
        dxb_ref = refs.pop(0) if has_b else None
        dg_ref = refs.pop(0) if has_g else None
        dw_ref = refs.pop(0)
        i = pl.program_id(1)
        wv = w_ref[...]
        xa_t, xa_p = xa_ref[...].astype(F32), xap_ref[...].astype(F32)
        x, xp = xa_t, xa_p
        if has_b:
            xb_t = xb_ref[...].astype(F32)
            x, xp = x * xb_t, xp * xbp_ref[...].astype(F32)
        xp = jnp.where(i == 0, 0.0, xp)
        x_ext = jnp.concatenate([xp, x], axis=0)
        dyv, dyn = dy_ref[...].astype(F32), dyn_ref[...].astype(F32)
        if has_g:
            dg_ref[...] = (dyv * _conv_taps(x_ext, wv, tm)).astype(dg_ref.dtype)
            dyv, dyn = dyv * g_ref[...].astype(F32), dyn * gn_ref[...].astype(F32)
        dyn = jnp.where(i == nrow - 1, 0.0, dyn)
        dy_ext = jnp.concatenate([dyv, dyn], axis=0)
        dx = dyv * wv[kk - 1:kk, :]
        row8 = lax.broadcasted_iota(jnp.int32, (8, tn), 0)
        dw = jnp.where(row8 == kk - 1, jnp.sum(dyv * x, axis=0, keepdims=True), 0.0)
        for j in range(kk - 1):
            s = kk - 1 - j
            dx = dx + pltpu.roll(dy_ext, tm + HALO - s, axis=0)[:tm] * wv[j:j + 1, :]
            dwj = jnp.sum(dyv * pltpu.roll(x_ext, s, axis=0)[HALO:], axis=0, keepdims=True)
            dw = dw + jnp.where(row8 == j, dwj, 0.0)
        if has_b:
            dxa_ref[...] = (dx * xb_t).astype(dxa_ref.dtype)
            dxb_ref[...] = (dx * xa_t).astype(dxb_ref.dtype)
        else:
            dxa_ref[...] = dx.astype(dxa_ref.dtype)

        @pl.when(i == 0)
        def _():
            dw_ref[...] = dw

        @pl.when(i > 0)
        def _():
            dw_ref[...] += dw

    a_tile, a_prev, _ = _col_specs(tm, tn, xa_col, t)
    ins, specs = [xa, xa], [a_tile, a_prev]
    if has_b:
        b_tile, b_prev, _ = _col_specs(tm, tn, xb_col, t)
        ins, specs = ins + [xb, xb], specs + [b_tile, b_prev]
    assert w_col % tn == 0
    ins, specs = ins + [w], specs + [pl.BlockSpec((kk, tn), lambda j, i: (0, w_col // tn + j))]
    if dy.ndim == 3:
        nh, per, last = ncols // 2 // tn, tm // HALO, t // HALO - 1
        assert dy_col == 0 and nh * tn * 2 == ncols and not has_g, (name, dy.shape, tn)
        dy = dy.reshape(2 * t, ncols // 2)
        d_tile = pl.BlockSpec((tm, tn), lambda j, i: (j // nh * nrow + i, j % nh))
        d_next = pl.BlockSpec(
            (HALO, tn), lambda j, i: (j // nh * (last + 1) + jnp.minimum((i + 1) * per, last), j % nh))
    else:
        d_tile, _, d_next = _col_specs(tm, tn, dy_col, t)
    ins, specs = ins + [dy, dy], specs + [d_tile, d_next]
    if has_g:
        g_tile, _, g_next = _col_specs(tm, tn, gate_col, t)
        ins, specs = ins + [gate, gate], specs + [g_tile, g_next]
    out_tile = pl.BlockSpec((tm, tn), lambda j, i: (i, j))
    shapes, ospecs = [jax.ShapeDtypeStruct((t, ncols), dx_dtype)], [out_tile]
    if has_b:
        shapes, ospecs = shapes + [jax.ShapeDtypeStruct((t, ncols), dx_dtype)], ospecs + [out_tile]
    if has_g:
        shapes, ospecs = shapes + [jax.ShapeDtypeStruct((t, ncols), dx_dtype)], ospecs + [out_tile]
    shapes = shapes + [jax.ShapeDtypeStruct((8, ncols), F32)]
    ospecs = ospecs + [pl.BlockSpec((8, tn), lambda j, i: (0, j))]
    outs = list(pl.pallas_call(
        body, name=name, grid=(ncols // tn, nrow), in_specs=specs, out_specs=ospecs, out_shape=shapes,
        compiler_params=_params("parallel", "arbitrary"),
    )(*ins))
    dxa = outs.pop(0)
    dxb = outs.pop(0) if has_b else None
    dgate = outs.pop(0) if has_g else None
    return dxa, dxb, dgate, outs.pop(0)[:kk]


def _ffn_act_fwd(upre, w):
    t, f2 = upre.shape
    f, kk = f2 // 2, w.shape[0]
    tm, tn = _tile(t, EW_ROWS, HALO), _tile(f, EW_COLS)
    nf = f // tn

    def body(g_ref, gp_ref, u_ref, up_ref, wg_ref, wu_ref, cg_ref, cu_ref, a_ref):
        first = pl.program_id(1) == 0

        def conv(x_ref, prev_ref, w_ref):
            prev = jnp.where(first, 0.0, prev_ref[...].astype(F32))
            return _conv_taps(jnp.concatenate([prev, x_ref[...].astype(F32)], axis=0), w_ref[...], tm)

        cg, cu = conv(g_ref, gp_ref, wg_ref), conv(u_ref, up_ref, wu_ref)
        cg_ref[...] = cg.astype(cg_ref.dtype)
        cu_ref[...] = cu.astype(cu_ref.dtype)
        a_ref[...] = (_silu(cg) * cu).astype(a_ref.dtype)

    g_tile, g_prev, _ = _col_specs(tm, tn, 0, t)
    u_tile, u_prev, _ = _col_specs(tm, tn, f, t)
    out = pl.BlockSpec((tm, tn), lambda j, i: (i, j))
    return pl.pallas_call(
        body, name="ffn_act_fwd", grid=(nf, t // tm),
        in_specs=[g_tile, g_prev, u_tile, u_prev, pl.BlockSpec((kk, tn), lambda j, i: (0, j)),
                  pl.BlockSpec((kk, tn), lambda j, i: (0, nf + j))],
        out_specs=[out, out, out], out_shape=[jax.ShapeDtypeStruct((t, f), MXU_DTYPE)] * 3,
        compiler_params=_params("parallel", "parallel"),
    )(upre, upre, upre, upre, w, w)


def _swiglu_bwd(ug, uu, da):
    t, f = ug.shape
    tm, tn = _tile(t, EW_ROWS, 16), _tile(f, EW_COLS)
    nf = f // tn

    def body(g_ref, u_ref, da_ref, o_ref):
        g, d = g_ref[...].astype(F32), da_ref[...].astype(F32)
        sg = _sigmoid(g)
        o_ref[0] = (d * u_ref[...].astype(F32) * (sg * (1.0 + g * (1.0 - sg)))).astype(o_ref.dtype)
        o_ref[1] = (d * (g * sg)).astype(o_ref.dtype)

    tile = pl.BlockSpec((tm, tn), lambda i, j: (i, j))
    return pl.pallas_call(
        body, name="swiglu_bwd", grid=(t // tm, nf), in_specs=[tile] * 3,
        out_specs=pl.BlockSpec((2, tm, tn), lambda i, j: (0, i, j)),
        out_shape=jax.ShapeDtypeStruct((2, t, f), MXU_DTYPE), compiler_params=_params("parallel", "parallel"),
    )(ug, uu, da)


def _gdn_prep(ops, qc, kc, vc, b_col, a_col, a_log, dt_bias):
    c, dh = qc.shape[-2:]
    q, k, v = _silu(qc), _silu(kc), _silu(vc)
    q = q * lax.rsqrt(jnp.sum(q * q, axis=-1, keepdims=True) + EPS) * (dh ** -0.5)
    k = k * lax.rsqrt(jnp.sum(k * k, axis=-1, keepdims=True) + EPS)
    beta = _sigmoid(b_col)
    g_col = -jnp.exp(a_log) * _softplus(a_col + dt_bias)
    r = lax.broadcasted_iota(jnp.int32, (c, c), 0)
    s = lax.broadcasted_iota(jnp.int32, (c, c), 1)
    g_row = jnp.sum(jnp.where(r == s, g_col, 0.0), axis=-2, keepdims=True)
    gc_col = jnp.sum(jnp.where(s <= r, g_row, 0.0), axis=-1, keepdims=True)
    gc_row = jnp.sum(jnp.where(r <= s, g_col, 0.0), axis=-2, keepdims=True)
    decay = jnp.exp(jnp.where(s <= r, gc_col - gc_row, -1e30))
    kb = k * beta
    a = jnp.where(s < r, ops.mm(kb, k, "nt") * decay, 0.0)
    tinv = ops.tri_inv(a)
    e_col = jnp.exp(gc_col)
    uw = ops.mmh(tinv, jnp.concatenate([v * beta, kb * e_col], axis=-1))
    u, w = uw[..., :dh], uw[..., dh:]
    attn = ops.mm(q, k, "nt") * decay
    g_last = jnp.sum(g_col, axis=-2, keepdims=True)
    return u, w, attn, q * e_col, k * jnp.exp(g_last - gc_col), g_last, tinv


def _gdn_step(ops, state, u, w, attn, q_dec, k_dec, g_last):
    v_new = u - ops.mm(w, state)
    o = ops.mm(q_dec, state) + ops.mm(attn, v_new)
    return o, state * jnp.exp(g_last) + ops.mm(k_dec, v_new, "tn")


PREP_HEADS, SCAN_HEADS = 4, 8


def _gdn_blocks(t, heads, hb_pref):
    tc = _tile(t, 256, CHUNK)
    hb = max(h for h in range(1, hb_pref + 1) if heads % h == 0)
    return tc, hb


def _to_chunks(ref, hb, dh):
    tc = ref.shape[0]
    return jnp.concatenate([ref[:, h * dh:(h + 1) * dh].astype(F32).reshape(tc // CHUNK, CHUNK, dh)
                            for h in range(hb)], axis=0)


def _from_chunks(ref, val, hb, dh):
    tc = ref.shape[0]
    ncb = tc // CHUNK
    for h in range(hb):
        ref[:, h * dh:(h + 1) * dh] = val[h * ncb:(h + 1) * ncb].reshape(tc, dh).astype(ref.dtype)


def _per_chunk(s, ncb):
    hb = s.shape[0]
    return jnp.broadcast_to(s[:, None], (hb, ncb, 1, 1)).reshape(hb * ncb, 1, 1)


def _gate_columns(pba, first_head, hb, heads):
    tc = pba.shape[0]
    lane = lax.broadcasted_iota(jnp.int32, pba.shape, 1)

    def pick(k):
        return jnp.sum(jnp.where(lane == k, pba, 0.0), axis=1, keepdims=True).reshape(tc // CHUNK, CHUNK, 1)

    return (jnp.concatenate([pick(first_head + h) for h in range(hb)], axis=0),
            jnp.concatenate([pick(heads + first_head + h) for h in range(hb)], axis=0))


def _gdn_prep_fwd(qkv, pba, a_log, dt_bias, heads, dh, comm=None):
    t = qkv.shape[0]
    tc, hb = _gdn_blocks(t, heads, PREP_HEADS)
    ncb, nhb, width = tc // CHUNK, heads // hb, heads * dh
    nc = t // CHUNK
    grid = (t // tc, nhb)
    c_in, c_out = (len(comm.ins), len(comm.outs)) if comm is not None else (0, 0)

    def body(*refs):
        q_ref, k_ref, v_ref, g_ref, al_ref, dt_ref = refs[:6]
        u_ref, w_ref, p_ref, qd_ref, kd_ref, gl_ref, ti_ref = refs[6 + c_in:13 + c_in]
        if comm is not None:
            comm_refs = (refs[6:6 + c_in], refs[13 + c_in:13 + c_in + c_out], refs[-2:])

            @pl.when(jnp.logical_and(pl.program_id(0) == 0, pl.program_id(1) == 0))
            def _():
                comm.start(*comm_refs)

        b_col, a_col = _gate_columns(g_ref[...], pl.program_id(1) * hb, hb, heads)
        u, w, p, qd, kd, gl, tinv = _gdn_prep(
            _PLAIN, _to_chunks(q_ref, hb, dh), _to_chunks(k_ref, hb, dh), _to_chunks(v_ref, hb, dh), b_col, a_col,
            _per_chunk(al_ref[...], ncb), _per_chunk(dt_ref[...], ncb))
        _from_chunks(u_ref, u, hb, dh)
        _from_chunks(w_ref, w, hb, dh)
        _from_chunks(qd_ref, qd, hb, dh)
        _from_chunks(kd_ref, kd, hb, dh)
        p_ref[...] = p.reshape(hb, tc, CHUNK).astype(p_ref.dtype)
        gl_ref[...] = gl.reshape(hb, ncb, 1, 1)
        ti_ref[...] = tinv.reshape(hb, tc, CHUNK)
        if comm is not None:
            @pl.when(jnp.logical_and(pl.program_id(0) == grid[0] - 1, pl.program_id(1) == grid[1] - 1))
            def _():
                comm.finish(*comm_refs)

    def tok(off):
        return pl.BlockSpec((tc, hb * dh), lambda i, j: (i, off * nhb + j))

    gate = pl.BlockSpec((tc, LANES), lambda i, j: (i, 0))
    scal = pl.BlockSpec((hb, 1, 1), lambda i, j: (j, 0, 0))
    square = pl.BlockSpec((hb, tc, CHUNK), lambda i, j: (j, i, 0))
    outs = pl.pallas_call(
        body, name="gdn_prep_fwd", grid=grid,
        in_specs=[tok(0), tok(1), tok(2), gate, scal, scal] + [_ANY] * c_in,
        out_specs=[tok(0), tok(0), square, tok(0), tok(0), pl.BlockSpec((hb, ncb, 1, 1), lambda i, j: (j, i, 0, 0)),
                   square] + [_ANY] * c_out,
        out_shape=[jax.ShapeDtypeStruct((t, width), F32), jax.ShapeDtypeStruct((t, width), MXU_DTYPE),
                   jax.ShapeDtypeStruct((heads, t, CHUNK), MXU_DTYPE), jax.ShapeDtypeStruct((t, width), MXU_DTYPE),
                   jax.ShapeDtypeStruct((t, width), MXU_DTYPE), jax.ShapeDtypeStruct((heads, nc, 1, 1), F32),
                   jax.ShapeDtypeStruct((heads, t, CHUNK), F32)] + (list(comm.outs) if comm is not None else []),
        scratch_shapes=_sem_pairs(comm.n_sems) if comm is not None else [],
        compiler_params=_params("arbitrary", "arbitrary") if comm is not None else _params("parallel", "parallel"),
    )(qkv, qkv, qkv, pba, a_log, dt_bias, *(comm.ins if comm is not None else ()))
    return tuple(outs[:6]), outs[6], list(outs[7:])


def _gdn_prep_bwd(qkv, pba, a_log, dt_bias, tinv, du, dw, dp, dqd, dkd, dgl, heads, dh):
    t = qkv.shape[0]
    tc, hb = _gdn_blocks(t, heads, PREP_HEADS)
    ncb, nhb, width = tc // CHUNK, heads // hb, heads * dh

    def body(q_ref, k_ref, v_ref, g_ref, al_ref, dt_ref, ti_ref, du_ref, dw_ref, dp_ref, dqd_ref, dkd_ref, dgl_ref,
             dq_ref, dk_ref, dv_ref, dg_ref, dal_ref, ddt_ref):
        first_head = pl.program_id(1) * hb
        b_col, a_col = _gate_columns(g_ref[...], first_head, hb, heads)
        ops = _Ops(True, ti_ref[...].reshape(hb * ncb, CHUNK, CHUNK))

        def prep(q, k, v, b, a, al, dt):
            return _gdn_prep(ops, q, k, v, b, a, _per_chunk(al, ncb), _per_chunk(dt, ncb))[:6]

        _, vjp = jax.vjp(prep, _to_chunks(q_ref, hb, dh), _to_chunks(k_ref, hb, dh), _to_chunks(v_ref, hb, dh),
                         b_col, a_col, al_ref[...], dt_ref[...])
        dq, dk, dv, db, da, dal, ddt = vjp((
            _to_chunks(du_ref, hb, dh), _to_chunks(dw_ref, hb, dh), dp_ref[...].reshape(hb * ncb, CHUNK, CHUNK),
            _to_chunks(dqd_ref, hb, dh), _to_chunks(dkd_ref, hb, dh), dgl_ref[...].reshape(hb * ncb, 1, 1)))
        _from_chunks(dq_ref, dq, hb, dh)
        _from_chunks(dk_ref, dk, hb, dh)
        _from_chunks(dv_ref, dv, hb, dh)
        dal_ref[...] = dal[None]
        ddt_ref[...] = ddt[None]
        lane = lax.broadcasted_iota(jnp.int32, (tc, LANES), 1)
        dgates = jnp.zeros((tc, LANES), F32)
        for h in range(hb):
            rows = slice(h * ncb, (h + 1) * ncb)
            dgates = dgates + jnp.where(lane == first_head + h, db[rows].reshape(tc, 1), 0.0) \
                + jnp.where(lane == heads + first_head + h, da[rows].reshape(tc, 1), 0.0)

        @pl.when(first_head == 0)
        def _():
            dg_ref[...] = dgates

        @pl.when(first_head > 0)
        def _():
            dg_ref[...] += dgates

    def tok(off):
        return pl.BlockSpec((tc, hb * dh), lambda i, j: (i, off * nhb + j))

    gate = pl.BlockSpec((tc, LANES), lambda i, j: (i, 0))
    scal = pl.BlockSpec((hb, 1, 1), lambda i, j: (j, 0, 0))
    part = pl.BlockSpec((1, hb, 1, 1), lambda i, j: (i, j, 0, 0))
    pspec = pl.BlockSpec((hb, tc, CHUNK), lambda i, j: (j, i, 0))
    glspec = pl.BlockSpec((hb, ncb, 1, 1), lambda i, j: (j, i, 0, 0))
    tokf = jax.ShapeDtypeStruct((t, width), F32)
    partf = jax.ShapeDtypeStruct((t // tc, heads, 1, 1), F32)
    return pl.pallas_call(
        body, name="gdn_prep_bwd", grid=(t // tc, nhb),
        in_specs=[tok(0), tok(1), tok(2), gate, scal, scal, pspec, tok(0), tok(0), pspec, tok(0), tok(0), glspec],
        out_specs=[tok(0), tok(0), tok(0), gate, part, part],
        out_shape=[tokf, tokf, tokf, jax.ShapeDtypeStruct((t, LANES), F32), partf, partf],
        compiler_params=_params("parallel", "arbitrary"),
    )(qkv, qkv, qkv, pba, a_log, dt_bias, tinv, du, dw, dp, dqd, dkd, dgl)


def _heads(ref, rows, hb, dh):
    return jnp.stack([ref[rows, h * dh:(h + 1) * dh].astype(F32) for h in range(hb)])


def _put_heads(ref, rows, val, dh):
    for h in range(val.shape[0]):
        ref[rows, h * dh:(h + 1) * dh] = val[h].astype(ref.dtype)


def _gdn_scan_fwd(u, w, p, qd, kd, gl, heads, dh):
    t = u.shape[0]
    tc, hb = _gdn_blocks(t, heads, SCAN_HEADS)
    ncb, nhb = tc // CHUNK, heads // hb
    nc = t // CHUNK

    def body(u_ref, w_ref, p_ref, qd_ref, kd_ref, gl_ref, o_ref, s_ref, state):
        @pl.when(pl.program_id(1) == 0)
        def _():
            state[...] = jnp.zeros_like(state)

        for c in range(ncb):
            rs = slice(c * CHUNK, (c + 1) * CHUNK)
            s_in = state[...]
            s_ref[:, c] = s_in
            o, s_out = _gdn_step(_PLAIN, s_in, _heads(u_ref, rs, hb, dh), _heads(w_ref, rs, hb, dh), p_ref[:, rs, :],
                                 _heads(qd_ref, rs, hb, dh), _heads(kd_ref, rs, hb, dh), gl_ref[:, c])
            _put_heads(o_ref, rs, o, dh)
            state[...] = s_out

    tok = pl.BlockSpec((tc, hb * dh), lambda j, i: (i, j))
    pspec = pl.BlockSpec((hb, tc, CHUNK), lambda j, i: (j, i, 0))
    glspec = pl.BlockSpec((hb, ncb, 1, 1), lambda j, i: (j, i, 0, 0))
    return pl.pallas_call(
        body, name="gdn_scan_fwd", grid=(nhb, t // tc),
        in_specs=[tok, tok, pspec, tok, tok, glspec],
        out_specs=[tok, pl.BlockSpec((hb, ncb, dh, dh), lambda j, i: (j, i, 0, 0))],
        out_shape=[jax.ShapeDtypeStruct((t, heads * dh), F32), jax.ShapeDtypeStruct((heads, nc, dh, dh), F32)],
        scratch_shapes=[pltpu.VMEM((hb, dh, dh), F32)],
        compiler_params=_params("arbitrary", "arbitrary"),
    )(u, w, p, qd, kd, gl)


def _gdn_scan_bwd(u, w, p, qd, kd, gl, states, do, heads, dh):
    t = u.shape[0]
    tc, hb = _gdn_blocks(t, heads, SCAN_HEADS)
    ncb, nhb = tc // CHUNK, heads // hb
    nc, nt = t // CHUNK, t // tc

    def body(u_ref, w_ref, p_ref, qd_ref, kd_ref, gl_ref, s_ref, do_ref,
             du_ref, dw_ref, dp_ref, dqd_ref, dkd_ref, dgl_ref, dstate):
        @pl.when(pl.program_id(1) == 0)
        def _():
            dstate[...] = jnp.zeros_like(dstate)

        for c in reversed(range(ncb)):
            rs = slice(c * CHUNK, (c + 1) * CHUNK)
            _, vjp = jax.vjp(functools.partial(_gdn_step, _DIFF), s_ref[:, c], _heads(u_ref, rs, hb, dh),
                             _heads(w_ref, rs, hb, dh), p_ref[:, rs, :].astype(F32), _heads(qd_ref, rs, hb, dh),
                             _heads(kd_ref, rs, hb, dh), gl_ref[:, c])
            ds, du, dw, dp, dqd, dkd, dgl = vjp((_heads(do_ref, rs, hb, dh), dstate[...]))
            dstate[...] = ds
            _put_heads(du_ref, rs, du, dh)
            _put_heads(dw_ref, rs, dw, dh)
            _put_heads(dqd_ref, rs, dqd, dh)
            _put_heads(dkd_ref, rs, dkd, dh)
            dp_ref[:, rs, :] = dp
            dgl_ref[:, c] = dgl

    tok = pl.BlockSpec((tc, hb * dh), lambda j, i: (nt - 1 - i, j))
    pspec = pl.BlockSpec((hb, tc, CHUNK), lambda j, i: (j, nt - 1 - i, 0))
    glspec = pl.BlockSpec((hb, ncb, 1, 1), lambda j, i: (j, nt - 1 - i, 0, 0))
    sspec = pl.BlockSpec((hb, ncb, dh, dh), lambda j, i: (j, nt - 1 - i, 0, 0))
    tokf = jax.ShapeDtypeStruct((t, heads * dh), F32)
    return pl.pallas_call(
        body, name="gdn_scan_bwd", grid=(nhb, nt),
        in_specs=[tok, tok, pspec, tok, tok, glspec, sspec, tok],
        out_specs=[tok, tok, pspec, tok, tok, glspec],
        out_shape=[tokf, tokf, jax.ShapeDtypeStruct((heads, t, CHUNK), F32), tokf, tokf,
                   jax.ShapeDtypeStruct((heads, nc, 1, 1), F32)],
        scratch_shapes=[pltpu.VMEM((hb, dh, dh), F32)],
        compiler_params=_params("arbitrary", "arbitrary"),
    )(u, w, p, qd, kd, gl, states, do)


def _gdn_post(o, z, gain):
    return _rms(o, gain) * _silu(z)


def _gdn_post_fwd(o, pm, z_col, gain, heads, dh):
    t, wid = o.shape
    tm = _tile(t, 256, 16)
    assert z_col % wid == 0

    def body(o_ref, z_ref, g_ref, y_ref):
        for h in range(heads):
            ls = slice(h * dh, (h + 1) * dh)
            y_ref[:, ls] = _gdn_post(o_ref[:, ls], z_ref[:, ls], g_ref[...]).astype(y_ref.dtype)

    blk = pl.BlockSpec((tm, wid), lambda i: (i, 0))
    return pl.pallas_call(
        body, name="gdn_post_fwd", grid=(t // tm,),
        in_specs=[blk, pl.BlockSpec((tm, wid), lambda i: (i, z_col // wid)), pl.BlockSpec((1, dh), lambda i: (0, 0))],
        out_specs=blk, out_shape=jax.ShapeDtypeStruct((t, wid), MXU_DTYPE), compiler_params=_params("parallel"),
    )(o, pm, gain.reshape(1, dh))


def _gdn_post_bwd(o, pm, z_col, gain, dy, heads, dh):
    t, wid = o.shape
    tm = _tile(t, 256, 16)
    assert z_col % wid == 0

    def body(o_ref, z_ref, g_ref, dy_ref, do_ref, dz_ref, dg_ref):
        dg = jnp.zeros((1, dh), F32)
        for h in range(heads):
            ls = slice(h * dh, (h + 1) * dh)
            _, vjp = jax.vjp(_gdn_post, o_ref[:, ls], z_ref[:, ls], g_ref[...])
            do, dz, dg_h = vjp(dy_ref[:, ls])
            do_ref[:, ls] = do
            dz_ref[:, ls] = dz.astype(dz_ref.dtype)
            dg = dg + dg_h
        first = pl.program_id(0) == 0

        @pl.when(first)
        def _():
            dg_ref[...] = dg

        @pl.when(jnp.logical_not(first))
        def _():
            dg_ref[...] += dg

    blk = pl.BlockSpec((tm, wid), lambda i: (i, 0))
    vec = pl.BlockSpec((1, dh), lambda i: (0, 0))
    do, dz, dg = pl.pallas_call(
        body, name="gdn_post_bwd", grid=(t // tm,),
        in_specs=[blk, pl.BlockSpec((tm, wid), lambda i: (i, z_col // wid)), vec, blk], out_specs=[blk, blk, vec],
        out_shape=[jax.ShapeDtypeStruct((t, wid), F32), jax.ShapeDtypeStruct((t, wid), MXU_DTYPE),
                   jax.ShapeDtypeStruct((1, dh), F32)],
        compiler_params=_params("arbitrary"),
    )(o, pm, gain.reshape(1, dh), dy)
    return do, dz, dg.reshape(dh)


def _attn(ops, q, kv):
    d = q.shape[1]
    hd = d // XATTN_HEADS
    outs = []
    for h in range(XATTN_HEADS):
        qh, kh, vh = q[:, h * hd:(h + 1) * hd], kv[:, h * hd:(h + 1) * hd], kv[:, d + h * hd:d + (h + 1) * hd]
        s = ops.mm(qh, kh, "nt") * (hd ** -0.5)
        e = jnp.exp(s - lax.stop_gradient(jnp.max(s, axis=-1, keepdims=True)))
        outs.append(ops.mm(e / jnp.sum(e, axis=-1, keepdims=True), vh))
    return jnp.concatenate(outs, axis=1)


def _attn_fwd(q, kv):
    t, d = q.shape
    nm = kv.shape[0]
    tm = _tile(t, 512, 16)

    def body(q_ref, kv_ref, o_ref):
        o_ref[...] = _attn(_PLAIN, q_ref[...], kv_ref[...]).astype(o_ref.dtype)

    return pl.pallas_call(
        body, name="xattn_fwd", grid=(t // tm,),
        in_specs=[pl.BlockSpec((tm, d), lambda i: (i, 0)), pl.BlockSpec((nm, 2 * d), lambda i: (0, 0))],
        out_specs=pl.BlockSpec((tm, d), lambda i: (i, 0)),
        out_shape=jax.ShapeDtypeStruct((t, d), MXU_DTYPE), compiler_params=_params("parallel"),
    )(q, kv)


def _attn_bwd(q, kv, do):
    t, d = q.shape
    nm = kv.shape[0]
    tm = _tile(t, 256, 16)

    def body(q_ref, kv_ref, do_ref, dq_ref, dkv_ref):
        _, vjp = jax.vjp(functools.partial(_attn, _DIFF), q_ref[...].astype(F32), kv_ref[...].astype(F32))
        dq, dkv = vjp(do_ref[...].astype(F32))
        dq_ref[...] = dq.astype(dq_ref.dtype)
        first = pl.program_id(0) == 0

        @pl.when(first)
        def _():
            dkv_ref[...] = dkv

        @pl.when(jnp.logical_not(first))
        def _():
            dkv_ref[...] += dkv

    row = pl.BlockSpec((tm, d), lambda i: (i, 0))
    full = pl.BlockSpec((nm, 2 * d), lambda i: (0, 0))
    return pl.pallas_call(
        body, name="xattn_bwd", grid=(t // tm,), in_specs=[row, full, row], out_specs=[row, full],
        out_shape=[jax.ShapeDtypeStruct((t, d), MXU_DTYPE), jax.ShapeDtypeStruct((nm, 2 * d), F32)],
        compiler_params=_params("arbitrary"),
    )(q, kv, do)


def _adamw(name, w, g, m, v):
    shape = w.shape
    cols = shape[-1]
    rows = w.size // cols
    w2, g2, m2, v2 = (a.reshape(rows, cols) for a in (w, g, m, v))
    tr = _tile(rows, max(8, (1 << 18) // cols // 8 * 8), 8)

    def body(w_ref, g_ref, m_ref, v_ref, d_ref, nm_ref, nv_ref):
        gv = g_ref[...]
        nm = ADAM_B1 * m_ref[...] + (1.0 - ADAM_B1) * gv
        nv = ADAM_B2 * v_ref[...] + (1.0 - ADAM_B2) * jnp.square(gv)
        m_hat = nm / (1.0 - ADAM_B1 ** ADAM_STEP)
        v_hat = nv / (1.0 - ADAM_B2 ** ADAM_STEP)
        d_ref[...] = -ADAM_LR * (m_hat / (jnp.sqrt(v_hat) + ADAM_EPS) + ADAM_WD * w_ref[...])
        nm_ref[...] = nm
        nv_ref[...] = nv

    blk = pl.BlockSpec((tr, cols), lambda i: (i, 0))
    out = jax.ShapeDtypeStruct((rows, cols), F32)
    d, nm, nv = pl.pallas_call(
        body, name=name, grid=(rows // tr,), in_specs=[blk] * 4, out_specs=[blk] * 3, out_shape=[out] * 3,
        compiler_params=_params("parallel"),
    )(w2, g2, m2, v2)
    return d.reshape(shape), nm.reshape(shape), nv.reshape(shape)


def _layer_fwd(x, mem, p, heads, dh, carry, late):
    wid = heads * dh
    sc = x.shape[1] - wid
    p, s, landed = dict(p), {"x0": x}, {}

    def arrived(name, brought):
        landed[name] = brought
        if name in late:
            p.update(late[name](brought))

    def mm(name, *args, **kwargs):
        if name not in carry:
            return _matmul(name, *args, **kwargs)
        out, brought = _matmul(name, *args, comm=carry[name], **kwargs)
        arrived(name, brought)
        return out

    s["h1"] = _rms_fwd("rms_mix", x, p["mix_norm"])
    s["pm"] = pm = mm("mm_mix_in", s["h1"], p["wmain"], "nn", F32)
    s["pba"] = mm("mm_mix_ba", s["h1"], p["wba"], "nn", F32)
    s["qkv"] = _conv_fwd("conv_gdn", pm, 0, p["gdn_conv"], 0, 3 * wid, F32)
    s["prep"], s["tinv"], brought = _gdn_prep_fwd(s["qkv"], s["pba"], p["a_log"], p["dt_bias"], heads, dh,
                                                  comm=carry.get("gdn_prep_fwd"))
    if brought:
        arrived("gdn_prep_fwd", brought)
    s["o"], s["states"] = _gdn_scan_fwd(*s["prep"], heads, dh)
    y_gdn = _gdn_post_fwd(s["o"], pm, 3 * wid, p["gdn_out_norm"], heads, dh)
    y_sc = _conv_fwd("conv_sc", pm, 4 * wid + sc, p["sc_conv"], 0, sc, MXU_DTYPE, xb=pm, xb_col=4 * wid + 2 * sc,
                     gate=pm, gate_col=4 * wid)
    s["ycat"] = jnp.concatenate([y_gdn, y_sc], axis=1)
    s["x1"] = x1 = mm("mm_mix_out", s["ycat"], p["wout"], "nn", F32, add=x)
    s["h2"] = _rms_fwd("rms_xattn", x1, p["xattn_norm"])
    s["q"] = mm("mm_xq", s["h2"], p["wq"], "nn", MXU_DTYPE)
    s["memn"] = _rms_fwd("rms_mem", mem, p["mem_norm"])
    s["kv"] = mm("mm_xkv", s["memn"], p["wkv"], "nn", MXU_DTYPE)
    s["ao"] = _attn_fwd(s["q"], s["kv"])
    s["x2"] = x2 = mm("mm_xo", s["ao"], p["wo"], "nn", F32, add=x1)
    s["h3"] = _rms_fwd("rms_ffn", x2, p["ffn_norm"])
    s["upre"] = mm("mm_ffn_up", s["h3"], p["wup"], "nn", MXU_DTYPE)
    s["ug"], s["uu"], s["act"] = _ffn_act_fwd(s["upre"], p["ffn_conv"])
    return mm("mm_ffn_down", s["act"], p["wdown"], "nn", F32, add=x2), s, landed, p


def _layer_bwd(dx3, dx3b, mem, s, p, heads, dh, reduce):
    wid = heads * dh
    sc = dx3.shape[1] - wid
    pm = s["pm"]
    g = {}

    mm = reduce.carried if reduce is not None else _matmul
    da = mm("mm_ffn_down_dx", dx3b, p["wdown"], "nt", MXU_DTYPE)
    g["wdown"] = mm("mm_ffn_down_dw", s["act"], dx3b, "tn", WIRE_DTYPE)
    du = _swiglu_bwd(s["ug"], s["uu"], da)
    dupre, _, _, g["ffn_conv"] = _conv_bwd("conv_ffn_bwd", s["upre"], 0, p["ffn_conv"], 0, du, 0,
                                           s["upre"].shape[1], MXU_DTYPE)
    dh3 = mm("mm_ffn_up_dx", dupre, p["wup"], "nt", F32)
    g["wup"] = mm("mm_ffn_up_dw", s["h3"], dupre, "tn", WIRE_DTYPE)
    dx2, dx2b, g["ffn_norm"] = _rms_bwd("rms_ffn_bwd", s["x2"], p["ffn_norm"], dh3, dx3)
    dao = mm("mm_xo_dx", dx2b, p["wo"], "nt", MXU_DTYPE)
    g["wo"] = mm("mm_xo_dw", s["ao"], dx2b, "tn", WIRE_DTYPE)
    dq, dkv = _attn_bwd(s["q"], s["kv"], dao)
    dh2 = mm("mm_xq_dx", dq, p["wq"], "nt", F32)
    g["wq"] = mm("mm_xq_dw", s["h2"], dq, "tn", WIRE_DTYPE)
    dmemn = mm("mm_xkv_dx", dkv, p["wkv"], "nt", F32)
    g["wkv"] = mm("mm_xkv_dw", s["memn"], dkv, "tn", WIRE_DTYPE)
    _, _, g["mem_norm"] = _rms_bwd("rms_mem_bwd", mem, p["mem_norm"], dmemn)
    dx1, dx1b, g["xattn_norm"] = _rms_bwd("rms_xattn_bwd", s["x1"], p["xattn_norm"], dh2, dx2)
    dycat = mm("mm_mix_out_dx", dx1b, p["wout"], "nt", F32)
    g["wout"] = mm("mm_mix_out_dw", s["ycat"], dx1b, "tn", WIRE_DTYPE)
    d_c, d_h, d_b, g["sc_conv"] = _conv_bwd("conv_sc_bwd", pm, 4 * wid + sc, p["sc_conv"], 0, dycat, wid, sc,
                                             MXU_DTYPE, xb=pm, xb_col=4 * wid + 2 * sc, gate=pm, gate_col=4 * wid)
    do, dz, g["gdn_out_norm"] = _gdn_post_bwd(s["o"], pm, 3 * wid, p["gdn_out_norm"], dycat, heads, dh)
    dprep = _gdn_scan_bwd(*s["prep"], s["states"], do, heads, dh)
    dqc, dkc, dvc, dpba, dal, ddt = _gdn_prep_bwd(s["qkv"], s["pba"], p["a_log"], p["dt_bias"], s["tinv"], *dprep,
                                                  heads, dh)
    g["a_log"], g["dt_bias"] = jnp.sum(dal, axis=0), jnp.sum(ddt, axis=0)
    dqkv, _, _, g["gdn_conv"] = _conv_bwd("conv_gdn_bwd", pm, 0, p["gdn_conv"], 0,
                                          jnp.concatenate([dqc, dkc, dvc], axis=1), 0, 3 * wid, MXU_DTYPE)
    dpm = jnp.concatenate([dqkv, dz, d_b, d_c, d_h], axis=1)
    dpba = dpba.astype(MXU_DTYPE)
    dh1 = mm("mm_mix_in_dx", dpm, p["wmain"], "nt", F32)
    dh1 = mm("mm_mix_ba_dx", dpba, p["wba"], "nt", F32, add=dh1)
    g["wmain"] = mm("mm_mix_in_dw", s["h1"], dpm, "tn", WIRE_DTYPE)
    g["wba"] = mm("mm_mix_ba_dw", s["h1"], dpba, "tn", WIRE_DTYPE)
    dx0, dx0b, g["mix_norm"] = _rms_bwd("rms_mix_bwd", s["x0"], p["mix_norm"], dh1, dx1)
    return dx0, dx0b, g


def _input_projection(win, heads, dh):
    wid = heads * dh
    return {"wmain": jnp.concatenate([win[:, :4 * wid], win[:, 4 * wid + 2 * heads:]], axis=1),
            "wba": jnp.pad(win[:, 4 * wid:4 * wid + 2 * heads], ((0, 0), (0, LANES - 2 * heads)))}


def _square_projections(wout, wq, wk, wv, wo, wdown):
    return {"wout": wout, "wq": wq, "wkv": jnp.concatenate([wk, wv], axis=1), "wo": wo, "wdown": wdown}


_ANY = pl.BlockSpec(memory_space=pl.ANY)
_VMEM = pl.BlockSpec(memory_space=pltpu.VMEM)


def _mesh_pos():
    return lax.axis_index("x"), lax.axis_index("y"), lax.axis_index("c")


def _other_chips(x, y):
    return [(1 - x, y), (x, 1 - y), (1 - x, 1 - y)]


def _push(src, dst, sems, k, to):
    return pltpu.make_async_remote_copy(src_ref=src, dst_ref=dst, send_sem=sems[0].at[k], recv_sem=sems[1].at[k],
                                        device_id=to, device_id_type=MESH)


def _sem_pairs(n):
    return [pltpu.SemaphoreType.DMA((n,)), pltpu.SemaphoreType.DMA((n,))]


class _Comm:
    def __init__(self, ins, outs, n_sems, start, finish, aliases=None):
        self.ins, self.outs, self.n_sems, self.start, self.finish = list(ins), list(outs), n_sems, start, finish
        self.aliases = aliases or {}


def _run_comm(name, comm):
    n_in, n_out = len(comm.ins), len(comm.outs)

    def body(*refs):
        parts = (refs[:n_in], refs[n_in:n_in + n_out], refs[n_in + n_out:])
        comm.start(*parts)
        comm.finish(*parts)

    return pl.pallas_call(
        body, name=name, in_specs=[_ANY] * n_in, out_specs=[_ANY] * n_out, out_shape=comm.outs,
        scratch_shapes=_sem_pairs(comm.n_sems), input_output_aliases=comm.aliases,
    )(*comm.ins)


def _allgather_comm(srcs):
    n = len(srcs)

    def first(src, out, sems):
        x, y, c = _mesh_pos()
        own, sends = [], []
        for t in range(n):
            half = src[t].shape[0] // 2
            mine = pl.ds(c * half, half)
            own.append(_push(src[t], out[t].at[2 * x + y], sems, 7 * t + 6, (x, y, 1 - c)))
            sends += [_push(src[t].at[mine], out[t].at[2 * x + y, mine], sems, 7 * t + k, (cx, cy, c))
                      for k, (cx, cy) in enumerate(_other_chips(x, y))]
        return own, sends

    def start(src, out, sems):
        own, sends = first(src, out, sems)
        for cp in own + sends:
            cp.start()

    def finish(src, out, sems):
        x, y, c = _mesh_pos()
        sibling = (x, y, 1 - c)
        own, sends = first(src, out, sems)
        fwds, relayed = [], []
        for t in range(n):
            half = src[t].shape[0] // 2
            for k, (cx, cy) in enumerate(_other_chips(x, y)):
                here = out[t].at[2 * cx + cy, pl.ds(c * half, half)]
                there = out[t].at[2 * cx + cy, pl.ds((1 - c) * half, half)]
                _push(here, here, sems, 7 * t + k, sibling).wait_recv()
                fwds.append(_push(here, here, sems, 7 * t + 3 + k, sibling))
                fwds[-1].start()
                relayed.append(_push(there, there, sems, 7 * t + 3 + k, sibling))
        for cp in relayed + own:
            cp.wait_recv()
        for cp in own + sends + fwds:
            cp.wait_send()

    return _Comm(srcs, [jax.ShapeDtypeStruct((N_CHIPS,) + s.shape, s.dtype) for s in srcs], 7 * n, start, finish)


def _start_wait(build):
    def start(src, out, sems):
        for cp in build(src, out, sems):
            cp.start()

    def finish(src, out, sems):
        for cp in build(src, out, sems):
            cp.wait()

    return start, finish


def _sibling_exchange_comm(bufs):
    def build(src, out, sems):
        x, y, c = _mesh_pos()
        return [_push(src[t].at[1 - c], out[t], sems, t, (x, y, 1 - c)) for t in range(len(bufs))]

    start, finish = _start_wait(build)
    return _Comm(bufs, [jax.ShapeDtypeStruct(b.shape[1:], b.dtype) for b in bufs], len(bufs), start, finish)


def _chip_exchange_comm(bufs):
    def build(src, out, sems):
        x, y, c = _mesh_pos()
        return [_push(src[t].at[2 * cx + cy], out[t].at[k], sems, 3 * t + k, (cx, cy, c))
                for t in range(len(bufs)) for k, (cx, cy) in enumerate(_other_chips(x, y))]

    start, finish = _start_wait(build)
    return _Comm(bufs, [jax.ShapeDtypeStruct((3,) + b.shape[1:], b.dtype) for b in bufs], 3 * len(bufs), start, finish)


def _sibling_share_comm(bufs):
    def build(src, out, sems):
        x, y, c = _mesh_pos()
        return [_push(src[t].at[c], out[t].at[c], sems, t, (x, y, 1 - c)) for t in range(len(bufs))]

    start, finish = _start_wait(build)
    return _Comm(bufs, [jax.ShapeDtypeStruct(b.shape, b.dtype) for b in bufs], len(bufs), start, finish,
                 aliases={t: t for t in range(len(bufs))})


def _allreduce_small(v):
    r, lanes = v.shape

    def body(v_ref, sum_ref, gath, send_sems, recv_sems):
        x, y, c = _mesh_pos()
        me = 4 * x + 2 * y + c
        gath[me] = v_ref[...]
        copies = []
        for rel in range(1, N_DEV):
            peer = tuple(1 - p if (rel >> b) & 1 else p for p, b in ((x, 2), (y, 1), (c, 0)))
            copies.append(pltpu.make_async_remote_copy(
                src_ref=v_ref, dst_ref=gath.at[me], send_sem=send_sems.at[rel - 1], recv_sem=recv_sems.at[rel - 1],
                device_id=peer, device_id_type=MESH))
        for cp in copies:
            cp.start()
        for cp in copies:
            cp.wait()
        total = gath[0]
        for k in range(1, N_DEV):
            total = total + gath[k]
        sum_ref[...] = total

    return pl.pallas_call(
        body, name="allreduce_small", in_specs=[_VMEM], out_specs=_VMEM,
        out_shape=jax.ShapeDtypeStruct((r, lanes), F32),
        scratch_shapes=[pltpu.VMEM((N_DEV, r, lanes), F32)] + _sem_pairs(N_DEV - 1),
        compiler_params=pltpu.CompilerParams(vmem_limit_bytes=VMEM_LIMIT),
    )(v)


def _sum_tile(rows, width):
    return _tile(rows, max(16, (1 << 19) // width // 16 * 16), 16)


def _sum_sibling(x, recv, core):
    _, n, w = x.shape
    tr = _sum_tile(n, w)

    def body(idx_ref, x_ref, r_ref, o_ref):
        o_ref[...] = (x_ref[...].astype(F32) + r_ref[...].astype(F32)).astype(o_ref.dtype)

    row = pl.BlockSpec((tr, w), lambda i, idx: (i, 0))
    return pl.pallas_call(
        body, name="rs_sum_sibling",
        grid_spec=pltpu.PrefetchScalarGridSpec(
            num_scalar_prefetch=1, grid=(n // tr,),
            in_specs=[pl.BlockSpec((None, tr, w), lambda i, idx: (idx[0], i, 0)), row], out_specs=row),
        out_shape=jax.ShapeDtypeStruct((n, w), x.dtype), compiler_params=_params("parallel"),
    )(core.reshape(1), x, recv)


def _sum_chips(s, recv, chip, core):
    _, m, w = s.shape
    tr = _sum_tile(m, w)

    def body(idx_ref, s_ref, r0_ref, r1_ref, r2_ref, o_ref):
        o_ref[...] = ((s_ref[...].astype(F32) + r0_ref[...].astype(F32)) + r1_ref[...].astype(F32)) \
            + r2_ref[...].astype(F32)

    def got(k):
        return pl.BlockSpec((None, tr, w), lambda i, idx: (k, i, 0))

    return pl.pallas_call(
        body, name="rs_sum_chips",
        grid_spec=pltpu.PrefetchScalarGridSpec(
            num_scalar_prefetch=1, grid=(m // tr,),
            in_specs=[pl.BlockSpec((None, tr, w), lambda i, idx: (idx[0], i, 0)), got(0), got(1), got(2)],
            out_specs=pl.BlockSpec((None, tr, w), lambda i, idx: (idx[1], i, 0))),
        out_shape=jax.ShapeDtypeStruct((2, m, w), F32), compiler_params=_params("parallel"),
    )(jnp.stack([chip, core]), s, recv, recv, recv)


_ROWS = ("w_mix_out", "w_xq", "w_xk", "w_xv", "w_xo", "w_ffn_down")
_CONVS = ("gdn_conv", "sc_conv", "ffn_conv")
_REPLICATED = ("mix_norm", "gdn_a_log", "gdn_dt_bias", "gdn_out_norm", "xattn_norm", "mem_norm", "ffn_norm",
               "final_norm")
_WEIGHTS = ("mix_norm", "w_mix_in", "gdn_conv", "gdn_a_log", "gdn_dt_bias", "gdn_out_norm", "sc_conv", "w_mix_out",
            "xattn_norm", "mem_norm", "w_xq", "w_xk", "w_xv", "w_xo", "ffn_norm", "w_ffn_up", "ffn_conv",
            "w_ffn_down", "final_norm")


def _pad_rows(flat, groups):
    unit = groups * 16 * LANES
    p = flat.shape[-1]
    pad = -p % unit
    if pad:
        flat = jnp.pad(flat, [(0, 0)] * (flat.ndim - 1) + [(0, pad)])
    return flat.reshape(flat.shape[:-1] + (groups, (p + pad) // (groups * LANES), LANES))


def _split_flat(flat, shapes):
    out, off = [], 0
    for shp in shapes:
        size = 1
        for n in shp:
            size *= n
        out.append(flat[..., off:off + size].reshape(flat.shape[:-1] + tuple(shp)))
        off += size
    return out


def _halves_by_chip(g):
    _, rows, w = g.shape
    return g.astype(WIRE_DTYPE).reshape(N_CHIPS, 2, rows // 2, w).transpose(1, 0, 2, 3)


def _by_chip_columns(g):
    rows, cols = g.shape
    return g.reshape(rows, N_CHIPS, cols // N_CHIPS).transpose(1, 0, 2)


class _ReduceScatter:
    STAGES = ("mm_ffn_down_dx", "mm_ffn_up_dx", "mm_ffn_up_dw", "mm_mix_in_dx")

    def __init__(self, bufs, chip, core):
        self.bufs, self.chip, self.core = list(bufs), chip, core
        self.sums = self.from_chips = self.reduced = self.result = None

    def comm(self, stage):
        if stage == self.STAGES[0]:
            return _sibling_exchange_comm(self.bufs)
        if stage == self.STAGES[1]:
            return _chip_exchange_comm(self.sums[-1:])
        if stage == self.STAGES[2]:
            return _chip_exchange_comm(self.sums[:-1])
        return _sibling_share_comm(self.reduced)

    def landed(self, stage, outs):
        if stage == self.STAGES[0]:
            self.sums = [_sum_sibling(b.reshape(2, -1, b.shape[-1]), r.reshape(-1, r.shape[-1]), self.core)
                         .reshape(r.shape) for b, r in zip(self.bufs, outs)]
        elif stage == self.STAGES[1]:
            self.from_chips = list(outs)
        elif stage == self.STAGES[2]:
            self.reduced = [_sum_chips(s, r, self.chip, self.core)
                            for s, r in zip(self.sums, list(outs) + self.from_chips)]
        else:
            self.result = list(outs)

    def carried(self, name, *args, **kwargs):
        if name not in self.STAGES:
            return _matmul(name, *args, **kwargs)
        out, outs = _matmul(name, *args, comm=self.comm(name), **kwargs)
        self.landed(name, outs)
        return out

    def run_alone(self):
        for stage, name in zip(self.STAGES, ("rs_sibling_exchange", "rs_chip_exchange_rows", "rs_chip_exchange_cols",
                                             "rs_sibling_share")):
            self.landed(stage, _run_comm(name, self.comm(stage)))


def kernel(x, mem, mix_norm, w_mix_in, gdn_conv, gdn_a_log, gdn_dt_bias, gdn_out_norm, sc_conv, w_mix_out, xattn_norm, mem_norm, w_xq, w_xk, w_xv, w_xo, ffn_norm, w_ffn_up, ffn_conv, w_ffn_down, final_norm, loss_target, m_mix_norm, m_w_mix_in, m_gdn_conv, m_gdn_a_log, m_gdn_dt_bias, m_gdn_out_norm, m_sc_conv, m_w_mix_out, m_xattn_norm, m_mem_norm, m_w_xq, m_w_xk, m_w_xv, m_w_xo, m_ffn_norm, m_w_ffn_up, m_ffn_conv, m_w_ffn_down, m_final_norm, v_mix_norm, v_w_mix_in, v_gdn_conv, v_gdn_a_log, v_gdn_dt_bias, v_gdn_out_norm, v_sc_conv, v_w_mix_out, v_xattn_norm, v_mem_norm, v_w_xq, v_w_xk, v_w_xv, v_w_xo, v_ffn_norm, v_w_ffn_up, v_ffn_conv, v_w_ffn_down, v_final_norm):
    w = dict(zip(_WEIGHTS, (mix_norm, w_mix_in, gdn_conv, gdn_a_log, gdn_dt_bias, gdn_out_norm, sc_conv, w_mix_out,
                            xattn_norm, mem_norm, w_xq, w_xk, w_xv, w_xo, ffn_norm, w_ffn_up, ffn_conv, w_ffn_down,
                            final_norm)))
    m = dict(zip(_WEIGHTS, (m_mix_norm, m_w_mix_in, m_gdn_conv, m_gdn_a_log, m_gdn_dt_bias, m_gdn_out_norm, m_sc_conv,
                            m_w_mix_out, m_xattn_norm, m_mem_norm, m_w_xq, m_w_xk, m_w_xv, m_w_xo, m_ffn_norm,
                            m_w_ffn_up, m_ffn_conv, m_w_ffn_down, m_final_norm)))
    v = dict(zip(_WEIGHTS, (v_mix_norm, v_w_mix_in, v_gdn_conv, v_gdn_a_log, v_gdn_dt_bias, v_gdn_out_norm, v_sc_conv,
                            v_w_mix_out, v_xattn_norm, v_mem_norm, v_w_xq, v_w_xk, v_w_xv, v_w_xo, v_ffn_norm,
                            v_w_ffn_up, v_ffn_conv, v_w_ffn_down, v_final_norm)))
    core = lax.axis_index("c")
    chip = 2 * lax.axis_index("x") + lax.axis_index("y")
    depth, heads = gdn_a_log.shape
    dh = gdn_out_norm.shape[1]
    d, wid = x.shape[2], heads * dh

    row_sizes = [w[n].shape[1] for n in _ROWS]
    row_offs = [sum(row_sizes[:k]) for k in range(len(_ROWS))]
    src_in, src_up = w_mix_in.astype(WIRE_DTYPE), w_ffn_up.astype(WIRE_DTYPE)
    src_rows = jnp.concatenate([w[n] for n in _ROWS], axis=1).astype(WIRE_DTYPE)
    src_convs = _pad_rows(jnp.concatenate([w[n].reshape(-1) for n in _CONVS]), 2)
    g_in, g_convs = _run_comm("allgather_first", _allgather_comm([src_in[0], src_convs]))
    conv_full = {n: jnp.moveaxis(part, 0, 2).reshape(depth, part.shape[2], -1)
                 for n, part in zip(_CONVS, _split_flat(g_convs.reshape(N_CHIPS, -1), [w[n].shape for n in _CONVS]))}
    side_by_side = lambda g: jnp.concatenate([g[j] for j in range(N_CHIPS)], axis=1)

    def from_rows(brought):
        g_rows, = brought
        return _square_projections(*[jnp.concatenate([g_rows[j, off:off + size] for j in range(N_CHIPS)], axis=0)
                                     for off, size in zip(row_offs, row_sizes)])

    xl, mem_l = x[0], mem[0]
    layers, saved, g_rows = [], [], None
    for l in range(depth):
        p = {"mix_norm": mix_norm[l], "xattn_norm": xattn_norm[l], "mem_norm": mem_norm[l], "ffn_norm": ffn_norm[l],
             "gdn_out_norm": gdn_out_norm[l], "a_log": gdn_a_log[l].reshape(heads, 1, 1),
             "dt_bias": gdn_dt_bias[l].reshape(heads, 1, 1), "gdn_conv": conv_full["gdn_conv"][l],
             "sc_conv": conv_full["sc_conv"][l], "ffn_conv": conv_full["ffn_conv"][l]}
        p.update(_input_projection(side_by_side(g_in), heads, dh))
        carry = {"gdn_prep_fwd": _allgather_comm([src_up[l]])}
        late = {"gdn_prep_fwd": lambda brought: {"wup": brought[0]}}
        if l == 0:
            carry["mm_mix_in"], late["mm_mix_in"] = _allgather_comm([src_rows[0]]), from_rows
        else:
            p.update(from_rows(g_rows))
        if l + 1 < depth:
            carry["mm_ffn_up"] = _allgather_comm([src_rows[l + 1]])
            carry["mm_ffn_down"] = _allgather_comm([src_in[l + 1]])
        xl, s, landed, p = _layer_fwd(xl, mem_l, p, heads, dh, carry, late)
        layers.append(p)
        saved.append(s)
        if l + 1 < depth:
            g_rows, (g_in,) = landed["mm_ffn_up"], landed["mm_ffn_down"]
    loss_row, dx, dxb, g_final = _final_loss(xl, final_norm, loss_target[0])

    def by_chip(g):
        g_win = jnp.concatenate([g["wmain"][:, :4 * wid], g["wba"][:, :2 * heads], g["wmain"][:, 4 * wid:]], axis=1)
        parts = (g["wout"], g["wq"], g["wkv"][:, :d], g["wkv"][:, d:], g["wo"], g["wdown"])
        by_rows = [p.reshape(N_CHIPS, p.shape[0] // N_CHIPS, p.shape[1]) for p in parts]
        return [_halves_by_chip(_by_chip_columns(g_win)), _halves_by_chip(_by_chip_columns(g["wup"])),
                _halves_by_chip(jnp.concatenate(by_rows, axis=1))]

    per_layer, shards, reduce = [None] * depth, [None] * depth, None
    for l in reversed(range(depth)):
        dx, dxb, per_layer[l] = _layer_bwd(dx, dxb, mem_l, saved[l], layers[l], heads, dh, reduce)
        if reduce is not None:
            shards[l + 1] = reduce.result
        reduce = _ReduceScatter(by_chip(per_layer[l]), chip, core)
    reduce.run_alone()
    shards[0] = reduce.result
    by_layer = [[s[t].reshape(-1, s[t].shape[-1]) for s in shards] for t in range(3)]
    grad = {"w_mix_in": jnp.stack(by_layer[0]), "w_ffn_up": jnp.stack(by_layer[1])}
    for n, off, size in zip(_ROWS, row_offs, row_sizes):
        grad[n] = jnp.stack([r[off:off + size] for r in by_layer[2]])

    stack = lambda k: jnp.stack([g[k] for g in per_layer])
    small_g = {"mix_norm": stack("mix_norm"), "gdn_a_log": stack("a_log").reshape(depth, heads),
               "gdn_dt_bias": stack("dt_bias").reshape(depth, heads), "gdn_out_norm": stack("gdn_out_norm"),
               "xattn_norm": stack("xattn_norm"), "mem_norm": stack("mem_norm"), "ffn_norm": stack("ffn_norm"),
               "final_norm": g_final, "gdn_conv": stack("gdn_conv"), "sc_conv": stack("sc_conv"),
               "ffn_conv": stack("ffn_conv")}
    names = _REPLICATED + _CONVS
    small = jnp.concatenate([small_g[n].reshape(-1) for n in names] + [loss_row[0, :1]])
    small_sum = _allreduce_small(_pad_rows(small, 1)[0]).reshape(-1)
    parts = _split_flat(small_sum, [small_g[n].shape for n in names] + [(1,)])
    g_rep = dict(zip(_REPLICATED, parts[:len(_REPLICATED)]))
    for n, part in zip(_CONVS, parts[len(_REPLICATED):-1]):
        grad[n] = lax.dynamic_slice_in_dim(part, chip * w[n].shape[2], w[n].shape[2], axis=2)
    loss = parts[-1][0]

    delta, new_m, new_v = {}, {}, {}
    for n in ("w_mix_in", "w_ffn_up") + _ROWS + _CONVS:
        delta[n], new_m[n], new_v[n] = _adamw("adamw_" + n, w[n], grad[n], m[n], v[n])
    pack_rep = lambda t: _pad_rows(jnp.concatenate([t[n].reshape(-1) for n in _REPLICATED]), 1)[0]
    outs = _adamw("adamw_replicated", pack_rep(w), pack_rep(g_rep), pack_rep(m), pack_rep(v))
    shapes = [w[n].shape for n in _REPLICATED]
    for tgt, packed_out in zip((delta, new_m, new_v), outs):
        tgt.update(zip(_REPLICATED, _split_flat(packed_out.reshape(-1), shapes)))
    grad.update(g_rep)
    return (loss, dx[None], *[grad[n] for n in _WEIGHTS], *[delta[n] for n in _WEIGHTS],
            *[new_m[n] for n in _WEIGHTS], *[new_v[n] for n in _WEIGHTS])
```
